```python
import jax, jax.numpy as jnp
from jax import lax
import numpy as np

D_MODEL = 2048
BATCH = 8
SEQ = 2048
DEPTH = 1

N_META = 16
CHUNK = 64
MIX_WIDTH = D_MODEL
GLA_WIDTH = MIX_WIDTH // 2
POOL_WIDTH = MIX_WIDTH - GLA_WIDTH
GLA_HEADS = 4
GLA_DV = GLA_WIDTH // GLA_HEADS
GLA_DK = GLA_DV // 2
GLA_KW = GLA_HEADS * GLA_DK
GATE_RANK = 16
GATE_TAU = 16.0
POOL_WINDOWS = (2, 4, 8, 16)
POOL_GROUPS = len(POOL_WINDOWS)
POOL_GC = POOL_WIDTH // POOL_GROUPS
D_FF = 4 * D_MODEL
EPS = 1e-6
SPLIT_POINTS = (
    GLA_KW,
    2 * GLA_KW,
    2 * GLA_KW + GLA_WIDTH,
    2 * GLA_KW + 2 * GLA_WIDTH,
    2 * GLA_KW + 2 * GLA_WIDTH + GATE_RANK,
)
D_IN = 2 * GLA_KW + 2 * GLA_WIDTH + GATE_RANK + POOL_WIDTH

kernel_name = "hybrid_gla_multiscale_pool_meta"


def rmsnorm(x, w):
    xf = x.astype(jnp.float32)
    y = xf * lax.rsqrt(jnp.mean(xf * xf, axis=-1, keepdims=True) + EPS)
    return (y * w.astype(jnp.float32)).astype(x.dtype)


def gla_chunked(q, k, v, logg):
    B, T, H, DK = q.shape
    DV = v.shape[-1]
    N = T // CHUNK

    def chunks(a):
        return a.reshape(B, N, CHUNK, H, a.shape[-1]).transpose(0, 3, 1, 2, 4)

    q, k, v, logg = chunks(q), chunks(k), chunks(v), chunks(logg)
    G = jnp.cumsum(logg, axis=3)
    G_last = G[:, :, :, -1:, :]
    q_dec = q * jnp.exp(G)
    k_inv = k * jnp.exp(-G)
    causal = jnp.tril(jnp.ones((CHUNK, CHUNK), dtype=bool))
    scores = jnp.einsum('bhncd,bhnsd->bhncs', q_dec, k_inv)
    scores = jnp.where(causal, scores, 0.0)
    o_intra = jnp.einsum('bhncs,bhnsv->bhncv', scores, v)
    k_to_end = k * jnp.exp(G_last - G)
    dS = jnp.einsum('bhncd,bhncv->bhndv', k_to_end, v)
    decay = jnp.exp(G_last[:, :, :, 0, :])

    def step(S, inp):
        dec, ds = inp
        return dec[..., None] * S + ds, S

    S0 = jnp.zeros((B, H, DK, DV), jnp.float32)
    _, S_prev = lax.scan(step, S0, (decay.transpose(2, 0, 1, 3), dS.transpose(2, 0, 1, 3, 4)))
    S_prev = S_prev.transpose(1, 2, 0, 3, 4)
    o_inter = jnp.einsum('bhncd,bhndv->bhncv', q_dec, S_prev)
    o = o_intra + o_inter
    return o.transpose(0, 2, 3, 1, 4).reshape(B, T, H, DV)


def multiscale_pool(pu, pool_w, pool_scale):
    B, L, _ = pu.shape
    xg = pu.astype(jnp.float32).reshape(B, L, POOL_GROUPS, POOL_GC)
    cs = jnp.pad(jnp.cumsum(xg, axis=1), ((0, 0), (1, 0), (0, 0), (0, 0)))
    t = jnp.arange(L)
    win = jnp.array(POOL_WINDOWS, dtype=jnp.int32)
    lo = jnp.maximum(t[:, None] + 1 - win[None, :], 0)
    g_idx = jnp.arange(POOL_GROUPS)[None, :]
    window_sum = cs[:, 1:] - cs[:, lo, g_idx]
    count = (t[:, None] + 1 - lo).astype(jnp.float32)[None, :, :, None]
    y = window_sum / count - xg
    y = jnp.einsum('blgc,gcd->blgd', y, pool_w.astype(jnp.float32))
    return y.reshape(B, L, POOL_WIDTH) * pool_scale.astype(jnp.float32)


def hybrid_layer(h, norm1_w, w_in, gate_w2, gate_b, gla_norm_w, pool_w, pool_scale,
                 w_out, norm2_w, mlp_w1, mlp_w2):
    B, L, _ = h.shape
    u = rmsnorm(h, norm1_w)
    proj = u @ w_in
    q, k, v, r, glr, pu = jnp.split(proj, SPLIT_POINTS, axis=-1)

    g_raw = (glr @ gate_w2 + gate_b).astype(jnp.float32)
    logg = jax.nn.log_sigmoid(g_raw) / GATE_TAU
    pad = (-N_META) % CHUNK

    def heads(a, d):
        a = a.astype(jnp.float32).reshape(B, L, GLA_HEADS, d)
        return jnp.pad(a, ((0, 0), (pad, 0), (0, 0), (0, 0)))

    o = gla_chunked(heads(q, GLA_DK) * (GLA_DK ** -0.5), heads(k, GLA_DK),
                    heads(v, GLA_DV), heads(logg, GLA_DK))[:, pad:]
    o = rmsnorm(o, gla_norm_w)
    gate_out = jax.nn.silu(r.astype(jnp.float32)).reshape(B, L, GLA_HEADS, GLA_DV)
    o_gla = (o * gate_out).reshape(B, L, GLA_WIDTH)

    o_pool = multiscale_pool(pu, pool_w, pool_scale)

    mixed = jnp.concatenate([o_gla, o_pool], axis=-1).astype(h.dtype)
    h = h + mixed @ w_out

    z = rmsnorm(h, norm2_w) @ mlp_w1
    h = h + jnp.square(jax.nn.relu(z)) @ mlp_w2
    return h


def _fwd_setup_inputs(seed: int = 0) -> dict:
    key = jax.random.key(seed)
    ks = jax.random.split(key, 16)
    f32 = jnp.float32
    nrm = lambda k, shape, s: jax.random.normal(k, shape, f32) * s
    return {
        "x": nrm(ks[0], (BATCH, SEQ, D_MODEL), 1.0),
        "meta_tokens": nrm(ks[1], (N_META, D_MODEL), 1.0),
        "norm1_w": 1.0 + nrm(ks[2], (DEPTH, D_MODEL), 0.02),
        "w_in": nrm(ks[3], (DEPTH, D_MODEL, D_IN), D_MODEL ** -0.5),
        "gate_w2": nrm(ks[4], (DEPTH, GATE_RANK, GLA_KW), GATE_RANK ** -0.5),
        "gate_b": nrm(ks[5], (DEPTH, GLA_KW), 0.1),
        "gla_norm_w": 1.0 + nrm(ks[6], (DEPTH, GLA_DV), 0.02),
        "pool_w": nrm(ks[7], (DEPTH, POOL_GROUPS, POOL_GC, POOL_GC), POOL_GC ** -0.5),
        "pool_scale": 1.0 + nrm(ks[8], (DEPTH, POOL_WIDTH), 0.1),
        "w_out": nrm(ks[9], (DEPTH, MIX_WIDTH, D_MODEL), MIX_WIDTH ** -0.5),
        "norm2_w": 1.0 + nrm(ks[10], (DEPTH, D_MODEL), 0.02),
        "mlp_w1": nrm(ks[11], (DEPTH, D_MODEL, D_FF), D_MODEL ** -0.5),
        "mlp_w2": nrm(ks[12], (DEPTH, D_FF, D_MODEL), D_FF ** -0.5),
        "final_norm_w": 1.0 + nrm(ks[13], (D_MODEL,), 0.02),
    }


def _fwd_reference(x, meta_tokens, norm1_w, w_in, gate_w2, gate_b, gla_norm_w, pool_w,
              pool_scale, w_out, norm2_w, mlp_w1, mlp_w2, final_norm_w):
    B = x.shape[0]
    meta = jnp.broadcast_to(meta_tokens[None].astype(x.dtype), (B, N_META, D_MODEL))
    h = jnp.concatenate([meta, x], axis=1)
    for i in range(DEPTH):
        h = hybrid_layer(h, norm1_w[i], w_in[i], gate_w2[i], gate_b[i], gla_norm_w[i],
                         pool_w[i], pool_scale[i], w_out[i], norm2_w[i], mlp_w1[i], mlp_w2[i])
    h = rmsnorm(h, final_norm_w)
    return h[:, N_META:]


import jax as _jax
import jax.numpy as _jnp

TWIN_FORMAT = 'train_step'
FWD_PARAMS = ['x', 'meta_tokens', 'norm1_w', 'w_in', 'gate_w2', 'gate_b', 'gla_norm_w', 'pool_w', 'pool_scale', 'w_out', 'norm2_w', 'mlp_w1', 'mlp_w2', 'final_norm_w']
TWIN_WEIGHTS = ['meta_tokens', 'norm1_w', 'w_in', 'gate_w2', 'gate_b', 'gla_norm_w', 'pool_w', 'pool_scale', 'w_out', 'norm2_w', 'mlp_w1', 'mlp_w2', 'final_norm_w']
TWIN_DIFF_INPUT = 'x'
TWIN_INPUTS = ['x', 'meta_tokens', 'norm1_w', 'w_in', 'gate_w2', 'gate_b', 'gla_norm_w', 'pool_w', 'pool_scale', 'w_out', 'norm2_w', 'mlp_w1', 'mlp_w2', 'final_norm_w', 'loss_target', 'm_meta_tokens', 'm_norm1_w', 'm_w_in', 'm_gate_w2', 'm_gate_b', 'm_gla_norm_w', 'm_pool_w', 'm_pool_scale', 'm_w_out', 'm_norm2_w', 'm_mlp_w1', 'm_mlp_w2', 'm_final_norm_w', 'v_meta_tokens', 'v_norm1_w', 'v_w_in', 'v_gate_w2', 'v_gate_b', 'v_gla_norm_w', 'v_pool_w', 'v_pool_scale', 'v_w_out', 'v_norm2_w', 'v_mlp_w1', 'v_mlp_w2', 'v_final_norm_w']
TWIN_OUTPUTS = ['loss', 'grad_x', 'grad_meta_tokens', 'grad_norm1_w', 'grad_w_in', 'grad_gate_w2', 'grad_gate_b', 'grad_gla_norm_w', 'grad_pool_w', 'grad_pool_scale', 'grad_w_out', 'grad_norm2_w', 'grad_mlp_w1', 'grad_mlp_w2', 'grad_final_norm_w', 'delta_meta_tokens', 'delta_norm1_w', 'delta_w_in', 'delta_gate_w2', 'delta_gate_b', 'delta_gla_norm_w', 'delta_pool_w', 'delta_pool_scale', 'delta_w_out', 'delta_norm2_w', 'delta_mlp_w1', 'delta_mlp_w2', 'delta_final_norm_w', 'new_m_meta_tokens', 'new_m_norm1_w', 'new_m_w_in', 'new_m_gate_w2', 'new_m_gate_b', 'new_m_gla_norm_w', 'new_m_pool_w', 'new_m_pool_scale', 'new_m_w_out', 'new_m_norm2_w', 'new_m_mlp_w1', 'new_m_mlp_w2', 'new_m_final_norm_w', 'new_v_meta_tokens', 'new_v_norm1_w', 'new_v_w_in', 'new_v_gate_w2', 'new_v_gate_b', 'new_v_gla_norm_w', 'new_v_pool_w', 'new_v_pool_scale', 'new_v_w_out', 'new_v_norm2_w', 'new_v_mlp_w1', 'new_v_mlp_w2', 'new_v_final_norm_w']
TWIN_LEAF_KINDS = {'loss': 'loss', 'grad_x': 'grad_x', 'grad_meta_tokens': 'grad_w', 'grad_norm1_w': 'grad_w', 'grad_w_in': 'grad_w', 'grad_gate_w2': 'grad_w', 'grad_gate_b': 'grad_w', 'grad_gla_norm_w': 'grad_w', 'grad_pool_w': 'grad_w', 'grad_pool_scale': 'grad_w', 'grad_w_out': 'grad_w', 'grad_norm2_w': 'grad_w', 'grad_mlp_w1': 'grad_w', 'grad_mlp_w2': 'grad_w', 'grad_final_norm_w': 'grad_w', 'delta_meta_tokens': 'delta_w', 'delta_norm1_w': 'delta_w', 'delta_w_in': 'delta_w', 'delta_gate_w2': 'delta_w', 'delta_gate_b': 'delta_w', 'delta_gla_norm_w': 'delta_w', 'delta_pool_w': 'delta_w', 'delta_pool_scale': 'delta_w', 'delta_w_out': 'delta_w', 'delta_norm2_w': 'delta_w', 'delta_mlp_w1': 'delta_w', 'delta_mlp_w2': 'delta_w', 'delta_final_norm_w': 'delta_w', 'new_m_meta_tokens': 'new_m', 'new_m_norm1_w': 'new_m', 'new_m_w_in': 'new_m', 'new_m_gate_w2': 'new_m', 'new_m_gate_b': 'new_m', 'new_m_gla_norm_w': 'new_m', 'new_m_pool_w': 'new_m', 'new_m_pool_scale': 'new_m', 'new_m_w_out': 'new_m', 'new_m_norm2_w': 'new_m', 'new_m_mlp_w1': 'new_m', 'new_m_mlp_w2': 'new_m', 'new_m_final_norm_w': 'new_m', 'new_v_meta_tokens': 'new_v', 'new_v_norm1_w': 'new_v', 'new_v_w_in': 'new_v', 'new_v_gate_w2': 'new_v', 'new_v_gate_b': 'new_v', 'new_v_gla_norm_w': 'new_v', 'new_v_pool_w': 'new_v', 'new_v_pool_scale': 'new_v', 'new_v_w_out': 'new_v', 'new_v_norm2_w': 'new_v', 'new_v_mlp_w1': 'new_v', 'new_v_mlp_w2': 'new_v', 'new_v_final_norm_w': 'new_v'}


def _forward(args):
    return _fwd_reference(*[args[k] for k in FWD_PARAMS])


def _output_shape():
    out = _jax.eval_shape(lambda: _forward(_fwd_setup_inputs(0)))
    return out.shape, out.dtype

N_MICROBATCH = 1
ADAM_LR = 0.001
ADAM_B1 = 0.9
ADAM_B2 = 0.999
ADAM_EPS = 1e-08
ADAM_WD = 0.01
ADAM_STEP = 10
PER_EXAMPLE_BATCH_AXIS = {'x': 0, 'loss_target': 0}
SHARED_INPUTS = []
_WEIGHT_DTYPES = {'meta_tokens': _jnp.float32, 'norm1_w': _jnp.float32, 'w_in': _jnp.float32, 'gate_w2': _jnp.float32, 'gate_b': _jnp.float32, 'gla_norm_w': _jnp.float32, 'pool_w': _jnp.float32, 'pool_scale': _jnp.float32, 'w_out': _jnp.float32, 'norm2_w': _jnp.float32, 'mlp_w1': _jnp.float32, 'mlp_w2': _jnp.float32, 'final_norm_w': _jnp.float32}
MOMENT_SCALE = {'meta_tokens': 1.563896e-03, 'norm1_w': 6.016591e-02, 'w_in': 4.076847e-02, 'gate_w2': 5.144766e-03, 'gate_b': 2.044415e-02, 'gla_norm_w': 7.120531e-02, 'pool_w': 4.741558e-02, 'pool_scale': 4.834787e-02, 'w_out': 4.070999e-02, 'norm2_w': 5.164442e-02, 'mlp_w1': 2.534259e-02, 'mlp_w2': 4.800771e-02, 'final_norm_w': 8.058278e+00}


def _to_microbatches(a, axis):
    t = _jnp.moveaxis(a, axis, 0)
    t = t.reshape((N_MICROBATCH, t.shape[0] // N_MICROBATCH) + t.shape[1:])
    return _jnp.moveaxis(t, 1, axis + 1)


def setup_inputs(seed: int = 0) -> dict:
    inp = _fwd_setup_inputs(seed)
    key = _jax.random.fold_in(_jax.random.key(seed), 7919)
    shape, _ = _output_shape()
    out = dict(inp)
    out["loss_target"] = _jax.random.normal(_jax.random.fold_in(key, 0), shape, _jnp.float32)
    for i, name in enumerate(TWIN_WEIGHTS):
        w = inp[name].astype(_jnp.float32)
        if MOMENT_SCALE is None:
            s = _jnp.sqrt(_jnp.mean(_jnp.square(w)) + 1e-30)
        else:
            s = MOMENT_SCALE[name]
        km, kv = _jax.random.split(_jax.random.fold_in(key, i + 1))
        out[name] = w
        out["m_" + name] = s * _jax.random.normal(km, w.shape, _jnp.float32)
        out["v_" + name] = (s * s) * _jax.random.uniform(kv, w.shape, _jnp.float32, 0.5, 1.5)
    if N_MICROBATCH > 1:
        for name, axis in PER_EXAMPLE_BATCH_AXIS.items():
            out[name] = _to_microbatches(out[name], axis)
    return {'x': out['x'], 'meta_tokens': out['meta_tokens'], 'norm1_w': out['norm1_w'], 'w_in': out['w_in'], 'gate_w2': out['gate_w2'], 'gate_b': out['gate_b'], 'gla_norm_w': out['gla_norm_w'], 'pool_w': out['pool_w'], 'pool_scale': out['pool_scale'], 'w_out': out['w_out'], 'norm2_w': out['norm2_w'], 'mlp_w1': out['mlp_w1'], 'mlp_w2': out['mlp_w2'], 'final_norm_w': out['final_norm_w'], 'loss_target': out['loss_target'], 'm_meta_tokens': out['m_meta_tokens'], 'm_norm1_w': out['m_norm1_w'], 'm_w_in': out['m_w_in'], 'm_gate_w2': out['m_gate_w2'], 'm_gate_b': out['m_gate_b'], 'm_gla_norm_w': out['m_gla_norm_w'], 'm_pool_w': out['m_pool_w'], 'm_pool_scale': out['m_pool_scale'], 'm_w_out': out['m_w_out'], 'm_norm2_w': out['m_norm2_w'], 'm_mlp_w1': out['m_mlp_w1'], 'm_mlp_w2': out['m_mlp_w2'], 'm_final_norm_w': out['m_final_norm_w'], 'v_meta_tokens': out['v_meta_tokens'], 'v_norm1_w': out['v_norm1_w'], 'v_w_in': out['v_w_in'], 'v_gate_w2': out['v_gate_w2'], 'v_gate_b': out['v_gate_b'], 'v_gla_norm_w': out['v_gla_norm_w'], 'v_pool_w': out['v_pool_w'], 'v_pool_scale': out['v_pool_scale'], 'v_w_out': out['v_w_out'], 'v_norm2_w': out['v_norm2_w'], 'v_mlp_w1': out['v_mlp_w1'], 'v_mlp_w2': out['v_mlp_w2'], 'v_final_norm_w': out['v_final_norm_w']}


def _loss(weights, diff, rest, loss_target):
    with _jax.named_scope("forward"):
        args = {**rest, TWIN_DIFF_INPUT: diff, **{k: w.astype(_WEIGHT_DTYPES[k]) for k, w in weights.items()}}
        y = _forward(args)
    with _jax.named_scope("loss_head"):
        err = _jnp.square(y.astype(_jnp.float32) - loss_target)
        return 0.5 * _jnp.sum(_jnp.mean(err, axis=-1)) if err.ndim else 0.5 * err


def _adamw(w, g, m, v):
    m = ADAM_B1 * m + (1.0 - ADAM_B1) * g
    v = ADAM_B2 * v + (1.0 - ADAM_B2) * _jnp.square(g)
    m_hat = m / (1.0 - ADAM_B1 ** ADAM_STEP)
    v_hat = v / (1.0 - ADAM_B2 ** ADAM_STEP)
    delta = -ADAM_LR * (m_hat / (_jnp.sqrt(v_hat) + ADAM_EPS) + ADAM_WD * w)
    return delta, m, v


def reference(x, meta_tokens, norm1_w, w_in, gate_w2, gate_b, gla_norm_w, pool_w, pool_scale, w_out, norm2_w, mlp_w1, mlp_w2, final_norm_w, loss_target, m_meta_tokens, m_norm1_w, m_w_in, m_gate_w2, m_gate_b, m_gla_norm_w, m_pool_w, m_pool_scale, m_w_out, m_norm2_w, m_mlp_w1, m_mlp_w2, m_final_norm_w, v_meta_tokens, v_norm1_w, v_w_in, v_gate_w2, v_gate_b, v_gla_norm_w, v_pool_w, v_pool_scale, v_w_out, v_norm2_w, v_mlp_w1, v_mlp_w2, v_final_norm_w):
    given = dict(x=x, meta_tokens=meta_tokens, norm1_w=norm1_w, w_in=w_in, gate_w2=gate_w2, gate_b=gate_b, gla_norm_w=gla_norm_w, pool_w=pool_w, pool_scale=pool_scale, w_out=w_out, norm2_w=norm2_w, mlp_w1=mlp_w1, mlp_w2=mlp_w2, final_norm_w=final_norm_w, loss_target=loss_target, m_meta_tokens=m_meta_tokens, m_norm1_w=m_norm1_w, m_w_in=m_w_in, m_gate_w2=m_gate_w2, m_gate_b=m_gate_b, m_gla_norm_w=m_gla_norm_w, m_pool_w=m_pool_w, m_pool_scale=m_pool_scale, m_w_out=m_w_out, m_norm2_w=m_norm2_w, m_mlp_w1=m_mlp_w1, m_mlp_w2=m_mlp_w2, m_final_norm_w=m_final_norm_w, v_meta_tokens=v_meta_tokens, v_norm1_w=v_norm1_w, v_w_in=v_w_in, v_gate_w2=v_gate_w2, v_gate_b=v_gate_b, v_gla_norm_w=v_gla_norm_w, v_pool_w=v_pool_w, v_pool_scale=v_pool_scale, v_w_out=v_w_out, v_norm2_w=v_norm2_w, v_mlp_w1=v_mlp_w1, v_mlp_w2=v_mlp_w2, v_final_norm_w=v_final_norm_w)
    weights = {n: given[n] for n in TWIN_WEIGHTS}
    shared = {n: given[n] for n in SHARED_INPUTS}
    per_example = {n: given[n] for n in ['x']}
    grad_fn = _jax.value_and_grad(_loss, argnums=(0, 1))

    def one_microbatch(ex, loss_target):
        ex = dict(ex)
        diff = ex.pop(TWIN_DIFF_INPUT)
        return grad_fn(weights, diff, {**shared, **ex}, loss_target)

    if N_MICROBATCH == 1:
        loss, (grad_w, grad_x) = one_microbatch(per_example, given["loss_target"])
    else:
        def body(carry, xs):
            loss_sum, grad_sum = carry
            l_k, (gw_k, gx_k) = one_microbatch(xs[0], xs[1])
            with _jax.named_scope("update"):
                return (loss_sum + l_k, _jax.tree.map(_jnp.add, grad_sum, gw_k)), gx_k

        init = (_jnp.zeros((), _jnp.float32), _jax.tree.map(_jnp.zeros_like, weights))
        (loss, grad_w), grad_x = _jax.lax.scan(body, init, (per_example, given["loss_target"]))
    with _jax.named_scope("update"):
        delta_w, new_m, new_v = {}, {}, {}
        for n in TWIN_WEIGHTS:
            delta_w[n], new_m[n], new_v[n] = _adamw(weights[n], grad_w[n], given["m_" + n], given["v_" + n])
    return (loss, grad_x, *[grad_w[n] for n in TWIN_WEIGHTS], *[delta_w[n] for n in TWIN_WEIGHTS],
            *[new_m[n] for n in TWIN_WEIGHTS], *[new_v[n] for n in TWIN_WEIGHTS])
```

```python
import functools

import jax
import jax.numpy as jnp
from jax import lax
from jax.experimental import pallas as pl
from jax.experimental.pallas import tpu as pltpu

F32 = jnp.float32
BF16 = jnp.bfloat16

D = 2048
SEQ = 2048
N_META = 16
CH = 64
TP = 2176
NCH = TP // CH
ROW_LO = 48
X_LO = 64
ROW_HI = 2112
HEADS = 4
DK = 128
DV = 256
KW = HEADS * DK
GW = HEADS * DV
RANK = 16
TAU = 16.0
WINDOWS = (2, 4, 8, 16)
PW = 1024
GC = 256
DFF = 8192
EPS = 1e-6
SHARD_IN = 1028
PAD_IN = 1152
N_CHIP = 4

LR = 0.001
B1 = 0.9
B2 = 0.999
AEPS = 1e-08
WD = 0.01
STEP = 10

VMEM_LIMIT = 60 * 1024 * 1024
ANY = pl.BlockSpec(memory_space=pl.ANY)
VMEM_FULL = pl.BlockSpec(memory_space=pltpu.VMEM)
MESH = pl.DeviceIdType.MESH


def _cp(sem=None):
    if sem is None:
        return pltpu.CompilerParams(vmem_limit_bytes=VMEM_LIMIT)
    return pltpu.CompilerParams(dimension_semantics=sem, vmem_limit_bytes=VMEM_LIMIT)


def _dot(a, b):
    return jnp.dot(a, b, preferred_element_type=F32)


def _dot_nt(a, b):
    return lax.dot_general(a, b, (((1,), (1,)), ((), ())), preferred_element_type=F32)


def _dot_tn(a, b):
    return lax.dot_general(a, b, (((0,), (0,)), ((), ())), preferred_element_type=F32)


def _sds(shape, dtype):
    return jax.ShapeDtypeStruct(shape, dtype)


def _embed_norm(x, meta_full, w):
    def body(x_ref, meta_ref, w_ref, h_ref, u_ref):
        i = pl.program_id(0)

        @pl.when(i == 0)
        def _():
            h_ref[...] = jnp.zeros_like(h_ref)
            h_ref[ROW_LO:X_LO, :] = meta_ref[...]

        @pl.when((i >= 1) & (i <= 32))
        def _():
            h_ref[...] = x_ref[...]

        @pl.when(i == 33)
        def _():
            h_ref[...] = jnp.zeros_like(h_ref)

        h = h_ref[...]
        r = lax.rsqrt(jnp.mean(h * h, axis=-1, keepdims=True) + EPS)
        u_ref[...] = ((h * r) * w_ref[...]).astype(BF16)

    return pl.pallas_call(
        body, name="embed_norm1", grid=(NCH,),
        in_specs=[pl.BlockSpec((CH, D), lambda i: (jnp.clip(i - 1, 0, 31), 0)),
                  pl.BlockSpec((N_META, D), lambda i: (0, 0)),
                  pl.BlockSpec((1, D), lambda i: (0, 0))],
        out_specs=[pl.BlockSpec((CH, D), lambda i: (i, 0)), pl.BlockSpec((CH, D), lambda i: (i, 0))],
        out_shape=[_sds((TP, D), F32), _sds((TP, D), BF16)],
        compiler_params=_cp(("arbitrary",)),
    )(x, meta_full, w)


def _norm_rows(h, w, name):
    tr = 272

    def body(h_ref, w_ref, o_ref):
        hv = h_ref[...]
        r = lax.rsqrt(jnp.mean(hv * hv, axis=-1, keepdims=True) + EPS)
        o_ref[...] = ((hv * r) * w_ref[...]).astype(BF16)

    return pl.pallas_call(
        body, name=name, grid=(TP // tr,),
        in_specs=[pl.BlockSpec((tr, D), lambda i: (i, 0)), pl.BlockSpec((1, D), lambda i: (0, 0))],
        out_specs=pl.BlockSpec((tr, D), lambda i: (i, 0)),
        out_shape=_sds((TP, D), BF16),
        compiler_params=_cp(("arbitrary",)),
    )(h, w)


def _loss_head(h2, target, fw):
    def body(h_ref, t_ref, w_ref, dh_ref, dhb_ref, dw_ref, loss_ref):
        i = pl.program_id(0)

        @pl.when(i == 0)
        def _():
            dw_ref[...] = jnp.zeros_like(dw_ref)
            loss_ref[...] = jnp.zeros_like(loss_ref)

        h = h_ref[...]
        w = w_ref[...]
        r = lax.rsqrt(jnp.mean(h * h, axis=-1, keepdims=True) + EPS)
        xh = h * r
        y = xh * w
        is_x = ((i >= 1) & (i <= 32)).astype(F32)
        diff = (y - t_ref[...]) * is_x
        loss_ref[...] += jnp.sum(diff * diff) * (0.5 / D)
        dy = diff * (1.0 / D)
        dw_ref[...] += jnp.sum(dy * xh, axis=0, keepdims=True)
        gx = dy * w
        dh = r * (gx - xh * jnp.mean(gx * xh, axis=-1, keepdims=True))
        dh_ref[...] = dh
        dhb_ref[...] = dh.astype(BF16)

    return pl.pallas_call(
        body, name="loss_head", grid=(NCH,),
        in_specs=[pl.BlockSpec((CH, D), lambda i: (i, 0)),
                  pl.BlockSpec((CH, D), lambda i: (jnp.clip(i - 1, 0, 31), 0)),
                  pl.BlockSpec((1, D), lambda i: (0, 0))],
        out_specs=[pl.BlockSpec((CH, D), lambda i: (i, 0)), pl.BlockSpec((CH, D), lambda i: (i, 0)),
                   pl.BlockSpec((1, D), lambda i: (0, 0)), pl.BlockSpec((8, 128), lambda i: (0, 0))],
        out_shape=[_sds((TP, D), F32), _sds((TP, D), BF16), _sds((1, D), F32), _sds((8, 128), F32)],
        compiler_params=_cp(("arbitrary",)),
    )(h2, target, fw)


def _norm_bwd(dn, h, dres, w, name):
    tr = 272

    def body(dn_ref, h_ref, dres_ref, w_ref, o_ref, ob_ref, dw_ref):
        @pl.when(pl.program_id(0) == 0)
        def _():
            dw_ref[...] = jnp.zeros_like(dw_ref)

        hv = h_ref[...]
        dnv = dn_ref[...]
        r = lax.rsqrt(jnp.mean(hv * hv, axis=-1, keepdims=True) + EPS)
        xh = hv * r
        dw_ref[...] += jnp.sum(dnv * xh, axis=0, keepdims=True)
        gx = dnv * w_ref[...]
        dh = dres_ref[...] + r * (gx - xh * jnp.mean(gx * xh, axis=-1, keepdims=True))
        o_ref[...] = dh
        ob_ref[...] = dh.astype(BF16)

    row = pl.BlockSpec((tr, D), lambda i: (i, 0))
    vec = pl.BlockSpec((1, D), lambda i: (0, 0))
    return pl.pallas_call(
        body, name=name, grid=(TP // tr,),
        in_specs=[row, row, row, vec], out_specs=[row, row, vec],
        out_shape=[_sds((TP, D), F32), _sds((TP, D), BF16), _sds((1, D), F32)],
        compiler_params=_cp(("arbitrary",)),
    )(dn, h, dres, w)


def _input_grad(du, h0, dh1, w):
    def body(du_ref, h_ref, dres_ref, w_ref, gx_ref, gm_ref, dw_ref):
        i = pl.program_id(0)

        @pl.when(i == 0)
        def _():
            dw_ref[...] = jnp.zeros_like(dw_ref)

        hv = h_ref[...]
        dnv = du_ref[...]
        r = lax.rsqrt(jnp.mean(hv * hv, axis=-1, keepdims=True) + EPS)
        xh = hv * r
        dw_ref[...] += jnp.sum(dnv * xh, axis=0, keepdims=True)
        g = dnv * w_ref[...]
        dh = dres_ref[...] + r * (g - xh * jnp.mean(g * xh, axis=-1, keepdims=True))

        @pl.when(i == 0)
        def _():
            gm_ref[...] = dh[ROW_LO:X_LO, :]

        @pl.when((i >= 1) & (i <= 32))
        def _():
            gx_ref[...] = dh

    row = pl.BlockSpec((CH, D), lambda i: (i, 0))
    vec = pl.BlockSpec((1, D), lambda i: (0, 0))
    return pl.pallas_call(
        body, name="input_grad", grid=(NCH,),
        in_specs=[row, row, row, vec],
        out_specs=[pl.BlockSpec((CH, D), lambda i: (jnp.clip(i - 1, 0, 31), 0)),
                   pl.BlockSpec((N_META, D), lambda i: (0, 0)), vec],
        out_shape=[_sds((SEQ, D), F32), _sds((N_META, D), F32), _sds((1, D), F32)],
        compiler_params=_cp(("arbitrary",)),
    )(du, h0, dh1, w)


def _in_proj(u, wg):
    def body(u_ref, w_ref, o_ref):
        o_ref[0] = _dot(u_ref[...], w_ref[0])

    return pl.pallas_call(
        body, name="in_proj", grid=(N_CHIP,),
        in_specs=[VMEM_FULL, pl.BlockSpec((1, D, PAD_IN), lambda k: (k, 0, 0))],
        out_specs=pl.BlockSpec((1, TP, PAD_IN), lambda k: (k, 0, 0)),
        out_shape=_sds((N_CHIP, TP, PAD_IN), F32),
        compiler_params=_cp(("arbitrary",)),
    )(u, wg)


def _out_proj(og, op, wout, h0):
    tn = 512

    def body(og_ref, op_ref, w_ref, h_ref, o_ref):
        acc = _dot(og_ref[...], w_ref[0:GW, :]) + _dot(op_ref[...], w_ref[GW:D, :])
        o_ref[...] = h_ref[...] + acc

    return pl.pallas_call(
        body, name="out_proj", grid=(D // tn,),
        in_specs=[VMEM_FULL, VMEM_FULL, pl.BlockSpec((D, tn), lambda j: (0, j)),
                  pl.BlockSpec((TP, tn), lambda j: (0, j))],
        out_specs=pl.BlockSpec((TP, tn), lambda j: (0, j)),
        out_shape=_sds((TP, D), F32),
        compiler_params=_cp(("arbitrary",)),
    )(og, op, wout, h0)


def _mlp_up(n2, w1g):
    tn = 512
    per = D // tn

    def body(n_ref, w_ref, zr_ref, a_ref):
        z = jnp.maximum(_dot(n_ref[...], w_ref[0]), 0.0)
        zr_ref[...] = z.astype(BF16)
        a_ref[...] = (z * z).astype(BF16)

    col = pl.BlockSpec((TP, tn), lambda k, j: (0, k * per + j))
    return pl.pallas_call(
        body, name="mlp_up", grid=(N_CHIP, per),
        in_specs=[VMEM_FULL, pl.BlockSpec((1, D, tn), lambda k, j: (k, 0, j))],
        out_specs=[col, col],
        out_shape=[_sds((TP, DFF), BF16), _sds((TP, DFF), BF16)],
        compiler_params=_cp(("arbitrary", "arbitrary")),
    )(n2, w1g)


def _mlp_down(a, w2, h1):
    tk = 512
    nk = DFF // tk

    def body(a_ref, w_ref, h_ref, o_ref, acc_ref):
        k = pl.program_id(0)

        @pl.when(k == 0)
        def _():
            pltpu.sync_copy(h_ref, acc_ref)

        acc_ref[...] += _dot(a_ref[...], w_ref[...])

        @pl.when(k == nk - 1)
        def _():
            pltpu.sync_copy(acc_ref, o_ref)

    return pl.pallas_call(
        body, name="mlp_down", grid=(nk,),
        in_specs=[pl.BlockSpec((TP, tk), lambda k: (0, k)), pl.BlockSpec((tk, D), lambda k: (k, 0)), ANY],
        out_specs=ANY,
        out_shape=_sds((TP, D), F32),
        scratch_shapes=[pltpu.VMEM((TP, D), F32)],
        compiler_params=_cp(("arbitrary",)),
    )(a, w2, h1)


def _mlp_dz(dh2b, w2, zr):
    tn = 512

    def body(d_ref, w_ref, z_ref, o_ref):
        da = _dot_nt(d_ref[...], w_ref[...])
        o_ref[...] = (da * (2.0 * z_ref[...].astype(F32))).astype(BF16)

    col = pl.BlockSpec((TP, tn), lambda j: (0, j))
    return pl.pallas_call(
        body, name="mlp_dz", grid=(DFF // tn,),
        in_specs=[VMEM_FULL, pl.BlockSpec((tn, D), lambda j: (j, 0)), col],
        out_specs=col,
        out_shape=_sds((TP, DFF), BF16),
        compiler_params=_cp(("arbitrary",)),
    )(dh2b, w2, zr)


def _grad_w2(a, dh2b):
    tm = 512

    def body(a_ref, d_ref, o_ref):
        o_ref[...] = _dot_tn(a_ref[...], d_ref[...])

    return pl.pallas_call(
        body, name="grad_w2", grid=(DFF // tm,),
        in_specs=[pl.BlockSpec((TP, tm), lambda j: (0, j)), VMEM_FULL],
        out_specs=pl.BlockSpec((tm, D), lambda j: (j, 0)),
        out_shape=_sds((DFF, D), F32),
        compiler_params=_cp(("arbitrary",)),
    )(a, dh2b)


def _grad_w1(n2, dz):
    tn = 512
    per = D // tn

    def body(n_ref, d_ref, o_ref):
        o_ref[0] = _dot_tn(n_ref[...], d_ref[...])

    return pl.pallas_call(
        body, name="grad_w1", grid=(N_CHIP, per),
        in_specs=[VMEM_FULL, pl.BlockSpec((TP, tn), lambda k, j: (0, k * per + j))],
        out_specs=pl.BlockSpec((1, D, tn), lambda k, j: (k, 0, j)),
        out_shape=_sds((N_CHIP, D, D), F32),
        compiler_params=_cp(("arbitrary", "arbitrary")),
    )(n2, dz)


def _mlp_dn(dz, w1g):
    tk = 512
    per = D // tk
    nk = DFF // tk

    def body(d_ref, w_ref, o_ref, acc_ref):
        k = pl.program_id(0)
        part = _dot_nt(d_ref[...], w_ref[0])

        @pl.when(k == 0)
        def _():
            acc_ref[...] = part

        @pl.when(k > 0)
        def _():
            acc_ref[...] += part

        @pl.when(k == nk - 1)
        def _():
            pltpu.sync_copy(acc_ref, o_ref)

    return pl.pallas_call(
        body, name="mlp_dn", grid=(nk,),
        in_specs=[pl.BlockSpec((TP, tk), lambda k: (0, k)),
                  pl.BlockSpec((1, D, tk), lambda k: (k // per, 0, k % per))],
        out_specs=ANY,
        out_shape=_sds((TP, D), F32),
        scratch_shapes=[pltpu.VMEM((TP, D), F32)],
        compiler_params=_cp(("arbitrary",)),
    )(dz, w1g)


def _mixed_grad(dh1b, wout):
    tn = 512

    def body(d_ref, w_ref, o_ref):
        o_ref[...] = _dot_nt(d_ref[...], w_ref[...])

    return pl.pallas_call(
        body, name="mixed_grad", grid=(D // tn,),
        in_specs=[VMEM_FULL, pl.BlockSpec((tn, D), lambda j: (j, 0))],
        out_specs=pl.BlockSpec((TP, tn), lambda j: (0, j)),
        out_shape=_sds((TP, D), F32),
        compiler_params=_cp(("arbitrary",)),
    )(dh1b, wout)


def _grad_wout(og, op, dh1b):
    tm = 512

    def body(og_ref, op_ref, d_ref, o_ref):
        j = pl.program_id(0)

        @pl.when(j < 2)
        def _():
            o_ref[0] = _dot_tn(og_ref[...], d_ref[...])

        @pl.when(j >= 2)
        def _():
            o_ref[0] = _dot_tn(op_ref[...], d_ref[...])

    return pl.pallas_call(
        body, name="grad_wout", grid=(N_CHIP,),
        in_specs=[pl.BlockSpec((TP, tm), lambda j: (0, jnp.minimum(j, 1))),
                  pl.BlockSpec((TP, tm), lambda j: (0, jnp.maximum(j - 2, 0))), VMEM_FULL],
        out_specs=pl.BlockSpec((1, tm, D), lambda j: (j, 0, 0)),
        out_shape=_sds((N_CHIP, tm, D), F32),
        compiler_params=_cp(("arbitrary",)),
    )(og, op, dh1b)


def _in_grad(dp, wg):
    def body(d_ref, w_ref, o_ref, acc_ref):
        k = pl.program_id(0)
        part = _dot_nt(d_ref[0], w_ref[0])

        @pl.when(k == 0)
        def _():
            acc_ref[...] = part

        @pl.when(k > 0)
        def _():
            acc_ref[...] += part

        @pl.when(k == N_CHIP - 1)
        def _():
            pltpu.sync_copy(acc_ref, o_ref)

    return pl.pallas_call(
        body, name="in_grad", grid=(N_CHIP,),
        in_specs=[pl.BlockSpec((1, TP, PAD_IN), lambda k: (k, 0, 0)),
                  pl.BlockSpec((1, D, PAD_IN), lambda k: (k, 0, 0))],
        out_specs=ANY,
        out_shape=_sds((TP, D), F32),
        scratch_shapes=[pltpu.VMEM((TP, D), F32)],
        compiler_params=_cp(("arbitrary",)),
    )(dp, wg)


def _grad_win(u, dp):
    def body(u_ref, d_ref, o_ref):
        g = _dot_tn(u_ref[...], d_ref[0])
        o_ref[0] = g[:, 0:SHARD_IN]

    return pl.pallas_call(
        body, name="grad_win", grid=(N_CHIP,),
        in_specs=[VMEM_FULL, pl.BlockSpec((1, TP, PAD_IN), lambda k: (k, 0, 0))],
        out_specs=pl.BlockSpec((1, D, SHARD_IN), lambda k: (k, 0, 0)),
        out_shape=_sds((N_CHIP, D, SHARD_IN), F32),
        compiler_params=_cp(("arbitrary",)),
    )(u, dp)


def _split3(x):
    hi = x.astype(BF16)
    r1 = x - hi.astype(F32)
    mid = r1.astype(BF16)
    lo = (r1 - mid.astype(F32)).astype(BF16)
    return hi, mid, lo


def _tri_sum(tri, x):
    hi, mid, lo = _split3(x)
    return _dot(tri, hi) + _dot(tri, mid) + _dot(tri, lo)


def _gla_common(n, glr, gw2, gb):
    rows = n * CH + lax.broadcasted_iota(jnp.int32, (CH, 1), 0)
    valid = (rows >= ROW_LO) & (rows < ROW_HI)
    g_raw = _dot(glr.astype(BF16), gw2.astype(BF16)) + gb
    logsig = jnp.minimum(g_raw, 0.0) - jnp.log(1.0 + jnp.exp(-jnp.abs(g_raw)))
    logg = jnp.where(valid, logsig * (1.0 / TAU), 0.0)
    ci = lax.broadcasted_iota(jnp.int32, (CH, CH), 0)
    si = lax.broadcasted_iota(jnp.int32, (CH, CH), 1)
    lower = ci >= si
    G = _tri_sum(lower.astype(BF16), logg)
    Gl = G[CH - 1:CH, :]
    return valid, g_raw, lower, G, Gl


def _gla_fwd(q, k, v, r, glr, gw2, gb, gnw):
    scale = DK ** -0.5

    def body(q_ref, k_ref, v_ref, r_ref, glr_ref, gw2_ref, gb_ref, gnw_ref, o_ref, og_ref, sp_ref, st_ref):
        n = pl.program_id(0)

        @pl.when(n == 0)
        def _():
            st_ref[...] = jnp.zeros_like(st_ref)

        _, _, lower, G, Gl = _gla_common(n, glr_ref[...], gw2_ref[...], gb_ref[...])
        eG = jnp.exp(G)
        eN = jnp.exp(-G)
        eE = jnp.exp(Gl - G)
        dec = jnp.exp(Gl)
        gnw_v = gnw_ref[...]
        for h in range(HEADS):
            ks = slice(h * DK, (h + 1) * DK)
            vs = slice(h * DV, (h + 1) * DV)
            kh = k_ref[:, ks]
            vh = v_ref[:, vs].astype(BF16)
            qd = ((q_ref[:, ks] * scale) * eG[:, ks]).astype(BF16)
            ki = (kh * eN[:, ks]).astype(BF16)
            ke = (kh * eE[:, ks]).astype(BF16)
            st = st_ref[h]
            a = jnp.where(lower, _dot_nt(qd, ki), 0.0).astype(BF16)
            o = _dot(a, vh) + _dot_nt(qd, st.astype(BF16))
            sp_ref[0, h] = st
            st_ref[h] = st * dec[:, ks] + _dot_tn(vh, ke)
            o_ref[:, vs] = o
            rs = lax.rsqrt(jnp.mean(o * o, axis=-1, keepdims=True) + EPS)
            rv = r_ref[:, vs]
            gate = rv / (1.0 + jnp.exp(-rv))
            og_ref[:, vs] = (((o * rs) * gnw_v) * gate).astype(BF16)

    rk = pl.BlockSpec((CH, KW), lambda n: (n, 0))
    rv_ = pl.BlockSpec((CH, GW), lambda n: (n, 0))

    def full(shape):
        return pl.BlockSpec(shape, lambda n: tuple(0 for _ in shape))

    return pl.pallas_call(
        body, name="gla_fwd", grid=(NCH,),
        in_specs=[rk, rk, rv_, rv_, pl.BlockSpec((CH, RANK), lambda n: (n, 0)),
                  full((RANK, KW)), full((1, KW)), full((1, DV))],
        out_specs=[rv_, rv_, pl.BlockSpec((1, HEADS, DV, DK), lambda n: (n, 0, 0, 0))],
        out_shape=[_sds((TP, GW), F32), _sds((TP, GW), BF16), _sds((NCH, HEADS, DV, DK), F32)],
        scratch_shapes=[pltpu.VMEM((HEADS, DV, DK), F32)],
        compiler_params=_cp(("arbitrary",)),
    )(q, k, v, r, glr, gw2, gb, gnw)


def _gla_bwd(dog, o, q, k, v, r, glr, gw2, gb, gnw, sp):
    scale = DK ** -0.5

    def body(dog_ref, o_ref, q_ref, k_ref, v_ref, r_ref, glr_ref, gw2_ref, gb_ref, gnw_ref, sp_ref,
             dq_ref, dk_ref, dv_ref, dr_ref, dglr_ref, dgw2_ref, dgb_ref, dgnw_ref, ds_ref):
        step = pl.program_id(0)
        n = NCH - 1 - step

        @pl.when(step == 0)
        def _():
            ds_ref[...] = jnp.zeros_like(ds_ref)
            dgw2_ref[...] = jnp.zeros_like(dgw2_ref)
            dgb_ref[...] = jnp.zeros_like(dgb_ref)
            dgnw_ref[...] = jnp.zeros_like(dgnw_ref)

        glr_v = glr_ref[...]
        gw2_b = gw2_ref[...].astype(BF16)
        valid, g_raw, lower, G, Gl = _gla_common(n, glr_v, gw2_ref[...], gb_ref[...])
        upper = lax.broadcasted_iota(jnp.int32, (CH, CH), 0) <= lax.broadcasted_iota(jnp.int32, (CH, CH), 1)
        eG = jnp.exp(G)
        eN = jnp.exp(-G)
        eE = jnp.exp(Gl - G)
        dec = jnp.exp(Gl)
        gnw_v = gnw_ref[...]
        last = lax.broadcasted_iota(jnp.int32, (CH, 1), 0) == CH - 1
        dgnw_acc = jnp.zeros((1, DV), F32)
        dG_parts = []
        for h in range(HEADS):
            ks = slice(h * DK, (h + 1) * DK)
            vs = slice(h * DV, (h + 1) * DV)
            oh = o_ref[:, vs]
            rv = r_ref[:, vs]
            dg = dog_ref[:, vs]
            sig = 1.0 / (1.0 + jnp.exp(-rv))
            gate = rv * sig
            rs = lax.rsqrt(jnp.mean(oh * oh, axis=-1, keepdims=True) + EPS)
            ohat = oh * rs
            dr_ref[:, vs] = (dg * (ohat * gnw_v)) * (sig * (1.0 + rv * (1.0 - sig)))
            don = dg * gate
            dgnw_acc = dgnw_acc + jnp.sum(don * ohat, axis=0, keepdims=True)
            gxn = don * gnw_v
            do = (rs * (gxn - ohat * jnp.mean(gxn * ohat, axis=-1, keepdims=True))).astype(BF16)
            kh = k_ref[:, ks]
            vh = v_ref[:, vs].astype(BF16)
            qd_f = (q_ref[:, ks] * scale) * eG[:, ks]
            ki_f = kh * eN[:, ks]
            ke_f = kh * eE[:, ks]
            qd, ki, ke = qd_f.astype(BF16), ki_f.astype(BF16), ke_f.astype(BF16)
            spt = sp_ref[0, h]
            dst = ds_ref[h]
            dst_b = dst.astype(BF16)
            a_t = jnp.where(upper, _dot_nt(ki, qd), 0.0).astype(BF16)
            da = jnp.where(lower, _dot_nt(do, vh), 0.0).astype(BF16)
            da_t = jnp.where(upper, _dot_nt(vh, do), 0.0).astype(BF16)
            dv_ref[:, vs] = _dot(a_t, do) + _dot_nt(ke, dst_b)
            dqd = _dot(da, ki) + _dot(do, spt.astype(BF16))
            dki = _dot(da_t, qd)
            dke = _dot(vh, dst_b)
            ddec = jnp.sum(spt * dst, axis=0, keepdims=True)
            ds_ref[h] = dst * dec[:, ks] + _dot_tn(do, qd)
            dq_ref[:, ks] = (dqd * eG[:, ks]) * scale
            dk_ref[:, ks] = dki * eN[:, ks] + dke * eE[:, ks]
            dke_ke = dke * ke_f
            dG = dqd * qd_f - dki * ki_f - dke_ke
            dGl = jnp.sum(dke_ke, axis=0, keepdims=True) + ddec * dec[:, ks]
            dG_parts.append(dG + jnp.where(last, dGl, 0.0))
        dgnw_ref[...] += dgnw_acc
        dG_all = jnp.concatenate(dG_parts, axis=1)
        dlogg = jnp.where(valid, _tri_sum(upper.astype(BF16), dG_all), 0.0)
        dg_raw = (dlogg * (1.0 / TAU)) * (1.0 / (1.0 + jnp.exp(g_raw)))
        dgb_ref[...] += jnp.sum(dg_raw, axis=0, keepdims=True)
        dg_b = dg_raw.astype(BF16)
        dgw2_ref[...] += _dot_tn(glr_v.astype(BF16), dg_b)
        dglr_ref[...] = _dot_nt(dg_b, gw2_b)

    rk = pl.BlockSpec((CH, KW), lambda s: (NCH - 1 - s, 0))
    rv_ = pl.BlockSpec((CH, GW), lambda s: (NCH - 1 - s, 0))
    rg = pl.BlockSpec((CH, RANK), lambda s: (NCH - 1 - s, 0))

    def full(shape):
        return pl.BlockSpec(shape, lambda s: tuple(0 for _ in shape))

    return pl.pallas_call(
        body, name="gla_bwd", grid=(NCH,),
        in_specs=[rv_, rv_, rk, rk, rv_, rv_, rg, full((RANK, KW)), full((1, KW)), full((1, DV)),
                  pl.BlockSpec((1, HEADS, DV, DK), lambda s: (NCH - 1 - s, 0, 0, 0))],
        out_specs=[rk, rk, rv_, rv_, rg, full((RANK, KW)), full((1, KW)), full((1, DV))],
        out_shape=[_sds((TP, KW), F32), _sds((TP, KW), F32), _sds((TP, GW), F32), _sds((TP, GW), F32),
                   _sds((TP, RANK), F32), _sds((RANK, KW), F32), _sds((1, KW), F32), _sds((1, DV), F32)],
        scratch_shapes=[pltpu.VMEM((HEADS, DV, DK), F32)],
        compiler_params=_cp(("arbitrary",)),
    )(dog, o, q, k, v, r, glr, gw2, gb, gnw, sp)


POOL_TR = 128
HALO = 16


def _pool_counts(base, nrows):
    rows = base + lax.broadcasted_iota(jnp.int32, (nrows, 1), 0)
    valid = (rows >= ROW_LO) & (rows < ROW_HI)
    t1 = (rows - ROW_LO + 1).astype(F32)
    cnts = [jnp.clip(t1, 1.0, float(w)) for w in WINDOWS]
    return valid, cnts


def _pool_fwd(pu, pw, ps):
    def body(cur_ref, prev_ref, pw_ref, ps_ref, y_ref, op_ref):
        i = pl.program_id(0)
        cur = cur_ref[...]
        full = jnp.concatenate([prev_ref[...], cur], axis=0)
        s2 = full + pltpu.roll(full, 1, 0)
        s4 = s2 + pltpu.roll(s2, 2, 0)
        s8 = s4 + pltpu.roll(s4, 4, 0)
        s16 = s8 + pltpu.roll(s8, 8, 0)
        valid, cnts = _pool_counts(i * POOL_TR, POOL_TR)
        for g, s in enumerate((s2, s4, s8, s16)):
            cs = slice(g * GC, (g + 1) * GC)
            y = s[HALO:, cs] / cnts[g] - cur[:, cs]
            yb = jnp.where(valid, y, 0.0).astype(BF16)
            y_ref[:, cs] = yb
            op_ref[:, cs] = (_dot(yb, pw_ref[g].astype(BF16)) * ps_ref[:, cs]).astype(BF16)

    row = pl.BlockSpec((POOL_TR, PW), lambda i: (i, 0))
    per = POOL_TR // HALO
    return pl.pallas_call(
        body, name="pool_fwd", grid=(TP // POOL_TR,),
        in_specs=[row, pl.BlockSpec((HALO, PW), lambda i: (jnp.maximum(i * per - 1, 0), 0)),
                  pl.BlockSpec((4, GC, GC), lambda i: (0, 0, 0)), pl.BlockSpec((1, PW), lambda i: (0, 0))],
        out_specs=[row, row],
        out_shape=[_sds((TP, PW), BF16), _sds((TP, PW), BF16)],
        compiler_params=_cp(("arbitrary",)),
    )(pu, pu, pw, ps)


def _pool_bwd(dop, y, pw, ps):
    nblk = TP // HALO

    def body(cur_ref, nxt_ref, y_ref, pw_ref, ps_ref, dpu_ref, dpw_ref, dps_ref):
        i = pl.program_id(0)

        @pl.when(i == 0)
        def _():
            dpw_ref[...] = jnp.zeros_like(dpw_ref)
            dps_ref[...] = jnp.zeros_like(dps_ref)

        n_all = POOL_TR + HALO
        dcur = cur_ref[...]
        dall = jnp.concatenate([dcur, nxt_ref[...]], axis=0)
        valid, cnts = _pool_counts(i * POOL_TR, n_all)
        for g in range(4):
            cs = slice(g * GC, (g + 1) * GC)
            pwb = pw_ref[g].astype(BF16)
            yb = y_ref[:, cs]
            dyw = (dall[:, cs] * ps_ref[:, cs]).astype(BF16)
            dps_ref[:, cs] += jnp.sum(dcur[:, cs] * _dot(yb, pwb), axis=0, keepdims=True)
            dpw_ref[g] += _dot_tn(yb, dyw[0:POOL_TR, :])
            dyv = jnp.where(valid, _dot_nt(dyw, pwb), 0.0)
            e = dyv / cnts[g]
            w = WINDOWS[g]
            sh = 1
            while sh < w:
                e = e + pltpu.roll(e, n_all - sh, 0)
                sh *= 2
            dpu_ref[:, cs] = e[0:POOL_TR, :] - dyv[0:POOL_TR, :]

    row = pl.BlockSpec((POOL_TR, PW), lambda i: (i, 0))
    per = POOL_TR // HALO
    return pl.pallas_call(
        body, name="pool_bwd", grid=(TP // POOL_TR,),
        in_specs=[pl.BlockSpec((POOL_TR, PW), lambda i: (i, 1)),
                  pl.BlockSpec((HALO, PW), lambda i: (jnp.minimum(i * per + per, nblk - 1), 1)),
                  row, pl.BlockSpec((4, GC, GC), lambda i: (0, 0, 0)), pl.BlockSpec((1, PW), lambda i: (0, 0))],
        out_specs=[row, pl.BlockSpec((4, GC, GC), lambda i: (0, 0, 0)), pl.BlockSpec((1, PW), lambda i: (0, 0))],
        out_shape=[_sds((TP, PW), F32), _sds((4, GC, GC), F32), _sds((1, PW), F32)],
        compiler_params=_cp(("arbitrary",)),
    )(dop, dop, y, pw, ps)


def _place():
    x, y, c = lax.axis_index("x"), lax.axis_index("y"), lax.axis_index("c")
    chips = [(1 - x, y), (x, 1 - y), (1 - x, 1 - y)]
    return x, y, c, chips


def _cast_pad(w, cols_out, name):
    rows, cols = w.shape
    tr = 256

    def body(w_ref, o_ref):
        if cols_out != cols:
            o_ref[...] = jnp.zeros_like(o_ref)
            o_ref[:, 0:cols] = w_ref[...].astype(BF16)
        else:
            o_ref[...] = w_ref[...].astype(BF16)

    return pl.pallas_call(
        body, name=name, grid=(rows // tr,),
        in_specs=[pl.BlockSpec((tr, cols), lambda i: (i, 0))],
        out_specs=pl.BlockSpec((tr, cols_out), lambda i: (i, 0)),
        out_shape=_sds((rows, cols_out), BF16),
        compiler_params=_cp(("arbitrary",)),
    )(w)


def _gather_weights(win_b, wout_b, w1_b, w2_b, meta, gw2, pw):
    bigs = [win_b, wout_b, w1_b, w2_b]
    nb = len(bigs)

    def body(win, wout, w1, w2, meta_r, gw2_r, pw_r, Win, Wout, W1, W2, metaF, gw2F, pwF, lsem, ssem, rsem):
        x, y, c, chips = _place()
        me = 2 * x + y
        sib = (x, y, 1 - c)
        srcs = [win, wout, w1, w2]
        dsts = [Win, Wout, W1, W2]

        def half(ref, k, which):
            h = ref.shape[1] // 2
            return ref.at[k, pl.ds(pl.multiple_of(which * h, 8), h), :]

        def src_half(ref, which):
            h = ref.shape[0] // 2
            return ref.at[pl.ds(pl.multiple_of(which * h, 8), h), :]

        local = [pltpu.make_async_copy(srcs[w], dsts[w].at[me], lsem.at[w]) for w in range(nb)]
        local.append(pltpu.make_async_copy(meta_r, metaF.at[:, pl.ds(pl.multiple_of(me * 512, 128), 512)], lsem.at[nb]))
        local.append(pltpu.make_async_copy(gw2_r, gw2F.at[:, pl.ds(pl.multiple_of(me * 128, 128), 128)], lsem.at[nb + 1]))
        local.append(pltpu.make_async_copy(pw_r, pwF.at[:, pl.ds(pl.multiple_of(me * 64, 8), 64), :], lsem.at[nb + 2]))
        for cp in local:
            cp.start()

        def remote(src, dst, idx, to):
            return pltpu.make_async_remote_copy(src_ref=src, dst_ref=dst, send_sem=ssem.at[idx], recv_sem=rsem.at[idx],
                                                device_id=to, device_id_type=MESH)

        sends = []
        for w in range(nb):
            for j, chip in enumerate(chips):
                sends.append(remote(src_half(srcs[w], c), half(dsts[w], me, c), 3 * w + j, (*chip, c)))
        for j, chip in enumerate(chips):
            to = (*chip, c)
            sends.append(remote(meta_r, metaF.at[:, pl.ds(pl.multiple_of(me * 512, 128), 512)], 24 + j, to))
            sends.append(remote(gw2_r, gw2F.at[:, pl.ds(pl.multiple_of(me * 128, 128), 128)], 27 + j, to))
            sends.append(remote(pw_r, pwF.at[:, pl.ds(pl.multiple_of(me * 64, 8), 64), :], 30 + j, to))
        for cp in sends:
            cp.start()
        fwd = []
        for w in range(nb):
            for j, (cx, cy) in enumerate(chips):
                kk = 2 * cx + cy
                remote(src_half(srcs[w], c), half(dsts[w], kk, c), 3 * w + j, sib).wait_recv()
                f = remote(half(dsts[w], kk, c), half(dsts[w], kk, c), 12 + 3 * w + j, sib)
                f.start()
                fwd.append(f)
        for w in range(nb):
            for j, (cx, cy) in enumerate(chips):
                kk = 2 * cx + cy
                remote(half(dsts[w], kk, 1 - c), half(dsts[w], kk, 1 - c), 12 + 3 * w + j, sib).wait_recv()
        for j, (cx, cy) in enumerate(chips):
            kk = 2 * cx + cy
            remote(meta_r, metaF.at[:, pl.ds(pl.multiple_of(kk * 512, 128), 512)], 24 + j, sib).wait_recv()
            remote(gw2_r, gw2F.at[:, pl.ds(pl.multiple_of(kk * 128, 128), 128)], 27 + j, sib).wait_recv()
            remote(pw_r, pwF.at[:, pl.ds(pl.multiple_of(kk * 64, 8), 64), :], 30 + j, sib).wait_recv()
        for cp in sends + fwd:
            cp.wait_send()
        for cp in local:
            cp.wait()

    out_shape = [_sds((N_CHIP,) + b.shape, BF16) for b in bigs]
    out_shape += [_sds((N_META, D), F32), _sds((RANK, KW), F32), _sds((4, GC, GC), F32)]
    return pl.pallas_call(
        body, name="gather_weights",
        in_specs=[ANY] * 7, out_specs=[ANY] * 7, out_shape=out_shape,
        scratch_shapes=[pltpu.SemaphoreType.DMA((nb + 3,)), pltpu.SemaphoreType.DMA((33,)),
                        pltpu.SemaphoreType.DMA((33,))],
    )(win_b, wout_b, w1_b, w2_b, meta, gw2, pw)


def _pair_exchange(grads):
    ng = len(grads)

    def body(*refs):
        g = refs[:ng]
        r = refs[ng:2 * ng]
        ssem, rsem = refs[2 * ng], refs[2 * ng + 1]
        x, y, c, _ = _place()
        sib = (x, y, 1 - c)
        cps = []
        for w in range(ng):
            h = g[w].shape[1] // 2
            src = g[w].at[:, pl.ds(pl.multiple_of((1 - c) * h, 8), h), :]
            cps.append(pltpu.make_async_remote_copy(src_ref=src, dst_ref=r[w], send_sem=ssem.at[w], recv_sem=rsem.at[w],
                                                    device_id=sib, device_id_type=MESH))
        for cp in cps:
            cp.start()
        for cp in cps:
            cp.wait()

    out_shape = [_sds((N_CHIP, g.shape[1] // 2, g.shape[2]), F32) for g in grads]
    return pl.pallas_call(
        body, name="pair_exchange",
        in_specs=[ANY] * ng, out_specs=[ANY] * ng, out_shape=out_shape,
        scratch_shapes=[pltpu.SemaphoreType.DMA((ng,)), pltpu.SemaphoreType.DMA((ng,))],
    )(*grads)


def _pair_sum(g, rcv, place, name):
    _, rows, cols = g.shape
    half = rows // 2
    tr = 256
    nt = half // tr

    def body(p_ref, g_ref, r_ref, sb_ref, sf_ref):
        s = pl.program_id(1)
        tot = g_ref[0] + r_ref[0]
        sb_ref[0] = tot.astype(BF16)

        @pl.when(s == p_ref[0])
        def _():
            sf_ref[...] = tot

    grid_spec = pltpu.PrefetchScalarGridSpec(
        num_scalar_prefetch=1, grid=(nt, N_CHIP),
        in_specs=[pl.BlockSpec((1, tr, cols), lambda t, s, p: (s, p[1] * nt + t, 0)),
                  pl.BlockSpec((1, tr, cols), lambda t, s, p: (s, t, 0))],
        out_specs=[pl.BlockSpec((1, tr, cols), lambda t, s, p: (s, t, 0)),
                   pl.BlockSpec((tr, cols), lambda t, s, p: (t, 0))])
    return pl.pallas_call(
        body, name=name, grid_spec=grid_spec,
        out_shape=[_sds((N_CHIP, half, cols), BF16), _sds((half, cols), F32)],
        compiler_params=_cp(("arbitrary", "arbitrary")),
    )(place, g, rcv)


def _chip_exchange(sbs):
    ng = len(sbs)

    def body(*refs):
        s = refs[:ng]
        r = refs[ng:2 * ng]
        ssem, rsem = refs[2 * ng], refs[2 * ng + 1]
        x, y, c, chips = _place()
        me = 2 * x + y
        cps = []
        for w in range(ng):
            for j, (cx, cy) in enumerate(chips):
                kk = 2 * cx + cy
                cps.append(pltpu.make_async_remote_copy(
                    src_ref=s[w].at[kk], dst_ref=r[w].at[j], send_sem=ssem.at[3 * w + j], recv_sem=rsem.at[3 * w + j],
                    device_id=(cx, cy, c), device_id_type=MESH))
        for cp in cps:
            cp.start()
        for cp in cps:
            cp.wait()

    out_shape = [_sds((3,) + s.shape[1:], BF16) for s in sbs]
    return pl.pallas_call(
        body, name="chip_exchange",
        in_specs=[ANY] * ng, out_specs=[ANY] * ng, out_shape=out_shape,
        scratch_shapes=[pltpu.SemaphoreType.DMA((3 * ng,)), pltpu.SemaphoreType.DMA((3 * ng,))],
    )(*sbs)


def _final_sum(sf, rb, name):
    half, cols = sf.shape
    tr = 256

    def body(sf_ref, r_ref, out_ref):
        acc = sf_ref[...]
        for j in range(3):
            acc = acc + r_ref[j].astype(F32)
        out_ref[...] = acc

    return pl.pallas_call(
        body, name=name, grid=(half // tr,),
        in_specs=[pl.BlockSpec((tr, cols), lambda t: (t, 0)), pl.BlockSpec((3, tr, cols), lambda t: (0, t, 0))],
        out_specs=pl.BlockSpec((tr, cols), lambda t: (t, 0)),
        out_shape=_sds((half, cols), F32),
        compiler_params=_cp(("arbitrary",)),
    )(sf, rb)


def _half_exchange(halves):
    ng = len(halves)

    def body(*refs):
        hs = refs[:ng]
        outs = refs[ng:2 * ng]
        lsem, ssem, rsem = refs[2 * ng], refs[2 * ng + 1], refs[2 * ng + 2]
        x, y, c, _ = _place()
        sib = (x, y, 1 - c)
        cps, loc = [], []
        for w in range(ng):
            h = hs[w].shape[0]
            mine = outs[w].at[pl.ds(pl.multiple_of(c * h, 8), h), :]
            loc.append(pltpu.make_async_copy(hs[w], mine, lsem.at[w]))
            cps.append(pltpu.make_async_remote_copy(src_ref=hs[w], dst_ref=mine, send_sem=ssem.at[w], recv_sem=rsem.at[w],
                                                    device_id=sib, device_id_type=MESH))
        for cp in loc + cps:
            cp.start()
        for w in range(ng):
            h = hs[w].shape[0]
            theirs = outs[w].at[pl.ds(pl.multiple_of((1 - c) * h, 8), h), :]
            pltpu.make_async_remote_copy(src_ref=hs[w], dst_ref=theirs, send_sem=ssem.at[w], recv_sem=rsem.at[w],
                                         device_id=sib, device_id_type=MESH).wait_recv()
        for cp in cps:
            cp.wait_send()
        for cp in loc:
            cp.wait()

    out_shape = [_sds((2 * h.shape[0], h.shape[1]), F32) for h in halves]
    return pl.pallas_call(
        body, name="half_exchange",
        in_specs=[ANY] * ng, out_specs=[ANY] * ng, out_shape=out_shape,
        scratch_shapes=[pltpu.SemaphoreType.DMA((ng,)), pltpu.SemaphoreType.DMA((ng,)), pltpu.SemaphoreType.DMA((ng,))],
    )(*halves)


def _small_allreduce(vec):
    nr = vec.shape[0]

    def body(v_ref, o_ref, rsib, pair, rchip, ssem, rsem):
        x, y, c, chips = _place()
        me = 2 * x + y
        sib = (x, y, 1 - c)
        first = pltpu.make_async_remote_copy(src_ref=v_ref, dst_ref=rsib, send_sem=ssem.at[0], recv_sem=rsem.at[0],
                                             device_id=sib, device_id_type=MESH)
        first.start()
        first.wait()
        pair[...] = v_ref[...] + rsib[...]
        cps = [pltpu.make_async_remote_copy(src_ref=pair, dst_ref=rchip.at[j], send_sem=ssem.at[1 + j], recv_sem=rsem.at[1 + j],
                                            device_id=(*chip, c), device_id_type=MESH) for j, chip in enumerate(chips)]
        for cp in cps:
            cp.start()
        for cp in cps:
            cp.wait()
        acc = None
        for kk in range(N_CHIP):
            d = jnp.bitwise_xor(me, kk)
            t = jnp.where(d == 0, pair[...], jnp.where(d == 2, rchip[0], jnp.where(d == 1, rchip[1], rchip[2])))
            acc = t if acc is None else acc + t
        o_ref[...] = acc

    return pl.pallas_call(
        body, name="small_allreduce",
        in_specs=[VMEM_FULL], out_specs=VMEM_FULL, out_shape=_sds((nr, 128), F32),
        scratch_shapes=[pltpu.VMEM((nr, 128), F32), pltpu.VMEM((nr, 128), F32), pltpu.VMEM((3, nr, 128), F32),
                        pltpu.SemaphoreType.DMA((4,)), pltpu.SemaphoreType.DMA((4,))],
        compiler_params=_cp(),
    )(vec)


def _adam_math(w, g, m, v):
    m = B1 * m + (1.0 - B1) * g
    v = B2 * v + (1.0 - B2) * (g * g)
    m_hat = m / (1.0 - B1 ** STEP)
    v_hat = v / (1.0 - B2 ** STEP)
    delta = -LR * (m_hat / (jnp.sqrt(v_hat) + AEPS) + WD * w)
    return delta, m, v


def _adam_big(w, g, m, v, name):
    rows, cols = w.shape
    tr = 128

    def body(w_ref, g_ref, m_ref, v_ref, d_ref, nm_ref, nv_ref):
        d, nm, nv = _adam_math(w_ref[...], g_ref[...], m_ref[...], v_ref[...])
        d_ref[...] = d
        nm_ref[...] = nm
        nv_ref[...] = nv

    blk = pl.BlockSpec((tr, cols), lambda i: (i, 0))
    return pl.pallas_call(
        body, name=name, grid=(rows // tr,),
        in_specs=[blk] * 4, out_specs=[blk] * 3, out_shape=[_sds((rows, cols), F32)] * 3,
        compiler_params=_cp(("arbitrary",)),
    )(w, g, m, v)


def _adam_small(ws, gs, ms, vs):
    n = len(ws)

    def body(*refs):
        for i in range(n):
            d, nm, nv = _adam_math(refs[i][...], refs[n + i][...], refs[2 * n + i][...], refs[3 * n + i][...])
            refs[4 * n + i][...] = d
            refs[5 * n + i][...] = nm
            refs[6 * n + i][...] = nv

    shapes = [_sds(w.shape, F32) for w in ws]
    outs = pl.pallas_call(
        body, name="adam_small",
        in_specs=[VMEM_FULL] * (4 * n), out_specs=[VMEM_FULL] * (3 * n), out_shape=shapes * 3,
        compiler_params=_cp(),
    )(*ws, *gs, *ms, *vs)
    return outs[:n], outs[n:2 * n], outs[2 * n:]


def _pad_rows8(a):
    flat = a.reshape(-1, 128)
    pad = (-flat.shape[0]) % 8
    if pad:
        flat = jnp.concatenate([flat, jnp.zeros((pad, 128), F32)], axis=0)
    return flat


def kernel(x, meta_tokens, norm1_w, w_in, gate_w2, gate_b, gla_norm_w, pool_w, pool_scale, w_out, norm2_w, mlp_w1, mlp_w2, final_norm_w, loss_target, m_meta_tokens, m_norm1_w, m_w_in, m_gate_w2, m_gate_b, m_gla_norm_w, m_pool_w, m_pool_scale, m_w_out, m_norm2_w, m_mlp_w1, m_mlp_w2, m_final_norm_w, v_meta_tokens, v_norm1_w, v_w_in, v_gate_w2, v_gate_b, v_gla_norm_w, v_pool_w, v_pool_scale, v_w_out, v_norm2_w, v_mlp_w1, v_mlp_w2, v_final_norm_w):
    cx, cy, cc = lax.axis_index("x"), lax.axis_index("y"), lax.axis_index("c")
    me = (2 * cx + cy).astype(jnp.int32)

    win_b = _cast_pad(w_in[0], PAD_IN, "cast_win")
    wout_b = _cast_pad(w_out[0], D, "cast_wout")
    w1_b = _cast_pad(mlp_w1[0], D, "cast_w1")
    w2_b = _cast_pad(mlp_w2[0], D, "cast_w2")
    Win, Wout4, W1, W24, metaF, gw2F, pwF = _gather_weights(win_b, wout_b, w1_b, w2_b, meta_tokens, gate_w2[0], pool_w[0])
    Wout = Wout4.reshape(D, D)
    W2 = W24.reshape(DFF, D)
    fw = final_norm_w.reshape(1, D)

    (grad_x, loss8, d_n1w, d_gb, d_gnw, d_ps, d_n2w, d_fw, d_meta, d_gw2, d_pw, g_win, g_wout, g_w1, g_w2) = _local_step(
        x[0], loss_target[0], Win, Wout, W1, W2, metaF, gw2F, pwF, norm1_w, gate_b, gla_norm_w, pool_scale, norm2_w, fw)
    return _reduce_and_update(
        me, cc, grad_x, loss8, d_n1w, d_gb, d_gnw, d_ps, d_n2w, d_fw, d_meta, d_gw2, d_pw, g_win, g_wout, g_w1, g_w2,
        meta_tokens, norm1_w, w_in, gate_w2, gate_b, gla_norm_w, pool_w, pool_scale, w_out, norm2_w, mlp_w1, mlp_w2, fw,
        m_meta_tokens, m_norm1_w, m_w_in, m_gate_w2, m_gate_b, m_gla_norm_w, m_pool_w, m_pool_scale, m_w_out, m_norm2_w,
        m_mlp_w1, m_mlp_w2, m_final_norm_w, v_meta_tokens, v_norm1_w, v_w_in, v_gate_w2, v_gate_b, v_gla_norm_w, v_pool_w,
        v_pool_scale, v_w_out, v_norm2_w, v_mlp_w1, v_mlp_w2, v_final_norm_w)


def _local_step(x, target, Win, Wout, W1, W2, metaF, gw2F, pwF, norm1_w, gate_b, gla_norm_w, pool_scale, norm2_w, fw):
    h0, u = _embed_norm(x, metaF, norm1_w)
    P = _in_proj(u, Win)
    q = P[0, :, 0:512]
    k = P[0, :, 512:1024]
    v = jnp.concatenate([P[0, :, 1024:1028], P[1, :, 0:1020]], axis=1)
    r = jnp.concatenate([P[1, :, 1020:1028], P[2, :, 0:1016]], axis=1)
    glr = jnp.concatenate([P[2, :, 1016:1028], P[3, :, 0:4]], axis=1)
    pu = P[3, :, 4:1028]
    o, og, sp = _gla_fwd(q, k, v, r, glr, gw2F, gate_b, gla_norm_w)
    yb, op = _pool_fwd(pu, pwF, pool_scale)
    h1 = _out_proj(og, op, Wout, h0)
    n2 = _norm_rows(h1, norm2_w, "norm2")
    zr, a = _mlp_up(n2, W1)
    h2 = _mlp_down(a, W2, h1)

    dh2, dh2b, d_fw, loss8 = _loss_head(h2, target, fw)
    dz = _mlp_dz(dh2b, W2, zr)
    g_w2 = _grad_w2(a, dh2b)
    g_w1 = _grad_w1(n2, dz)
    dn2 = _mlp_dn(dz, W1)
    dh1, dh1b, d_n2w = _norm_bwd(dn2, h1, dh2, norm2_w, "norm2_bwd")
    dmixed = _mixed_grad(dh1b, Wout)
    g_wout = _grad_wout(og, op, dh1b)
    dpu, d_pw, d_ps = _pool_bwd(dmixed, yb, pwF, pool_scale)
    dq, dk, dv, dr, dglr, d_gw2, d_gb, d_gnw = _gla_bwd(dmixed, o, q, k, v, r, glr, gw2F, gate_b, gla_norm_w, sp)
    zpad = jnp.zeros((TP, PAD_IN - SHARD_IN), F32)
    dP = jnp.stack([
        jnp.concatenate([dq, dk, dv[:, 0:4], zpad], axis=1),
        jnp.concatenate([dv[:, 4:], dr[:, 0:8], zpad], axis=1),
        jnp.concatenate([dr[:, 8:], dglr[:, 0:12], zpad], axis=1),
        jnp.concatenate([dglr[:, 12:], dpu, zpad], axis=1)]).astype(BF16)
    g_win = _grad_win(u, dP)
    du = _in_grad(dP, Win)
    grad_x, d_meta, d_n1w = _input_grad(du, h0, dh1, norm1_w)
    return grad_x, loss8, d_n1w, d_gb, d_gnw, d_ps, d_n2w, d_fw, d_meta, d_gw2, d_pw, g_win, g_wout, g_w1, g_w2


def _reduce_and_update(me, cc, grad_x, loss8, d_n1w, d_gb, d_gnw, d_ps, d_n2w, d_fw, d_meta, d_gw2, d_pw, g_win, g_wout, g_w1,
                       g_w2, meta_tokens, norm1_w, w_in, gate_w2, gate_b, gla_norm_w, pool_w, pool_scale, w_out, norm2_w,
                       mlp_w1, mlp_w2, fw, m_meta_tokens, m_norm1_w, m_w_in, m_gate_w2, m_gate_b, m_gla_norm_w, m_pool_w,
                       m_pool_scale, m_w_out, m_norm2_w, m_mlp_w1, m_mlp_w2, m_final_norm_w, v_meta_tokens, v_norm1_w, v_w_in,
                       v_gate_w2, v_gate_b, v_gla_norm_w, v_pool_w, v_pool_scale, v_w_out, v_norm2_w, v_mlp_w1, v_mlp_w2,
                       v_final_norm_w):
    grads = [g_win, g_wout, g_w1, g_w2.reshape(N_CHIP, D, D)]
    rcv = _pair_exchange(grads)
    place = jnp.stack([me, cc.astype(jnp.int32)])
    names = ["win", "wout", "w1", "w2"]
    sbs, sfs = [], []
    for g, rc, nm in zip(grads, rcv, names):
        sb, sf = _pair_sum(g, rc, place, "pair_sum_" + nm)
        sbs.append(sb)
        sfs.append(sf)
    rbs = _chip_exchange(sbs)
    halves = [_final_sum(sf, rb, "final_sum_" + nm) for sf, rb, nm in zip(sfs, rbs, names)]
    G_win, G_wout, G_w1, G_w2 = _half_exchange(halves)

    parts = [loss8, d_n1w, d_gb, d_gnw, d_ps, d_n2w, d_fw, d_meta, d_gw2, d_pw]
    packed = [_pad_rows8(p) for p in parts]
    sizes = [p.shape[0] for p in packed]
    red = _small_allreduce(jnp.concatenate(packed, axis=0))
    offs = [0]
    for s in sizes:
        offs.append(offs[-1] + s)

    def take(i, shape):
        n = 1
        for d in shape:
            n *= d
        return red[offs[i]:offs[i] + n // 128].reshape(shape)

    loss = red[0, 0]
    G_n1w = take(1, (1, D))
    G_gb = take(2, (1, KW))
    G_gnw = take(3, (1, DV))
    G_ps = take(4, (1, PW))
    G_n2w = take(5, (1, D))
    G_fw = take(6, (1, D))
    G_meta = lax.dynamic_slice(take(7, (N_META, D)), (0, me * 512), (N_META, 512))
    G_gw2 = lax.dynamic_slice(take(8, (RANK, KW)), (0, me * 128), (RANK, 128))
    G_pw = lax.dynamic_slice(take(9, (4, GC, GC)), (0, me * 64, 0), (4, 64, GC))

    d_win, nm_win, nv_win = _adam_big(w_in[0], G_win, m_w_in[0], v_w_in[0], "adam_win")
    d_wout, nm_wout, nv_wout = _adam_big(w_out[0], G_wout, m_w_out[0], v_w_out[0], "adam_wout")
    d_w1, nm_w1, nv_w1 = _adam_big(mlp_w1[0], G_w1, m_mlp_w1[0], v_mlp_w1[0], "adam_w1")
    d_w2, nm_w2, nv_w2 = _adam_big(mlp_w2[0], G_w2, m_mlp_w2[0], v_mlp_w2[0], "adam_w2")
    ws = [meta_tokens, norm1_w, gate_w2[0], gate_b, gla_norm_w, pool_w[0], pool_scale, norm2_w, fw]
    gs = [G_meta, G_n1w, G_gw2, G_gb, G_gnw, G_pw, G_ps, G_n2w, G_fw]
    ms = [m_meta_tokens, m_norm1_w, m_gate_w2[0], m_gate_b, m_gla_norm_w, m_pool_w[0], m_pool_scale, m_norm2_w,
          m_final_norm_w.reshape(1, D)]
    vs = [v_meta_tokens, v_norm1_w, v_gate_w2[0], v_gate_b, v_gla_norm_w, v_pool_w[0], v_pool_scale, v_norm2_w,
          v_final_norm_w.reshape(1, D)]
    ds, nms, nvs = _adam_small(ws, gs, ms, vs)

    def assemble(small, win_, wout_, w1_, w2_):
        meta_, n1_, gw2_, gb_, gnw_, pw_, ps_, n2_, fw_ = small
        return (meta_, n1_, win_[None], gw2_[None], gb_, gnw_, pw_[None], ps_, wout_[None], n2_, w1_[None], w2_[None],
                fw_.reshape(D))

    grads_out = assemble(gs, G_win, G_wout, G_w1, G_w2)
    deltas = assemble(ds, d_win, d_wout, d_w1, d_w2)
    new_m = assemble(nms, nm_win, nm_wout, nm_w1, nm_w2)
    new_v = assemble(nvs, nv_win, nv_wout, nv_w1, nv_w2)
    return (loss, grad_x[None], *grads_out, *deltas, *new_m, *new_v)
```

```python
import functools

import jax
import jax.numpy as jnp
from jax import lax
from jax.experimental import pallas as pl
from jax.experimental.pallas import tpu as pltpu

F32 = jnp.float32
BF16 = jnp.bfloat16

D = 2048
SEQ = 2048
N_META = 16
CH = 64
TP = 2176
NCH = TP // CH
ROW_LO = 48
X_LO = 64
ROW_HI = 2112
HEADS = 4
DK = 128
DV = 256
KW = HEADS * DK
GW = HEADS * DV
RANK = 16
TAU = 16.0
WINDOWS = (2, 4, 8, 16)
PW = 1024
GC = 256
DFF = 8192
EPS = 1e-6
SHARD_IN = 1028
PAD_IN = 1152
N_CHIP = 4

LR = 0.001
B1 = 0.9
B2 = 0.999
AEPS = 1e-08
WD = 0.01
STEP = 10

VMEM_LIMIT = 60 * 1024 * 1024
ANY = pl.BlockSpec(memory_space=pl.ANY)
VMEM_FULL = pl.BlockSpec(memory_space=pltpu.VMEM)
MESH = pl.DeviceIdType.MESH


def _cp(sem=None):
    if sem is None:
        return pltpu.CompilerParams(vmem_limit_bytes=VMEM_LIMIT)
    return pltpu.CompilerParams(dimension_semantics=sem, vmem_limit_bytes=VMEM_LIMIT)


def _dot(a, b):
    return jnp.dot(a, b, preferred_element_type=F32)


def _dot_nt(a, b):
    return lax.dot_general(a, b, (((1,), (1,)), ((), ())), preferred_element_type=F32)


def _dot_tn(a, b):
    return lax.dot_general(a, b, (((0,), (0,)), ((), ())), preferred_element_type=F32)


def _sds(shape, dtype):
    return jax.ShapeDtypeStruct(shape, dtype)


def _embed_norm(x, meta_full, w):
    def body(x_ref, meta_ref, w_ref, h_ref, u_ref):
        i = pl.program_id(0)

        @pl.when(i == 0)
        def _():
            h_ref[...] = jnp.zeros_like(h_ref)
            h_ref[ROW_LO:X_LO, :] = meta_ref[...]

        @pl.when((i >= 1) & (i <= 32))
        def _():
            h_ref[...] = x_ref[...]

        @pl.when(i == 33)
        def _():
            h_ref[...] = jnp.zeros_like(h_ref)

        h = h_ref[...]
        r = lax.rsqrt(jnp.mean(h * h, axis=-1, keepdims=True) + EPS)
        u_ref[...] = ((h * r) * w_ref[...]).astype(BF16)

    return pl.pallas_call(
        body, name="embed_norm1", grid=(NCH,),
        in_specs=[pl.BlockSpec((CH, D), lambda i: (jnp.clip(i - 1, 0, 31), 0)),
                  pl.BlockSpec((N_META, D), lambda i: (0, 0)),
                  pl.BlockSpec((1, D), lambda i: (0, 0))],
        out_specs=[pl.BlockSpec((CH, D), lambda i: (i, 0)), pl.BlockSpec((CH, D), lambda i: (i, 0))],
        out_shape=[_sds((TP, D), F32), _sds((TP, D), BF16)],
        compiler_params=_cp(("arbitrary",)),
    )(x, meta_full, w)


def _norm_rows(h, w, name):
    tr = 272

    def body(h_ref, w_ref, o_ref):
        hv = h_ref[...]
        r = lax.rsqrt(jnp.mean(hv * hv, axis=-1, keepdims=True) + EPS)
        o_ref[...] = ((hv * r) * w_ref[...]).astype(BF16)

    return pl.pallas_call(
        body, name=name, grid=(TP // tr,),
        in_specs=[pl.BlockSpec((tr, D), lambda i: (i, 0)), pl.BlockSpec((1, D), lambda i: (0, 0))],
        out_specs=pl.BlockSpec((tr, D), lambda i: (i, 0)),
        out_shape=_sds((TP, D), BF16),
        compiler_params=_cp(("arbitrary",)),
    )(h, w)


def _loss_head(h2, target, fw):
    def body(h_ref, t_ref, w_ref, dh_ref, dhb_ref, dw_ref, loss_ref):
        i = pl.program_id(0)

        @pl.when(i == 0)
        def _():
            dw_ref[...] = jnp.zeros_like(dw_ref)
            loss_ref[...] = jnp.zeros_like(loss_ref)

        h = h_ref[...]
        w = w_ref[...]
        r = lax.rsqrt(jnp.mean(h * h, axis=-1, keepdims=True) + EPS)
        xh = h * r
        y = xh * w
        is_x = ((i >= 1) & (i <= 32)).astype(F32)
        diff = (y - t_ref[...]) * is_x
        loss_ref[...] += jnp.sum(diff * diff) * (0.5 / D)
        dy = diff * (1.0 / D)
        dw_ref[...] += jnp.sum(dy * xh, axis=0, keepdims=True)
        gx = dy * w
        dh = r * (gx - xh * jnp.mean(gx * xh, axis=-1, keepdims=True))
        dh_ref[...] = dh
        dhb_ref[...] = dh.astype(BF16)

    return pl.pallas_call(
        body, name="loss_head", grid=(NCH,),
        in_specs=[pl.BlockSpec((CH, D), lambda i: (i, 0)),
                  pl.BlockSpec((CH, D), lambda i: (jnp.clip(i - 1, 0, 31), 0)),
                  pl.BlockSpec((1, D), lambda i: (0, 0))],
        out_specs=[pl.BlockSpec((CH, D), lambda i: (i, 0)), pl.BlockSpec((CH, D), lambda i: (i, 0)),
                   pl.BlockSpec((1, D), lambda i: (0, 0)), pl.BlockSpec((8, 128), lambda i: (0, 0))],
        out_shape=[_sds((TP, D), F32), _sds((TP, D), BF16), _sds((1, D), F32), _sds((8, 128), F32)],
        compiler_params=_cp(("arbitrary",)),
    )(h2, target, fw)


def _norm_bwd(dn, h, dres, w, name):
    tr = 272

    def body(dn_ref, h_ref, dres_ref, w_ref, o_ref, ob_ref, dw_ref):
        @pl.when(pl.program_id(0) == 0)
        def _():
            dw_ref[...] = jnp.zeros_like(dw_ref)

        hv = h_ref[...]
        dnv = dn_ref[...]
        r = lax.rsqrt(jnp.mean(hv * hv, axis=-1, keepdims=True) + EPS)
        xh = hv * r
        dw_ref[...] += jnp.sum(dnv * xh, axis=0, keepdims=True)
        gx = dnv * w_ref[...]
        dh = dres_ref[...] + r * (gx - xh * jnp.mean(gx * xh, axis=-1, keepdims=True))
        o_ref[...] = dh
        ob_ref[...] = dh.astype(BF16)

    row = pl.BlockSpec((tr, D), lambda i: (i, 0))
    vec = pl.BlockSpec((1, D), lambda i: (0, 0))
    return pl.pallas_call(
        body, name=name, grid=(TP // tr,),
        in_specs=[row, row, row, vec], out_specs=[row, row, vec],
        out_shape=[_sds((TP, D), F32), _sds((TP, D), BF16), _sds((1, D), F32)],
        compiler_params=_cp(("arbitrary",)),
    )(dn, h, dres, w)


def _input_grad(du, h0, dh1, w):
    def body(du_ref, h_ref, dres_ref, w_ref, gx_ref, gm_ref, dw_ref):
        i = pl.program_id(0)

        @pl.when(i == 0)
        def _():
            dw_ref[...] = jnp.zeros_like(dw_ref)

        hv = h_ref[...]
        dnv = du_ref[...]
        r = lax.rsqrt(jnp.mean(hv * hv, axis=-1, keepdims=True) + EPS)
        xh = hv * r
        dw_ref[...] += jnp.sum(dnv * xh, axis=0, keepdims=True)
        g = dnv * w_ref[...]
        dh = dres_ref[...] + r * (g - xh * jnp.mean(g * xh, axis=-1, keepdims=True))

        @pl.when(i == 0)
        def _():
            gm_ref[...] = dh[ROW_LO:X_LO, :]

        @pl.when((i >= 1) & (i <= 32))
        def _():
            gx_ref[...] = dh

    row = pl.BlockSpec((CH, D), lambda i: (i, 0))
    vec = pl.BlockSpec((1, D), lambda i: (0, 0))
    return pl.pallas_call(
        body, name="input_grad", grid=(NCH,),
        in_specs=[row, row, row, vec],
        out_specs=[pl.BlockSpec((CH, D), lambda i: (jnp.clip(i - 1, 0, 31), 0)),
                   pl.BlockSpec((N_META, D), lambda i: (0, 0)), vec],
        out_shape=[_sds((SEQ, D), F32), _sds((N_META, D), F32), _sds((1, D), F32)],
        compiler_params=_cp(("arbitrary",)),
    )(du, h0, dh1, w)


def _in_proj(u, wg):
    def body(u_ref, w_ref, o_ref):
        o_ref[0] = _dot(u_ref[...], w_ref[0])

    return pl.pallas_call(
        body, name="in_proj", grid=(N_CHIP,),
        in_specs=[VMEM_FULL, pl.BlockSpec((1, D, PAD_IN), lambda k: (k, 0, 0))],
        out_specs=pl.BlockSpec((1, TP, PAD_IN), lambda k: (k, 0, 0)),
        out_shape=_sds((N_CHIP, TP, PAD_IN), F32),
        compiler_params=_cp(("arbitrary",)),
    )(u, wg)


def _out_proj(og, op, wout, h0):
    tn = 512

    def body(og_ref, op_ref, w_ref, h_ref, o_ref):
        acc = _dot(og_ref[...], w_ref[0:GW, :]) + _dot(op_ref[...], w_ref[GW:D, :])
        o_ref[...] = h_ref[...] + acc

    return pl.pallas_call(
        body, name="out_proj", grid=(D // tn,),
        in_specs=[VMEM_FULL, VMEM_FULL, pl.BlockSpec((D, tn), lambda j: (0, j)),
                  pl.BlockSpec((TP, tn), lambda j: (0, j))],
        out_specs=pl.BlockSpec((TP, tn), lambda j: (0, j)),
        out_shape=_sds((TP, D), F32),
        compiler_params=_cp(("arbitrary",)),
    )(og, op, wout, h0)


def _mlp_up(n2, w1g):
    tn = 512
    per = D // tn

    def body(n_ref, w_ref, zr_ref, a_ref):
        z = jnp.maximum(_dot(n_ref[...], w_ref[0]), 0.0)
        zr_ref[...] = z.astype(BF16)
        a_ref[...] = (z * z).astype(BF16)

    col = pl.BlockSpec((TP, tn), lambda k, j: (0, k * per + j))
    return pl.pallas_call(
        body, name="mlp_up", grid=(N_CHIP, per),
        in_specs=[VMEM_FULL, pl.BlockSpec((1, D, tn), lambda k, j: (k, 0, j))],
        out_specs=[col, col],
        out_shape=[_sds((TP, DFF), BF16), _sds((TP, DFF), BF16)],
        compiler_params=_cp(("arbitrary", "arbitrary")),
    )(n2, w1g)


def _mlp_down(a, w2, h1):
    tk = 512
    nk = DFF // tk

    def body(a_ref, w_ref, h_ref, o_ref, acc_ref):
        k = pl.program_id(0)

        @pl.when(k == 0)
        def _():
            pltpu.sync_copy(h_ref, acc_ref)

        acc_ref[...] += _dot(a_ref[...], w_ref[...])

        @pl.when(k == nk - 1)
        def _():
            pltpu.sync_copy(acc_ref, o_ref)

    return pl.pallas_call(
        body, name="mlp_down", grid=(nk,),
        in_specs=[pl.BlockSpec((TP, tk), lambda k: (0, k)), pl.BlockSpec((tk, D), lambda k: (k, 0)), ANY],
        out_specs=ANY,
        out_shape=_sds((TP, D), F32),
        scratch_shapes=[pltpu.VMEM((TP, D), F32)],
        compiler_params=_cp(("arbitrary",)),
    )(a, w2, h1)


def _mlp_dz(dh2b, w2, zr):
    tn = 512

    def body(d_ref, w_ref, z_ref, o_ref):
        da = _dot_nt(d_ref[...], w_ref[...])
        o_ref[...] = (da * (2.0 * z_ref[...].astype(F32))).astype(BF16)

    col = pl.BlockSpec((TP, tn), lambda j: (0, j))
    return pl.pallas_call(
        body, name="mlp_dz", grid=(DFF // tn,),
        in_specs=[VMEM_FULL, pl.BlockSpec((tn, D), lambda j: (j, 0)), col],
        out_specs=col,
        out_shape=_sds((TP, DFF), BF16),
        compiler_params=_cp(("arbitrary",)),
    )(dh2b, w2, zr)


def _grad_w2(a, dh2b):
    tm = 512

    def body(a_ref, d_ref, o_ref):
        o_ref[...] = _dot_tn(a_ref[...], d_ref[...])

    return pl.pallas_call(
        body, name="grad_w2", grid=(DFF // tm,),
        in_specs=[pl.BlockSpec((TP, tm), lambda j: (0, j)), VMEM_FULL],
        out_specs=pl.BlockSpec((tm, D), lambda j: (j, 0)),
        out_shape=_sds((DFF, D), F32),
        compiler_params=_cp(("arbitrary",)),
    )(a, dh2b)


def _grad_w1(n2, dz):
    tn = 512
    per = D // tn

    def body(n_ref, d_ref, o_ref):
        o_ref[0] = _dot_tn(n_ref[...], d_ref[...])

    return pl.pallas_call(
        body, name="grad_w1", grid=(N_CHIP, per),
        in_specs=[VMEM_FULL, pl.BlockSpec((TP, tn), lambda k, j: (0, k * per + j))],
        out_specs=pl.BlockSpec((1, D, tn), lambda k, j: (k, 0, j)),
        out_shape=_sds((N_CHIP, D, D), F32),
        compiler_params=_cp(("arbitrary", "arbitrary")),
    )(n2, dz)


def _mlp_dn(dz, w1g):
    tk = 512
    per = D // tk
    nk = DFF // tk

    def body(d_ref, w_ref, o_ref, acc_ref):
        k = pl.program_id(0)
        part = _dot_nt(d_ref[...], w_ref[0])

        @pl.when(k == 0)
        def _():
            acc_ref[...] = part

        @pl.when(k > 0)
        def _():
            acc_ref[...] += part

        @pl.when(k == nk - 1)
        def _():
            pltpu.sync_copy(acc_ref, o_ref)

    return pl.pallas_call(
        body, name="mlp_dn", grid=(nk,),
        in_specs=[pl.BlockSpec((TP, tk), lambda k: (0, k)),
                  pl.BlockSpec((1, D, tk), lambda k: (k // per, 0, k % per))],
        out_specs=ANY,
        out_shape=_sds((TP, D), F32),
        scratch_shapes=[pltpu.VMEM((TP, D), F32)],
        compiler_params=_cp(("arbitrary",)),
    )(dz, w1g)


def _mixed_grad(dh1b, wout):
    tn = 512

    def body(d_ref, w_ref, o_ref):
        o_ref[...] = _dot_nt(d_ref[...], w_ref[...])

    return pl.pallas_call(
        body, name="mixed_grad", grid=(D // tn,),
        in_specs=[VMEM_FULL, pl.BlockSpec((tn, D), lambda j: (j, 0))],
        out_specs=pl.BlockSpec((TP, tn), lambda j: (0, j)),
        out_shape=_sds((TP, D), F32),
        compiler_params=_cp(("arbitrary",)),
    )(dh1b, wout)


def _grad_wout(og, op, dh1b):
    tm = 512

    def body(og_ref, op_ref, d_ref, o_ref):
        j = pl.program_id(0)

        @pl.when(j < 2)
        def _():
            o_ref[0] = _dot_tn(og_ref[...], d_ref[...])

        @pl.when(j >= 2)
        def _():
            o_ref[0] = _dot_tn(op_ref[...], d_ref[...])

    return pl.pallas_call(
        body, name="grad_wout", grid=(N_CHIP,),
        in_specs=[pl.BlockSpec((TP, tm), lambda j: (0, jnp.minimum(j, 1))),
                  pl.BlockSpec((TP, tm), lambda j: (0, jnp.maximum(j - 2, 0))), VMEM_FULL],
        out_specs=pl.BlockSpec((1, tm, D), lambda j: (j, 0, 0)),
        out_shape=_sds((N_CHIP, tm, D), F32),
        compiler_params=_cp(("arbitrary",)),
    )(og, op, dh1b)


def _in_grad(dp, wg):
    def body(d_ref, w_ref, o_ref, acc_ref):
        k = pl.program_id(0)
        part = _dot_nt(d_ref[0], w_ref[0])

        @pl.when(k == 0)
        def _():
            acc_ref[...] = part

        @pl.when(k > 0)
        def _():
            acc_ref[...] += part

        @pl.when(k == N_CHIP - 1)
        def _():
            pltpu.sync_copy(acc_ref, o_ref)

    return pl.pallas_call(
        body, name="in_grad", grid=(N_CHIP,),
        in_specs=[pl.BlockSpec((1, TP, PAD_IN), lambda k: (k, 0, 0)),
                  pl.BlockSpec((1, D, PAD_IN), lambda k: (k, 0, 0))],
        out_specs=ANY,
        out_shape=_sds((TP, D), F32),
        scratch_shapes=[pltpu.VMEM((TP, D), F32)],
        compiler_params=_cp(("arbitrary",)),
    )(dp, wg)


def _grad_win(u, dp):
    def body(u_ref, d_ref, o_ref):
        g = _dot_tn(u_ref[...], d_ref[0])
        o_ref[0] = g[:, 0:SHARD_IN]

    return pl.pallas_call(
        body, name="grad_win", grid=(N_CHIP,),
        in_specs=[VMEM_FULL, pl.BlockSpec((1, TP, PAD_IN), lambda k: (k, 0, 0))],
        out_specs=pl.BlockSpec((1, D, SHARD_IN), lambda k: (k, 0, 0)),
        out_shape=_sds((N_CHIP, D, SHARD_IN), F32),
        compiler_params=_cp(("arbitrary",)),
    )(u, dp)


def _split3(x):
    hi = x.astype(BF16)
    r1 = x - hi.astype(F32)
    mid = r1.astype(BF16)
    lo = (r1 - mid.astype(F32)).astype(BF16)
    return hi, mid, lo


def _tri_sum(tri, x):
    hi, mid, lo = _split3(x)
    return _dot(tri, hi) + _dot(tri, mid) + _dot(tri, lo)


def _gla_common(n, glr, gw2, gb):
    rows = n * CH + lax.broadcasted_iota(jnp.int32, (CH, 1), 0)
    valid = (rows >= ROW_LO) & (rows < ROW_HI)
    g_raw = _dot(glr.astype(BF16), gw2.astype(BF16)) + gb
    logsig = jnp.minimum(g_raw, 0.0) - jnp.log(1.0 + jnp.exp(-jnp.abs(g_raw)))
    logg = jnp.where(valid, logsig * (1.0 / TAU), 0.0)
    ci = lax.broadcasted_iota(jnp.int32, (CH, CH), 0)
    si = lax.broadcasted_iota(jnp.int32, (CH, CH), 1)
    lower = ci >= si
    G = _tri_sum(lower.astype(BF16), logg)
    Gl = G[CH - 1:CH, :]
    return valid, g_raw, lower, G, Gl


def _gla_fwd(q, k, v, r, glr, gw2, gb, gnw):
    scale = DK ** -0.5

    def body(q_ref, k_ref, v_ref, r_ref, glr_ref, gw2_ref, gb_ref, gnw_ref, o_ref, og_ref, sp_ref, st_ref):
        n = pl.program_id(0)

        @pl.when(n == 0)
        def _():
            st_ref[...] = jnp.zeros_like(st_ref)

        _, _, lower, G, Gl = _gla_common(n, glr_ref[...], gw2_ref[...], gb_ref[...])
        eG = jnp.exp(G)
        eN = jnp.exp(-G)
        eE = jnp.exp(Gl - G)
        dec = jnp.exp(Gl)
        gnw_v = gnw_ref[...]
        for h in range(HEADS):
            ks = slice(h * DK, (h + 1) * DK)
            vs = slice(h * DV, (h + 1) * DV)
            kh = k_ref[:, ks]
            vh = v_ref[:, vs].astype(BF16)
            qd = ((q_ref[:, ks] * scale) * eG[:, ks]).astype(BF16)
            ki = (kh * eN[:, ks]).astype(BF16)
            ke = (kh * eE[:, ks]).astype(BF16)
            st = st_ref[h]
            a = jnp.where(lower, _dot_nt(qd, ki), 0.0).astype(BF16)
            o = _dot(a, vh) + _dot_nt(qd, st.astype(BF16))
            sp_ref[0, h] = st
            st_ref[h] = st * dec[:, ks] + _dot_tn(vh, ke)
            o_ref[:, vs] = o
            rs = lax.rsqrt(jnp.mean(o * o, axis=-1, keepdims=True) + EPS)
            rv = r_ref[:, vs]
            gate = rv / (1.0 + jnp.exp(-rv))
            og_ref[:, vs] = (((o * rs) * gnw_v) * gate).astype(BF16)

    rk = pl.BlockSpec((CH, KW), lambda n: (n, 0))
    rv_ = pl.BlockSpec((CH, GW), lambda n: (n, 0))

    def full(shape):
        return pl.BlockSpec(shape, lambda n: tuple(0 for _ in shape))

    return pl.pallas_call(
        body, name="gla_fwd", grid=(NCH,),
        in_specs=[rk, rk, rv_, rv_, pl.BlockSpec((CH, RANK), lambda n: (n, 0)),
                  full((RANK, KW)), full((1, KW)), full((1, DV))],
        out_specs=[rv_, rv_, pl.BlockSpec((1, HEADS, DV, DK), lambda n: (n, 0, 0, 0))],
        out_shape=[_sds((TP, GW), F32), _sds((TP, GW), BF16), _sds((NCH, HEADS, DV, DK), F32)],
        scratch_shapes=[pltpu.VMEM((HEADS, DV, DK), F32)],
        compiler_params=_cp(("arbitrary",)),
    )(q, k, v, r, glr, gw2, gb, gnw)


def _gla_bwd(dog, o, q, k, v, r, glr, gw2, gb, gnw, sp):
    scale = DK ** -0.5

    def body(dog_ref, o_ref, q_ref, k_ref, v_ref, r_ref, glr_ref, gw2_ref, gb_ref, gnw_ref, sp_ref,
             dq_ref, dk_ref, dv_ref, dr_ref, dglr_ref, dgw2_ref, dgb_ref, dgnw_ref, ds_ref):
        step = pl.program_id(0)
        n = NCH - 1 - step

        @pl.when(step == 0)
        def _():
            ds_ref[...] = jnp.zeros_like(ds_ref)
            dgw2_ref[...] = jnp.zeros_like(dgw2_ref)
            dgb_ref[...] = jnp.zeros_like(dgb_ref)
            dgnw_ref[...] = jnp.zeros_like(dgnw_ref)

        glr_v = glr_ref[...]
        gw2_b = gw2_ref[...].astype(BF16)
        valid, g_raw, lower, G, Gl = _gla_common(n, glr_v, gw2_ref[...], gb_ref[...])
        upper = lax.broadcasted_iota(jnp.int32, (CH, CH), 0) <= lax.broadcasted_iota(jnp.int32, (CH, CH), 1)
        eG = jnp.exp(G)
        eN = jnp.exp(-G)
        eE = jnp.exp(Gl - G)
        dec = jnp.exp(Gl)
        gnw_v = gnw_ref[...]
        last = lax.broadcasted_iota(jnp.int32, (CH, 1), 0) == CH - 1
        dgnw_acc = jnp.zeros((1, DV), F32)
        dG_parts = []
        for h in range(HEADS):
            ks = slice(h * DK, (h + 1) * DK)
            vs = slice(h * DV, (h + 1) * DV)
            oh = o_ref[:, vs]
            rv = r_ref[:, vs]
            dg = dog_ref[:, vs]
            sig = 1.0 / (1.0 + jnp.exp(-rv))
            gate = rv * sig
            rs = lax.rsqrt(jnp.mean(oh * oh, axis=-1, keepdims=True) + EPS)
            ohat = oh * rs
            dr_ref[:, vs] = (dg * (ohat * gnw_v)) * (sig * (1.0 + rv * (1.0 - sig)))
            don = dg * gate
            dgnw_acc = dgnw_acc + jnp.sum(don * ohat, axis=0, keepdims=True)
            gxn = don * gnw_v
            do = (rs * (gxn - ohat * jnp.mean(gxn * ohat, axis=-1, keepdims=True))).astype(BF16)
            kh = k_ref[:, ks]
            vh = v_ref[:, vs].astype(BF16)
            qd_f = (q_ref[:, ks] * scale) * eG[:, ks]
            ki_f = kh * eN[:, ks]
            ke_f = kh * eE[:, ks]
            qd, ki, ke = qd_f.astype(BF16), ki_f.astype(BF16), ke_f.astype(BF16)
            spt = sp_ref[0, h]
            dst = ds_ref[h]
            dst_b = dst.astype(BF16)
            a_t = jnp.where(upper, _dot_nt(ki, qd), 0.0).astype(BF16)
            da = jnp.where(lower, _dot_nt(do, vh), 0.0).astype(BF16)
            da_t = jnp.where(upper, _dot_nt(vh, do), 0.0).astype(BF16)
            dv_ref[:, vs] = _dot(a_t, do) + _dot_nt(ke, dst_b)
            dqd = _dot(da, ki) + _dot(do, spt.astype(BF16))
            dki = _dot(da_t, qd)
            dke = _dot(vh, dst_b)
            ddec = jnp.sum(spt * dst, axis=0, keepdims=True)
            ds_ref[h] = dst * dec[:, ks] + _dot_tn(do, qd)
            dq_ref[:, ks] = (dqd * eG[:, ks]) * scale
            dk_ref[:, ks] = dki * eN[:, ks] + dke * eE[:, ks]
            dke_ke = dke * ke_f
            dG = dqd * qd_f - dki * ki_f - dke_ke
            dGl = jnp.sum(dke_ke, axis=0, keepdims=True) + ddec * dec[:, ks]
            dG_parts.append(dG + jnp.where(last, dGl, 0.0))
        dgnw_ref[...] += dgnw_acc
        dG_all = jnp.concatenate(dG_parts, axis=1)
        dlogg = jnp.where(valid, _tri_sum(upper.astype(BF16), dG_all), 0.0)
        dg_raw = (dlogg * (1.0 / TAU)) * (1.0 / (1.0 + jnp.exp(g_raw)))
        dgb_ref[...] += jnp.sum(dg_raw, axis=0, keepdims=True)
        dg_b = dg_raw.astype(BF16)
        dgw2_ref[...] += _dot_tn(glr_v.astype(BF16), dg_b)
        dglr_ref[...] = _dot_nt(dg_b, gw2_b)

    rk = pl.BlockSpec((CH, KW), lambda s: (NCH - 1 - s, 0))
    rv_ = pl.BlockSpec((CH, GW), lambda s: (NCH - 1 - s, 0))
    rg = pl.BlockSpec((CH, RANK), lambda s: (NCH - 1 - s, 0))

    def full(shape):
        return pl.BlockSpec(shape, lambda s: tuple(0 for _ in shape))

    return pl.pallas_call(
        body, name="gla_bwd", grid=(NCH,),
        in_specs=[rv_, rv_, rk, rk, rv_, rv_, rg, full((RANK, KW)), full((1, KW)), full((1, DV)),
                  pl.BlockSpec((1, HEADS, DV, DK), lambda s: (NCH - 1 - s, 0, 0, 0))],
        out_specs=[rk, rk, rv_, rv_, rg, full((RANK, KW)), full((1, KW)), full((1, DV))],
        out_shape=[_sds((TP, KW), F32), _sds((TP, KW), F32), _sds((TP, GW), F32), _sds((TP, GW), F32),
                   _sds((TP, RANK), F32), _sds((RANK, KW), F32), _sds((1, KW), F32), _sds((1, DV), F32)],
        scratch_shapes=[pltpu.VMEM((HEADS, DV, DK), F32)],
        compiler_params=_cp(("arbitrary",)),
    )(dog, o, q, k, v, r, glr, gw2, gb, gnw, sp)


POOL_TR = 128
HALO = 16


def _pool_counts(base, nrows):
    rows = base + lax.broadcasted_iota(jnp.int32, (nrows, 1), 0)
    valid = (rows >= ROW_LO) & (rows < ROW_HI)
    t1 = (rows - ROW_LO + 1).astype(F32)
    cnts = [jnp.clip(t1, 1.0, float(w)) for w in WINDOWS]
    return valid, cnts


def _pool_fwd(pu, pw, ps):
    def body(cur_ref, prev_ref, pw_ref, ps_ref, y_ref, op_ref):
        i = pl.program_id(0)
        cur = cur_ref[...]
        full = jnp.concatenate([prev_ref[...], cur], axis=0)
        s2 = full + pltpu.roll(full, 1, 0)
        s4 = s2 + pltpu.roll(s2, 2, 0)
        s8 = s4 + pltpu.roll(s4, 4, 0)
        s16 = s8 + pltpu.roll(s8, 8, 0)
        valid, cnts = _pool_counts(i * POOL_TR, POOL_TR)
        for g, s in enumerate((s2, s4, s8, s16)):
            cs = slice(g * GC, (g + 1) * GC)
            y = s[HALO:, cs] / cnts[g] - cur[:, cs]
            yb = jnp.where(valid, y, 0.0).astype(BF16)
            y_ref[:, cs] = yb
            op_ref[:, cs] = (_dot(yb, pw_ref[g].astype(BF16)) * ps_ref[:, cs]).astype(BF16)

    row = pl.BlockSpec((POOL_TR, PW), lambda i: (i, 0))
    per = POOL_TR // HALO
    return pl.pallas_call(
        body, name="pool_fwd", grid=(TP // POOL_TR,),
        in_specs=[row, pl.BlockSpec((HALO, PW), lambda i: (jnp.maximum(i * per - 1, 0), 0)),
                  pl.BlockSpec((4, GC, GC), lambda i: (0, 0, 0)), pl.BlockSpec((1, PW), lambda i: (0, 0))],
        out_specs=[row, row],
        out_shape=[_sds((TP, PW), BF16), _sds((TP, PW), BF16)],
        compiler_params=_cp(("arbitrary",)),
    )(pu, pu, pw, ps)


def _pool_bwd(dop, y, pw, ps):
    nblk = TP // HALO

    def body(cur_ref, nxt_ref, y_ref, pw_ref, ps_ref, dpu_ref, dpw_ref, dps_ref):
        i = pl.program_id(0)

        @pl.when(i == 0)
        def _():
            dpw_ref[...] = jnp.zeros_like(dpw_ref)
            dps_ref[...] = jnp.zeros_like(dps_ref)

        n_all = POOL_TR + HALO
        dcur = cur_ref[...]
        dall = jnp.concatenate([dcur, nxt_ref[...]], axis=0)
        valid, cnts = _pool_counts(i * POOL_TR, n_all)
        for g in range(4):
            cs = slice(g * GC, (g + 1) * GC)
            pwb = pw_ref[g].astype(BF16)
            yb = y_ref[:, cs]
            dyw = (dall[:, cs] * ps_ref[:, cs]).astype(BF16)
            dps_ref[:, cs] += jnp.sum(dcur[:, cs] * _dot(yb, pwb), axis=0, keepdims=True)
            dpw_ref[g] += _dot_tn(yb, dyw[0:POOL_TR, :])
            dyv = jnp.where(valid, _dot_nt(dyw, pwb), 0.0)
            e = dyv / cnts[g]
            w = WINDOWS[g]
            sh = 1
            while sh < w:
                e = e + pltpu.roll(e, n_all - sh, 0)
                sh *= 2
            dpu_ref[:, cs] = e[0:POOL_TR, :] - dyv[0:POOL_TR, :]

    row = pl.BlockSpec((POOL_TR, PW), lambda i: (i, 0))
    per = POOL_TR // HALO
    return pl.pallas_call(
        body, name="pool_bwd", grid=(TP // POOL_TR,),
        in_specs=[pl.BlockSpec((POOL_TR, PW), lambda i: (i, 1)),
                  pl.BlockSpec((HALO, PW), lambda i: (jnp.minimum(i * per + per, nblk - 1), 1)),
                  row, pl.BlockSpec((4, GC, GC), lambda i: (0, 0, 0)), pl.BlockSpec((1, PW), lambda i: (0, 0))],
        out_specs=[row, pl.BlockSpec((4, GC, GC), lambda i: (0, 0, 0)), pl.BlockSpec((1, PW), lambda i: (0, 0))],
        out_shape=[_sds((TP, PW), F32), _sds((4, GC, GC), F32), _sds((1, PW), F32)],
        compiler_params=_cp(("arbitrary",)),
    )(dop, dop, y, pw, ps)


def _place():
    x, y, c = lax.axis_index("x"), lax.axis_index("y"), lax.axis_index("c")
    chips = [(1 - x, y), (x, 1 - y), (1 - x, 1 - y)]
    return x, y, c, chips


HBM = pl.BlockSpec(memory_space=pltpu.HBM)
SEM = pl.BlockSpec(memory_space=pltpu.SEMAPHORE)
EFFECT = pltpu.SideEffectType.DATAFLOW_SIDE_EFFECTING


def _cast_into(w, place, cols_out, name):
    rows, cols = w.shape
    tr = 256

    def body(p_ref, w_ref, o_ref):
        if cols_out != cols:
            o_ref[0] = jnp.zeros((tr, cols_out), BF16)
            o_ref[0, :, 0:cols] = w_ref[...].astype(BF16)
        else:
            o_ref[0] = w_ref[...].astype(BF16)

    grid_spec = pltpu.PrefetchScalarGridSpec(
        num_scalar_prefetch=1, grid=(rows // tr,),
        in_specs=[pl.BlockSpec((tr, cols), lambda i, p: (i, 0))],
        out_specs=pl.BlockSpec((1, tr, cols_out), lambda i, p: (p[0], i, 0)))
    return pl.pallas_call(
        body, name=name, grid_spec=grid_spec,
        out_shape=_sds((N_CHIP, rows, cols_out), BF16),
        compiler_params=_cp(("arbitrary",)),
    )(place, w)


def _half_rows(ref, k, which):
    h = ref.shape[1] // 2
    return ref.at[k, pl.ds(pl.multiple_of(which * h, 8), h), :]


def _gather_small(meta, gw2, pw):
    def body(meta_r, gw2_r, pw_r, metaF, gw2F, pwF, lsem, ssem, rsem):
        x, y, c, chips = _place()
        me = 2 * x + y

        def slots(k):
            return (metaF.at[:, pl.ds(pl.multiple_of(k * 512, 128), 512)],
                    gw2F.at[:, pl.ds(pl.multiple_of(k * 128, 128), 128)],
                    pwF.at[:, pl.ds(pl.multiple_of(k * 64, 8), 64), :])

        srcs = (meta_r, gw2_r, pw_r)
        local = [pltpu.make_async_copy(s, d, lsem.at[i]) for i, (s, d) in enumerate(zip(srcs, slots(me)))]
        sends = []
        for j, chip in enumerate(chips):
            for i, (s, d) in enumerate(zip(srcs, slots(me))):
                sends.append(pltpu.make_async_remote_copy(src_ref=s, dst_ref=d, send_sem=ssem.at[3 * j + i],
                                                          recv_sem=rsem.at[3 * j + i], device_id=(*chip, c),
                                                          device_id_type=MESH))
        for cp in local + sends:
            cp.start()
        for j, (cx, cy) in enumerate(chips):
            for i, (s, d) in enumerate(zip(srcs, slots(2 * cx + cy))):
                pltpu.make_async_remote_copy(src_ref=s, dst_ref=d, send_sem=ssem.at[3 * j + i], recv_sem=rsem.at[3 * j + i],
                                             device_id=(cx, cy, c), device_id_type=MESH).wait_recv()
        for cp in sends:
            cp.wait_send()
        for cp in local:
            cp.wait()

    return pl.pallas_call(
        body, name="gather_small",
        in_specs=[ANY] * 3, out_specs=[ANY] * 3,
        out_shape=[_sds((N_META, D), F32), _sds((RANK, KW), F32), _sds((4, GC, GC), F32)],
        scratch_shapes=[pltpu.SemaphoreType.DMA((3,)), pltpu.SemaphoreType.DMA((9,)), pltpu.SemaphoreType.DMA((9,))],
    )(meta, gw2, pw)


def _gather_start(ws):
    n = len(ws)

    def body(*refs):
        ins = refs[:n]
        ssems = refs[n:2 * n]
        rsems = refs[2 * n:3 * n]
        x, y, c, chips = _place()
        me = 2 * x + y
        for w in range(n):
            blk = _half_rows(ins[w], me, c)
            for j, chip in enumerate(chips):
                pltpu.make_async_remote_copy(src_ref=blk, dst_ref=blk, send_sem=ssems[w].at[j], recv_sem=rsems[w].at[j],
                                             device_id=(*chip, c), device_id_type=MESH).start()

    sem3 = pltpu.SemaphoreType.DMA((3,))
    outs = pl.pallas_call(
        body, name="gather_start",
        out_shape=tuple([sem3] * (2 * n) + [pltpu.HBM(w.shape, w.dtype) for w in ws]),
        in_specs=(HBM,) * n, out_specs=(SEM,) * (2 * n) + (HBM,) * n,
        input_output_aliases={w: 2 * n + w for w in range(n)},
        compiler_params=pltpu.CompilerParams(has_side_effects=EFFECT),
    )(*[pltpu.with_memory_space_constraint(w, pltpu.HBM) for w in ws])
    return outs[:n], outs[n:2 * n], outs[2 * n:]


def _gather_wait(w, ssem, rsem, after, name):
    def body(w_ref, ssem_ref, rsem_ref, after_ref, out_ref):
        x, y, c, chips = _place()
        me = 2 * x + y
        mine = _half_rows(w_ref, me, c)
        for j, (cx, cy) in enumerate(chips):
            cp = pltpu.make_async_remote_copy(src_ref=mine, dst_ref=_half_rows(w_ref, 2 * cx + cy, c),
                                              send_sem=ssem_ref.at[j], recv_sem=rsem_ref.at[j],
                                              device_id=(cx, cy, c), device_id_type=MESH)
            cp.wait_send()
            cp.wait_recv()

    return pl.pallas_call(
        body, name=name, out_shape=pltpu.HBM(w.shape, w.dtype),
        in_specs=(HBM, SEM, SEM, ANY), out_specs=HBM, input_output_aliases={0: 0},
        compiler_params=pltpu.CompilerParams(has_side_effects=EFFECT),
    )(w, ssem, rsem, after)


def _forward_halves(w, name):
    def body(w_ref, o_ref, ssem, rsem):
        x, y, c, chips = _place()
        sib = (x, y, 1 - c)
        cps = []
        for j, (cx, cy) in enumerate(chips):
            blk = _half_rows(o_ref, 2 * cx + cy, c)
            cps.append(pltpu.make_async_remote_copy(src_ref=blk, dst_ref=blk, send_sem=ssem.at[j], recv_sem=rsem.at[j],
                                                    device_id=sib, device_id_type=MESH))
        for cp in cps:
            cp.start()
        for j, (cx, cy) in enumerate(chips):
            blk = _half_rows(o_ref, 2 * cx + cy, 1 - c)
            pltpu.make_async_remote_copy(src_ref=blk, dst_ref=blk, send_sem=ssem.at[j], recv_sem=rsem.at[j],
                                         device_id=sib, device_id_type=MESH).wait_recv()
        for cp in cps:
            cp.wait_send()

    return pl.pallas_call(
        body, name=name, in_specs=[ANY], out_specs=ANY, out_shape=_sds(w.shape, w.dtype),
        input_output_aliases={0: 0},
        scratch_shapes=[pltpu.SemaphoreType.DMA((3,)), pltpu.SemaphoreType.DMA((3,))],
    )(w)


def _rs_start(sb, name):
    _, half, cols = sb.shape

    def body(sb_ref, land_ref, ssem, rsem, sb_out, land_out):
        x, y, c, chips = _place()
        for j, (cx, cy) in enumerate(chips):
            pltpu.make_async_remote_copy(src_ref=sb_ref.at[2 * cx + cy], dst_ref=land_ref.at[j], send_sem=ssem.at[j],
                                         recv_sem=rsem.at[j], device_id=(cx, cy, c), device_id_type=MESH).start()

    sem3 = pltpu.SemaphoreType.DMA((3,))
    land = lax.empty((3, half, cols), BF16)
    return pl.pallas_call(
        body, name=name,
        out_shape=(sem3, sem3, pltpu.HBM(sb.shape, sb.dtype), pltpu.HBM(land.shape, land.dtype)),
        in_specs=(HBM, HBM), out_specs=(SEM, SEM, HBM, HBM), input_output_aliases={0: 2, 1: 3},
        compiler_params=pltpu.CompilerParams(has_side_effects=EFFECT),
    )(pltpu.with_memory_space_constraint(sb, pltpu.HBM), pltpu.with_memory_space_constraint(land, pltpu.HBM))


def _rs_wait(sb, land, ssem, rsem, after, name):
    def body(sb_ref, land_ref, ssem_ref, rsem_ref, after_ref, sb_out, land_out):
        x, y, c, chips = _place()
        for j, (cx, cy) in enumerate(chips):
            cp = pltpu.make_async_remote_copy(src_ref=sb_ref.at[2 * cx + cy], dst_ref=land_ref.at[j], send_sem=ssem_ref.at[j],
                                              recv_sem=rsem_ref.at[j], device_id=(cx, cy, c), device_id_type=MESH)
            cp.wait_send()
            cp.wait_recv()

    return pl.pallas_call(
        body, name=name,
        out_shape=(pltpu.HBM(sb.shape, sb.dtype), pltpu.HBM(land.shape, land.dtype)),
        in_specs=(HBM, HBM, SEM, SEM, ANY), out_specs=(HBM, HBM), input_output_aliases={0: 0, 1: 1},
        compiler_params=pltpu.CompilerParams(has_side_effects=EFFECT),
    )(sb, land, ssem, rsem, after)[1]


def _pair_exchange(grads, name):
    ng = len(grads)

    def body(*refs):
        g = refs[:ng]
        r = refs[ng:2 * ng]
        ssem, rsem = refs[2 * ng], refs[2 * ng + 1]
        x, y, c, _ = _place()
        sib = (x, y, 1 - c)
        cps = []
        for w in range(ng):
            h = g[w].shape[1] // 2
            src = g[w].at[:, pl.ds(pl.multiple_of((1 - c) * h, 8), h), :]
            cps.append(pltpu.make_async_remote_copy(src_ref=src, dst_ref=r[w], send_sem=ssem.at[w], recv_sem=rsem.at[w],
                                                    device_id=sib, device_id_type=MESH))
        for cp in cps:
            cp.start()
        for cp in cps:
            cp.wait()

    out_shape = [_sds((N_CHIP, g.shape[1] // 2, g.shape[2]), F32) for g in grads]
    return pl.pallas_call(
        body, name=name,
        in_specs=[ANY] * ng, out_specs=[ANY] * ng, out_shape=out_shape,
        scratch_shapes=[pltpu.SemaphoreType.DMA((ng,)), pltpu.SemaphoreType.DMA((ng,))],
    )(*grads)


def _pair_sum(g, rcv, place, name):
    _, rows, cols = g.shape
    half = rows // 2
    tr = 256
    nt = half // tr

    def body(p_ref, g_ref, r_ref, sb_ref, sf_ref):
        s = pl.program_id(1)
        tot = g_ref[0] + r_ref[0]
        sb_ref[0] = tot.astype(BF16)

        @pl.when(s == p_ref[0])
        def _():
            sf_ref[...] = tot

    grid_spec = pltpu.PrefetchScalarGridSpec(
        num_scalar_prefetch=1, grid=(nt, N_CHIP),
        in_specs=[pl.BlockSpec((1, tr, cols), lambda t, s, p: (s, p[1] * nt + t, 0)),
                  pl.BlockSpec((1, tr, cols), lambda t, s, p: (s, t, 0))],
        out_specs=[pl.BlockSpec((1, tr, cols), lambda t, s, p: (s, t, 0)),
                   pl.BlockSpec((tr, cols), lambda t, s, p: (t, 0))])
    return pl.pallas_call(
        body, name=name, grid_spec=grid_spec,
        out_shape=[_sds((N_CHIP, half, cols), BF16), _sds((half, cols), F32)],
        compiler_params=_cp(("arbitrary", "arbitrary")),
    )(place, g, rcv)


def _final_sum(sf, rb, place, name):
    half, cols = sf.shape
    tr = 256
    nt = half // tr

    def body(p_ref, sf_ref, r_ref, out_ref):
        acc = sf_ref[...]
        for j in range(3):
            acc = acc + r_ref[j].astype(F32)
        out_ref[...] = acc

    grid_spec = pltpu.PrefetchScalarGridSpec(
        num_scalar_prefetch=1, grid=(nt,),
        in_specs=[pl.BlockSpec((tr, cols), lambda t, p: (t, 0)), pl.BlockSpec((3, tr, cols), lambda t, p: (0, t, 0))],
        out_specs=pl.BlockSpec((tr, cols), lambda t, p: (p[1] * nt + t, 0)))
    return pl.pallas_call(
        body, name=name, grid_spec=grid_spec,
        out_shape=_sds((2 * half, cols), F32),
        compiler_params=_cp(("arbitrary",)),
    )(place, sf, rb)


def _half_exchange(fulls):
    ng = len(fulls)

    def body(*refs):
        outs = refs[ng:2 * ng]
        ssem, rsem = refs[2 * ng], refs[2 * ng + 1]
        x, y, c, _ = _place()
        sib = (x, y, 1 - c)
        cps = []
        for w in range(ng):
            h = outs[w].shape[0] // 2
            mine = outs[w].at[pl.ds(pl.multiple_of(c * h, 8), h), :]
            cps.append(pltpu.make_async_remote_copy(src_ref=mine, dst_ref=mine, send_sem=ssem.at[w], recv_sem=rsem.at[w],
                                                    device_id=sib, device_id_type=MESH))
        for cp in cps:
            cp.start()
        for w in range(ng):
            h = outs[w].shape[0] // 2
            theirs = outs[w].at[pl.ds(pl.multiple_of((1 - c) * h, 8), h), :]
            pltpu.make_async_remote_copy(src_ref=theirs, dst_ref=theirs, send_sem=ssem.at[w], recv_sem=rsem.at[w],
                                         device_id=sib, device_id_type=MESH).wait_recv()
        for cp in cps:
            cp.wait_send()

    return pl.pallas_call(
        body, name="half_exchange",
        in_specs=[ANY] * ng, out_specs=[ANY] * ng, out_shape=[_sds(f.shape, F32) for f in fulls],
        input_output_aliases={w: w for w in range(ng)},
        scratch_shapes=[pltpu.SemaphoreType.DMA((ng,)), pltpu.SemaphoreType.DMA((ng,))],
    )(*fulls)


def _small_allreduce(vec):
    nr = vec.shape[0]

    def body(v_ref, o_ref, rsib, pair, rchip, ssem, rsem):
        x, y, c, chips = _place()
        me = 2 * x + y
        sib = (x, y, 1 - c)
        first = pltpu.make_async_remote_copy(src_ref=v_ref, dst_ref=rsib, send_sem=ssem.at[0], recv_sem=rsem.at[0],
                                             device_id=sib, device_id_type=MESH)
        first.start()
        first.wait()
        pair[...] = v_ref[...] + rsib[...]
        cps = [pltpu.make_async_remote_copy(src_ref=pair, dst_ref=rchip.at[j], send_sem=ssem.at[1 + j], recv_sem=rsem.at[1 + j],
                                            device_id=(*chip, c), device_id_type=MESH) for j, chip in enumerate(chips)]
        for cp in cps:
            cp.start()
        for cp in cps:
            cp.wait()
        acc = None
        for kk in range(N_CHIP):
            d = jnp.bitwise_xor(me, kk)
            t = jnp.where(d == 0, pair[...], jnp.where(d == 2, rchip[0], jnp.where(d == 1, rchip[1], rchip[2])))
            acc = t if acc is None else acc + t
        o_ref[...] = acc

    return pl.pallas_call(
        body, name="small_allreduce",
        in_specs=[VMEM_FULL], out_specs=VMEM_FULL, out_shape=_sds((nr, 128), F32),
        scratch_shapes=[pltpu.VMEM((nr, 128), F32), pltpu.VMEM((nr, 128), F32), pltpu.VMEM((3, nr, 128), F32),
                        pltpu.SemaphoreType.DMA((4,)), pltpu.SemaphoreType.DMA((4,))],
        compiler_params=_cp(),
    )(vec)


def _adam_math(w, g, m, v):
    m = B1 * m + (1.0 - B1) * g
    v = B2 * v + (1.0 - B2) * (g * g)
    m_hat = m / (1.0 - B1 ** STEP)
    v_hat = v / (1.0 - B2 ** STEP)
    delta = -LR * (m_hat / (jnp.sqrt(v_hat) + AEPS) + WD * w)
    return delta, m, v


def _adam_big(w, g, m, v, name):
    rows, cols = w.shape
    tr = 128

    def body(w_ref, g_ref, m_ref, v_ref, go_ref, d_ref, nm_ref, nv_ref):
        g = g_ref[...]
        d, nm, nv = _adam_math(w_ref[...], g, m_ref[...], v_ref[...])
        go_ref[...] = g
        d_ref[...] = d
        nm_ref[...] = nm
        nv_ref[...] = nv

    blk = pl.BlockSpec((tr, cols), lambda i: (i, 0))
    return pl.pallas_call(
        body, name=name, grid=(rows // tr,),
        in_specs=[blk] * 4, out_specs=[blk] * 4, out_shape=[_sds((rows, cols), F32)] * 4,
        compiler_params=_cp(("arbitrary",)),
    )(w, g, m, v)


def _adam_small(ws, gs, ms, vs):
    n = len(ws)

    def body(*refs):
        for i in range(n):
            d, nm, nv = _adam_math(refs[i][...], refs[n + i][...], refs[2 * n + i][...], refs[3 * n + i][...])
            refs[4 * n + i][...] = d
            refs[5 * n + i][...] = nm
            refs[6 * n + i][...] = nv

    shapes = [_sds(w.shape, F32) for w in ws]
    outs = pl.pallas_call(
        body, name="adam_small",
        in_specs=[VMEM_FULL] * (4 * n), out_specs=[VMEM_FULL] * (3 * n), out_shape=shapes * 3,
        compiler_params=_cp(),
    )(*ws, *gs, *ms, *vs)
    return outs[:n], outs[n:2 * n], outs[2 * n:]


def _pad_rows8(a):
    flat = a.reshape(-1, 128)
    pad = (-flat.shape[0]) % 8
    if pad:
        flat = jnp.concatenate([flat, jnp.zeros((pad, 128), F32)], axis=0)
    return flat


def kernel(x, meta_tokens, norm1_w, w_in, gate_w2, gate_b, gla_norm_w, pool_w, pool_scale, w_out, norm2_w, mlp_w1, mlp_w2, final_norm_w, loss_target, m_meta_tokens, m_norm1_w, m_w_in, m_gate_w2, m_gate_b, m_gla_norm_w, m_pool_w, m_pool_scale, m_w_out, m_norm2_w, m_mlp_w1, m_mlp_w2, m_final_norm_w, v_meta_tokens, v_norm1_w, v_w_in, v_gate_w2, v_gate_b, v_gla_norm_w, v_pool_w, v_pool_scale, v_w_out, v_norm2_w, v_mlp_w1, v_mlp_w2, v_final_norm_w):
    cx, cy, cc = lax.axis_index("x"), lax.axis_index("y"), lax.axis_index("c")
    me = (2 * cx + cy).astype(jnp.int32)

    place = jnp.stack([me, cc.astype(jnp.int32)])
    fw = final_norm_w.reshape(1, D)

    metaF, gw2F, pwF = _gather_small(meta_tokens, gate_w2[0], pool_w[0])
    mine = [_cast_into(w_in[0], place, PAD_IN, "cast_win"), _cast_into(w_out[0], place, D, "cast_wout"),
            _cast_into(mlp_w1[0], place, D, "cast_w1"), _cast_into(mlp_w2[0], place, D, "cast_w2")]
    ssems, rsems, flying = _gather_start(mine)

    def arrive(i, nm, after):
        return _forward_halves(_gather_wait(flying[i], ssems[i], rsems[i], after, "gather_wait_" + nm), "forward_" + nm)

    Win = arrive(0, "win", metaF)

    pending = {}

    def on_grad(nm, g):
        rcv, = _pair_exchange([g], "pair_exchange_" + nm)
        sb, sf = _pair_sum(g, rcv, place, "pair_sum_" + nm)
        pending[nm] = (sf, _rs_start(sb, "rs_start_" + nm))

    (grad_x, loss8, d_n1w, d_gb, d_gnw, d_ps, d_n2w, d_fw, d_meta, d_gw2, d_pw) = _local_step(
        x[0], loss_target[0], Win, lambda after: arrive(1, "wout", after).reshape(D, D), lambda after: arrive(2, "w1", after),
        lambda after: arrive(3, "w2", after).reshape(DFF, D), metaF, gw2F, pwF, norm1_w, gate_b, gla_norm_w, pool_scale,
        norm2_w, fw, on_grad)
    fulls = []
    for nm in ("win", "wout", "w1", "w2"):
        sf, (ssem, rsem, sb_thru, land) = pending[nm]
        rb = _rs_wait(sb_thru, land, ssem, rsem, grad_x, "rs_wait_" + nm)
        fulls.append(_final_sum(sf, rb, place, "final_sum_" + nm))
    G_win, G_wout, G_w1, G_w2 = _half_exchange(fulls)
    return _reduce_and_update(
        me, grad_x, loss8, d_n1w, d_gb, d_gnw, d_ps, d_n2w, d_fw, d_meta, d_gw2, d_pw, G_win, G_wout, G_w1, G_w2,
        meta_tokens, norm1_w, w_in, gate_w2, gate_b, gla_norm_w, pool_w, pool_scale, w_out, norm2_w, mlp_w1, mlp_w2, fw,
        m_meta_tokens, m_norm1_w, m_w_in, m_gate_w2, m_gate_b, m_gla_norm_w, m_pool_w, m_pool_scale, m_w_out, m_norm2_w,
        m_mlp_w1, m_mlp_w2, m_final_norm_w, v_meta_tokens, v_norm1_w, v_w_in, v_gate_w2, v_gate_b, v_gla_norm_w, v_pool_w,
        v_pool_scale, v_w_out, v_norm2_w, v_mlp_w1, v_mlp_w2, v_final_norm_w)


def _local_step(x, target, Win, get_wout, get_w1, get_w2, metaF, gw2F, pwF, norm1_w, gate_b, gla_norm_w, pool_scale, norm2_w,
                fw, on_grad):
    h0, u = _embed_norm(x, metaF, norm1_w)
    P = _in_proj(u, Win)
    q = P[0, :, 0:512]
    k = P[0, :, 512:1024]
    v = jnp.concatenate([P[0, :, 1024:1028], P[1, :, 0:1020]], axis=1)
    r = jnp.concatenate([P[1, :, 1020:1028], P[2, :, 0:1016]], axis=1)
    glr = jnp.concatenate([P[2, :, 1016:1028], P[3, :, 0:4]], axis=1)
    pu = P[3, :, 4:1028]
    o, og, sp = _gla_fwd(q, k, v, r, glr, gw2F, gate_b, gla_norm_w)
    yb, op = _pool_fwd(pu, pwF, pool_scale)
    Wout = get_wout(op)
    h1 = _out_proj(og, op, Wout, h0)
    n2 = _norm_rows(h1, norm2_w, "norm2")
    W1 = get_w1(n2)
    zr, a = _mlp_up(n2, W1)
    W2 = get_w2(a)
    h2 = _mlp_down(a, W2, h1)

    dh2, dh2b, d_fw, loss8 = _loss_head(h2, target, fw)
    dz = _mlp_dz(dh2b, W2, zr)
    on_grad("w2", _grad_w2(a, dh2b).reshape(N_CHIP, D, D))
    on_grad("w1", _grad_w1(n2, dz))
    dn2 = _mlp_dn(dz, W1)
    dh1, dh1b, d_n2w = _norm_bwd(dn2, h1, dh2, norm2_w, "norm2_bwd")
    dmixed = _mixed_grad(dh1b, Wout)
    on_grad("wout", _grad_wout(og, op, dh1b))
    dpu, d_pw, d_ps = _pool_bwd(dmixed, yb, pwF, pool_scale)
    dq, dk, dv, dr, dglr, d_gw2, d_gb, d_gnw = _gla_bwd(dmixed, o, q, k, v, r, glr, gw2F, gate_b, gla_norm_w, sp)
    zpad = jnp.zeros((TP, PAD_IN - SHARD_IN), F32)
    dP = jnp.stack([
        jnp.concatenate([dq, dk, dv[:, 0:4], zpad], axis=1),
        jnp.concatenate([dv[:, 4:], dr[:, 0:8], zpad], axis=1),
        jnp.concatenate([dr[:, 8:], dglr[:, 0:12], zpad], axis=1),
        jnp.concatenate([dglr[:, 12:], dpu, zpad], axis=1)]).astype(BF16)
    on_grad("win", _grad_win(u, dP))
    du = _in_grad(dP, Win)
    grad_x, d_meta, d_n1w = _input_grad(du, h0, dh1, norm1_w)
    return grad_x, loss8, d_n1w, d_gb, d_gnw, d_ps, d_n2w, d_fw, d_meta, d_gw2, d_pw


def _reduce_and_update(me, grad_x, loss8, d_n1w, d_gb, d_gnw, d_ps, d_n2w, d_fw, d_meta, d_gw2, d_pw, G_win, G_wout, G_w1,
                       G_w2, meta_tokens, norm1_w, w_in, gate_w2, gate_b, gla_norm_w, pool_w, pool_scale, w_out, norm2_w,
                       mlp_w1, mlp_w2, fw, m_meta_tokens, m_norm1_w, m_w_in, m_gate_w2, m_gate_b, m_gla_norm_w, m_pool_w,
                       m_pool_scale, m_w_out, m_norm2_w, m_mlp_w1, m_mlp_w2, m_final_norm_w, v_meta_tokens, v_norm1_w, v_w_in,
                       v_gate_w2, v_gate_b, v_gla_norm_w, v_pool_w, v_pool_scale, v_w_out, v_norm2_w, v_mlp_w1, v_mlp_w2,
                       v_final_norm_w):
    parts = [loss8, d_n1w, d_gb, d_gnw, d_ps, d_n2w, d_fw, d_meta, d_gw2, d_pw]
    packed = [_pad_rows8(p) for p in parts]
    sizes = [p.shape[0] for p in packed]
    red = _small_allreduce(jnp.concatenate(packed, axis=0))
    offs = [0]
    for s in sizes:
        offs.append(offs[-1] + s)

    def take(i, shape):
        n = 1
        for d in shape:
            n *= d
        return red[offs[i]:offs[i] + n // 128].reshape(shape)

    loss = red[0, 0]
    G_n1w = take(1, (1, D))
    G_gb = take(2, (1, KW))
    G_gnw = take(3, (1, DV))
    G_ps = take(4, (1, PW))
    G_n2w = take(5, (1, D))
    G_fw = take(6, (1, D))
    G_meta = lax.dynamic_slice(take(7, (N_META, D)), (0, me * 512), (N_META, 512))
    G_gw2 = lax.dynamic_slice(take(8, (RANK, KW)), (0, me * 128), (RANK, 128))
    G_pw = lax.dynamic_slice(take(9, (4, GC, GC)), (0, me * 64, 0), (4, 64, GC))

    G_win, d_win, nm_win, nv_win = _adam_big(w_in[0], G_win, m_w_in[0], v_w_in[0], "adam_win")
    G_wout, d_wout, nm_wout, nv_wout = _adam_big(w_out[0], G_wout, m_w_out[0], v_w_out[0], "adam_wout")
    G_w1, d_w1, nm_w1, nv_w1 = _adam_big(mlp_w1[0], G_w1, m_mlp_w1[0], v_mlp_w1[0], "adam_w1")
    G_w2, d_w2, nm_w2, nv_w2 = _adam_big(mlp_w2[0], G_w2, m_mlp_w2[0], v_mlp_w2[0], "adam_w2")
    ws = [meta_tokens, norm1_w, gate_w2[0], gate_b, gla_norm_w, pool_w[0], pool_scale, norm2_w, fw]
    gs = [G_meta, G_n1w, G_gw2, G_gb, G_gnw, G_pw, G_ps, G_n2w, G_fw]
    ms = [m_meta_tokens, m_norm1_w, m_gate_w2[0], m_gate_b, m_gla_norm_w, m_pool_w[0], m_pool_scale, m_norm2_w,
          m_final_norm_w.reshape(1, D)]
    vs = [v_meta_tokens, v_norm1_w, v_gate_w2[0], v_gate_b, v_gla_norm_w, v_pool_w[0], v_pool_scale, v_norm2_w,
          v_final_norm_w.reshape(1, D)]
    ds, nms, nvs = _adam_small(ws, gs, ms, vs)

    def assemble(small, win_, wout_, w1_, w2_):
        meta_, n1_, gw2_, gb_, gnw_, pw_, ps_, n2_, fw_ = small
        return (meta_, n1_, win_[None], gw2_[None], gb_, gnw_, pw_[None], ps_, wout_[None], n2_, w1_[None], w2_[None],
                fw_.reshape(D))

    grads_out = assemble(gs, G_win, G_wout, G_w1, G_w2)
    deltas = assemble(ds, d_win, d_wout, d_w1, d_w2)
    new_m = assemble(nms, nm_win, nm_wout, nm_w1, nm_w2)
    new_v = assemble(nvs, nv_win, nv_wout, nv_w1, nv_w2)
    return (loss, grad_x[None], *grads_out, *deltas, *new_m, *new_v)
```

```python
import functools

import jax
import jax.numpy as jnp
from jax import lax
from jax.experimental import pallas as pl
from jax.experimental.pallas import tpu as pltpu

F32 = jnp.float32
BF16 = jnp.bfloat16

D = 2048
SEQ = 2048
N_META = 16
CH = 64
TP = 2176
NCH = TP // CH
ROW_LO = 48
X_LO = 64
ROW_HI = 2112
HEADS = 4
DK = 128
DV = 256
KW = HEADS * DK
GW = HEADS * DV
RANK = 16
TAU = 16.0
WINDOWS = (2, 4, 8, 16)
PW = 1024
GC = 256
DFF = 8192
EPS = 1e-6
SHARD_IN = 1028
PAD_IN = 1152
N_CHIP = 4

LR = 0.001
B1 = 0.9
B2 = 0.999
AEPS = 1e-08
WD = 0.01
STEP = 10

VMEM_LIMIT = 60 * 1024 * 1024
ANY = pl.BlockSpec(memory_space=pl.ANY)
VMEM_FULL = pl.BlockSpec(memory_space=pltpu.VMEM)
MESH = pl.DeviceIdType.MESH


def _cp(sem=None):
    if sem is None:
        return pltpu.CompilerParams(vmem_limit_bytes=VMEM_LIMIT)
    return pltpu.CompilerParams(dimension_semantics=sem, vmem_limit_bytes=VMEM_LIMIT)


def _dot(a, b):
    return jnp.dot(a, b, preferred_element_type=F32)


def _dot_nt(a, b):
    return lax.dot_general(a, b, (((1,), (1,)), ((), ())), preferred_element_type=F32)


def _dot_tn(a, b):
    return lax.dot_general(a, b, (((0,), (0,)), ((), ())), preferred_element_type=F32)


def _sds(shape, dtype):
    return jax.ShapeDtypeStruct(shape, dtype)


def _embed_norm(x, meta_full, w):
    def body(x_ref, meta_ref, w_ref, h_ref, u_ref):
        i = pl.program_id(0)

        @pl.when(i == 0)
        def _():
            h_ref[...] = jnp.zeros_like(h_ref)
            h_ref[ROW_LO:X_LO, :] = meta_ref[...]

        @pl.when((i >= 1) & (i <= 32))
        def _():
            h_ref[...] = x_ref[...]

        @pl.when(i == 33)
        def _():
            h_ref[...] = jnp.zeros_like(h_ref)

        h = h_ref[...]
        r = lax.rsqrt(jnp.mean(h * h, axis=-1, keepdims=True) + EPS)
        u_ref[...] = ((h * r) * w_ref[...]).astype(BF16)

    return pl.pallas_call(
        body, name="embed_norm1", grid=(NCH,),
        in_specs=[pl.BlockSpec((CH, D), lambda i: (jnp.clip(i - 1, 0, 31), 0)),
                  pl.BlockSpec((N_META, D), lambda i: (0, 0)),
                  pl.BlockSpec((1, D), lambda i: (0, 0))],
        out_specs=[pl.BlockSpec((CH, D), lambda i: (i, 0)), pl.BlockSpec((CH, D), lambda i: (i, 0))],
        out_shape=[_sds((TP, D), F32), _sds((TP, D), BF16)],
        compiler_params=_cp(("arbitrary",)),
    )(x, meta_full, w)


def _norm_rows(h, w, name):
    tr = 272

    def body(h_ref, w_ref, o_ref):
        hv = h_ref[...]
        r = lax.rsqrt(jnp.mean(hv * hv, axis=-1, keepdims=True) + EPS)
        o_ref[...] = ((hv * r) * w_ref[...]).astype(BF16)

    return pl.pallas_call(
        body, name=name, grid=(TP // tr,),
        in_specs=[pl.BlockSpec((tr, D), lambda i: (i, 0)), pl.BlockSpec((1, D), lambda i: (0, 0))],
        out_specs=pl.BlockSpec((tr, D), lambda i: (i, 0)),
        out_shape=_sds((TP, D), BF16),
        compiler_params=_cp(("arbitrary",)),
    )(h, w)


def _loss_head(h2, target, fw):
    def body(h_ref, t_ref, w_ref, dh_ref, dhb_ref, dw_ref, loss_ref):
        i = pl.program_id(0)

        @pl.when(i == 0)
        def _():
            dw_ref[...] = jnp.zeros_like(dw_ref)
            loss_ref[...] = jnp.zeros_like(loss_ref)

        h = h_ref[...]
        w = w_ref[...]
        r = lax.rsqrt(jnp.mean(h * h, axis=-1, keepdims=True) + EPS)
        xh = h * r
        y = xh * w
        is_x = ((i >= 1) & (i <= 32)).astype(F32)
        diff = (y - t_ref[...]) * is_x
        loss_ref[...] += jnp.sum(diff * diff) * (0.5 / D)
        dy = diff * (1.0 / D)
        dw_ref[...] += jnp.sum(dy * xh, axis=0, keepdims=True)
        gx = dy * w
        dh = r * (gx - xh * jnp.mean(gx * xh, axis=-1, keepdims=True))
        dh_ref[...] = dh
        dhb_ref[...] = dh.astype(BF16)

    return pl.pallas_call(
        body, name="loss_head", grid=(NCH,),
        in_specs=[pl.BlockSpec((CH, D), lambda i: (i, 0)),
                  pl.BlockSpec((CH, D), lambda i: (jnp.clip(i - 1, 0, 31), 0)),
                  pl.BlockSpec((1, D), lambda i: (0, 0))],
        out_specs=[pl.BlockSpec((CH, D), lambda i: (i, 0)), pl.BlockSpec((CH, D), lambda i: (i, 0)),
                   pl.BlockSpec((1, D), lambda i: (0, 0)), pl.BlockSpec((8, 128), lambda i: (0, 0))],
        out_shape=[_sds((TP, D), F32), _sds((TP, D), BF16), _sds((1, D), F32), _sds((8, 128), F32)],
        compiler_params=_cp(("arbitrary",)),
    )(h2, target, fw)


def _norm_bwd(dn, h, dres, w, name):
    tr = 272

    def body(dn_ref, h_ref, dres_ref, w_ref, o_ref, ob_ref, dw_ref):
        @pl.when(pl.program_id(0) == 0)
        def _():
            dw_ref[...] = jnp.zeros_like(dw_ref)

        hv = h_ref[...]
        dnv = dn_ref[...]
        r = lax.rsqrt(jnp.mean(hv * hv, axis=-1, keepdims=True) + EPS)
        xh = hv * r
        dw_ref[...] += jnp.sum(dnv * xh, axis=0, keepdims=True)
        gx = dnv * w_ref[...]
        dh = dres_ref[...] + r * (gx - xh * jnp.mean(gx * xh, axis=-1, keepdims=True))
        o_ref[...] = dh
        ob_ref[...] = dh.astype(BF16)

    row = pl.BlockSpec((tr, D), lambda i: (i, 0))
    vec = pl.BlockSpec((1, D), lambda i: (0, 0))
    return pl.pallas_call(
        body, name=name, grid=(TP // tr,),
        in_specs=[row, row, row, vec], out_specs=[row, row, vec],
        out_shape=[_sds((TP, D), F32), _sds((TP, D), BF16), _sds((1, D), F32)],
        compiler_params=_cp(("arbitrary",)),
    )(dn, h, dres, w)


def _input_grad(du, h0, dh1, w):
    def body(du_ref, h_ref, dres_ref, w_ref, gx_ref, gm_ref, dw_ref):
        i = pl.program_id(0)

        @pl.when(i == 0)
        def _():
            dw_ref[...] = jnp.zeros_like(dw_ref)

        hv = h_ref[...]
        dnv = du_ref[...]
        r = lax.rsqrt(jnp.mean(hv * hv, axis=-1, keepdims=True) + EPS)
        xh = hv * r
        dw_ref[...] += jnp.sum(dnv * xh, axis=0, keepdims=True)
        g = dnv * w_ref[...]
        dh = dres_ref[...] + r * (g - xh * jnp.mean(g * xh, axis=-1, keepdims=True))

        @pl.when(i == 0)
        def _():
            gm_ref[...] = dh[ROW_LO:X_LO, :]

        @pl.when((i >= 1) & (i <= 32))
        def _():
            gx_ref[...] = dh

    row = pl.BlockSpec((CH, D), lambda i: (i, 0))
    vec = pl.BlockSpec((1, D), lambda i: (0, 0))
    return pl.pallas_call(
        body, name="input_grad", grid=(NCH,),
        in_specs=[row, row, row, vec],
        out_specs=[pl.BlockSpec((CH, D), lambda i: (jnp.clip(i - 1, 0, 31), 0)),
                   pl.BlockSpec((N_META, D), lambda i: (0, 0)), vec],
        out_shape=[_sds((SEQ, D), F32), _sds((N_META, D), F32), _sds((1, D), F32)],
        compiler_params=_cp(("arbitrary",)),
    )(du, h0, dh1, w)


def _in_proj(u, wg):
    def body(u_ref, w_ref, o_ref):
        o_ref[0] = _dot(u_ref[...], w_ref[0])

    return pl.pallas_call(
        body, name="in_proj", grid=(N_CHIP,),
        in_specs=[VMEM_FULL, pl.BlockSpec((1, D, PAD_IN), lambda k: (k, 0, 0))],
        out_specs=pl.BlockSpec((1, TP, PAD_IN), lambda k: (k, 0, 0)),
        out_shape=_sds((N_CHIP, TP, PAD_IN), F32),
        compiler_params=_cp(("arbitrary",)),
    )(u, wg)


def _out_proj(og, op, wout, h0):
    tn = 512

    def body(og_ref, op_ref, w_ref, h_ref, o_ref):
        acc = _dot(og_ref[...], w_ref[0:GW, :]) + _dot(op_ref[...], w_ref[GW:D, :])
        o_ref[...] = h_ref[...] + acc

    return pl.pallas_call(
        body, name="out_proj", grid=(D // tn,),
        in_specs=[VMEM_FULL, VMEM_FULL, pl.BlockSpec((D, tn), lambda j: (0, j)),
                  pl.BlockSpec((TP, tn), lambda j: (0, j))],
        out_specs=pl.BlockSpec((TP, tn), lambda j: (0, j)),
        out_shape=_sds((TP, D), F32),
        compiler_params=_cp(("arbitrary",)),
    )(og, op, wout, h0)


def _mlp_up(n2, w1g):
    tn = 512
    per = D // tn

    def body(n_ref, w_ref, zr_ref, a_ref):
        z = jnp.maximum(_dot(n_ref[...], w_ref[0]), 0.0)
        zr_ref[...] = z.astype(BF16)
        a_ref[...] = (z * z).astype(BF16)

    col = pl.BlockSpec((TP, tn), lambda k, j: (0, k * per + j))
    return pl.pallas_call(
        body, name="mlp_up", grid=(N_CHIP, per),
        in_specs=[VMEM_FULL, pl.BlockSpec((1, D, tn), lambda k, j: (k, 0, j))],
        out_specs=[col, col],
        out_shape=[_sds((TP, DFF), BF16), _sds((TP, DFF), BF16)],
        compiler_params=_cp(("arbitrary", "arbitrary")),
    )(n2, w1g)


def _mlp_down(a, w2, h1):
    tk = 512
    nk = DFF // tk

    def body(a_ref, w_ref, h_ref, o_ref, acc_ref):
        k = pl.program_id(0)

        @pl.when(k == 0)
        def _():
            pltpu.sync_copy(h_ref, acc_ref)

        acc_ref[...] += _dot(a_ref[...], w_ref[...])

        @pl.when(k == nk - 1)
        def _():
            pltpu.sync_copy(acc_ref, o_ref)

    return pl.pallas_call(
        body, name="mlp_down", grid=(nk,),
        in_specs=[pl.BlockSpec((TP, tk), lambda k: (0, k)), pl.BlockSpec((tk, D), lambda k: (k, 0)), ANY],
        out_specs=ANY,
        out_shape=_sds((TP, D), F32),
        scratch_shapes=[pltpu.VMEM((TP, D), F32)],
        compiler_params=_cp(("arbitrary",)),
    )(a, w2, h1)


def _mlp_dz(dh2b, w2, zr):
    tn = 512

    def body(d_ref, w_ref, z_ref, o_ref):
        da = _dot_nt(d_ref[...], w_ref[...])
        o_ref[...] = (da * (2.0 * z_ref[...].astype(F32))).astype(BF16)

    col = pl.BlockSpec((TP, tn), lambda j: (0, j))
    return pl.pallas_call(
        body, name="mlp_dz", grid=(DFF // tn,),
        in_specs=[VMEM_FULL, pl.BlockSpec((tn, D), lambda j: (j, 0)), col],
        out_specs=col,
        out_shape=_sds((TP, DFF), BF16),
        compiler_params=_cp(("arbitrary",)),
    )(dh2b, w2, zr)


def _grad_w2(a, dh2b):
    tm = 512

    def body(a_ref, d_ref, o_ref):
        o_ref[...] = _dot_tn(a_ref[...], d_ref[...])

    return pl.pallas_call(
        body, name="grad_w2", grid=(DFF // tm,),
        in_specs=[pl.BlockSpec((TP, tm), lambda j: (0, j)), VMEM_FULL],
        out_specs=pl.BlockSpec((tm, D), lambda j: (j, 0)),
        out_shape=_sds((DFF, D), F32),
        compiler_params=_cp(("arbitrary",)),
    )(a, dh2b)


def _dep(token):
    return jnp.zeros((8, 128), F32) if token is None else token


def _grad_w1(n2, dz, dep=None):
    tn = 512
    per = D // tn

    def body(n_ref, d_ref, dep_ref, o_ref):
        o_ref[0] = _dot_tn(n_ref[...], d_ref[...])

    return pl.pallas_call(
        body, name="grad_w1", grid=(N_CHIP, per),
        in_specs=[VMEM_FULL, pl.BlockSpec((TP, tn), lambda k, j: (0, k * per + j)), ANY],
        out_specs=pl.BlockSpec((1, D, tn), lambda k, j: (k, 0, j)),
        out_shape=_sds((N_CHIP, D, D), F32),
        compiler_params=_cp(("arbitrary", "arbitrary")),
    )(n2, dz, _dep(dep))


def _mlp_dn(dz, w1g, dep=None):
    tk = 512
    per = D // tk
    nk = DFF // tk

    def body(d_ref, w_ref, dep_ref, o_ref, acc_ref):
        k = pl.program_id(0)
        part = _dot_nt(d_ref[...], w_ref[0])

        @pl.when(k == 0)
        def _():
            acc_ref[...] = part

        @pl.when(k > 0)
        def _():
            acc_ref[...] += part

        @pl.when(k == nk - 1)
        def _():
            pltpu.sync_copy(acc_ref, o_ref)

    return pl.pallas_call(
        body, name="mlp_dn", grid=(nk,),
        in_specs=[pl.BlockSpec((TP, tk), lambda k: (0, k)),
                  pl.BlockSpec((1, D, tk), lambda k: (k // per, 0, k % per)), ANY],
        out_specs=ANY,
        out_shape=_sds((TP, D), F32),
        scratch_shapes=[pltpu.VMEM((TP, D), F32)],
        compiler_params=_cp(("arbitrary",)),
    )(dz, w1g, _dep(dep))


def _mixed_grad(dh1b, wout):
    tn = 512

    def body(d_ref, w_ref, o_ref):
        o_ref[...] = _dot_nt(d_ref[...], w_ref[...])

    return pl.pallas_call(
        body, name="mixed_grad", grid=(D // tn,),
        in_specs=[VMEM_FULL, pl.BlockSpec((tn, D), lambda j: (j, 0))],
        out_specs=pl.BlockSpec((TP, tn), lambda j: (0, j)),
        out_shape=_sds((TP, D), F32),
        compiler_params=_cp(("arbitrary",)),
    )(dh1b, wout)


def _grad_wout(og, op, dh1b):
    tm = 512

    def body(og_ref, op_ref, d_ref, o_ref):
        j = pl.program_id(0)

        @pl.when(j < 2)
        def _():
            o_ref[0] = _dot_tn(og_ref[...], d_ref[...])

        @pl.when(j >= 2)
        def _():
            o_ref[0] = _dot_tn(op_ref[...], d_ref[...])

    return pl.pallas_call(
        body, name="grad_wout", grid=(N_CHIP,),
        in_specs=[pl.BlockSpec((TP, tm), lambda j: (0, jnp.minimum(j, 1))),
                  pl.BlockSpec((TP, tm), lambda j: (0, jnp.maximum(j - 2, 0))), VMEM_FULL],
        out_specs=pl.BlockSpec((1, tm, D), lambda j: (j, 0, 0)),
        out_shape=_sds((N_CHIP, tm, D), F32),
        compiler_params=_cp(("arbitrary",)),
    )(og, op, dh1b)


def _in_grad(dp, wg, dep=None):
    def body(d_ref, w_ref, dep_ref, o_ref, acc_ref):
        k = pl.program_id(0)
        part = _dot_nt(d_ref[0], w_ref[0])

        @pl.when(k == 0)
        def _():
            acc_ref[...] = part

        @pl.when(k > 0)
        def _():
            acc_ref[...] += part

        @pl.when(k == N_CHIP - 1)
        def _():
            pltpu.sync_copy(acc_ref, o_ref)

    return pl.pallas_call(
        body, name="in_grad", grid=(N_CHIP,),
        in_specs=[pl.BlockSpec((1, TP, PAD_IN), lambda k: (k, 0, 0)),
                  pl.BlockSpec((1, D, PAD_IN), lambda k: (k, 0, 0)), ANY],
        out_specs=ANY,
        out_shape=_sds((TP, D), F32),
        scratch_shapes=[pltpu.VMEM((TP, D), F32)],
        compiler_params=_cp(("arbitrary",)),
    )(dp, wg, _dep(dep))


def _grad_win(u, dp):
    def body(u_ref, d_ref, o_ref):
        g = _dot_tn(u_ref[...], d_ref[0])
        o_ref[0] = g[:, 0:SHARD_IN]

    return pl.pallas_call(
        body, name="grad_win", grid=(N_CHIP,),
        in_specs=[VMEM_FULL, pl.BlockSpec((1, TP, PAD_IN), lambda k: (k, 0, 0))],
        out_specs=pl.BlockSpec((1, D, SHARD_IN), lambda k: (k, 0, 0)),
        out_shape=_sds((N_CHIP, D, SHARD_IN), F32),
        compiler_params=_cp(("arbitrary",)),
    )(u, dp)


def _split3(x):
    hi = x.astype(BF16)
    r1 = x - hi.astype(F32)
    mid = r1.astype(BF16)
    lo = (r1 - mid.astype(F32)).astype(BF16)
    return hi, mid, lo


def _tri_sum(tri, x):
    hi, mid, lo = _split3(x)
    return _dot(tri, hi) + _dot(tri, mid) + _dot(tri, lo)


def _gla_common(n, glr, gw2, gb):
    rows = n * CH + lax.broadcasted_iota(jnp.int32, (CH, 1), 0)
    valid = (rows >= ROW_LO) & (rows < ROW_HI)
    g_raw = _dot(glr.astype(BF16), gw2.astype(BF16)) + gb
    logsig = jnp.minimum(g_raw, 0.0) - jnp.log(1.0 + jnp.exp(-jnp.abs(g_raw)))
    logg = jnp.where(valid, logsig * (1.0 / TAU), 0.0)
    ci = lax.broadcasted_iota(jnp.int32, (CH, CH), 0)
    si = lax.broadcasted_iota(jnp.int32, (CH, CH), 1)
    lower = ci >= si
    G = _tri_sum(lower.astype(BF16), logg)
    Gl = G[CH - 1:CH, :]
    return valid, g_raw, lower, G, Gl


def _gla_fwd(q, k, v, r, glr, gw2, gb, gnw):
    scale = DK ** -0.5

    def body(q_ref, k_ref, v_ref, r_ref, glr_ref, gw2_ref, gb_ref, gnw_ref, o_ref, og_ref, sp_ref, st_ref):
        n = pl.program_id(0)

        @pl.when(n == 0)
        def _():
            st_ref[...] = jnp.zeros_like(st_ref)

        _, _, lower, G, Gl = _gla_common(n, glr_ref[...], gw2_ref[...], gb_ref[...])
        eG = jnp.exp(G)
        eN = jnp.exp(-G)
        eE = jnp.exp(Gl - G)
        dec = jnp.exp(Gl)
        gnw_v = gnw_ref[...]
        for h in range(HEADS):
            ks = slice(h * DK, (h + 1) * DK)
            vs = slice(h * DV, (h + 1) * DV)
            kh = k_ref[:, ks]
            vh = v_ref[:, vs].astype(BF16)
            qd = ((q_ref[:, ks] * scale) * eG[:, ks]).astype(BF16)
            ki = (kh * eN[:, ks]).astype(BF16)
            ke = (kh * eE[:, ks]).astype(BF16)
            st = st_ref[h]
            a = jnp.where(lower, _dot_nt(qd, ki), 0.0).astype(BF16)
            o = _dot(a, vh) + _dot_nt(qd, st.astype(BF16))
            sp_ref[0, h] = st
            st_ref[h] = st * dec[:, ks] + _dot_tn(vh, ke)
            o_ref[:, vs] = o
            rs = lax.rsqrt(jnp.mean(o * o, axis=-1, keepdims=True) + EPS)
            rv = r_ref[:, vs]
            gate = rv / (1.0 + jnp.exp(-rv))
            og_ref[:, vs] = (((o * rs) * gnw_v) * gate).astype(BF16)

    rk = pl.BlockSpec((CH, KW), lambda n: (n, 0))
    rv_ = pl.BlockSpec((CH, GW), lambda n: (n, 0))

    def full(shape):
        return pl.BlockSpec(shape, lambda n: tuple(0 for _ in shape))

    return pl.pallas_call(
        body, name="gla_fwd", grid=(NCH,),
        in_specs=[rk, rk, rv_, rv_, pl.BlockSpec((CH, RANK), lambda n: (n, 0)),
                  full((RANK, KW)), full((1, KW)), full((1, DV))],
        out_specs=[rv_, rv_, pl.BlockSpec((1, HEADS, DV, DK), lambda n: (n, 0, 0, 0))],
        out_shape=[_sds((TP, GW), F32), _sds((TP, GW), BF16), _sds((NCH, HEADS, DV, DK), F32)],
        scratch_shapes=[pltpu.VMEM((HEADS, DV, DK), F32)],
        compiler_params=_cp(("arbitrary",)),
    )(q, k, v, r, glr, gw2, gb, gnw)


def _gla_bwd(dog, o, q, k, v, r, glr, gw2, gb, gnw, sp, dep=None):
    scale = DK ** -0.5

    def body(dog_ref, o_ref, q_ref, k_ref, v_ref, r_ref, glr_ref, gw2_ref, gb_ref, gnw_ref, sp_ref, dep_ref,
             dq_ref, dk_ref, dv_ref, dr_ref, dglr_ref, dgw2_ref, dgb_ref, dgnw_ref, ds_ref):
        step = pl.program_id(0)
        n = NCH - 1 - step

        @pl.when(step == 0)
        def _():
            ds_ref[...] = jnp.zeros_like(ds_ref)
            dgw2_ref[...] = jnp.zeros_like(dgw2_ref)
            dgb_ref[...] = jnp.zeros_like(dgb_ref)
            dgnw_ref[...] = jnp.zeros_like(dgnw_ref)

        glr_v = glr_ref[...]
        gw2_b = gw2_ref[...].astype(BF16)
        valid, g_raw, lower, G, Gl = _gla_common(n, glr_v, gw2_ref[...], gb_ref[...])
        upper = lax.broadcasted_iota(jnp.int32, (CH, CH), 0) <= lax.broadcasted_iota(jnp.int32, (CH, CH), 1)
        eG = jnp.exp(G)
        eN = jnp.exp(-G)
        eE = jnp.exp(Gl - G)
        dec = jnp.exp(Gl)
        gnw_v = gnw_ref[...]
        last = lax.broadcasted_iota(jnp.int32, (CH, 1), 0) == CH - 1
        dgnw_acc = jnp.zeros((1, DV), F32)
        dG_parts = []
        for h in range(HEADS):
            ks = slice(h * DK, (h + 1) * DK)
            vs = slice(h * DV, (h + 1) * DV)
            oh = o_ref[:, vs]
            rv = r_ref[:, vs]
            dg = dog_ref[:, vs]
            sig = 1.0 / (1.0 + jnp.exp(-rv))
            gate = rv * sig
            rs = lax.rsqrt(jnp.mean(oh * oh, axis=-1, keepdims=True) + EPS)
            ohat = oh * rs
            dr_ref[:, vs] = (dg * (ohat * gnw_v)) * (sig * (1.0 + rv * (1.0 - sig)))
            don = dg * gate
            dgnw_acc = dgnw_acc + jnp.sum(don * ohat, axis=0, keepdims=True)
            gxn = don * gnw_v
            do = (rs * (gxn - ohat * jnp.mean(gxn * ohat, axis=-1, keepdims=True))).astype(BF16)
            kh = k_ref[:, ks]
            vh = v_ref[:, vs].astype(BF16)
            qd_f = (q_ref[:, ks] * scale) * eG[:, ks]
            ki_f = kh * eN[:, ks]
            ke_f = kh * eE[:, ks]
            qd, ki, ke = qd_f.astype(BF16), ki_f.astype(BF16), ke_f.astype(BF16)
            spt = sp_ref[0, h]
            dst = ds_ref[h]
            dst_b = dst.astype(BF16)
            a_t = jnp.where(upper, _dot_nt(ki, qd), 0.0).astype(BF16)
            da = jnp.where(lower, _dot_nt(do, vh), 0.0).astype(BF16)
            da_t = jnp.where(upper, _dot_nt(vh, do), 0.0).astype(BF16)
            dv_ref[:, vs] = _dot(a_t, do) + _dot_nt(ke, dst_b)
            dqd = _dot(da, ki) + _dot(do, spt.astype(BF16))
            dki = _dot(da_t, qd)
            dke = _dot(vh, dst_b)
            ddec = jnp.sum(spt * dst, axis=0, keepdims=True)
            ds_ref[h] = dst * dec[:, ks] + _dot_tn(do, qd)
            dq_ref[:, ks] = (dqd * eG[:, ks]) * scale
            dk_ref[:, ks] = dki * eN[:, ks] + dke * eE[:, ks]
            dke_ke = dke * ke_f
            dG = dqd * qd_f - dki * ki_f - dke_ke
            dGl = jnp.sum(dke_ke, axis=0, keepdims=True) + ddec * dec[:, ks]
            dG_parts.append(dG + jnp.where(last, dGl, 0.0))
        dgnw_ref[...] += dgnw_acc
        dG_all = jnp.concatenate(dG_parts, axis=1)
        dlogg = jnp.where(valid, _tri_sum(upper.astype(BF16), dG_all), 0.0)
        dg_raw = (dlogg * (1.0 / TAU)) * (1.0 / (1.0 + jnp.exp(g_raw)))
        dgb_ref[...] += jnp.sum(dg_raw, axis=0, keepdims=True)
        dg_b = dg_raw.astype(BF16)
        dgw2_ref[...] += _dot_tn(glr_v.astype(BF16), dg_b)
        dglr_ref[...] = _dot_nt(dg_b, gw2_b)

    rk = pl.BlockSpec((CH, KW), lambda s: (NCH - 1 - s, 0))
    rv_ = pl.BlockSpec((CH, GW), lambda s: (NCH - 1 - s, 0))
    rg = pl.BlockSpec((CH, RANK), lambda s: (NCH - 1 - s, 0))

    def full(shape):
        return pl.BlockSpec(shape, lambda s: tuple(0 for _ in shape))

    return pl.pallas_call(
        body, name="gla_bwd", grid=(NCH,),
        in_specs=[rv_, rv_, rk, rk, rv_, rv_, rg, full((RANK, KW)), full((1, KW)), full((1, DV)),
                  pl.BlockSpec((1, HEADS, DV, DK), lambda s: (NCH - 1 - s, 0, 0, 0)), ANY],
        out_specs=[rk, rk, rv_, rv_, rg, full((RANK, KW)), full((1, KW)), full((1, DV))],
        out_shape=[_sds((TP, KW), F32), _sds((TP, KW), F32), _sds((TP, GW), F32), _sds((TP, GW), F32),
                   _sds((TP, RANK), F32), _sds((RANK, KW), F32), _sds((1, KW), F32), _sds((1, DV), F32)],
        scratch_shapes=[pltpu.VMEM((HEADS, DV, DK), F32)],
        compiler_params=_cp(("arbitrary",)),
    )(dog, o, q, k, v, r, glr, gw2, gb, gnw, sp, _dep(dep))


POOL_TR = 128
HALO = 16


def _pool_counts(base, nrows):
    rows = base + lax.broadcasted_iota(jnp.int32, (nrows, 1), 0)
    valid = (rows >= ROW_LO) & (rows < ROW_HI)
    t1 = (rows - ROW_LO + 1).astype(F32)
    cnts = [jnp.clip(t1, 1.0, float(w)) for w in WINDOWS]
    return valid, cnts


def _pool_fwd(pu, pw, ps):
    def body(cur_ref, prev_ref, pw_ref, ps_ref, y_ref, op_ref):
        i = pl.program_id(0)
        cur = cur_ref[...]
        full = jnp.concatenate([prev_ref[...], cur], axis=0)
        s2 = full + pltpu.roll(full, 1, 0)
        s4 = s2 + pltpu.roll(s2, 2, 0)
        s8 = s4 + pltpu.roll(s4, 4, 0)
        s16 = s8 + pltpu.roll(s8, 8, 0)
        valid, cnts = _pool_counts(i * POOL_TR, POOL_TR)
        for g, s in enumerate((s2, s4, s8, s16)):
            cs = slice(g * GC, (g + 1) * GC)
            y = s[HALO:, cs] / cnts[g] - cur[:, cs]
            yb = jnp.where(valid, y, 0.0).astype(BF16)
            y_ref[:, cs] = yb
            op_ref[:, cs] = (_dot(yb, pw_ref[g].astype(BF16)) * ps_ref[:, cs]).astype(BF16)

    row = pl.BlockSpec((POOL_TR, PW), lambda i: (i, 0))
    per = POOL_TR // HALO
    return pl.pallas_call(
        body, name="pool_fwd", grid=(TP // POOL_TR,),
        in_specs=[row, pl.BlockSpec((HALO, PW), lambda i: (jnp.maximum(i * per - 1, 0), 0)),
                  pl.BlockSpec((4, GC, GC), lambda i: (0, 0, 0)), pl.BlockSpec((1, PW), lambda i: (0, 0))],
        out_specs=[row, row],
        out_shape=[_sds((TP, PW), BF16), _sds((TP, PW), BF16)],
        compiler_params=_cp(("arbitrary",)),
    )(pu, pu, pw, ps)


def _pool_bwd(dop, y, pw, ps, dep=None):
    nblk = TP // HALO

    def body(cur_ref, nxt_ref, y_ref, pw_ref, ps_ref, dep_ref, dpu_ref, dpw_ref, dps_ref):
        i = pl.program_id(0)

        @pl.when(i == 0)
        def _():
            dpw_ref[...] = jnp.zeros_like(dpw_ref)
            dps_ref[...] = jnp.zeros_like(dps_ref)

        n_all = POOL_TR + HALO
        dcur = cur_ref[...]
        dall = jnp.concatenate([dcur, nxt_ref[...]], axis=0)
        valid, cnts = _pool_counts(i * POOL_TR, n_all)
        for g in range(4):
            cs = slice(g * GC, (g + 1) * GC)
            pwb = pw_ref[g].astype(BF16)
            yb = y_ref[:, cs]
            dyw = (dall[:, cs] * ps_ref[:, cs]).astype(BF16)
            dps_ref[:, cs] += jnp.sum(dcur[:, cs] * _dot(yb, pwb), axis=0, keepdims=True)
            dpw_ref[g] += _dot_tn(yb, dyw[0:POOL_TR, :])
            dyv = jnp.where(valid, _dot_nt(dyw, pwb), 0.0)
            e = dyv / cnts[g]
            w = WINDOWS[g]
            sh = 1
            while sh < w:
                e = e + pltpu.roll(e, n_all - sh, 0)
                sh *= 2
            dpu_ref[:, cs] = e[0:POOL_TR, :] - dyv[0:POOL_TR, :]

    row = pl.BlockSpec((POOL_TR, PW), lambda i: (i, 0))
    per = POOL_TR // HALO
    return pl.pallas_call(
        body, name="pool_bwd", grid=(TP // POOL_TR,),
        in_specs=[pl.BlockSpec((POOL_TR, PW), lambda i: (i, 1)),
                  pl.BlockSpec((HALO, PW), lambda i: (jnp.minimum(i * per + per, nblk - 1), 1)),
                  row, pl.BlockSpec((4, GC, GC), lambda i: (0, 0, 0)), pl.BlockSpec((1, PW), lambda i: (0, 0)), ANY],
        out_specs=[row, pl.BlockSpec((4, GC, GC), lambda i: (0, 0, 0)), pl.BlockSpec((1, PW), lambda i: (0, 0))],
        out_shape=[_sds((TP, PW), F32), _sds((4, GC, GC), F32), _sds((1, PW), F32)],
        compiler_params=_cp(("arbitrary",)),
    )(dop, dop, y, pw, ps, _dep(dep))


def _place():
    x, y, c = lax.axis_index("x"), lax.axis_index("y"), lax.axis_index("c")
    chips = [(1 - x, y), (x, 1 - y), (1 - x, 1 - y)]
    return x, y, c, chips


HBM = pl.BlockSpec(memory_space=pltpu.HBM)
SEM = pl.BlockSpec(memory_space=pltpu.SEMAPHORE)
EFFECT = pltpu.SideEffectType.DATAFLOW_SIDE_EFFECTING


def _cast_into(w, place, cols_out, name, dep=None):
    rows, cols = w.shape
    tr = 256

    def body(p_ref, w_ref, dep_ref, o_ref):
        if cols_out != cols:
            o_ref[0] = jnp.zeros((tr, cols_out), BF16)
            o_ref[0, :, 0:cols] = w_ref[...].astype(BF16)
        else:
            o_ref[0] = w_ref[...].astype(BF16)

    grid_spec = pltpu.PrefetchScalarGridSpec(
        num_scalar_prefetch=1, grid=(rows // tr,),
        in_specs=[pl.BlockSpec((tr, cols), lambda i, p: (i, 0)), ANY],
        out_specs=pl.BlockSpec((1, tr, cols_out), lambda i, p: (p[0], i, 0)))
    return pl.pallas_call(
        body, name=name, grid_spec=grid_spec,
        out_shape=_sds((N_CHIP, rows, cols_out), BF16),
        compiler_params=_cp(("arbitrary",)),
    )(place, w, _dep(dep))


def _half_rows(ref, k, which):
    h = ref.shape[1] // 2
    return ref.at[k, pl.ds(pl.multiple_of(which * h, 8), h), :]


def _gather_small(meta, gw2, pw, dep=None):
    def body(meta_r, gw2_r, pw_r, dep_ref, metaF, gw2F, pwF, lsem, ssem, rsem):
        x, y, c, chips = _place()
        me = 2 * x + y

        def slots(k):
            return (metaF.at[:, pl.ds(pl.multiple_of(k * 512, 128), 512)],
                    gw2F.at[:, pl.ds(pl.multiple_of(k * 128, 128), 128)],
                    pwF.at[:, pl.ds(pl.multiple_of(k * 64, 8), 64), :])

        srcs = (meta_r, gw2_r, pw_r)
        local = [pltpu.make_async_copy(s, d, lsem.at[i]) for i, (s, d) in enumerate(zip(srcs, slots(me)))]
        sends = []
        for j, chip in enumerate(chips):
            for i, (s, d) in enumerate(zip(srcs, slots(me))):
                sends.append(pltpu.make_async_remote_copy(src_ref=s, dst_ref=d, send_sem=ssem.at[3 * j + i],
                                                          recv_sem=rsem.at[3 * j + i], device_id=(*chip, c),
                                                          device_id_type=MESH))
        for cp in local + sends:
            cp.start()
        for j, (cx, cy) in enumerate(chips):
            for i, (s, d) in enumerate(zip(srcs, slots(2 * cx + cy))):
                pltpu.make_async_remote_copy(src_ref=s, dst_ref=d, send_sem=ssem.at[3 * j + i], recv_sem=rsem.at[3 * j + i],
                                             device_id=(cx, cy, c), device_id_type=MESH).wait_recv()
        for cp in sends:
            cp.wait_send()
        for cp in local:
            cp.wait()

    return pl.pallas_call(
        body, name="gather_small",
        in_specs=[ANY] * 4, out_specs=[ANY] * 3,
        out_shape=[_sds((N_META, D), F32), _sds((RANK, KW), F32), _sds((4, GC, GC), F32)],
        scratch_shapes=[pltpu.SemaphoreType.DMA((3,)), pltpu.SemaphoreType.DMA((9,)), pltpu.SemaphoreType.DMA((9,))],
    )(meta, gw2, pw, _dep(dep))


def _gather_start(ws, name):
    n = len(ws)

    def body(*refs):
        ins = refs[:n]
        ssems = refs[n:2 * n]
        rsems = refs[2 * n:3 * n]
        token = refs[4 * n]
        x, y, c, chips = _place()
        me = 2 * x + y
        for w in range(n):
            blk = _half_rows(ins[w], me, c)
            for j, chip in enumerate(chips):
                pltpu.make_async_remote_copy(src_ref=blk, dst_ref=blk, send_sem=ssems[w].at[j], recv_sem=rsems[w].at[j],
                                             device_id=(*chip, c), device_id_type=MESH).start()
        token[...] = jnp.zeros_like(token)

    sem3 = pltpu.SemaphoreType.DMA((3,))
    outs = pl.pallas_call(
        body, name=name,
        out_shape=tuple([sem3] * (2 * n) + [pltpu.HBM(w.shape, w.dtype) for w in ws] + [_sds((8, 128), F32)]),
        in_specs=(HBM,) * n, out_specs=(SEM,) * (2 * n) + (HBM,) * n + (VMEM_FULL,),
        input_output_aliases={w: 2 * n + w for w in range(n)},
        compiler_params=pltpu.CompilerParams(has_side_effects=EFFECT),
    )(*[pltpu.with_memory_space_constraint(w, pltpu.HBM) for w in ws])
    return outs[:n], outs[n:2 * n], outs[2 * n:3 * n], outs[3 * n]


def _gather_wait(w, ssem, rsem, after, name):
    def body(w_ref, ssem_ref, rsem_ref, after_ref, out_ref):
        x, y, c, chips = _place()
        me = 2 * x + y
        mine = _half_rows(w_ref, me, c)
        for j, (cx, cy) in enumerate(chips):
            cp = pltpu.make_async_remote_copy(src_ref=mine, dst_ref=_half_rows(w_ref, 2 * cx + cy, c),
                                              send_sem=ssem_ref.at[j], recv_sem=rsem_ref.at[j],
                                              device_id=(cx, cy, c), device_id_type=MESH)
            cp.wait_send()
            cp.wait_recv()

    return pl.pallas_call(
        body, name=name, out_shape=pltpu.HBM(w.shape, w.dtype),
        in_specs=(HBM, SEM, SEM, ANY), out_specs=HBM, input_output_aliases={0: 0},
        compiler_params=pltpu.CompilerParams(has_side_effects=EFFECT),
    )(w, ssem, rsem, after)


def _forward_halves(w, name):
    def body(w_ref, o_ref, ssem, rsem):
        x, y, c, chips = _place()
        sib = (x, y, 1 - c)
        cps = []
        for j, (cx, cy) in enumerate(chips):
            blk = _half_rows(o_ref, 2 * cx + cy, c)
            cps.append(pltpu.make_async_remote_copy(src_ref=blk, dst_ref=blk, send_sem=ssem.at[j], recv_sem=rsem.at[j],
                                                    device_id=sib, device_id_type=MESH))
        for cp in cps:
            cp.start()
        for j, (cx, cy) in enumerate(chips):
            blk = _half_rows(o_ref, 2 * cx + cy, 1 - c)
            pltpu.make_async_remote_copy(src_ref=blk, dst_ref=blk, send_sem=ssem.at[j], recv_sem=rsem.at[j],
                                         device_id=sib, device_id_type=MESH).wait_recv()
        for cp in cps:
            cp.wait_send()

    return pl.pallas_call(
        body, name=name, in_specs=[ANY], out_specs=ANY, out_shape=_sds(w.shape, w.dtype),
        input_output_aliases={0: 0},
        scratch_shapes=[pltpu.SemaphoreType.DMA((3,)), pltpu.SemaphoreType.DMA((3,))],
    )(w)


def _rs_start(sb, name):
    _, half, cols = sb.shape

    def body(sb_ref, land_ref, ssem, rsem, sb_out, land_out, token):
        x, y, c, chips = _place()
        for j, (cx, cy) in enumerate(chips):
            pltpu.make_async_remote_copy(src_ref=sb_ref.at[2 * cx + cy], dst_ref=land_ref.at[j], send_sem=ssem.at[j],
                                         recv_sem=rsem.at[j], device_id=(cx, cy, c), device_id_type=MESH).start()
        token[...] = jnp.zeros_like(token)

    sem3 = pltpu.SemaphoreType.DMA((3,))
    land = lax.empty((3, half, cols), BF16)
    return pl.pallas_call(
        body, name=name,
        out_shape=(sem3, sem3, pltpu.HBM(sb.shape, sb.dtype), pltpu.HBM(land.shape, land.dtype), _sds((8, 128), F32)),
        in_specs=(HBM, HBM), out_specs=(SEM, SEM, HBM, HBM, VMEM_FULL), input_output_aliases={0: 2, 1: 3},
        compiler_params=pltpu.CompilerParams(has_side_effects=EFFECT),
    )(pltpu.with_memory_space_constraint(sb, pltpu.HBM), pltpu.with_memory_space_constraint(land, pltpu.HBM))


def _rs_wait(sb, land, ssem, rsem, after, name):
    def body(sb_ref, land_ref, ssem_ref, rsem_ref, after_ref, sb_out, land_out):
        x, y, c, chips = _place()
        for j, (cx, cy) in enumerate(chips):
            cp = pltpu.make_async_remote_copy(src_ref=sb_ref.at[2 * cx + cy], dst_ref=land_ref.at[j], send_sem=ssem_ref.at[j],
                                              recv_sem=rsem_ref.at[j], device_id=(cx, cy, c), device_id_type=MESH)
            cp.wait_send()
            cp.wait_recv()

    return pl.pallas_call(
        body, name=name,
        out_shape=(pltpu.HBM(sb.shape, sb.dtype), pltpu.HBM(land.shape, land.dtype)),
        in_specs=(HBM, HBM, SEM, SEM, ANY), out_specs=(HBM, HBM), input_output_aliases={0: 0, 1: 1},
        compiler_params=pltpu.CompilerParams(has_side_effects=EFFECT),
    )(sb, land, ssem, rsem, after)[1]


def _pair_exchange(grads, name):
    ng = len(grads)

    def body(*refs):
        g = refs[:ng]
        r = refs[ng:2 * ng]
        ssem, rsem = refs[2 * ng], refs[2 * ng + 1]
        x, y, c, _ = _place()
        sib = (x, y, 1 - c)
        cps = []
        for w in range(ng):
            h = g[w].shape[1] // 2
            src = g[w].at[:, pl.ds(pl.multiple_of((1 - c) * h, 8), h), :]
            cps.append(pltpu.make_async_remote_copy(src_ref=src, dst_ref=r[w], send_sem=ssem.at[w], recv_sem=rsem.at[w],
                                                    device_id=sib, device_id_type=MESH))
        for cp in cps:
            cp.start()
        for cp in cps:
            cp.wait()

    out_shape = [_sds((N_CHIP, g.shape[1] // 2, g.shape[2]), F32) for g in grads]
    return pl.pallas_call(
        body, name=name,
        in_specs=[ANY] * ng, out_specs=[ANY] * ng, out_shape=out_shape,
        scratch_shapes=[pltpu.SemaphoreType.DMA((ng,)), pltpu.SemaphoreType.DMA((ng,))],
    )(*grads)


def _pair_sum(g, rcv, place, name):
    _, rows, cols = g.shape
    half = rows // 2
    tr = 256
    nt = half // tr

    def body(p_ref, g_ref, r_ref, sb_ref, sf_ref):
        s = pl.program_id(1)
        tot = g_ref[0] + r_ref[0]
        sb_ref[0] = tot.astype(BF16)

        @pl.when(s == p_ref[0])
        def _():
            sf_ref[...] = tot

    grid_spec = pltpu.PrefetchScalarGridSpec(
        num_scalar_prefetch=1, grid=(nt, N_CHIP),
        in_specs=[pl.BlockSpec((1, tr, cols), lambda t, s, p: (s, p[1] * nt + t, 0)),
                  pl.BlockSpec((1, tr, cols), lambda t, s, p: (s, t, 0))],
        out_specs=[pl.BlockSpec((1, tr, cols), lambda t, s, p: (s, t, 0)),
                   pl.BlockSpec((tr, cols), lambda t, s, p: (t, 0))])
    return pl.pallas_call(
        body, name=name, grid_spec=grid_spec,
        out_shape=[_sds((N_CHIP, half, cols), BF16), _sds((half, cols), F32)],
        compiler_params=_cp(("arbitrary", "arbitrary")),
    )(place, g, rcv)


def _final_sum(sf, rb, place, name):
    half, cols = sf.shape
    tr = 256
    nt = half // tr

    def body(p_ref, sf_ref, r_ref, out_ref):
        acc = sf_ref[...]
        for j in range(3):
            acc = acc + r_ref[j].astype(F32)
        out_ref[...] = acc

    grid_spec = pltpu.PrefetchScalarGridSpec(
        num_scalar_prefetch=1, grid=(nt,),
        in_specs=[pl.BlockSpec((tr, cols), lambda t, p: (t, 0)), pl.BlockSpec((3, tr, cols), lambda t, p: (0, t, 0))],
        out_specs=pl.BlockSpec((tr, cols), lambda t, p: (p[1] * nt + t, 0)))
    return pl.pallas_call(
        body, name=name, grid_spec=grid_spec,
        out_shape=_sds((2 * half, cols), F32),
        compiler_params=_cp(("arbitrary",)),
    )(place, sf, rb)


def _half_exchange(fulls, name):
    ng = len(fulls)

    def body(*refs):
        outs = refs[ng:2 * ng]
        ssem, rsem = refs[2 * ng], refs[2 * ng + 1]
        x, y, c, _ = _place()
        sib = (x, y, 1 - c)
        cps = []
        for w in range(ng):
            h = outs[w].shape[0] // 2
            mine = outs[w].at[pl.ds(pl.multiple_of(c * h, 8), h), :]
            cps.append(pltpu.make_async_remote_copy(src_ref=mine, dst_ref=mine, send_sem=ssem.at[w], recv_sem=rsem.at[w],
                                                    device_id=sib, device_id_type=MESH))
        for cp in cps:
            cp.start()
        for w in range(ng):
            h = outs[w].shape[0] // 2
            theirs = outs[w].at[pl.ds(pl.multiple_of((1 - c) * h, 8), h), :]
            pltpu.make_async_remote_copy(src_ref=theirs, dst_ref=theirs, send_sem=ssem.at[w], recv_sem=rsem.at[w],
                                         device_id=sib, device_id_type=MESH).wait_recv()
        for cp in cps:
            cp.wait_send()

    return pl.pallas_call(
        body, name=name,
        in_specs=[ANY] * ng, out_specs=[ANY] * ng, out_shape=[_sds(f.shape, F32) for f in fulls],
        input_output_aliases={w: w for w in range(ng)},
        scratch_shapes=[pltpu.SemaphoreType.DMA((ng,)), pltpu.SemaphoreType.DMA((ng,))],
    )(*fulls)


def _small_allreduce(vec):
    nr = vec.shape[0]

    def body(v_ref, o_ref, rsib, pair, rchip, ssem, rsem):
        x, y, c, chips = _place()
        me = 2 * x + y
        sib = (x, y, 1 - c)
        first = pltpu.make_async_remote_copy(src_ref=v_ref, dst_ref=rsib, send_sem=ssem.at[0], recv_sem=rsem.at[0],
                                             device_id=sib, device_id_type=MESH)
        first.start()
        first.wait()
        pair[...] = v_ref[...] + rsib[...]
        cps = [pltpu.make_async_remote_copy(src_ref=pair, dst_ref=rchip.at[j], send_sem=ssem.at[1 + j], recv_sem=rsem.at[1 + j],
                                            device_id=(*chip, c), device_id_type=MESH) for j, chip in enumerate(chips)]
        for cp in cps:
            cp.start()
        for cp in cps:
            cp.wait()
        acc = None
        for kk in range(N_CHIP):
            d = jnp.bitwise_xor(me, kk)
            t = jnp.where(d == 0, pair[...], jnp.where(d == 2, rchip[0], jnp.where(d == 1, rchip[1], rchip[2])))
            acc = t if acc is None else acc + t
        o_ref[...] = acc

    return pl.pallas_call(
        body, name="small_allreduce",
        in_specs=[VMEM_FULL], out_specs=VMEM_FULL, out_shape=_sds((nr, 128), F32),
        scratch_shapes=[pltpu.VMEM((nr, 128), F32), pltpu.VMEM((nr, 128), F32), pltpu.VMEM((3, nr, 128), F32),
                        pltpu.SemaphoreType.DMA((4,)), pltpu.SemaphoreType.DMA((4,))],
        compiler_params=_cp(),
    )(vec)


def _adam_math(w, g, m, v):
    m = B1 * m + (1.0 - B1) * g
    v = B2 * v + (1.0 - B2) * (g * g)
    m_hat = m / (1.0 - B1 ** STEP)
    v_hat = v / (1.0 - B2 ** STEP)
    delta = -LR * (m_hat / (jnp.sqrt(v_hat) + AEPS) + WD * w)
    return delta, m, v


def _adam_big(w, g, m, v, name):
    rows, cols = w.shape
    tr = 128

    def body(w_ref, g_ref, m_ref, v_ref, go_ref, d_ref, nm_ref, nv_ref):
        g = g_ref[...]
        d, nm, nv = _adam_math(w_ref[...], g, m_ref[...], v_ref[...])
        go_ref[...] = g
        d_ref[...] = d
        nm_ref[...] = nm
        nv_ref[...] = nv

    blk = pl.BlockSpec((tr, cols), lambda i: (i, 0))
    return pl.pallas_call(
        body, name=name, grid=(rows // tr,),
        in_specs=[blk] * 4, out_specs=[blk] * 4, out_shape=[_sds((rows, cols), F32)] * 4,
        compiler_params=_cp(("arbitrary",)),
    )(w, g, m, v)


def _adam_small(ws, gs, ms, vs):
    n = len(ws)

    def body(*refs):
        for i in range(n):
            d, nm, nv = _adam_math(refs[i][...], refs[n + i][...], refs[2 * n + i][...], refs[3 * n + i][...])
            refs[4 * n + i][...] = d
            refs[5 * n + i][...] = nm
            refs[6 * n + i][...] = nv

    shapes = [_sds(w.shape, F32) for w in ws]
    outs = pl.pallas_call(
        body, name="adam_small",
        in_specs=[VMEM_FULL] * (4 * n), out_specs=[VMEM_FULL] * (3 * n), out_shape=shapes * 3,
        compiler_params=_cp(),
    )(*ws, *gs, *ms, *vs)
    return outs[:n], outs[n:2 * n], outs[2 * n:]


def _pad_rows8(a):
    flat = a.reshape(-1, 128)
    pad = (-flat.shape[0]) % 8
    if pad:
        flat = jnp.concatenate([flat, jnp.zeros((pad, 128), F32)], axis=0)
    return flat


def kernel(x, meta_tokens, norm1_w, w_in, gate_w2, gate_b, gla_norm_w, pool_w, pool_scale, w_out, norm2_w, mlp_w1, mlp_w2, final_norm_w, loss_target, m_meta_tokens, m_norm1_w, m_w_in, m_gate_w2, m_gate_b, m_gla_norm_w, m_pool_w, m_pool_scale, m_w_out, m_norm2_w, m_mlp_w1, m_mlp_w2, m_final_norm_w, v_meta_tokens, v_norm1_w, v_w_in, v_gate_w2, v_gate_b, v_gla_norm_w, v_pool_w, v_pool_scale, v_w_out, v_norm2_w, v_mlp_w1, v_mlp_w2, v_final_norm_w):
    cx, cy, cc = lax.axis_index("x"), lax.axis_index("y"), lax.axis_index("c")
    me = (2 * cx + cy).astype(jnp.int32)

    place = jnp.stack([me, cc.astype(jnp.int32)])
    fw = final_norm_w.reshape(1, D)

    (s_win,), (r_win,), (f_win,), tok = _gather_start([_cast_into(w_in[0], place, PAD_IN, "cast_win")], "gather_start_win")
    rest = [_cast_into(w_out[0], place, D, "cast_wout", tok), _cast_into(mlp_w1[0], place, D, "cast_w1", tok),
            _cast_into(mlp_w2[0], place, D, "cast_w2", tok)]
    ssems, rsems, flying, tok = _gather_start(rest, "gather_start_rest")
    ssems, rsems, flying = [s_win, *ssems], [r_win, *rsems], [f_win, *flying]
    metaF, gw2F, pwF = _gather_small(meta_tokens, gate_w2[0], pool_w[0], tok)

    def arrive(i, nm, after):
        return _forward_halves(_gather_wait(flying[i], ssems[i], rsems[i], after, "gather_wait_" + nm), "forward_" + nm)

    pending = {}

    def on_grad(nm, g):
        rcv, = _pair_exchange([g], "pair_exchange_" + nm)
        sb, sf = _pair_sum(g, rcv, place, "pair_sum_" + nm)
        ssem, rsem, sb_thru, land, token = _rs_start(sb, "rs_start_" + nm)
        pending[nm] = (sf, ssem, rsem, sb_thru, land)
        return token

    (grad_x, loss8, d_n1w, d_gb, d_gnw, d_ps, d_n2w, d_fw, d_meta, d_gw2, d_pw) = _local_step(
        x[0], loss_target[0], lambda after: arrive(0, "win", after), lambda after: arrive(1, "wout", after).reshape(D, D),
        lambda after: arrive(2, "w1", after), lambda after: arrive(3, "w2", after).reshape(DFF, D), metaF, gw2F, pwF,
        norm1_w, gate_b, gla_norm_w, pool_scale, norm2_w, fw, on_grad)
    return _reduce_and_update(
        me, place, pending, grad_x, loss8, d_n1w, d_gb, d_gnw, d_ps, d_n2w, d_fw, d_meta, d_gw2, d_pw,
        meta_tokens, norm1_w, w_in, gate_w2, gate_b, gla_norm_w, pool_w, pool_scale, w_out, norm2_w, mlp_w1, mlp_w2, fw,
        m_meta_tokens, m_norm1_w, m_w_in, m_gate_w2, m_gate_b, m_gla_norm_w, m_pool_w, m_pool_scale, m_w_out, m_norm2_w,
        m_mlp_w1, m_mlp_w2, m_final_norm_w, v_meta_tokens, v_norm1_w, v_w_in, v_gate_w2, v_gate_b, v_gla_norm_w, v_pool_w,
        v_pool_scale, v_w_out, v_norm2_w, v_mlp_w1, v_mlp_w2, v_final_norm_w)


def _local_step(x, target, get_win, get_wout, get_w1, get_w2, metaF, gw2F, pwF, norm1_w, gate_b, gla_norm_w, pool_scale,
                norm2_w, fw, on_grad):
    h0, u = _embed_norm(x, metaF, norm1_w)
    Win = get_win(u)
    P = _in_proj(u, Win)
    q = P[0, :, 0:512]
    k = P[0, :, 512:1024]
    v = jnp.concatenate([P[0, :, 1024:1028], P[1, :, 0:1020]], axis=1)
    r = jnp.concatenate([P[1, :, 1020:1028], P[2, :, 0:1016]], axis=1)
    glr = jnp.concatenate([P[2, :, 1016:1028], P[3, :, 0:4]], axis=1)
    pu = P[3, :, 4:1028]
    o, og, sp = _gla_fwd(q, k, v, r, glr, gw2F, gate_b, gla_norm_w)
    yb, op = _pool_fwd(pu, pwF, pool_scale)
    Wout = get_wout(op)
    h1 = _out_proj(og, op, Wout, h0)
    n2 = _norm_rows(h1, norm2_w, "norm2")
    W1 = get_w1(n2)
    zr, a = _mlp_up(n2, W1)
    W2 = get_w2(a)
    h2 = _mlp_down(a, W2, h1)

    dh2, dh2b, d_fw, loss8 = _loss_head(h2, target, fw)
    dz = _mlp_dz(dh2b, W2, zr)
    tok = on_grad("w2", _grad_w2(a, dh2b).reshape(N_CHIP, D, D))
    tok = on_grad("w1", _grad_w1(n2, dz, tok))
    dn2 = _mlp_dn(dz, W1, tok)
    dh1, dh1b, d_n2w = _norm_bwd(dn2, h1, dh2, norm2_w, "norm2_bwd")
    dmixed = _mixed_grad(dh1b, Wout)
    tok = on_grad("wout", _grad_wout(og, op, dh1b))
    dpu, d_pw, d_ps = _pool_bwd(dmixed, yb, pwF, pool_scale, tok)
    dq, dk, dv, dr, dglr, d_gw2, d_gb, d_gnw = _gla_bwd(dmixed, o, q, k, v, r, glr, gw2F, gate_b, gla_norm_w, sp, tok)
    zpad = jnp.zeros((TP, PAD_IN - SHARD_IN), F32)
    dP = jnp.stack([
        jnp.concatenate([dq, dk, dv[:, 0:4], zpad], axis=1),
        jnp.concatenate([dv[:, 4:], dr[:, 0:8], zpad], axis=1),
        jnp.concatenate([dr[:, 8:], dglr[:, 0:12], zpad], axis=1),
        jnp.concatenate([dglr[:, 12:], dpu, zpad], axis=1)]).astype(BF16)
    tok = on_grad("win", _grad_win(u, dP))
    du = _in_grad(dP, Win, tok)
    grad_x, d_meta, d_n1w = _input_grad(du, h0, dh1, norm1_w)
    return grad_x, loss8, d_n1w, d_gb, d_gnw, d_ps, d_n2w, d_fw, d_meta, d_gw2, d_pw


def _reduce_and_update(me, place, pending, grad_x, loss8, d_n1w, d_gb, d_gnw, d_ps, d_n2w, d_fw, d_meta, d_gw2, d_pw,
                       meta_tokens, norm1_w, w_in, gate_w2, gate_b, gla_norm_w, pool_w, pool_scale, w_out, norm2_w,
                       mlp_w1, mlp_w2, fw, m_meta_tokens, m_norm1_w, m_w_in, m_gate_w2, m_gate_b, m_gla_norm_w, m_pool_w,
                       m_pool_scale, m_w_out, m_norm2_w, m_mlp_w1, m_mlp_w2, m_final_norm_w, v_meta_tokens, v_norm1_w, v_w_in,
                       v_gate_w2, v_gate_b, v_gla_norm_w, v_pool_w, v_pool_scale, v_w_out, v_norm2_w, v_mlp_w1, v_mlp_w2,
                       v_final_norm_w):
    parts = [loss8, d_n1w, d_gb, d_gnw, d_ps, d_n2w, d_fw, d_meta, d_gw2, d_pw]
    packed = [_pad_rows8(p) for p in parts]
    sizes = [p.shape[0] for p in packed]
    red = _small_allreduce(jnp.concatenate(packed, axis=0))
    offs = [0]
    for s in sizes:
        offs.append(offs[-1] + s)

    def take(i, shape):
        n = 1
        for d in shape:
            n *= d
        return red[offs[i]:offs[i] + n // 128].reshape(shape)

    loss = red[0, 0]
    G_n1w = take(1, (1, D))
    G_gb = take(2, (1, KW))
    G_gnw = take(3, (1, DV))
    G_ps = take(4, (1, PW))
    G_n2w = take(5, (1, D))
    G_fw = take(6, (1, D))
    G_meta = lax.dynamic_slice(take(7, (N_META, D)), (0, me * 512), (N_META, 512))
    G_gw2 = lax.dynamic_slice(take(8, (RANK, KW)), (0, me * 128), (RANK, 128))
    G_pw = lax.dynamic_slice(take(9, (4, GC, GC)), (0, me * 64, 0), (4, 64, GC))

    big = {}
    after = red
    for nm, w, m, v in (("w2", mlp_w2, m_mlp_w2, v_mlp_w2), ("w1", mlp_w1, m_mlp_w1, v_mlp_w1),
                        ("wout", w_out, m_w_out, v_w_out), ("win", w_in, m_w_in, v_w_in)):
        sf, ssem, rsem, sb_thru, land = pending[nm]
        rb = _rs_wait(sb_thru, land, ssem, rsem, after, "rs_wait_" + nm)
        full, = _half_exchange([_final_sum(sf, rb, place, "final_sum_" + nm)], "half_exchange_" + nm)
        big[nm] = _adam_big(w[0], full, m[0], v[0], "adam_" + nm)
        after = big[nm][3]
    G_win, d_win, nm_win, nv_win = big["win"]
    G_wout, d_wout, nm_wout, nv_wout = big["wout"]
    G_w1, d_w1, nm_w1, nv_w1 = big["w1"]
    G_w2, d_w2, nm_w2, nv_w2 = big["w2"]
    ws = [meta_tokens, norm1_w, gate_w2[0], gate_b, gla_norm_w, pool_w[0], pool_scale, norm2_w, fw]
    gs = [G_meta, G_n1w, G_gw2, G_gb, G_gnw, G_pw, G_ps, G_n2w, G_fw]
    ms = [m_meta_tokens, m_norm1_w, m_gate_w2[0], m_gate_b, m_gla_norm_w, m_pool_w[0], m_pool_scale, m_norm2_w,
          m_final_norm_w.reshape(1, D)]
    vs = [v_meta_tokens, v_norm1_w, v_gate_w2[0], v_gate_b, v_gla_norm_w, v_pool_w[0], v_pool_scale, v_norm2_w,
          v_final_norm_w.reshape(1, D)]
    ds, nms, nvs = _adam_small(ws, gs, ms, vs)

    def assemble(small, win_, wout_, w1_, w2_):
        meta_, n1_, gw2_, gb_, gnw_, pw_, ps_, n2_, fw_ = small
        return (meta_, n1_, win_[None], gw2_[None], gb_, gnw_, pw_[None], ps_, wout_[None], n2_, w1_[None], w2_[None],
                fw_.reshape(D))

    grads_out = assemble(gs, G_win, G_wout, G_w1, G_w2)
    deltas = assemble(ds, d_win, d_wout, d_w1, d_w2)
    new_m = assemble(nms, nm_win, nm_wout, nm_w1, nm_w2)
    new_v = assemble(nvs, nv_win, nv_wout, nv_w1, nv_w2)
    return (loss, grad_x[None], *grads_out, *deltas, *new_m, *new_v)
```

```python
import functools

import jax
import jax.numpy as jnp
from jax import lax
from jax.experimental import pallas as pl
from jax.experimental.pallas import tpu as pltpu

F32 = jnp.float32
BF16 = jnp.bfloat16

D = 2048
SEQ = 2048
N_META = 16
CH = 64
TP = 2176
NCH = TP // CH
ROW_LO = 112
X_LO = 128
ROW_HI = TP
XT = 128
NXT = TP // XT
HEADS = 4
DK = 128
DV = 256
KW = HEADS * DK
GW = HEADS * DV
RANK = 16
TAU = 16.0
WINDOWS = (2, 4, 8, 16)
PW = 1024
GC = 256
DFF = 8192
EPS = 1e-6
SHARD_IN = 1028
PAD_IN = 1152
N_CHIP = 4

LR = 0.001
B1 = 0.9
B2 = 0.999
AEPS = 1e-08
WD = 0.01
STEP = 10

VMEM_LIMIT = 60 * 1024 * 1024
ANY = pl.BlockSpec(memory_space=pl.ANY)
VMEM_FULL = pl.BlockSpec(memory_space=pltpu.VMEM)
MESH = pl.DeviceIdType.MESH


def _cp(sem=None):
    if sem is None:
        return pltpu.CompilerParams(vmem_limit_bytes=VMEM_LIMIT)
    return pltpu.CompilerParams(dimension_semantics=sem, vmem_limit_bytes=VMEM_LIMIT)


def _dot(a, b):
    return jnp.dot(a, b, preferred_element_type=F32)


def _dot_nt(a, b):
    return lax.dot_general(a, b, (((1,), (1,)), ((), ())), preferred_element_type=F32)


def _dot_tn(a, b):
    return lax.dot_general(a, b, (((0,), (0,)), ((), ())), preferred_element_type=F32)


def _sds(shape, dtype):
    return jax.ShapeDtypeStruct(shape, dtype)


def _embed_norm(x, meta_full, w, dep=None):
    def body(x_ref, meta_ref, w_ref, dep_ref, h_ref, u_ref):
        i = pl.program_id(0)

        @pl.when(i == 0)
        def _():
            h_ref[...] = jnp.zeros_like(h_ref)
            h_ref[ROW_LO:X_LO, :] = meta_ref[...]

        @pl.when(i >= 1)
        def _():
            h_ref[...] = x_ref[...]

        h = h_ref[...]
        r = lax.rsqrt(jnp.mean(h * h, axis=-1, keepdims=True) + EPS)
        u_ref[...] = ((h * r) * w_ref[...]).astype(BF16)

    return pl.pallas_call(
        body, name="embed_norm1", grid=(NXT,),
        in_specs=[pl.BlockSpec((XT, D), lambda i: (jnp.maximum(i - 1, 0), 0)),
                  pl.BlockSpec((N_META, D), lambda i: (0, 0)),
                  pl.BlockSpec((1, D), lambda i: (0, 0)), ANY],
        out_specs=[pl.BlockSpec((XT, D), lambda i: (i, 0)), pl.BlockSpec((XT, D), lambda i: (i, 0))],
        out_shape=[_sds((TP, D), F32), _sds((TP, D), BF16)],
        compiler_params=_cp(("arbitrary",)),
    )(x, meta_full, w, _dep(dep))


def _norm_rows(h, w, name):
    tr = 272

    def body(h_ref, w_ref, o_ref):
        hv = h_ref[...]
        r = lax.rsqrt(jnp.mean(hv * hv, axis=-1, keepdims=True) + EPS)
        o_ref[...] = ((hv * r) * w_ref[...]).astype(BF16)

    return pl.pallas_call(
        body, name=name, grid=(TP // tr,),
        in_specs=[pl.BlockSpec((tr, D), lambda i: (i, 0)), pl.BlockSpec((1, D), lambda i: (0, 0))],
        out_specs=pl.BlockSpec((tr, D), lambda i: (i, 0)),
        out_shape=_sds((TP, D), BF16),
        compiler_params=_cp(("arbitrary",)),
    )(h, w)


def _loss_head(h2, target, fw):
    def body(h_ref, t_ref, w_ref, dh_ref, dhb_ref, dw_ref, loss_ref):
        i = pl.program_id(0)

        @pl.when(i == 0)
        def _():
            dw_ref[...] = jnp.zeros_like(dw_ref)
            loss_ref[...] = jnp.zeros_like(loss_ref)

        h = h_ref[...]
        w = w_ref[...]
        r = lax.rsqrt(jnp.mean(h * h, axis=-1, keepdims=True) + EPS)
        xh = h * r
        y = xh * w
        is_x = (i >= 1).astype(F32)
        diff = (y - t_ref[...]) * is_x
        loss_ref[...] += jnp.sum(diff * diff) * (0.5 / D)
        dy = diff * (1.0 / D)
        dw_ref[...] += jnp.sum(dy * xh, axis=0, keepdims=True)
        gx = dy * w
        dh = r * (gx - xh * jnp.mean(gx * xh, axis=-1, keepdims=True))
        dh_ref[...] = dh
        dhb_ref[...] = dh.astype(BF16)

    return pl.pallas_call(
        body, name="loss_head", grid=(NXT,),
        in_specs=[pl.BlockSpec((XT, D), lambda i: (i, 0)),
                  pl.BlockSpec((XT, D), lambda i: (jnp.maximum(i - 1, 0), 0)),
                  pl.BlockSpec((1, D), lambda i: (0, 0))],
        out_specs=[pl.BlockSpec((XT, D), lambda i: (i, 0)), pl.BlockSpec((XT, D), lambda i: (i, 0)),
                   pl.BlockSpec((1, D), lambda i: (0, 0)), pl.BlockSpec((8, 128), lambda i: (0, 0))],
        out_shape=[_sds((TP, D), F32), _sds((TP, D), BF16), _sds((1, D), F32), _sds((8, 128), F32)],
        compiler_params=_cp(("arbitrary",)),
    )(h2, target, fw)


def _norm_bwd(dn, h, dres, w, name, dep=None):
    tr = 272

    def body(dn_ref, h_ref, dres_ref, w_ref, dep_ref, o_ref, ob_ref, dw_ref):
        @pl.when(pl.program_id(0) == 0)
        def _():
            dw_ref[...] = jnp.zeros_like(dw_ref)

        hv = h_ref[...]
        dnv = dn_ref[...]
        r = lax.rsqrt(jnp.mean(hv * hv, axis=-1, keepdims=True) + EPS)
        xh = hv * r
        dw_ref[...] += jnp.sum(dnv * xh, axis=0, keepdims=True)
        gx = dnv * w_ref[...]
        dh = dres_ref[...] + r * (gx - xh * jnp.mean(gx * xh, axis=-1, keepdims=True))
        o_ref[...] = dh
        ob_ref[...] = dh.astype(BF16)

    row = pl.BlockSpec((tr, D), lambda i: (i, 0))
    vec = pl.BlockSpec((1, D), lambda i: (0, 0))
    return pl.pallas_call(
        body, name=name, grid=(TP // tr,),
        in_specs=[row, row, row, vec, ANY], out_specs=[row, row, vec],
        out_shape=[_sds((TP, D), F32), _sds((TP, D), BF16), _sds((1, D), F32)],
        compiler_params=_cp(("arbitrary",)),
    )(dn, h, dres, w, _dep(dep))


def _input_grad(du, h0, dh1, w, dep=None):
    def body(du_ref, h_ref, dres_ref, w_ref, dep_ref, gx_ref, gm_ref, dw_ref):
        i = pl.program_id(0)

        @pl.when(i == 0)
        def _():
            dw_ref[...] = jnp.zeros_like(dw_ref)

        hv = h_ref[...]
        dnv = du_ref[...]
        r = lax.rsqrt(jnp.mean(hv * hv, axis=-1, keepdims=True) + EPS)
        xh = hv * r
        dw_ref[...] += jnp.sum(dnv * xh, axis=0, keepdims=True)
        g = dnv * w_ref[...]
        dh = dres_ref[...] + r * (g - xh * jnp.mean(g * xh, axis=-1, keepdims=True))

        @pl.when(i == 0)
        def _():
            gm_ref[...] = dh[ROW_LO:X_LO, :]

        @pl.when(i >= 1)
        def _():
            gx_ref[...] = dh

    row = pl.BlockSpec((XT, D), lambda i: (i, 0))
    vec = pl.BlockSpec((1, D), lambda i: (0, 0))
    return pl.pallas_call(
        body, name="input_grad", grid=(NXT,),
        in_specs=[row, row, row, vec, ANY],
        out_specs=[pl.BlockSpec((XT, D), lambda i: (jnp.maximum(i - 1, 0), 0)),
                   pl.BlockSpec((N_META, D), lambda i: (0, 0)), vec],
        out_shape=[_sds((SEQ, D), F32), _sds((N_META, D), F32), _sds((1, D), F32)],
        compiler_params=_cp(("arbitrary",)),
    )(du, h0, dh1, w, _dep(dep))


def _in_proj(u, wg):
    def body(u_ref, w_ref, o_ref):
        o_ref[0] = _dot(u_ref[...], w_ref[0])

    return pl.pallas_call(
        body, name="in_proj", grid=(N_CHIP,),
        in_specs=[VMEM_FULL, pl.BlockSpec((1, D, PAD_IN), lambda k: (k, 0, 0))],
        out_specs=pl.BlockSpec((1, TP, PAD_IN), lambda k: (k, 0, 0)),
        out_shape=_sds((N_CHIP, TP, PAD_IN), F32),
        compiler_params=_cp(("arbitrary",)),
    )(u, wg)


def _out_proj(og, op, wout, h0):
    tn = 512

    def body(og_ref, op_ref, w_ref, h_ref, o_ref):
        acc = _dot(og_ref[...], w_ref[0:GW, :]) + _dot(op_ref[...], w_ref[GW:D, :])
        o_ref[...] = h_ref[...] + acc

    return pl.pallas_call(
        body, name="out_proj", grid=(D // tn,),
        in_specs=[VMEM_FULL, VMEM_FULL, pl.BlockSpec((D, tn), lambda j: (0, j)),
                  pl.BlockSpec((TP, tn), lambda j: (0, j))],
        out_specs=pl.BlockSpec((TP, tn), lambda j: (0, j)),
        out_shape=_sds((TP, D), F32),
        compiler_params=_cp(("arbitrary",)),
    )(og, op, wout, h0)


def _mlp_up(n2, w1g):
    tn = 512
    per = D // tn

    def body(n_ref, w_ref, zr_ref, a_ref):
        z = jnp.maximum(_dot(n_ref[...], w_ref[0]), 0.0)
        zr_ref[...] = z.astype(BF16)
        a_ref[...] = (z * z).astype(BF16)

    col = pl.BlockSpec((TP, tn), lambda k, j: (0, k * per + j))
    return pl.pallas_call(
        body, name="mlp_up", grid=(N_CHIP, per),
        in_specs=[VMEM_FULL, pl.BlockSpec((1, D, tn), lambda k, j: (k, 0, j))],
        out_specs=[col, col],
        out_shape=[_sds((TP, DFF), BF16), _sds((TP, DFF), BF16)],
        compiler_params=_cp(("arbitrary", "arbitrary")),
    )(n2, w1g)


def _mlp_down(a, w2, h1):
    tk = 512
    nk = DFF // tk

    def body(a_ref, w_ref, h_ref, o_ref, acc_ref):
        k = pl.program_id(0)

        @pl.when(k == 0)
        def _():
            pltpu.sync_copy(h_ref, acc_ref)

        acc_ref[...] += _dot(a_ref[...], w_ref[...])

        @pl.when(k == nk - 1)
        def _():
            pltpu.sync_copy(acc_ref, o_ref)

    return pl.pallas_call(
        body, name="mlp_down", grid=(nk,),
        in_specs=[pl.BlockSpec((TP, tk), lambda k: (0, k)), pl.BlockSpec((tk, D), lambda k: (k, 0)), ANY],
        out_specs=ANY,
        out_shape=_sds((TP, D), F32),
        scratch_shapes=[pltpu.VMEM((TP, D), F32)],
        compiler_params=_cp(("arbitrary",)),
    )(a, w2, h1)


def _mlp_dz(dh2b, w2, zr, dep=None):
    tn = 512

    def body(d_ref, w_ref, z_ref, dep_ref, o_ref):
        da = _dot_nt(d_ref[...], w_ref[...])
        o_ref[...] = (da * (2.0 * z_ref[...].astype(F32))).astype(BF16)

    col = pl.BlockSpec((TP, tn), lambda j: (0, j))
    return pl.pallas_call(
        body, name="mlp_dz", grid=(DFF // tn,),
        in_specs=[VMEM_FULL, pl.BlockSpec((tn, D), lambda j: (j, 0)), col, ANY],
        out_specs=col,
        out_shape=_sds((TP, DFF), BF16),
        compiler_params=_cp(("arbitrary",)),
    )(dh2b, w2, zr, _dep(dep))


def _grad_w2(a, dh2b):
    tm = 512

    def body(a_ref, d_ref, o_ref):
        o_ref[...] = _dot_tn(a_ref[...], d_ref[...])

    return pl.pallas_call(
        body, name="grad_w2", grid=(DFF // tm,),
        in_specs=[pl.BlockSpec((TP, tm), lambda j: (0, j)), VMEM_FULL],
        out_specs=pl.BlockSpec((tm, D), lambda j: (j, 0)),
        out_shape=_sds((DFF, D), F32),
        compiler_params=_cp(("arbitrary",)),
    )(a, dh2b)


def _dep(token):
    return jnp.zeros((8, 128), F32) if token is None else token


def _grad_w1(n2, dz, dep=None):
    tn = 512
    per = D // tn

    def body(n_ref, d_ref, dep_ref, o_ref):
        o_ref[0] = _dot_tn(n_ref[...], d_ref[...])

    return pl.pallas_call(
        body, name="grad_w1", grid=(N_CHIP, per),
        in_specs=[VMEM_FULL, pl.BlockSpec((TP, tn), lambda k, j: (0, k * per + j)), ANY],
        out_specs=pl.BlockSpec((1, D, tn), lambda k, j: (k, 0, j)),
        out_shape=_sds((N_CHIP, D, D), F32),
        compiler_params=_cp(("arbitrary", "arbitrary")),
    )(n2, dz, _dep(dep))


def _mlp_dn(dz, w1g, dep=None):
    tk = 512
    per = D // tk
    nk = DFF // tk

    def body(d_ref, w_ref, dep_ref, o_ref, acc_ref):
        k = pl.program_id(0)
        part = _dot_nt(d_ref[...], w_ref[0])

        @pl.when(k == 0)
        def _():
            acc_ref[...] = part

        @pl.when(k > 0)
        def _():
            acc_ref[...] += part

        @pl.when(k == nk - 1)
        def _():
            pltpu.sync_copy(acc_ref, o_ref)

    return pl.pallas_call(
        body, name="mlp_dn", grid=(nk,),
        in_specs=[pl.BlockSpec((TP, tk), lambda k: (0, k)),
                  pl.BlockSpec((1, D, tk), lambda k: (k // per, 0, k % per)), ANY],
        out_specs=ANY,
        out_shape=_sds((TP, D), F32),
        scratch_shapes=[pltpu.VMEM((TP, D), F32)],
        compiler_params=_cp(("arbitrary",)),
    )(dz, w1g, _dep(dep))


def _mixed_grad(dh1b, wout):
    tn = 512

    def body(d_ref, w_ref, o_ref):
        o_ref[...] = _dot_nt(d_ref[...], w_ref[...])

    return pl.pallas_call(
        body, name="mixed_grad", grid=(D // tn,),
        in_specs=[VMEM_FULL, pl.BlockSpec((tn, D), lambda j: (j, 0))],
        out_specs=pl.BlockSpec((TP, tn), lambda j: (0, j)),
        out_shape=_sds((TP, D), F32),
        compiler_params=_cp(("arbitrary",)),
    )(dh1b, wout)


def _grad_wout(og, op, dh1b):
    tm = 512

    def body(og_ref, op_ref, d_ref, o_ref):
        j = pl.program_id(0)

        @pl.when(j < 2)
        def _():
            o_ref[0] = _dot_tn(og_ref[...], d_ref[...])

        @pl.when(j >= 2)
        def _():
            o_ref[0] = _dot_tn(op_ref[...], d_ref[...])

    return pl.pallas_call(
        body, name="grad_wout", grid=(N_CHIP,),
        in_specs=[pl.BlockSpec((TP, tm), lambda j: (0, jnp.minimum(j, 1))),
                  pl.BlockSpec((TP, tm), lambda j: (0, jnp.maximum(j - 2, 0))), VMEM_FULL],
        out_specs=pl.BlockSpec((1, tm, D), lambda j: (j, 0, 0)),
        out_shape=_sds((N_CHIP, tm, D), F32),
        compiler_params=_cp(("arbitrary",)),
    )(og, op, dh1b)


def _in_grad(dp, wg, dep=None):
    def body(d_ref, w_ref, dep_ref, o_ref, acc_ref):
        k = pl.program_id(0)
        part = _dot_nt(d_ref[0], w_ref[0])

        @pl.when(k == 0)
        def _():
            acc_ref[...] = part

        @pl.when(k > 0)
        def _():
            acc_ref[...] += part

        @pl.when(k == N_CHIP - 1)
        def _():
            pltpu.sync_copy(acc_ref, o_ref)

    return pl.pallas_call(
        body, name="in_grad", grid=(N_CHIP,),
        in_specs=[pl.BlockSpec((1, TP, PAD_IN), lambda k: (k, 0, 0)),
                  pl.BlockSpec((1, D, PAD_IN), lambda k: (k, 0, 0)), ANY],
        out_specs=ANY,
        out_shape=_sds((TP, D), F32),
        scratch_shapes=[pltpu.VMEM((TP, D), F32)],
        compiler_params=_cp(("arbitrary",)),
    )(dp, wg, _dep(dep))


def _grad_win(u, dp, dep=None):
    def body(u_ref, d_ref, dep_ref, o_ref):
        g = _dot_tn(u_ref[...], d_ref[0])
        o_ref[0] = g[:, 0:SHARD_IN]

    return pl.pallas_call(
        body, name="grad_win", grid=(N_CHIP,),
        in_specs=[VMEM_FULL, pl.BlockSpec((1, TP, PAD_IN), lambda k: (k, 0, 0)), ANY],
        out_specs=pl.BlockSpec((1, D, SHARD_IN), lambda k: (k, 0, 0)),
        out_shape=_sds((N_CHIP, D, SHARD_IN), F32),
        compiler_params=_cp(("arbitrary",)),
    )(u, dp, _dep(dep))


def _split3(x):
    hi = x.astype(BF16)
    r1 = x - hi.astype(F32)
    mid = r1.astype(BF16)
    lo = (r1 - mid.astype(F32)).astype(BF16)
    return hi, mid, lo


def _tri_sum(tri, x):
    hi, mid, lo = _split3(x)
    return _dot(tri, hi) + _dot(tri, mid) + _dot(tri, lo)


def _gla_common(n, glr, gw2, gb):
    rows = n * CH + lax.broadcasted_iota(jnp.int32, (CH, 1), 0)
    valid = (rows >= ROW_LO) & (rows < ROW_HI)
    g_raw = _dot(glr.astype(BF16), gw2.astype(BF16)) + gb
    logsig = jnp.minimum(g_raw, 0.0) - jnp.log(1.0 + jnp.exp(-jnp.abs(g_raw)))
    logg = jnp.where(valid, logsig * (1.0 / TAU), 0.0)
    ci = lax.broadcasted_iota(jnp.int32, (CH, CH), 0)
    si = lax.broadcasted_iota(jnp.int32, (CH, CH), 1)
    lower = ci >= si
    G = _tri_sum(lower.astype(BF16), logg)
    Gl = G[CH - 1:CH, :]
    return valid, g_raw, lower, G, Gl


def _gla_fwd(q, k, v, r, glr, gw2, gb, gnw):
    scale = DK ** -0.5

    def body(q_ref, k_ref, v_ref, r_ref, glr_ref, gw2_ref, gb_ref, gnw_ref, o_ref, og_ref, sp_ref, st_ref):
        n = pl.program_id(0)

        @pl.when(n == 0)
        def _():
            st_ref[...] = jnp.zeros_like(st_ref)

        _, _, lower, G, Gl = _gla_common(n, glr_ref[...], gw2_ref[...], gb_ref[...])
        eG = jnp.exp(G)
        eN = jnp.exp(-G)
        eE = jnp.exp(Gl - G)
        dec = jnp.exp(Gl)
        gnw_v = gnw_ref[...]
        for h in range(HEADS):
            ks = slice(h * DK, (h + 1) * DK)
            vs = slice(h * DV, (h + 1) * DV)
            kh = k_ref[:, ks]
            vh = v_ref[:, vs].astype(BF16)
            qd = ((q_ref[:, ks] * scale) * eG[:, ks]).astype(BF16)
            ki = (kh * eN[:, ks]).astype(BF16)
            ke = (kh * eE[:, ks]).astype(BF16)
            st = st_ref[h]
            a = jnp.where(lower, _dot_nt(qd, ki), 0.0).astype(BF16)
            o = _dot(a, vh) + _dot_nt(qd, st.astype(BF16))
            sp_ref[0, h] = st
            st_ref[h] = st * dec[:, ks] + _dot_tn(vh, ke)
            o_ref[:, vs] = o
            rs = lax.rsqrt(jnp.mean(o * o, axis=-1, keepdims=True) + EPS)
            rv = r_ref[:, vs]
            gate = rv / (1.0 + jnp.exp(-rv))
            og_ref[:, vs] = (((o * rs) * gnw_v) * gate).astype(BF16)

    rk = pl.BlockSpec((CH, KW), lambda n: (n, 0))
    rv_ = pl.BlockSpec((CH, GW), lambda n: (n, 0))

    def full(shape):
        return pl.BlockSpec(shape, lambda n: tuple(0 for _ in shape))

    return pl.pallas_call(
        body, name="gla_fwd", grid=(NCH,),
        in_specs=[rk, rk, rv_, rv_, pl.BlockSpec((CH, RANK), lambda n: (n, 0)),
                  full((RANK, KW)), full((1, KW)), full((1, DV))],
        out_specs=[rv_, rv_, pl.BlockSpec((1, HEADS, DV, DK), lambda n: (n, 0, 0, 0))],
        out_shape=[_sds((TP, GW), F32), _sds((TP, GW), BF16), _sds((NCH, HEADS, DV, DK), F32)],
        scratch_shapes=[pltpu.VMEM((HEADS, DV, DK), F32)],
        compiler_params=_cp(("arbitrary",)),
    )(q, k, v, r, glr, gw2, gb, gnw)


def _gla_bwd(dog, o, q, k, v, r, glr, gw2, gb, gnw, sp, dep=None):
    scale = DK ** -0.5

    def body(dog_ref, o_ref, q_ref, k_ref, v_ref, r_ref, glr_ref, gw2_ref, gb_ref, gnw_ref, sp_ref, dep_ref,
             dq_ref, dk_ref, dv_ref, dr_ref, dglr_ref, dgw2_ref, dgb_ref, dgnw_ref, ds_ref):
        step = pl.program_id(0)
        n = NCH - 1 - step

        @pl.when(step == 0)
        def _():
            ds_ref[...] = jnp.zeros_like(ds_ref)
            dgw2_ref[...] = jnp.zeros_like(dgw2_ref)
            dgb_ref[...] = jnp.zeros_like(dgb_ref)
            dgnw_ref[...] = jnp.zeros_like(dgnw_ref)

        glr_v = glr_ref[...]
        gw2_b = gw2_ref[...].astype(BF16)
        valid, g_raw, lower, G, Gl = _gla_common(n, glr_v, gw2_ref[...], gb_ref[...])
        upper = lax.broadcasted_iota(jnp.int32, (CH, CH), 0) <= lax.broadcasted_iota(jnp.int32, (CH, CH), 1)
        eG = jnp.exp(G)
        eN = jnp.exp(-G)
        eE = jnp.exp(Gl - G)
        dec = jnp.exp(Gl)
        gnw_v = gnw_ref[...]
        last = lax.broadcasted_iota(jnp.int32, (CH, 1), 0) == CH - 1
        dgnw_acc = jnp.zeros((1, DV), F32)
        dG_parts = []
        for h in range(HEADS):
            ks = slice(h * DK, (h + 1) * DK)
            vs = slice(h * DV, (h + 1) * DV)
            oh = o_ref[:, vs]
            rv = r_ref[:, vs]
            dg = dog_ref[:, vs]
            sig = 1.0 / (1.0 + jnp.exp(-rv))
            gate = rv * sig
            rs = lax.rsqrt(jnp.mean(oh * oh, axis=-1, keepdims=True) + EPS)
            ohat = oh * rs
            dr_ref[:, vs] = (dg * (ohat * gnw_v)) * (sig * (1.0 + rv * (1.0 - sig)))
            don = dg * gate
            dgnw_acc = dgnw_acc + jnp.sum(don * ohat, axis=0, keepdims=True)
            gxn = don * gnw_v
            do = (rs * (gxn - ohat * jnp.mean(gxn * ohat, axis=-1, keepdims=True))).astype(BF16)
            kh = k_ref[:, ks]
            vh = v_ref[:, vs].astype(BF16)
            qd_f = (q_ref[:, ks] * scale) * eG[:, ks]
            ki_f = kh * eN[:, ks]
            ke_f = kh * eE[:, ks]
            qd, ki, ke = qd_f.astype(BF16), ki_f.astype(BF16), ke_f.astype(BF16)
            spt = sp_ref[0, h]
            dst = ds_ref[h]
            dst_b = dst.astype(BF16)
            a_t = jnp.where(upper, _dot_nt(ki, qd), 0.0).astype(BF16)
            da = jnp.where(lower, _dot_nt(do, vh), 0.0).astype(BF16)
            da_t = jnp.where(upper, _dot_nt(vh, do), 0.0).astype(BF16)
            dv_ref[:, vs] = _dot(a_t, do) + _dot_nt(ke, dst_b)
            dqd = _dot(da, ki) + _dot(do, spt.astype(BF16))
            dki = _dot(da_t, qd)
            dke = _dot(vh, dst_b)
            ddec = jnp.sum(spt * dst, axis=0, keepdims=True)
            ds_ref[h] = dst * dec[:, ks] + _dot_tn(do, qd)
            dq_ref[:, ks] = (dqd * eG[:, ks]) * scale
            dk_ref[:, ks] = dki * eN[:, ks] + dke * eE[:, ks]
            dke_ke = dke * ke_f
            dG = dqd * qd_f - dki * ki_f - dke_ke
            dGl = jnp.sum(dke_ke, axis=0, keepdims=True) + ddec * dec[:, ks]
            dG_parts.append(dG + jnp.where(last, dGl, 0.0))
        dgnw_ref[...] += dgnw_acc
        dG_all = jnp.concatenate(dG_parts, axis=1)
        dlogg = jnp.where(valid, _tri_sum(upper.astype(BF16), dG_all), 0.0)
        dg_raw = (dlogg * (1.0 / TAU)) * (1.0 / (1.0 + jnp.exp(g_raw)))
        dgb_ref[...] += jnp.sum(dg_raw, axis=0, keepdims=True)
        dg_b = dg_raw.astype(BF16)
        dgw2_ref[...] += _dot_tn(glr_v.astype(BF16), dg_b)
        dglr_ref[...] = _dot_nt(dg_b, gw2_b)

    rk = pl.BlockSpec((CH, KW), lambda s: (NCH - 1 - s, 0))
    rv_ = pl.BlockSpec((CH, GW), lambda s: (NCH - 1 - s, 0))
    rg = pl.BlockSpec((CH, RANK), lambda s: (NCH - 1 - s, 0))

    def full(shape):
        return pl.BlockSpec(shape, lambda s: tuple(0 for _ in shape))

    return pl.pallas_call(
        body, name="gla_bwd", grid=(NCH,),
        in_specs=[rv_, rv_, rk, rk, rv_, rv_, rg, full((RANK, KW)), full((1, KW)), full((1, DV)),
                  pl.BlockSpec((1, HEADS, DV, DK), lambda s: (NCH - 1 - s, 0, 0, 0)), ANY],
        out_specs=[rk, rk, rv_, rv_, rg, full((RANK, KW)), full((1, KW)), full((1, DV))],
        out_shape=[_sds((TP, KW), F32), _sds((TP, KW), F32), _sds((TP, GW), F32), _sds((TP, GW), F32),
                   _sds((TP, RANK), F32), _sds((RANK, KW), F32), _sds((1, KW), F32), _sds((1, DV), F32)],
        scratch_shapes=[pltpu.VMEM((HEADS, DV, DK), F32)],
        compiler_params=_cp(("arbitrary",)),
    )(dog, o, q, k, v, r, glr, gw2, gb, gnw, sp, _dep(dep))


POOL_TR = 128
HALO = 16


def _pool_counts(base, nrows):
    rows = base + lax.broadcasted_iota(jnp.int32, (nrows, 1), 0)
    valid = (rows >= ROW_LO) & (rows < ROW_HI)
    t1 = (rows - ROW_LO + 1).astype(F32)
    cnts = [jnp.clip(t1, 1.0, float(w)) for w in WINDOWS]
    return valid, cnts


def _pool_fwd(pu, pw, ps):
    def body(cur_ref, prev_ref, pw_ref, ps_ref, y_ref, op_ref):
        i = pl.program_id(0)
        cur = cur_ref[...]
        full = jnp.concatenate([prev_ref[...], cur], axis=0)
        s2 = full + pltpu.roll(full, 1, 0)
        s4 = s2 + pltpu.roll(s2, 2, 0)
        s8 = s4 + pltpu.roll(s4, 4, 0)
        s16 = s8 + pltpu.roll(s8, 8, 0)
        valid, cnts = _pool_counts(i * POOL_TR, POOL_TR)
        for g, s in enumerate((s2, s4, s8, s16)):
            cs = slice(g * GC, (g + 1) * GC)
            y = s[HALO:, cs] / cnts[g] - cur[:, cs]
            yb = jnp.where(valid, y, 0.0).astype(BF16)
            y_ref[:, cs] = yb
            op_ref[:, cs] = (_dot(yb, pw_ref[g].astype(BF16)) * ps_ref[:, cs]).astype(BF16)

    row = pl.BlockSpec((POOL_TR, PW), lambda i: (i, 0))
    per = POOL_TR // HALO
    return pl.pallas_call(
        body, name="pool_fwd", grid=(TP // POOL_TR,),
        in_specs=[row, pl.BlockSpec((HALO, PW), lambda i: (jnp.maximum(i * per - 1, 0), 0)),
                  pl.BlockSpec((4, GC, GC), lambda i: (0, 0, 0)), pl.BlockSpec((1, PW), lambda i: (0, 0))],
        out_specs=[row, row],
        out_shape=[_sds((TP, PW), BF16), _sds((TP, PW), BF16)],
        compiler_params=_cp(("arbitrary",)),
    )(pu, pu, pw, ps)


def _pool_bwd(dop, y, pw, ps, dep=None):
    nblk = TP // HALO

    def body(cur_ref, nxt_ref, y_ref, pw_ref, ps_ref, dep_ref, dpu_ref, dpw_ref, dps_ref):
        i = pl.program_id(0)

        @pl.when(i == 0)
        def _():
            dpw_ref[...] = jnp.zeros_like(dpw_ref)
            dps_ref[...] = jnp.zeros_like(dps_ref)

        n_all = POOL_TR + HALO
        dcur = cur_ref[...]
        dall = jnp.concatenate([dcur, nxt_ref[...]], axis=0)
        valid, cnts = _pool_counts(i * POOL_TR, n_all)
        for g in range(4):
            cs = slice(g * GC, (g + 1) * GC)
            pwb = pw_ref[g].astype(BF16)
            yb = y_ref[:, cs]
            dyw = (dall[:, cs] * ps_ref[:, cs]).astype(BF16)
            dps_ref[:, cs] += jnp.sum(dcur[:, cs] * _dot(yb, pwb), axis=0, keepdims=True)
            dpw_ref[g] += _dot_tn(yb, dyw[0:POOL_TR, :])
            dyv = jnp.where(valid, _dot_nt(dyw, pwb), 0.0)
            e = dyv / cnts[g]
            w = WINDOWS[g]
            sh = 1
            while sh < w:
                e = e + pltpu.roll(e, n_all - sh, 0)
                sh *= 2
            dpu_ref[:, cs] = e[0:POOL_TR, :] - dyv[0:POOL_TR, :]

    row = pl.BlockSpec((POOL_TR, PW), lambda i: (i, 0))
    per = POOL_TR // HALO
    return pl.pallas_call(
        body, name="pool_bwd", grid=(TP // POOL_TR,),
        in_specs=[pl.BlockSpec((POOL_TR, PW), lambda i: (i, 1)),
                  pl.BlockSpec((HALO, PW), lambda i: (jnp.minimum(i * per + per, nblk - 1), 1)),
                  row, pl.BlockSpec((4, GC, GC), lambda i: (0, 0, 0)), pl.BlockSpec((1, PW), lambda i: (0, 0)), ANY],
        out_specs=[row, pl.BlockSpec((4, GC, GC), lambda i: (0, 0, 0)), pl.BlockSpec((1, PW), lambda i: (0, 0))],
        out_shape=[_sds((TP, PW), F32), _sds((4, GC, GC), F32), _sds((1, PW), F32)],
        compiler_params=_cp(("arbitrary",)),
    )(dop, dop, y, pw, ps, _dep(dep))


def _place():
    x, y, c = lax.axis_index("x"), lax.axis_index("y"), lax.axis_index("c")
    chips = [(1 - x, y), (x, 1 - y), (1 - x, 1 - y)]
    return x, y, c, chips


HBM = pl.BlockSpec(memory_space=pltpu.HBM)
SEM = pl.BlockSpec(memory_space=pltpu.SEMAPHORE)
EFFECT = pltpu.SideEffectType.DATAFLOW_SIDE_EFFECTING


def _cast_into(w, place, cols_out, name, dep=None):
    rows, cols = w.shape
    tr = 256

    def body(p_ref, w_ref, dep_ref, o_ref):
        if cols_out != cols:
            o_ref[0] = jnp.zeros((tr, cols_out), BF16)
            o_ref[0, :, 0:cols] = w_ref[...].astype(BF16)
        else:
            o_ref[0] = w_ref[...].astype(BF16)

    grid_spec = pltpu.PrefetchScalarGridSpec(
        num_scalar_prefetch=1, grid=(rows // tr,),
        in_specs=[pl.BlockSpec((tr, cols), lambda i, p: (i, 0)), ANY],
        out_specs=pl.BlockSpec((1, tr, cols_out), lambda i, p: (p[0], i, 0)))
    return pl.pallas_call(
        body, name=name, grid_spec=grid_spec,
        out_shape=_sds((N_CHIP, rows, cols_out), BF16),
        compiler_params=_cp(("arbitrary",)),
    )(place, w, _dep(dep))


def _half_rows(ref, k, which):
    h = ref.shape[1] // 2
    return ref.at[k, pl.ds(pl.multiple_of(which * h, 8), h), :]


def _gather_small(meta, gw2, pw, dep=None):
    def body(meta_r, gw2_r, pw_r, dep_ref, metaF, gw2F, pwF, lsem, ssem, rsem):
        x, y, c, chips = _place()
        me = 2 * x + y

        def slots(k):
            return (metaF.at[:, pl.ds(pl.multiple_of(k * 512, 128), 512)],
                    gw2F.at[:, pl.ds(pl.multiple_of(k * 128, 128), 128)],
                    pwF.at[:, pl.ds(pl.multiple_of(k * 64, 8), 64), :])

        srcs = (meta_r, gw2_r, pw_r)
        local = [pltpu.make_async_copy(s, d, lsem.at[i]) for i, (s, d) in enumerate(zip(srcs, slots(me)))]
        sends = []
        for j, chip in enumerate(chips):
            for i, (s, d) in enumerate(zip(srcs, slots(me))):
                sends.append(pltpu.make_async_remote_copy(src_ref=s, dst_ref=d, send_sem=ssem.at[3 * j + i],
                                                          recv_sem=rsem.at[3 * j + i], device_id=(*chip, c),
                                                          device_id_type=MESH))
        for cp in local + sends:
            cp.start()
        for j, (cx, cy) in enumerate(chips):
            for i, (s, d) in enumerate(zip(srcs, slots(2 * cx + cy))):
                pltpu.make_async_remote_copy(src_ref=s, dst_ref=d, send_sem=ssem.at[3 * j + i], recv_sem=rsem.at[3 * j + i],
                                             device_id=(cx, cy, c), device_id_type=MESH).wait_recv()
        for cp in sends:
            cp.wait_send()
        for cp in local:
            cp.wait()

    return pl.pallas_call(
        body, name="gather_small",
        in_specs=[ANY] * 4, out_specs=[ANY] * 3,
        out_shape=[_sds((N_META, D), F32), _sds((RANK, KW), F32), _sds((4, GC, GC), F32)],
        scratch_shapes=[pltpu.SemaphoreType.DMA((3,)), pltpu.SemaphoreType.DMA((9,)), pltpu.SemaphoreType.DMA((9,))],
    )(meta, gw2, pw, _dep(dep))


def _gather_start(ws, name):
    n = len(ws)

    def body(*refs):
        ins = refs[:n]
        ssems = refs[n:2 * n]
        rsems = refs[2 * n:3 * n]
        token = refs[4 * n]
        x, y, c, chips = _place()
        me = 2 * x + y
        for w in range(n):
            blk = _half_rows(ins[w], me, c)
            for j, chip in enumerate(chips):
                pltpu.make_async_remote_copy(src_ref=blk, dst_ref=blk, send_sem=ssems[w].at[j], recv_sem=rsems[w].at[j],
                                             device_id=(*chip, c), device_id_type=MESH).start()
        token[...] = jnp.zeros_like(token)

    sem3 = pltpu.SemaphoreType.DMA((3,))
    outs = pl.pallas_call(
        body, name=name,
        out_shape=tuple([sem3] * (2 * n) + [pltpu.HBM(w.shape, w.dtype) for w in ws] + [_sds((8, 128), F32)]),
        in_specs=(HBM,) * n, out_specs=(SEM,) * (2 * n) + (HBM,) * n + (VMEM_FULL,),
        input_output_aliases={w: 2 * n + w for w in range(n)},
        compiler_params=pltpu.CompilerParams(has_side_effects=EFFECT),
    )(*[pltpu.with_memory_space_constraint(w, pltpu.HBM) for w in ws])
    return outs[:n], outs[n:2 * n], outs[2 * n:3 * n], outs[3 * n]


def _gather_wait(w, ssem, rsem, after, name):
    def body(w_ref, ssem_ref, rsem_ref, after_ref, out_ref):
        x, y, c, chips = _place()
        me = 2 * x + y
        mine = _half_rows(w_ref, me, c)
        for j, (cx, cy) in enumerate(chips):
            cp = pltpu.make_async_remote_copy(src_ref=mine, dst_ref=_half_rows(w_ref, 2 * cx + cy, c),
                                              send_sem=ssem_ref.at[j], recv_sem=rsem_ref.at[j],
                                              device_id=(cx, cy, c), device_id_type=MESH)
            cp.wait_send()
            cp.wait_recv()

    return pl.pallas_call(
        body, name=name, out_shape=pltpu.HBM(w.shape, w.dtype),
        in_specs=(HBM, SEM, SEM, ANY), out_specs=HBM, input_output_aliases={0: 0},
        compiler_params=pltpu.CompilerParams(has_side_effects=EFFECT),
    )(w, ssem, rsem, after)


def _forward_halves(w, name):
    def body(w_ref, o_ref, ssem, rsem):
        x, y, c, chips = _place()
        sib = (x, y, 1 - c)
        cps = []
        for j, (cx, cy) in enumerate(chips):
            blk = _half_rows(o_ref, 2 * cx + cy, c)
            cps.append(pltpu.make_async_remote_copy(src_ref=blk, dst_ref=blk, send_sem=ssem.at[j], recv_sem=rsem.at[j],
                                                    device_id=sib, device_id_type=MESH))
        for cp in cps:
            cp.start()
        for j, (cx, cy) in enumerate(chips):
            blk = _half_rows(o_ref, 2 * cx + cy, 1 - c)
            pltpu.make_async_remote_copy(src_ref=blk, dst_ref=blk, send_sem=ssem.at[j], recv_sem=rsem.at[j],
                                         device_id=sib, device_id_type=MESH).wait_recv()
        for cp in cps:
            cp.wait_send()

    return pl.pallas_call(
        body, name=name, in_specs=[ANY], out_specs=ANY, out_shape=_sds(w.shape, w.dtype),
        input_output_aliases={0: 0},
        scratch_shapes=[pltpu.SemaphoreType.DMA((3,)), pltpu.SemaphoreType.DMA((3,))],
    )(w)


def _rs_start(sb, name):
    _, half, cols = sb.shape

    def body(sb_ref, land_ref, ssem, rsem, sb_out, land_out, token):
        x, y, c, chips = _place()
        for j, (cx, cy) in enumerate(chips):
            pltpu.make_async_remote_copy(src_ref=sb_ref.at[2 * cx + cy], dst_ref=land_ref.at[j], send_sem=ssem.at[j],
                                         recv_sem=rsem.at[j], device_id=(cx, cy, c), device_id_type=MESH).start()
        token[...] = jnp.zeros_like(token)

    sem3 = pltpu.SemaphoreType.DMA((3,))
    land = lax.empty((3, half, cols), BF16)
    return pl.pallas_call(
        body, name=name,
        out_shape=(sem3, sem3, pltpu.HBM(sb.shape, sb.dtype), pltpu.HBM(land.shape, land.dtype), _sds((8, 128), F32)),
        in_specs=(HBM, HBM), out_specs=(SEM, SEM, HBM, HBM, VMEM_FULL), input_output_aliases={0: 2, 1: 3},
        compiler_params=pltpu.CompilerParams(has_side_effects=EFFECT),
    )(pltpu.with_memory_space_constraint(sb, pltpu.HBM), pltpu.with_memory_space_constraint(land, pltpu.HBM))


def _rs_wait(sb, land, ssem, rsem, after, name):
    def body(sb_ref, land_ref, ssem_ref, rsem_ref, after_ref, sb_out, land_out):
        x, y, c, chips = _place()
        for j, (cx, cy) in enumerate(chips):
            cp = pltpu.make_async_remote_copy(src_ref=sb_ref.at[2 * cx + cy], dst_ref=land_ref.at[j], send_sem=ssem_ref.at[j],
                                              recv_sem=rsem_ref.at[j], device_id=(cx, cy, c), device_id_type=MESH)
            cp.wait_send()
            cp.wait_recv()

    return pl.pallas_call(
        body, name=name,
        out_shape=(pltpu.HBM(sb.shape, sb.dtype), pltpu.HBM(land.shape, land.dtype)),
        in_specs=(HBM, HBM, SEM, SEM, ANY), out_specs=(HBM, HBM), input_output_aliases={0: 0, 1: 1},
        compiler_params=pltpu.CompilerParams(has_side_effects=EFFECT),
    )(sb, land, ssem, rsem, after)[1]


def _pair_copy(g_ref, land_ref, ssem, rsem):
    x, y, c, _ = _place()
    h = g_ref.shape[1] // 2
    src = g_ref.at[:, pl.ds(pl.multiple_of((1 - c) * h, 8), h), :]
    return pltpu.make_async_remote_copy(src_ref=src, dst_ref=land_ref, send_sem=ssem.at[0], recv_sem=rsem.at[0],
                                        device_id=(x, y, 1 - c), device_id_type=MESH)


def _pair_start(g, name):
    def body(g_ref, land_ref, ssem, rsem, g_out, land_out, token):
        _pair_copy(g_ref, land_ref, ssem, rsem).start()
        token[...] = jnp.zeros_like(token)

    sem1 = pltpu.SemaphoreType.DMA((1,))
    land = lax.empty((N_CHIP, g.shape[1] // 2, g.shape[2]), F32)
    return pl.pallas_call(
        body, name=name,
        out_shape=(sem1, sem1, pltpu.HBM(g.shape, g.dtype), pltpu.HBM(land.shape, land.dtype), _sds((8, 128), F32)),
        in_specs=(HBM, HBM), out_specs=(SEM, SEM, HBM, HBM, VMEM_FULL), input_output_aliases={0: 2, 1: 3},
        compiler_params=pltpu.CompilerParams(has_side_effects=EFFECT),
    )(pltpu.with_memory_space_constraint(g, pltpu.HBM), pltpu.with_memory_space_constraint(land, pltpu.HBM))


def _pair_wait(g, land, ssem, rsem, after, name):
    def body(g_ref, land_ref, ssem_ref, rsem_ref, after_ref, g_out, land_out):
        cp = _pair_copy(g_ref, land_ref, ssem_ref, rsem_ref)
        cp.wait_send()
        cp.wait_recv()

    return pl.pallas_call(
        body, name=name,
        out_shape=(pltpu.HBM(g.shape, g.dtype), pltpu.HBM(land.shape, land.dtype)),
        in_specs=(HBM, HBM, SEM, SEM, ANY), out_specs=(HBM, HBM), input_output_aliases={0: 0, 1: 1},
        compiler_params=pltpu.CompilerParams(has_side_effects=EFFECT),
    )(g, land, ssem, rsem, after)


def _pair_sum(g, rcv, place, name):
    _, rows, cols = g.shape
    half = rows // 2
    tr = 256
    nt = half // tr

    def body(p_ref, g_ref, r_ref, sb_ref, sf_ref):
        s = pl.program_id(1)
        tot = g_ref[0] + r_ref[0]
        sb_ref[0] = tot.astype(BF16)

        @pl.when(s == p_ref[0])
        def _():
            sf_ref[...] = tot

    grid_spec = pltpu.PrefetchScalarGridSpec(
        num_scalar_prefetch=1, grid=(nt, N_CHIP),
        in_specs=[pl.BlockSpec((1, tr, cols), lambda t, s, p: (s, p[1] * nt + t, 0)),
                  pl.BlockSpec((1, tr, cols), lambda t, s, p: (s, t, 0))],
        out_specs=[pl.BlockSpec((1, tr, cols), lambda t, s, p: (s, t, 0)),
                   pl.BlockSpec((tr, cols), lambda t, s, p: (t, 0))])
    return pl.pallas_call(
        body, name=name, grid_spec=grid_spec,
        out_shape=[_sds((N_CHIP, half, cols), BF16), _sds((half, cols), F32)],
        compiler_params=_cp(("arbitrary", "arbitrary")),
    )(place, g, rcv)


def _final_sum(sf, rb, place, name):
    half, cols = sf.shape
    tr = 256
    nt = half // tr

    def body(p_ref, sf_ref, r_ref, out_ref):
        acc = sf_ref[...]
        for j in range(3):
            acc = acc + r_ref[j].astype(F32)
        out_ref[...] = acc

    grid_spec = pltpu.PrefetchScalarGridSpec(
        num_scalar_prefetch=1, grid=(nt,),
        in_specs=[pl.BlockSpec((tr, cols), lambda t, p: (t, 0)), pl.BlockSpec((3, tr, cols), lambda t, p: (0, t, 0))],
        out_specs=pl.BlockSpec((tr, cols), lambda t, p: (p[1] * nt + t, 0)))
    return pl.pallas_call(
        body, name=name, grid_spec=grid_spec,
        out_shape=_sds((2 * half, cols), F32),
        compiler_params=_cp(("arbitrary",)),
    )(place, sf, rb)


def _half_exchange(fulls, name):
    ng = len(fulls)

    def body(*refs):
        outs = refs[ng:2 * ng]
        ssem, rsem = refs[2 * ng], refs[2 * ng + 1]
        x, y, c, _ = _place()
        sib = (x, y, 1 - c)
        cps = []
        for w in range(ng):
            h = outs[w].shape[0] // 2
            mine = outs[w].at[pl.ds(pl.multiple_of(c * h, 8), h), :]
            cps.append(pltpu.make_async_remote_copy(src_ref=mine, dst_ref=mine, send_sem=ssem.at[w], recv_sem=rsem.at[w],
                                                    device_id=sib, device_id_type=MESH))
        for cp in cps:
            cp.start()
        for w in range(ng):
            h = outs[w].shape[0] // 2
            theirs = outs[w].at[pl.ds(pl.multiple_of((1 - c) * h, 8), h), :]
            pltpu.make_async_remote_copy(src_ref=theirs, dst_ref=theirs, send_sem=ssem.at[w], recv_sem=rsem.at[w],
                                         device_id=sib, device_id_type=MESH).wait_recv()
        for cp in cps:
            cp.wait_send()

    return pl.pallas_call(
        body, name=name,
        in_specs=[ANY] * ng, out_specs=[ANY] * ng, out_shape=[_sds(f.shape, F32) for f in fulls],
        input_output_aliases={w: w for w in range(ng)},
        scratch_shapes=[pltpu.SemaphoreType.DMA((ng,)), pltpu.SemaphoreType.DMA((ng,))],
    )(*fulls)


def _small_allreduce(vec):
    nr = vec.shape[0]

    def body(v_ref, o_ref, rsib, pair, rchip, ssem, rsem):
        x, y, c, chips = _place()
        me = 2 * x + y
        sib = (x, y, 1 - c)
        first = pltpu.make_async_remote_copy(src_ref=v_ref, dst_ref=rsib, send_sem=ssem.at[0], recv_sem=rsem.at[0],
                                             device_id=sib, device_id_type=MESH)
        first.start()
        first.wait()
        pair[...] = v_ref[...] + rsib[...]
        cps = [pltpu.make_async_remote_copy(src_ref=pair, dst_ref=rchip.at[j], send_sem=ssem.at[1 + j], recv_sem=rsem.at[1 + j],
                                            device_id=(*chip, c), device_id_type=MESH) for j, chip in enumerate(chips)]
        for cp in cps:
            cp.start()
        for cp in cps:
            cp.wait()
        acc = None
        for kk in range(N_CHIP):
            d = jnp.bitwise_xor(me, kk)
            t = jnp.where(d == 0, pair[...], jnp.where(d == 2, rchip[0], jnp.where(d == 1, rchip[1], rchip[2])))
            acc = t if acc is None else acc + t
        o_ref[...] = acc

    return pl.pallas_call(
        body, name="small_allreduce",
        in_specs=[VMEM_FULL], out_specs=VMEM_FULL, out_shape=_sds((nr, 128), F32),
        scratch_shapes=[pltpu.VMEM((nr, 128), F32), pltpu.VMEM((nr, 128), F32), pltpu.VMEM((3, nr, 128), F32),
                        pltpu.SemaphoreType.DMA((4,)), pltpu.SemaphoreType.DMA((4,))],
        compiler_params=_cp(),
    )(vec)


def _adam_math(w, g, m, v):
    m = B1 * m + (1.0 - B1) * g
    v = B2 * v + (1.0 - B2) * (g * g)
    m_hat = m / (1.0 - B1 ** STEP)
    v_hat = v / (1.0 - B2 ** STEP)
    delta = -LR * (m_hat / (jnp.sqrt(v_hat) + AEPS) + WD * w)
    return delta, m, v


def _adam_big(w, g, m, v, name):
    rows, cols = w.shape
    tr = 128

    def body(w_ref, g_ref, m_ref, v_ref, go_ref, d_ref, nm_ref, nv_ref):
        g = g_ref[...]
        d, nm, nv = _adam_math(w_ref[...], g, m_ref[...], v_ref[...])
        go_ref[...] = g
        d_ref[...] = d
        nm_ref[...] = nm
        nv_ref[...] = nv

    blk = pl.BlockSpec((tr, cols), lambda i: (i, 0))
    return pl.pallas_call(
        body, name=name, grid=(rows // tr,),
        in_specs=[blk] * 4, out_specs=[blk] * 4, out_shape=[_sds((rows, cols), F32)] * 4,
        compiler_params=_cp(("arbitrary",)),
    )(w, g, m, v)


def _adam_small(ws, gs, ms, vs):
    n = len(ws)

    def body(*refs):
        for i in range(n):
            d, nm, nv = _adam_math(refs[i][...], refs[n + i][...], refs[2 * n + i][...], refs[3 * n + i][...])
            refs[4 * n + i][...] = d
            refs[5 * n + i][...] = nm
            refs[6 * n + i][...] = nv

    shapes = [_sds(w.shape, F32) for w in ws]
    outs = pl.pallas_call(
        body, name="adam_small",
        in_specs=[VMEM_FULL] * (4 * n), out_specs=[VMEM_FULL] * (3 * n), out_shape=shapes * 3,
        compiler_params=_cp(),
    )(*ws, *gs, *ms, *vs)
    return outs[:n], outs[n:2 * n], outs[2 * n:]


def _pad_rows8(a):
    flat = a.reshape(-1, 128)
    pad = (-flat.shape[0]) % 8
    if pad:
        flat = jnp.concatenate([flat, jnp.zeros((pad, 128), F32)], axis=0)
    return flat


def kernel(x, meta_tokens, norm1_w, w_in, gate_w2, gate_b, gla_norm_w, pool_w, pool_scale, w_out, norm2_w, mlp_w1, mlp_w2, final_norm_w, loss_target, m_meta_tokens, m_norm1_w, m_w_in, m_gate_w2, m_gate_b, m_gla_norm_w, m_pool_w, m_pool_scale, m_w_out, m_norm2_w, m_mlp_w1, m_mlp_w2, m_final_norm_w, v_meta_tokens, v_norm1_w, v_w_in, v_gate_w2, v_gate_b, v_gla_norm_w, v_pool_w, v_pool_scale, v_w_out, v_norm2_w, v_mlp_w1, v_mlp_w2, v_final_norm_w):
    cx, cy, cc = lax.axis_index("x"), lax.axis_index("y"), lax.axis_index("c")
    me = (2 * cx + cy).astype(jnp.int32)

    place = jnp.stack([me, cc.astype(jnp.int32)])
    fw = final_norm_w.reshape(1, D)

    metaF, gw2F, pwF = _gather_small(meta_tokens, gate_w2[0], pool_w[0])
    (s_win,), (r_win,), (f_win,), tok = _gather_start([_cast_into(w_in[0], place, PAD_IN, "cast_win", gw2F)],
                                                       "gather_start_win")
    rest = [_cast_into(w_out[0], place, D, "cast_wout", tok), _cast_into(mlp_w1[0], place, D, "cast_w1", tok),
            _cast_into(mlp_w2[0], place, D, "cast_w2", tok)]
    ssems, rsems, flying, tok = _gather_start(rest, "gather_start_rest")
    ssems, rsems, flying = [s_win, *ssems], [r_win, *rsems], [f_win, *flying]

    def arrive(i, nm, after):
        return _forward_halves(_gather_wait(flying[i], ssems[i], rsems[i], after, "gather_wait_" + nm), "forward_" + nm)

    pairs, pending = {}, {}

    def grad_start(nm, g):
        ssem, rsem, g_thru, land, token = _pair_start(g, "pair_start_" + nm)
        pairs[nm] = (ssem, rsem, g_thru, land)
        return token

    def grad_finish(nm, after):
        ssem, rsem, g_thru, land = pairs[nm]
        g, rcv = _pair_wait(g_thru, land, ssem, rsem, after, "pair_wait_" + nm)
        sb, sf = _pair_sum(g, rcv, place, "pair_sum_" + nm)
        ssem, rsem, sb_thru, land, token = _rs_start(sb, "rs_start_" + nm)
        pending[nm] = (sf, ssem, rsem, sb_thru, land)
        return token

    (grad_x, loss8, d_n1w, d_gb, d_gnw, d_ps, d_n2w, d_fw, d_meta, d_gw2, d_pw) = _local_step(
        x[0], loss_target[0], lambda after: arrive(0, "win", after), lambda after: arrive(1, "wout", after).reshape(D, D),
        lambda after: arrive(2, "w1", after), lambda after: arrive(3, "w2", after).reshape(DFF, D), metaF, gw2F, pwF,
        norm1_w, gate_b, gla_norm_w, pool_scale, norm2_w, fw, grad_start, grad_finish, tok)
    return _reduce_and_update(
        me, place, pending, grad_x, loss8, d_n1w, d_gb, d_gnw, d_ps, d_n2w, d_fw, d_meta, d_gw2, d_pw,
        meta_tokens, norm1_w, w_in, gate_w2, gate_b, gla_norm_w, pool_w, pool_scale, w_out, norm2_w, mlp_w1, mlp_w2, fw,
        m_meta_tokens, m_norm1_w, m_w_in, m_gate_w2, m_gate_b, m_gla_norm_w, m_pool_w, m_pool_scale, m_w_out, m_norm2_w,
        m_mlp_w1, m_mlp_w2, m_final_norm_w, v_meta_tokens, v_norm1_w, v_w_in, v_gate_w2, v_gate_b, v_gla_norm_w, v_pool_w,
        v_pool_scale, v_w_out, v_norm2_w, v_mlp_w1, v_mlp_w2, v_final_norm_w)


def _local_step(x, target, get_win, get_wout, get_w1, get_w2, metaF, gw2F, pwF, norm1_w, gate_b, gla_norm_w, pool_scale,
                norm2_w, fw, grad_start, grad_finish, first=None):
    h0, u = _embed_norm(x, metaF, norm1_w, first)
    Win = get_win(u)
    P = _in_proj(u, Win)
    q = P[0, :, 0:512]
    k = P[0, :, 512:1024]
    v = jnp.concatenate([P[0, :, 1024:1028], P[1, :, 0:1020]], axis=1)
    r = jnp.concatenate([P[1, :, 1020:1028], P[2, :, 0:1016]], axis=1)
    glr = jnp.concatenate([P[2, :, 1016:1028], P[3, :, 0:4]], axis=1)
    pu = P[3, :, 4:1028]
    o, og, sp = _gla_fwd(q, k, v, r, glr, gw2F, gate_b, gla_norm_w)
    yb, op = _pool_fwd(pu, pwF, pool_scale)
    Wout = get_wout(op)
    h1 = _out_proj(og, op, Wout, h0)
    n2 = _norm_rows(h1, norm2_w, "norm2")
    W1 = get_w1(n2)
    zr, a = _mlp_up(n2, W1)
    W2 = get_w2(a)
    h2 = _mlp_down(a, W2, h1)

    dh2, dh2b, d_fw, loss8 = _loss_head(h2, target, fw)
    tok = grad_start("w2", _grad_w2(a, dh2b).reshape(N_CHIP, D, D))
    dz = _mlp_dz(dh2b, W2, zr, tok)
    tok = grad_finish("w2", dz)
    tok = grad_start("w1", _grad_w1(n2, dz, tok))
    dn2 = _mlp_dn(dz, W1, tok)
    tok = grad_finish("w1", dn2)
    dh1, dh1b, d_n2w = _norm_bwd(dn2, h1, dh2, norm2_w, "norm2_bwd", tok)
    dmixed = _mixed_grad(dh1b, Wout)
    tok = grad_start("wout", _grad_wout(og, op, dh1b))
    dpu, d_pw, d_ps = _pool_bwd(dmixed, yb, pwF, pool_scale, tok)
    dq, dk, dv, dr, dglr, d_gw2, d_gb, d_gnw = _gla_bwd(dmixed, o, q, k, v, r, glr, gw2F, gate_b, gla_norm_w, sp, tok)
    tok = grad_finish("wout", dq)
    zpad = jnp.zeros((TP, PAD_IN - SHARD_IN), F32)
    dP = jnp.stack([
        jnp.concatenate([dq, dk, dv[:, 0:4], zpad], axis=1),
        jnp.concatenate([dv[:, 4:], dr[:, 0:8], zpad], axis=1),
        jnp.concatenate([dr[:, 8:], dglr[:, 0:12], zpad], axis=1),
        jnp.concatenate([dglr[:, 12:], dpu, zpad], axis=1)]).astype(BF16)
    tok = grad_start("win", _grad_win(u, dP, tok))
    du = _in_grad(dP, Win, tok)
    tok = grad_finish("win", du)
    grad_x, d_meta, d_n1w = _input_grad(du, h0, dh1, norm1_w, tok)
    return grad_x, loss8, d_n1w, d_gb, d_gnw, d_ps, d_n2w, d_fw, d_meta, d_gw2, d_pw


def _reduce_and_update(me, place, pending, grad_x, loss8, d_n1w, d_gb, d_gnw, d_ps, d_n2w, d_fw, d_meta, d_gw2, d_pw,
                       meta_tokens, norm1_w, w_in, gate_w2, gate_b, gla_norm_w, pool_w, pool_scale, w_out, norm2_w,
                       mlp_w1, mlp_w2, fw, m_meta_tokens, m_norm1_w, m_w_in, m_gate_w2, m_gate_b, m_gla_norm_w, m_pool_w,
                       m_pool_scale, m_w_out, m_norm2_w, m_mlp_w1, m_mlp_w2, m_final_norm_w, v_meta_tokens, v_norm1_w, v_w_in,
                       v_gate_w2, v_gate_b, v_gla_norm_w, v_pool_w, v_pool_scale, v_w_out, v_norm2_w, v_mlp_w1, v_mlp_w2,
                       v_final_norm_w):
    parts = [loss8, d_n1w, d_gb, d_gnw, d_ps, d_n2w, d_fw, d_meta, d_gw2, d_pw]
    packed = [_pad_rows8(p) for p in parts]
    sizes = [p.shape[0] for p in packed]
    red = _small_allreduce(jnp.concatenate(packed, axis=0))
    offs = [0]
    for s in sizes:
        offs.append(offs[-1] + s)

    def take(i, shape):
        n = 1
        for d in shape:
            n *= d
        return red[offs[i]:offs[i] + n // 128].reshape(shape)

    loss = red[0, 0]
    G_n1w = take(1, (1, D))
    G_gb = take(2, (1, KW))
    G_gnw = take(3, (1, DV))
    G_ps = take(4, (1, PW))
    G_n2w = take(5, (1, D))
    G_fw = take(6, (1, D))
    G_meta = lax.dynamic_slice(take(7, (N_META, D)), (0, me * 512), (N_META, 512))
    G_gw2 = lax.dynamic_slice(take(8, (RANK, KW)), (0, me * 128), (RANK, 128))
    G_pw = lax.dynamic_slice(take(9, (4, GC, GC)), (0, me * 64, 0), (4, 64, GC))

    big = {}
    after = red
    for nm, w, m, v in (("w2", mlp_w2, m_mlp_w2, v_mlp_w2), ("w1", mlp_w1, m_mlp_w1, v_mlp_w1),
                        ("wout", w_out, m_w_out, v_w_out), ("win", w_in, m_w_in, v_w_in)):
        sf, ssem, rsem, sb_thru, land = pending[nm]
        rb = _rs_wait(sb_thru, land, ssem, rsem, after, "rs_wait_" + nm)
        full, = _half_exchange([_final_sum(sf, rb, place, "final_sum_" + nm)], "half_exchange_" + nm)
        big[nm] = _adam_big(w[0], full, m[0], v[0], "adam_" + nm)
        after = big[nm][3]
    G_win, d_win, nm_win, nv_win = big["win"]
    G_wout, d_wout, nm_wout, nv_wout = big["wout"]
    G_w1, d_w1, nm_w1, nv_w1 = big["w1"]
    G_w2, d_w2, nm_w2, nv_w2 = big["w2"]
    ws = [meta_tokens, norm1_w, gate_w2[0], gate_b, gla_norm_w, pool_w[0], pool_scale, norm2_w, fw]
    gs = [G_meta, G_n1w, G_gw2, G_gb, G_gnw, G_pw, G_ps, G_n2w, G_fw]
    ms = [m_meta_tokens, m_norm1_w, m_gate_w2[0], m_gate_b, m_gla_norm_w, m_pool_w[0], m_pool_scale, m_norm2_w,
          m_final_norm_w.reshape(1, D)]
    vs = [v_meta_tokens, v_norm1_w, v_gate_w2[0], v_gate_b, v_gla_norm_w, v_pool_w[0], v_pool_scale, v_norm2_w,
          v_final_norm_w.reshape(1, D)]
    ds, nms, nvs = _adam_small(ws, gs, ms, vs)

    def assemble(small, win_, wout_, w1_, w2_):
        meta_, n1_, gw2_, gb_, gnw_, pw_, ps_, n2_, fw_ = small
        return (meta_, n1_, win_[None], gw2_[None], gb_, gnw_, pw_[None], ps_, wout_[None], n2_, w1_[None], w2_[None],
                fw_.reshape(D))

    grads_out = assemble(gs, G_win, G_wout, G_w1, G_w2)
    deltas = assemble(ds, d_win, d_wout, d_w1, d_w2)
    new_m = assemble(nms, nm_win, nm_wout, nm_w1, nm_w2)
    new_v = assemble(nvs, nv_win, nv_wout, nv_w1, nv_w2)
    return (loss, grad_x[None], *grads_out, *deltas, *new_m, *new_v)
```

```python
import functools

import jax
import jax.numpy as jnp
from jax import lax
from jax.experimental import pallas as pl
from jax.experimental.pallas import tpu as pltpu

F32 = jnp.float32
BF16 = jnp.bfloat16

D = 2048
SEQ = 2048
N_META = 16
CH = 64
TP = 2176
NCH = TP // CH
ROW_LO = 112
X_LO = 128
ROW_HI = TP
XT = 128
NXT = TP // XT
HEADS = 4
DK = 128
DV = 256
KW = HEADS * DK
GW = HEADS * DV
RANK = 16
TAU = 16.0
WINDOWS = (2, 4, 8, 16)
PW = 1024
GC = 256
DFF = 8192
EPS = 1e-6
SHARD_IN = 1028
PAD_IN = 1152
N_CHIP = 4

LR = 0.001
B1 = 0.9
B2 = 0.999
AEPS = 1e-08
WD = 0.01
STEP = 10

VMEM_LIMIT = 60 * 1024 * 1024
ANY = pl.BlockSpec(memory_space=pl.ANY)
VMEM_FULL = pl.BlockSpec(memory_space=pltpu.VMEM)
MESH = pl.DeviceIdType.MESH


def _cp(sem=None):
    if sem is None:
        return pltpu.CompilerParams(vmem_limit_bytes=VMEM_LIMIT)
    return pltpu.CompilerParams(dimension_semantics=sem, vmem_limit_bytes=VMEM_LIMIT)


def _dot(a, b):
    return jnp.dot(a, b, preferred_element_type=F32)


def _dot_nt(a, b):
    return lax.dot_general(a, b, (((1,), (1,)), ((), ())), preferred_element_type=F32)


def _dot_tn(a, b):
    return lax.dot_general(a, b, (((0,), (0,)), ((), ())), preferred_element_type=F32)


def _sds(shape, dtype):
    return jax.ShapeDtypeStruct(shape, dtype)


def _embed_norm(x, meta_full, w, dep=None):
    def body(x_ref, meta_ref, w_ref, dep_ref, h_ref, u_ref):
        i = pl.program_id(0)

        @pl.when(i == 0)
        def _():
            h_ref[...] = jnp.zeros_like(h_ref)
            h_ref[ROW_LO:X_LO, :] = meta_ref[...]

        @pl.when(i >= 1)
        def _():
            h_ref[...] = x_ref[...]

        h = h_ref[...]
        r = lax.rsqrt(jnp.mean(h * h, axis=-1, keepdims=True) + EPS)
        u_ref[...] = ((h * r) * w_ref[...]).astype(BF16)

    return pl.pallas_call(
        body, name="embed_norm1", grid=(NXT,),
        in_specs=[pl.BlockSpec((XT, D), lambda i: (jnp.maximum(i - 1, 0), 0)),
                  pl.BlockSpec((N_META, D), lambda i: (0, 0)),
                  pl.BlockSpec((1, D), lambda i: (0, 0)), ANY],
        out_specs=[pl.BlockSpec((XT, D), lambda i: (i, 0)), pl.BlockSpec((XT, D), lambda i: (i, 0))],
        out_shape=[_sds((TP, D), F32), _sds((TP, D), BF16)],
        compiler_params=_cp(("arbitrary",)),
    )(x, meta_full, w, _dep(dep))


def _norm_rows(h, w, name):
    tr = 272

    def body(h_ref, w_ref, o_ref):
        hv = h_ref[...]
        r = lax.rsqrt(jnp.mean(hv * hv, axis=-1, keepdims=True) + EPS)
        o_ref[...] = ((hv * r) * w_ref[...]).astype(BF16)

    return pl.pallas_call(
        body, name=name, grid=(TP // tr,),
        in_specs=[pl.BlockSpec((tr, D), lambda i: (i, 0)), pl.BlockSpec((1, D), lambda i: (0, 0))],
        out_specs=pl.BlockSpec((tr, D), lambda i: (i, 0)),
        out_shape=_sds((TP, D), BF16),
        compiler_params=_cp(("arbitrary",)),
    )(h, w)


def _loss_head(h2, target, fw):
    def body(h_ref, t_ref, w_ref, dh_ref, dhb_ref, dw_ref, loss_ref):
        i = pl.program_id(0)

        @pl.when(i == 0)
        def _():
            dw_ref[...] = jnp.zeros_like(dw_ref)
            loss_ref[...] = jnp.zeros_like(loss_ref)

        h = h_ref[...]
        w = w_ref[...]
        r = lax.rsqrt(jnp.mean(h * h, axis=-1, keepdims=True) + EPS)
        xh = h * r
        y = xh * w
        is_x = (i >= 1).astype(F32)
        diff = (y - t_ref[...]) * is_x
        loss_ref[...] += jnp.sum(diff * diff) * (0.5 / D)
        dy = diff * (1.0 / D)
        dw_ref[...] += jnp.sum(dy * xh, axis=0, keepdims=True)
        gx = dy * w
        dh = r * (gx - xh * jnp.mean(gx * xh, axis=-1, keepdims=True))
        dh_ref[...] = dh
        dhb_ref[...] = dh.astype(BF16)

    return pl.pallas_call(
        body, name="loss_head", grid=(NXT,),
        in_specs=[pl.BlockSpec((XT, D), lambda i: (i, 0)),
                  pl.BlockSpec((XT, D), lambda i: (jnp.maximum(i - 1, 0), 0)),
                  pl.BlockSpec((1, D), lambda i: (0, 0))],
        out_specs=[pl.BlockSpec((XT, D), lambda i: (i, 0)), pl.BlockSpec((XT, D), lambda i: (i, 0)),
                   pl.BlockSpec((1, D), lambda i: (0, 0)), pl.BlockSpec((8, 128), lambda i: (0, 0))],
        out_shape=[_sds((TP, D), F32), _sds((TP, D), BF16), _sds((1, D), F32), _sds((8, 128), F32)],
        compiler_params=_cp(("arbitrary",)),
    )(h2, target, fw)


def _norm_bwd(dn, h, dres, w, name, dep=None):
    tr = 272

    def body(dn_ref, h_ref, dres_ref, w_ref, dep_ref, o_ref, ob_ref, dw_ref):
        @pl.when(pl.program_id(0) == 0)
        def _():
            dw_ref[...] = jnp.zeros_like(dw_ref)

        hv = h_ref[...]
        dnv = dn_ref[...]
        r = lax.rsqrt(jnp.mean(hv * hv, axis=-1, keepdims=True) + EPS)
        xh = hv * r
        dw_ref[...] += jnp.sum(dnv * xh, axis=0, keepdims=True)
        gx = dnv * w_ref[...]
        dh = dres_ref[...] + r * (gx - xh * jnp.mean(gx * xh, axis=-1, keepdims=True))
        o_ref[...] = dh
        ob_ref[...] = dh.astype(BF16)

    row = pl.BlockSpec((tr, D), lambda i: (i, 0))
    vec = pl.BlockSpec((1, D), lambda i: (0, 0))
    return pl.pallas_call(
        body, name=name, grid=(TP // tr,),
        in_specs=[row, row, row, vec, ANY], out_specs=[row, row, vec],
        out_shape=[_sds((TP, D), F32), _sds((TP, D), BF16), _sds((1, D), F32)],
        compiler_params=_cp(("arbitrary",)),
    )(dn, h, dres, w, _dep(dep))


def _input_grad(du, h0, dh1, w, dep=None):
    def body(du_ref, h_ref, dres_ref, w_ref, dep_ref, gx_ref, gm_ref, dw_ref):
        i = pl.program_id(0)

        @pl.when(i == 0)
        def _():
            dw_ref[...] = jnp.zeros_like(dw_ref)

        hv = h_ref[...]
        dnv = du_ref[...]
        r = lax.rsqrt(jnp.mean(hv * hv, axis=-1, keepdims=True) + EPS)
        xh = hv * r
        dw_ref[...] += jnp.sum(dnv * xh, axis=0, keepdims=True)
        g = dnv * w_ref[...]
        dh = dres_ref[...] + r * (g - xh * jnp.mean(g * xh, axis=-1, keepdims=True))

        @pl.when(i == 0)
        def _():
            gm_ref[...] = dh[ROW_LO:X_LO, :]

        @pl.when(i >= 1)
        def _():
            gx_ref[...] = dh

    row = pl.BlockSpec((XT, D), lambda i: (i, 0))
    vec = pl.BlockSpec((1, D), lambda i: (0, 0))
    return pl.pallas_call(
        body, name="input_grad", grid=(NXT,),
        in_specs=[row, row, row, vec, ANY],
        out_specs=[pl.BlockSpec((XT, D), lambda i: (jnp.maximum(i - 1, 0), 0)),
                   pl.BlockSpec((N_META, D), lambda i: (0, 0)), vec],
        out_shape=[_sds((SEQ, D), F32), _sds((N_META, D), F32), _sds((1, D), F32)],
        compiler_params=_cp(("arbitrary",)),
    )(du, h0, dh1, w, _dep(dep))


def _in_proj(u, wg):
    def body(u_ref, w_ref, o_ref):
        o_ref[0] = _dot(u_ref[...], w_ref[0])

    return pl.pallas_call(
        body, name="in_proj", grid=(N_CHIP,),
        in_specs=[VMEM_FULL, pl.BlockSpec((1, D, PAD_IN), lambda k: (k, 0, 0))],
        out_specs=pl.BlockSpec((1, TP, PAD_IN), lambda k: (k, 0, 0)),
        out_shape=_sds((N_CHIP, TP, PAD_IN), F32),
        compiler_params=_cp(("arbitrary",)),
    )(u, wg)


def _out_proj(og, op, wout, h0):
    tn = 512

    def body(og_ref, op_ref, w_ref, h_ref, o_ref):
        acc = _dot(og_ref[...], w_ref[0:GW, :]) + _dot(op_ref[...], w_ref[GW:D, :])
        o_ref[...] = h_ref[...] + acc

    return pl.pallas_call(
        body, name="out_proj", grid=(D // tn,),
        in_specs=[VMEM_FULL, VMEM_FULL, pl.BlockSpec((D, tn), lambda j: (0, j)),
                  pl.BlockSpec((TP, tn), lambda j: (0, j))],
        out_specs=pl.BlockSpec((TP, tn), lambda j: (0, j)),
        out_shape=_sds((TP, D), F32),
        compiler_params=_cp(("arbitrary",)),
    )(og, op, wout, h0)


def _mlp_up(n2, w1g):
    tn = 512
    per = D // tn

    def body(n_ref, w_ref, zr_ref, a_ref):
        z = jnp.maximum(_dot(n_ref[...], w_ref[0]), 0.0)
        zr_ref[...] = z.astype(BF16)
        a_ref[...] = (z * z).astype(BF16)

    col = pl.BlockSpec((TP, tn), lambda k, j: (0, k * per + j))
    return pl.pallas_call(
        body, name="mlp_up", grid=(N_CHIP, per),
        in_specs=[VMEM_FULL, pl.BlockSpec((1, D, tn), lambda k, j: (k, 0, j))],
        out_specs=[col, col],
        out_shape=[_sds((TP, DFF), BF16), _sds((TP, DFF), BF16)],
        compiler_params=_cp(("arbitrary", "arbitrary")),
    )(n2, w1g)


def _mlp_down(a, w2, h1):
    tk = 1024
    nk = DFF // tk

    def body(a_ref, w_ref, h_ref, o_ref, acc_ref):
        k = pl.program_id(0)

        @pl.when(k == 0)
        def _():
            pltpu.sync_copy(h_ref, acc_ref)

        acc_ref[...] += _dot(a_ref[...], w_ref[...])

        @pl.when(k == nk - 1)
        def _():
            pltpu.sync_copy(acc_ref, o_ref)

    return pl.pallas_call(
        body, name="mlp_down", grid=(nk,),
        in_specs=[pl.BlockSpec((TP, tk), lambda k: (0, k)), pl.BlockSpec((tk, D), lambda k: (k, 0)), ANY],
        out_specs=ANY,
        out_shape=_sds((TP, D), F32),
        scratch_shapes=[pltpu.VMEM((TP, D), F32)],
        compiler_params=_cp(("arbitrary",)),
    )(a, w2, h1)


def _mlp_dz(dh2b, w2, zr, dep=None):
    tn = 512

    def body(d_ref, w_ref, z_ref, dep_ref, o_ref):
        da = _dot_nt(d_ref[...], w_ref[...])
        o_ref[...] = (da * (2.0 * z_ref[...].astype(F32))).astype(BF16)

    col = pl.BlockSpec((TP, tn), lambda j: (0, j))
    return pl.pallas_call(
        body, name="mlp_dz", grid=(DFF // tn,),
        in_specs=[VMEM_FULL, pl.BlockSpec((tn, D), lambda j: (j, 0)), col, ANY],
        out_specs=col,
        out_shape=_sds((TP, DFF), BF16),
        compiler_params=_cp(("arbitrary",)),
    )(dh2b, w2, zr, _dep(dep))


def _grad_w2(a, dh2b):
    tm = 512

    def body(a_ref, d_ref, o_ref):
        o_ref[...] = _dot_tn(a_ref[...], d_ref[...])

    return pl.pallas_call(
        body, name="grad_w2", grid=(DFF // tm,),
        in_specs=[pl.BlockSpec((TP, tm), lambda j: (0, j)), VMEM_FULL],
        out_specs=pl.BlockSpec((tm, D), lambda j: (j, 0)),
        out_shape=_sds((DFF, D), F32),
        compiler_params=_cp(("arbitrary",)),
    )(a, dh2b)


def _dep(token):
    return jnp.zeros((8, 128), F32) if token is None else token


def _grad_w1(n2, dz, dep=None):
    tn = 512
    per = D // tn

    def body(n_ref, d_ref, dep_ref, o_ref):
        o_ref[0] = _dot_tn(n_ref[...], d_ref[...])

    return pl.pallas_call(
        body, name="grad_w1", grid=(N_CHIP, per),
        in_specs=[VMEM_FULL, pl.BlockSpec((TP, tn), lambda k, j: (0, k * per + j)), ANY],
        out_specs=pl.BlockSpec((1, D, tn), lambda k, j: (k, 0, j)),
        out_shape=_sds((N_CHIP, D, D), F32),
        compiler_params=_cp(("arbitrary", "arbitrary")),
    )(n2, dz, _dep(dep))


def _mlp_dn(dz, w1g, dep=None):
    tk = 1024
    per = D // tk
    nk = DFF // tk

    def body(d_ref, w_ref, dep_ref, o_ref, acc_ref):
        k = pl.program_id(0)
        part = _dot_nt(d_ref[...], w_ref[0])

        @pl.when(k == 0)
        def _():
            acc_ref[...] = part

        @pl.when(k > 0)
        def _():
            acc_ref[...] += part

        @pl.when(k == nk - 1)
        def _():
            pltpu.sync_copy(acc_ref, o_ref)

    return pl.pallas_call(
        body, name="mlp_dn", grid=(nk,),
        in_specs=[pl.BlockSpec((TP, tk), lambda k: (0, k)),
                  pl.BlockSpec((1, D, tk), lambda k: (k // per, 0, k % per)), ANY],
        out_specs=ANY,
        out_shape=_sds((TP, D), F32),
        scratch_shapes=[pltpu.VMEM((TP, D), F32)],
        compiler_params=_cp(("arbitrary",)),
    )(dz, w1g, _dep(dep))


def _mixed_grad(dh1b, wout):
    tn = 512

    def body(d_ref, w_ref, o_ref):
        o_ref[...] = _dot_nt(d_ref[...], w_ref[...])

    return pl.pallas_call(
        body, name="mixed_grad", grid=(D // tn,),
        in_specs=[VMEM_FULL, pl.BlockSpec((tn, D), lambda j: (j, 0))],
        out_specs=pl.BlockSpec((TP, tn), lambda j: (0, j)),
        out_shape=_sds((TP, D), F32),
        compiler_params=_cp(("arbitrary",)),
    )(dh1b, wout)


def _grad_wout(og, op, dh1b):
    tm = 512

    def body(og_ref, op_ref, d_ref, o_ref):
        j = pl.program_id(0)

        @pl.when(j < 2)
        def _():
            o_ref[0] = _dot_tn(og_ref[...], d_ref[...])

        @pl.when(j >= 2)
        def _():
            o_ref[0] = _dot_tn(op_ref[...], d_ref[...])

    return pl.pallas_call(
        body, name="grad_wout", grid=(N_CHIP,),
        in_specs=[pl.BlockSpec((TP, tm), lambda j: (0, jnp.minimum(j, 1))),
                  pl.BlockSpec((TP, tm), lambda j: (0, jnp.maximum(j - 2, 0))), VMEM_FULL],
        out_specs=pl.BlockSpec((1, tm, D), lambda j: (j, 0, 0)),
        out_shape=_sds((N_CHIP, tm, D), F32),
        compiler_params=_cp(("arbitrary",)),
    )(og, op, dh1b)


def _in_grad(dp, wg, dep=None):
    def body(d_ref, w_ref, dep_ref, o_ref, acc_ref):
        k = pl.program_id(0)
        part = _dot_nt(d_ref[0], w_ref[0])

        @pl.when(k == 0)
        def _():
            acc_ref[...] = part

        @pl.when(k > 0)
        def _():
            acc_ref[...] += part

        @pl.when(k == N_CHIP - 1)
        def _():
            pltpu.sync_copy(acc_ref, o_ref)

    return pl.pallas_call(
        body, name="in_grad", grid=(N_CHIP,),
        in_specs=[pl.BlockSpec((1, TP, PAD_IN), lambda k: (k, 0, 0)),
                  pl.BlockSpec((1, D, PAD_IN), lambda k: (k, 0, 0)), ANY],
        out_specs=ANY,
        out_shape=_sds((TP, D), F32),
        scratch_shapes=[pltpu.VMEM((TP, D), F32)],
        compiler_params=_cp(("arbitrary",)),
    )(dp, wg, _dep(dep))


def _grad_win(u, dp, dep=None):
    def body(u_ref, d_ref, dep_ref, o_ref):
        g = _dot_tn(u_ref[...], d_ref[0])
        o_ref[0] = g[:, 0:SHARD_IN]

    return pl.pallas_call(
        body, name="grad_win", grid=(N_CHIP,),
        in_specs=[VMEM_FULL, pl.BlockSpec((1, TP, PAD_IN), lambda k: (k, 0, 0)), ANY],
        out_specs=pl.BlockSpec((1, D, SHARD_IN), lambda k: (k, 0, 0)),
        out_shape=_sds((N_CHIP, D, SHARD_IN), F32),
        compiler_params=_cp(("arbitrary",)),
    )(u, dp, _dep(dep))


def _split3(x):
    hi = x.astype(BF16)
    r1 = x - hi.astype(F32)
    mid = r1.astype(BF16)
    lo = (r1 - mid.astype(F32)).astype(BF16)
    return hi, mid, lo


def _tri_sum(tri, x):
    hi, mid, lo = _split3(x)
    return _dot(tri, hi) + _dot(tri, mid) + _dot(tri, lo)


def _gla_common(n, glr, gw2, gb):
    rows = n * CH + lax.broadcasted_iota(jnp.int32, (CH, 1), 0)
    valid = (rows >= ROW_LO) & (rows < ROW_HI)
    g_raw = _dot(glr.astype(BF16), gw2.astype(BF16)) + gb
    logsig = jnp.minimum(g_raw, 0.0) - jnp.log(1.0 + jnp.exp(-jnp.abs(g_raw)))
    logg = jnp.where(valid, logsig * (1.0 / TAU), 0.0)
    ci = lax.broadcasted_iota(jnp.int32, (CH, CH), 0)
    si = lax.broadcasted_iota(jnp.int32, (CH, CH), 1)
    lower = ci >= si
    G = _tri_sum(lower.astype(BF16), logg)
    Gl = G[CH - 1:CH, :]
    return valid, g_raw, lower, G, Gl


def _gla_fwd(q, k, v, r, glr, gw2, gb, gnw):
    scale = DK ** -0.5

    def body(q_ref, k_ref, v_ref, r_ref, glr_ref, gw2_ref, gb_ref, gnw_ref, o_ref, og_ref, sp_ref, st_ref):
        n = pl.program_id(0)

        @pl.when(n == 0)
        def _():
            st_ref[...] = jnp.zeros_like(st_ref)

        _, _, lower, G, Gl = _gla_common(n, glr_ref[...], gw2_ref[...], gb_ref[...])
        eG = jnp.exp(G)
        eN = jnp.exp(-G)
        eE = jnp.exp(Gl - G)
        dec = jnp.exp(Gl)
        gnw_v = gnw_ref[...]
        for h in range(HEADS):
            ks = slice(h * DK, (h + 1) * DK)
            vs = slice(h * DV, (h + 1) * DV)
            kh = k_ref[:, ks]
            vh = v_ref[:, vs].astype(BF16)
            qd = ((q_ref[:, ks] * scale) * eG[:, ks]).astype(BF16)
            ki = (kh * eN[:, ks]).astype(BF16)
            ke = (kh * eE[:, ks]).astype(BF16)
            st = st_ref[h]
            a = jnp.where(lower, _dot_nt(qd, ki), 0.0).astype(BF16)
            o = _dot(a, vh) + _dot_nt(qd, st.astype(BF16))
            sp_ref[0, h] = st
            st_ref[h] = st * dec[:, ks] + _dot_tn(vh, ke)
            o_ref[:, vs] = o
            rs = lax.rsqrt(jnp.mean(o * o, axis=-1, keepdims=True) + EPS)
            rv = r_ref[:, vs]
            gate = rv / (1.0 + jnp.exp(-rv))
            og_ref[:, vs] = (((o * rs) * gnw_v) * gate).astype(BF16)

    rk = pl.BlockSpec((CH, KW), lambda n: (n, 0))
    rv_ = pl.BlockSpec((CH, GW), lambda n: (n, 0))

    def full(shape):
        return pl.BlockSpec(shape, lambda n: tuple(0 for _ in shape))

    return pl.pallas_call(
        body, name="gla_fwd", grid=(NCH,),
        in_specs=[rk, rk, rv_, rv_, pl.BlockSpec((CH, RANK), lambda n: (n, 0)),
                  full((RANK, KW)), full((1, KW)), full((1, DV))],
        out_specs=[rv_, rv_, pl.BlockSpec((1, HEADS, DV, DK), lambda n: (n, 0, 0, 0))],
        out_shape=[_sds((TP, GW), F32), _sds((TP, GW), BF16), _sds((NCH, HEADS, DV, DK), F32)],
        scratch_shapes=[pltpu.VMEM((HEADS, DV, DK), F32)],
        compiler_params=_cp(("arbitrary",)),
    )(q, k, v, r, glr, gw2, gb, gnw)


def _gla_bwd(dog, o, q, k, v, r, glr, gw2, gb, gnw, sp, dep=None):
    scale = DK ** -0.5

    def body(dog_ref, o_ref, q_ref, k_ref, v_ref, r_ref, glr_ref, gw2_ref, gb_ref, gnw_ref, sp_ref, dep_ref,
             dq_ref, dk_ref, dv_ref, dr_ref, dglr_ref, dgw2_ref, dgb_ref, dgnw_ref, ds_ref):
        step = pl.program_id(0)
        n = NCH - 1 - step

        @pl.when(step == 0)
        def _():
            ds_ref[...] = jnp.zeros_like(ds_ref)
            dgw2_ref[...] = jnp.zeros_like(dgw2_ref)
            dgb_ref[...] = jnp.zeros_like(dgb_ref)
            dgnw_ref[...] = jnp.zeros_like(dgnw_ref)

        glr_v = glr_ref[...]
        gw2_b = gw2_ref[...].astype(BF16)
        valid, g_raw, lower, G, Gl = _gla_common(n, glr_v, gw2_ref[...], gb_ref[...])
        upper = lax.broadcasted_iota(jnp.int32, (CH, CH), 0) <= lax.broadcasted_iota(jnp.int32, (CH, CH), 1)
        eG = jnp.exp(G)
        eN = jnp.exp(-G)
        eE = jnp.exp(Gl - G)
        dec = jnp.exp(Gl)
        gnw_v = gnw_ref[...]
        last = lax.broadcasted_iota(jnp.int32, (CH, 1), 0) == CH - 1
        dgnw_acc = jnp.zeros((1, DV), F32)
        dG_parts = []
        for h in range(HEADS):
            ks = slice(h * DK, (h + 1) * DK)
            vs = slice(h * DV, (h + 1) * DV)
            oh = o_ref[:, vs]
            rv = r_ref[:, vs]
            dg = dog_ref[:, vs]
            sig = 1.0 / (1.0 + jnp.exp(-rv))
            gate = rv * sig
            rs = lax.rsqrt(jnp.mean(oh * oh, axis=-1, keepdims=True) + EPS)
            ohat = oh * rs
            dr_ref[:, vs] = (dg * (ohat * gnw_v)) * (sig * (1.0 + rv * (1.0 - sig)))
            don = dg * gate
            dgnw_acc = dgnw_acc + jnp.sum(don * ohat, axis=0, keepdims=True)
            gxn = don * gnw_v
            do = (rs * (gxn - ohat * jnp.mean(gxn * ohat, axis=-1, keepdims=True))).astype(BF16)
            kh = k_ref[:, ks]
            vh = v_ref[:, vs].astype(BF16)
            qd_f = (q_ref[:, ks] * scale) * eG[:, ks]
            ki_f = kh * eN[:, ks]
            ke_f = kh * eE[:, ks]
            qd, ki, ke = qd_f.astype(BF16), ki_f.astype(BF16), ke_f.astype(BF16)
            spt = sp_ref[0, h]
            dst = ds_ref[h]
            dst_b = dst.astype(BF16)
            a_t = jnp.where(upper, _dot_nt(ki, qd), 0.0).astype(BF16)
            da = jnp.where(lower, _dot_nt(do, vh), 0.0).astype(BF16)
            da_t = jnp.where(upper, _dot_nt(vh, do), 0.0).astype(BF16)
            dv_ref[:, vs] = _dot(a_t, do) + _dot_nt(ke, dst_b)
            dqd = _dot(da, ki) + _dot(do, spt.astype(BF16))
            dki = _dot(da_t, qd)
            dke = _dot(vh, dst_b)
            ddec = jnp.sum(spt * dst, axis=0, keepdims=True)
            ds_ref[h] = dst * dec[:, ks] + _dot_tn(do, qd)
            dq_ref[:, ks] = (dqd * eG[:, ks]) * scale
            dk_ref[:, ks] = dki * eN[:, ks] + dke * eE[:, ks]
            dke_ke = dke * ke_f
            dG = dqd * qd_f - dki * ki_f - dke_ke
            dGl = jnp.sum(dke_ke, axis=0, keepdims=True) + ddec * dec[:, ks]
            dG_parts.append(dG + jnp.where(last, dGl, 0.0))
        dgnw_ref[...] += dgnw_acc
        dG_all = jnp.concatenate(dG_parts, axis=1)
        dlogg = jnp.where(valid, _tri_sum(upper.astype(BF16), dG_all), 0.0)
        dg_raw = (dlogg * (1.0 / TAU)) * (1.0 / (1.0 + jnp.exp(g_raw)))
        dgb_ref[...] += jnp.sum(dg_raw, axis=0, keepdims=True)
        dg_b = dg_raw.astype(BF16)
        dgw2_ref[...] += _dot_tn(glr_v.astype(BF16), dg_b)
        dglr_ref[...] = _dot_nt(dg_b, gw2_b)

    rk = pl.BlockSpec((CH, KW), lambda s: (NCH - 1 - s, 0))
    rv_ = pl.BlockSpec((CH, GW), lambda s: (NCH - 1 - s, 0))
    rg = pl.BlockSpec((CH, RANK), lambda s: (NCH - 1 - s, 0))

    def full(shape):
        return pl.BlockSpec(shape, lambda s: tuple(0 for _ in shape))

    return pl.pallas_call(
        body, name="gla_bwd", grid=(NCH,),
        in_specs=[rv_, rv_, rk, rk, rv_, rv_, rg, full((RANK, KW)), full((1, KW)), full((1, DV)),
                  pl.BlockSpec((1, HEADS, DV, DK), lambda s: (NCH - 1 - s, 0, 0, 0)), ANY],
        out_specs=[rk, rk, rv_, rv_, rg, full((RANK, KW)), full((1, KW)), full((1, DV))],
        out_shape=[_sds((TP, KW), F32), _sds((TP, KW), F32), _sds((TP, GW), F32), _sds((TP, GW), F32),
                   _sds((TP, RANK), F32), _sds((RANK, KW), F32), _sds((1, KW), F32), _sds((1, DV), F32)],
        scratch_shapes=[pltpu.VMEM((HEADS, DV, DK), F32)],
        compiler_params=_cp(("arbitrary",)),
    )(dog, o, q, k, v, r, glr, gw2, gb, gnw, sp, _dep(dep))


POOL_TR = 128
HALO = 16


def _pool_counts(base, nrows):
    rows = base + lax.broadcasted_iota(jnp.int32, (nrows, 1), 0)
    valid = (rows >= ROW_LO) & (rows < ROW_HI)
    t1 = (rows - ROW_LO + 1).astype(F32)
    cnts = [jnp.clip(t1, 1.0, float(w)) for w in WINDOWS]
    return valid, cnts


def _pool_fwd(pu, pw, ps):
    def body(cur_ref, prev_ref, pw_ref, ps_ref, y_ref, op_ref):
        i = pl.program_id(0)
        cur = cur_ref[...]
        full = jnp.concatenate([prev_ref[...], cur], axis=0)
        s2 = full + pltpu.roll(full, 1, 0)
        s4 = s2 + pltpu.roll(s2, 2, 0)
        s8 = s4 + pltpu.roll(s4, 4, 0)
        s16 = s8 + pltpu.roll(s8, 8, 0)
        valid, cnts = _pool_counts(i * POOL_TR, POOL_TR)
        for g, s in enumerate((s2, s4, s8, s16)):
            cs = slice(g * GC, (g + 1) * GC)
            y = s[HALO:, cs] / cnts[g] - cur[:, cs]
            yb = jnp.where(valid, y, 0.0).astype(BF16)
            y_ref[:, cs] = yb
            op_ref[:, cs] = (_dot(yb, pw_ref[g].astype(BF16)) * ps_ref[:, cs]).astype(BF16)

    row = pl.BlockSpec((POOL_TR, PW), lambda i: (i, 0))
    per = POOL_TR // HALO
    return pl.pallas_call(
        body, name="pool_fwd", grid=(TP // POOL_TR,),
        in_specs=[row, pl.BlockSpec((HALO, PW), lambda i: (jnp.maximum(i * per - 1, 0), 0)),
                  pl.BlockSpec((4, GC, GC), lambda i: (0, 0, 0)), pl.BlockSpec((1, PW), lambda i: (0, 0))],
        out_specs=[row, row],
        out_shape=[_sds((TP, PW), BF16), _sds((TP, PW), BF16)],
        compiler_params=_cp(("arbitrary",)),
    )(pu, pu, pw, ps)


def _pool_bwd(dop, y, pw, ps, dep=None):
    nblk = TP // HALO

    def body(cur_ref, nxt_ref, y_ref, pw_ref, ps_ref, dep_ref, dpu_ref, dpw_ref, dps_ref):
        i = pl.program_id(0)

        @pl.when(i == 0)
        def _():
            dpw_ref[...] = jnp.zeros_like(dpw_ref)
            dps_ref[...] = jnp.zeros_like(dps_ref)

        n_all = POOL_TR + HALO
        dcur = cur_ref[...]
        dall = jnp.concatenate([dcur, nxt_ref[...]], axis=0)
        valid, cnts = _pool_counts(i * POOL_TR, n_all)
        for g in range(4):
            cs = slice(g * GC, (g + 1) * GC)
            pwb = pw_ref[g].astype(BF16)
            yb = y_ref[:, cs]
            dyw = (dall[:, cs] * ps_ref[:, cs]).astype(BF16)
            dps_ref[:, cs] += jnp.sum(dcur[:, cs] * _dot(yb, pwb), axis=0, keepdims=True)
            dpw_ref[g] += _dot_tn(yb, dyw[0:POOL_TR, :])
            dyv = jnp.where(valid, _dot_nt(dyw, pwb), 0.0)
            e = dyv / cnts[g]
            w = WINDOWS[g]
            sh = 1
            while sh < w:
                e = e + pltpu.roll(e, n_all - sh, 0)
                sh *= 2
            dpu_ref[:, cs] = e[0:POOL_TR, :] - dyv[0:POOL_TR, :]

    row = pl.BlockSpec((POOL_TR, PW), lambda i: (i, 0))
    per = POOL_TR // HALO
    return pl.pallas_call(
        body, name="pool_bwd", grid=(TP // POOL_TR,),
        in_specs=[pl.BlockSpec((POOL_TR, PW), lambda i: (i, 1)),
                  pl.BlockSpec((HALO, PW), lambda i: (jnp.minimum(i * per + per, nblk - 1), 1)),
                  row, pl.BlockSpec((4, GC, GC), lambda i: (0, 0, 0)), pl.BlockSpec((1, PW), lambda i: (0, 0)), ANY],
        out_specs=[row, pl.BlockSpec((4, GC, GC), lambda i: (0, 0, 0)), pl.BlockSpec((1, PW), lambda i: (0, 0))],
        out_shape=[_sds((TP, PW), F32), _sds((4, GC, GC), F32), _sds((1, PW), F32)],
        compiler_params=_cp(("arbitrary",)),
    )(dop, dop, y, pw, ps, _dep(dep))


def _place():
    x, y, c = lax.axis_index("x"), lax.axis_index("y"), lax.axis_index("c")
    chips = [(1 - x, y), (x, 1 - y), (1 - x, 1 - y)]
    return x, y, c, chips


HBM = pl.BlockSpec(memory_space=pltpu.HBM)
SEM = pl.BlockSpec(memory_space=pltpu.SEMAPHORE)
EFFECT = pltpu.SideEffectType.DATAFLOW_SIDE_EFFECTING


def _cast_into(w, place, cols_out, name, dep=None):
    rows, cols = w.shape
    tr = 256

    def body(p_ref, w_ref, dep_ref, o_ref):
        if cols_out != cols:
            o_ref[0] = jnp.zeros((tr, cols_out), BF16)
            o_ref[0, :, 0:cols] = w_ref[...].astype(BF16)
        else:
            o_ref[0] = w_ref[...].astype(BF16)

    grid_spec = pltpu.PrefetchScalarGridSpec(
        num_scalar_prefetch=1, grid=(rows // tr,),
        in_specs=[pl.BlockSpec((tr, cols), lambda i, p: (i, 0)), ANY],
        out_specs=pl.BlockSpec((1, tr, cols_out), lambda i, p: (p[0], i, 0)))
    return pl.pallas_call(
        body, name=name, grid_spec=grid_spec,
        out_shape=_sds((N_CHIP, rows, cols_out), BF16),
        compiler_params=_cp(("arbitrary",)),
    )(place, w, _dep(dep))


def _half_rows(ref, k, which):
    h = ref.shape[1] // 2
    return ref.at[k, pl.ds(pl.multiple_of(which * h, 8), h), :]


def _gather_small(meta, gw2, pw, dep=None):
    def body(meta_r, gw2_r, pw_r, dep_ref, metaF, gw2F, pwF, lsem, ssem, rsem):
        x, y, c, chips = _place()
        me = 2 * x + y

        def slots(k):
            return (metaF.at[:, pl.ds(pl.multiple_of(k * 512, 128), 512)],
                    gw2F.at[:, pl.ds(pl.multiple_of(k * 128, 128), 128)],
                    pwF.at[:, pl.ds(pl.multiple_of(k * 64, 8), 64), :])

        srcs = (meta_r, gw2_r, pw_r)
        local = [pltpu.make_async_copy(s, d, lsem.at[i]) for i, (s, d) in enumerate(zip(srcs, slots(me)))]
        sends = []
        for j, chip in enumerate(chips):
            for i, (s, d) in enumerate(zip(srcs, slots(me))):
                sends.append(pltpu.make_async_remote_copy(src_ref=s, dst_ref=d, send_sem=ssem.at[3 * j + i],
                                                          recv_sem=rsem.at[3 * j + i], device_id=(*chip, c),
                                                          device_id_type=MESH))
        for cp in local + sends:
            cp.start()
        for j, (cx, cy) in enumerate(chips):
            for i, (s, d) in enumerate(zip(srcs, slots(2 * cx + cy))):
                pltpu.make_async_remote_copy(src_ref=s, dst_ref=d, send_sem=ssem.at[3 * j + i], recv_sem=rsem.at[3 * j + i],
                                             device_id=(cx, cy, c), device_id_type=MESH).wait_recv()
        for cp in sends:
            cp.wait_send()
        for cp in local:
            cp.wait()

    return pl.pallas_call(
        body, name="gather_small",
        in_specs=[ANY] * 4, out_specs=[ANY] * 3,
        out_shape=[_sds((N_META, D), F32), _sds((RANK, KW), F32), _sds((4, GC, GC), F32)],
        scratch_shapes=[pltpu.SemaphoreType.DMA((3,)), pltpu.SemaphoreType.DMA((9,)), pltpu.SemaphoreType.DMA((9,))],
    )(meta, gw2, pw, _dep(dep))


def _gather_start(ws, name):
    n = len(ws)

    def body(*refs):
        ins = refs[:n]
        ssems = refs[n:2 * n]
        rsems = refs[2 * n:3 * n]
        token = refs[4 * n]
        x, y, c, chips = _place()
        me = 2 * x + y
        for w in range(n):
            blk = _half_rows(ins[w], me, c)
            for j, chip in enumerate(chips):
                pltpu.make_async_remote_copy(src_ref=blk, dst_ref=blk, send_sem=ssems[w].at[j], recv_sem=rsems[w].at[j],
                                             device_id=(*chip, c), device_id_type=MESH).start()
        token[...] = jnp.zeros_like(token)

    sem3 = pltpu.SemaphoreType.DMA((3,))
    outs = pl.pallas_call(
        body, name=name,
        out_shape=tuple([sem3] * (2 * n) + [pltpu.HBM(w.shape, w.dtype) for w in ws] + [_sds((8, 128), F32)]),
        in_specs=(HBM,) * n, out_specs=(SEM,) * (2 * n) + (HBM,) * n + (VMEM_FULL,),
        input_output_aliases={w: 2 * n + w for w in range(n)},
        compiler_params=pltpu.CompilerParams(has_side_effects=EFFECT),
    )(*[pltpu.with_memory_space_constraint(w, pltpu.HBM) for w in ws])
    return outs[:n], outs[n:2 * n], outs[2 * n:3 * n], outs[3 * n]


def _gather_wait(w, ssem, rsem, after, name):
    def body(w_ref, ssem_ref, rsem_ref, after_ref, out_ref):
        x, y, c, chips = _place()
        me = 2 * x + y
        mine = _half_rows(w_ref, me, c)
        for j, (cx, cy) in enumerate(chips):
            cp = pltpu.make_async_remote_copy(src_ref=mine, dst_ref=_half_rows(w_ref, 2 * cx + cy, c),
                                              send_sem=ssem_ref.at[j], recv_sem=rsem_ref.at[j],
                                              device_id=(cx, cy, c), device_id_type=MESH)
            cp.wait_send()
            cp.wait_recv()

    return pl.pallas_call(
        body, name=name, out_shape=pltpu.HBM(w.shape, w.dtype),
        in_specs=(HBM, SEM, SEM, ANY), out_specs=HBM, input_output_aliases={0: 0},
        compiler_params=pltpu.CompilerParams(has_side_effects=EFFECT),
    )(w, ssem, rsem, after)


def _forward_halves(w, name):
    def body(w_ref, o_ref, ssem, rsem):
        x, y, c, chips = _place()
        sib = (x, y, 1 - c)
        cps = []
        for j, (cx, cy) in enumerate(chips):
            blk = _half_rows(o_ref, 2 * cx + cy, c)
            cps.append(pltpu.make_async_remote_copy(src_ref=blk, dst_ref=blk, send_sem=ssem.at[j], recv_sem=rsem.at[j],
                                                    device_id=sib, device_id_type=MESH))
        for cp in cps:
            cp.start()
        for j, (cx, cy) in enumerate(chips):
            blk = _half_rows(o_ref, 2 * cx + cy, 1 - c)
            pltpu.make_async_remote_copy(src_ref=blk, dst_ref=blk, send_sem=ssem.at[j], recv_sem=rsem.at[j],
                                         device_id=sib, device_id_type=MESH).wait_recv()
        for cp in cps:
            cp.wait_send()

    return pl.pallas_call(
        body, name=name, in_specs=[ANY], out_specs=ANY, out_shape=_sds(w.shape, w.dtype),
        input_output_aliases={0: 0},
        scratch_shapes=[pltpu.SemaphoreType.DMA((3,)), pltpu.SemaphoreType.DMA((3,))],
    )(w)


def _rs_start(sb, name, after=None):
    _, half, cols = sb.shape

    def body(sb_ref, land_ref, after_ref, ssem, rsem, sb_out, land_out, token):
        x, y, c, chips = _place()
        for j, (cx, cy) in enumerate(chips):
            pltpu.make_async_remote_copy(src_ref=sb_ref.at[2 * cx + cy], dst_ref=land_ref.at[j], send_sem=ssem.at[j],
                                         recv_sem=rsem.at[j], device_id=(cx, cy, c), device_id_type=MESH).start()
        token[...] = jnp.zeros_like(token)

    sem3 = pltpu.SemaphoreType.DMA((3,))
    land = lax.empty((3, half, cols), BF16)
    return pl.pallas_call(
        body, name=name,
        out_shape=(sem3, sem3, pltpu.HBM(sb.shape, sb.dtype), pltpu.HBM(land.shape, land.dtype), _sds((8, 128), F32)),
        in_specs=(HBM, HBM, ANY), out_specs=(SEM, SEM, HBM, HBM, VMEM_FULL), input_output_aliases={0: 2, 1: 3},
        compiler_params=pltpu.CompilerParams(has_side_effects=EFFECT),
    )(pltpu.with_memory_space_constraint(sb, pltpu.HBM), pltpu.with_memory_space_constraint(land, pltpu.HBM), _dep(after))


def _rs_wait(sb, land, ssem, rsem, after, name):
    def body(sb_ref, land_ref, ssem_ref, rsem_ref, after_ref, sb_out, land_out):
        x, y, c, chips = _place()
        for j, (cx, cy) in enumerate(chips):
            cp = pltpu.make_async_remote_copy(src_ref=sb_ref.at[2 * cx + cy], dst_ref=land_ref.at[j], send_sem=ssem_ref.at[j],
                                              recv_sem=rsem_ref.at[j], device_id=(cx, cy, c), device_id_type=MESH)
            cp.wait_send()
            cp.wait_recv()

    return pl.pallas_call(
        body, name=name,
        out_shape=(pltpu.HBM(sb.shape, sb.dtype), pltpu.HBM(land.shape, land.dtype)),
        in_specs=(HBM, HBM, SEM, SEM, ANY), out_specs=(HBM, HBM), input_output_aliases={0: 0, 1: 1},
        compiler_params=pltpu.CompilerParams(has_side_effects=EFFECT),
    )(sb, land, ssem, rsem, after)[1]


def _pair_copy(g_ref, land_ref, ssem, rsem):
    x, y, c, _ = _place()
    h = g_ref.shape[1] // 2
    src = g_ref.at[:, pl.ds(pl.multiple_of((1 - c) * h, 8), h), :]
    return pltpu.make_async_remote_copy(src_ref=src, dst_ref=land_ref, send_sem=ssem.at[0], recv_sem=rsem.at[0],
                                        device_id=(x, y, 1 - c), device_id_type=MESH)


def _pair_start(g, name):
    def body(g_ref, land_ref, ssem, rsem, g_out, land_out, token):
        _pair_copy(g_ref, land_ref, ssem, rsem).start()
        token[...] = jnp.zeros_like(token)

    sem1 = pltpu.SemaphoreType.DMA((1,))
    land = lax.empty((N_CHIP, g.shape[1] // 2, g.shape[2]), F32)
    return pl.pallas_call(
        body, name=name,
        out_shape=(sem1, sem1, pltpu.HBM(g.shape, g.dtype), pltpu.HBM(land.shape, land.dtype), _sds((8, 128), F32)),
        in_specs=(HBM, HBM), out_specs=(SEM, SEM, HBM, HBM, VMEM_FULL), input_output_aliases={0: 2, 1: 3},
        compiler_params=pltpu.CompilerParams(has_side_effects=EFFECT),
    )(pltpu.with_memory_space_constraint(g, pltpu.HBM), pltpu.with_memory_space_constraint(land, pltpu.HBM))


def _pair_wait(g, land, ssem, rsem, after, name):
    def body(g_ref, land_ref, ssem_ref, rsem_ref, after_ref, g_out, land_out):
        cp = _pair_copy(g_ref, land_ref, ssem_ref, rsem_ref)
        cp.wait_send()
        cp.wait_recv()

    return pl.pallas_call(
        body, name=name,
        out_shape=(pltpu.HBM(g.shape, g.dtype), pltpu.HBM(land.shape, land.dtype)),
        in_specs=(HBM, HBM, SEM, SEM, ANY), out_specs=(HBM, HBM), input_output_aliases={0: 0, 1: 1},
        compiler_params=pltpu.CompilerParams(has_side_effects=EFFECT),
    )(g, land, ssem, rsem, after)


def _pair_sum(g, rcv, place, name):
    _, rows, cols = g.shape
    half = rows // 2
    tr = 256
    nt = half // tr

    def body(p_ref, g_ref, r_ref, sb_ref, sf_ref):
        s = pl.program_id(1)
        tot = g_ref[0] + r_ref[0]
        sb_ref[0] = tot.astype(BF16)

        @pl.when(s == p_ref[0])
        def _():
            sf_ref[...] = tot

    grid_spec = pltpu.PrefetchScalarGridSpec(
        num_scalar_prefetch=1, grid=(nt, N_CHIP),
        in_specs=[pl.BlockSpec((1, tr, cols), lambda t, s, p: (s, p[1] * nt + t, 0)),
                  pl.BlockSpec((1, tr, cols), lambda t, s, p: (s, t, 0))],
        out_specs=[pl.BlockSpec((1, tr, cols), lambda t, s, p: (s, t, 0)),
                   pl.BlockSpec((tr, cols), lambda t, s, p: (t, 0))])
    return pl.pallas_call(
        body, name=name, grid_spec=grid_spec,
        out_shape=[_sds((N_CHIP, half, cols), BF16), _sds((half, cols), F32)],
        compiler_params=_cp(("arbitrary", "arbitrary")),
    )(place, g, rcv)


def _final_sum(sf, rb, place, name):
    half, cols = sf.shape
    tr = 256
    nt = half // tr

    def body(p_ref, sf_ref, r_ref, out_ref):
        acc = sf_ref[...]
        for j in range(3):
            acc = acc + r_ref[j].astype(F32)
        out_ref[...] = acc

    grid_spec = pltpu.PrefetchScalarGridSpec(
        num_scalar_prefetch=1, grid=(nt,),
        in_specs=[pl.BlockSpec((tr, cols), lambda t, p: (t, 0)), pl.BlockSpec((3, tr, cols), lambda t, p: (0, t, 0))],
        out_specs=pl.BlockSpec((tr, cols), lambda t, p: (p[1] * nt + t, 0)))
    return pl.pallas_call(
        body, name=name, grid_spec=grid_spec,
        out_shape=_sds((2 * half, cols), F32),
        compiler_params=_cp(("arbitrary",)),
    )(place, sf, rb)


def _half_exchange(fulls, name):
    ng = len(fulls)

    def body(*refs):
        outs = refs[ng:2 * ng]
        ssem, rsem = refs[2 * ng], refs[2 * ng + 1]
        x, y, c, _ = _place()
        sib = (x, y, 1 - c)
        cps = []
        for w in range(ng):
            h = outs[w].shape[0] // 2
            mine = outs[w].at[pl.ds(pl.multiple_of(c * h, 8), h), :]
            cps.append(pltpu.make_async_remote_copy(src_ref=mine, dst_ref=mine, send_sem=ssem.at[w], recv_sem=rsem.at[w],
                                                    device_id=sib, device_id_type=MESH))
        for cp in cps:
            cp.start()
        for w in range(ng):
            h = outs[w].shape[0] // 2
            theirs = outs[w].at[pl.ds(pl.multiple_of((1 - c) * h, 8), h), :]
            pltpu.make_async_remote_copy(src_ref=theirs, dst_ref=theirs, send_sem=ssem.at[w], recv_sem=rsem.at[w],
                                         device_id=sib, device_id_type=MESH).wait_recv()
        for cp in cps:
            cp.wait_send()

    return pl.pallas_call(
        body, name=name,
        in_specs=[ANY] * ng, out_specs=[ANY] * ng, out_shape=[_sds(f.shape, F32) for f in fulls],
        input_output_aliases={w: w for w in range(ng)},
        scratch_shapes=[pltpu.SemaphoreType.DMA((ng,)), pltpu.SemaphoreType.DMA((ng,))],
    )(*fulls)


def _small_allreduce(vec):
    nr = vec.shape[0]

    def body(v_ref, o_ref, rsib, pair, rchip, ssem, rsem):
        x, y, c, chips = _place()
        me = 2 * x + y
        sib = (x, y, 1 - c)
        first = pltpu.make_async_remote_copy(src_ref=v_ref, dst_ref=rsib, send_sem=ssem.at[0], recv_sem=rsem.at[0],
                                             device_id=sib, device_id_type=MESH)
        first.start()
        first.wait()
        pair[...] = v_ref[...] + rsib[...]
        cps = [pltpu.make_async_remote_copy(src_ref=pair, dst_ref=rchip.at[j], send_sem=ssem.at[1 + j], recv_sem=rsem.at[1 + j],
                                            device_id=(*chip, c), device_id_type=MESH) for j, chip in enumerate(chips)]
        for cp in cps:
            cp.start()
        for cp in cps:
            cp.wait()
        acc = None
        for kk in range(N_CHIP):
            d = jnp.bitwise_xor(me, kk)
            t = jnp.where(d == 0, pair[...], jnp.where(d == 2, rchip[0], jnp.where(d == 1, rchip[1], rchip[2])))
            acc = t if acc is None else acc + t
        o_ref[...] = acc

    return pl.pallas_call(
        body, name="small_allreduce",
        in_specs=[VMEM_FULL], out_specs=VMEM_FULL, out_shape=_sds((nr, 128), F32),
        scratch_shapes=[pltpu.VMEM((nr, 128), F32), pltpu.VMEM((nr, 128), F32), pltpu.VMEM((3, nr, 128), F32),
                        pltpu.SemaphoreType.DMA((4,)), pltpu.SemaphoreType.DMA((4,))],
        compiler_params=_cp(),
    )(vec)


def _adam_math(w, g, m, v):
    m = B1 * m + (1.0 - B1) * g
    v = B2 * v + (1.0 - B2) * (g * g)
    m_hat = m / (1.0 - B1 ** STEP)
    v_hat = v / (1.0 - B2 ** STEP)
    delta = -LR * (m_hat / (jnp.sqrt(v_hat) + AEPS) + WD * w)
    return delta, m, v


def _adam_big(w, g, m, v, name):
    rows, cols = w.shape
    tr = 128

    def body(w_ref, g_ref, m_ref, v_ref, go_ref, d_ref, nm_ref, nv_ref):
        g = g_ref[...]
        d, nm, nv = _adam_math(w_ref[...], g, m_ref[...], v_ref[...])
        go_ref[...] = g
        d_ref[...] = d
        nm_ref[...] = nm
        nv_ref[...] = nv

    blk = pl.BlockSpec((tr, cols), lambda i: (i, 0))
    return pl.pallas_call(
        body, name=name, grid=(rows // tr,),
        in_specs=[blk] * 4, out_specs=[blk] * 4, out_shape=[_sds((rows, cols), F32)] * 4,
        compiler_params=_cp(("arbitrary",)),
    )(w, g, m, v)


def _adam_small(ws, gs, ms, vs):
    n = len(ws)

    def body(*refs):
        for i in range(n):
            d, nm, nv = _adam_math(refs[i][...], refs[n + i][...], refs[2 * n + i][...], refs[3 * n + i][...])
            refs[4 * n + i][...] = d
            refs[5 * n + i][...] = nm
            refs[6 * n + i][...] = nv

    shapes = [_sds(w.shape, F32) for w in ws]
    outs = pl.pallas_call(
        body, name="adam_small",
        in_specs=[VMEM_FULL] * (4 * n), out_specs=[VMEM_FULL] * (3 * n), out_shape=shapes * 3,
        compiler_params=_cp(),
    )(*ws, *gs, *ms, *vs)
    return outs[:n], outs[n:2 * n], outs[2 * n:]


def _pad_rows8(a):
    flat = a.reshape(-1, 128)
    pad = (-flat.shape[0]) % 8
    if pad:
        flat = jnp.concatenate([flat, jnp.zeros((pad, 128), F32)], axis=0)
    return flat


def kernel(x, meta_tokens, norm1_w, w_in, gate_w2, gate_b, gla_norm_w, pool_w, pool_scale, w_out, norm2_w, mlp_w1, mlp_w2, final_norm_w, loss_target, m_meta_tokens, m_norm1_w, m_w_in, m_gate_w2, m_gate_b, m_gla_norm_w, m_pool_w, m_pool_scale, m_w_out, m_norm2_w, m_mlp_w1, m_mlp_w2, m_final_norm_w, v_meta_tokens, v_norm1_w, v_w_in, v_gate_w2, v_gate_b, v_gla_norm_w, v_pool_w, v_pool_scale, v_w_out, v_norm2_w, v_mlp_w1, v_mlp_w2, v_final_norm_w):
    cx, cy, cc = lax.axis_index("x"), lax.axis_index("y"), lax.axis_index("c")
    me = (2 * cx + cy).astype(jnp.int32)

    place = jnp.stack([me, cc.astype(jnp.int32)])
    fw = final_norm_w.reshape(1, D)

    metaF, gw2F, pwF = _gather_small(meta_tokens, gate_w2[0], pool_w[0])
    (s_win,), (r_win,), (f_win,), tok = _gather_start([_cast_into(w_in[0], place, PAD_IN, "cast_win", gw2F)],
                                                       "gather_start_win")
    rest = [_cast_into(w_out[0], place, D, "cast_wout", tok), _cast_into(mlp_w1[0], place, D, "cast_w1", tok),
            _cast_into(mlp_w2[0], place, D, "cast_w2", tok)]
    ssems, rsems, flying, tok = _gather_start(rest, "gather_start_rest")
    ssems, rsems, flying = [s_win, *ssems], [r_win, *rsems], [f_win, *flying]

    def arrive(i, nm, after):
        return _forward_halves(_gather_wait(flying[i], ssems[i], rsems[i], after, "gather_wait_" + nm), "forward_" + nm)

    pairs, pending = {}, {}

    def grad_start(nm, g):
        ssem, rsem, g_thru, land, token = _pair_start(g, "pair_start_" + nm)
        pairs[nm] = (ssem, rsem, g_thru, land)
        return token

    def grad_finish(nm, after):
        ssem, rsem, g_thru, land = pairs[nm]
        g, rcv = _pair_wait(g_thru, land, ssem, rsem, after, "pair_wait_" + nm)
        sb, sf = _pair_sum(g, rcv, place, "pair_sum_" + nm)
        if nm == "win":
            pending[nm] = (sf, sb)
            return sf
        ssem, rsem, sb_thru, land, token = _rs_start(sb, "rs_start_" + nm)
        pending[nm] = (sf, ssem, rsem, sb_thru, land)
        return token

    (grad_x, loss8, d_n1w, d_gb, d_gnw, d_ps, d_n2w, d_fw, d_meta, d_gw2, d_pw) = _local_step(
        x[0], loss_target[0], lambda after: arrive(0, "win", after), lambda after: arrive(1, "wout", after).reshape(D, D),
        lambda after: arrive(2, "w1", after), lambda after: arrive(3, "w2", after).reshape(DFF, D), metaF, gw2F, pwF,
        norm1_w, gate_b, gla_norm_w, pool_scale, norm2_w, fw, grad_start, grad_finish, tok)
    return _reduce_and_update(
        me, place, pending, grad_x, loss8, d_n1w, d_gb, d_gnw, d_ps, d_n2w, d_fw, d_meta, d_gw2, d_pw,
        meta_tokens, norm1_w, w_in, gate_w2, gate_b, gla_norm_w, pool_w, pool_scale, w_out, norm2_w, mlp_w1, mlp_w2, fw,
        m_meta_tokens, m_norm1_w, m_w_in, m_gate_w2, m_gate_b, m_gla_norm_w, m_pool_w, m_pool_scale, m_w_out, m_norm2_w,
        m_mlp_w1, m_mlp_w2, m_final_norm_w, v_meta_tokens, v_norm1_w, v_w_in, v_gate_w2, v_gate_b, v_gla_norm_w, v_pool_w,
        v_pool_scale, v_w_out, v_norm2_w, v_mlp_w1, v_mlp_w2, v_final_norm_w)


def _local_step(x, target, get_win, get_wout, get_w1, get_w2, metaF, gw2F, pwF, norm1_w, gate_b, gla_norm_w, pool_scale,
                norm2_w, fw, grad_start, grad_finish, first=None):
    h0, u = _embed_norm(x, metaF, norm1_w, first)
    Win = get_win(u)
    P = _in_proj(u, Win)
    q = P[0, :, 0:512]
    k = P[0, :, 512:1024]
    v = jnp.concatenate([P[0, :, 1024:1028], P[1, :, 0:1020]], axis=1)
    r = jnp.concatenate([P[1, :, 1020:1028], P[2, :, 0:1016]], axis=1)
    glr = jnp.concatenate([P[2, :, 1016:1028], P[3, :, 0:4]], axis=1)
    pu = P[3, :, 4:1028]
    o, og, sp = _gla_fwd(q, k, v, r, glr, gw2F, gate_b, gla_norm_w)
    yb, op = _pool_fwd(pu, pwF, pool_scale)
    Wout = get_wout(op)
    h1 = _out_proj(og, op, Wout, h0)
    n2 = _norm_rows(h1, norm2_w, "norm2")
    W1 = get_w1(n2)
    zr, a = _mlp_up(n2, W1)
    W2 = get_w2(a)
    h2 = _mlp_down(a, W2, h1)

    dh2, dh2b, d_fw, loss8 = _loss_head(h2, target, fw)
    tok = grad_start("w2", _grad_w2(a, dh2b).reshape(N_CHIP, D, D))
    dz = _mlp_dz(dh2b, W2, zr, tok)
    tok = grad_finish("w2", dz)
    tok = grad_start("w1", _grad_w1(n2, dz, tok))
    dn2 = _mlp_dn(dz, W1, tok)
    tok = grad_finish("w1", dn2)
    dh1, dh1b, d_n2w = _norm_bwd(dn2, h1, dh2, norm2_w, "norm2_bwd", tok)
    dmixed = _mixed_grad(dh1b, Wout)
    tok = grad_start("wout", _grad_wout(og, op, dh1b))
    dpu, d_pw, d_ps = _pool_bwd(dmixed, yb, pwF, pool_scale, tok)
    dq, dk, dv, dr, dglr, d_gw2, d_gb, d_gnw = _gla_bwd(dmixed, o, q, k, v, r, glr, gw2F, gate_b, gla_norm_w, sp, tok)
    tok = grad_finish("wout", dq)
    zpad = jnp.zeros((TP, PAD_IN - SHARD_IN), F32)
    dP = jnp.stack([
        jnp.concatenate([dq, dk, dv[:, 0:4], zpad], axis=1),
        jnp.concatenate([dv[:, 4:], dr[:, 0:8], zpad], axis=1),
        jnp.concatenate([dr[:, 8:], dglr[:, 0:12], zpad], axis=1),
        jnp.concatenate([dglr[:, 12:], dpu, zpad], axis=1)]).astype(BF16)
    tok = grad_start("win", _grad_win(u, dP, tok))
    du = _in_grad(dP, Win, tok)
    tok = grad_finish("win", du)
    grad_x, d_meta, d_n1w = _input_grad(du, h0, dh1, norm1_w, tok)
    return grad_x, loss8, d_n1w, d_gb, d_gnw, d_ps, d_n2w, d_fw, d_meta, d_gw2, d_pw


def _reduce_and_update(me, place, pending, grad_x, loss8, d_n1w, d_gb, d_gnw, d_ps, d_n2w, d_fw, d_meta, d_gw2, d_pw,
                       meta_tokens, norm1_w, w_in, gate_w2, gate_b, gla_norm_w, pool_w, pool_scale, w_out, norm2_w,
                       mlp_w1, mlp_w2, fw, m_meta_tokens, m_norm1_w, m_w_in, m_gate_w2, m_gate_b, m_gla_norm_w, m_pool_w,
                       m_pool_scale, m_w_out, m_norm2_w, m_mlp_w1, m_mlp_w2, m_final_norm_w, v_meta_tokens, v_norm1_w, v_w_in,
                       v_gate_w2, v_gate_b, v_gla_norm_w, v_pool_w, v_pool_scale, v_w_out, v_norm2_w, v_mlp_w1, v_mlp_w2,
                       v_final_norm_w):
    parts = [loss8, d_n1w, d_gb, d_gnw, d_ps, d_n2w, d_fw, d_meta, d_gw2, d_pw]
    packed = [_pad_rows8(p) for p in parts]
    sizes = [p.shape[0] for p in packed]
    red = _small_allreduce(jnp.concatenate(packed, axis=0))
    offs = [0]
    for s in sizes:
        offs.append(offs[-1] + s)

    def take(i, shape):
        n = 1
        for d in shape:
            n *= d
        return red[offs[i]:offs[i] + n // 128].reshape(shape)

    loss = red[0, 0]
    G_n1w = take(1, (1, D))
    G_gb = take(2, (1, KW))
    G_gnw = take(3, (1, DV))
    G_ps = take(4, (1, PW))
    G_n2w = take(5, (1, D))
    G_fw = take(6, (1, D))
    G_meta = lax.dynamic_slice(take(7, (N_META, D)), (0, me * 512), (N_META, 512))
    G_gw2 = lax.dynamic_slice(take(8, (RANK, KW)), (0, me * 128), (RANK, 128))
    G_pw = lax.dynamic_slice(take(9, (4, GC, GC)), (0, me * 64, 0), (4, 64, GC))

    big = {}
    sf, sb = pending["win"]
    ssem, rsem, sb_thru, land, after = _rs_start(sb, "rs_start_win", red)
    pending["win"] = (sf, ssem, rsem, sb_thru, land)
    for nm, w, m, v in (("w2", mlp_w2, m_mlp_w2, v_mlp_w2), ("w1", mlp_w1, m_mlp_w1, v_mlp_w1),
                        ("wout", w_out, m_w_out, v_w_out), ("win", w_in, m_w_in, v_w_in)):
        sf, ssem, rsem, sb_thru, land = pending[nm]
        rb = _rs_wait(sb_thru, land, ssem, rsem, after, "rs_wait_" + nm)
        full, = _half_exchange([_final_sum(sf, rb, place, "final_sum_" + nm)], "half_exchange_" + nm)
        big[nm] = _adam_big(w[0], full, m[0], v[0], "adam_" + nm)
        after = big[nm][3]
    G_win, d_win, nm_win, nv_win = big["win"]
    G_wout, d_wout, nm_wout, nv_wout = big["wout"]
    G_w1, d_w1, nm_w1, nv_w1 = big["w1"]
    G_w2, d_w2, nm_w2, nv_w2 = big["w2"]
    ws = [meta_tokens, norm1_w, gate_w2[0], gate_b, gla_norm_w, pool_w[0], pool_scale, norm2_w, fw]
    gs = [G_meta, G_n1w, G_gw2, G_gb, G_gnw, G_pw, G_ps, G_n2w, G_fw]
    ms = [m_meta_tokens, m_norm1_w, m_gate_w2[0], m_gate_b, m_gla_norm_w, m_pool_w[0], m_pool_scale, m_norm2_w,
          m_final_norm_w.reshape(1, D)]
    vs = [v_meta_tokens, v_norm1_w, v_gate_w2[0], v_gate_b, v_gla_norm_w, v_pool_w[0], v_pool_scale, v_norm2_w,
          v_final_norm_w.reshape(1, D)]
    ds, nms, nvs = _adam_small(ws, gs, ms, vs)

    def assemble(small, win_, wout_, w1_, w2_):
        meta_, n1_, gw2_, gb_, gnw_, pw_, ps_, n2_, fw_ = small
        return (meta_, n1_, win_[None], gw2_[None], gb_, gnw_, pw_[None], ps_, wout_[None], n2_, w1_[None], w2_[None],
                fw_.reshape(D))

    grads_out = assemble(gs, G_win, G_wout, G_w1, G_w2)
    deltas = assemble(ds, d_win, d_wout, d_w1, d_w2)
    new_m = assemble(nms, nm_win, nm_wout, nm_w1, nm_w2)
    new_v = assemble(nvs, nv_win, nv_wout, nv_w1, nv_w2)
    return (loss, grad_x[None], *grads_out, *deltas, *new_m, *new_v)
```

```python
import functools

import jax
import jax.numpy as jnp
from jax import lax
from jax.experimental import pallas as pl
from jax.experimental.pallas import tpu as pltpu

F32 = jnp.float32
BF16 = jnp.bfloat16

D = 2048
SEQ = 2048
N_META = 16
CH = 64
TP = 2176
NCH = TP // CH
ROW_LO = 112
X_LO = 128
ROW_HI = TP
XT = 128
NXT = TP // XT
HEADS = 4
DK = 128
DV = 256
KW = HEADS * DK
GW = HEADS * DV
RANK = 16
TAU = 16.0
WINDOWS = (2, 4, 8, 16)
PW = 1024
GC = 256
DFF = 8192
EPS = 1e-6
SHARD_IN = 1028
PAD_IN = 1152
N_CHIP = 4

LR = 0.001
B1 = 0.9
B2 = 0.999
AEPS = 1e-08
WD = 0.01
STEP = 10

VMEM_LIMIT = 60 * 1024 * 1024
ANY = pl.BlockSpec(memory_space=pl.ANY)
VMEM_FULL = pl.BlockSpec(memory_space=pltpu.VMEM)
MESH = pl.DeviceIdType.MESH


def _cp(sem=None):
    if sem is None:
        return pltpu.CompilerParams(vmem_limit_bytes=VMEM_LIMIT)
    return pltpu.CompilerParams(dimension_semantics=sem, vmem_limit_bytes=VMEM_LIMIT)


def _dot(a, b):
    return jnp.dot(a, b, preferred_element_type=F32)


def _dot_nt(a, b):
    return lax.dot_general(a, b, (((1,), (1,)), ((), ())), preferred_element_type=F32)


def _dot_tn(a, b):
    return lax.dot_general(a, b, (((0,), (0,)), ((), ())), preferred_element_type=F32)


def _sds(shape, dtype):
    return jax.ShapeDtypeStruct(shape, dtype)


def _embed_norm(x, meta_full, w, dep=None):
    def body(x_ref, meta_ref, w_ref, dep_ref, h_ref, u_ref):
        i = pl.program_id(0)

        @pl.when(i == 0)
        def _():
            h_ref[...] = jnp.zeros_like(h_ref)
            h_ref[ROW_LO:X_LO, :] = meta_ref[...]

        @pl.when(i >= 1)
        def _():
            h_ref[...] = x_ref[...]

        h = h_ref[...]
        r = lax.rsqrt(jnp.mean(h * h, axis=-1, keepdims=True) + EPS)
        u_ref[...] = ((h * r) * w_ref[...]).astype(BF16)

    return pl.pallas_call(
        body, name="embed_norm1", grid=(NXT,),
        in_specs=[pl.BlockSpec((XT, D), lambda i: (jnp.maximum(i - 1, 0), 0)),
                  pl.BlockSpec((N_META, D), lambda i: (0, 0)),
                  pl.BlockSpec((1, D), lambda i: (0, 0)), ANY],
        out_specs=[pl.BlockSpec((XT, D), lambda i: (i, 0)), pl.BlockSpec((XT, D), lambda i: (i, 0))],
        out_shape=[_sds((TP, D), F32), _sds((TP, D), BF16)],
        compiler_params=_cp(("arbitrary",)),
    )(x, meta_full, w, _dep(dep))


def _norm_rows(h, w, name):
    tr = 272

    def body(h_ref, w_ref, o_ref):
        hv = h_ref[...]
        r = lax.rsqrt(jnp.mean(hv * hv, axis=-1, keepdims=True) + EPS)
        o_ref[...] = ((hv * r) * w_ref[...]).astype(BF16)

    return pl.pallas_call(
        body, name=name, grid=(TP // tr,),
        in_specs=[pl.BlockSpec((tr, D), lambda i: (i, 0)), pl.BlockSpec((1, D), lambda i: (0, 0))],
        out_specs=pl.BlockSpec((tr, D), lambda i: (i, 0)),
        out_shape=_sds((TP, D), BF16),
        compiler_params=_cp(("arbitrary",)),
    )(h, w)


def _loss_head(h2, target, fw):
    def body(h_ref, t_ref, w_ref, dh_ref, dhb_ref, dw_ref, loss_ref):
        i = pl.program_id(0)

        @pl.when(i == 0)
        def _():
            dw_ref[...] = jnp.zeros_like(dw_ref)
            loss_ref[...] = jnp.zeros_like(loss_ref)

        h = h_ref[...]
        w = w_ref[...]
        r = lax.rsqrt(jnp.mean(h * h, axis=-1, keepdims=True) + EPS)
        xh = h * r
        y = xh * w
        is_x = (i >= 1).astype(F32)
        diff = (y - t_ref[...]) * is_x
        loss_ref[...] += jnp.sum(diff * diff) * (0.5 / D)
        dy = diff * (1.0 / D)
        dw_ref[...] += jnp.sum(dy * xh, axis=0, keepdims=True)
        gx = dy * w
        dh = r * (gx - xh * jnp.mean(gx * xh, axis=-1, keepdims=True))
        dh_ref[...] = dh
        dhb_ref[...] = dh.astype(BF16)

    return pl.pallas_call(
        body, name="loss_head", grid=(NXT,),
        in_specs=[pl.BlockSpec((XT, D), lambda i: (i, 0)),
                  pl.BlockSpec((XT, D), lambda i: (jnp.maximum(i - 1, 0), 0)),
                  pl.BlockSpec((1, D), lambda i: (0, 0))],
        out_specs=[pl.BlockSpec((XT, D), lambda i: (i, 0)), pl.BlockSpec((XT, D), lambda i: (i, 0)),
                   pl.BlockSpec((1, D), lambda i: (0, 0)), pl.BlockSpec((8, 128), lambda i: (0, 0))],
        out_shape=[_sds((TP, D), F32), _sds((TP, D), BF16), _sds((1, D), F32), _sds((8, 128), F32)],
        compiler_params=_cp(("arbitrary",)),
    )(h2, target, fw)


def _norm_bwd(dn, h, dres, w, name, dep=None):
    tr = 272

    def body(dn_ref, h_ref, dres_ref, w_ref, dep_ref, o_ref, ob_ref, dw_ref):
        @pl.when(pl.program_id(0) == 0)
        def _():
            dw_ref[...] = jnp.zeros_like(dw_ref)

        hv = h_ref[...]
        dnv = dn_ref[...]
        r = lax.rsqrt(jnp.mean(hv * hv, axis=-1, keepdims=True) + EPS)
        xh = hv * r
        dw_ref[...] += jnp.sum(dnv * xh, axis=0, keepdims=True)
        gx = dnv * w_ref[...]
        dh = dres_ref[...] + r * (gx - xh * jnp.mean(gx * xh, axis=-1, keepdims=True))
        o_ref[...] = dh
        ob_ref[...] = dh.astype(BF16)

    row = pl.BlockSpec((tr, D), lambda i: (i, 0))
    vec = pl.BlockSpec((1, D), lambda i: (0, 0))
    return pl.pallas_call(
        body, name=name, grid=(TP // tr,),
        in_specs=[row, row, row, vec, ANY], out_specs=[row, row, vec],
        out_shape=[_sds((TP, D), F32), _sds((TP, D), BF16), _sds((1, D), F32)],
        compiler_params=_cp(("arbitrary",)),
    )(dn, h, dres, w, _dep(dep))


def _input_grad(du, h0, dh1, w, dep=None):
    def body(du_ref, h_ref, dres_ref, w_ref, dep_ref, gx_ref, gm_ref, dw_ref):
        i = pl.program_id(0)

        @pl.when(i == 0)
        def _():
            dw_ref[...] = jnp.zeros_like(dw_ref)

        hv = h_ref[...]
        dnv = du_ref[...]
        r = lax.rsqrt(jnp.mean(hv * hv, axis=-1, keepdims=True) + EPS)
        xh = hv * r
        dw_ref[...] += jnp.sum(dnv * xh, axis=0, keepdims=True)
        g = dnv * w_ref[...]
        dh = dres_ref[...] + r * (g - xh * jnp.mean(g * xh, axis=-1, keepdims=True))

        @pl.when(i == 0)
        def _():
            gm_ref[...] = dh[ROW_LO:X_LO, :]

        @pl.when(i >= 1)
        def _():
            gx_ref[...] = dh

    row = pl.BlockSpec((XT, D), lambda i: (i, 0))
    vec = pl.BlockSpec((1, D), lambda i: (0, 0))
    return pl.pallas_call(
        body, name="input_grad", grid=(NXT,),
        in_specs=[row, row, row, vec, ANY],
        out_specs=[pl.BlockSpec((XT, D), lambda i: (jnp.maximum(i - 1, 0), 0)),
                   pl.BlockSpec((N_META, D), lambda i: (0, 0)), vec],
        out_shape=[_sds((SEQ, D), F32), _sds((N_META, D), F32), _sds((1, D), F32)],
        compiler_params=_cp(("arbitrary",)),
    )(du, h0, dh1, w, _dep(dep))


def _in_proj(u, wg):
    def body(u_ref, w_ref, o_ref):
        o_ref[0] = _dot(u_ref[...], w_ref[0])

    return pl.pallas_call(
        body, name="in_proj", grid=(N_CHIP,),
        in_specs=[VMEM_FULL, pl.BlockSpec((1, D, PAD_IN), lambda k: (k, 0, 0))],
        out_specs=pl.BlockSpec((1, TP, PAD_IN), lambda k: (k, 0, 0)),
        out_shape=_sds((N_CHIP, TP, PAD_IN), F32),
        compiler_params=_cp(("arbitrary",)),
    )(u, wg)


def _out_proj(og, op, wout, h0):
    tn = 512

    def body(og_ref, op_ref, w_ref, h_ref, o_ref):
        acc = _dot(og_ref[...], w_ref[0:GW, :]) + _dot(op_ref[...], w_ref[GW:D, :])
        o_ref[...] = h_ref[...] + acc

    return pl.pallas_call(
        body, name="out_proj", grid=(D // tn,),
        in_specs=[VMEM_FULL, VMEM_FULL, pl.BlockSpec((D, tn), lambda j: (0, j)),
                  pl.BlockSpec((TP, tn), lambda j: (0, j))],
        out_specs=pl.BlockSpec((TP, tn), lambda j: (0, j)),
        out_shape=_sds((TP, D), F32),
        compiler_params=_cp(("arbitrary",)),
    )(og, op, wout, h0)


def _mlp_up(n2, w1g):
    tn = 512
    per = D // tn

    def body(n_ref, w_ref, zr_ref, a_ref):
        z = jnp.maximum(_dot(n_ref[...], w_ref[0]), 0.0)
        zr_ref[...] = z.astype(BF16)
        a_ref[...] = (z * z).astype(BF16)

    col = pl.BlockSpec((TP, tn), lambda k, j: (0, k * per + j))
    return pl.pallas_call(
        body, name="mlp_up", grid=(N_CHIP, per),
        in_specs=[VMEM_FULL, pl.BlockSpec((1, D, tn), lambda k, j: (k, 0, j))],
        out_specs=[col, col],
        out_shape=[_sds((TP, DFF), BF16), _sds((TP, DFF), BF16)],
        compiler_params=_cp(("arbitrary", "arbitrary")),
    )(n2, w1g)


def _mlp_down(a, w2, h1):
    tk = 1024
    nk = DFF // tk

    def body(a_ref, w_ref, h_ref, o_ref, acc_ref):
        k = pl.program_id(0)

        @pl.when(k == 0)
        def _():
            pltpu.sync_copy(h_ref, acc_ref)

        acc_ref[...] += _dot(a_ref[...], w_ref[...])

        @pl.when(k == nk - 1)
        def _():
            pltpu.sync_copy(acc_ref, o_ref)

    return pl.pallas_call(
        body, name="mlp_down", grid=(nk,),
        in_specs=[pl.BlockSpec((TP, tk), lambda k: (0, k)), pl.BlockSpec((tk, D), lambda k: (k, 0)), ANY],
        out_specs=ANY,
        out_shape=_sds((TP, D), F32),
        scratch_shapes=[pltpu.VMEM((TP, D), F32)],
        compiler_params=_cp(("arbitrary",)),
    )(a, w2, h1)


def _mlp_dz(dh2b, w2, zr, dep=None):
    tn = 512

    def body(d_ref, w_ref, z_ref, dep_ref, o_ref):
        da = _dot_nt(d_ref[...], w_ref[...])
        o_ref[...] = (da * (2.0 * z_ref[...].astype(F32))).astype(BF16)

    col = pl.BlockSpec((TP, tn), lambda j: (0, j))
    return pl.pallas_call(
        body, name="mlp_dz", grid=(DFF // tn,),
        in_specs=[VMEM_FULL, pl.BlockSpec((tn, D), lambda j: (j, 0)), col, ANY],
        out_specs=col,
        out_shape=_sds((TP, DFF), BF16),
        compiler_params=_cp(("arbitrary",)),
    )(dh2b, w2, zr, _dep(dep))


def _grad_w2(a, dh2b):
    tm = 512

    def body(a_ref, d_ref, o_ref):
        o_ref[...] = _dot_tn(a_ref[...], d_ref[...])

    return pl.pallas_call(
        body, name="grad_w2", grid=(DFF // tm,),
        in_specs=[pl.BlockSpec((TP, tm), lambda j: (0, j)), VMEM_FULL],
        out_specs=pl.BlockSpec((tm, D), lambda j: (j, 0)),
        out_shape=_sds((DFF, D), F32),
        compiler_params=_cp(("arbitrary",)),
    )(a, dh2b)


def _dep(token):
    return jnp.zeros((8, 128), F32) if token is None else token


def _grad_w1(n2, dz, dep=None):
    tn = 512
    per = D // tn

    def body(n_ref, d_ref, dep_ref, o_ref):
        o_ref[0] = _dot_tn(n_ref[...], d_ref[...])

    return pl.pallas_call(
        body, name="grad_w1", grid=(N_CHIP, per),
        in_specs=[VMEM_FULL, pl.BlockSpec((TP, tn), lambda k, j: (0, k * per + j)), ANY],
        out_specs=pl.BlockSpec((1, D, tn), lambda k, j: (k, 0, j)),
        out_shape=_sds((N_CHIP, D, D), F32),
        compiler_params=_cp(("arbitrary", "arbitrary")),
    )(n2, dz, _dep(dep))


def _mlp_dn(dz, w1g, dep=None):
    tk = 1024
    per = D // tk
    nk = DFF // tk

    def body(d_ref, w_ref, dep_ref, o_ref, acc_ref):
        k = pl.program_id(0)
        part = _dot_nt(d_ref[...], w_ref[0])

        @pl.when(k == 0)
        def _():
            acc_ref[...] = part

        @pl.when(k > 0)
        def _():
            acc_ref[...] += part

        @pl.when(k == nk - 1)
        def _():
            pltpu.sync_copy(acc_ref, o_ref)

    return pl.pallas_call(
        body, name="mlp_dn", grid=(nk,),
        in_specs=[pl.BlockSpec((TP, tk), lambda k: (0, k)),
                  pl.BlockSpec((1, D, tk), lambda k: (k // per, 0, k % per)), ANY],
        out_specs=ANY,
        out_shape=_sds((TP, D), F32),
        scratch_shapes=[pltpu.VMEM((TP, D), F32)],
        compiler_params=_cp(("arbitrary",)),
    )(dz, w1g, _dep(dep))


def _mixed_grad(dh1b, wout):
    tn = 512

    def body(d_ref, w_ref, o_ref):
        o_ref[...] = _dot_nt(d_ref[...], w_ref[...])

    return pl.pallas_call(
        body, name="mixed_grad", grid=(D // tn,),
        in_specs=[VMEM_FULL, pl.BlockSpec((tn, D), lambda j: (j, 0))],
        out_specs=pl.BlockSpec((TP, tn), lambda j: (0, j)),
        out_shape=_sds((TP, D), F32),
        compiler_params=_cp(("arbitrary",)),
    )(dh1b, wout)


def _grad_wout(og, op, dh1b):
    tm = 512

    def body(og_ref, op_ref, d_ref, o_ref):
        j = pl.program_id(0)

        @pl.when(j < 2)
        def _():
            o_ref[0] = _dot_tn(og_ref[...], d_ref[...])

        @pl.when(j >= 2)
        def _():
            o_ref[0] = _dot_tn(op_ref[...], d_ref[...])

    return pl.pallas_call(
        body, name="grad_wout", grid=(N_CHIP,),
        in_specs=[pl.BlockSpec((TP, tm), lambda j: (0, jnp.minimum(j, 1))),
                  pl.BlockSpec((TP, tm), lambda j: (0, jnp.maximum(j - 2, 0))), VMEM_FULL],
        out_specs=pl.BlockSpec((1, tm, D), lambda j: (j, 0, 0)),
        out_shape=_sds((N_CHIP, tm, D), F32),
        compiler_params=_cp(("arbitrary",)),
    )(og, op, dh1b)


def _in_grad(dq, dk, dv, dr, dglr, dpu, wg, dep=None):
    def body(dq_ref, dk_ref, dv_ref, dr_ref, dg_ref, dpu_ref, w_ref, dep_ref, o_ref):
        dv, dr, dg = dv_ref[...], dr_ref[...], dg_ref[...]
        head, tail = slice(0, GW), slice(GW, PAD_IN)
        o_ref[...] = (_dot_nt(dq_ref[...], w_ref[0, :, 0:KW]) + _dot_nt(dk_ref[...], w_ref[0, :, KW:GW])
                      + _dot_nt(dv[:, 0:128], w_ref[0, :, tail])
                      + _dot_nt(dv, w_ref[1, :, head]) + _dot_nt(dr[:, 0:128], w_ref[1, :, tail])
                      + _dot_nt(dr, w_ref[2, :, head]) + _dot_nt(dg, w_ref[2, :, tail])
                      + _dot_nt(dpu_ref[...], w_ref[3, :, head]) + _dot_nt(dg, w_ref[3, :, tail]))

    tn = 512
    return pl.pallas_call(
        body, name="in_grad", grid=(D // tn,),
        in_specs=[VMEM_FULL] * 6 + [pl.BlockSpec((N_CHIP, tn, PAD_IN), lambda j: (0, j, 0)), ANY],
        out_specs=pl.BlockSpec((TP, tn), lambda j: (0, j)),
        out_shape=_sds((TP, D), F32),
        compiler_params=_cp(("arbitrary",)),
    )(dq, dk, dv, dr, dglr, dpu, wg, _dep(dep))


def _grad_win(u, dq, dk, dv, dr, dglr, dpu, dep=None):
    tm = 512

    def body(u_ref, dq_hbm, dk_hbm, dv_hbm, dr_hbm, dg_hbm, dpu_hbm, dep_ref, o_ref, dp_ref, sem):
        k = pl.program_id(0)
        head, tail = slice(0, GW), slice(GW, PAD_IN)

        def bring(pieces):
            cps = [pltpu.make_async_copy(src, dp_ref.at[:, cols], sem.at[i]) for i, (src, cols) in enumerate(pieces)]
            for cp in cps:
                cp.start()
            for cp in cps:
                cp.wait()

        @pl.when(pl.program_id(1) == 0)
        def _():
            @pl.when(k == 0)
            def _():
                bring([(dq_hbm, slice(0, KW)), (dk_hbm, slice(KW, GW)), (dv_hbm.at[:, 0:128], tail)])

            @pl.when(k == 1)
            def _():
                bring([(dv_hbm, head), (dr_hbm.at[:, 0:128], tail)])

            @pl.when(k == 2)
            def _():
                bring([(dr_hbm, head), (dg_hbm, tail)])

            @pl.when(k == 3)
            def _():
                bring([(dpu_hbm, head), (dg_hbm, tail)])

        g = _dot_tn(u_ref[...], dp_ref[...])
        lane = lax.broadcasted_iota(jnp.int32, (tm, PAD_IN), 1)
        for kk in range(N_CHIP):
            @pl.when(k == kk)
            def _(kk=kk):
                if kk == 0:
                    nat = g
                elif kk < 3:
                    nat = pltpu.roll(g, PAD_IN - 4 * kk, 1)
                else:
                    nat = jnp.where(lane < 4, pltpu.roll(g, PAD_IN - (GW + 12), 1), pltpu.roll(g, 4, 1))
                o_ref[0] = nat[:, 0:SHARD_IN]

    return pl.pallas_call(
        body, name="grad_win", grid=(N_CHIP, D // tm),
        in_specs=[pl.BlockSpec((TP, tm), lambda k, m: (0, m))] + [ANY] * 7,
        out_specs=pl.BlockSpec((1, tm, SHARD_IN), lambda k, m: (k, m, 0)),
        out_shape=_sds((N_CHIP, D, SHARD_IN), F32),
        scratch_shapes=[pltpu.VMEM((TP, PAD_IN), BF16), pltpu.SemaphoreType.DMA((3,))],
        compiler_params=_cp(("arbitrary", "arbitrary")),
    )(u, dq, dk, dv, dr, dglr, dpu, _dep(dep))


def _split3(x):
    hi = x.astype(BF16)
    r1 = x - hi.astype(F32)
    mid = r1.astype(BF16)
    lo = (r1 - mid.astype(F32)).astype(BF16)
    return hi, mid, lo


def _tri_sum(tri, x):
    hi, mid, lo = _split3(x)
    return _dot(tri, hi) + _dot(tri, mid) + _dot(tri, lo)


def _gla_common(n, glr, gw2, gb):
    rows = n * CH + lax.broadcasted_iota(jnp.int32, (CH, 1), 0)
    valid = (rows >= ROW_LO) & (rows < ROW_HI)
    g_raw = _dot(glr.astype(BF16), gw2.astype(BF16)) + gb
    logsig = jnp.minimum(g_raw, 0.0) - jnp.log(1.0 + jnp.exp(-jnp.abs(g_raw)))
    logg = jnp.where(valid, logsig * (1.0 / TAU), 0.0)
    ci = lax.broadcasted_iota(jnp.int32, (CH, CH), 0)
    si = lax.broadcasted_iota(jnp.int32, (CH, CH), 1)
    lower = ci >= si
    G = _tri_sum(lower.astype(BF16), logg)
    Gl = G[CH - 1:CH, :]
    return valid, g_raw, lower, G, Gl


def _p_specs(index):
    def spec(width, shard, col):
        return pl.BlockSpec((1, CH, width), lambda s: (shard, index(s), col))

    return [spec(KW, 0, 0), spec(KW, 0, 1), spec(GW, 1, 0), spec(128, 0, 8), spec(GW, 2, 0), spec(128, 1, 8),
            spec(128, 2, 8), spec(128, 3, 8)]


def _p_load(q_ref, k_ref, vm_ref, vh_ref, rm_ref, rh_ref, ga_ref, gb_ref):
    def joined(main, head):
        return jnp.concatenate([main[:, 0:128] + head, main[:, 128:]], axis=1)

    return q_ref[0], k_ref[0], joined(vm_ref[0], vh_ref[0]), joined(rm_ref[0], rh_ref[0]), ga_ref[0] + gb_ref[0]


def _gla_fwd(P, gw2, gb, gnw):
    scale = DK ** -0.5

    def body(p0, p1, p2, p3, p4, p5, p6, p7, gw2_ref, gb_ref, gnw_ref, o_ref, og_ref, sp_ref, st_ref):
        n = pl.program_id(0)

        @pl.when(n == 0)
        def _():
            st_ref[...] = jnp.zeros_like(st_ref)

        q_all, k_all, v_all, r_all, glr = _p_load(p0, p1, p2, p3, p4, p5, p6, p7)
        _, _, lower, G, Gl = _gla_common(n, glr, gw2_ref[...], gb_ref[...])
        eG = jnp.exp(G)
        eN = jnp.exp(-G)
        eE = jnp.exp(Gl - G)
        dec = jnp.exp(Gl)
        gnw_v = gnw_ref[...]
        for h in range(HEADS):
            ks = slice(h * DK, (h + 1) * DK)
            vs = slice(h * DV, (h + 1) * DV)
            kh = k_all[:, ks]
            vh = v_all[:, vs].astype(BF16)
            qd = ((q_all[:, ks] * scale) * eG[:, ks]).astype(BF16)
            ki = (kh * eN[:, ks]).astype(BF16)
            ke = (kh * eE[:, ks]).astype(BF16)
            st = st_ref[h]
            a = jnp.where(lower, _dot_nt(qd, ki), 0.0).astype(BF16)
            o = _dot(a, vh) + _dot_nt(qd, st.astype(BF16))
            sp_ref[0, h] = st
            st_ref[h] = st * dec[:, ks] + _dot_tn(vh, ke)
            o_ref[:, vs] = o
            rs = lax.rsqrt(jnp.mean(o * o, axis=-1, keepdims=True) + EPS)
            rv = r_all[:, vs]
            gate = rv / (1.0 + jnp.exp(-rv))
            og_ref[:, vs] = (((o * rs) * gnw_v) * gate).astype(BF16)

    rv_ = pl.BlockSpec((CH, GW), lambda n: (n, 0))

    def full(shape):
        return pl.BlockSpec(shape, lambda n: tuple(0 for _ in shape))

    return pl.pallas_call(
        body, name="gla_fwd", grid=(NCH,),
        in_specs=_p_specs(lambda n: n) + [full((128, KW)), full((1, KW)), full((1, DV))],
        out_specs=[rv_, rv_, pl.BlockSpec((1, HEADS, DV, DK), lambda n: (n, 0, 0, 0))],
        out_shape=[_sds((TP, GW), F32), _sds((TP, GW), BF16), _sds((NCH, HEADS, DV, DK), F32)],
        scratch_shapes=[pltpu.VMEM((HEADS, DV, DK), F32)],
        compiler_params=_cp(("arbitrary",)),
    )(*([P] * 8), gw2, gb, gnw)


def _gla_bwd(dog, o, P, gw2, gb, gnw, sp, dep=None):
    scale = DK ** -0.5

    def body(dog_ref, o_ref, p0, p1, p2, p3, p4, p5, p6, p7, gw2_ref, gb_ref, gnw_ref, sp_ref, dep_ref,
             dq_ref, dk_ref, dv_ref, dr_ref, dglr_ref, dgw2_ref, dgb_ref, dgnw_ref, ds_ref):
        step = pl.program_id(0)
        n = NCH - 1 - step

        @pl.when(step == 0)
        def _():
            ds_ref[...] = jnp.zeros_like(ds_ref)
            dgw2_ref[...] = jnp.zeros_like(dgw2_ref)
            dgb_ref[...] = jnp.zeros_like(dgb_ref)
            dgnw_ref[...] = jnp.zeros_like(dgnw_ref)

        q_all, k_all, v_all, r_all, glr_v = _p_load(p0, p1, p2, p3, p4, p5, p6, p7)
        gw2_b = gw2_ref[...].astype(BF16)
        valid, g_raw, lower, G, Gl = _gla_common(n, glr_v, gw2_ref[...], gb_ref[...])
        upper = lax.broadcasted_iota(jnp.int32, (CH, CH), 0) <= lax.broadcasted_iota(jnp.int32, (CH, CH), 1)
        eG = jnp.exp(G)
        eN = jnp.exp(-G)
        eE = jnp.exp(Gl - G)
        dec = jnp.exp(Gl)
        gnw_v = gnw_ref[...]
        last = lax.broadcasted_iota(jnp.int32, (CH, 1), 0) == CH - 1
        dgnw_acc = jnp.zeros((1, DV), F32)
        dG_parts = []
        for h in range(HEADS):
            ks = slice(h * DK, (h + 1) * DK)
            vs = slice(h * DV, (h + 1) * DV)
            oh = o_ref[:, vs]
            rv = r_all[:, vs]
            dg = dog_ref[:, vs]
            sig = 1.0 / (1.0 + jnp.exp(-rv))
            gate = rv * sig
            rs = lax.rsqrt(jnp.mean(oh * oh, axis=-1, keepdims=True) + EPS)
            ohat = oh * rs
            dr_ref[:, vs] = ((dg * (ohat * gnw_v)) * (sig * (1.0 + rv * (1.0 - sig)))).astype(BF16)
            don = dg * gate
            dgnw_acc = dgnw_acc + jnp.sum(don * ohat, axis=0, keepdims=True)
            gxn = don * gnw_v
            do = (rs * (gxn - ohat * jnp.mean(gxn * ohat, axis=-1, keepdims=True))).astype(BF16)
            kh = k_all[:, ks]
            vh = v_all[:, vs].astype(BF16)
            qd_f = (q_all[:, ks] * scale) * eG[:, ks]
            ki_f = kh * eN[:, ks]
            ke_f = kh * eE[:, ks]
            qd, ki, ke = qd_f.astype(BF16), ki_f.astype(BF16), ke_f.astype(BF16)
            spt = sp_ref[0, h]
            dst = ds_ref[h]
            dst_b = dst.astype(BF16)
            a_t = jnp.where(upper, _dot_nt(ki, qd), 0.0).astype(BF16)
            da = jnp.where(lower, _dot_nt(do, vh), 0.0).astype(BF16)
            da_t = jnp.where(upper, _dot_nt(vh, do), 0.0).astype(BF16)
            dv_ref[:, vs] = (_dot(a_t, do) + _dot_nt(ke, dst_b)).astype(BF16)
            dqd = _dot(da, ki) + _dot(do, spt.astype(BF16))
            dki = _dot(da_t, qd)
            dke = _dot(vh, dst_b)
            ddec = jnp.sum(spt * dst, axis=0, keepdims=True)
            ds_ref[h] = dst * dec[:, ks] + _dot_tn(do, qd)
            dq_ref[:, ks] = ((dqd * eG[:, ks]) * scale).astype(BF16)
            dk_ref[:, ks] = (dki * eN[:, ks] + dke * eE[:, ks]).astype(BF16)
            dke_ke = dke * ke_f
            dG = dqd * qd_f - dki * ki_f - dke_ke
            dGl = jnp.sum(dke_ke, axis=0, keepdims=True) + ddec * dec[:, ks]
            dG_parts.append(dG + jnp.where(last, dGl, 0.0))
        dgnw_ref[...] += dgnw_acc
        dG_all = jnp.concatenate(dG_parts, axis=1)
        dlogg = jnp.where(valid, _tri_sum(upper.astype(BF16), dG_all), 0.0)
        dg_raw = (dlogg * (1.0 / TAU)) * (1.0 / (1.0 + jnp.exp(g_raw)))
        dgb_ref[...] += jnp.sum(dg_raw, axis=0, keepdims=True)
        dg_b = dg_raw.astype(BF16)
        dgw2_ref[...] += _dot_tn(glr_v.astype(BF16), dg_b)
        dglr_ref[...] = _dot_nt(dg_b, gw2_b).astype(BF16)

    def back(s):
        return NCH - 1 - s

    rk = pl.BlockSpec((CH, KW), lambda s: (back(s), 0))
    rv_ = pl.BlockSpec((CH, GW), lambda s: (back(s), 0))
    rg = pl.BlockSpec((CH, 128), lambda s: (back(s), 0))

    def full(shape):
        return pl.BlockSpec(shape, lambda s: tuple(0 for _ in shape))

    return pl.pallas_call(
        body, name="gla_bwd", grid=(NCH,),
        in_specs=[rv_, rv_] + _p_specs(back) + [full((128, KW)), full((1, KW)), full((1, DV)),
                  pl.BlockSpec((1, HEADS, DV, DK), lambda s: (back(s), 0, 0, 0)), ANY],
        out_specs=[rk, rk, rv_, rv_, rg, full((128, KW)), full((1, KW)), full((1, DV))],
        out_shape=[_sds((TP, KW), BF16), _sds((TP, KW), BF16), _sds((TP, GW), BF16), _sds((TP, GW), BF16),
                   _sds((TP, 128), BF16), _sds((128, KW), F32), _sds((1, KW), F32), _sds((1, DV), F32)],
        scratch_shapes=[pltpu.VMEM((HEADS, DV, DK), F32)],
        compiler_params=_cp(("arbitrary",)),
    )(dog, o, *([P] * 8), gw2, gb, gnw, sp, _dep(dep))


POOL_TR = 128
HALO = 16


def _pool_counts(base, nrows):
    rows = base + lax.broadcasted_iota(jnp.int32, (nrows, 1), 0)
    valid = (rows >= ROW_LO) & (rows < ROW_HI)
    t1 = (rows - ROW_LO + 1).astype(F32)
    cnts = [jnp.clip(t1, 1.0, float(w)) for w in WINDOWS]
    return valid, cnts


def _pool_fwd(P, pw, ps):
    def body(cur_ref, prev_ref, pw_ref, ps_ref, y_ref, op_ref):
        i = pl.program_id(0)
        cur = cur_ref[0]
        full = jnp.concatenate([prev_ref[0], cur], axis=0)
        s2 = full + pltpu.roll(full, 1, 0)
        s4 = s2 + pltpu.roll(s2, 2, 0)
        s8 = s4 + pltpu.roll(s4, 4, 0)
        s16 = s8 + pltpu.roll(s8, 8, 0)
        valid, cnts = _pool_counts(i * POOL_TR, POOL_TR)
        for g, s in enumerate((s2, s4, s8, s16)):
            cs = slice(g * GC, (g + 1) * GC)
            y = s[HALO:, cs] / cnts[g] - cur[:, cs]
            yb = jnp.where(valid, y, 0.0).astype(BF16)
            y_ref[:, cs] = yb
            op_ref[:, cs] = (_dot(yb, pw_ref[g].astype(BF16)) * ps_ref[:, cs]).astype(BF16)

    row = pl.BlockSpec((POOL_TR, PW), lambda i: (i, 0))
    per = POOL_TR // HALO
    return pl.pallas_call(
        body, name="pool_fwd", grid=(TP // POOL_TR,),
        in_specs=[pl.BlockSpec((1, POOL_TR, PW), lambda i: (3, i, 0)),
                  pl.BlockSpec((1, HALO, PW), lambda i: (3, jnp.maximum(i * per - 1, 0), 0)),
                  pl.BlockSpec((4, GC, GC), lambda i: (0, 0, 0)), pl.BlockSpec((1, PW), lambda i: (0, 0))],
        out_specs=[row, row],
        out_shape=[_sds((TP, PW), BF16), _sds((TP, PW), BF16)],
        compiler_params=_cp(("arbitrary",)),
    )(P, P, pw, ps)


def _pool_bwd(dop, y, pw, ps, dep=None):
    nblk = TP // HALO

    def body(cur_ref, nxt_ref, y_ref, pw_ref, ps_ref, dep_ref, dpu_ref, dpw_ref, dps_ref):
        i = pl.program_id(0)

        @pl.when(i == 0)
        def _():
            dpw_ref[...] = jnp.zeros_like(dpw_ref)
            dps_ref[...] = jnp.zeros_like(dps_ref)

        n_all = POOL_TR + HALO
        dcur = cur_ref[...]
        dall = jnp.concatenate([dcur, nxt_ref[...]], axis=0)
        valid, cnts = _pool_counts(i * POOL_TR, n_all)
        for g in range(4):
            cs = slice(g * GC, (g + 1) * GC)
            pwb = pw_ref[g].astype(BF16)
            yb = y_ref[:, cs]
            dyw = (dall[:, cs] * ps_ref[:, cs]).astype(BF16)
            dps_ref[:, cs] += jnp.sum(dcur[:, cs] * _dot(yb, pwb), axis=0, keepdims=True)
            dpw_ref[g] += _dot_tn(yb, dyw[0:POOL_TR, :])
            dyv = jnp.where(valid, _dot_nt(dyw, pwb), 0.0)
            e = dyv / cnts[g]
            w = WINDOWS[g]
            sh = 1
            while sh < w:
                e = e + pltpu.roll(e, n_all - sh, 0)
                sh *= 2
            dpu_ref[:, cs] = (e[0:POOL_TR, :] - dyv[0:POOL_TR, :]).astype(BF16)

    row = pl.BlockSpec((POOL_TR, PW), lambda i: (i, 0))
    per = POOL_TR // HALO
    return pl.pallas_call(
        body, name="pool_bwd", grid=(TP // POOL_TR,),
        in_specs=[pl.BlockSpec((POOL_TR, PW), lambda i: (i, 1)),
                  pl.BlockSpec((HALO, PW), lambda i: (jnp.minimum(i * per + per, nblk - 1), 1)),
                  row, pl.BlockSpec((4, GC, GC), lambda i: (0, 0, 0)), pl.BlockSpec((1, PW), lambda i: (0, 0)), ANY],
        out_specs=[row, pl.BlockSpec((4, GC, GC), lambda i: (0, 0, 0)), pl.BlockSpec((1, PW), lambda i: (0, 0))],
        out_shape=[_sds((TP, PW), BF16), _sds((4, GC, GC), F32), _sds((1, PW), F32)],
        compiler_params=_cp(("arbitrary",)),
    )(dop, dop, y, pw, ps, _dep(dep))


def _place():
    x, y, c = lax.axis_index("x"), lax.axis_index("y"), lax.axis_index("c")
    chips = [(1 - x, y), (x, 1 - y), (1 - x, 1 - y)]
    return x, y, c, chips


HBM = pl.BlockSpec(memory_space=pltpu.HBM)
SEM = pl.BlockSpec(memory_space=pltpu.SEMAPHORE)
EFFECT = pltpu.SideEffectType.DATAFLOW_SIDE_EFFECTING


def _cast_into(w, place, cols_out, name, dep=None):
    rows, cols = w.shape
    tr = 256

    def body(p_ref, w_ref, dep_ref, o_ref):
        if cols_out != cols:
            o_ref[0] = jnp.zeros((tr, cols_out), BF16)
            o_ref[0, :, 0:cols] = w_ref[...].astype(BF16)
        else:
            o_ref[0] = w_ref[...].astype(BF16)

    grid_spec = pltpu.PrefetchScalarGridSpec(
        num_scalar_prefetch=1, grid=(rows // tr,),
        in_specs=[pl.BlockSpec((tr, cols), lambda i, p: (i, 0)), ANY],
        out_specs=pl.BlockSpec((1, tr, cols_out), lambda i, p: (p[0], i, 0)))
    return pl.pallas_call(
        body, name=name, grid_spec=grid_spec,
        out_shape=_sds((N_CHIP, rows, cols_out), BF16),
        compiler_params=_cp(("arbitrary",)),
    )(place, w, _dep(dep))


def _cast_win(w, place, dep=None):
    rows, cols = w.shape
    tr = 256

    def body(p_ref, w_ref, dep_ref, o_ref, t_ref):
        t_ref[...] = jnp.zeros_like(t_ref)
        t_ref[:, 0:cols] = w_ref[...]
        t = t_ref[...]
        lane = lax.broadcasted_iota(jnp.int32, (tr, PAD_IN), 1)
        for kk in range(N_CHIP):
            @pl.when(p_ref[0] == kk)
            def _(kk=kk):
                if kk == 0:
                    placed = t
                elif kk < 3:
                    placed = pltpu.roll(t, 4 * kk, 1)
                else:
                    pool = pltpu.roll(t, PAD_IN - 4, 1)
                    gate = pltpu.roll(t, GW + 12, 1)
                    placed = jnp.where(lane < GW, pool, jnp.where((lane >= GW + 12) & (lane < GW + 16), gate, 0.0))
                o_ref[0] = placed.astype(BF16)

    grid_spec = pltpu.PrefetchScalarGridSpec(
        num_scalar_prefetch=1, grid=(rows // tr,),
        in_specs=[pl.BlockSpec((tr, cols), lambda i, p: (i, 0)), ANY],
        out_specs=pl.BlockSpec((1, tr, PAD_IN), lambda i, p: (p[0], i, 0)),
        scratch_shapes=[pltpu.VMEM((tr, PAD_IN), F32)])
    return pl.pallas_call(
        body, name="cast_win", grid_spec=grid_spec,
        out_shape=_sds((N_CHIP, rows, PAD_IN), BF16),
        compiler_params=_cp(("arbitrary",)),
    )(place, w, _dep(dep))


def _half_rows(ref, k, which):
    h = ref.shape[1] // 2
    return ref.at[k, pl.ds(pl.multiple_of(which * h, 8), h), :]


def _gather_small(meta, gw2, pw, dep=None):
    def body(meta_r, gw2_r, pw_r, dep_ref, metaF, gw2F, pwF, lsem, ssem, rsem):
        x, y, c, chips = _place()
        me = 2 * x + y

        def slots(k):
            return (metaF.at[:, pl.ds(pl.multiple_of(k * 512, 128), 512)],
                    gw2F.at[:, pl.ds(pl.multiple_of(k * 128, 128), 128)],
                    pwF.at[:, pl.ds(pl.multiple_of(k * 64, 8), 64), :])

        srcs = (meta_r, gw2_r, pw_r)
        local = [pltpu.make_async_copy(s, d, lsem.at[i]) for i, (s, d) in enumerate(zip(srcs, slots(me)))]
        sends = []
        for j, chip in enumerate(chips):
            for i, (s, d) in enumerate(zip(srcs, slots(me))):
                sends.append(pltpu.make_async_remote_copy(src_ref=s, dst_ref=d, send_sem=ssem.at[3 * j + i],
                                                          recv_sem=rsem.at[3 * j + i], device_id=(*chip, c),
                                                          device_id_type=MESH))
        for cp in local + sends:
            cp.start()
        for j, (cx, cy) in enumerate(chips):
            for i, (s, d) in enumerate(zip(srcs, slots(2 * cx + cy))):
                pltpu.make_async_remote_copy(src_ref=s, dst_ref=d, send_sem=ssem.at[3 * j + i], recv_sem=rsem.at[3 * j + i],
                                             device_id=(cx, cy, c), device_id_type=MESH).wait_recv()
        for cp in sends:
            cp.wait_send()
        for cp in local:
            cp.wait()

    return pl.pallas_call(
        body, name="gather_small",
        in_specs=[ANY] * 4, out_specs=[ANY] * 3,
        out_shape=[_sds((N_META, D), F32), _sds((RANK, KW), F32), _sds((4, GC, GC), F32)],
        scratch_shapes=[pltpu.SemaphoreType.DMA((3,)), pltpu.SemaphoreType.DMA((9,)), pltpu.SemaphoreType.DMA((9,))],
    )(meta, gw2, pw, _dep(dep))


def _gather_start(ws, name):
    n = len(ws)

    def body(*refs):
        ins = refs[:n]
        ssems = refs[n:2 * n]
        rsems = refs[2 * n:3 * n]
        token = refs[4 * n]
        x, y, c, chips = _place()
        me = 2 * x + y
        for w in range(n):
            blk = _half_rows(ins[w], me, c)
            for j, chip in enumerate(chips):
                pltpu.make_async_remote_copy(src_ref=blk, dst_ref=blk, send_sem=ssems[w].at[j], recv_sem=rsems[w].at[j],
                                             device_id=(*chip, c), device_id_type=MESH).start()
        token[...] = jnp.zeros_like(token)

    sem3 = pltpu.SemaphoreType.DMA((3,))
    outs = pl.pallas_call(
        body, name=name,
        out_shape=tuple([sem3] * (2 * n) + [pltpu.HBM(w.shape, w.dtype) for w in ws] + [_sds((8, 128), F32)]),
        in_specs=(HBM,) * n, out_specs=(SEM,) * (2 * n) + (HBM,) * n + (VMEM_FULL,),
        input_output_aliases={w: 2 * n + w for w in range(n)},
        compiler_params=pltpu.CompilerParams(has_side_effects=EFFECT),
    )(*[pltpu.with_memory_space_constraint(w, pltpu.HBM) for w in ws])
    return outs[:n], outs[n:2 * n], outs[2 * n:3 * n], outs[3 * n]


def _gather_wait(w, ssem, rsem, after, name):
    def body(w_ref, ssem_ref, rsem_ref, after_ref, out_ref):
        x, y, c, chips = _place()
        me = 2 * x + y
        mine = _half_rows(w_ref, me, c)
        for j, (cx, cy) in enumerate(chips):
            cp = pltpu.make_async_remote_copy(src_ref=mine, dst_ref=_half_rows(w_ref, 2 * cx + cy, c),
                                              send_sem=ssem_ref.at[j], recv_sem=rsem_ref.at[j],
                                              device_id=(cx, cy, c), device_id_type=MESH)
            cp.wait_send()
            cp.wait_recv()

    return pl.pallas_call(
        body, name=name, out_shape=pltpu.HBM(w.shape, w.dtype),
        in_specs=(HBM, SEM, SEM, ANY), out_specs=HBM, input_output_aliases={0: 0},
        compiler_params=pltpu.CompilerParams(has_side_effects=EFFECT),
    )(w, ssem, rsem, after)


def _forward_halves(w, name):
    def body(w_ref, o_ref, ssem, rsem):
        x, y, c, chips = _place()
        sib = (x, y, 1 - c)
        cps = []
        for j, (cx, cy) in enumerate(chips):
            blk = _half_rows(o_ref, 2 * cx + cy, c)
            cps.append(pltpu.make_async_remote_copy(src_ref=blk, dst_ref=blk, send_sem=ssem.at[j], recv_sem=rsem.at[j],
                                                    device_id=sib, device_id_type=MESH))
        for cp in cps:
            cp.start()
        for j, (cx, cy) in enumerate(chips):
            blk = _half_rows(o_ref, 2 * cx + cy, 1 - c)
            pltpu.make_async_remote_copy(src_ref=blk, dst_ref=blk, send_sem=ssem.at[j], recv_sem=rsem.at[j],
                                         device_id=sib, device_id_type=MESH).wait_recv()
        for cp in cps:
            cp.wait_send()

    return pl.pallas_call(
        body, name=name, in_specs=[ANY], out_specs=ANY, out_shape=_sds(w.shape, w.dtype),
        input_output_aliases={0: 0},
        scratch_shapes=[pltpu.SemaphoreType.DMA((3,)), pltpu.SemaphoreType.DMA((3,))],
    )(w)


def _rs_start(sb, name, after=None):
    _, half, cols = sb.shape

    def body(sb_ref, land_ref, after_ref, ssem, rsem, sb_out, land_out, token):
        x, y, c, chips = _place()
        for j, (cx, cy) in enumerate(chips):
            pltpu.make_async_remote_copy(src_ref=sb_ref.at[2 * cx + cy], dst_ref=land_ref.at[j], send_sem=ssem.at[j],
                                         recv_sem=rsem.at[j], device_id=(cx, cy, c), device_id_type=MESH).start()
        token[...] = jnp.zeros_like(token)

    sem3 = pltpu.SemaphoreType.DMA((3,))
    land = lax.empty((3, half, cols), BF16)
    return pl.pallas_call(
        body, name=name,
        out_shape=(sem3, sem3, pltpu.HBM(sb.shape, sb.dtype), pltpu.HBM(land.shape, land.dtype), _sds((8, 128), F32)),
        in_specs=(HBM, HBM, ANY), out_specs=(SEM, SEM, HBM, HBM, VMEM_FULL), input_output_aliases={0: 2, 1: 3},
        compiler_params=pltpu.CompilerParams(has_side_effects=EFFECT),
    )(pltpu.with_memory_space_constraint(sb, pltpu.HBM), pltpu.with_memory_space_constraint(land, pltpu.HBM), _dep(after))


def _rs_wait(sb, land, ssem, rsem, after, name):
    def body(sb_ref, land_ref, ssem_ref, rsem_ref, after_ref, sb_out, land_out):
        x, y, c, chips = _place()
        for j, (cx, cy) in enumerate(chips):
            cp = pltpu.make_async_remote_copy(src_ref=sb_ref.at[2 * cx + cy], dst_ref=land_ref.at[j], send_sem=ssem_ref.at[j],
                                              recv_sem=rsem_ref.at[j], device_id=(cx, cy, c), device_id_type=MESH)
            cp.wait_send()
            cp.wait_recv()

    return pl.pallas_call(
        body, name=name,
        out_shape=(pltpu.HBM(sb.shape, sb.dtype), pltpu.HBM(land.shape, land.dtype)),
        in_specs=(HBM, HBM, SEM, SEM, ANY), out_specs=(HBM, HBM), input_output_aliases={0: 0, 1: 1},
        compiler_params=pltpu.CompilerParams(has_side_effects=EFFECT),
    )(sb, land, ssem, rsem, after)[1]


def _pair_copy(g_ref, land_ref, ssem, rsem):
    x, y, c, _ = _place()
    h = g_ref.shape[1] // 2
    src = g_ref.at[:, pl.ds(pl.multiple_of((1 - c) * h, 8), h), :]
    return pltpu.make_async_remote_copy(src_ref=src, dst_ref=land_ref, send_sem=ssem.at[0], recv_sem=rsem.at[0],
                                        device_id=(x, y, 1 - c), device_id_type=MESH)


def _pair_start(g, name):
    def body(g_ref, land_ref, ssem, rsem, g_out, land_out, token):
        _pair_copy(g_ref, land_ref, ssem, rsem).start()
        token[...] = jnp.zeros_like(token)

    sem1 = pltpu.SemaphoreType.DMA((1,))
    land = lax.empty((N_CHIP, g.shape[1] // 2, g.shape[2]), F32)
    return pl.pallas_call(
        body, name=name,
        out_shape=(sem1, sem1, pltpu.HBM(g.shape, g.dtype), pltpu.HBM(land.shape, land.dtype), _sds((8, 128), F32)),
        in_specs=(HBM, HBM), out_specs=(SEM, SEM, HBM, HBM, VMEM_FULL), input_output_aliases={0: 2, 1: 3},
        compiler_params=pltpu.CompilerParams(has_side_effects=EFFECT),
    )(pltpu.with_memory_space_constraint(g, pltpu.HBM), pltpu.with_memory_space_constraint(land, pltpu.HBM))


def _pair_wait(g, land, ssem, rsem, after, name):
    def body(g_ref, land_ref, ssem_ref, rsem_ref, after_ref, g_out, land_out):
        cp = _pair_copy(g_ref, land_ref, ssem_ref, rsem_ref)
        cp.wait_send()
        cp.wait_recv()

    return pl.pallas_call(
        body, name=name,
        out_shape=(pltpu.HBM(g.shape, g.dtype), pltpu.HBM(land.shape, land.dtype)),
        in_specs=(HBM, HBM, SEM, SEM, ANY), out_specs=(HBM, HBM), input_output_aliases={0: 0, 1: 1},
        compiler_params=pltpu.CompilerParams(has_side_effects=EFFECT),
    )(g, land, ssem, rsem, after)


def _pair_sum(g, rcv, place, name):
    _, rows, cols = g.shape
    half = rows // 2
    tr = 256
    nt = half // tr

    def body(p_ref, g_ref, r_ref, sb_ref, sf_ref):
        s = pl.program_id(1)
        tot = g_ref[0] + r_ref[0]
        sb_ref[0] = tot.astype(BF16)

        @pl.when(s == p_ref[0])
        def _():
            sf_ref[...] = tot

    grid_spec = pltpu.PrefetchScalarGridSpec(
        num_scalar_prefetch=1, grid=(nt, N_CHIP),
        in_specs=[pl.BlockSpec((1, tr, cols), lambda t, s, p: (s, p[1] * nt + t, 0)),
                  pl.BlockSpec((1, tr, cols), lambda t, s, p: (s, t, 0))],
        out_specs=[pl.BlockSpec((1, tr, cols), lambda t, s, p: (s, t, 0)),
                   pl.BlockSpec((tr, cols), lambda t, s, p: (t, 0))])
    return pl.pallas_call(
        body, name=name, grid_spec=grid_spec,
        out_shape=[_sds((N_CHIP, half, cols), BF16), _sds((half, cols), F32)],
        compiler_params=_cp(("arbitrary", "arbitrary")),
    )(place, g, rcv)


def _final_sum(sf, rb, place, name):
    half, cols = sf.shape
    tr = 256
    nt = half // tr

    def body(p_ref, sf_ref, r_ref, out_ref):
        acc = sf_ref[...]
        for j in range(3):
            acc = acc + r_ref[j].astype(F32)
        out_ref[...] = acc

    grid_spec = pltpu.PrefetchScalarGridSpec(
        num_scalar_prefetch=1, grid=(nt,),
        in_specs=[pl.BlockSpec((tr, cols), lambda t, p: (t, 0)), pl.BlockSpec((3, tr, cols), lambda t, p: (0, t, 0))],
        out_specs=pl.BlockSpec((tr, cols), lambda t, p: (p[1] * nt + t, 0)))
    return pl.pallas_call(
        body, name=name, grid_spec=grid_spec,
        out_shape=_sds((2 * half, cols), F32),
        compiler_params=_cp(("arbitrary",)),
    )(place, sf, rb)


def _half_copy(f_ref, which, ssem, rsem):
    x, y, c, _ = _place()
    h = f_ref.shape[0] // 2
    rows = f_ref.at[pl.ds(pl.multiple_of(which * h, 8), h), :]
    return pltpu.make_async_remote_copy(src_ref=rows, dst_ref=rows, send_sem=ssem.at[0], recv_sem=rsem.at[0],
                                        device_id=(x, y, 1 - c), device_id_type=MESH)


def _half_start(full, name, after=None):
    def body(f_ref, after_ref, ssem, rsem, f_out, token):
        _half_copy(f_ref, lax.axis_index("c"), ssem, rsem).start()
        token[...] = jnp.zeros_like(token)

    sem1 = pltpu.SemaphoreType.DMA((1,))
    return pl.pallas_call(
        body, name=name,
        out_shape=(sem1, sem1, pltpu.HBM(full.shape, full.dtype), _sds((8, 128), F32)),
        in_specs=(HBM, ANY), out_specs=(SEM, SEM, HBM, VMEM_FULL), input_output_aliases={0: 2},
        compiler_params=pltpu.CompilerParams(has_side_effects=EFFECT),
    )(pltpu.with_memory_space_constraint(full, pltpu.HBM), _dep(after))


def _half_wait(full, ssem, rsem, after, name):
    def body(f_ref, ssem_ref, rsem_ref, after_ref, f_out):
        c = lax.axis_index("c")
        _half_copy(f_ref, c, ssem_ref, rsem_ref).wait_send()
        _half_copy(f_ref, 1 - c, ssem_ref, rsem_ref).wait_recv()

    return pl.pallas_call(
        body, name=name, out_shape=pltpu.HBM(full.shape, full.dtype),
        in_specs=(HBM, SEM, SEM, ANY), out_specs=HBM, input_output_aliases={0: 0},
        compiler_params=pltpu.CompilerParams(has_side_effects=EFFECT),
    )(full, ssem, rsem, after)


def _small_allreduce(vec):
    nr = vec.shape[0]

    def body(v_ref, o_ref, rsib, pair, rchip, ssem, rsem):
        x, y, c, chips = _place()
        me = 2 * x + y
        sib = (x, y, 1 - c)
        first = pltpu.make_async_remote_copy(src_ref=v_ref, dst_ref=rsib, send_sem=ssem.at[0], recv_sem=rsem.at[0],
                                             device_id=sib, device_id_type=MESH)
        first.start()
        first.wait()
        pair[...] = v_ref[...] + rsib[...]
        cps = [pltpu.make_async_remote_copy(src_ref=pair, dst_ref=rchip.at[j], send_sem=ssem.at[1 + j], recv_sem=rsem.at[1 + j],
                                            device_id=(*chip, c), device_id_type=MESH) for j, chip in enumerate(chips)]
        for cp in cps:
            cp.start()
        for cp in cps:
            cp.wait()
        acc = None
        for kk in range(N_CHIP):
            d = jnp.bitwise_xor(me, kk)
            t = jnp.where(d == 0, pair[...], jnp.where(d == 2, rchip[0], jnp.where(d == 1, rchip[1], rchip[2])))
            acc = t if acc is None else acc + t
        o_ref[...] = acc

    return pl.pallas_call(
        body, name="small_allreduce",
        in_specs=[VMEM_FULL], out_specs=VMEM_FULL, out_shape=_sds((nr, 128), F32),
        scratch_shapes=[pltpu.VMEM((nr, 128), F32), pltpu.VMEM((nr, 128), F32), pltpu.VMEM((3, nr, 128), F32),
                        pltpu.SemaphoreType.DMA((4,)), pltpu.SemaphoreType.DMA((4,))],
        compiler_params=_cp(),
    )(vec)


def _adam_math(w, g, m, v):
    m = B1 * m + (1.0 - B1) * g
    v = B2 * v + (1.0 - B2) * (g * g)
    m_hat = m / (1.0 - B1 ** STEP)
    v_hat = v / (1.0 - B2 ** STEP)
    delta = -LR * (m_hat / (jnp.sqrt(v_hat) + AEPS) + WD * w)
    return delta, m, v


def _adam_big(w, g, m, v, name):
    rows, cols = w.shape
    tr = 128

    def body(w_ref, g_ref, m_ref, v_ref, go_ref, d_ref, nm_ref, nv_ref):
        g = g_ref[...]
        d, nm, nv = _adam_math(w_ref[...], g, m_ref[...], v_ref[...])
        go_ref[...] = g
        d_ref[...] = d
        nm_ref[...] = nm
        nv_ref[...] = nv

    blk = pl.BlockSpec((tr, cols), lambda i: (i, 0))
    return pl.pallas_call(
        body, name=name, grid=(rows // tr,),
        in_specs=[blk] * 4, out_specs=[blk] * 4, out_shape=[_sds((rows, cols), F32)] * 4,
        compiler_params=_cp(("arbitrary",)),
    )(w, g, m, v)


def _adam_small(ws, gs, ms, vs, dep=None):
    n = len(ws)

    def body(*refs):
        for i in range(n):
            d, nm, nv = _adam_math(refs[i][...], refs[n + i][...], refs[2 * n + i][...], refs[3 * n + i][...])
            refs[4 * n + 1 + i][...] = d
            refs[5 * n + 1 + i][...] = nm
            refs[6 * n + 1 + i][...] = nv

    shapes = [_sds(w.shape, F32) for w in ws]
    outs = pl.pallas_call(
        body, name="adam_small",
        in_specs=[VMEM_FULL] * (4 * n) + [ANY], out_specs=[VMEM_FULL] * (3 * n), out_shape=shapes * 3,
        compiler_params=_cp(),
    )(*ws, *gs, *ms, *vs, _dep(dep))
    return outs[:n], outs[n:2 * n], outs[2 * n:]


def _pad_rows8(a):
    flat = a.reshape(-1, 128)
    pad = (-flat.shape[0]) % 8
    if pad:
        flat = jnp.concatenate([flat, jnp.zeros((pad, 128), F32)], axis=0)
    return flat


def kernel(x, meta_tokens, norm1_w, w_in, gate_w2, gate_b, gla_norm_w, pool_w, pool_scale, w_out, norm2_w, mlp_w1, mlp_w2, final_norm_w, loss_target, m_meta_tokens, m_norm1_w, m_w_in, m_gate_w2, m_gate_b, m_gla_norm_w, m_pool_w, m_pool_scale, m_w_out, m_norm2_w, m_mlp_w1, m_mlp_w2, m_final_norm_w, v_meta_tokens, v_norm1_w, v_w_in, v_gate_w2, v_gate_b, v_gla_norm_w, v_pool_w, v_pool_scale, v_w_out, v_norm2_w, v_mlp_w1, v_mlp_w2, v_final_norm_w):
    cx, cy, cc = lax.axis_index("x"), lax.axis_index("y"), lax.axis_index("c")
    me = (2 * cx + cy).astype(jnp.int32)

    place = jnp.stack([me, cc.astype(jnp.int32)])
    fw = final_norm_w.reshape(1, D)

    metaF, gw2F, pwF = _gather_small(meta_tokens, gate_w2[0], pool_w[0])
    (s_win,), (r_win,), (f_win,), tok = _gather_start([_cast_win(w_in[0], place, gw2F)], "gather_start_win")
    rest = [_cast_into(w_out[0], place, D, "cast_wout", tok), _cast_into(mlp_w1[0], place, D, "cast_w1", tok),
            _cast_into(mlp_w2[0], place, D, "cast_w2", tok)]
    ssems, rsems, flying, tok = _gather_start(rest, "gather_start_rest")
    ssems, rsems, flying = [s_win, *ssems], [r_win, *rsems], [f_win, *flying]

    def arrive(i, nm, after):
        return _forward_halves(_gather_wait(flying[i], ssems[i], rsems[i], after, "gather_wait_" + nm), "forward_" + nm)

    pairs, pending = {}, {}

    def grad_start(nm, g):
        ssem, rsem, g_thru, land, token = _pair_start(g, "pair_start_" + nm)
        pairs[nm] = (ssem, rsem, g_thru, land)
        return token

    def grad_finish(nm, after):
        ssem, rsem, g_thru, land = pairs[nm]
        g, rcv = _pair_wait(g_thru, land, ssem, rsem, after, "pair_wait_" + nm)
        sb, sf = _pair_sum(g, rcv, place, "pair_sum_" + nm)
        if nm == "win":
            pending[nm] = (sf, sb)
            return sf
        ssem, rsem, sb_thru, land, token = _rs_start(sb, "rs_start_" + nm)
        pending[nm] = (sf, ssem, rsem, sb_thru, land)
        return token

    (grad_x, loss8, d_n1w, d_gb, d_gnw, d_ps, d_n2w, d_fw, d_meta, d_gw2, d_pw) = _local_step(
        x[0], loss_target[0], lambda after: arrive(0, "win", after), lambda after: arrive(1, "wout", after).reshape(D, D),
        lambda after: arrive(2, "w1", after), lambda after: arrive(3, "w2", after).reshape(DFF, D), metaF, gw2F, pwF,
        norm1_w, gate_b, gla_norm_w, pool_scale, norm2_w, fw, grad_start, grad_finish, tok)
    return _reduce_and_update(
        me, place, pending, grad_x, loss8, d_n1w, d_gb, d_gnw, d_ps, d_n2w, d_fw, d_meta, d_gw2, d_pw,
        meta_tokens, norm1_w, w_in, gate_w2, gate_b, gla_norm_w, pool_w, pool_scale, w_out, norm2_w, mlp_w1, mlp_w2, fw,
        m_meta_tokens, m_norm1_w, m_w_in, m_gate_w2, m_gate_b, m_gla_norm_w, m_pool_w, m_pool_scale, m_w_out, m_norm2_w,
        m_mlp_w1, m_mlp_w2, m_final_norm_w, v_meta_tokens, v_norm1_w, v_w_in, v_gate_w2, v_gate_b, v_gla_norm_w, v_pool_w,
        v_pool_scale, v_w_out, v_norm2_w, v_mlp_w1, v_mlp_w2, v_final_norm_w)


def _local_step(x, target, get_win, get_wout, get_w1, get_w2, metaF, gw2F, pwF, norm1_w, gate_b, gla_norm_w, pool_scale,
                norm2_w, fw, grad_start, grad_finish, first=None):
    h0, u = _embed_norm(x, metaF, norm1_w, first)
    Win = get_win(u)
    P = _in_proj(u, Win)
    gw2p = jnp.pad(gw2F, ((0, 128 - RANK), (0, 0)))
    o, og, sp = _gla_fwd(P, gw2p, gate_b, gla_norm_w)
    yb, op = _pool_fwd(P, pwF, pool_scale)
    Wout = get_wout(op)
    h1 = _out_proj(og, op, Wout, h0)
    n2 = _norm_rows(h1, norm2_w, "norm2")
    W1 = get_w1(n2)
    zr, a = _mlp_up(n2, W1)
    W2 = get_w2(a)
    h2 = _mlp_down(a, W2, h1)

    dh2, dh2b, d_fw, loss8 = _loss_head(h2, target, fw)
    tok = grad_start("w2", _grad_w2(a, dh2b).reshape(N_CHIP, D, D))
    dz = _mlp_dz(dh2b, W2, zr, tok)
    tok = grad_finish("w2", dz)
    tok = grad_start("w1", _grad_w1(n2, dz, tok))
    dn2 = _mlp_dn(dz, W1, tok)
    tok = grad_finish("w1", dn2)
    dh1, dh1b, d_n2w = _norm_bwd(dn2, h1, dh2, norm2_w, "norm2_bwd", tok)
    dmixed = _mixed_grad(dh1b, Wout)
    tok = grad_start("wout", _grad_wout(og, op, dh1b))
    dpu, d_pw, d_ps = _pool_bwd(dmixed, yb, pwF, pool_scale, tok)
    dq, dk, dv, dr, dglr, d_gw2p, d_gb, d_gnw = _gla_bwd(dmixed, o, P, gw2p, gate_b, gla_norm_w, sp, tok)
    d_gw2 = d_gw2p[0:RANK]
    tok = grad_finish("wout", dq)
    tok = grad_start("win", _grad_win(u, dq, dk, dv, dr, dglr, dpu, tok))
    du = _in_grad(dq, dk, dv, dr, dglr, dpu, Win, tok)
    tok = grad_finish("win", du)
    grad_x, d_meta, d_n1w = _input_grad(du, h0, dh1, norm1_w, tok)
    return grad_x, loss8, d_n1w, d_gb, d_gnw, d_ps, d_n2w, d_fw, d_meta, d_gw2, d_pw


def _reduce_and_update(me, place, pending, grad_x, loss8, d_n1w, d_gb, d_gnw, d_ps, d_n2w, d_fw, d_meta, d_gw2, d_pw,
                       meta_tokens, norm1_w, w_in, gate_w2, gate_b, gla_norm_w, pool_w, pool_scale, w_out, norm2_w,
                       mlp_w1, mlp_w2, fw, m_meta_tokens, m_norm1_w, m_w_in, m_gate_w2, m_gate_b, m_gla_norm_w, m_pool_w,
                       m_pool_scale, m_w_out, m_norm2_w, m_mlp_w1, m_mlp_w2, m_final_norm_w, v_meta_tokens, v_norm1_w, v_w_in,
                       v_gate_w2, v_gate_b, v_gla_norm_w, v_pool_w, v_pool_scale, v_w_out, v_norm2_w, v_mlp_w1, v_mlp_w2,
                       v_final_norm_w):
    parts = [loss8, d_n1w, d_gb, d_gnw, d_ps, d_n2w, d_fw, d_meta, d_gw2, d_pw]
    packed = [_pad_rows8(p) for p in parts]
    sizes = [p.shape[0] for p in packed]
    red = _small_allreduce(jnp.concatenate(packed, axis=0))
    offs = [0]
    for s in sizes:
        offs.append(offs[-1] + s)

    def take(i, shape):
        n = 1
        for d in shape:
            n *= d
        return red[offs[i]:offs[i] + n // 128].reshape(shape)

    loss = red[0, 0]
    G_n1w = take(1, (1, D))
    G_gb = take(2, (1, KW))
    G_gnw = take(3, (1, DV))
    G_ps = take(4, (1, PW))
    G_n2w = take(5, (1, D))
    G_fw = take(6, (1, D))
    G_meta = lax.dynamic_slice(take(7, (N_META, D)), (0, me * 512), (N_META, 512))
    G_gw2 = lax.dynamic_slice(take(8, (RANK, KW)), (0, me * 128), (RANK, 128))
    G_pw = lax.dynamic_slice(take(9, (4, GC, GC)), (0, me * 64, 0), (4, 64, GC))

    big, halves = {}, {}
    sf, sb = pending["win"]
    ssem, rsem, sb_thru, land, after = _rs_start(sb, "rs_start_win", red)
    pending["win"] = (sf, ssem, rsem, sb_thru, land)
    params = {"w2": (mlp_w2, m_mlp_w2, v_mlp_w2), "w1": (mlp_w1, m_mlp_w1, v_mlp_w1), "wout": (w_out, m_w_out, v_w_out),
              "win": (w_in, m_w_in, v_w_in)}

    def reduce_(nm, after):
        sf, ssem, rsem, sb_thru, land = pending[nm]
        rb = _rs_wait(sb_thru, land, ssem, rsem, after, "rs_wait_" + nm)
        hs, hr, full, token = _half_start(_final_sum(sf, rb, place, "final_sum_" + nm), "half_start_" + nm)
        halves[nm] = (hs, hr, full)
        return token

    def update(nm, after):
        hs, hr, full = halves[nm]
        w, m, v = params[nm]
        big[nm] = _adam_big(w[0], _half_wait(full, hs, hr, after, "half_wait_" + nm), m[0], v[0], "adam_" + nm)
        return big[nm][3]

    after = reduce_("w2", after)
    after = reduce_("w1", after)
    after = update("w2", after)
    after = update("w1", after)
    after = reduce_("win", after)
    after = reduce_("wout", after)
    after = update("win", after)
    after = update("wout", after)
    G_win, d_win, nm_win, nv_win = big["win"]
    G_wout, d_wout, nm_wout, nv_wout = big["wout"]
    G_w1, d_w1, nm_w1, nv_w1 = big["w1"]
    G_w2, d_w2, nm_w2, nv_w2 = big["w2"]
    ws = [meta_tokens, norm1_w, gate_w2[0], gate_b, gla_norm_w, pool_w[0], pool_scale, norm2_w, fw]
    gs = [G_meta, G_n1w, G_gw2, G_gb, G_gnw, G_pw, G_ps, G_n2w, G_fw]
    ms = [m_meta_tokens, m_norm1_w, m_gate_w2[0], m_gate_b, m_gla_norm_w, m_pool_w[0], m_pool_scale, m_norm2_w,
          m_final_norm_w.reshape(1, D)]
    vs = [v_meta_tokens, v_norm1_w, v_gate_w2[0], v_gate_b, v_gla_norm_w, v_pool_w[0], v_pool_scale, v_norm2_w,
          v_final_norm_w.reshape(1, D)]
    ds, nms, nvs = _adam_small(ws, gs, ms, vs, after)

    def assemble(small, win_, wout_, w1_, w2_):
        meta_, n1_, gw2_, gb_, gnw_, pw_, ps_, n2_, fw_ = small
        return (meta_, n1_, win_[None], gw2_[None], gb_, gnw_, pw_[None], ps_, wout_[None], n2_, w1_[None], w2_[None],
                fw_.reshape(D))

    grads_out = assemble(gs, G_win, G_wout, G_w1, G_w2)
    deltas = assemble(ds, d_win, d_wout, d_w1, d_w2)
    new_m = assemble(nms, nm_win, nm_wout, nm_w1, nm_w2)
    new_v = assemble(nvs, nv_win, nv_wout, nv_w1, nv_w2)
    return (loss, grad_x[None], *grads_out, *deltas, *new_m, *new_v)
```

```python
import functools

import jax
import jax.numpy as jnp
from jax import lax
from jax.experimental import pallas as pl
from jax.experimental.pallas import tpu as pltpu

F32 = jnp.float32
BF16 = jnp.bfloat16

D = 2048
SEQ = 2048
N_META = 16
CH = 64
TP = 2176
NCH = TP // CH
ROW_LO = 112
X_LO = 128
ROW_HI = TP
XT = 128
NXT = TP // XT
HEADS = 4
DK = 128
DV = 256
KW = HEADS * DK
GW = HEADS * DV
RANK = 16
TAU = 16.0
WINDOWS = (2, 4, 8, 16)
PW = 1024
GC = 256
DFF = 8192
EPS = 1e-6
SHARD_IN = 1028
PAD_IN = 1152
N_CHIP = 4

LR = 0.001
B1 = 0.9
B2 = 0.999
AEPS = 1e-08
WD = 0.01
STEP = 10

VMEM_LIMIT = 60 * 1024 * 1024
ANY = pl.BlockSpec(memory_space=pl.ANY)
VMEM_FULL = pl.BlockSpec(memory_space=pltpu.VMEM)
MESH = pl.DeviceIdType.MESH


def _cp(sem=None):
    if sem is None:
        return pltpu.CompilerParams(vmem_limit_bytes=VMEM_LIMIT)
    return pltpu.CompilerParams(dimension_semantics=sem, vmem_limit_bytes=VMEM_LIMIT)


def _dot(a, b):
    return jnp.dot(a, b, preferred_element_type=F32)


def _dot_nt(a, b):
    return lax.dot_general(a, b, (((1,), (1,)), ((), ())), preferred_element_type=F32)


def _dot_tn(a, b):
    return lax.dot_general(a, b, (((0,), (0,)), ((), ())), preferred_element_type=F32)


def _sds(shape, dtype):
    return jax.ShapeDtypeStruct(shape, dtype)


def _embed_norm(x, meta_full, w, dep=None):
    def body(x_ref, meta_ref, w_ref, dep_ref, h_ref, u_ref):
        i = pl.program_id(0)

        @pl.when(i == 0)
        def _():
            h_ref[...] = jnp.zeros_like(h_ref)
            h_ref[ROW_LO:X_LO, :] = meta_ref[...]

        @pl.when(i >= 1)
        def _():
            h_ref[...] = x_ref[...]

        h = h_ref[...]
        r = lax.rsqrt(jnp.mean(h * h, axis=-1, keepdims=True) + EPS)
        u_ref[...] = ((h * r) * w_ref[...]).astype(BF16)

    return pl.pallas_call(
        body, name="embed_norm1", grid=(NXT,),
        in_specs=[pl.BlockSpec((XT, D), lambda i: (jnp.maximum(i - 1, 0), 0)),
                  pl.BlockSpec((N_META, D), lambda i: (0, 0)),
                  pl.BlockSpec((1, D), lambda i: (0, 0)), ANY],
        out_specs=[pl.BlockSpec((XT, D), lambda i: (i, 0)), pl.BlockSpec((XT, D), lambda i: (i, 0))],
        out_shape=[_sds((TP, D), F32), _sds((TP, D), BF16)],
        compiler_params=_cp(("arbitrary",)),
    )(x, meta_full, w, _dep(dep))


def _norm_rows(h, w, name):
    tr = 272

    def body(h_ref, w_ref, o_ref):
        hv = h_ref[...]
        r = lax.rsqrt(jnp.mean(hv * hv, axis=-1, keepdims=True) + EPS)
        o_ref[...] = ((hv * r) * w_ref[...]).astype(BF16)

    return pl.pallas_call(
        body, name=name, grid=(TP // tr,),
        in_specs=[pl.BlockSpec((tr, D), lambda i: (i, 0)), pl.BlockSpec((1, D), lambda i: (0, 0))],
        out_specs=pl.BlockSpec((tr, D), lambda i: (i, 0)),
        out_shape=_sds((TP, D), BF16),
        compiler_params=_cp(("arbitrary",)),
    )(h, w)


def _loss_head(h2, target, fw):
    def body(h_ref, t_ref, w_ref, dh_ref, dhb_ref, dw_ref, loss_ref):
        i = pl.program_id(0)

        @pl.when(i == 0)
        def _():
            dw_ref[...] = jnp.zeros_like(dw_ref)
            loss_ref[...] = jnp.zeros_like(loss_ref)

        h = h_ref[...]
        w = w_ref[...]
        r = lax.rsqrt(jnp.mean(h * h, axis=-1, keepdims=True) + EPS)
        xh = h * r
        y = xh * w
        is_x = (i >= 1).astype(F32)
        diff = (y - t_ref[...]) * is_x
        loss_ref[...] += jnp.sum(diff * diff) * (0.5 / D)
        dy = diff * (1.0 / D)
        dw_ref[...] += jnp.sum(dy * xh, axis=0, keepdims=True)
        gx = dy * w
        dh = r * (gx - xh * jnp.mean(gx * xh, axis=-1, keepdims=True))
        dh_ref[...] = dh
        dhb_ref[...] = dh.astype(BF16)

    return pl.pallas_call(
        body, name="loss_head", grid=(NXT,),
        in_specs=[pl.BlockSpec((XT, D), lambda i: (i, 0)),
                  pl.BlockSpec((XT, D), lambda i: (jnp.maximum(i - 1, 0), 0)),
                  pl.BlockSpec((1, D), lambda i: (0, 0))],
        out_specs=[pl.BlockSpec((XT, D), lambda i: (i, 0)), pl.BlockSpec((XT, D), lambda i: (i, 0)),
                   pl.BlockSpec((1, D), lambda i: (0, 0)), pl.BlockSpec((8, 128), lambda i: (0, 0))],
        out_shape=[_sds((TP, D), F32), _sds((TP, D), BF16), _sds((1, D), F32), _sds((8, 128), F32)],
        compiler_params=_cp(("arbitrary",)),
    )(h2, target, fw)


def _norm_bwd(dn, h, dres, w, name, dep=None):
    tr = 272

    def body(dn_ref, h_ref, dres_ref, w_ref, dep_ref, o_ref, ob_ref, dw_ref):
        @pl.when(pl.program_id(0) == 0)
        def _():
            dw_ref[...] = jnp.zeros_like(dw_ref)

        hv = h_ref[...]
        dnv = dn_ref[...]
        r = lax.rsqrt(jnp.mean(hv * hv, axis=-1, keepdims=True) + EPS)
        xh = hv * r
        dw_ref[...] += jnp.sum(dnv * xh, axis=0, keepdims=True)
        gx = dnv * w_ref[...]
        dh = dres_ref[...] + r * (gx - xh * jnp.mean(gx * xh, axis=-1, keepdims=True))
        o_ref[...] = dh
        ob_ref[...] = dh.astype(BF16)

    row = pl.BlockSpec((tr, D), lambda i: (i, 0))
    vec = pl.BlockSpec((1, D), lambda i: (0, 0))
    return pl.pallas_call(
        body, name=name, grid=(TP // tr,),
        in_specs=[row, row, row, vec, ANY], out_specs=[row, row, vec],
        out_shape=[_sds((TP, D), F32), _sds((TP, D), BF16), _sds((1, D), F32)],
        compiler_params=_cp(("arbitrary",)),
    )(dn, h, dres, w, _dep(dep))


def _input_grad(du, h0, dh1, w, dep=None):
    def body(du_ref, h_ref, dres_ref, w_ref, dep_ref, gx_ref, gm_ref, dw_ref):
        i = pl.program_id(0)

        @pl.when(i == 0)
        def _():
            dw_ref[...] = jnp.zeros_like(dw_ref)

        hv = h_ref[...]
        dnv = du_ref[...]
        r = lax.rsqrt(jnp.mean(hv * hv, axis=-1, keepdims=True) + EPS)
        xh = hv * r
        dw_ref[...] += jnp.sum(dnv * xh, axis=0, keepdims=True)
        g = dnv * w_ref[...]
        dh = dres_ref[...] + r * (g - xh * jnp.mean(g * xh, axis=-1, keepdims=True))

        @pl.when(i == 0)
        def _():
            gm_ref[...] = dh[ROW_LO:X_LO, :]

        @pl.when(i >= 1)
        def _():
            gx_ref[...] = dh

    row = pl.BlockSpec((XT, D), lambda i: (i, 0))
    vec = pl.BlockSpec((1, D), lambda i: (0, 0))
    return pl.pallas_call(
        body, name="input_grad", grid=(NXT,),
        in_specs=[row, row, row, vec, ANY],
        out_specs=[pl.BlockSpec((XT, D), lambda i: (jnp.maximum(i - 1, 0), 0)),
                   pl.BlockSpec((N_META, D), lambda i: (0, 0)), vec],
        out_shape=[_sds((SEQ, D), F32), _sds((N_META, D), F32), _sds((1, D), F32)],
        compiler_params=_cp(("arbitrary",)),
    )(du, h0, dh1, w, _dep(dep))


def _in_proj(u, wg):
    def body(u_ref, w_ref, o_ref):
        o_ref[0] = _dot(u_ref[...], w_ref[0])

    return pl.pallas_call(
        body, name="in_proj", grid=(N_CHIP,),
        in_specs=[VMEM_FULL, pl.BlockSpec((1, D, PAD_IN), lambda k: (k, 0, 0))],
        out_specs=pl.BlockSpec((1, TP, PAD_IN), lambda k: (k, 0, 0)),
        out_shape=_sds((N_CHIP, TP, PAD_IN), F32),
        compiler_params=_cp(("arbitrary",)),
    )(u, wg)


def _out_proj(og, op, wout, h0):
    tn = 512

    def body(og_ref, op_ref, w_ref, h_ref, o_ref):
        acc = _dot(og_ref[...], w_ref[0:GW, :]) + _dot(op_ref[...], w_ref[GW:D, :])
        o_ref[...] = h_ref[...] + acc

    return pl.pallas_call(
        body, name="out_proj", grid=(D // tn,),
        in_specs=[VMEM_FULL, VMEM_FULL, pl.BlockSpec((D, tn), lambda j: (0, j)),
                  pl.BlockSpec((TP, tn), lambda j: (0, j))],
        out_specs=pl.BlockSpec((TP, tn), lambda j: (0, j)),
        out_shape=_sds((TP, D), F32),
        compiler_params=_cp(("arbitrary",)),
    )(og, op, wout, h0)


def _mlp_up(n2, w1g, early=()):
    tn = 512
    per = D // tn
    ne = len(early)

    def body(n_ref, w_ref, *rest):
        zr_ref, a_ref = rest[ne:]
        z = jnp.maximum(_dot(n_ref[...], w_ref[0]), 0.0)
        zr_ref[...] = z.astype(BF16)
        a_ref[...] = (z * z).astype(BF16)

    col = pl.BlockSpec((TP, tn), lambda k, j: (0, k * per + j))
    return pl.pallas_call(
        body, name="mlp_up", grid=(N_CHIP, per),
        in_specs=[VMEM_FULL, pl.BlockSpec((1, D, tn), lambda k, j: (k, 0, j))] + [ANY] * ne,
        out_specs=[col, col],
        out_shape=[_sds((TP, DFF), BF16), _sds((TP, DFF), BF16)],
        compiler_params=_cp(("arbitrary", "arbitrary")),
    )(n2, w1g, *early)


def _mlp_down(a, w2, h1):
    tk = 1024
    nk = DFF // tk

    def body(a_ref, w_ref, h_ref, o_ref, acc_ref):
        k = pl.program_id(0)

        @pl.when(k == 0)
        def _():
            pltpu.sync_copy(h_ref, acc_ref)

        acc_ref[...] += _dot(a_ref[...], w_ref[...])

        @pl.when(k == nk - 1)
        def _():
            pltpu.sync_copy(acc_ref, o_ref)

    return pl.pallas_call(
        body, name="mlp_down", grid=(nk,),
        in_specs=[pl.BlockSpec((TP, tk), lambda k: (0, k)), pl.BlockSpec((tk, D), lambda k: (k, 0)), ANY],
        out_specs=ANY,
        out_shape=_sds((TP, D), F32),
        scratch_shapes=[pltpu.VMEM((TP, D), F32)],
        compiler_params=_cp(("arbitrary",)),
    )(a, w2, h1)


def _mlp_dz(dh2b, w2, zr, dep=None):
    tn = 512

    def body(d_ref, w_ref, z_ref, dep_ref, o_ref):
        da = _dot_nt(d_ref[...], w_ref[...])
        o_ref[...] = (da * (2.0 * z_ref[...].astype(F32))).astype(BF16)

    col = pl.BlockSpec((TP, tn), lambda j: (0, j))
    return pl.pallas_call(
        body, name="mlp_dz", grid=(DFF // tn,),
        in_specs=[VMEM_FULL, pl.BlockSpec((tn, D), lambda j: (j, 0)), col, ANY],
        out_specs=col,
        out_shape=_sds((TP, DFF), BF16),
        compiler_params=_cp(("arbitrary",)),
    )(dh2b, w2, zr, _dep(dep))


def _grad_w2(a, dh2b):
    tm = 512

    def body(a_ref, d_ref, o_ref):
        o_ref[...] = _dot_tn(a_ref[...], d_ref[...])

    return pl.pallas_call(
        body, name="grad_w2", grid=(DFF // tm,),
        in_specs=[pl.BlockSpec((TP, tm), lambda j: (0, j)), VMEM_FULL],
        out_specs=pl.BlockSpec((tm, D), lambda j: (j, 0)),
        out_shape=_sds((DFF, D), F32),
        compiler_params=_cp(("arbitrary",)),
    )(a, dh2b)


def _dep(token):
    return jnp.zeros((8, 128), F32) if token is None else token


def _grad_w1(n2, dz, dep=None):
    tn = 512
    per = D // tn

    def body(n_ref, d_ref, dep_ref, o_ref):
        o_ref[0] = _dot_tn(n_ref[...], d_ref[...])

    return pl.pallas_call(
        body, name="grad_w1", grid=(N_CHIP, per),
        in_specs=[VMEM_FULL, pl.BlockSpec((TP, tn), lambda k, j: (0, k * per + j)), ANY],
        out_specs=pl.BlockSpec((1, D, tn), lambda k, j: (k, 0, j)),
        out_shape=_sds((N_CHIP, D, D), F32),
        compiler_params=_cp(("arbitrary", "arbitrary")),
    )(n2, dz, _dep(dep))


def _mlp_dn(dz, w1g, dep=None):
    tk = 1024
    per = D // tk
    nk = DFF // tk

    def body(d_ref, w_ref, dep_ref, o_ref, acc_ref):
        k = pl.program_id(0)
        part = _dot_nt(d_ref[...], w_ref[0])

        @pl.when(k == 0)
        def _():
            acc_ref[...] = part

        @pl.when(k > 0)
        def _():
            acc_ref[...] += part

        @pl.when(k == nk - 1)
        def _():
            pltpu.sync_copy(acc_ref, o_ref)

    return pl.pallas_call(
        body, name="mlp_dn", grid=(nk,),
        in_specs=[pl.BlockSpec((TP, tk), lambda k: (0, k)),
                  pl.BlockSpec((1, D, tk), lambda k: (k // per, 0, k % per)), ANY],
        out_specs=ANY,
        out_shape=_sds((TP, D), F32),
        scratch_shapes=[pltpu.VMEM((TP, D), F32)],
        compiler_params=_cp(("arbitrary",)),
    )(dz, w1g, _dep(dep))


def _mixed_grad(dh1b, wout):
    tn = 512

    def body(d_ref, w_ref, o_ref):
        o_ref[...] = _dot_nt(d_ref[...], w_ref[...])

    return pl.pallas_call(
        body, name="mixed_grad", grid=(D // tn,),
        in_specs=[VMEM_FULL, pl.BlockSpec((tn, D), lambda j: (j, 0))],
        out_specs=pl.BlockSpec((TP, tn), lambda j: (0, j)),
        out_shape=_sds((TP, D), F32),
        compiler_params=_cp(("arbitrary",)),
    )(dh1b, wout)


def _grad_wout(og, op, dh1b):
    tm = 512

    def body(og_ref, op_ref, d_ref, o_ref):
        j = pl.program_id(0)

        @pl.when(j < 2)
        def _():
            o_ref[0] = _dot_tn(og_ref[...], d_ref[...])

        @pl.when(j >= 2)
        def _():
            o_ref[0] = _dot_tn(op_ref[...], d_ref[...])

    return pl.pallas_call(
        body, name="grad_wout", grid=(N_CHIP,),
        in_specs=[pl.BlockSpec((TP, tm), lambda j: (0, jnp.minimum(j, 1))),
                  pl.BlockSpec((TP, tm), lambda j: (0, jnp.maximum(j - 2, 0))), VMEM_FULL],
        out_specs=pl.BlockSpec((1, tm, D), lambda j: (j, 0, 0)),
        out_shape=_sds((N_CHIP, tm, D), F32),
        compiler_params=_cp(("arbitrary",)),
    )(og, op, dh1b)


def _in_grad(dq, dk, dv, dr, dglr, dpu, wg, dep=None):
    def body(dq_ref, dk_ref, dv_ref, dr_ref, dg_ref, dpu_ref, w_ref, dep_ref, o_ref):
        dv, dr, dg = dv_ref[...], dr_ref[...], dg_ref[...]
        head, tail = slice(0, GW), slice(GW, PAD_IN)
        o_ref[...] = (_dot_nt(dq_ref[...], w_ref[0, :, 0:KW]) + _dot_nt(dk_ref[...], w_ref[0, :, KW:GW])
                      + _dot_nt(dv[:, 0:128], w_ref[0, :, tail])
                      + _dot_nt(dv, w_ref[1, :, head]) + _dot_nt(dr[:, 0:128], w_ref[1, :, tail])
                      + _dot_nt(dr, w_ref[2, :, head]) + _dot_nt(dg, w_ref[2, :, tail])
                      + _dot_nt(dpu_ref[...], w_ref[3, :, head]) + _dot_nt(dg, w_ref[3, :, tail]))

    tn = 512
    return pl.pallas_call(
        body, name="in_grad", grid=(D // tn,),
        in_specs=[VMEM_FULL] * 6 + [pl.BlockSpec((N_CHIP, tn, PAD_IN), lambda j: (0, j, 0)), ANY],
        out_specs=pl.BlockSpec((TP, tn), lambda j: (0, j)),
        out_shape=_sds((TP, D), F32),
        compiler_params=_cp(("arbitrary",)),
    )(dq, dk, dv, dr, dglr, dpu, wg, _dep(dep))


def _grad_win(u, dq, dk, dv, dr, dglr, dpu, dep=None):
    tm = 512

    def body(u_ref, dq_hbm, dk_hbm, dv_hbm, dr_hbm, dg_hbm, dpu_hbm, dep_ref, o_ref, dp_ref, sem):
        k = pl.program_id(0)
        head, tail = slice(0, GW), slice(GW, PAD_IN)

        def bring(pieces):
            cps = [pltpu.make_async_copy(src, dp_ref.at[:, cols], sem.at[i]) for i, (src, cols) in enumerate(pieces)]
            for cp in cps:
                cp.start()
            for cp in cps:
                cp.wait()

        @pl.when(pl.program_id(1) == 0)
        def _():
            @pl.when(k == 0)
            def _():
                bring([(dq_hbm, slice(0, KW)), (dk_hbm, slice(KW, GW)), (dv_hbm.at[:, 0:128], tail)])

            @pl.when(k == 1)
            def _():
                bring([(dv_hbm, head), (dr_hbm.at[:, 0:128], tail)])

            @pl.when(k == 2)
            def _():
                bring([(dr_hbm, head), (dg_hbm, tail)])

            @pl.when(k == 3)
            def _():
                bring([(dpu_hbm, head), (dg_hbm, tail)])

        g = _dot_tn(u_ref[...], dp_ref[...])
        lane = lax.broadcasted_iota(jnp.int32, (tm, PAD_IN), 1)
        for kk in range(N_CHIP):
            @pl.when(k == kk)
            def _(kk=kk):
                if kk == 0:
                    nat = g
                elif kk < 3:
                    nat = pltpu.roll(g, PAD_IN - 4 * kk, 1)
                else:
                    nat = jnp.where(lane < 4, pltpu.roll(g, PAD_IN - (GW + 12), 1), pltpu.roll(g, 4, 1))
                o_ref[0] = nat[:, 0:SHARD_IN]

    return pl.pallas_call(
        body, name="grad_win", grid=(N_CHIP, D // tm),
        in_specs=[pl.BlockSpec((TP, tm), lambda k, m: (0, m))] + [ANY] * 7,
        out_specs=pl.BlockSpec((1, tm, SHARD_IN), lambda k, m: (k, m, 0)),
        out_shape=_sds((N_CHIP, D, SHARD_IN), F32),
        scratch_shapes=[pltpu.VMEM((TP, PAD_IN), BF16), pltpu.SemaphoreType.DMA((3,))],
        compiler_params=_cp(("arbitrary", "arbitrary")),
    )(u, dq, dk, dv, dr, dglr, dpu, _dep(dep))


def _split3(x):
    hi = x.astype(BF16)
    r1 = x - hi.astype(F32)
    mid = r1.astype(BF16)
    lo = (r1 - mid.astype(F32)).astype(BF16)
    return hi, mid, lo


def _tri_sum(tri, x):
    hi, mid, lo = _split3(x)
    return _dot(tri, hi) + _dot(tri, mid) + _dot(tri, lo)


def _gla_common(n, glr, gw2, gb):
    rows = n * CH + lax.broadcasted_iota(jnp.int32, (CH, 1), 0)
    valid = (rows >= ROW_LO) & (rows < ROW_HI)
    g_raw = _dot(glr.astype(BF16), gw2.astype(BF16)) + gb
    logsig = jnp.minimum(g_raw, 0.0) - jnp.log(1.0 + jnp.exp(-jnp.abs(g_raw)))
    logg = jnp.where(valid, logsig * (1.0 / TAU), 0.0)
    ci = lax.broadcasted_iota(jnp.int32, (CH, CH), 0)
    si = lax.broadcasted_iota(jnp.int32, (CH, CH), 1)
    lower = ci >= si
    G = _tri_sum(lower.astype(BF16), logg)
    Gl = G[CH - 1:CH, :]
    return valid, g_raw, lower, G, Gl


def _p_specs(index):
    def spec(width, shard, col):
        return pl.BlockSpec((1, CH, width), lambda s: (shard, index(s), col))

    return [spec(KW, 0, 0), spec(KW, 0, 1), spec(GW, 1, 0), spec(128, 0, 8), spec(GW, 2, 0), spec(128, 1, 8),
            spec(128, 2, 8), spec(128, 3, 8)]


def _p_load(q_ref, k_ref, vm_ref, vh_ref, rm_ref, rh_ref, ga_ref, gb_ref):
    def joined(main, head):
        return jnp.concatenate([main[:, 0:128] + head, main[:, 128:]], axis=1)

    return q_ref[0], k_ref[0], joined(vm_ref[0], vh_ref[0]), joined(rm_ref[0], rh_ref[0]), ga_ref[0] + gb_ref[0]


def _gla_fwd(P, gw2, gb, gnw):
    scale = DK ** -0.5

    def body(p0, p1, p2, p3, p4, p5, p6, p7, gw2_ref, gb_ref, gnw_ref, o_ref, og_ref, sp_ref, st_ref):
        n = pl.program_id(0)

        @pl.when(n == 0)
        def _():
            st_ref[...] = jnp.zeros_like(st_ref)

        q_all, k_all, v_all, r_all, glr = _p_load(p0, p1, p2, p3, p4, p5, p6, p7)
        _, _, lower, G, Gl = _gla_common(n, glr, gw2_ref[...], gb_ref[...])
        eG = jnp.exp(G)
        eN = jnp.exp(-G)
        eE = jnp.exp(Gl - G)
        dec = jnp.exp(Gl)
        gnw_v = gnw_ref[...]
        for h in range(HEADS):
            ks = slice(h * DK, (h + 1) * DK)
            vs = slice(h * DV, (h + 1) * DV)
            kh = k_all[:, ks]
            vh = v_all[:, vs].astype(BF16)
            qd = ((q_all[:, ks] * scale) * eG[:, ks]).astype(BF16)
            ki = (kh * eN[:, ks]).astype(BF16)
            ke = (kh * eE[:, ks]).astype(BF16)
            st = st_ref[h]
            a = jnp.where(lower, _dot_nt(qd, ki), 0.0).astype(BF16)
            o = _dot(a, vh) + _dot_nt(qd, st.astype(BF16))
            sp_ref[0, h] = st
            st_ref[h] = st * dec[:, ks] + _dot_tn(vh, ke)
            o_ref[:, vs] = o
            rs = lax.rsqrt(jnp.mean(o * o, axis=-1, keepdims=True) + EPS)
            rv = r_all[:, vs]
            gate = rv / (1.0 + jnp.exp(-rv))
            og_ref[:, vs] = (((o * rs) * gnw_v) * gate).astype(BF16)

    rv_ = pl.BlockSpec((CH, GW), lambda n: (n, 0))

    def full(shape):
        return pl.BlockSpec(shape, lambda n: tuple(0 for _ in shape))

    return pl.pallas_call(
        body, name="gla_fwd", grid=(NCH,),
        in_specs=_p_specs(lambda n: n) + [full((128, KW)), full((1, KW)), full((1, DV))],
        out_specs=[rv_, rv_, pl.BlockSpec((1, HEADS, DV, DK), lambda n: (n, 0, 0, 0))],
        out_shape=[_sds((TP, GW), F32), _sds((TP, GW), BF16), _sds((NCH, HEADS, DV, DK), F32)],
        scratch_shapes=[pltpu.VMEM((HEADS, DV, DK), F32)],
        compiler_params=_cp(("arbitrary",)),
    )(*([P] * 8), gw2, gb, gnw)


def _gla_bwd(dog, o, P, gw2, gb, gnw, sp, dep=None):
    scale = DK ** -0.5

    def body(dog_ref, o_ref, p0, p1, p2, p3, p4, p5, p6, p7, gw2_ref, gb_ref, gnw_ref, sp_ref, dep_ref,
             dq_ref, dk_ref, dv_ref, dr_ref, dglr_ref, dgw2_ref, dgb_ref, dgnw_ref, ds_ref):
        step = pl.program_id(0)
        n = NCH - 1 - step

        @pl.when(step == 0)
        def _():
            ds_ref[...] = jnp.zeros_like(ds_ref)
            dgw2_ref[...] = jnp.zeros_like(dgw2_ref)
            dgb_ref[...] = jnp.zeros_like(dgb_ref)
            dgnw_ref[...] = jnp.zeros_like(dgnw_ref)

        q_all, k_all, v_all, r_all, glr_v = _p_load(p0, p1, p2, p3, p4, p5, p6, p7)
        gw2_b = gw2_ref[...].astype(BF16)
        valid, g_raw, lower, G, Gl = _gla_common(n, glr_v, gw2_ref[...], gb_ref[...])
        upper = lax.broadcasted_iota(jnp.int32, (CH, CH), 0) <= lax.broadcasted_iota(jnp.int32, (CH, CH), 1)
        eG = jnp.exp(G)
        eN = jnp.exp(-G)
        eE = jnp.exp(Gl - G)
        dec = jnp.exp(Gl)
        gnw_v = gnw_ref[...]
        last = lax.broadcasted_iota(jnp.int32, (CH, 1), 0) == CH - 1
        dgnw_acc = jnp.zeros((1, DV), F32)
        dG_parts = []
        for h in range(HEADS):
            ks = slice(h * DK, (h + 1) * DK)
            vs = slice(h * DV, (h + 1) * DV)
            oh = o_ref[:, vs]
            rv = r_all[:, vs]
            dg = dog_ref[:, vs]
            sig = 1.0 / (1.0 + jnp.exp(-rv))
            gate = rv * sig
            rs = lax.rsqrt(jnp.mean(oh * oh, axis=-1, keepdims=True) + EPS)
            ohat = oh * rs
            dr_ref[:, vs] = ((dg * (ohat * gnw_v)) * (sig * (1.0 + rv * (1.0 - sig)))).astype(BF16)
            don = dg * gate
            dgnw_acc = dgnw_acc + jnp.sum(don * ohat, axis=0, keepdims=True)
            gxn = don * gnw_v
            do = (rs * (gxn - ohat * jnp.mean(gxn * ohat, axis=-1, keepdims=True))).astype(BF16)
            kh = k_all[:, ks]
            vh = v_all[:, vs].astype(BF16)
            qd_f = (q_all[:, ks] * scale) * eG[:, ks]
            ki_f = kh * eN[:, ks]
            ke_f = kh * eE[:, ks]
            qd, ki, ke = qd_f.astype(BF16), ki_f.astype(BF16), ke_f.astype(BF16)
            spt = sp_ref[0, h]
            dst = ds_ref[h]
            dst_b = dst.astype(BF16)
            a_t = jnp.where(upper, _dot_nt(ki, qd), 0.0).astype(BF16)
            da = jnp.where(lower, _dot_nt(do, vh), 0.0).astype(BF16)
            da_t = jnp.where(upper, _dot_nt(vh, do), 0.0).astype(BF16)
            dv_ref[:, vs] = (_dot(a_t, do) + _dot_nt(ke, dst_b)).astype(BF16)
            dqd = _dot(da, ki) + _dot(do, spt.astype(BF16))
            dki = _dot(da_t, qd)
            dke = _dot(vh, dst_b)
            ddec = jnp.sum(spt * dst, axis=0, keepdims=True)
            ds_ref[h] = dst * dec[:, ks] + _dot_tn(do, qd)
            dq_ref[:, ks] = ((dqd * eG[:, ks]) * scale).astype(BF16)
            dk_ref[:, ks] = (dki * eN[:, ks] + dke * eE[:, ks]).astype(BF16)
            dke_ke = dke * ke_f
            dG = dqd * qd_f - dki * ki_f - dke_ke
            dGl = jnp.sum(dke_ke, axis=0, keepdims=True) + ddec * dec[:, ks]
            dG_parts.append(dG + jnp.where(last, dGl, 0.0))
        dgnw_ref[...] += dgnw_acc
        dG_all = jnp.concatenate(dG_parts, axis=1)
        dlogg = jnp.where(valid, _tri_sum(upper.astype(BF16), dG_all), 0.0)
        dg_raw = (dlogg * (1.0 / TAU)) * (1.0 / (1.0 + jnp.exp(g_raw)))
        dgb_ref[...] += jnp.sum(dg_raw, axis=0, keepdims=True)
        dg_b = dg_raw.astype(BF16)
        dgw2_ref[...] += _dot_tn(glr_v.astype(BF16), dg_b)
        dglr_ref[...] = _dot_nt(dg_b, gw2_b).astype(BF16)

    def back(s):
        return NCH - 1 - s

    rk = pl.BlockSpec((CH, KW), lambda s: (back(s), 0))
    rv_ = pl.BlockSpec((CH, GW), lambda s: (back(s), 0))
    rg = pl.BlockSpec((CH, 128), lambda s: (back(s), 0))

    def full(shape):
        return pl.BlockSpec(shape, lambda s: tuple(0 for _ in shape))

    return pl.pallas_call(
        body, name="gla_bwd", grid=(NCH,),
        in_specs=[rv_, rv_] + _p_specs(back) + [full((128, KW)), full((1, KW)), full((1, DV)),
                  pl.BlockSpec((1, HEADS, DV, DK), lambda s: (back(s), 0, 0, 0)), ANY],
        out_specs=[rk, rk, rv_, rv_, rg, full((128, KW)), full((1, KW)), full((1, DV))],
        out_shape=[_sds((TP, KW), BF16), _sds((TP, KW), BF16), _sds((TP, GW), BF16), _sds((TP, GW), BF16),
                   _sds((TP, 128), BF16), _sds((128, KW), F32), _sds((1, KW), F32), _sds((1, DV), F32)],
        scratch_shapes=[pltpu.VMEM((HEADS, DV, DK), F32)],
        compiler_params=_cp(("arbitrary",)),
    )(dog, o, *([P] * 8), gw2, gb, gnw, sp, _dep(dep))


POOL_TR = 128
HALO = 16


def _pool_counts(base, nrows):
    rows = base + lax.broadcasted_iota(jnp.int32, (nrows, 1), 0)
    valid = (rows >= ROW_LO) & (rows < ROW_HI)
    t1 = (rows - ROW_LO + 1).astype(F32)
    cnts = [jnp.clip(t1, 1.0, float(w)) for w in WINDOWS]
    return valid, cnts


def _pool_fwd(P, pw, ps):
    def body(cur_ref, prev_ref, pw_ref, ps_ref, y_ref, op_ref):
        i = pl.program_id(0)
        cur = cur_ref[0]
        full = jnp.concatenate([prev_ref[0], cur], axis=0)
        s2 = full + pltpu.roll(full, 1, 0)
        s4 = s2 + pltpu.roll(s2, 2, 0)
        s8 = s4 + pltpu.roll(s4, 4, 0)
        s16 = s8 + pltpu.roll(s8, 8, 0)
        valid, cnts = _pool_counts(i * POOL_TR, POOL_TR)
        for g, s in enumerate((s2, s4, s8, s16)):
            cs = slice(g * GC, (g + 1) * GC)
            y = s[HALO:, cs] / cnts[g] - cur[:, cs]
            yb = jnp.where(valid, y, 0.0).astype(BF16)
            y_ref[:, cs] = yb
            op_ref[:, cs] = (_dot(yb, pw_ref[g].astype(BF16)) * ps_ref[:, cs]).astype(BF16)

    row = pl.BlockSpec((POOL_TR, PW), lambda i: (i, 0))
    per = POOL_TR // HALO
    return pl.pallas_call(
        body, name="pool_fwd", grid=(TP // POOL_TR,),
        in_specs=[pl.BlockSpec((1, POOL_TR, PW), lambda i: (3, i, 0)),
                  pl.BlockSpec((1, HALO, PW), lambda i: (3, jnp.maximum(i * per - 1, 0), 0)),
                  pl.BlockSpec((4, GC, GC), lambda i: (0, 0, 0)), pl.BlockSpec((1, PW), lambda i: (0, 0))],
        out_specs=[row, row],
        out_shape=[_sds((TP, PW), BF16), _sds((TP, PW), BF16)],
        compiler_params=_cp(("arbitrary",)),
    )(P, P, pw, ps)


def _pool_bwd(dop, y, pw, ps, dep=None):
    nblk = TP // HALO

    def body(cur_ref, nxt_ref, y_ref, pw_ref, ps_ref, dep_ref, dpu_ref, dpw_ref, dps_ref):
        i = pl.program_id(0)

        @pl.when(i == 0)
        def _():
            dpw_ref[...] = jnp.zeros_like(dpw_ref)
            dps_ref[...] = jnp.zeros_like(dps_ref)

        n_all = POOL_TR + HALO
        dcur = cur_ref[...]
        dall = jnp.concatenate([dcur, nxt_ref[...]], axis=0)
        valid, cnts = _pool_counts(i * POOL_TR, n_all)
        for g in range(4):
            cs = slice(g * GC, (g + 1) * GC)
            pwb = pw_ref[g].astype(BF16)
            yb = y_ref[:, cs]
            dyw = (dall[:, cs] * ps_ref[:, cs]).astype(BF16)
            dps_ref[:, cs] += jnp.sum(dcur[:, cs] * _dot(yb, pwb), axis=0, keepdims=True)
            dpw_ref[g] += _dot_tn(yb, dyw[0:POOL_TR, :])
            dyv = jnp.where(valid, _dot_nt(dyw, pwb), 0.0)
            e = dyv / cnts[g]
            w = WINDOWS[g]
            sh = 1
            while sh < w:
                e = e + pltpu.roll(e, n_all - sh, 0)
                sh *= 2
            dpu_ref[:, cs] = (e[0:POOL_TR, :] - dyv[0:POOL_TR, :]).astype(BF16)

    row = pl.BlockSpec((POOL_TR, PW), lambda i: (i, 0))
    per = POOL_TR // HALO
    return pl.pallas_call(
        body, name="pool_bwd", grid=(TP // POOL_TR,),
        in_specs=[pl.BlockSpec((POOL_TR, PW), lambda i: (i, 1)),
                  pl.BlockSpec((HALO, PW), lambda i: (jnp.minimum(i * per + per, nblk - 1), 1)),
                  row, pl.BlockSpec((4, GC, GC), lambda i: (0, 0, 0)), pl.BlockSpec((1, PW), lambda i: (0, 0)), ANY],
        out_specs=[row, pl.BlockSpec((4, GC, GC), lambda i: (0, 0, 0)), pl.BlockSpec((1, PW), lambda i: (0, 0))],
        out_shape=[_sds((TP, PW), BF16), _sds((4, GC, GC), F32), _sds((1, PW), F32)],
        compiler_params=_cp(("arbitrary",)),
    )(dop, dop, y, pw, ps, _dep(dep))


def _place():
    x, y, c = lax.axis_index("x"), lax.axis_index("y"), lax.axis_index("c")
    chips = [(1 - x, y), (x, 1 - y), (1 - x, 1 - y)]
    return x, y, c, chips


HBM = pl.BlockSpec(memory_space=pltpu.HBM)
SEM = pl.BlockSpec(memory_space=pltpu.SEMAPHORE)
EFFECT = pltpu.SideEffectType.DATAFLOW_SIDE_EFFECTING


def _cast_into(w, place, cols_out, name, dep=None):
    rows, cols = w.shape
    tr = 256

    def body(p_ref, w_ref, dep_ref, o_ref):
        if cols_out != cols:
            o_ref[0] = jnp.zeros((tr, cols_out), BF16)
            o_ref[0, :, 0:cols] = w_ref[...].astype(BF16)
        else:
            o_ref[0] = w_ref[...].astype(BF16)

    grid_spec = pltpu.PrefetchScalarGridSpec(
        num_scalar_prefetch=1, grid=(rows // tr,),
        in_specs=[pl.BlockSpec((tr, cols), lambda i, p: (i, 0)), ANY],
        out_specs=pl.BlockSpec((1, tr, cols_out), lambda i, p: (p[0], i, 0)))
    return pl.pallas_call(
        body, name=name, grid_spec=grid_spec,
        out_shape=_sds((N_CHIP, rows, cols_out), BF16),
        compiler_params=_cp(("arbitrary",)),
    )(place, w, _dep(dep))


def _cast_win(w, place, dep=None):
    rows, cols = w.shape
    tr = 256

    def body(p_ref, w_ref, dep_ref, o_ref, t_ref):
        t_ref[...] = jnp.zeros_like(t_ref)
        t_ref[:, 0:cols] = w_ref[...]
        t = t_ref[...]
        lane = lax.broadcasted_iota(jnp.int32, (tr, PAD_IN), 1)
        for kk in range(N_CHIP):
            @pl.when(p_ref[0] == kk)
            def _(kk=kk):
                if kk == 0:
                    placed = t
                elif kk < 3:
                    placed = pltpu.roll(t, 4 * kk, 1)
                else:
                    pool = pltpu.roll(t, PAD_IN - 4, 1)
                    gate = pltpu.roll(t, GW + 12, 1)
                    placed = jnp.where(lane < GW, pool, jnp.where((lane >= GW + 12) & (lane < GW + 16), gate, 0.0))
                o_ref[0] = placed.astype(BF16)

    grid_spec = pltpu.PrefetchScalarGridSpec(
        num_scalar_prefetch=1, grid=(rows // tr,),
        in_specs=[pl.BlockSpec((tr, cols), lambda i, p: (i, 0)), ANY],
        out_specs=pl.BlockSpec((1, tr, PAD_IN), lambda i, p: (p[0], i, 0)),
        scratch_shapes=[pltpu.VMEM((tr, PAD_IN), F32)])
    return pl.pallas_call(
        body, name="cast_win", grid_spec=grid_spec,
        out_shape=_sds((N_CHIP, rows, PAD_IN), BF16),
        compiler_params=_cp(("arbitrary",)),
    )(place, w, _dep(dep))


def _half_rows(ref, k, which):
    h = ref.shape[1] // 2
    return ref.at[k, pl.ds(pl.multiple_of(which * h, 8), h), :]


def _gather_small(meta, gw2, pw, dep=None):
    def body(meta_r, gw2_r, pw_r, dep_ref, metaF, gw2F, pwF, lsem, ssem, rsem):
        x, y, c, chips = _place()
        me = 2 * x + y

        def slots(k):
            return (metaF.at[:, pl.ds(pl.multiple_of(k * 512, 128), 512)],
                    gw2F.at[:, pl.ds(pl.multiple_of(k * 128, 128), 128)],
                    pwF.at[:, pl.ds(pl.multiple_of(k * 64, 8), 64), :])

        srcs = (meta_r, gw2_r, pw_r)
        local = [pltpu.make_async_copy(s, d, lsem.at[i]) for i, (s, d) in enumerate(zip(srcs, slots(me)))]
        sends = []
        for j, chip in enumerate(chips):
            for i, (s, d) in enumerate(zip(srcs, slots(me))):
                sends.append(pltpu.make_async_remote_copy(src_ref=s, dst_ref=d, send_sem=ssem.at[3 * j + i],
                                                          recv_sem=rsem.at[3 * j + i], device_id=(*chip, c),
                                                          device_id_type=MESH))
        for cp in local + sends:
            cp.start()
        for j, (cx, cy) in enumerate(chips):
            for i, (s, d) in enumerate(zip(srcs, slots(2 * cx + cy))):
                pltpu.make_async_remote_copy(src_ref=s, dst_ref=d, send_sem=ssem.at[3 * j + i], recv_sem=rsem.at[3 * j + i],
                                             device_id=(cx, cy, c), device_id_type=MESH).wait_recv()
        for cp in sends:
            cp.wait_send()
        for cp in local:
            cp.wait()

    return pl.pallas_call(
        body, name="gather_small",
        in_specs=[ANY] * 4, out_specs=[ANY] * 3,
        out_shape=[_sds((N_META, D), F32), _sds((RANK, KW), F32), _sds((4, GC, GC), F32)],
        scratch_shapes=[pltpu.SemaphoreType.DMA((3,)), pltpu.SemaphoreType.DMA((9,)), pltpu.SemaphoreType.DMA((9,))],
    )(meta, gw2, pw, _dep(dep))


def _gather_start(ws, name):
    n = len(ws)

    def body(*refs):
        ins = refs[:n]
        ssems = refs[n:2 * n]
        rsems = refs[2 * n:3 * n]
        token = refs[4 * n]
        x, y, c, chips = _place()
        me = 2 * x + y
        for w in range(n):
            blk = _half_rows(ins[w], me, c)
            for j, chip in enumerate(chips):
                pltpu.make_async_remote_copy(src_ref=blk, dst_ref=blk, send_sem=ssems[w].at[j], recv_sem=rsems[w].at[j],
                                             device_id=(*chip, c), device_id_type=MESH).start()
        token[...] = jnp.zeros_like(token)

    sem3 = pltpu.SemaphoreType.DMA((3,))
    outs = pl.pallas_call(
        body, name=name,
        out_shape=tuple([sem3] * (2 * n) + [pltpu.HBM(w.shape, w.dtype) for w in ws] + [_sds((8, 128), F32)]),
        in_specs=(HBM,) * n, out_specs=(SEM,) * (2 * n) + (HBM,) * n + (VMEM_FULL,),
        input_output_aliases={w: 2 * n + w for w in range(n)},
        compiler_params=pltpu.CompilerParams(has_side_effects=EFFECT),
    )(*[pltpu.with_memory_space_constraint(w, pltpu.HBM) for w in ws])
    return outs[:n], outs[n:2 * n], outs[2 * n:3 * n], outs[3 * n]


def _gather_wait(w, ssem, rsem, after, name):
    def body(w_ref, ssem_ref, rsem_ref, after_ref, out_ref):
        x, y, c, chips = _place()
        me = 2 * x + y
        mine = _half_rows(w_ref, me, c)
        for j, (cx, cy) in enumerate(chips):
            cp = pltpu.make_async_remote_copy(src_ref=mine, dst_ref=_half_rows(w_ref, 2 * cx + cy, c),
                                              send_sem=ssem_ref.at[j], recv_sem=rsem_ref.at[j],
                                              device_id=(cx, cy, c), device_id_type=MESH)
            cp.wait_send()
            cp.wait_recv()

    return pl.pallas_call(
        body, name=name, out_shape=pltpu.HBM(w.shape, w.dtype),
        in_specs=(HBM, SEM, SEM, ANY), out_specs=HBM, input_output_aliases={0: 0},
        compiler_params=pltpu.CompilerParams(has_side_effects=EFFECT),
    )(w, ssem, rsem, after)


def _forward_halves(w, name):
    def body(w_ref, o_ref, ssem, rsem):
        x, y, c, chips = _place()
        sib = (x, y, 1 - c)
        cps = []
        for j, (cx, cy) in enumerate(chips):
            blk = _half_rows(o_ref, 2 * cx + cy, c)
            cps.append(pltpu.make_async_remote_copy(src_ref=blk, dst_ref=blk, send_sem=ssem.at[j], recv_sem=rsem.at[j],
                                                    device_id=sib, device_id_type=MESH))
        for cp in cps:
            cp.start()
        for j, (cx, cy) in enumerate(chips):
            blk = _half_rows(o_ref, 2 * cx + cy, 1 - c)
            pltpu.make_async_remote_copy(src_ref=blk, dst_ref=blk, send_sem=ssem.at[j], recv_sem=rsem.at[j],
                                         device_id=sib, device_id_type=MESH).wait_recv()
        for cp in cps:
            cp.wait_send()

    return pl.pallas_call(
        body, name=name, in_specs=[ANY], out_specs=ANY, out_shape=_sds(w.shape, w.dtype),
        input_output_aliases={0: 0},
        scratch_shapes=[pltpu.SemaphoreType.DMA((3,)), pltpu.SemaphoreType.DMA((3,))],
    )(w)


def _rs_start(sb, name, after=None):
    _, half, cols = sb.shape

    def body(sb_ref, land_ref, after_ref, ssem, rsem, sb_out, land_out, token):
        x, y, c, chips = _place()
        for j, (cx, cy) in enumerate(chips):
            pltpu.make_async_remote_copy(src_ref=sb_ref.at[2 * cx + cy], dst_ref=land_ref.at[j], send_sem=ssem.at[j],
                                         recv_sem=rsem.at[j], device_id=(cx, cy, c), device_id_type=MESH).start()
        token[...] = jnp.zeros_like(token)

    sem3 = pltpu.SemaphoreType.DMA((3,))
    land = lax.empty((3, half, cols), BF16)
    return pl.pallas_call(
        body, name=name,
        out_shape=(sem3, sem3, pltpu.HBM(sb.shape, sb.dtype), pltpu.HBM(land.shape, land.dtype), _sds((8, 128), F32)),
        in_specs=(HBM, HBM, ANY), out_specs=(SEM, SEM, HBM, HBM, VMEM_FULL), input_output_aliases={0: 2, 1: 3},
        compiler_params=pltpu.CompilerParams(has_side_effects=EFFECT),
    )(pltpu.with_memory_space_constraint(sb, pltpu.HBM), pltpu.with_memory_space_constraint(land, pltpu.HBM), _dep(after))


def _rs_wait(sb, land, ssem, rsem, after, name):
    def body(sb_ref, land_ref, ssem_ref, rsem_ref, after_ref, sb_out, land_out):
        x, y, c, chips = _place()
        for j, (cx, cy) in enumerate(chips):
            cp = pltpu.make_async_remote_copy(src_ref=sb_ref.at[2 * cx + cy], dst_ref=land_ref.at[j], send_sem=ssem_ref.at[j],
                                              recv_sem=rsem_ref.at[j], device_id=(cx, cy, c), device_id_type=MESH)
            cp.wait_send()
            cp.wait_recv()

    return pl.pallas_call(
        body, name=name,
        out_shape=(pltpu.HBM(sb.shape, sb.dtype), pltpu.HBM(land.shape, land.dtype)),
        in_specs=(HBM, HBM, SEM, SEM, ANY), out_specs=(HBM, HBM), input_output_aliases={0: 0, 1: 1},
        compiler_params=pltpu.CompilerParams(has_side_effects=EFFECT),
    )(sb, land, ssem, rsem, after)[1]


def _pair_copy(g_ref, land_ref, ssem, rsem):
    x, y, c, _ = _place()
    h = g_ref.shape[1] // 2
    src = g_ref.at[:, pl.ds(pl.multiple_of((1 - c) * h, 8), h), :]
    return pltpu.make_async_remote_copy(src_ref=src, dst_ref=land_ref, send_sem=ssem.at[0], recv_sem=rsem.at[0],
                                        device_id=(x, y, 1 - c), device_id_type=MESH)


def _pair_start(g, name):
    def body(g_ref, land_ref, ssem, rsem, g_out, land_out, token):
        _pair_copy(g_ref, land_ref, ssem, rsem).start()
        token[...] = jnp.zeros_like(token)

    sem1 = pltpu.SemaphoreType.DMA((1,))
    land = lax.empty((N_CHIP, g.shape[1] // 2, g.shape[2]), F32)
    return pl.pallas_call(
        body, name=name,
        out_shape=(sem1, sem1, pltpu.HBM(g.shape, g.dtype), pltpu.HBM(land.shape, land.dtype), _sds((8, 128), F32)),
        in_specs=(HBM, HBM), out_specs=(SEM, SEM, HBM, HBM, VMEM_FULL), input_output_aliases={0: 2, 1: 3},
        compiler_params=pltpu.CompilerParams(has_side_effects=EFFECT),
    )(pltpu.with_memory_space_constraint(g, pltpu.HBM), pltpu.with_memory_space_constraint(land, pltpu.HBM))


def _pair_wait(g, land, ssem, rsem, after, name):
    def body(g_ref, land_ref, ssem_ref, rsem_ref, after_ref, g_out, land_out):
        cp = _pair_copy(g_ref, land_ref, ssem_ref, rsem_ref)
        cp.wait_send()
        cp.wait_recv()

    return pl.pallas_call(
        body, name=name,
        out_shape=(pltpu.HBM(g.shape, g.dtype), pltpu.HBM(land.shape, land.dtype)),
        in_specs=(HBM, HBM, SEM, SEM, ANY), out_specs=(HBM, HBM), input_output_aliases={0: 0, 1: 1},
        compiler_params=pltpu.CompilerParams(has_side_effects=EFFECT),
    )(g, land, ssem, rsem, after)


def _pair_sum(g, rcv, place, name):
    _, rows, cols = g.shape
    half = rows // 2
    tr = 256
    nt = half // tr

    def body(p_ref, g_ref, r_ref, sb_ref, sf_ref):
        s = pl.program_id(1)
        tot = g_ref[0] + r_ref[0]
        sb_ref[0] = tot.astype(BF16)

        @pl.when(s == p_ref[0])
        def _():
            sf_ref[...] = tot

    grid_spec = pltpu.PrefetchScalarGridSpec(
        num_scalar_prefetch=1, grid=(nt, N_CHIP),
        in_specs=[pl.BlockSpec((1, tr, cols), lambda t, s, p: (s, p[1] * nt + t, 0)),
                  pl.BlockSpec((1, tr, cols), lambda t, s, p: (s, t, 0))],
        out_specs=[pl.BlockSpec((1, tr, cols), lambda t, s, p: (s, t, 0)),
                   pl.BlockSpec((tr, cols), lambda t, s, p: (t, 0))])
    return pl.pallas_call(
        body, name=name, grid_spec=grid_spec,
        out_shape=[_sds((N_CHIP, half, cols), BF16), _sds((half, cols), F32)],
        compiler_params=_cp(("arbitrary", "arbitrary")),
    )(place, g, rcv)


def _final_sum(sf, rb, place, name):
    half, cols = sf.shape
    tr = 256
    nt = half // tr

    def body(p_ref, sf_ref, r_ref, out_ref):
        acc = sf_ref[...]
        for j in range(3):
            acc = acc + r_ref[j].astype(F32)
        out_ref[...] = acc

    grid_spec = pltpu.PrefetchScalarGridSpec(
        num_scalar_prefetch=1, grid=(nt,),
        in_specs=[pl.BlockSpec((tr, cols), lambda t, p: (t, 0)), pl.BlockSpec((3, tr, cols), lambda t, p: (0, t, 0))],
        out_specs=pl.BlockSpec((tr, cols), lambda t, p: (p[1] * nt + t, 0)))
    return pl.pallas_call(
        body, name=name, grid_spec=grid_spec,
        out_shape=_sds((2 * half, cols), F32),
        compiler_params=_cp(("arbitrary",)),
    )(place, sf, rb)


def _half_copy(f_ref, which, ssem, rsem):
    x, y, c, _ = _place()
    h = f_ref.shape[0] // 2
    rows = f_ref.at[pl.ds(pl.multiple_of(which * h, 8), h), :]
    return pltpu.make_async_remote_copy(src_ref=rows, dst_ref=rows, send_sem=ssem.at[0], recv_sem=rsem.at[0],
                                        device_id=(x, y, 1 - c), device_id_type=MESH)


def _half_start(full, name, after=None):
    def body(f_ref, after_ref, ssem, rsem, f_out, token):
        _half_copy(f_ref, lax.axis_index("c"), ssem, rsem).start()
        token[...] = jnp.zeros_like(token)

    sem1 = pltpu.SemaphoreType.DMA((1,))
    return pl.pallas_call(
        body, name=name,
        out_shape=(sem1, sem1, pltpu.HBM(full.shape, full.dtype), _sds((8, 128), F32)),
        in_specs=(HBM, ANY), out_specs=(SEM, SEM, HBM, VMEM_FULL), input_output_aliases={0: 2},
        compiler_params=pltpu.CompilerParams(has_side_effects=EFFECT),
    )(pltpu.with_memory_space_constraint(full, pltpu.HBM), _dep(after))


def _half_wait(full, ssem, rsem, after, name):
    def body(f_ref, ssem_ref, rsem_ref, after_ref, f_out):
        c = lax.axis_index("c")
        _half_copy(f_ref, c, ssem_ref, rsem_ref).wait_send()
        _half_copy(f_ref, 1 - c, ssem_ref, rsem_ref).wait_recv()

    return pl.pallas_call(
        body, name=name, out_shape=pltpu.HBM(full.shape, full.dtype),
        in_specs=(HBM, SEM, SEM, ANY), out_specs=HBM, input_output_aliases={0: 0},
        compiler_params=pltpu.CompilerParams(has_side_effects=EFFECT),
    )(full, ssem, rsem, after)


def _small_copies(src_ref, land_ref, ssem, rsem, first):
    x, y, c, chips = _place()
    if first:
        return [pltpu.make_async_remote_copy(src_ref=src_ref, dst_ref=land_ref, send_sem=ssem.at[0], recv_sem=rsem.at[0],
                                             device_id=(x, y, 1 - c), device_id_type=MESH)]
    return [pltpu.make_async_remote_copy(src_ref=src_ref, dst_ref=land_ref.at[j], send_sem=ssem.at[j], recv_sem=rsem.at[j],
                                         device_id=(*chip, c), device_id_type=MESH) for j, chip in enumerate(chips)]


def _small_start(src, first, name, after=None):
    n = 1 if first else 3

    def body(src_ref, land_ref, after_ref, ssem, rsem, src_out, land_out, token):
        for cp in _small_copies(src_ref, land_ref, ssem, rsem, first):
            cp.start()
        token[...] = jnp.zeros_like(token)

    sems = pltpu.SemaphoreType.DMA((n,))
    land = lax.empty(src.shape if first else (3,) + src.shape, F32)
    return pl.pallas_call(
        body, name=name,
        out_shape=(sems, sems, pltpu.HBM(src.shape, F32), pltpu.HBM(land.shape, F32), _sds((8, 128), F32)),
        in_specs=(HBM, HBM, ANY), out_specs=(SEM, SEM, HBM, HBM, VMEM_FULL), input_output_aliases={0: 2, 1: 3},
        compiler_params=pltpu.CompilerParams(has_side_effects=EFFECT),
    )(pltpu.with_memory_space_constraint(src, pltpu.HBM), pltpu.with_memory_space_constraint(land, pltpu.HBM), _dep(after))


def _small_wait(src, land, ssem, rsem, first, after, name):
    def body(src_ref, land_ref, ssem_ref, rsem_ref, after_ref, src_out, land_out):
        for cp in _small_copies(src_ref, land_ref, ssem_ref, rsem_ref, first):
            cp.wait_send()
            cp.wait_recv()

    return pl.pallas_call(
        body, name=name,
        out_shape=(pltpu.HBM(src.shape, F32), pltpu.HBM(land.shape, F32)),
        in_specs=(HBM, HBM, SEM, SEM, ANY), out_specs=(HBM, HBM), input_output_aliases={0: 0, 1: 1},
        compiler_params=pltpu.CompilerParams(has_side_effects=EFFECT),
    )(src, land, ssem, rsem, after)


def _small_pair_sum(vec, got):
    def body(v_ref, g_ref, o_ref):
        o_ref[...] = v_ref[...] + g_ref[...]

    return pl.pallas_call(body, name="small_pair_sum", in_specs=[VMEM_FULL] * 2, out_specs=VMEM_FULL,
                          out_shape=_sds(vec.shape, F32), compiler_params=_cp())(vec, got)


def _small_chip_sum(pair, got, place):
    def body(p_ref, pair_ref, got_ref, o_ref):
        acc = None
        for kk in range(N_CHIP):
            d = jnp.bitwise_xor(p_ref[0], kk)
            t = jnp.where(d == 0, pair_ref[...], jnp.where(d == 2, got_ref[0], jnp.where(d == 1, got_ref[1], got_ref[2])))
            acc = t if acc is None else acc + t
        o_ref[...] = acc

    grid_spec = pltpu.PrefetchScalarGridSpec(
        num_scalar_prefetch=1, grid=(1,),
        in_specs=[pl.BlockSpec(pair.shape, lambda i, p: (0, 0)), pl.BlockSpec(got.shape, lambda i, p: (0, 0, 0))],
        out_specs=pl.BlockSpec(pair.shape, lambda i, p: (0, 0)))
    return pl.pallas_call(body, name="small_chip_sum", grid_spec=grid_spec, out_shape=_sds(pair.shape, F32),
                          compiler_params=_cp(("arbitrary",)))(place, pair, got)


def _adam_math(w, g, m, v):
    m = B1 * m + (1.0 - B1) * g
    v = B2 * v + (1.0 - B2) * (g * g)
    m_hat = m / (1.0 - B1 ** STEP)
    v_hat = v / (1.0 - B2 ** STEP)
    delta = -LR * (m_hat / (jnp.sqrt(v_hat) + AEPS) + WD * w)
    return delta, m, v


def _adam_big(w, g, m, v, name):
    rows, cols = w.shape
    tr = 128

    def body(w_ref, g_ref, m_ref, v_ref, go_ref, d_ref, nm_ref, nv_ref):
        g = g_ref[...]
        d, nm, nv = _adam_math(w_ref[...], g, m_ref[...], v_ref[...])
        go_ref[...] = g
        d_ref[...] = d
        nm_ref[...] = nm
        nv_ref[...] = nv

    blk = pl.BlockSpec((tr, cols), lambda i: (i, 0))
    return pl.pallas_call(
        body, name=name, grid=(rows // tr,),
        in_specs=[blk] * 4, out_specs=[blk] * 4, out_shape=[_sds((rows, cols), F32)] * 4,
        compiler_params=_cp(("arbitrary",)),
    )(w, g, m, v)


def _adam_small(ws, gs, ms, vs, dep=None):
    n = len(ws)

    def body(*refs):
        for i in range(n):
            d, nm, nv = _adam_math(refs[i][...], refs[n + i][...], refs[2 * n + i][...], refs[3 * n + i][...])
            refs[4 * n + 1 + i][...] = d
            refs[5 * n + 1 + i][...] = nm
            refs[6 * n + 1 + i][...] = nv

    shapes = [_sds(w.shape, F32) for w in ws]
    outs = pl.pallas_call(
        body, name="adam_small",
        in_specs=[VMEM_FULL] * (4 * n) + [ANY], out_specs=[VMEM_FULL] * (3 * n), out_shape=shapes * 3,
        compiler_params=_cp(),
    )(*ws, *gs, *ms, *vs, _dep(dep))
    return outs[:n], outs[n:2 * n], outs[2 * n:]


def _pad_rows8(a):
    flat = a.reshape(-1, 128)
    pad = (-flat.shape[0]) % 8
    if pad:
        flat = jnp.concatenate([flat, jnp.zeros((pad, 128), F32)], axis=0)
    return flat


def kernel(x, meta_tokens, norm1_w, w_in, gate_w2, gate_b, gla_norm_w, pool_w, pool_scale, w_out, norm2_w, mlp_w1, mlp_w2, final_norm_w, loss_target, m_meta_tokens, m_norm1_w, m_w_in, m_gate_w2, m_gate_b, m_gla_norm_w, m_pool_w, m_pool_scale, m_w_out, m_norm2_w, m_mlp_w1, m_mlp_w2, m_final_norm_w, v_meta_tokens, v_norm1_w, v_w_in, v_gate_w2, v_gate_b, v_gla_norm_w, v_pool_w, v_pool_scale, v_w_out, v_norm2_w, v_mlp_w1, v_mlp_w2, v_final_norm_w):
    cx, cy, cc = lax.axis_index("x"), lax.axis_index("y"), lax.axis_index("c")
    me = (2 * cx + cy).astype(jnp.int32)

    place = jnp.stack([me, cc.astype(jnp.int32)])
    fw = final_norm_w.reshape(1, D)

    metaF, gw2F, pwF = _gather_small(meta_tokens, gate_w2[0], pool_w[0])
    (s_win,), (r_win,), (f_win,), tok = _gather_start([_cast_win(w_in[0], place, gw2F)], "gather_start_win")
    rest = [_cast_into(w_out[0], place, D, "cast_wout", tok), _cast_into(mlp_w1[0], place, D, "cast_w1", tok),
            _cast_into(mlp_w2[0], place, D, "cast_w2", tok)]
    ssems, rsems, flying, tok = _gather_start(rest, "gather_start_rest")
    ssems, rsems, flying = [s_win, *ssems], [r_win, *rsems], [f_win, *flying]

    def arrive(i, nm, after):
        return _forward_halves(_gather_wait(flying[i], ssems[i], rsems[i], after, "gather_wait_" + nm), "forward_" + nm)

    pairs, pending = {}, {}

    def grad_start(nm, g):
        ssem, rsem, g_thru, land, token = _pair_start(g, "pair_start_" + nm)
        pairs[nm] = (ssem, rsem, g_thru, land)
        return token

    def grad_finish(nm, after):
        ssem, rsem, g_thru, land = pairs[nm]
        g, rcv = _pair_wait(g_thru, land, ssem, rsem, after, "pair_wait_" + nm)
        sb, sf = _pair_sum(g, rcv, place, "pair_sum_" + nm)
        if nm == "win":
            pending[nm] = (sf, sb)
            return sf
        ssem, rsem, sb_thru, land, token = _rs_start(sb, "rs_start_" + nm)
        pending[nm] = (sf, ssem, rsem, sb_thru, land)
        return token

    (grad_x, loss8, d_n1w, d_gb, d_gnw, d_ps, d_n2w, d_fw, d_meta, d_gw2, d_pw) = _local_step(
        x[0], loss_target[0], lambda after: arrive(0, "win", after), lambda after: arrive(1, "wout", after).reshape(D, D),
        lambda after: arrive(2, "w1", after), lambda after: arrive(3, "w2", after).reshape(DFF, D), metaF, gw2F, pwF,
        norm1_w, gate_b, gla_norm_w, pool_scale, norm2_w, fw, grad_start, grad_finish, tok, (m_w_in[0], v_w_in[0]))
    return _reduce_and_update(
        me, place, pending, grad_x, loss8, d_n1w, d_gb, d_gnw, d_ps, d_n2w, d_fw, d_meta, d_gw2, d_pw,
        meta_tokens, norm1_w, w_in, gate_w2, gate_b, gla_norm_w, pool_w, pool_scale, w_out, norm2_w, mlp_w1, mlp_w2, fw,
        m_meta_tokens, m_norm1_w, m_w_in, m_gate_w2, m_gate_b, m_gla_norm_w, m_pool_w, m_pool_scale, m_w_out, m_norm2_w,
        m_mlp_w1, m_mlp_w2, m_final_norm_w, v_meta_tokens, v_norm1_w, v_w_in, v_gate_w2, v_gate_b, v_gla_norm_w, v_pool_w,
        v_pool_scale, v_w_out, v_norm2_w, v_mlp_w1, v_mlp_w2, v_final_norm_w)


def _local_step(x, target, get_win, get_wout, get_w1, get_w2, metaF, gw2F, pwF, norm1_w, gate_b, gla_norm_w, pool_scale,
                norm2_w, fw, grad_start, grad_finish, first=None, early=()):
    h0, u = _embed_norm(x, metaF, norm1_w, first)
    Win = get_win(u)
    P = _in_proj(u, Win)
    gw2p = jnp.pad(gw2F, ((0, 128 - RANK), (0, 0)))
    o, og, sp = _gla_fwd(P, gw2p, gate_b, gla_norm_w)
    yb, op = _pool_fwd(P, pwF, pool_scale)
    Wout = get_wout(op)
    h1 = _out_proj(og, op, Wout, h0)
    n2 = _norm_rows(h1, norm2_w, "norm2")
    W1 = get_w1(n2)
    zr, a = _mlp_up(n2, W1, early)
    W2 = get_w2(a)
    h2 = _mlp_down(a, W2, h1)

    dh2, dh2b, d_fw, loss8 = _loss_head(h2, target, fw)
    tok = grad_start("w2", _grad_w2(a, dh2b).reshape(N_CHIP, D, D))
    dz = _mlp_dz(dh2b, W2, zr, tok)
    tok = grad_finish("w2", dz)
    tok = grad_start("w1", _grad_w1(n2, dz, tok))
    dn2 = _mlp_dn(dz, W1, tok)
    tok = grad_finish("w1", dn2)
    dh1, dh1b, d_n2w = _norm_bwd(dn2, h1, dh2, norm2_w, "norm2_bwd", tok)
    dmixed = _mixed_grad(dh1b, Wout)
    tok = grad_start("wout", _grad_wout(og, op, dh1b))
    dpu, d_pw, d_ps = _pool_bwd(dmixed, yb, pwF, pool_scale, tok)
    dq, dk, dv, dr, dglr, d_gw2p, d_gb, d_gnw = _gla_bwd(dmixed, o, P, gw2p, gate_b, gla_norm_w, sp, tok)
    d_gw2 = d_gw2p[0:RANK]
    tok = grad_finish("wout", dq)
    tok = grad_start("win", _grad_win(u, dq, dk, dv, dr, dglr, dpu, tok))
    du = _in_grad(dq, dk, dv, dr, dglr, dpu, Win, tok)
    tok = grad_finish("win", du)
    grad_x, d_meta, d_n1w = _input_grad(du, h0, dh1, norm1_w, tok)
    return grad_x, loss8, d_n1w, d_gb, d_gnw, d_ps, d_n2w, d_fw, d_meta, d_gw2, d_pw


def _reduce_and_update(me, place, pending, grad_x, loss8, d_n1w, d_gb, d_gnw, d_ps, d_n2w, d_fw, d_meta, d_gw2, d_pw,
                       meta_tokens, norm1_w, w_in, gate_w2, gate_b, gla_norm_w, pool_w, pool_scale, w_out, norm2_w,
                       mlp_w1, mlp_w2, fw, m_meta_tokens, m_norm1_w, m_w_in, m_gate_w2, m_gate_b, m_gla_norm_w, m_pool_w,
                       m_pool_scale, m_w_out, m_norm2_w, m_mlp_w1, m_mlp_w2, m_final_norm_w, v_meta_tokens, v_norm1_w, v_w_in,
                       v_gate_w2, v_gate_b, v_gla_norm_w, v_pool_w, v_pool_scale, v_w_out, v_norm2_w, v_mlp_w1, v_mlp_w2,
                       v_final_norm_w):
    parts = [loss8, d_n1w, d_gb, d_gnw, d_ps, d_n2w, d_fw, d_meta, d_gw2, d_pw]
    packed = [_pad_rows8(p) for p in parts]
    sizes = [p.shape[0] for p in packed]
    vec = jnp.concatenate(packed, axis=0)

    big, halves = {}, {}
    params = {"w2": (mlp_w2, m_mlp_w2, v_mlp_w2), "w1": (mlp_w1, m_mlp_w1, v_mlp_w1), "wout": (w_out, m_w_out, v_w_out),
              "win": (w_in, m_w_in, v_w_in)}

    def reduce_(nm, after):
        sf, ssem, rsem, sb_thru, land = pending[nm]
        rb = _rs_wait(sb_thru, land, ssem, rsem, after, "rs_wait_" + nm)
        hs, hr, full, token = _half_start(_final_sum(sf, rb, place, "final_sum_" + nm), "half_start_" + nm)
        halves[nm] = (hs, hr, full)
        return token

    def update(nm, after):
        hs, hr, full = halves[nm]
        w, m, v = params[nm]
        big[nm] = _adam_big(w[0], _half_wait(full, hs, hr, after, "half_wait_" + nm), m[0], v[0], "adam_" + nm)
        return big[nm][3]

    s1, r1, vec, land1, tok = _small_start(vec, True, "small_start_pair")
    tok = reduce_("w2", tok)
    vec, got = _small_wait(vec, land1, s1, r1, True, tok, "small_wait_pair")
    pair = _small_pair_sum(vec, got)
    s2, r2, pair, land2, tok = _small_start(pair, False, "small_start_chips")
    sf, sb = pending["win"]
    ssem, rsem, sb_thru, land, tok = _rs_start(sb, "rs_start_win", tok)
    pending["win"] = (sf, ssem, rsem, sb_thru, land)
    tok = reduce_("w1", tok)
    tok = update("w2", tok)
    tok = update("w1", tok)
    pair, got = _small_wait(pair, land2, s2, r2, False, tok, "small_wait_chips")
    red = _small_chip_sum(pair, got, place)
    tok = reduce_("win", red)
    tok = reduce_("wout", tok)
    tok = update("win", tok)
    after = update("wout", tok)
    offs = [0]
    for s in sizes:
        offs.append(offs[-1] + s)

    def take(i, shape):
        n = 1
        for d in shape:
            n *= d
        return red[offs[i]:offs[i] + n // 128].reshape(shape)

    loss = red[0, 0]
    G_n1w = take(1, (1, D))
    G_gb = take(2, (1, KW))
    G_gnw = take(3, (1, DV))
    G_ps = take(4, (1, PW))
    G_n2w = take(5, (1, D))
    G_fw = take(6, (1, D))
    G_meta = lax.dynamic_slice(take(7, (N_META, D)), (0, me * 512), (N_META, 512))
    G_gw2 = lax.dynamic_slice(take(8, (RANK, KW)), (0, me * 128), (RANK, 128))
    G_pw = lax.dynamic_slice(take(9, (4, GC, GC)), (0, me * 64, 0), (4, 64, GC))

    G_win, d_win, nm_win, nv_win = big["win"]
    G_wout, d_wout, nm_wout, nv_wout = big["wout"]
    G_w1, d_w1, nm_w1, nv_w1 = big["w1"]
    G_w2, d_w2, nm_w2, nv_w2 = big["w2"]
    ws = [meta_tokens, norm1_w, gate_w2[0], gate_b, gla_norm_w, pool_w[0], pool_scale, norm2_w, fw]
    gs = [G_meta, G_n1w, G_gw2, G_gb, G_gnw, G_pw, G_ps, G_n2w, G_fw]
    ms = [m_meta_tokens, m_norm1_w, m_gate_w2[0], m_gate_b, m_gla_norm_w, m_pool_w[0], m_pool_scale, m_norm2_w,
          m_final_norm_w.reshape(1, D)]
    vs = [v_meta_tokens, v_norm1_w, v_gate_w2[0], v_gate_b, v_gla_norm_w, v_pool_w[0], v_pool_scale, v_norm2_w,
          v_final_norm_w.reshape(1, D)]
    ds, nms, nvs = _adam_small(ws, gs, ms, vs, after)

    def assemble(small, win_, wout_, w1_, w2_):
        meta_, n1_, gw2_, gb_, gnw_, pw_, ps_, n2_, fw_ = small
        return (meta_, n1_, win_[None], gw2_[None], gb_, gnw_, pw_[None], ps_, wout_[None], n2_, w1_[None], w2_[None],
                fw_.reshape(D))

    grads_out = assemble(gs, G_win, G_wout, G_w1, G_w2)
    deltas = assemble(ds, d_win, d_wout, d_w1, d_w2)
    new_m = assemble(nms, nm_win, nm_wout, nm_w1, nm_w2)
    new_v = assemble(nvs, nv_win, nv_wout, nv_w1, nv_w2)
    return (loss, grad_x[None], *grads_out, *deltas, *new_m, *new_v)
```

```python
import functools

import jax
import jax.numpy as jnp
from jax import lax
from jax.experimental import pallas as pl
from jax.experimental.pallas import tpu as pltpu

F32 = jnp.float32
BF16 = jnp.bfloat16

D = 2048
SEQ = 2048
N_META = 16
CH = 64
TP = 2176
NCH = TP // CH
ROW_LO = 112
X_LO = 128
ROW_HI = TP
XT = 128
NXT = TP // XT
HEADS = 4
DK = 128
DV = 256
KW = HEADS * DK
GW = HEADS * DV
RANK = 16
TAU = 16.0
WINDOWS = (2, 4, 8, 16)
PW = 1024
GC = 256
DFF = 8192
EPS = 1e-6
SHARD_IN = 1028
PAD_IN = 1152
N_CHIP = 4

LR = 0.001
B1 = 0.9
B2 = 0.999
AEPS = 1e-08
WD = 0.01
STEP = 10

VMEM_LIMIT = 60 * 1024 * 1024
ANY = pl.BlockSpec(memory_space=pl.ANY)
VMEM_FULL = pl.BlockSpec(memory_space=pltpu.VMEM)
MESH = pl.DeviceIdType.MESH


def _cp(sem=None):
    if sem is None:
        return pltpu.CompilerParams(vmem_limit_bytes=VMEM_LIMIT)
    return pltpu.CompilerParams(dimension_semantics=sem, vmem_limit_bytes=VMEM_LIMIT)


def _dot(a, b):
    return jnp.dot(a, b, preferred_element_type=F32)


def _dot_nt(a, b):
    return lax.dot_general(a, b, (((1,), (1,)), ((), ())), preferred_element_type=F32)


def _dot_tn(a, b):
    return lax.dot_general(a, b, (((0,), (0,)), ((), ())), preferred_element_type=F32)


def _sds(shape, dtype):
    return jax.ShapeDtypeStruct(shape, dtype)


def _embed_norm(x, meta_full, w, dep=None):
    def body(x_ref, meta_ref, w_ref, dep_ref, h_ref, u_ref):
        i = pl.program_id(0)

        @pl.when(i == 0)
        def _():
            h_ref[...] = jnp.zeros_like(h_ref)
            h_ref[ROW_LO:X_LO, :] = meta_ref[...]

        @pl.when(i >= 1)
        def _():
            h_ref[...] = x_ref[...]

        h = h_ref[...]
        r = lax.rsqrt(jnp.mean(h * h, axis=-1, keepdims=True) + EPS)
        u_ref[...] = ((h * r) * w_ref[...]).astype(BF16)

    return pl.pallas_call(
        body, name="embed_norm1", grid=(NXT,),
        in_specs=[pl.BlockSpec((XT, D), lambda i: (jnp.maximum(i - 1, 0), 0)),
                  pl.BlockSpec((N_META, D), lambda i: (0, 0)),
                  pl.BlockSpec((1, D), lambda i: (0, 0)), ANY],
        out_specs=[pl.BlockSpec((XT, D), lambda i: (i, 0)), pl.BlockSpec((XT, D), lambda i: (i, 0))],
        out_shape=[_sds((TP, D), F32), _sds((TP, D), BF16)],
        compiler_params=_cp(("arbitrary",)),
    )(x, meta_full, w, _dep(dep))


def _norm_rows(h, w, name):
    tr = 272

    def body(h_ref, w_ref, o_ref):
        hv = h_ref[...]
        r = lax.rsqrt(jnp.mean(hv * hv, axis=-1, keepdims=True) + EPS)
        o_ref[...] = ((hv * r) * w_ref[...]).astype(BF16)

    return pl.pallas_call(
        body, name=name, grid=(TP // tr,),
        in_specs=[pl.BlockSpec((tr, D), lambda i: (i, 0)), pl.BlockSpec((1, D), lambda i: (0, 0))],
        out_specs=pl.BlockSpec((tr, D), lambda i: (i, 0)),
        out_shape=_sds((TP, D), BF16),
        compiler_params=_cp(("arbitrary",)),
    )(h, w)


def _loss_head(h2, target, fw):
    def body(h_ref, t_ref, w_ref, dh_ref, dhb_ref, dw_ref, loss_ref):
        i = pl.program_id(0)

        @pl.when(i == 0)
        def _():
            dw_ref[...] = jnp.zeros_like(dw_ref)
            loss_ref[...] = jnp.zeros_like(loss_ref)

        h = h_ref[...]
        w = w_ref[...]
        r = lax.rsqrt(jnp.mean(h * h, axis=-1, keepdims=True) + EPS)
        xh = h * r
        y = xh * w
        is_x = (i >= 1).astype(F32)
        diff = (y - t_ref[...]) * is_x
        loss_ref[...] += jnp.sum(diff * diff) * (0.5 / D)
        dy = diff * (1.0 / D)
        dw_ref[...] += jnp.sum(dy * xh, axis=0, keepdims=True)
        gx = dy * w
        dh = r * (gx - xh * jnp.mean(gx * xh, axis=-1, keepdims=True))
        dh_ref[...] = dh
        dhb_ref[...] = dh.astype(BF16)

    return pl.pallas_call(
        body, name="loss_head", grid=(NXT,),
        in_specs=[pl.BlockSpec((XT, D), lambda i: (i, 0)),
                  pl.BlockSpec((XT, D), lambda i: (jnp.maximum(i - 1, 0), 0)),
                  pl.BlockSpec((1, D), lambda i: (0, 0))],
        out_specs=[pl.BlockSpec((XT, D), lambda i: (i, 0)), pl.BlockSpec((XT, D), lambda i: (i, 0)),
                   pl.BlockSpec((1, D), lambda i: (0, 0)), pl.BlockSpec((8, 128), lambda i: (0, 0))],
        out_shape=[_sds((TP, D), F32), _sds((TP, D), BF16), _sds((1, D), F32), _sds((8, 128), F32)],
        compiler_params=_cp(("arbitrary",)),
    )(h2, target, fw)


def _norm_bwd(dn, h, dres, w, name, dep=None):
    tr = 272

    def body(dn_ref, h_ref, dres_ref, w_ref, dep_ref, o_ref, ob_ref, dw_ref):
        @pl.when(pl.program_id(0) == 0)
        def _():
            dw_ref[...] = jnp.zeros_like(dw_ref)

        hv = h_ref[...]
        dnv = dn_ref[...]
        r = lax.rsqrt(jnp.mean(hv * hv, axis=-1, keepdims=True) + EPS)
        xh = hv * r
        dw_ref[...] += jnp.sum(dnv * xh, axis=0, keepdims=True)
        gx = dnv * w_ref[...]
        dh = dres_ref[...] + r * (gx - xh * jnp.mean(gx * xh, axis=-1, keepdims=True))
        o_ref[...] = dh
        ob_ref[...] = dh.astype(BF16)

    row = pl.BlockSpec((tr, D), lambda i: (i, 0))
    vec = pl.BlockSpec((1, D), lambda i: (0, 0))
    return pl.pallas_call(
        body, name=name, grid=(TP // tr,),
        in_specs=[row, row, row, vec, ANY], out_specs=[row, row, vec],
        out_shape=[_sds((TP, D), F32), _sds((TP, D), BF16), _sds((1, D), F32)],
        compiler_params=_cp(("arbitrary",)),
    )(dn, h, dres, w, _dep(dep))


def _input_grad(du, h0, dh1, w, dep=None):
    def body(du_ref, h_ref, dres_ref, w_ref, dep_ref, gx_ref, gm_ref, dw_ref):
        i = pl.program_id(0)

        @pl.when(i == 0)
        def _():
            dw_ref[...] = jnp.zeros_like(dw_ref)

        hv = h_ref[...]
        dnv = du_ref[...]
        r = lax.rsqrt(jnp.mean(hv * hv, axis=-1, keepdims=True) + EPS)
        xh = hv * r
        dw_ref[...] += jnp.sum(dnv * xh, axis=0, keepdims=True)
        g = dnv * w_ref[...]
        dh = dres_ref[...] + r * (g - xh * jnp.mean(g * xh, axis=-1, keepdims=True))

        @pl.when(i == 0)
        def _():
            gm_ref[...] = dh[ROW_LO:X_LO, :]

        @pl.when(i >= 1)
        def _():
            gx_ref[...] = dh

    row = pl.BlockSpec((XT, D), lambda i: (i, 0))
    vec = pl.BlockSpec((1, D), lambda i: (0, 0))
    return pl.pallas_call(
        body, name="input_grad", grid=(NXT,),
        in_specs=[row, row, row, vec, ANY],
        out_specs=[pl.BlockSpec((XT, D), lambda i: (jnp.maximum(i - 1, 0), 0)),
                   pl.BlockSpec((N_META, D), lambda i: (0, 0)), vec],
        out_shape=[_sds((SEQ, D), F32), _sds((N_META, D), F32), _sds((1, D), F32)],
        compiler_params=_cp(("arbitrary",)),
    )(du, h0, dh1, w, _dep(dep))


def _in_proj(u, wg):
    def body(u_ref, w_ref, o_ref):
        o_ref[0] = _dot(u_ref[...], w_ref[0])

    return pl.pallas_call(
        body, name="in_proj", grid=(N_CHIP,),
        in_specs=[VMEM_FULL, pl.BlockSpec((1, D, PAD_IN), lambda k: (k, 0, 0))],
        out_specs=pl.BlockSpec((1, TP, PAD_IN), lambda k: (k, 0, 0)),
        out_shape=_sds((N_CHIP, TP, PAD_IN), F32),
        compiler_params=_cp(("arbitrary",)),
    )(u, wg)


def _out_proj(og, op, wout, h0):
    tn = 512

    def body(og_ref, op_ref, w_ref, h_ref, o_ref):
        acc = _dot(og_ref[...], w_ref[0:GW, :]) + _dot(op_ref[...], w_ref[GW:D, :])
        o_ref[...] = h_ref[...] + acc

    return pl.pallas_call(
        body, name="out_proj", grid=(D // tn,),
        in_specs=[VMEM_FULL, VMEM_FULL, pl.BlockSpec((D, tn), lambda j: (0, j)),
                  pl.BlockSpec((TP, tn), lambda j: (0, j))],
        out_specs=pl.BlockSpec((TP, tn), lambda j: (0, j)),
        out_shape=_sds((TP, D), F32),
        compiler_params=_cp(("arbitrary",)),
    )(og, op, wout, h0)


def _mlp_up(n2, w1g, early=()):
    tn = 1024
    per = D // tn
    ne = len(early)

    def body(n_ref, w_ref, *rest):
        zr_ref, a_ref = rest[ne:]
        z = jnp.maximum(_dot(n_ref[...], w_ref[0]), 0.0)
        zr_ref[...] = z.astype(BF16)
        a_ref[...] = (z * z).astype(BF16)

    col = pl.BlockSpec((TP, tn), lambda k, j: (0, k * per + j))
    return pl.pallas_call(
        body, name="mlp_up", grid=(N_CHIP, per),
        in_specs=[VMEM_FULL, pl.BlockSpec((1, D, tn), lambda k, j: (k, 0, j))] + [ANY] * ne,
        out_specs=[col, col],
        out_shape=[_sds((TP, DFF), BF16), _sds((TP, DFF), BF16)],
        compiler_params=_cp(("arbitrary", "arbitrary")),
    )(n2, w1g, *early)


def _mlp_down(a, w2, h1):
    tk = 1024
    nk = DFF // tk

    def body(a_ref, w_ref, h_ref, o_ref, acc_ref):
        k = pl.program_id(0)

        @pl.when(k == 0)
        def _():
            pltpu.sync_copy(h_ref, acc_ref)

        acc_ref[...] += _dot(a_ref[...], w_ref[...])

        @pl.when(k == nk - 1)
        def _():
            pltpu.sync_copy(acc_ref, o_ref)

    return pl.pallas_call(
        body, name="mlp_down", grid=(nk,),
        in_specs=[pl.BlockSpec((TP, tk), lambda k: (0, k)), pl.BlockSpec((tk, D), lambda k: (k, 0)), ANY],
        out_specs=ANY,
        out_shape=_sds((TP, D), F32),
        scratch_shapes=[pltpu.VMEM((TP, D), F32)],
        compiler_params=_cp(("arbitrary",)),
    )(a, w2, h1)


def _mlp_dz(dh2b, w2, zr, dep=None):
    tn = 1024

    def body(d_ref, w_ref, z_ref, dep_ref, o_ref):
        da = _dot_nt(d_ref[...], w_ref[...])
        o_ref[...] = (da * (2.0 * z_ref[...].astype(F32))).astype(BF16)

    col = pl.BlockSpec((TP, tn), lambda j: (0, j))
    return pl.pallas_call(
        body, name="mlp_dz", grid=(DFF // tn,),
        in_specs=[VMEM_FULL, pl.BlockSpec((tn, D), lambda j: (j, 0)), col, ANY],
        out_specs=col,
        out_shape=_sds((TP, DFF), BF16),
        compiler_params=_cp(("arbitrary",)),
    )(dh2b, w2, zr, _dep(dep))


def _grad_w2(a, dh2b):
    tm = 1024

    def body(a_ref, d_ref, o_ref):
        o_ref[...] = _dot_tn(a_ref[...], d_ref[...])

    return pl.pallas_call(
        body, name="grad_w2", grid=(DFF // tm,),
        in_specs=[pl.BlockSpec((TP, tm), lambda j: (0, j)), VMEM_FULL],
        out_specs=pl.BlockSpec((tm, D), lambda j: (j, 0)),
        out_shape=_sds((DFF, D), F32),
        compiler_params=_cp(("arbitrary",)),
    )(a, dh2b)


def _dep(token):
    return jnp.zeros((8, 128), F32) if token is None else token


def _grad_w1(n2, dz, dep=None):
    tn = 1024
    per = D // tn

    def body(n_ref, d_ref, dep_ref, o_ref):
        o_ref[0] = _dot_tn(n_ref[...], d_ref[...])

    return pl.pallas_call(
        body, name="grad_w1", grid=(N_CHIP, per),
        in_specs=[VMEM_FULL, pl.BlockSpec((TP, tn), lambda k, j: (0, k * per + j)), ANY],
        out_specs=pl.BlockSpec((1, D, tn), lambda k, j: (k, 0, j)),
        out_shape=_sds((N_CHIP, D, D), F32),
        compiler_params=_cp(("arbitrary", "arbitrary")),
    )(n2, dz, _dep(dep))


def _mlp_dn(dz, w1g, dep=None):
    tk = 1024
    per = D // tk
    nk = DFF // tk

    def body(d_ref, w_ref, dep_ref, o_ref, acc_ref):
        k = pl.program_id(0)
        part = _dot_nt(d_ref[...], w_ref[0])

        @pl.when(k == 0)
        def _():
            acc_ref[...] = part

        @pl.when(k > 0)
        def _():
            acc_ref[...] += part

        @pl.when(k == nk - 1)
        def _():
            pltpu.sync_copy(acc_ref, o_ref)

    return pl.pallas_call(
        body, name="mlp_dn", grid=(nk,),
        in_specs=[pl.BlockSpec((TP, tk), lambda k: (0, k)),
                  pl.BlockSpec((1, D, tk), lambda k: (k // per, 0, k % per)), ANY],
        out_specs=ANY,
        out_shape=_sds((TP, D), F32),
        scratch_shapes=[pltpu.VMEM((TP, D), F32)],
        compiler_params=_cp(("arbitrary",)),
    )(dz, w1g, _dep(dep))


def _mixed_grad(dh1b, wout):
    tn = 512

    def body(d_ref, w_ref, o_ref):
        o_ref[...] = _dot_nt(d_ref[...], w_ref[...])

    return pl.pallas_call(
        body, name="mixed_grad", grid=(D // tn,),
        in_specs=[VMEM_FULL, pl.BlockSpec((tn, D), lambda j: (j, 0))],
        out_specs=pl.BlockSpec((TP, tn), lambda j: (0, j)),
        out_shape=_sds((TP, D), F32),
        compiler_params=_cp(("arbitrary",)),
    )(dh1b, wout)


def _grad_wout(og, op, dh1b):
    tm = 512

    def body(og_ref, op_ref, d_ref, o_ref):
        j = pl.program_id(0)

        @pl.when(j < 2)
        def _():
            o_ref[0] = _dot_tn(og_ref[...], d_ref[...])

        @pl.when(j >= 2)
        def _():
            o_ref[0] = _dot_tn(op_ref[...], d_ref[...])

    return pl.pallas_call(
        body, name="grad_wout", grid=(N_CHIP,),
        in_specs=[pl.BlockSpec((TP, tm), lambda j: (0, jnp.minimum(j, 1))),
                  pl.BlockSpec((TP, tm), lambda j: (0, jnp.maximum(j - 2, 0))), VMEM_FULL],
        out_specs=pl.BlockSpec((1, tm, D), lambda j: (j, 0, 0)),
        out_shape=_sds((N_CHIP, tm, D), F32),
        compiler_params=_cp(("arbitrary",)),
    )(og, op, dh1b)


def _in_grad(dq, dk, dv, dr, dglr, dpu, wg, dep=None):
    def body(dq_ref, dk_ref, dv_ref, dr_ref, dg_ref, dpu_ref, w_ref, dep_ref, o_ref):
        dv, dr, dg = dv_ref[...], dr_ref[...], dg_ref[...]
        head, tail = slice(0, GW), slice(GW, PAD_IN)
        o_ref[...] = (_dot_nt(dq_ref[...], w_ref[0, :, 0:KW]) + _dot_nt(dk_ref[...], w_ref[0, :, KW:GW])
                      + _dot_nt(dv[:, 0:128], w_ref[0, :, tail])
                      + _dot_nt(dv, w_ref[1, :, head]) + _dot_nt(dr[:, 0:128], w_ref[1, :, tail])
                      + _dot_nt(dr, w_ref[2, :, head]) + _dot_nt(dg, w_ref[2, :, tail])
                      + _dot_nt(dpu_ref[...], w_ref[3, :, head]) + _dot_nt(dg, w_ref[3, :, tail]))

    tn = 512
    return pl.pallas_call(
        body, name="in_grad", grid=(D // tn,),
        in_specs=[VMEM_FULL] * 6 + [pl.BlockSpec((N_CHIP, tn, PAD_IN), lambda j: (0, j, 0)), ANY],
        out_specs=pl.BlockSpec((TP, tn), lambda j: (0, j)),
        out_shape=_sds((TP, D), F32),
        compiler_params=_cp(("arbitrary",)),
    )(dq, dk, dv, dr, dglr, dpu, wg, _dep(dep))


def _grad_win(u, dq, dk, dv, dr, dglr, dpu, dep=None):
    tm = 512

    def body(u_ref, dq_hbm, dk_hbm, dv_hbm, dr_hbm, dg_hbm, dpu_hbm, dep_ref, o_ref, dp_ref, sem):
        k, m = pl.program_id(0), pl.program_id(1)
        head, tail = slice(0, GW), slice(GW, PAD_IN)
        pieces = [[(dq_hbm, slice(0, KW)), (dk_hbm, slice(KW, GW)), (dv_hbm.at[:, 0:128], tail)],
                  [(dv_hbm, head), (dr_hbm.at[:, 0:128], tail)],
                  [(dr_hbm, head), (dg_hbm, tail)],
                  [(dpu_hbm, head), (dg_hbm, tail)]]

        def copies(kk):
            return [pltpu.make_async_copy(src, dp_ref.at[kk % 2, :, cols], sem.at[kk % 2, i])
                    for i, (src, cols) in enumerate(pieces[kk])]

        @pl.when((k == 0) & (m == 0))
        def _():
            for cp in copies(0):
                cp.start()

        for kk in range(N_CHIP):
            @pl.when((k == kk) & (m == 0))
            def _(kk=kk):
                for cp in copies(kk):
                    cp.wait()
                if kk + 1 < N_CHIP:
                    for cp in copies(kk + 1):
                        cp.start()

        g = _dot_tn(u_ref[...], dp_ref[k % 2])
        lane = lax.broadcasted_iota(jnp.int32, (tm, PAD_IN), 1)
        for kk in range(N_CHIP):
            @pl.when(k == kk)
            def _(kk=kk):
                if kk == 0:
                    nat = g
                elif kk < 3:
                    nat = pltpu.roll(g, PAD_IN - 4 * kk, 1)
                else:
                    nat = jnp.where(lane < 4, pltpu.roll(g, PAD_IN - (GW + 12), 1), pltpu.roll(g, 4, 1))
                o_ref[0] = nat[:, 0:SHARD_IN]

    return pl.pallas_call(
        body, name="grad_win", grid=(N_CHIP, D // tm),
        in_specs=[pl.BlockSpec((TP, tm), lambda k, m: (0, m))] + [ANY] * 7,
        out_specs=pl.BlockSpec((1, tm, SHARD_IN), lambda k, m: (k, m, 0)),
        out_shape=_sds((N_CHIP, D, SHARD_IN), F32),
        scratch_shapes=[pltpu.VMEM((2, TP, PAD_IN), BF16), pltpu.SemaphoreType.DMA((2, 3))],
        compiler_params=_cp(("arbitrary", "arbitrary")),
    )(u, dq, dk, dv, dr, dglr, dpu, _dep(dep))


def _split3(x):
    hi = x.astype(BF16)
    r1 = x - hi.astype(F32)
    mid = r1.astype(BF16)
    lo = (r1 - mid.astype(F32)).astype(BF16)
    return hi, mid, lo


def _tri_sum(tri, x):
    hi, mid, lo = _split3(x)
    return _dot(tri, hi) + _dot(tri, mid) + _dot(tri, lo)


def _gla_common(n, glr, gw2, gb):
    rows = n * CH + lax.broadcasted_iota(jnp.int32, (CH, 1), 0)
    valid = (rows >= ROW_LO) & (rows < ROW_HI)
    g_raw = _dot(glr.astype(BF16), gw2.astype(BF16)) + gb
    logsig = jnp.minimum(g_raw, 0.0) - jnp.log(1.0 + jnp.exp(-jnp.abs(g_raw)))
    logg = jnp.where(valid, logsig * (1.0 / TAU), 0.0)
    ci = lax.broadcasted_iota(jnp.int32, (CH, CH), 0)
    si = lax.broadcasted_iota(jnp.int32, (CH, CH), 1)
    lower = ci >= si
    G = _tri_sum(lower.astype(BF16), logg)
    Gl = G[CH - 1:CH, :]
    return valid, g_raw, lower, G, Gl


def _p_specs(index):
    def spec(width, shard, col):
        return pl.BlockSpec((1, CH, width), lambda s: (shard, index(s), col))

    return [spec(KW, 0, 0), spec(KW, 0, 1), spec(GW, 1, 0), spec(128, 0, 8), spec(GW, 2, 0), spec(128, 1, 8),
            spec(128, 2, 8), spec(128, 3, 8)]


def _p_load(q_ref, k_ref, vm_ref, vh_ref, rm_ref, rh_ref, ga_ref, gb_ref):
    def joined(main, head):
        return jnp.concatenate([main[:, 0:128] + head, main[:, 128:]], axis=1)

    return q_ref[0], k_ref[0], joined(vm_ref[0], vh_ref[0]), joined(rm_ref[0], rh_ref[0]), ga_ref[0] + gb_ref[0]


def _gla_fwd(P, gw2, gb, gnw):
    scale = DK ** -0.5

    def body(p0, p1, p2, p3, p4, p5, p6, p7, gw2_ref, gb_ref, gnw_ref, o_ref, og_ref, sp_ref, st_ref):
        n = pl.program_id(0)

        @pl.when(n == 0)
        def _():
            st_ref[...] = jnp.zeros_like(st_ref)

        q_all, k_all, v_all, r_all, glr = _p_load(p0, p1, p2, p3, p4, p5, p6, p7)
        _, _, lower, G, Gl = _gla_common(n, glr, gw2_ref[...], gb_ref[...])
        eG = jnp.exp(G)
        eN = jnp.exp(-G)
        eE = jnp.exp(Gl - G)
        dec = jnp.exp(Gl)
        gnw_v = gnw_ref[...]
        for h in range(HEADS):
            ks = slice(h * DK, (h + 1) * DK)
            vs = slice(h * DV, (h + 1) * DV)
            kh = k_all[:, ks]
            vh = v_all[:, vs].astype(BF16)
            qd = ((q_all[:, ks] * scale) * eG[:, ks]).astype(BF16)
            ki = (kh * eN[:, ks]).astype(BF16)
            ke = (kh * eE[:, ks]).astype(BF16)
            st = st_ref[h]
            a = jnp.where(lower, _dot_nt(qd, ki), 0.0).astype(BF16)
            o = _dot(a, vh) + _dot_nt(qd, st.astype(BF16))
            sp_ref[0, h] = st
            st_ref[h] = st * dec[:, ks] + _dot_tn(vh, ke)
            o_ref[:, vs] = o
            rs = lax.rsqrt(jnp.mean(o * o, axis=-1, keepdims=True) + EPS)
            rv = r_all[:, vs]
            gate = rv / (1.0 + jnp.exp(-rv))
            og_ref[:, vs] = (((o * rs) * gnw_v) * gate).astype(BF16)

    rv_ = pl.BlockSpec((CH, GW), lambda n: (n, 0))

    def full(shape):
        return pl.BlockSpec(shape, lambda n: tuple(0 for _ in shape))

    return pl.pallas_call(
        body, name="gla_fwd", grid=(NCH,),
        in_specs=_p_specs(lambda n: n) + [full((128, KW)), full((1, KW)), full((1, DV))],
        out_specs=[rv_, rv_, pl.BlockSpec((1, HEADS, DV, DK), lambda n: (n, 0, 0, 0))],
        out_shape=[_sds((TP, GW), F32), _sds((TP, GW), BF16), _sds((NCH, HEADS, DV, DK), F32)],
        scratch_shapes=[pltpu.VMEM((HEADS, DV, DK), F32)],
        compiler_params=_cp(("arbitrary",)),
    )(*([P] * 8), gw2, gb, gnw)


def _gla_bwd(dog, o, P, gw2, gb, gnw, sp, dep=None):
    scale = DK ** -0.5

    def body(dog_ref, o_ref, p0, p1, p2, p3, p4, p5, p6, p7, gw2_ref, gb_ref, gnw_ref, sp_ref, dep_ref,
             dq_ref, dk_ref, dv_ref, dr_ref, dglr_ref, dgw2_ref, dgb_ref, dgnw_ref, ds_ref):
        step = pl.program_id(0)
        n = NCH - 1 - step

        @pl.when(step == 0)
        def _():
            ds_ref[...] = jnp.zeros_like(ds_ref)
            dgw2_ref[...] = jnp.zeros_like(dgw2_ref)
            dgb_ref[...] = jnp.zeros_like(dgb_ref)
            dgnw_ref[...] = jnp.zeros_like(dgnw_ref)

        q_all, k_all, v_all, r_all, glr_v = _p_load(p0, p1, p2, p3, p4, p5, p6, p7)
        gw2_b = gw2_ref[...].astype(BF16)
        valid, g_raw, lower, G, Gl = _gla_common(n, glr_v, gw2_ref[...], gb_ref[...])
        upper = lax.broadcasted_iota(jnp.int32, (CH, CH), 0) <= lax.broadcasted_iota(jnp.int32, (CH, CH), 1)
        eG = jnp.exp(G)
        eN = jnp.exp(-G)
        eE = jnp.exp(Gl - G)
        dec = jnp.exp(Gl)
        gnw_v = gnw_ref[...]
        last = lax.broadcasted_iota(jnp.int32, (CH, 1), 0) == CH - 1
        dgnw_acc = jnp.zeros((1, DV), F32)
        dG_parts = []
        for h in range(HEADS):
            ks = slice(h * DK, (h + 1) * DK)
            vs = slice(h * DV, (h + 1) * DV)
            oh = o_ref[:, vs]
            rv = r_all[:, vs]
            dg = dog_ref[:, vs]
            sig = 1.0 / (1.0 + jnp.exp(-rv))
            gate = rv * sig
            rs = lax.rsqrt(jnp.mean(oh * oh, axis=-1, keepdims=True) + EPS)
            ohat = oh * rs
            dr_ref[:, vs] = ((dg * (ohat * gnw_v)) * (sig * (1.0 + rv * (1.0 - sig)))).astype(BF16)
            don = dg * gate
            dgnw_acc = dgnw_acc + jnp.sum(don * ohat, axis=0, keepdims=True)
            gxn = don * gnw_v
            do = (rs * (gxn - ohat * jnp.mean(gxn * ohat, axis=-1, keepdims=True))).astype(BF16)
            kh = k_all[:, ks]
            vh = v_all[:, vs].astype(BF16)
            qd_f = (q_all[:, ks] * scale) * eG[:, ks]
            ki_f = kh * eN[:, ks]
            ke_f = kh * eE[:, ks]
            qd, ki, ke = qd_f.astype(BF16), ki_f.astype(BF16), ke_f.astype(BF16)
            spt = sp_ref[0, h]
            dst = ds_ref[h]
            dst_b = dst.astype(BF16)
            a_t = jnp.where(upper, _dot_nt(ki, qd), 0.0).astype(BF16)
            da = jnp.where(lower, _dot_nt(do, vh), 0.0).astype(BF16)
            da_t = jnp.where(upper, _dot_nt(vh, do), 0.0).astype(BF16)
            dv_ref[:, vs] = (_dot(a_t, do) + _dot_nt(ke, dst_b)).astype(BF16)
            dqd = _dot(da, ki) + _dot(do, spt.astype(BF16))
            dki = _dot(da_t, qd)
            dke = _dot(vh, dst_b)
            ddec = jnp.sum(spt * dst, axis=0, keepdims=True)
            ds_ref[h] = dst * dec[:, ks] + _dot_tn(do, qd)
            dq_ref[:, ks] = ((dqd * eG[:, ks]) * scale).astype(BF16)
            dk_ref[:, ks] = (dki * eN[:, ks] + dke * eE[:, ks]).astype(BF16)
            dke_ke = dke * ke_f
            dG = dqd * qd_f - dki * ki_f - dke_ke
            dGl = jnp.sum(dke_ke, axis=0, keepdims=True) + ddec * dec[:, ks]
            dG_parts.append(dG + jnp.where(last, dGl, 0.0))
        dgnw_ref[...] += dgnw_acc
        dG_all = jnp.concatenate(dG_parts, axis=1)
        dlogg = jnp.where(valid, _tri_sum(upper.astype(BF16), dG_all), 0.0)
        dg_raw = (dlogg * (1.0 / TAU)) * (1.0 / (1.0 + jnp.exp(g_raw)))
        dgb_ref[...] += jnp.sum(dg_raw, axis=0, keepdims=True)
        dg_b = dg_raw.astype(BF16)
        dgw2_ref[...] += _dot_tn(glr_v.astype(BF16), dg_b)
        dglr_ref[...] = _dot_nt(dg_b, gw2_b).astype(BF16)

    def back(s):
        return NCH - 1 - s

    rk = pl.BlockSpec((CH, KW), lambda s: (back(s), 0))
    rv_ = pl.BlockSpec((CH, GW), lambda s: (back(s), 0))
    rg = pl.BlockSpec((CH, 128), lambda s: (back(s), 0))

    def full(shape):
        return pl.BlockSpec(shape, lambda s: tuple(0 for _ in shape))

    return pl.pallas_call(
        body, name="gla_bwd", grid=(NCH,),
        in_specs=[rv_, rv_] + _p_specs(back) + [full((128, KW)), full((1, KW)), full((1, DV)),
                  pl.BlockSpec((1, HEADS, DV, DK), lambda s: (back(s), 0, 0, 0)), ANY],
        out_specs=[rk, rk, rv_, rv_, rg, full((128, KW)), full((1, KW)), full((1, DV))],
        out_shape=[_sds((TP, KW), BF16), _sds((TP, KW), BF16), _sds((TP, GW), BF16), _sds((TP, GW), BF16),
                   _sds((TP, 128), BF16), _sds((128, KW), F32), _sds((1, KW), F32), _sds((1, DV), F32)],
        scratch_shapes=[pltpu.VMEM((HEADS, DV, DK), F32)],
        compiler_params=_cp(("arbitrary",)),
    )(dog, o, *([P] * 8), gw2, gb, gnw, sp, _dep(dep))


POOL_TR = 128
HALO = 16


def _pool_counts(base, nrows):
    rows = base + lax.broadcasted_iota(jnp.int32, (nrows, 1), 0)
    valid = (rows >= ROW_LO) & (rows < ROW_HI)
    t1 = (rows - ROW_LO + 1).astype(F32)
    cnts = [jnp.clip(t1, 1.0, float(w)) for w in WINDOWS]
    return valid, cnts


def _pool_fwd(P, pw, ps):
    def body(cur_ref, prev_ref, pw_ref, ps_ref, y_ref, op_ref):
        i = pl.program_id(0)
        cur = cur_ref[0]
        full = jnp.concatenate([prev_ref[0], cur], axis=0)
        s2 = full + pltpu.roll(full, 1, 0)
        s4 = s2 + pltpu.roll(s2, 2, 0)
        s8 = s4 + pltpu.roll(s4, 4, 0)
        s16 = s8 + pltpu.roll(s8, 8, 0)
        valid, cnts = _pool_counts(i * POOL_TR, POOL_TR)
        for g, s in enumerate((s2, s4, s8, s16)):
            cs = slice(g * GC, (g + 1) * GC)
            y = s[HALO:, cs] / cnts[g] - cur[:, cs]
            yb = jnp.where(valid, y, 0.0).astype(BF16)
            y_ref[:, cs] = yb
            op_ref[:, cs] = (_dot(yb, pw_ref[g].astype(BF16)) * ps_ref[:, cs]).astype(BF16)

    row = pl.BlockSpec((POOL_TR, PW), lambda i: (i, 0))
    per = POOL_TR // HALO
    return pl.pallas_call(
        body, name="pool_fwd", grid=(TP // POOL_TR,),
        in_specs=[pl.BlockSpec((1, POOL_TR, PW), lambda i: (3, i, 0)),
                  pl.BlockSpec((1, HALO, PW), lambda i: (3, jnp.maximum(i * per - 1, 0), 0)),
                  pl.BlockSpec((4, GC, GC), lambda i: (0, 0, 0)), pl.BlockSpec((1, PW), lambda i: (0, 0))],
        out_specs=[row, row],
        out_shape=[_sds((TP, PW), BF16), _sds((TP, PW), BF16)],
        compiler_params=_cp(("arbitrary",)),
    )(P, P, pw, ps)


def _pool_bwd(dop, y, pw, ps, dep=None):
    nblk = TP // HALO

    def body(cur_ref, nxt_ref, y_ref, pw_ref, ps_ref, dep_ref, dpu_ref, dpw_ref, dps_ref):
        i = pl.program_id(0)

        @pl.when(i == 0)
        def _():
            dpw_ref[...] = jnp.zeros_like(dpw_ref)
            dps_ref[...] = jnp.zeros_like(dps_ref)

        n_all = POOL_TR + HALO
        dcur = cur_ref[...]
        dall = jnp.concatenate([dcur, nxt_ref[...]], axis=0)
        valid, cnts = _pool_counts(i * POOL_TR, n_all)
        for g in range(4):
            cs = slice(g * GC, (g + 1) * GC)
            pwb = pw_ref[g].astype(BF16)
            yb = y_ref[:, cs]
            dyw = (dall[:, cs] * ps_ref[:, cs]).astype(BF16)
            dps_ref[:, cs] += jnp.sum(dcur[:, cs] * _dot(yb, pwb), axis=0, keepdims=True)
            dpw_ref[g] += _dot_tn(yb, dyw[0:POOL_TR, :])
            dyv = jnp.where(valid, _dot_nt(dyw, pwb), 0.0)
            e = dyv / cnts[g]
            w = WINDOWS[g]
            sh = 1
            while sh < w:
                e = e + pltpu.roll(e, n_all - sh, 0)
                sh *= 2
            dpu_ref[:, cs] = (e[0:POOL_TR, :] - dyv[0:POOL_TR, :]).astype(BF16)

    row = pl.BlockSpec((POOL_TR, PW), lambda i: (i, 0))
    per = POOL_TR // HALO
    return pl.pallas_call(
        body, name="pool_bwd", grid=(TP // POOL_TR,),
        in_specs=[pl.BlockSpec((POOL_TR, PW), lambda i: (i, 1)),
                  pl.BlockSpec((HALO, PW), lambda i: (jnp.minimum(i * per + per, nblk - 1), 1)),
                  row, pl.BlockSpec((4, GC, GC), lambda i: (0, 0, 0)), pl.BlockSpec((1, PW), lambda i: (0, 0)), ANY],
        out_specs=[row, pl.BlockSpec((4, GC, GC), lambda i: (0, 0, 0)), pl.BlockSpec((1, PW), lambda i: (0, 0))],
        out_shape=[_sds((TP, PW), BF16), _sds((4, GC, GC), F32), _sds((1, PW), F32)],
        compiler_params=_cp(("arbitrary",)),
    )(dop, dop, y, pw, ps, _dep(dep))


def _place():
    x, y, c = lax.axis_index("x"), lax.axis_index("y"), lax.axis_index("c")
    chips = [(1 - x, y), (x, 1 - y), (1 - x, 1 - y)]
    return x, y, c, chips


HBM = pl.BlockSpec(memory_space=pltpu.HBM)
SEM = pl.BlockSpec(memory_space=pltpu.SEMAPHORE)
EFFECT = pltpu.SideEffectType.DATAFLOW_SIDE_EFFECTING


def _cast_into(w, place, cols_out, name, dep=None):
    rows, cols = w.shape
    tr = 256

    def body(p_ref, w_ref, dep_ref, o_ref):
        if cols_out != cols:
            o_ref[0] = jnp.zeros((tr, cols_out), BF16)
            o_ref[0, :, 0:cols] = w_ref[...].astype(BF16)
        else:
            o_ref[0] = w_ref[...].astype(BF16)

    grid_spec = pltpu.PrefetchScalarGridSpec(
        num_scalar_prefetch=1, grid=(rows // tr,),
        in_specs=[pl.BlockSpec((tr, cols), lambda i, p: (i, 0)), ANY],
        out_specs=pl.BlockSpec((1, tr, cols_out), lambda i, p: (p[0], i, 0)))
    return pl.pallas_call(
        body, name=name, grid_spec=grid_spec,
        out_shape=_sds((N_CHIP, rows, cols_out), BF16),
        compiler_params=_cp(("arbitrary",)),
    )(place, w, _dep(dep))


def _cast_win(w, place, dep=None):
    rows, cols = w.shape
    tr = 256

    def body(p_ref, w_ref, dep_ref, o_ref, t_ref):
        t_ref[...] = jnp.zeros_like(t_ref)
        t_ref[:, 0:cols] = w_ref[...]
        t = t_ref[...]
        lane = lax.broadcasted_iota(jnp.int32, (tr, PAD_IN), 1)
        for kk in range(N_CHIP):
            @pl.when(p_ref[0] == kk)
            def _(kk=kk):
                if kk == 0:
                    placed = t
                elif kk < 3:
                    placed = pltpu.roll(t, 4 * kk, 1)
                else:
                    pool = pltpu.roll(t, PAD_IN - 4, 1)
                    gate = pltpu.roll(t, GW + 12, 1)
                    placed = jnp.where(lane < GW, pool, jnp.where((lane >= GW + 12) & (lane < GW + 16), gate, 0.0))
                o_ref[0] = placed.astype(BF16)

    grid_spec = pltpu.PrefetchScalarGridSpec(
        num_scalar_prefetch=1, grid=(rows // tr,),
        in_specs=[pl.BlockSpec((tr, cols), lambda i, p: (i, 0)), ANY],
        out_specs=pl.BlockSpec((1, tr, PAD_IN), lambda i, p: (p[0], i, 0)),
        scratch_shapes=[pltpu.VMEM((tr, PAD_IN), F32)])
    return pl.pallas_call(
        body, name="cast_win", grid_spec=grid_spec,
        out_shape=_sds((N_CHIP, rows, PAD_IN), BF16),
        compiler_params=_cp(("arbitrary",)),
    )(place, w, _dep(dep))


def _half_rows(ref, k, which):
    h = ref.shape[1] // 2
    return ref.at[k, pl.ds(pl.multiple_of(which * h, 8), h), :]


def _sent_rows(ref, k, which, whole):
    return ref.at[k] if whole else _half_rows(ref, k, which)


def _gather_start(ws, name, whole=None):
    n = len(ws)
    whole = whole or [False] * n

    def body(*refs):
        ins = refs[:n]
        ssems = refs[n:2 * n]
        rsems = refs[2 * n:3 * n]
        token = refs[4 * n]
        x, y, c, chips = _place()
        me = 2 * x + y
        for w in range(n):
            blk = _sent_rows(ins[w], me, c, whole[w])
            for j, chip in enumerate(chips):
                pltpu.make_async_remote_copy(src_ref=blk, dst_ref=blk, send_sem=ssems[w].at[j], recv_sem=rsems[w].at[j],
                                             device_id=(*chip, c), device_id_type=MESH).start()
        token[...] = jnp.zeros_like(token)

    sem3 = pltpu.SemaphoreType.DMA((3,))
    outs = pl.pallas_call(
        body, name=name,
        out_shape=tuple([sem3] * (2 * n) + [pltpu.HBM(w.shape, w.dtype) for w in ws] + [_sds((8, 128), F32)]),
        in_specs=(HBM,) * n, out_specs=(SEM,) * (2 * n) + (HBM,) * n + (VMEM_FULL,),
        input_output_aliases={w: 2 * n + w for w in range(n)},
        compiler_params=pltpu.CompilerParams(has_side_effects=EFFECT),
    )(*[pltpu.with_memory_space_constraint(w, pltpu.HBM) for w in ws])
    return outs[:n], outs[n:2 * n], outs[2 * n:3 * n], outs[3 * n]


def _gather_wait(w, ssem, rsem, after, name, whole=False):
    def body(w_ref, ssem_ref, rsem_ref, after_ref, out_ref):
        x, y, c, chips = _place()
        me = 2 * x + y
        mine = _sent_rows(w_ref, me, c, whole)
        for j, (cx, cy) in enumerate(chips):
            cp = pltpu.make_async_remote_copy(src_ref=mine, dst_ref=_sent_rows(w_ref, 2 * cx + cy, c, whole),
                                              send_sem=ssem_ref.at[j], recv_sem=rsem_ref.at[j],
                                              device_id=(cx, cy, c), device_id_type=MESH)
            cp.wait_send()
            cp.wait_recv()

    return pl.pallas_call(
        body, name=name, out_shape=pltpu.HBM(w.shape, w.dtype),
        in_specs=(HBM, SEM, SEM, ANY), out_specs=HBM, input_output_aliases={0: 0},
        compiler_params=pltpu.CompilerParams(has_side_effects=EFFECT),
    )(w, ssem, rsem, after)


def _forward_halves(w, name):
    def body(w_ref, o_ref, ssem, rsem):
        x, y, c, chips = _place()
        sib = (x, y, 1 - c)
        cps = []
        for j, (cx, cy) in enumerate(chips):
            blk = _half_rows(o_ref, 2 * cx + cy, c)
            cps.append(pltpu.make_async_remote_copy(src_ref=blk, dst_ref=blk, send_sem=ssem.at[j], recv_sem=rsem.at[j],
                                                    device_id=sib, device_id_type=MESH))
        for cp in cps:
            cp.start()
        for j, (cx, cy) in enumerate(chips):
            blk = _half_rows(o_ref, 2 * cx + cy, 1 - c)
            pltpu.make_async_remote_copy(src_ref=blk, dst_ref=blk, send_sem=ssem.at[j], recv_sem=rsem.at[j],
                                         device_id=sib, device_id_type=MESH).wait_recv()
        for cp in cps:
            cp.wait_send()

    return pl.pallas_call(
        body, name=name, in_specs=[ANY], out_specs=ANY, out_shape=_sds(w.shape, w.dtype),
        input_output_aliases={0: 0},
        scratch_shapes=[pltpu.SemaphoreType.DMA((3,)), pltpu.SemaphoreType.DMA((3,))],
    )(w)


def _rs_start(sb, name, after=None):
    _, half, cols = sb.shape

    def body(sb_ref, land_ref, after_ref, ssem, rsem, sb_out, land_out, token):
        x, y, c, chips = _place()
        for j, (cx, cy) in enumerate(chips):
            pltpu.make_async_remote_copy(src_ref=sb_ref.at[2 * cx + cy], dst_ref=land_ref.at[j], send_sem=ssem.at[j],
                                         recv_sem=rsem.at[j], device_id=(cx, cy, c), device_id_type=MESH).start()
        token[...] = jnp.zeros_like(token)

    sem3 = pltpu.SemaphoreType.DMA((3,))
    land = lax.empty((3, half, cols), BF16)
    return pl.pallas_call(
        body, name=name,
        out_shape=(sem3, sem3, pltpu.HBM(sb.shape, sb.dtype), pltpu.HBM(land.shape, land.dtype), _sds((8, 128), F32)),
        in_specs=(HBM, HBM, ANY), out_specs=(SEM, SEM, HBM, HBM, VMEM_FULL), input_output_aliases={0: 2, 1: 3},
        compiler_params=pltpu.CompilerParams(has_side_effects=EFFECT),
    )(pltpu.with_memory_space_constraint(sb, pltpu.HBM), pltpu.with_memory_space_constraint(land, pltpu.HBM), _dep(after))


def _rs_wait(sb, land, ssem, rsem, after, name):
    def body(sb_ref, land_ref, ssem_ref, rsem_ref, after_ref, sb_out, land_out):
        x, y, c, chips = _place()
        for j, (cx, cy) in enumerate(chips):
            cp = pltpu.make_async_remote_copy(src_ref=sb_ref.at[2 * cx + cy], dst_ref=land_ref.at[j], send_sem=ssem_ref.at[j],
                                              recv_sem=rsem_ref.at[j], device_id=(cx, cy, c), device_id_type=MESH)
            cp.wait_send()
            cp.wait_recv()

    return pl.pallas_call(
        body, name=name,
        out_shape=(pltpu.HBM(sb.shape, sb.dtype), pltpu.HBM(land.shape, land.dtype)),
        in_specs=(HBM, HBM, SEM, SEM, ANY), out_specs=(HBM, HBM), input_output_aliases={0: 0, 1: 1},
        compiler_params=pltpu.CompilerParams(has_side_effects=EFFECT),
    )(sb, land, ssem, rsem, after)[1]


def _pair_copy(g_ref, land_ref, ssem, rsem):
    x, y, c, _ = _place()
    h = g_ref.shape[1] // 2
    src = g_ref.at[:, pl.ds(pl.multiple_of((1 - c) * h, 8), h), :]
    return pltpu.make_async_remote_copy(src_ref=src, dst_ref=land_ref, send_sem=ssem.at[0], recv_sem=rsem.at[0],
                                        device_id=(x, y, 1 - c), device_id_type=MESH)


def _pair_start(g, name):
    def body(g_ref, land_ref, ssem, rsem, g_out, land_out, token):
        _pair_copy(g_ref, land_ref, ssem, rsem).start()
        token[...] = jnp.zeros_like(token)

    sem1 = pltpu.SemaphoreType.DMA((1,))
    land = lax.empty((N_CHIP, g.shape[1] // 2, g.shape[2]), F32)
    return pl.pallas_call(
        body, name=name,
        out_shape=(sem1, sem1, pltpu.HBM(g.shape, g.dtype), pltpu.HBM(land.shape, land.dtype), _sds((8, 128), F32)),
        in_specs=(HBM, HBM), out_specs=(SEM, SEM, HBM, HBM, VMEM_FULL), input_output_aliases={0: 2, 1: 3},
        compiler_params=pltpu.CompilerParams(has_side_effects=EFFECT),
    )(pltpu.with_memory_space_constraint(g, pltpu.HBM), pltpu.with_memory_space_constraint(land, pltpu.HBM))


def _pair_wait(g, land, ssem, rsem, after, name):
    def body(g_ref, land_ref, ssem_ref, rsem_ref, after_ref, g_out, land_out):
        cp = _pair_copy(g_ref, land_ref, ssem_ref, rsem_ref)
        cp.wait_send()
        cp.wait_recv()

    return pl.pallas_call(
        body, name=name,
        out_shape=(pltpu.HBM(g.shape, g.dtype), pltpu.HBM(land.shape, land.dtype)),
        in_specs=(HBM, HBM, SEM, SEM, ANY), out_specs=(HBM, HBM), input_output_aliases={0: 0, 1: 1},
        compiler_params=pltpu.CompilerParams(has_side_effects=EFFECT),
    )(g, land, ssem, rsem, after)


def _pair_sum(g, rcv, place, name):
    _, rows, cols = g.shape
    half = rows // 2
    tr = 256
    nt = half // tr

    def body(p_ref, g_ref, r_ref, sb_ref, sf_ref):
        s = pl.program_id(1)
        tot = g_ref[0] + r_ref[0]
        sb_ref[0] = tot.astype(BF16)

        @pl.when(s == p_ref[0])
        def _():
            sf_ref[...] = tot

    grid_spec = pltpu.PrefetchScalarGridSpec(
        num_scalar_prefetch=1, grid=(nt, N_CHIP),
        in_specs=[pl.BlockSpec((1, tr, cols), lambda t, s, p: (s, p[1] * nt + t, 0)),
                  pl.BlockSpec((1, tr, cols), lambda t, s, p: (s, t, 0))],
        out_specs=[pl.BlockSpec((1, tr, cols), lambda t, s, p: (s, t, 0)),
                   pl.BlockSpec((tr, cols), lambda t, s, p: (t, 0))])
    return pl.pallas_call(
        body, name=name, grid_spec=grid_spec,
        out_shape=[_sds((N_CHIP, half, cols), BF16), _sds((half, cols), F32)],
        compiler_params=_cp(("arbitrary", "arbitrary")),
    )(place, g, rcv)


def _final_sum(sf, rb, place, name):
    half, cols = sf.shape
    tr = 256
    nt = half // tr

    def body(p_ref, sf_ref, r_ref, out_ref):
        acc = sf_ref[...]
        for j in range(3):
            acc = acc + r_ref[j].astype(F32)
        out_ref[...] = acc

    grid_spec = pltpu.PrefetchScalarGridSpec(
        num_scalar_prefetch=1, grid=(nt,),
        in_specs=[pl.BlockSpec((tr, cols), lambda t, p: (t, 0)), pl.BlockSpec((3, tr, cols), lambda t, p: (0, t, 0))],
        out_specs=pl.BlockSpec((tr, cols), lambda t, p: (p[1] * nt + t, 0)))
    return pl.pallas_call(
        body, name=name, grid_spec=grid_spec,
        out_shape=_sds((2 * half, cols), F32),
        compiler_params=_cp(("arbitrary",)),
    )(place, sf, rb)


def _half_copy(f_ref, which, ssem, rsem):
    x, y, c, _ = _place()
    h = f_ref.shape[0] // 2
    rows = f_ref.at[pl.ds(pl.multiple_of(which * h, 8), h), :]
    return pltpu.make_async_remote_copy(src_ref=rows, dst_ref=rows, send_sem=ssem.at[0], recv_sem=rsem.at[0],
                                        device_id=(x, y, 1 - c), device_id_type=MESH)


def _half_start(full, name, after=None):
    def body(f_ref, after_ref, ssem, rsem, f_out, token):
        _half_copy(f_ref, lax.axis_index("c"), ssem, rsem).start()
        token[...] = jnp.zeros_like(token)

    sem1 = pltpu.SemaphoreType.DMA((1,))
    return pl.pallas_call(
        body, name=name,
        out_shape=(sem1, sem1, pltpu.HBM(full.shape, full.dtype), _sds((8, 128), F32)),
        in_specs=(HBM, ANY), out_specs=(SEM, SEM, HBM, VMEM_FULL), input_output_aliases={0: 2},
        compiler_params=pltpu.CompilerParams(has_side_effects=EFFECT),
    )(pltpu.with_memory_space_constraint(full, pltpu.HBM), _dep(after))


def _half_wait(full, ssem, rsem, after, name):
    def body(f_ref, ssem_ref, rsem_ref, after_ref, f_out):
        c = lax.axis_index("c")
        _half_copy(f_ref, c, ssem_ref, rsem_ref).wait_send()
        _half_copy(f_ref, 1 - c, ssem_ref, rsem_ref).wait_recv()

    return pl.pallas_call(
        body, name=name, out_shape=pltpu.HBM(full.shape, full.dtype),
        in_specs=(HBM, SEM, SEM, ANY), out_specs=HBM, input_output_aliases={0: 0},
        compiler_params=pltpu.CompilerParams(has_side_effects=EFFECT),
    )(full, ssem, rsem, after)


def _small_copies(src_ref, land_ref, ssem, rsem, first):
    x, y, c, chips = _place()
    if first:
        return [pltpu.make_async_remote_copy(src_ref=src_ref, dst_ref=land_ref, send_sem=ssem.at[0], recv_sem=rsem.at[0],
                                             device_id=(x, y, 1 - c), device_id_type=MESH)]
    return [pltpu.make_async_remote_copy(src_ref=src_ref, dst_ref=land_ref.at[j], send_sem=ssem.at[j], recv_sem=rsem.at[j],
                                         device_id=(*chip, c), device_id_type=MESH) for j, chip in enumerate(chips)]


def _small_start(src, first, name, after=None):
    n = 1 if first else 3

    def body(src_ref, land_ref, after_ref, ssem, rsem, src_out, land_out, token):
        for cp in _small_copies(src_ref, land_ref, ssem, rsem, first):
            cp.start()
        token[...] = jnp.zeros_like(token)

    sems = pltpu.SemaphoreType.DMA((n,))
    land = lax.empty(src.shape if first else (3,) + src.shape, F32)
    return pl.pallas_call(
        body, name=name,
        out_shape=(sems, sems, pltpu.HBM(src.shape, F32), pltpu.HBM(land.shape, F32), _sds((8, 128), F32)),
        in_specs=(HBM, HBM, ANY), out_specs=(SEM, SEM, HBM, HBM, VMEM_FULL), input_output_aliases={0: 2, 1: 3},
        compiler_params=pltpu.CompilerParams(has_side_effects=EFFECT),
    )(pltpu.with_memory_space_constraint(src, pltpu.HBM), pltpu.with_memory_space_constraint(land, pltpu.HBM), _dep(after))


def _small_wait(src, land, ssem, rsem, first, after, name):
    def body(src_ref, land_ref, ssem_ref, rsem_ref, after_ref, src_out, land_out):
        for cp in _small_copies(src_ref, land_ref, ssem_ref, rsem_ref, first):
            cp.wait_send()
            cp.wait_recv()

    return pl.pallas_call(
        body, name=name,
        out_shape=(pltpu.HBM(src.shape, F32), pltpu.HBM(land.shape, F32)),
        in_specs=(HBM, HBM, SEM, SEM, ANY), out_specs=(HBM, HBM), input_output_aliases={0: 0, 1: 1},
        compiler_params=pltpu.CompilerParams(has_side_effects=EFFECT),
    )(src, land, ssem, rsem, after)


def _small_pair_sum(vec, got):
    def body(v_ref, g_ref, o_ref):
        o_ref[...] = v_ref[...] + g_ref[...]

    return pl.pallas_call(body, name="small_pair_sum", in_specs=[VMEM_FULL] * 2, out_specs=VMEM_FULL,
                          out_shape=_sds(vec.shape, F32), compiler_params=_cp())(vec, got)


def _small_chip_sum(pair, got, place):
    def body(p_ref, pair_ref, got_ref, o_ref):
        acc = None
        for kk in range(N_CHIP):
            d = jnp.bitwise_xor(p_ref[0], kk)
            t = jnp.where(d == 0, pair_ref[...], jnp.where(d == 2, got_ref[0], jnp.where(d == 1, got_ref[1], got_ref[2])))
            acc = t if acc is None else acc + t
        o_ref[...] = acc

    grid_spec = pltpu.PrefetchScalarGridSpec(
        num_scalar_prefetch=1, grid=(1,),
        in_specs=[pl.BlockSpec(pair.shape, lambda i, p: (0, 0)), pl.BlockSpec(got.shape, lambda i, p: (0, 0, 0))],
        out_specs=pl.BlockSpec(pair.shape, lambda i, p: (0, 0)))
    return pl.pallas_call(body, name="small_chip_sum", grid_spec=grid_spec, out_shape=_sds(pair.shape, F32),
                          compiler_params=_cp(("arbitrary",)))(place, pair, got)


def _adam_math(w, g, m, v):
    m = B1 * m + (1.0 - B1) * g
    v = B2 * v + (1.0 - B2) * (g * g)
    m_hat = m / (1.0 - B1 ** STEP)
    v_hat = v / (1.0 - B2 ** STEP)
    delta = -LR * (m_hat / (jnp.sqrt(v_hat) + AEPS) + WD * w)
    return delta, m, v


def _adam_big(w, g, m, v, name):
    rows, cols = w.shape
    tr = 128

    def body(w_ref, g_ref, m_ref, v_ref, go_ref, d_ref, nm_ref, nv_ref):
        g = g_ref[...]
        d, nm, nv = _adam_math(w_ref[...], g, m_ref[...], v_ref[...])
        go_ref[...] = g
        d_ref[...] = d
        nm_ref[...] = nm
        nv_ref[...] = nv

    blk = pl.BlockSpec((tr, cols), lambda i: (i, 0))
    return pl.pallas_call(
        body, name=name, grid=(rows // tr,),
        in_specs=[blk] * 4, out_specs=[blk] * 4, out_shape=[_sds((rows, cols), F32)] * 4,
        compiler_params=_cp(("arbitrary",)),
    )(w, g, m, v)


def _adam_small(ws, gs, ms, vs, dep=None):
    n = len(ws)

    def body(*refs):
        for i in range(n):
            d, nm, nv = _adam_math(refs[i][...], refs[n + i][...], refs[2 * n + i][...], refs[3 * n + i][...])
            refs[4 * n + 1 + i][...] = d
            refs[5 * n + 1 + i][...] = nm
            refs[6 * n + 1 + i][...] = nv

    shapes = [_sds(w.shape, F32) for w in ws]
    outs = pl.pallas_call(
        body, name="adam_small",
        in_specs=[VMEM_FULL] * (4 * n) + [ANY], out_specs=[VMEM_FULL] * (3 * n), out_shape=shapes * 3,
        compiler_params=_cp(),
    )(*ws, *gs, *ms, *vs, _dep(dep))
    return outs[:n], outs[n:2 * n], outs[2 * n:]


def _pad_rows8(a):
    flat = a.reshape(-1, 128)
    pad = (-flat.shape[0]) % 8
    if pad:
        flat = jnp.concatenate([flat, jnp.zeros((pad, 128), F32)], axis=0)
    return flat


def kernel(x, meta_tokens, norm1_w, w_in, gate_w2, gate_b, gla_norm_w, pool_w, pool_scale, w_out, norm2_w, mlp_w1, mlp_w2, final_norm_w, loss_target, m_meta_tokens, m_norm1_w, m_w_in, m_gate_w2, m_gate_b, m_gla_norm_w, m_pool_w, m_pool_scale, m_w_out, m_norm2_w, m_mlp_w1, m_mlp_w2, m_final_norm_w, v_meta_tokens, v_norm1_w, v_w_in, v_gate_w2, v_gate_b, v_gla_norm_w, v_pool_w, v_pool_scale, v_w_out, v_norm2_w, v_mlp_w1, v_mlp_w2, v_final_norm_w):
    cx, cy, cc = lax.axis_index("x"), lax.axis_index("y"), lax.axis_index("c")
    me = (2 * cx + cy).astype(jnp.int32)

    place = jnp.stack([me, cc.astype(jnp.int32)])
    fw = final_norm_w.reshape(1, D)

    mine = jnp.concatenate([meta_tokens.reshape(64, 128), gate_w2[0], pool_w[0].reshape(512, 128)], axis=0)
    small = lax.dynamic_update_slice(jnp.zeros((N_CHIP, 592, 128), F32), mine[None], (me, 0, 0))
    (s_sm, s_win), (r_sm, r_win), (f_sm, f_win), tok = _gather_start(
        [small, _cast_win(w_in[0], place)], "gather_start_win", [True, False])
    rest = [_cast_into(w_out[0], place, D, "cast_wout", tok), _cast_into(mlp_w1[0], place, D, "cast_w1", tok),
            _cast_into(mlp_w2[0], place, D, "cast_w2", tok)]
    ssems, rsems, flying, tok = _gather_start(rest, "gather_start_rest")
    ssems, rsems, flying = [s_win, *ssems], [r_win, *rsems], [f_win, *flying]
    small = _gather_wait(f_sm, s_sm, r_sm, tok, "gather_wait_small", True)
    metaF = jnp.concatenate([small[k, 0:64].reshape(N_META, 512) for k in range(N_CHIP)], axis=1)
    gw2F = jnp.concatenate([small[k, 64:80] for k in range(N_CHIP)], axis=1)
    pwF = jnp.concatenate([small[k, 80:592].reshape(4, 64, GC) for k in range(N_CHIP)], axis=1)

    def arrive(i, nm, after):
        return _forward_halves(_gather_wait(flying[i], ssems[i], rsems[i], after, "gather_wait_" + nm), "forward_" + nm)

    pairs, pending = {}, {}

    def grad_start(nm, g):
        ssem, rsem, g_thru, land, token = _pair_start(g, "pair_start_" + nm)
        pairs[nm] = (ssem, rsem, g_thru, land)
        return token

    def grad_finish(nm, after):
        ssem, rsem, g_thru, land = pairs[nm]
        g, rcv = _pair_wait(g_thru, land, ssem, rsem, after, "pair_wait_" + nm)
        sb, sf = _pair_sum(g, rcv, place, "pair_sum_" + nm)
        if nm == "win":
            pending[nm] = (sf, sb)
            return sf
        ssem, rsem, sb_thru, land, token = _rs_start(sb, "rs_start_" + nm)
        pending[nm] = (sf, ssem, rsem, sb_thru, land)
        return token

    (grad_x, loss8, d_n1w, d_gb, d_gnw, d_ps, d_n2w, d_fw, d_meta, d_gw2, d_pw) = _local_step(
        x[0], loss_target[0], lambda after: arrive(0, "win", after), lambda after: arrive(1, "wout", after).reshape(D, D),
        lambda after: arrive(2, "w1", after), lambda after: arrive(3, "w2", after).reshape(DFF, D), metaF, gw2F, pwF,
        norm1_w, gate_b, gla_norm_w, pool_scale, norm2_w, fw, grad_start, grad_finish, tok, (m_w_in[0], v_w_in[0]))
    return _reduce_and_update(
        me, place, pending, grad_x, loss8, d_n1w, d_gb, d_gnw, d_ps, d_n2w, d_fw, d_meta, d_gw2, d_pw,
        meta_tokens, norm1_w, w_in, gate_w2, gate_b, gla_norm_w, pool_w, pool_scale, w_out, norm2_w, mlp_w1, mlp_w2, fw,
        m_meta_tokens, m_norm1_w, m_w_in, m_gate_w2, m_gate_b, m_gla_norm_w, m_pool_w, m_pool_scale, m_w_out, m_norm2_w,
        m_mlp_w1, m_mlp_w2, m_final_norm_w, v_meta_tokens, v_norm1_w, v_w_in, v_gate_w2, v_gate_b, v_gla_norm_w, v_pool_w,
        v_pool_scale, v_w_out, v_norm2_w, v_mlp_w1, v_mlp_w2, v_final_norm_w)


def _local_step(x, target, get_win, get_wout, get_w1, get_w2, metaF, gw2F, pwF, norm1_w, gate_b, gla_norm_w, pool_scale,
                norm2_w, fw, grad_start, grad_finish, first=None, early=()):
    h0, u = _embed_norm(x, metaF, norm1_w, first)
    Win = get_win(u)
    P = _in_proj(u, Win)
    gw2p = jnp.pad(gw2F, ((0, 128 - RANK), (0, 0)))
    o, og, sp = _gla_fwd(P, gw2p, gate_b, gla_norm_w)
    yb, op = _pool_fwd(P, pwF, pool_scale)
    Wout = get_wout(op)
    h1 = _out_proj(og, op, Wout, h0)
    n2 = _norm_rows(h1, norm2_w, "norm2")
    W1 = get_w1(n2)
    zr, a = _mlp_up(n2, W1, early)
    W2 = get_w2(a)
    h2 = _mlp_down(a, W2, h1)

    dh2, dh2b, d_fw, loss8 = _loss_head(h2, target, fw)
    tok = grad_start("w2", _grad_w2(a, dh2b).reshape(N_CHIP, D, D))
    dz = _mlp_dz(dh2b, W2, zr, tok)
    tok = grad_finish("w2", dz)
    tok = grad_start("w1", _grad_w1(n2, dz, tok))
    dn2 = _mlp_dn(dz, W1, tok)
    tok = grad_finish("w1", dn2)
    dh1, dh1b, d_n2w = _norm_bwd(dn2, h1, dh2, norm2_w, "norm2_bwd", tok)
    dmixed = _mixed_grad(dh1b, Wout)
    tok = grad_start("wout", _grad_wout(og, op, dh1b))
    dpu, d_pw, d_ps = _pool_bwd(dmixed, yb, pwF, pool_scale, tok)
    dq, dk, dv, dr, dglr, d_gw2p, d_gb, d_gnw = _gla_bwd(dmixed, o, P, gw2p, gate_b, gla_norm_w, sp, tok)
    d_gw2 = d_gw2p[0:RANK]
    tok = grad_finish("wout", dq)
    tok = grad_start("win", _grad_win(u, dq, dk, dv, dr, dglr, dpu, tok))
    du = _in_grad(dq, dk, dv, dr, dglr, dpu, Win, tok)
    tok = grad_finish("win", du)
    grad_x, d_meta, d_n1w = _input_grad(du, h0, dh1, norm1_w, tok)
    return grad_x, loss8, d_n1w, d_gb, d_gnw, d_ps, d_n2w, d_fw, d_meta, d_gw2, d_pw


def _reduce_and_update(me, place, pending, grad_x, loss8, d_n1w, d_gb, d_gnw, d_ps, d_n2w, d_fw, d_meta, d_gw2, d_pw,
                       meta_tokens, norm1_w, w_in, gate_w2, gate_b, gla_norm_w, pool_w, pool_scale, w_out, norm2_w,
                       mlp_w1, mlp_w2, fw, m_meta_tokens, m_norm1_w, m_w_in, m_gate_w2, m_gate_b, m_gla_norm_w, m_pool_w,
                       m_pool_scale, m_w_out, m_norm2_w, m_mlp_w1, m_mlp_w2, m_final_norm_w, v_meta_tokens, v_norm1_w, v_w_in,
                       v_gate_w2, v_gate_b, v_gla_norm_w, v_pool_w, v_pool_scale, v_w_out, v_norm2_w, v_mlp_w1, v_mlp_w2,
                       v_final_norm_w):
    parts = [loss8, d_n1w, d_gb, d_gnw, d_ps, d_n2w, d_fw, d_meta, d_gw2, d_pw]
    packed = [_pad_rows8(p) for p in parts]
    sizes = [p.shape[0] for p in packed]
    vec = jnp.concatenate(packed, axis=0)

    big, halves = {}, {}
    params = {"w2": (mlp_w2, m_mlp_w2, v_mlp_w2), "w1": (mlp_w1, m_mlp_w1, v_mlp_w1), "wout": (w_out, m_w_out, v_w_out),
              "win": (w_in, m_w_in, v_w_in)}

    def reduce_(nm, after):
        sf, ssem, rsem, sb_thru, land = pending[nm]
        rb = _rs_wait(sb_thru, land, ssem, rsem, after, "rs_wait_" + nm)
        hs, hr, full, token = _half_start(_final_sum(sf, rb, place, "final_sum_" + nm), "half_start_" + nm)
        halves[nm] = (hs, hr, full)
        return token

    def update(nm, after):
        hs, hr, full = halves[nm]
        w, m, v = params[nm]
        big[nm] = _adam_big(w[0], _half_wait(full, hs, hr, after, "half_wait_" + nm), m[0], v[0], "adam_" + nm)
        return big[nm][3]

    s1, r1, vec, land1, tok = _small_start(vec, True, "small_start_pair")
    tok = reduce_("w2", tok)
    vec, got = _small_wait(vec, land1, s1, r1, True, tok, "small_wait_pair")
    pair = _small_pair_sum(vec, got)
    s2, r2, pair, land2, tok = _small_start(pair, False, "small_start_chips")
    sf, sb = pending["win"]
    ssem, rsem, sb_thru, land, tok = _rs_start(sb, "rs_start_win", tok)
    pending["win"] = (sf, ssem, rsem, sb_thru, land)
    tok = reduce_("w1", tok)
    tok = update("w2", tok)
    tok = update("w1", tok)
    pair, got = _small_wait(pair, land2, s2, r2, False, tok, "small_wait_chips")
    red = _small_chip_sum(pair, got, place)
    tok = reduce_("wout", red)
    tok = update("wout", tok)
    tok = reduce_("win", tok)
    after = update("win", tok)
    offs = [0]
    for s in sizes:
        offs.append(offs[-1] + s)

    def take(i, shape):
        n = 1
        for d in shape:
            n *= d
        return red[offs[i]:offs[i] + n // 128].reshape(shape)

    loss = red[0, 0]
    G_n1w = take(1, (1, D))
    G_gb = take(2, (1, KW))
    G_gnw = take(3, (1, DV))
    G_ps = take(4, (1, PW))
    G_n2w = take(5, (1, D))
    G_fw = take(6, (1, D))
    G_meta = lax.dynamic_slice(take(7, (N_META, D)), (0, me * 512), (N_META, 512))
    G_gw2 = lax.dynamic_slice(take(8, (RANK, KW)), (0, me * 128), (RANK, 128))
    G_pw = lax.dynamic_slice(take(9, (4, GC, GC)), (0, me * 64, 0), (4, 64, GC))

    G_win, d_win, nm_win, nv_win = big["win"]
    G_wout, d_wout, nm_wout, nv_wout = big["wout"]
    G_w1, d_w1, nm_w1, nv_w1 = big["w1"]
    G_w2, d_w2, nm_w2, nv_w2 = big["w2"]
    ws = [meta_tokens, norm1_w, gate_w2[0], gate_b, gla_norm_w, pool_w[0], pool_scale, norm2_w, fw]
    gs = [G_meta, G_n1w, G_gw2, G_gb, G_gnw, G_pw, G_ps, G_n2w, G_fw]
    ms = [m_meta_tokens, m_norm1_w, m_gate_w2[0], m_gate_b, m_gla_norm_w, m_pool_w[0], m_pool_scale, m_norm2_w,
          m_final_norm_w.reshape(1, D)]
    vs = [v_meta_tokens, v_norm1_w, v_gate_w2[0], v_gate_b, v_gla_norm_w, v_pool_w[0], v_pool_scale, v_norm2_w,
          v_final_norm_w.reshape(1, D)]
    ds, nms, nvs = _adam_small(ws, gs, ms, vs, after)

    def assemble(small, win_, wout_, w1_, w2_):
        meta_, n1_, gw2_, gb_, gnw_, pw_, ps_, n2_, fw_ = small
        return (meta_, n1_, win_[None], gw2_[None], gb_, gnw_, pw_[None], ps_, wout_[None], n2_, w1_[None], w2_[None],
                fw_.reshape(D))

    grads_out = assemble(gs, G_win, G_wout, G_w1, G_w2)
    deltas = assemble(ds, d_win, d_wout, d_w1, d_w2)
    new_m = assemble(nms, nm_win, nm_wout, nm_w1, nm_w2)
    new_v = assemble(nvs, nv_win, nv_wout, nv_w1, nv_w2)
    return (loss, grad_x[None], *grads_out, *deltas, *new_m, *new_v)
```

```python
import functools

import jax
import jax.numpy as jnp
from jax import lax
from jax.experimental import pallas as pl
from jax.experimental.pallas import tpu as pltpu

F32 = jnp.float32
BF16 = jnp.bfloat16

D = 2048
SEQ = 2048
N_META = 16
CH = 64
TP = 2176
NCH = TP // CH
ROW_LO = 112
X_LO = 128
ROW_HI = TP
XT = 128
NXT = TP // XT
HEADS = 4
DK = 128
DV = 256
KW = HEADS * DK
GW = HEADS * DV
RANK = 16
TAU = 16.0
WINDOWS = (2, 4, 8, 16)
PW = 1024
GC = 256
DFF = 8192
EPS = 1e-6
SHARD_IN = 1028
PAD_IN = 1152
N_CHIP = 4

LR = 0.001
B1 = 0.9
B2 = 0.999
AEPS = 1e-08
WD = 0.01
STEP = 10

VMEM_LIMIT = 60 * 1024 * 1024
ANY = pl.BlockSpec(memory_space=pl.ANY)
VMEM_FULL = pl.BlockSpec(memory_space=pltpu.VMEM)
MESH = pl.DeviceIdType.MESH


def _cp(sem=None):
    if sem is None:
        return pltpu.CompilerParams(vmem_limit_bytes=VMEM_LIMIT)
    return pltpu.CompilerParams(dimension_semantics=sem, vmem_limit_bytes=VMEM_LIMIT)


def _dot(a, b):
    return jnp.dot(a, b, preferred_element_type=F32)


def _dot_nt(a, b):
    return lax.dot_general(a, b, (((1,), (1,)), ((), ())), preferred_element_type=F32)


def _dot_tn(a, b):
    return lax.dot_general(a, b, (((0,), (0,)), ((), ())), preferred_element_type=F32)


def _sds(shape, dtype):
    return jax.ShapeDtypeStruct(shape, dtype)


def _embed_norm(x, meta_full, w, dep=None):
    def body(x_ref, meta_ref, w_ref, dep_ref, h_ref, u_ref):
        i = pl.program_id(0)

        @pl.when(i == 0)
        def _():
            h_ref[...] = jnp.zeros_like(h_ref)
            h_ref[ROW_LO:X_LO, :] = meta_ref[...]

        @pl.when(i >= 1)
        def _():
            h_ref[...] = x_ref[...]

        h = h_ref[...]
        r = lax.rsqrt(jnp.mean(h * h, axis=-1, keepdims=True) + EPS)
        u_ref[...] = ((h * r) * w_ref[...]).astype(BF16)

    return pl.pallas_call(
        body, name="embed_norm1", grid=(NXT,),
        in_specs=[pl.BlockSpec((XT, D), lambda i: (jnp.maximum(i - 1, 0), 0)),
                  pl.BlockSpec((N_META, D), lambda i: (0, 0)),
                  pl.BlockSpec((1, D), lambda i: (0, 0)), ANY],
        out_specs=[pl.BlockSpec((XT, D), lambda i: (i, 0)), pl.BlockSpec((XT, D), lambda i: (i, 0))],
        out_shape=[_sds((TP, D), F32), _sds((TP, D), BF16)],
        compiler_params=_cp(("arbitrary",)),
    )(x, meta_full, w, _dep(dep))


def _norm_rows(h, w, name):
    tr = 272

    def body(h_ref, w_ref, o_ref):
        hv = h_ref[...]
        r = lax.rsqrt(jnp.mean(hv * hv, axis=-1, keepdims=True) + EPS)
        o_ref[...] = ((hv * r) * w_ref[...]).astype(BF16)

    return pl.pallas_call(
        body, name=name, grid=(TP // tr,),
        in_specs=[pl.BlockSpec((tr, D), lambda i: (i, 0)), pl.BlockSpec((1, D), lambda i: (0, 0))],
        out_specs=pl.BlockSpec((tr, D), lambda i: (i, 0)),
        out_shape=_sds((TP, D), BF16),
        compiler_params=_cp(("arbitrary",)),
    )(h, w)


def _loss_head(h2, target, fw):
    def body(h_ref, t_ref, w_ref, dh_ref, dhb_ref, dw_ref, loss_ref):
        i = pl.program_id(0)

        @pl.when(i == 0)
        def _():
            dw_ref[...] = jnp.zeros_like(dw_ref)
            loss_ref[...] = jnp.zeros_like(loss_ref)

        h = h_ref[...]
        w = w_ref[...]
        r = lax.rsqrt(jnp.mean(h * h, axis=-1, keepdims=True) + EPS)
        xh = h * r
        y = xh * w
        is_x = (i >= 1).astype(F32)
        diff = (y - t_ref[...]) * is_x
        loss_ref[...] += jnp.sum(diff * diff) * (0.5 / D)
        dy = diff * (1.0 / D)
        dw_ref[...] += jnp.sum(dy * xh, axis=0, keepdims=True)
        gx = dy * w
        dh = r * (gx - xh * jnp.mean(gx * xh, axis=-1, keepdims=True))
        dh_ref[...] = dh
        dhb_ref[...] = dh.astype(BF16)

    return pl.pallas_call(
        body, name="loss_head", grid=(NXT,),
        in_specs=[pl.BlockSpec((XT, D), lambda i: (i, 0)),
                  pl.BlockSpec((XT, D), lambda i: (jnp.maximum(i - 1, 0), 0)),
                  pl.BlockSpec((1, D), lambda i: (0, 0))],
        out_specs=[pl.BlockSpec((XT, D), lambda i: (i, 0)), pl.BlockSpec((XT, D), lambda i: (i, 0)),
                   pl.BlockSpec((1, D), lambda i: (0, 0)), pl.BlockSpec((8, 128), lambda i: (0, 0))],
        out_shape=[_sds((TP, D), F32), _sds((TP, D), BF16), _sds((1, D), F32), _sds((8, 128), F32)],
        compiler_params=_cp(("arbitrary",)),
    )(h2, target, fw)


def _norm_bwd(dn, h, dres, w, name, dep=None):
    tr = 272

    def body(dn_ref, h_ref, dres_ref, w_ref, dep_ref, o_ref, ob_ref, dw_ref):
        @pl.when(pl.program_id(0) == 0)
        def _():
            dw_ref[...] = jnp.zeros_like(dw_ref)

        hv = h_ref[...]
        dnv = dn_ref[...]
        r = lax.rsqrt(jnp.mean(hv * hv, axis=-1, keepdims=True) + EPS)
        xh = hv * r
        dw_ref[...] += jnp.sum(dnv * xh, axis=0, keepdims=True)
        gx = dnv * w_ref[...]
        dh = dres_ref[...] + r * (gx - xh * jnp.mean(gx * xh, axis=-1, keepdims=True))
        o_ref[...] = dh
        ob_ref[...] = dh.astype(BF16)

    row = pl.BlockSpec((tr, D), lambda i: (i, 0))
    vec = pl.BlockSpec((1, D), lambda i: (0, 0))
    return pl.pallas_call(
        body, name=name, grid=(TP // tr,),
        in_specs=[row, row, row, vec, ANY], out_specs=[row, row, vec],
        out_shape=[_sds((TP, D), F32), _sds((TP, D), BF16), _sds((1, D), F32)],
        compiler_params=_cp(("arbitrary",)),
    )(dn, h, dres, w, _dep(dep))


def _input_grad(du, h0, dh1, w, dep=None):
    def body(du_ref, h_ref, dres_ref, w_ref, dep_ref, gx_ref, gm_ref, dw_ref):
        i = pl.program_id(0)

        @pl.when(i == 0)
        def _():
            dw_ref[...] = jnp.zeros_like(dw_ref)

        hv = h_ref[...]
        dnv = du_ref[...]
        r = lax.rsqrt(jnp.mean(hv * hv, axis=-1, keepdims=True) + EPS)
        xh = hv * r
        dw_ref[...] += jnp.sum(dnv * xh, axis=0, keepdims=True)
        g = dnv * w_ref[...]
        dh = dres_ref[...] + r * (g - xh * jnp.mean(g * xh, axis=-1, keepdims=True))

        @pl.when(i == 0)
        def _():
            gm_ref[...] = dh[ROW_LO:X_LO, :]

        @pl.when(i >= 1)
        def _():
            gx_ref[...] = dh

    row = pl.BlockSpec((XT, D), lambda i: (i, 0))
    vec = pl.BlockSpec((1, D), lambda i: (0, 0))
    return pl.pallas_call(
        body, name="input_grad", grid=(NXT,),
        in_specs=[row, row, row, vec, ANY],
        out_specs=[pl.BlockSpec((XT, D), lambda i: (jnp.maximum(i - 1, 0), 0)),
                   pl.BlockSpec((N_META, D), lambda i: (0, 0)), vec],
        out_shape=[_sds((SEQ, D), F32), _sds((N_META, D), F32), _sds((1, D), F32)],
        compiler_params=_cp(("arbitrary",)),
    )(du, h0, dh1, w, _dep(dep))


def _in_proj(u, wg):
    def body(u_ref, w_ref, o_ref):
        o_ref[0] = _dot(u_ref[...], w_ref[0])

    return pl.pallas_call(
        body, name="in_proj", grid=(N_CHIP,),
        in_specs=[VMEM_FULL, pl.BlockSpec((1, D, PAD_IN), lambda k: (k, 0, 0))],
        out_specs=pl.BlockSpec((1, TP, PAD_IN), lambda k: (k, 0, 0)),
        out_shape=_sds((N_CHIP, TP, PAD_IN), F32),
        compiler_params=_cp(("arbitrary",)),
    )(u, wg)


def _out_proj(og, op, wout, h0):
    tn = 512

    def body(og_ref, op_ref, w_ref, h_ref, o_ref):
        acc = _dot(og_ref[...], w_ref[0:GW, :]) + _dot(op_ref[...], w_ref[GW:D, :])
        o_ref[...] = h_ref[...] + acc

    return pl.pallas_call(
        body, name="out_proj", grid=(D // tn,),
        in_specs=[VMEM_FULL, VMEM_FULL, pl.BlockSpec((D, tn), lambda j: (0, j)),
                  pl.BlockSpec((TP, tn), lambda j: (0, j))],
        out_specs=pl.BlockSpec((TP, tn), lambda j: (0, j)),
        out_shape=_sds((TP, D), F32),
        compiler_params=_cp(("arbitrary",)),
    )(og, op, wout, h0)


def _mlp_up(n2, w1g, early=()):
    tn = 1024
    per = D // tn
    ne = len(early)

    def body(n_ref, w_ref, *rest):
        zr_ref, a_ref = rest[ne:]
        z = jnp.maximum(_dot(n_ref[...], w_ref[0]), 0.0)
        zr_ref[...] = z.astype(BF16)
        a_ref[...] = (z * z).astype(BF16)

    col = pl.BlockSpec((TP, tn), lambda k, j: (0, k * per + j))
    return pl.pallas_call(
        body, name="mlp_up", grid=(N_CHIP, per),
        in_specs=[VMEM_FULL, pl.BlockSpec((1, D, tn), lambda k, j: (k, 0, j))] + [ANY] * ne,
        out_specs=[col, col],
        out_shape=[_sds((TP, DFF), BF16), _sds((TP, DFF), BF16)],
        compiler_params=_cp(("arbitrary", "arbitrary")),
    )(n2, w1g, *early)


def _mlp_down(a, w2, h1):
    tk = 1024
    nk = DFF // tk

    def body(a_ref, w_ref, h_ref, o_ref, acc_ref):
        k = pl.program_id(0)

        @pl.when(k == 0)
        def _():
            pltpu.sync_copy(h_ref, acc_ref)

        acc_ref[...] += _dot(a_ref[...], w_ref[...])

        @pl.when(k == nk - 1)
        def _():
            pltpu.sync_copy(acc_ref, o_ref)

    return pl.pallas_call(
        body, name="mlp_down", grid=(nk,),
        in_specs=[pl.BlockSpec((TP, tk), lambda k: (0, k)), pl.BlockSpec((tk, D), lambda k: (k, 0)), ANY],
        out_specs=ANY,
        out_shape=_sds((TP, D), F32),
        scratch_shapes=[pltpu.VMEM((TP, D), F32)],
        compiler_params=_cp(("arbitrary",)),
    )(a, w2, h1)


def _mlp_dz(dh2b, w2, zr, dep=None):
    tn = 1024

    def body(d_ref, w_ref, z_ref, dep_ref, o_ref):
        da = _dot_nt(d_ref[...], w_ref[...])
        o_ref[...] = (da * (2.0 * z_ref[...].astype(F32))).astype(BF16)

    col = pl.BlockSpec((TP, tn), lambda j: (0, j))
    return pl.pallas_call(
        body, name="mlp_dz", grid=(DFF // tn,),
        in_specs=[VMEM_FULL, pl.BlockSpec((tn, D), lambda j: (j, 0)), col, ANY],
        out_specs=col,
        out_shape=_sds((TP, DFF), BF16),
        compiler_params=_cp(("arbitrary",)),
    )(dh2b, w2, zr, _dep(dep))


def _grad_w2(a, dh2b):
    tm = 1024

    def body(a_ref, d_ref, o_ref):
        o_ref[...] = _dot_tn(a_ref[...], d_ref[...])

    return pl.pallas_call(
        body, name="grad_w2", grid=(DFF // tm,),
        in_specs=[pl.BlockSpec((TP, tm), lambda j: (0, j)), VMEM_FULL],
        out_specs=pl.BlockSpec((tm, D), lambda j: (j, 0)),
        out_shape=_sds((DFF, D), F32),
        compiler_params=_cp(("arbitrary",)),
    )(a, dh2b)


def _dep(token):
    return jnp.zeros((8, 128), F32) if token is None else token


def _grad_w1(n2, dz, dep=None):
    tn = 1024
    per = D // tn

    def body(n_ref, d_ref, dep_ref, o_ref):
        o_ref[0] = _dot_tn(n_ref[...], d_ref[...])

    return pl.pallas_call(
        body, name="grad_w1", grid=(N_CHIP, per),
        in_specs=[VMEM_FULL, pl.BlockSpec((TP, tn), lambda k, j: (0, k * per + j)), ANY],
        out_specs=pl.BlockSpec((1, D, tn), lambda k, j: (k, 0, j)),
        out_shape=_sds((N_CHIP, D, D), F32),
        compiler_params=_cp(("arbitrary", "arbitrary")),
    )(n2, dz, _dep(dep))


def _mlp_dn(dz, w1g, dep=None):
    tk = 1024
    per = D // tk
    nk = DFF // tk

    def body(d_ref, w_ref, dep_ref, o_ref, acc_ref):
        k = pl.program_id(0)
        part = _dot_nt(d_ref[...], w_ref[0])

        @pl.when(k == 0)
        def _():
            acc_ref[...] = part

        @pl.when(k > 0)
        def _():
            acc_ref[...] += part

        @pl.when(k == nk - 1)
        def _():
            pltpu.sync_copy(acc_ref, o_ref)

    return pl.pallas_call(
        body, name="mlp_dn", grid=(nk,),
        in_specs=[pl.BlockSpec((TP, tk), lambda k: (0, k)),
                  pl.BlockSpec((1, D, tk), lambda k: (k // per, 0, k % per)), ANY],
        out_specs=ANY,
        out_shape=_sds((TP, D), F32),
        scratch_shapes=[pltpu.VMEM((TP, D), F32)],
        compiler_params=_cp(("arbitrary",)),
    )(dz, w1g, _dep(dep))


def _mixed_grad(dh1b, wout):
    tn = 512

    def body(d_ref, w_ref, o_ref):
        o_ref[...] = _dot_nt(d_ref[...], w_ref[...])

    return pl.pallas_call(
        body, name="mixed_grad", grid=(D // tn,),
        in_specs=[VMEM_FULL, pl.BlockSpec((tn, D), lambda j: (j, 0))],
        out_specs=pl.BlockSpec((TP, tn), lambda j: (0, j)),
        out_shape=_sds((TP, D), F32),
        compiler_params=_cp(("arbitrary",)),
    )(dh1b, wout)


def _grad_wout(og, op, dh1b):
    tm = 512

    def body(og_ref, op_ref, d_ref, o_ref):
        j = pl.program_id(0)

        @pl.when(j < 2)
        def _():
            o_ref[0] = _dot_tn(og_ref[...], d_ref[...])

        @pl.when(j >= 2)
        def _():
            o_ref[0] = _dot_tn(op_ref[...], d_ref[...])

    return pl.pallas_call(
        body, name="grad_wout", grid=(N_CHIP,),
        in_specs=[pl.BlockSpec((TP, tm), lambda j: (0, jnp.minimum(j, 1))),
                  pl.BlockSpec((TP, tm), lambda j: (0, jnp.maximum(j - 2, 0))), VMEM_FULL],
        out_specs=pl.BlockSpec((1, tm, D), lambda j: (j, 0, 0)),
        out_shape=_sds((N_CHIP, tm, D), F32),
        compiler_params=_cp(("arbitrary",)),
    )(og, op, dh1b)


def _in_grad(dq, dk, dv, dr, dglr, dpu, wg, dep=None):
    def body(dq_ref, dk_ref, dv_ref, dr_ref, dg_ref, dpu_ref, w_ref, dep_ref, o_ref):
        dv, dr, dg = dv_ref[...], dr_ref[...], dg_ref[...]
        head, tail = slice(0, GW), slice(GW, PAD_IN)
        o_ref[...] = (_dot_nt(dq_ref[...], w_ref[0, :, 0:KW]) + _dot_nt(dk_ref[...], w_ref[0, :, KW:GW])
                      + _dot_nt(dv[:, 0:128], w_ref[0, :, tail])
                      + _dot_nt(dv, w_ref[1, :, head]) + _dot_nt(dr[:, 0:128], w_ref[1, :, tail])
                      + _dot_nt(dr, w_ref[2, :, head]) + _dot_nt(dg, w_ref[2, :, tail])
                      + _dot_nt(dpu_ref[...], w_ref[3, :, head]) + _dot_nt(dg, w_ref[3, :, tail]))

    tn = 512
    return pl.pallas_call(
        body, name="in_grad", grid=(D // tn,),
        in_specs=[VMEM_FULL] * 6 + [pl.BlockSpec((N_CHIP, tn, PAD_IN), lambda j: (0, j, 0)), ANY],
        out_specs=pl.BlockSpec((TP, tn), lambda j: (0, j)),
        out_shape=_sds((TP, D), F32),
        compiler_params=_cp(("arbitrary",)),
    )(dq, dk, dv, dr, dglr, dpu, wg, _dep(dep))


def _grad_win(u, dq, dk, dv, dr, dglr, dpu, dep=None):
    tm = 512

    def body(u_ref, dq_hbm, dk_hbm, dv_hbm, dr_hbm, dg_hbm, dpu_hbm, dep_ref, o_ref, dp_ref, sem):
        k, m = pl.program_id(0), pl.program_id(1)
        head, tail = slice(0, GW), slice(GW, PAD_IN)
        pieces = [[(dq_hbm, slice(0, KW)), (dk_hbm, slice(KW, GW)), (dv_hbm.at[:, 0:128], tail)],
                  [(dv_hbm, head), (dr_hbm.at[:, 0:128], tail)],
                  [(dr_hbm, head), (dg_hbm, tail)],
                  [(dpu_hbm, head), (dg_hbm, tail)]]

        def copies(kk):
            return [pltpu.make_async_copy(src, dp_ref.at[kk % 2, :, cols], sem.at[kk % 2, i])
                    for i, (src, cols) in enumerate(pieces[kk])]

        @pl.when((k == 0) & (m == 0))
        def _():
            for cp in copies(0):
                cp.start()

        for kk in range(N_CHIP):
            @pl.when((k == kk) & (m == 0))
            def _(kk=kk):
                for cp in copies(kk):
                    cp.wait()
                if kk + 1 < N_CHIP:
                    for cp in copies(kk + 1):
                        cp.start()

        g = _dot_tn(u_ref[...], dp_ref[k % 2])
        lane = lax.broadcasted_iota(jnp.int32, (tm, PAD_IN), 1)
        for kk in range(N_CHIP):
            @pl.when(k == kk)
            def _(kk=kk):
                if kk == 0:
                    nat = g
                elif kk < 3:
                    nat = pltpu.roll(g, PAD_IN - 4 * kk, 1)
                else:
                    nat = jnp.where(lane < 4, pltpu.roll(g, PAD_IN - (GW + 12), 1), pltpu.roll(g, 4, 1))
                o_ref[0] = nat[:, 0:SHARD_IN]

    return pl.pallas_call(
        body, name="grad_win", grid=(N_CHIP, D // tm),
        in_specs=[pl.BlockSpec((TP, tm), lambda k, m: (0, m))] + [ANY] * 7,
        out_specs=pl.BlockSpec((1, tm, SHARD_IN), lambda k, m: (k, m, 0)),
        out_shape=_sds((N_CHIP, D, SHARD_IN), F32),
        scratch_shapes=[pltpu.VMEM((2, TP, PAD_IN), BF16), pltpu.SemaphoreType.DMA((2, 3))],
        compiler_params=_cp(("arbitrary", "arbitrary")),
    )(u, dq, dk, dv, dr, dglr, dpu, _dep(dep))


def _split3(x):
    hi = x.astype(BF16)
    r1 = x - hi.astype(F32)
    mid = r1.astype(BF16)
    lo = (r1 - mid.astype(F32)).astype(BF16)
    return hi, mid, lo


def _tri_sum(tri, x):
    hi, mid, lo = _split3(x)
    return _dot(tri, hi) + _dot(tri, mid) + _dot(tri, lo)


def _gla_common(n, glr, gw2, gb):
    rows = n * CH + lax.broadcasted_iota(jnp.int32, (CH, 1), 0)
    valid = (rows >= ROW_LO) & (rows < ROW_HI)
    g_raw = _dot(glr.astype(BF16), gw2.astype(BF16)) + gb
    logsig = jnp.minimum(g_raw, 0.0) - jnp.log(1.0 + jnp.exp(-jnp.abs(g_raw)))
    logg = jnp.where(valid, logsig * (1.0 / TAU), 0.0)
    ci = lax.broadcasted_iota(jnp.int32, (CH, CH), 0)
    si = lax.broadcasted_iota(jnp.int32, (CH, CH), 1)
    lower = ci >= si
    G = _tri_sum(lower.astype(BF16), logg)
    Gl = G[CH - 1:CH, :]
    return valid, g_raw, lower, G, Gl


def _p_specs(index):
    def spec(width, shard, col):
        return pl.BlockSpec((1, CH, width), lambda s: (shard, index(s), col))

    return [spec(KW, 0, 0), spec(KW, 0, 1), spec(GW, 1, 0), spec(128, 0, 8), spec(GW, 2, 0), spec(128, 1, 8),
            spec(128, 2, 8), spec(128, 3, 8)]


def _p_load(q_ref, k_ref, vm_ref, vh_ref, rm_ref, rh_ref, ga_ref, gb_ref):
    def joined(main, head):
        return jnp.concatenate([main[:, 0:128] + head, main[:, 128:]], axis=1)

    return q_ref[0], k_ref[0], joined(vm_ref[0], vh_ref[0]), joined(rm_ref[0], rh_ref[0]), ga_ref[0] + gb_ref[0]


def _gla_fwd(P, gw2, gb, gnw, dep=None):
    scale = DK ** -0.5

    def body(p0, p1, p2, p3, p4, p5, p6, p7, gw2_ref, gb_ref, gnw_ref, dep_ref, o_ref, og_ref, sp_ref, st_ref):
        n = pl.program_id(0)

        @pl.when(n == 0)
        def _():
            st_ref[...] = jnp.zeros_like(st_ref)

        q_all, k_all, v_all, r_all, glr = _p_load(p0, p1, p2, p3, p4, p5, p6, p7)
        _, _, lower, G, Gl = _gla_common(n, glr, gw2_ref[...], gb_ref[...])
        eG = jnp.exp(G)
        eN = jnp.exp(-G)
        eE = jnp.exp(Gl - G)
        dec = jnp.exp(Gl)
        gnw_v = gnw_ref[...]
        for h in range(HEADS):
            ks = slice(h * DK, (h + 1) * DK)
            vs = slice(h * DV, (h + 1) * DV)
            kh = k_all[:, ks]
            vh = v_all[:, vs].astype(BF16)
            qd = ((q_all[:, ks] * scale) * eG[:, ks]).astype(BF16)
            ki = (kh * eN[:, ks]).astype(BF16)
            ke = (kh * eE[:, ks]).astype(BF16)
            st = st_ref[h]
            a = jnp.where(lower, _dot_nt(qd, ki), 0.0).astype(BF16)
            o = _dot(a, vh) + _dot_nt(qd, st.astype(BF16))
            sp_ref[0, h] = st
            st_ref[h] = st * dec[:, ks] + _dot_tn(vh, ke)
            o_ref[:, vs] = o
            rs = lax.rsqrt(jnp.mean(o * o, axis=-1, keepdims=True) + EPS)
            rv = r_all[:, vs]
            gate = rv / (1.0 + jnp.exp(-rv))
            og_ref[:, vs] = (((o * rs) * gnw_v) * gate).astype(BF16)

    rv_ = pl.BlockSpec((CH, GW), lambda n: (n, 0))

    def full(shape):
        return pl.BlockSpec(shape, lambda n: tuple(0 for _ in shape))

    return pl.pallas_call(
        body, name="gla_fwd", grid=(NCH,),
        in_specs=_p_specs(lambda n: n) + [full((128, KW)), full((1, KW)), full((1, DV)), ANY],
        out_specs=[rv_, rv_, pl.BlockSpec((1, HEADS, DV, DK), lambda n: (n, 0, 0, 0))],
        out_shape=[_sds((TP, GW), F32), _sds((TP, GW), BF16), _sds((NCH, HEADS, DV, DK), F32)],
        scratch_shapes=[pltpu.VMEM((HEADS, DV, DK), F32)],
        compiler_params=_cp(("arbitrary",)),
    )(*([P] * 8), gw2, gb, gnw, _dep(dep))


def _gla_bwd(dog, o, P, gw2, gb, gnw, sp, dep=None):
    scale = DK ** -0.5

    def body(dog_ref, o_ref, p0, p1, p2, p3, p4, p5, p6, p7, gw2_ref, gb_ref, gnw_ref, sp_ref, dep_ref,
             dq_ref, dk_ref, dv_ref, dr_ref, dglr_ref, dgw2_ref, dgb_ref, dgnw_ref, ds_ref):
        step = pl.program_id(0)
        n = NCH - 1 - step

        @pl.when(step == 0)
        def _():
            ds_ref[...] = jnp.zeros_like(ds_ref)
            dgw2_ref[...] = jnp.zeros_like(dgw2_ref)
            dgb_ref[...] = jnp.zeros_like(dgb_ref)
            dgnw_ref[...] = jnp.zeros_like(dgnw_ref)

        q_all, k_all, v_all, r_all, glr_v = _p_load(p0, p1, p2, p3, p4, p5, p6, p7)
        gw2_b = gw2_ref[...].astype(BF16)
        valid, g_raw, lower, G, Gl = _gla_common(n, glr_v, gw2_ref[...], gb_ref[...])
        upper = lax.broadcasted_iota(jnp.int32, (CH, CH), 0) <= lax.broadcasted_iota(jnp.int32, (CH, CH), 1)
        eG = jnp.exp(G)
        eN = jnp.exp(-G)
        eE = jnp.exp(Gl - G)
        dec = jnp.exp(Gl)
        gnw_v = gnw_ref[...]
        last = lax.broadcasted_iota(jnp.int32, (CH, 1), 0) == CH - 1
        dgnw_acc = jnp.zeros((1, DV), F32)
        dG_parts = []
        for h in range(HEADS):
            ks = slice(h * DK, (h + 1) * DK)
            vs = slice(h * DV, (h + 1) * DV)
            oh = o_ref[:, vs]
            rv = r_all[:, vs]
            dg = dog_ref[:, vs]
            sig = 1.0 / (1.0 + jnp.exp(-rv))
            gate = rv * sig
            rs = lax.rsqrt(jnp.mean(oh * oh, axis=-1, keepdims=True) + EPS)
            ohat = oh * rs
            dr_ref[:, vs] = ((dg * (ohat * gnw_v)) * (sig * (1.0 + rv * (1.0 - sig)))).astype(BF16)
            don = dg * gate
            dgnw_acc = dgnw_acc + jnp.sum(don * ohat, axis=0, keepdims=True)
            gxn = don * gnw_v
            do = (rs * (gxn - ohat * jnp.mean(gxn * ohat, axis=-1, keepdims=True))).astype(BF16)
            kh = k_all[:, ks]
            vh = v_all[:, vs].astype(BF16)
            qd_f = (q_all[:, ks] * scale) * eG[:, ks]
            ki_f = kh * eN[:, ks]
            ke_f = kh * eE[:, ks]
            qd, ki, ke = qd_f.astype(BF16), ki_f.astype(BF16), ke_f.astype(BF16)
            spt = sp_ref[0, h]
            dst = ds_ref[h]
            dst_b = dst.astype(BF16)
            a_t = jnp.where(upper, _dot_nt(ki, qd), 0.0).astype(BF16)
            da = jnp.where(lower, _dot_nt(do, vh), 0.0).astype(BF16)
            da_t = jnp.where(upper, _dot_nt(vh, do), 0.0).astype(BF16)
            dv_ref[:, vs] = (_dot(a_t, do) + _dot_nt(ke, dst_b)).astype(BF16)
            dqd = _dot(da, ki) + _dot(do, spt.astype(BF16))
            dki = _dot(da_t, qd)
            dke = _dot(vh, dst_b)
            ddec = jnp.sum(spt * dst, axis=0, keepdims=True)
            ds_ref[h] = dst * dec[:, ks] + _dot_tn(do, qd)
            dq_ref[:, ks] = ((dqd * eG[:, ks]) * scale).astype(BF16)
            dk_ref[:, ks] = (dki * eN[:, ks] + dke * eE[:, ks]).astype(BF16)
            dke_ke = dke * ke_f
            dG = dqd * qd_f - dki * ki_f - dke_ke
            dGl = jnp.sum(dke_ke, axis=0, keepdims=True) + ddec * dec[:, ks]
            dG_parts.append(dG + jnp.where(last, dGl, 0.0))
        dgnw_ref[...] += dgnw_acc
        dG_all = jnp.concatenate(dG_parts, axis=1)
        dlogg = jnp.where(valid, _tri_sum(upper.astype(BF16), dG_all), 0.0)
        dg_raw = (dlogg * (1.0 / TAU)) * (1.0 / (1.0 + jnp.exp(g_raw)))
        dgb_ref[...] += jnp.sum(dg_raw, axis=0, keepdims=True)
        dg_b = dg_raw.astype(BF16)
        dgw2_ref[...] += _dot_tn(glr_v.astype(BF16), dg_b)
        dglr_ref[...] = _dot_nt(dg_b, gw2_b).astype(BF16)

    def back(s):
        return NCH - 1 - s

    rk = pl.BlockSpec((CH, KW), lambda s: (back(s), 0))
    rv_ = pl.BlockSpec((CH, GW), lambda s: (back(s), 0))
    rg = pl.BlockSpec((CH, 128), lambda s: (back(s), 0))

    def full(shape):
        return pl.BlockSpec(shape, lambda s: tuple(0 for _ in shape))

    return pl.pallas_call(
        body, name="gla_bwd", grid=(NCH,),
        in_specs=[rv_, rv_] + _p_specs(back) + [full((128, KW)), full((1, KW)), full((1, DV)),
                  pl.BlockSpec((1, HEADS, DV, DK), lambda s: (back(s), 0, 0, 0)), ANY],
        out_specs=[rk, rk, rv_, rv_, rg, full((128, KW)), full((1, KW)), full((1, DV))],
        out_shape=[_sds((TP, KW), BF16), _sds((TP, KW), BF16), _sds((TP, GW), BF16), _sds((TP, GW), BF16),
                   _sds((TP, 128), BF16), _sds((128, KW), F32), _sds((1, KW), F32), _sds((1, DV), F32)],
        scratch_shapes=[pltpu.VMEM((HEADS, DV, DK), F32)],
        compiler_params=_cp(("arbitrary",)),
    )(dog, o, *([P] * 8), gw2, gb, gnw, sp, _dep(dep))


POOL_TR = 128
HALO = 16


def _pool_counts(base, nrows):
    rows = base + lax.broadcasted_iota(jnp.int32, (nrows, 1), 0)
    valid = (rows >= ROW_LO) & (rows < ROW_HI)
    t1 = (rows - ROW_LO + 1).astype(F32)
    cnts = [jnp.clip(t1, 1.0, float(w)) for w in WINDOWS]
    return valid, cnts


def _pool_fwd(P, pw, ps):
    def body(cur_ref, prev_ref, pw_ref, ps_ref, y_ref, op_ref):
        i = pl.program_id(0)
        cur = cur_ref[0]
        full = jnp.concatenate([prev_ref[0], cur], axis=0)
        s2 = full + pltpu.roll(full, 1, 0)
        s4 = s2 + pltpu.roll(s2, 2, 0)
        s8 = s4 + pltpu.roll(s4, 4, 0)
        s16 = s8 + pltpu.roll(s8, 8, 0)
        valid, cnts = _pool_counts(i * POOL_TR, POOL_TR)
        for g, s in enumerate((s2, s4, s8, s16)):
            cs = slice(g * GC, (g + 1) * GC)
            y = s[HALO:, cs] / cnts[g] - cur[:, cs]
            yb = jnp.where(valid, y, 0.0).astype(BF16)
            y_ref[:, cs] = yb
            op_ref[:, cs] = (_dot(yb, pw_ref[g].astype(BF16)) * ps_ref[:, cs]).astype(BF16)

    row = pl.BlockSpec((POOL_TR, PW), lambda i: (i, 0))
    per = POOL_TR // HALO
    return pl.pallas_call(
        body, name="pool_fwd", grid=(TP // POOL_TR,),
        in_specs=[pl.BlockSpec((1, POOL_TR, PW), lambda i: (3, i, 0)),
                  pl.BlockSpec((1, HALO, PW), lambda i: (3, jnp.maximum(i * per - 1, 0), 0)),
                  pl.BlockSpec((4, GC, GC), lambda i: (0, 0, 0)), pl.BlockSpec((1, PW), lambda i: (0, 0))],
        out_specs=[row, row],
        out_shape=[_sds((TP, PW), BF16), _sds((TP, PW), BF16)],
        compiler_params=_cp(("arbitrary",)),
    )(P, P, pw, ps)


def _pool_bwd(dop, y, pw, ps, dep=None):
    nblk = TP // HALO

    def body(cur_ref, nxt_ref, y_ref, pw_ref, ps_ref, dep_ref, dpu_ref, dpw_ref, dps_ref):
        i = pl.program_id(0)

        @pl.when(i == 0)
        def _():
            dpw_ref[...] = jnp.zeros_like(dpw_ref)
            dps_ref[...] = jnp.zeros_like(dps_ref)

        n_all = POOL_TR + HALO
        dcur = cur_ref[...]
        dall = jnp.concatenate([dcur, nxt_ref[...]], axis=0)
        valid, cnts = _pool_counts(i * POOL_TR, n_all)
        for g in range(4):
            cs = slice(g * GC, (g + 1) * GC)
            pwb = pw_ref[g].astype(BF16)
            yb = y_ref[:, cs]
            dyw = (dall[:, cs] * ps_ref[:, cs]).astype(BF16)
            dps_ref[:, cs] += jnp.sum(dcur[:, cs] * _dot(yb, pwb), axis=0, keepdims=True)
            dpw_ref[g] += _dot_tn(yb, dyw[0:POOL_TR, :])
            dyv = jnp.where(valid, _dot_nt(dyw, pwb), 0.0)
            e = dyv / cnts[g]
            w = WINDOWS[g]
            sh = 1
            while sh < w:
                e = e + pltpu.roll(e, n_all - sh, 0)
                sh *= 2
            dpu_ref[:, cs] = (e[0:POOL_TR, :] - dyv[0:POOL_TR, :]).astype(BF16)

    row = pl.BlockSpec((POOL_TR, PW), lambda i: (i, 0))
    per = POOL_TR // HALO
    return pl.pallas_call(
        body, name="pool_bwd", grid=(TP // POOL_TR,),
        in_specs=[pl.BlockSpec((POOL_TR, PW), lambda i: (i, 1)),
                  pl.BlockSpec((HALO, PW), lambda i: (jnp.minimum(i * per + per, nblk - 1), 1)),
                  row, pl.BlockSpec((4, GC, GC), lambda i: (0, 0, 0)), pl.BlockSpec((1, PW), lambda i: (0, 0)), ANY],
        out_specs=[row, pl.BlockSpec((4, GC, GC), lambda i: (0, 0, 0)), pl.BlockSpec((1, PW), lambda i: (0, 0))],
        out_shape=[_sds((TP, PW), BF16), _sds((4, GC, GC), F32), _sds((1, PW), F32)],
        compiler_params=_cp(("arbitrary",)),
    )(dop, dop, y, pw, ps, _dep(dep))


def _place():
    x, y, c = lax.axis_index("x"), lax.axis_index("y"), lax.axis_index("c")
    chips = [(1 - x, y), (x, 1 - y), (1 - x, 1 - y)]
    return x, y, c, chips


HBM = pl.BlockSpec(memory_space=pltpu.HBM)
SEM = pl.BlockSpec(memory_space=pltpu.SEMAPHORE)
EFFECT = pltpu.SideEffectType.DATAFLOW_SIDE_EFFECTING


def _cast_into(w, place, cols_out, name, dep=None):
    rows, cols = w.shape
    tr = 256

    def body(p_ref, w_ref, dep_ref, o_ref):
        if cols_out != cols:
            o_ref[0] = jnp.zeros((tr, cols_out), BF16)
            o_ref[0, :, 0:cols] = w_ref[...].astype(BF16)
        else:
            o_ref[0] = w_ref[...].astype(BF16)

    grid_spec = pltpu.PrefetchScalarGridSpec(
        num_scalar_prefetch=1, grid=(rows // tr,),
        in_specs=[pl.BlockSpec((tr, cols), lambda i, p: (i, 0)), ANY],
        out_specs=pl.BlockSpec((1, tr, cols_out), lambda i, p: (p[0], i, 0)))
    return pl.pallas_call(
        body, name=name, grid_spec=grid_spec,
        out_shape=_sds((N_CHIP, rows, cols_out), BF16),
        compiler_params=_cp(("arbitrary",)),
    )(place, w, _dep(dep))


def _cast_win(w, place, dep=None):
    rows, cols = w.shape
    tr = 256

    def body(p_ref, w_ref, dep_ref, o_ref, t_ref):
        t_ref[...] = jnp.zeros_like(t_ref)
        t_ref[:, 0:cols] = w_ref[...]
        t = t_ref[...]
        lane = lax.broadcasted_iota(jnp.int32, (tr, PAD_IN), 1)
        for kk in range(N_CHIP):
            @pl.when(p_ref[0] == kk)
            def _(kk=kk):
                if kk == 0:
                    placed = t
                elif kk < 3:
                    placed = pltpu.roll(t, 4 * kk, 1)
                else:
                    pool = pltpu.roll(t, PAD_IN - 4, 1)
                    gate = pltpu.roll(t, GW + 12, 1)
                    placed = jnp.where(lane < GW, pool, jnp.where((lane >= GW + 12) & (lane < GW + 16), gate, 0.0))
                o_ref[0] = placed.astype(BF16)

    grid_spec = pltpu.PrefetchScalarGridSpec(
        num_scalar_prefetch=1, grid=(rows // tr,),
        in_specs=[pl.BlockSpec((tr, cols), lambda i, p: (i, 0)), ANY],
        out_specs=pl.BlockSpec((1, tr, PAD_IN), lambda i, p: (p[0], i, 0)),
        scratch_shapes=[pltpu.VMEM((tr, PAD_IN), F32)])
    return pl.pallas_call(
        body, name="cast_win", grid_spec=grid_spec,
        out_shape=_sds((N_CHIP, rows, PAD_IN), BF16),
        compiler_params=_cp(("arbitrary",)),
    )(place, w, _dep(dep))


def _half_rows(ref, k, which):
    h = ref.shape[1] // 2
    return ref.at[k, pl.ds(pl.multiple_of(which * h, 8), h), :]


def _sent_rows(ref, k, which, whole):
    return ref.at[k] if whole else _half_rows(ref, k, which)


def _gather_start(ws, name, whole=None):
    n = len(ws)
    whole = whole or [False] * n

    def body(*refs):
        ins = refs[:n]
        ssems = refs[n:2 * n]
        rsems = refs[2 * n:3 * n]
        token = refs[4 * n]
        x, y, c, chips = _place()
        me = 2 * x + y
        for w in range(n):
            blk = _sent_rows(ins[w], me, c, whole[w])
            for j, chip in enumerate(chips):
                pltpu.make_async_remote_copy(src_ref=blk, dst_ref=blk, send_sem=ssems[w].at[j], recv_sem=rsems[w].at[j],
                                             device_id=(*chip, c), device_id_type=MESH).start()
        token[...] = jnp.zeros_like(token)

    sem3 = pltpu.SemaphoreType.DMA((3,))
    outs = pl.pallas_call(
        body, name=name,
        out_shape=tuple([sem3] * (2 * n) + [pltpu.HBM(w.shape, w.dtype) for w in ws] + [_sds((8, 128), F32)]),
        in_specs=(HBM,) * n, out_specs=(SEM,) * (2 * n) + (HBM,) * n + (VMEM_FULL,),
        input_output_aliases={w: 2 * n + w for w in range(n)},
        compiler_params=pltpu.CompilerParams(has_side_effects=EFFECT),
    )(*[pltpu.with_memory_space_constraint(w, pltpu.HBM) for w in ws])
    return outs[:n], outs[n:2 * n], outs[2 * n:3 * n], outs[3 * n]


def _gather_wait(w, ssem, rsem, after, name, whole=False):
    def body(w_ref, ssem_ref, rsem_ref, after_ref, out_ref):
        x, y, c, chips = _place()
        me = 2 * x + y
        mine = _sent_rows(w_ref, me, c, whole)
        for j, (cx, cy) in enumerate(chips):
            cp = pltpu.make_async_remote_copy(src_ref=mine, dst_ref=_sent_rows(w_ref, 2 * cx + cy, c, whole),
                                              send_sem=ssem_ref.at[j], recv_sem=rsem_ref.at[j],
                                              device_id=(cx, cy, c), device_id_type=MESH)
            cp.wait_send()
            cp.wait_recv()

    return pl.pallas_call(
        body, name=name, out_shape=pltpu.HBM(w.shape, w.dtype),
        in_specs=(HBM, SEM, SEM, ANY), out_specs=HBM, input_output_aliases={0: 0},
        compiler_params=pltpu.CompilerParams(has_side_effects=EFFECT),
    )(w, ssem, rsem, after)


def _gather_copies(ref, kind, ssem, rsem):
    x, y, c, _ = _place()
    xn, yn = (1 - x, y, c), (x, 1 - y, c)
    half = ref.shape[1] // 2
    if kind == "d":
        blk = _half_rows(ref, 2 * x + y, c)
        pairs = [(blk, xn), (blk, yn)]
    else:
        quarter = half // 2

        def piece(k, q):
            return ref.at[k, pl.ds(pl.multiple_of(c * half + q * quarter, 8), quarter), :]

        pairs = [(piece(2 * x + (1 - y), 1), xn), (piece(2 * (1 - x) + y, 0), yn)]
    return [pltpu.make_async_remote_copy(src_ref=blk, dst_ref=blk, send_sem=ssem.at[i], recv_sem=rsem.at[i],
                                         device_id=to, device_id_type=MESH) for i, (blk, to) in enumerate(pairs)]


def _gather_step(name, arrs, waits, starts, sems_in=(), after=None):
    n, nw, ns = len(arrs), len(waits), len(starts)

    def body(*refs):
        a_in = refs[:n]
        s_in = refs[n:n + 2 * nw]
        outs = refs[n + 2 * nw + 1:]
        s_out = outs[:2 * ns]
        for i, (ai, kind) in enumerate(waits):
            for cp in _gather_copies(a_in[ai], kind, s_in[2 * i], s_in[2 * i + 1]):
                cp.wait_send()
                cp.wait_recv()
        for i, (ai, kind) in enumerate(starts):
            for cp in _gather_copies(a_in[ai], kind, s_out[2 * i], s_out[2 * i + 1]):
                cp.start()
        if ns:
            token = outs[2 * ns + n]
            token[...] = jnp.zeros_like(token)

    sem2 = pltpu.SemaphoreType.DMA((2,))
    flat_in = [s for pair in sems_in for s in pair]
    arrs = [pltpu.with_memory_space_constraint(a, pltpu.HBM) for a in arrs]
    outs = pl.pallas_call(
        body, name=name,
        out_shape=tuple([sem2] * (2 * ns) + [pltpu.HBM(a.shape, a.dtype) for a in arrs]
                        + ([_sds((8, 128), F32)] if ns else [])),
        in_specs=(HBM,) * n + (SEM,) * (2 * nw) + (ANY,),
        out_specs=(SEM,) * (2 * ns) + (HBM,) * n + ((VMEM_FULL,) if ns else ()),
        input_output_aliases={i: 2 * ns + i for i in range(n)},
        compiler_params=pltpu.CompilerParams(has_side_effects=EFFECT),
    )(*arrs, *flat_in, _dep(after))
    sems = [(outs[2 * i], outs[2 * i + 1]) for i in range(ns)]
    return sems, list(outs[2 * ns:2 * ns + n]), (outs[2 * ns + n] if ns else None)


def _forward_halves(w, name):
    def body(w_ref, o_ref, ssem, rsem):
        x, y, c, chips = _place()
        sib = (x, y, 1 - c)
        cps = []
        for j, (cx, cy) in enumerate(chips):
            blk = _half_rows(o_ref, 2 * cx + cy, c)
            cps.append(pltpu.make_async_remote_copy(src_ref=blk, dst_ref=blk, send_sem=ssem.at[j], recv_sem=rsem.at[j],
                                                    device_id=sib, device_id_type=MESH))
        for cp in cps:
            cp.start()
        for j, (cx, cy) in enumerate(chips):
            blk = _half_rows(o_ref, 2 * cx + cy, 1 - c)
            pltpu.make_async_remote_copy(src_ref=blk, dst_ref=blk, send_sem=ssem.at[j], recv_sem=rsem.at[j],
                                         device_id=sib, device_id_type=MESH).wait_recv()
        for cp in cps:
            cp.wait_send()

    return pl.pallas_call(
        body, name=name, in_specs=[ANY], out_specs=ANY, out_shape=_sds(w.shape, w.dtype),
        input_output_aliases={0: 0},
        scratch_shapes=[pltpu.SemaphoreType.DMA((3,)), pltpu.SemaphoreType.DMA((3,))],
    )(w)


def _rs_start(sb, name, after=None):
    _, half, cols = sb.shape

    def body(sb_ref, land_ref, after_ref, ssem, rsem, sb_out, land_out, token):
        x, y, c, chips = _place()
        for j, (cx, cy) in enumerate(chips):
            pltpu.make_async_remote_copy(src_ref=sb_ref.at[2 * cx + cy], dst_ref=land_ref.at[j], send_sem=ssem.at[j],
                                         recv_sem=rsem.at[j], device_id=(cx, cy, c), device_id_type=MESH).start()
        token[...] = jnp.zeros_like(token)

    sem3 = pltpu.SemaphoreType.DMA((3,))
    land = lax.empty((3, half, cols), BF16)
    return pl.pallas_call(
        body, name=name,
        out_shape=(sem3, sem3, pltpu.HBM(sb.shape, sb.dtype), pltpu.HBM(land.shape, land.dtype), _sds((8, 128), F32)),
        in_specs=(HBM, HBM, ANY), out_specs=(SEM, SEM, HBM, HBM, VMEM_FULL), input_output_aliases={0: 2, 1: 3},
        compiler_params=pltpu.CompilerParams(has_side_effects=EFFECT),
    )(pltpu.with_memory_space_constraint(sb, pltpu.HBM), pltpu.with_memory_space_constraint(land, pltpu.HBM), _dep(after))


def _rs_wait(sb, land, ssem, rsem, after, name):
    def body(sb_ref, land_ref, ssem_ref, rsem_ref, after_ref, sb_out, land_out):
        x, y, c, chips = _place()
        for j, (cx, cy) in enumerate(chips):
            cp = pltpu.make_async_remote_copy(src_ref=sb_ref.at[2 * cx + cy], dst_ref=land_ref.at[j], send_sem=ssem_ref.at[j],
                                              recv_sem=rsem_ref.at[j], device_id=(cx, cy, c), device_id_type=MESH)
            cp.wait_send()
            cp.wait_recv()

    return pl.pallas_call(
        body, name=name,
        out_shape=(pltpu.HBM(sb.shape, sb.dtype), pltpu.HBM(land.shape, land.dtype)),
        in_specs=(HBM, HBM, SEM, SEM, ANY), out_specs=(HBM, HBM), input_output_aliases={0: 0, 1: 1},
        compiler_params=pltpu.CompilerParams(has_side_effects=EFFECT),
    )(sb, land, ssem, rsem, after)[1]


def _pair_copy(g_ref, land_ref, ssem, rsem):
    x, y, c, _ = _place()
    h = g_ref.shape[1] // 2
    src = g_ref.at[:, pl.ds(pl.multiple_of((1 - c) * h, 8), h), :]
    return pltpu.make_async_remote_copy(src_ref=src, dst_ref=land_ref, send_sem=ssem.at[0], recv_sem=rsem.at[0],
                                        device_id=(x, y, 1 - c), device_id_type=MESH)


def _pair_start(g, name):
    def body(g_ref, land_ref, ssem, rsem, g_out, land_out, token):
        _pair_copy(g_ref, land_ref, ssem, rsem).start()
        token[...] = jnp.zeros_like(token)

    sem1 = pltpu.SemaphoreType.DMA((1,))
    land = lax.empty((N_CHIP, g.shape[1] // 2, g.shape[2]), F32)
    return pl.pallas_call(
        body, name=name,
        out_shape=(sem1, sem1, pltpu.HBM(g.shape, g.dtype), pltpu.HBM(land.shape, land.dtype), _sds((8, 128), F32)),
        in_specs=(HBM, HBM), out_specs=(SEM, SEM, HBM, HBM, VMEM_FULL), input_output_aliases={0: 2, 1: 3},
        compiler_params=pltpu.CompilerParams(has_side_effects=EFFECT),
    )(pltpu.with_memory_space_constraint(g, pltpu.HBM), pltpu.with_memory_space_constraint(land, pltpu.HBM))


def _pair_wait(g, land, ssem, rsem, after, name):
    def body(g_ref, land_ref, ssem_ref, rsem_ref, after_ref, g_out, land_out):
        cp = _pair_copy(g_ref, land_ref, ssem_ref, rsem_ref)
        cp.wait_send()
        cp.wait_recv()

    return pl.pallas_call(
        body, name=name,
        out_shape=(pltpu.HBM(g.shape, g.dtype), pltpu.HBM(land.shape, land.dtype)),
        in_specs=(HBM, HBM, SEM, SEM, ANY), out_specs=(HBM, HBM), input_output_aliases={0: 0, 1: 1},
        compiler_params=pltpu.CompilerParams(has_side_effects=EFFECT),
    )(g, land, ssem, rsem, after)


def _pair_sum(g, rcv, place, name):
    _, rows, cols = g.shape
    half = rows // 2
    tr = 256
    nt = half // tr

    def body(p_ref, g_ref, r_ref, sb_ref, sf_ref):
        s = pl.program_id(1)
        tot = g_ref[0] + r_ref[0]
        sb_ref[0] = tot.astype(BF16)

        @pl.when(s == p_ref[0])
        def _():
            sf_ref[...] = tot

    grid_spec = pltpu.PrefetchScalarGridSpec(
        num_scalar_prefetch=1, grid=(nt, N_CHIP),
        in_specs=[pl.BlockSpec((1, tr, cols), lambda t, s, p: (s, p[1] * nt + t, 0)),
                  pl.BlockSpec((1, tr, cols), lambda t, s, p: (s, t, 0))],
        out_specs=[pl.BlockSpec((1, tr, cols), lambda t, s, p: (s, t, 0)),
                   pl.BlockSpec((tr, cols), lambda t, s, p: (t, 0))])
    return pl.pallas_call(
        body, name=name, grid_spec=grid_spec,
        out_shape=[_sds((N_CHIP, half, cols), BF16), _sds((half, cols), F32)],
        compiler_params=_cp(("arbitrary", "arbitrary")),
    )(place, g, rcv)


def _final_sum(sf, rb, place, name):
    half, cols = sf.shape
    tr = 256
    nt = half // tr

    def body(p_ref, sf_ref, r_ref, out_ref):
        acc = sf_ref[...]
        for j in range(3):
            acc = acc + r_ref[j].astype(F32)
        out_ref[...] = acc

    grid_spec = pltpu.PrefetchScalarGridSpec(
        num_scalar_prefetch=1, grid=(nt,),
        in_specs=[pl.BlockSpec((tr, cols), lambda t, p: (t, 0)), pl.BlockSpec((3, tr, cols), lambda t, p: (0, t, 0))],
        out_specs=pl.BlockSpec((tr, cols), lambda t, p: (p[1] * nt + t, 0)))
    return pl.pallas_call(
        body, name=name, grid_spec=grid_spec,
        out_shape=_sds((2 * half, cols), F32),
        compiler_params=_cp(("arbitrary",)),
    )(place, sf, rb)


def _half_copy(f_ref, which, ssem, rsem):
    x, y, c, _ = _place()
    h = f_ref.shape[0] // 2
    rows = f_ref.at[pl.ds(pl.multiple_of(which * h, 8), h), :]
    return pltpu.make_async_remote_copy(src_ref=rows, dst_ref=rows, send_sem=ssem.at[0], recv_sem=rsem.at[0],
                                        device_id=(x, y, 1 - c), device_id_type=MESH)


def _half_start(full, name, after=None):
    def body(f_ref, after_ref, ssem, rsem, f_out, token):
        _half_copy(f_ref, lax.axis_index("c"), ssem, rsem).start()
        token[...] = jnp.zeros_like(token)

    sem1 = pltpu.SemaphoreType.DMA((1,))
    return pl.pallas_call(
        body, name=name,
        out_shape=(sem1, sem1, pltpu.HBM(full.shape, full.dtype), _sds((8, 128), F32)),
        in_specs=(HBM, ANY), out_specs=(SEM, SEM, HBM, VMEM_FULL), input_output_aliases={0: 2},
        compiler_params=pltpu.CompilerParams(has_side_effects=EFFECT),
    )(pltpu.with_memory_space_constraint(full, pltpu.HBM), _dep(after))


def _half_wait(full, ssem, rsem, after, name):
    def body(f_ref, ssem_ref, rsem_ref, after_ref, f_out):
        c = lax.axis_index("c")
        _half_copy(f_ref, c, ssem_ref, rsem_ref).wait_send()
        _half_copy(f_ref, 1 - c, ssem_ref, rsem_ref).wait_recv()

    return pl.pallas_call(
        body, name=name, out_shape=pltpu.HBM(full.shape, full.dtype),
        in_specs=(HBM, SEM, SEM, ANY), out_specs=HBM, input_output_aliases={0: 0},
        compiler_params=pltpu.CompilerParams(has_side_effects=EFFECT),
    )(full, ssem, rsem, after)


def _small_copies(src_ref, land_ref, ssem, rsem, first):
    x, y, c, chips = _place()
    if first:
        return [pltpu.make_async_remote_copy(src_ref=src_ref, dst_ref=land_ref, send_sem=ssem.at[0], recv_sem=rsem.at[0],
                                             device_id=(x, y, 1 - c), device_id_type=MESH)]
    return [pltpu.make_async_remote_copy(src_ref=src_ref, dst_ref=land_ref.at[j], send_sem=ssem.at[j], recv_sem=rsem.at[j],
                                         device_id=(*chip, c), device_id_type=MESH) for j, chip in enumerate(chips)]


def _small_start(src, first, name, after=None):
    n = 1 if first else 3

    def body(src_ref, land_ref, after_ref, ssem, rsem, src_out, land_out, token):
        for cp in _small_copies(src_ref, land_ref, ssem, rsem, first):
            cp.start()
        token[...] = jnp.zeros_like(token)

    sems = pltpu.SemaphoreType.DMA((n,))
    land = lax.empty(src.shape if first else (3,) + src.shape, F32)
    return pl.pallas_call(
        body, name=name,
        out_shape=(sems, sems, pltpu.HBM(src.shape, F32), pltpu.HBM(land.shape, F32), _sds((8, 128), F32)),
        in_specs=(HBM, HBM, ANY), out_specs=(SEM, SEM, HBM, HBM, VMEM_FULL), input_output_aliases={0: 2, 1: 3},
        compiler_params=pltpu.CompilerParams(has_side_effects=EFFECT),
    )(pltpu.with_memory_space_constraint(src, pltpu.HBM), pltpu.with_memory_space_constraint(land, pltpu.HBM), _dep(after))


def _small_wait(src, land, ssem, rsem, first, after, name):
    def body(src_ref, land_ref, ssem_ref, rsem_ref, after_ref, src_out, land_out):
        for cp in _small_copies(src_ref, land_ref, ssem_ref, rsem_ref, first):
            cp.wait_send()
            cp.wait_recv()

    return pl.pallas_call(
        body, name=name,
        out_shape=(pltpu.HBM(src.shape, F32), pltpu.HBM(land.shape, F32)),
        in_specs=(HBM, HBM, SEM, SEM, ANY), out_specs=(HBM, HBM), input_output_aliases={0: 0, 1: 1},
        compiler_params=pltpu.CompilerParams(has_side_effects=EFFECT),
    )(src, land, ssem, rsem, after)


def _small_pair_sum(vec, got):
    def body(v_ref, g_ref, o_ref):
        o_ref[...] = v_ref[...] + g_ref[...]

    return pl.pallas_call(body, name="small_pair_sum", in_specs=[VMEM_FULL] * 2, out_specs=VMEM_FULL,
                          out_shape=_sds(vec.shape, F32), compiler_params=_cp())(vec, got)


def _small_chip_sum(pair, got, place):
    def body(p_ref, pair_ref, got_ref, o_ref):
        acc = None
        for kk in range(N_CHIP):
            d = jnp.bitwise_xor(p_ref[0], kk)
            t = jnp.where(d == 0, pair_ref[...], jnp.where(d == 2, got_ref[0], jnp.where(d == 1, got_ref[1], got_ref[2])))
            acc = t if acc is None else acc + t
        o_ref[...] = acc

    grid_spec = pltpu.PrefetchScalarGridSpec(
        num_scalar_prefetch=1, grid=(1,),
        in_specs=[pl.BlockSpec(pair.shape, lambda i, p: (0, 0)), pl.BlockSpec(got.shape, lambda i, p: (0, 0, 0))],
        out_specs=pl.BlockSpec(pair.shape, lambda i, p: (0, 0)))
    return pl.pallas_call(body, name="small_chip_sum", grid_spec=grid_spec, out_shape=_sds(pair.shape, F32),
                          compiler_params=_cp(("arbitrary",)))(place, pair, got)


def _adam_math(w, g, m, v):
    m = B1 * m + (1.0 - B1) * g
    v = B2 * v + (1.0 - B2) * (g * g)
    m_hat = m / (1.0 - B1 ** STEP)
    v_hat = v / (1.0 - B2 ** STEP)
    delta = -LR * (m_hat / (jnp.sqrt(v_hat) + AEPS) + WD * w)
    return delta, m, v


def _adam_big(w, g, m, v, name):
    rows, cols = w.shape
    tr = 128

    def body(w_ref, g_ref, m_ref, v_ref, go_ref, d_ref, nm_ref, nv_ref):
        g = g_ref[...]
        d, nm, nv = _adam_math(w_ref[...], g, m_ref[...], v_ref[...])
        go_ref[...] = g
        d_ref[...] = d
        nm_ref[...] = nm
        nv_ref[...] = nv

    blk = pl.BlockSpec((tr, cols), lambda i: (i, 0))
    return pl.pallas_call(
        body, name=name, grid=(rows // tr,),
        in_specs=[blk] * 4, out_specs=[blk] * 4, out_shape=[_sds((rows, cols), F32)] * 4,
        compiler_params=_cp(("arbitrary",)),
    )(w, g, m, v)


def _adam_small(ws, gs, ms, vs, dep=None):
    n = len(ws)

    def body(*refs):
        for i in range(n):
            d, nm, nv = _adam_math(refs[i][...], refs[n + i][...], refs[2 * n + i][...], refs[3 * n + i][...])
            refs[4 * n + 1 + i][...] = d
            refs[5 * n + 1 + i][...] = nm
            refs[6 * n + 1 + i][...] = nv

    shapes = [_sds(w.shape, F32) for w in ws]
    outs = pl.pallas_call(
        body, name="adam_small",
        in_specs=[VMEM_FULL] * (4 * n) + [ANY], out_specs=[VMEM_FULL] * (3 * n), out_shape=shapes * 3,
        compiler_params=_cp(),
    )(*ws, *gs, *ms, *vs, _dep(dep))
    return outs[:n], outs[n:2 * n], outs[2 * n:]


def _pad_rows8(a):
    flat = a.reshape(-1, 128)
    pad = (-flat.shape[0]) % 8
    if pad:
        flat = jnp.concatenate([flat, jnp.zeros((pad, 128), F32)], axis=0)
    return flat


def kernel(x, meta_tokens, norm1_w, w_in, gate_w2, gate_b, gla_norm_w, pool_w, pool_scale, w_out, norm2_w, mlp_w1, mlp_w2, final_norm_w, loss_target, m_meta_tokens, m_norm1_w, m_w_in, m_gate_w2, m_gate_b, m_gla_norm_w, m_pool_w, m_pool_scale, m_w_out, m_norm2_w, m_mlp_w1, m_mlp_w2, m_final_norm_w, v_meta_tokens, v_norm1_w, v_w_in, v_gate_w2, v_gate_b, v_gla_norm_w, v_pool_w, v_pool_scale, v_w_out, v_norm2_w, v_mlp_w1, v_mlp_w2, v_final_norm_w):
    cx, cy, cc = lax.axis_index("x"), lax.axis_index("y"), lax.axis_index("c")
    me = (2 * cx + cy).astype(jnp.int32)

    place = jnp.stack([me, cc.astype(jnp.int32)])
    fw = final_norm_w.reshape(1, D)

    mine = jnp.concatenate([meta_tokens.reshape(64, 128), gate_w2[0], pool_w[0].reshape(512, 128)], axis=0)
    small = lax.dynamic_update_slice(jnp.zeros((N_CHIP, 592, 128), F32), mine[None], (me, 0, 0))
    (s_sm,), (r_sm,), (f_sm,), tok = _gather_start([small], "gather_start_small", [True])
    (sem_win_d,), (win,), tok = _gather_step("gather_start_win", [_cast_win(w_in[0], place, tok)], [], [(0, "d")])
    wout, w1, w2 = (_cast_into(w_out[0], place, D, "cast_wout", tok), _cast_into(mlp_w1[0], place, D, "cast_w1", tok),
                    _cast_into(mlp_w2[0], place, D, "cast_w2", tok))
    small = _gather_wait(f_sm, s_sm, r_sm, w2, "gather_wait_small", True)
    metaF = jnp.concatenate([small[k, 0:64].reshape(N_META, 512) for k in range(N_CHIP)], axis=1)
    gw2F = jnp.concatenate([small[k, 64:80] for k in range(N_CHIP)], axis=1)
    pwF = jnp.concatenate([small[k, 80:592].reshape(4, 64, GC) for k in range(N_CHIP)], axis=1)

    fly = {"win": win, "wout": wout, "w1": w1, "w2": w2}
    sems = {"win_d": sem_win_d}

    def landed(nm, after):
        _, (w,), _ = _gather_step("gather_wait_" + nm, [fly[nm]], [(0, "r")], [], [sems[nm + "_r"]], after)
        return _forward_halves(w, "forward_" + nm)

    def get_win(after):
        (sems["win_r"], sems["wout_d"], sems["w1_d"]), (fly["win"], fly["wout"], fly["w1"]), tok = _gather_step(
            "gather_relay_win", [fly["win"], fly["wout"], fly["w1"]], [(0, "d")], [(0, "r"), (1, "d"), (2, "d")],
            [sems["win_d"]], after)
        return landed("win", tok)

    def relay_mid(after):
        (sems["wout_r"], sems["w1_r"], sems["w2_d"]), (fly["wout"], fly["w1"], fly["w2"]), tok = _gather_step(
            "gather_relay_mid", [fly["wout"], fly["w1"], fly["w2"]], [(0, "d"), (1, "d")],
            [(0, "r"), (1, "r"), (2, "d")], [sems["wout_d"], sems["w1_d"]], after)
        return tok

    def get_w1(after):
        W1 = landed("w1", after)
        (sems["w2_r"],), (fly["w2"],), tok = _gather_step("gather_relay_w2", [fly["w2"]], [(0, "d")], [(0, "r")],
                                                          [sems["w2_d"]], W1)
        return W1, tok

    pairs, pending = {}, {}

    def grad_start(nm, g):
        ssem, rsem, g_thru, land, token = _pair_start(g, "pair_start_" + nm)
        pairs[nm] = (ssem, rsem, g_thru, land)
        return token

    def grad_finish(nm, after):
        ssem, rsem, g_thru, land = pairs[nm]
        g, rcv = _pair_wait(g_thru, land, ssem, rsem, after, "pair_wait_" + nm)
        sb, sf = _pair_sum(g, rcv, place, "pair_sum_" + nm)
        if nm == "win":
            pending[nm] = (sf, sb)
            return sf
        ssem, rsem, sb_thru, land, token = _rs_start(sb, "rs_start_" + nm)
        pending[nm] = (sf, ssem, rsem, sb_thru, land)
        return token

    (grad_x, loss8, d_n1w, d_gb, d_gnw, d_ps, d_n2w, d_fw, d_meta, d_gw2, d_pw) = _local_step(
        x[0], loss_target[0], get_win, relay_mid, lambda after: landed("wout", after).reshape(D, D), get_w1,
        lambda after: landed("w2", after).reshape(DFF, D), metaF, gw2F, pwF,
        norm1_w, gate_b, gla_norm_w, pool_scale, norm2_w, fw, grad_start, grad_finish, None, (m_w_in[0], v_w_in[0]))
    return _reduce_and_update(
        me, place, pending, grad_x, loss8, d_n1w, d_gb, d_gnw, d_ps, d_n2w, d_fw, d_meta, d_gw2, d_pw,
        meta_tokens, norm1_w, w_in, gate_w2, gate_b, gla_norm_w, pool_w, pool_scale, w_out, norm2_w, mlp_w1, mlp_w2, fw,
        m_meta_tokens, m_norm1_w, m_w_in, m_gate_w2, m_gate_b, m_gla_norm_w, m_pool_w, m_pool_scale, m_w_out, m_norm2_w,
        m_mlp_w1, m_mlp_w2, m_final_norm_w, v_meta_tokens, v_norm1_w, v_w_in, v_gate_w2, v_gate_b, v_gla_norm_w, v_pool_w,
        v_pool_scale, v_w_out, v_norm2_w, v_mlp_w1, v_mlp_w2, v_final_norm_w)


def _local_step(x, target, get_win, relay_mid, get_wout, get_w1, get_w2, metaF, gw2F, pwF, norm1_w, gate_b, gla_norm_w,
                pool_scale, norm2_w, fw, grad_start, grad_finish, first=None, early=()):
    h0, u = _embed_norm(x, metaF, norm1_w, first)
    Win = get_win(u)
    P = _in_proj(u, Win)
    gw2p = jnp.pad(gw2F, ((0, 128 - RANK), (0, 0)))
    yb, op = _pool_fwd(P, pwF, pool_scale)
    o, og, sp = _gla_fwd(P, gw2p, gate_b, gla_norm_w, relay_mid(op))
    Wout = get_wout(og)
    h1 = _out_proj(og, op, Wout, h0)
    n2 = _norm_rows(h1, norm2_w, "norm2")
    W1, tok = get_w1(n2)
    zr, a = _mlp_up(n2, W1, tuple(early) + (() if tok is None else (tok,)))
    W2 = get_w2(a)
    h2 = _mlp_down(a, W2, h1)

    dh2, dh2b, d_fw, loss8 = _loss_head(h2, target, fw)
    tok = grad_start("w2", _grad_w2(a, dh2b).reshape(N_CHIP, D, D))
    dz = _mlp_dz(dh2b, W2, zr, tok)
    tok = grad_finish("w2", dz)
    tok = grad_start("w1", _grad_w1(n2, dz, tok))
    dn2 = _mlp_dn(dz, W1, tok)
    tok = grad_finish("w1", dn2)
    dh1, dh1b, d_n2w = _norm_bwd(dn2, h1, dh2, norm2_w, "norm2_bwd", tok)
    dmixed = _mixed_grad(dh1b, Wout)
    tok = grad_start("wout", _grad_wout(og, op, dh1b))
    dpu, d_pw, d_ps = _pool_bwd(dmixed, yb, pwF, pool_scale, tok)
    dq, dk, dv, dr, dglr, d_gw2p, d_gb, d_gnw = _gla_bwd(dmixed, o, P, gw2p, gate_b, gla_norm_w, sp, tok)
    d_gw2 = d_gw2p[0:RANK]
    tok = grad_finish("wout", dq)
    tok = grad_start("win", _grad_win(u, dq, dk, dv, dr, dglr, dpu, tok))
    du = _in_grad(dq, dk, dv, dr, dglr, dpu, Win, tok)
    tok = grad_finish("win", du)
    grad_x, d_meta, d_n1w = _input_grad(du, h0, dh1, norm1_w, tok)
    return grad_x, loss8, d_n1w, d_gb, d_gnw, d_ps, d_n2w, d_fw, d_meta, d_gw2, d_pw


def _reduce_and_update(me, place, pending, grad_x, loss8, d_n1w, d_gb, d_gnw, d_ps, d_n2w, d_fw, d_meta, d_gw2, d_pw,
                       meta_tokens, norm1_w, w_in, gate_w2, gate_b, gla_norm_w, pool_w, pool_scale, w_out, norm2_w,
                       mlp_w1, mlp_w2, fw, m_meta_tokens, m_norm1_w, m_w_in, m_gate_w2, m_gate_b, m_gla_norm_w, m_pool_w,
                       m_pool_scale, m_w_out, m_norm2_w, m_mlp_w1, m_mlp_w2, m_final_norm_w, v_meta_tokens, v_norm1_w, v_w_in,
                       v_gate_w2, v_gate_b, v_gla_norm_w, v_pool_w, v_pool_scale, v_w_out, v_norm2_w, v_mlp_w1, v_mlp_w2,
                       v_final_norm_w):
    parts = [loss8, d_n1w, d_gb, d_gnw, d_ps, d_n2w, d_fw, d_meta, d_gw2, d_pw]
    packed = [_pad_rows8(p) for p in parts]
    sizes = [p.shape[0] for p in packed]
    vec = jnp.concatenate(packed, axis=0)

    big, halves = {}, {}
    params = {"w2": (mlp_w2, m_mlp_w2, v_mlp_w2), "w1": (mlp_w1, m_mlp_w1, v_mlp_w1), "wout": (w_out, m_w_out, v_w_out),
              "win": (w_in, m_w_in, v_w_in)}

    def reduce_(nm, after):
        sf, ssem, rsem, sb_thru, land = pending[nm]
        rb = _rs_wait(sb_thru, land, ssem, rsem, after, "rs_wait_" + nm)
        hs, hr, full, token = _half_start(_final_sum(sf, rb, place, "final_sum_" + nm), "half_start_" + nm)
        halves[nm] = (hs, hr, full)
        return token

    def update(nm, after):
        hs, hr, full = halves[nm]
        w, m, v = params[nm]
        big[nm] = _adam_big(w[0], _half_wait(full, hs, hr, after, "half_wait_" + nm), m[0], v[0], "adam_" + nm)
        return big[nm][3]

    s1, r1, vec, land1, tok = _small_start(vec, True, "small_start_pair")
    tok = reduce_("w2", tok)
    vec, got = _small_wait(vec, land1, s1, r1, True, tok, "small_wait_pair")
    pair = _small_pair_sum(vec, got)
    s2, r2, pair, land2, tok = _small_start(pair, False, "small_start_chips")
    sf, sb = pending["win"]
    ssem, rsem, sb_thru, land, tok = _rs_start(sb, "rs_start_win", tok)
    pending["win"] = (sf, ssem, rsem, sb_thru, land)
    tok = reduce_("w1", tok)
    tok = update("w2", tok)
    tok = update("w1", tok)
    pair, got = _small_wait(pair, land2, s2, r2, False, tok, "small_wait_chips")
    red = _small_chip_sum(pair, got, place)
    tok = reduce_("wout", red)
    tok = update("wout", tok)
    tok = reduce_("win", tok)
    after = update("win", tok)
    offs = [0]
    for s in sizes:
        offs.append(offs[-1] + s)

    def take(i, shape):
        n = 1
        for d in shape:
            n *= d
        return red[offs[i]:offs[i] + n // 128].reshape(shape)

    loss = red[0, 0]
    G_n1w = take(1, (1, D))
    G_gb = take(2, (1, KW))
    G_gnw = take(3, (1, DV))
    G_ps = take(4, (1, PW))
    G_n2w = take(5, (1, D))
    G_fw = take(6, (1, D))
    G_meta = lax.dynamic_slice(take(7, (N_META, D)), (0, me * 512), (N_META, 512))
    G_gw2 = lax.dynamic_slice(take(8, (RANK, KW)), (0, me * 128), (RANK, 128))
    G_pw = lax.dynamic_slice(take(9, (4, GC, GC)), (0, me * 64, 0), (4, 64, GC))

    G_win, d_win, nm_win, nv_win = big["win"]
    G_wout, d_wout, nm_wout, nv_wout = big["wout"]
    G_w1, d_w1, nm_w1, nv_w1 = big["w1"]
    G_w2, d_w2, nm_w2, nv_w2 = big["w2"]
    ws = [meta_tokens, norm1_w, gate_w2[0], gate_b, gla_norm_w, pool_w[0], pool_scale, norm2_w, fw]
    gs = [G_meta, G_n1w, G_gw2, G_gb, G_gnw, G_pw, G_ps, G_n2w, G_fw]
    ms = [m_meta_tokens, m_norm1_w, m_gate_w2[0], m_gate_b, m_gla_norm_w, m_pool_w[0], m_pool_scale, m_norm2_w,
          m_final_norm_w.reshape(1, D)]
    vs = [v_meta_tokens, v_norm1_w, v_gate_w2[0], v_gate_b, v_gla_norm_w, v_pool_w[0], v_pool_scale, v_norm2_w,
          v_final_norm_w.reshape(1, D)]
    ds, nms, nvs = _adam_small(ws, gs, ms, vs, after)

    def assemble(small, win_, wout_, w1_, w2_):
        meta_, n1_, gw2_, gb_, gnw_, pw_, ps_, n2_, fw_ = small
        return (meta_, n1_, win_[None], gw2_[None], gb_, gnw_, pw_[None], ps_, wout_[None], n2_, w1_[None], w2_[None],
                fw_.reshape(D))

    grads_out = assemble(gs, G_win, G_wout, G_w1, G_w2)
    deltas = assemble(ds, d_win, d_wout, d_w1, d_w2)
    new_m = assemble(nms, nm_win, nm_wout, nm_w1, nm_w2)
    new_v = assemble(nvs, nv_win, nv_wout, nv_w1, nv_w2)
    return (loss, grad_x[None], *grads_out, *deltas, *new_m, *new_v)
```

```python
import functools

import jax
import jax.numpy as jnp
from jax import lax
from jax.experimental import pallas as pl
from jax.experimental.pallas import tpu as pltpu

F32 = jnp.float32
BF16 = jnp.bfloat16

D = 2048
SEQ = 2048
N_META = 16
CH = 64
TP = 2176
NCH = TP // CH
ROW_LO = 112
X_LO = 128
ROW_HI = TP
XT = 128
NXT = TP // XT
HEADS = 4
DK = 128
DV = 256
KW = HEADS * DK
GW = HEADS * DV
RANK = 16
TAU = 16.0
WINDOWS = (2, 4, 8, 16)
PW = 1024
GC = 256
DFF = 8192
EPS = 1e-6
SHARD_IN = 1028
PAD_IN = 1152
N_CHIP = 4

LR = 0.001
B1 = 0.9
B2 = 0.999
AEPS = 1e-08
WD = 0.01
STEP = 10

VMEM_LIMIT = 60 * 1024 * 1024
ANY = pl.BlockSpec(memory_space=pl.ANY)
VMEM_FULL = pl.BlockSpec(memory_space=pltpu.VMEM)
MESH = pl.DeviceIdType.MESH


def _cp(sem=None):
    if sem is None:
        return pltpu.CompilerParams(vmem_limit_bytes=VMEM_LIMIT)
    return pltpu.CompilerParams(dimension_semantics=sem, vmem_limit_bytes=VMEM_LIMIT)


def _dot(a, b):
    return jnp.dot(a, b, preferred_element_type=F32)


def _dot_nt(a, b):
    return lax.dot_general(a, b, (((1,), (1,)), ((), ())), preferred_element_type=F32)


def _dot_tn(a, b):
    return lax.dot_general(a, b, (((0,), (0,)), ((), ())), preferred_element_type=F32)


def _sds(shape, dtype):
    return jax.ShapeDtypeStruct(shape, dtype)


def _embed_norm(x, meta_full, w, dep=None):
    def body(x_ref, meta_ref, w_ref, dep_ref, h_ref, u_ref):
        i = pl.program_id(0)

        @pl.when(i == 0)
        def _():
            h_ref[...] = jnp.zeros_like(h_ref)
            h_ref[ROW_LO:X_LO, :] = meta_ref[...]

        @pl.when(i >= 1)
        def _():
            h_ref[...] = x_ref[...]

        h = h_ref[...]
        r = lax.rsqrt(jnp.mean(h * h, axis=-1, keepdims=True) + EPS)
        u_ref[...] = ((h * r) * w_ref[...]).astype(BF16)

    return pl.pallas_call(
        body, name="embed_norm1", grid=(NXT,),
        in_specs=[pl.BlockSpec((XT, D), lambda i: (jnp.maximum(i - 1, 0), 0)),
                  pl.BlockSpec((N_META, D), lambda i: (0, 0)),
                  pl.BlockSpec((1, D), lambda i: (0, 0)), ANY],
        out_specs=[pl.BlockSpec((XT, D), lambda i: (i, 0)), pl.BlockSpec((XT, D), lambda i: (i, 0))],
        out_shape=[_sds((TP, D), F32), _sds((TP, D), BF16)],
        compiler_params=_cp(("arbitrary",)),
    )(x, meta_full, w, _dep(dep))


def _norm_rows(h, w, name):
    tr = 272

    def body(h_ref, w_ref, o_ref):
        hv = h_ref[...]
        r = lax.rsqrt(jnp.mean(hv * hv, axis=-1, keepdims=True) + EPS)
        o_ref[...] = ((hv * r) * w_ref[...]).astype(BF16)

    return pl.pallas_call(
        body, name=name, grid=(TP // tr,),
        in_specs=[pl.BlockSpec((tr, D), lambda i: (i, 0)), pl.BlockSpec((1, D), lambda i: (0, 0))],
        out_specs=pl.BlockSpec((tr, D), lambda i: (i, 0)),
        out_shape=_sds((TP, D), BF16),
        compiler_params=_cp(("arbitrary",)),
    )(h, w)


def _loss_head(h2, target, fw):
    def body(h_ref, t_ref, w_ref, dh_ref, dhb_ref, dw_ref, loss_ref):
        i = pl.program_id(0)

        @pl.when(i == 0)
        def _():
            dw_ref[...] = jnp.zeros_like(dw_ref)
            loss_ref[...] = jnp.zeros_like(loss_ref)

        h = h_ref[...]
        w = w_ref[...]
        r = lax.rsqrt(jnp.mean(h * h, axis=-1, keepdims=True) + EPS)
        xh = h * r
        y = xh * w
        is_x = (i >= 1).astype(F32)
        diff = (y - t_ref[...]) * is_x
        loss_ref[...] += jnp.sum(diff * diff) * (0.5 / D)
        dy = diff * (1.0 / D)
        dw_ref[...] += jnp.sum(dy * xh, axis=0, keepdims=True)
        gx = dy * w
        dh = r * (gx - xh * jnp.mean(gx * xh, axis=-1, keepdims=True))
        dh_ref[...] = dh
        dhb_ref[...] = dh.astype(BF16)

    return pl.pallas_call(
        body, name="loss_head", grid=(NXT,),
        in_specs=[pl.BlockSpec((XT, D), lambda i: (i, 0)),
                  pl.BlockSpec((XT, D), lambda i: (jnp.maximum(i - 1, 0), 0)),
                  pl.BlockSpec((1, D), lambda i: (0, 0))],
        out_specs=[pl.BlockSpec((XT, D), lambda i: (i, 0)), pl.BlockSpec((XT, D), lambda i: (i, 0)),
                   pl.BlockSpec((1, D), lambda i: (0, 0)), pl.BlockSpec((8, 128), lambda i: (0, 0))],
        out_shape=[_sds((TP, D), F32), _sds((TP, D), BF16), _sds((1, D), F32), _sds((8, 128), F32)],
        compiler_params=_cp(("arbitrary",)),
    )(h2, target, fw)


def _norm_bwd(dn, h, dres, w, name, dep=None):
    tr = 272

    def body(dn_ref, h_ref, dres_ref, w_ref, dep_ref, o_ref, ob_ref, dw_ref):
        @pl.when(pl.program_id(0) == 0)
        def _():
            dw_ref[...] = jnp.zeros_like(dw_ref)

        hv = h_ref[...]
        dnv = dn_ref[...]
        r = lax.rsqrt(jnp.mean(hv * hv, axis=-1, keepdims=True) + EPS)
        xh = hv * r
        dw_ref[...] += jnp.sum(dnv * xh, axis=0, keepdims=True)
        gx = dnv * w_ref[...]
        dh = dres_ref[...] + r * (gx - xh * jnp.mean(gx * xh, axis=-1, keepdims=True))
        o_ref[...] = dh
        ob_ref[...] = dh.astype(BF16)

    row = pl.BlockSpec((tr, D), lambda i: (i, 0))
    vec = pl.BlockSpec((1, D), lambda i: (0, 0))
    return pl.pallas_call(
        body, name=name, grid=(TP // tr,),
        in_specs=[row, row, row, vec, ANY], out_specs=[row, row, vec],
        out_shape=[_sds((TP, D), F32), _sds((TP, D), BF16), _sds((1, D), F32)],
        compiler_params=_cp(("arbitrary",)),
    )(dn, h, dres, w, _dep(dep))


def _input_grad(du, h0, dh1, w, dep=None):
    def body(du_ref, h_ref, dres_ref, w_ref, dep_ref, gx_ref, gm_ref, dw_ref):
        i = pl.program_id(0)

        @pl.when(i == 0)
        def _():
            dw_ref[...] = jnp.zeros_like(dw_ref)

        hv = h_ref[...]
        dnv = du_ref[...]
        r = lax.rsqrt(jnp.mean(hv * hv, axis=-1, keepdims=True) + EPS)
        xh = hv * r
        dw_ref[...] += jnp.sum(dnv * xh, axis=0, keepdims=True)
        g = dnv * w_ref[...]
        dh = dres_ref[...] + r * (g - xh * jnp.mean(g * xh, axis=-1, keepdims=True))

        @pl.when(i == 0)
        def _():
            gm_ref[...] = dh[ROW_LO:X_LO, :]

        @pl.when(i >= 1)
        def _():
            gx_ref[...] = dh

    row = pl.BlockSpec((XT, D), lambda i: (i, 0))
    vec = pl.BlockSpec((1, D), lambda i: (0, 0))
    return pl.pallas_call(
        body, name="input_grad", grid=(NXT,),
        in_specs=[row, row, row, vec, ANY],
        out_specs=[pl.BlockSpec((XT, D), lambda i: (jnp.maximum(i - 1, 0), 0)),
                   pl.BlockSpec((N_META, D), lambda i: (0, 0)), vec],
        out_shape=[_sds((SEQ, D), F32), _sds((N_META, D), F32), _sds((1, D), F32)],
        compiler_params=_cp(("arbitrary",)),
    )(du, h0, dh1, w, _dep(dep))


def _in_proj(u, wg):
    def body(u_ref, w_ref, o_ref):
        o_ref[0] = _dot(u_ref[...], w_ref[0])

    return pl.pallas_call(
        body, name="in_proj", grid=(N_CHIP,),
        in_specs=[VMEM_FULL, pl.BlockSpec((1, D, PAD_IN), lambda k: (k, 0, 0))],
        out_specs=pl.BlockSpec((1, TP, PAD_IN), lambda k: (k, 0, 0)),
        out_shape=_sds((N_CHIP, TP, PAD_IN), F32),
        compiler_params=_cp(("arbitrary",)),
    )(u, wg)


def _out_proj(og, op, wout, h0, dep=None):
    tn = 512

    def body(og_ref, op_ref, w_ref, h_ref, dep_ref, o_ref):
        acc = _dot(og_ref[...], w_ref[0:GW, :]) + _dot(op_ref[...], w_ref[GW:D, :])
        o_ref[...] = h_ref[...] + acc

    return pl.pallas_call(
        body, name="out_proj", grid=(D // tn,),
        in_specs=[VMEM_FULL, VMEM_FULL, pl.BlockSpec((D, tn), lambda j: (0, j)),
                  pl.BlockSpec((TP, tn), lambda j: (0, j)), ANY],
        out_specs=pl.BlockSpec((TP, tn), lambda j: (0, j)),
        out_shape=_sds((TP, D), F32),
        compiler_params=_cp(("arbitrary",)),
    )(og, op, wout, h0, _dep(dep))


def _mlp_up(n2, w1g, early=()):
    tn = 1024
    per = D // tn
    ne = len(early)

    def body(n_ref, w_ref, *rest):
        zr_ref, a_ref = rest[ne:]
        z = jnp.maximum(_dot(n_ref[...], w_ref[0]), 0.0)
        zr_ref[...] = z.astype(BF16)
        a_ref[...] = (z * z).astype(BF16)

    col = pl.BlockSpec((TP, tn), lambda k, j: (0, k * per + j))
    return pl.pallas_call(
        body, name="mlp_up", grid=(N_CHIP, per),
        in_specs=[VMEM_FULL, pl.BlockSpec((1, D, tn), lambda k, j: (k, 0, j))] + [ANY] * ne,
        out_specs=[col, col],
        out_shape=[_sds((TP, DFF), BF16), _sds((TP, DFF), BF16)],
        compiler_params=_cp(("arbitrary", "arbitrary")),
    )(n2, w1g, *early)


def _mlp_down(a, w2, h1):
    tk = 1024
    nk = DFF // tk

    def body(a_ref, w_ref, h_ref, o_ref, acc_ref):
        k = pl.program_id(0)

        @pl.when(k == 0)
        def _():
            pltpu.sync_copy(h_ref, acc_ref)

        acc_ref[...] += _dot(a_ref[...], w_ref[...])

        @pl.when(k == nk - 1)
        def _():
            pltpu.sync_copy(acc_ref, o_ref)

    return pl.pallas_call(
        body, name="mlp_down", grid=(nk,),
        in_specs=[pl.BlockSpec((TP, tk), lambda k: (0, k)), pl.BlockSpec((tk, D), lambda k: (k, 0)), ANY],
        out_specs=ANY,
        out_shape=_sds((TP, D), F32),
        scratch_shapes=[pltpu.VMEM((TP, D), F32)],
        compiler_params=_cp(("arbitrary",)),
    )(a, w2, h1)


def _mlp_dz(dh2b, w2, zr, dep=None):
    tn = 1024

    def body(d_ref, w_ref, z_ref, dep_ref, o_ref):
        da = _dot_nt(d_ref[...], w_ref[...])
        o_ref[...] = (da * (2.0 * z_ref[...].astype(F32))).astype(BF16)

    col = pl.BlockSpec((TP, tn), lambda j: (0, j))
    return pl.pallas_call(
        body, name="mlp_dz", grid=(DFF // tn,),
        in_specs=[VMEM_FULL, pl.BlockSpec((tn, D), lambda j: (j, 0)), col, ANY],
        out_specs=col,
        out_shape=_sds((TP, DFF), BF16),
        compiler_params=_cp(("arbitrary",)),
    )(dh2b, w2, zr, _dep(dep))


def _grad_w2(a, dh2b):
    tm = 1024

    def body(a_ref, d_ref, o_ref):
        o_ref[...] = _dot_tn(a_ref[...], d_ref[...])

    return pl.pallas_call(
        body, name="grad_w2", grid=(DFF // tm,),
        in_specs=[pl.BlockSpec((TP, tm), lambda j: (0, j)), VMEM_FULL],
        out_specs=pl.BlockSpec((tm, D), lambda j: (j, 0)),
        out_shape=_sds((DFF, D), F32),
        compiler_params=_cp(("arbitrary",)),
    )(a, dh2b)


def _dep(token):
    return jnp.zeros((8, 128), F32) if token is None else token


def _grad_w1(n2, dz, dep=None):
    tn = 1024
    per = D // tn

    def body(n_ref, d_ref, dep_ref, o_ref):
        o_ref[0] = _dot_tn(n_ref[...], d_ref[...])

    return pl.pallas_call(
        body, name="grad_w1", grid=(N_CHIP, per),
        in_specs=[VMEM_FULL, pl.BlockSpec((TP, tn), lambda k, j: (0, k * per + j)), ANY],
        out_specs=pl.BlockSpec((1, D, tn), lambda k, j: (k, 0, j)),
        out_shape=_sds((N_CHIP, D, D), F32),
        compiler_params=_cp(("arbitrary", "arbitrary")),
    )(n2, dz, _dep(dep))


def _mlp_dn(dz, w1g, dep=None):
    tk = 1024
    per = D // tk
    nk = DFF // tk

    def body(d_ref, w_ref, dep_ref, o_ref, acc_ref):
        k = pl.program_id(0)
        part = _dot_nt(d_ref[...], w_ref[0])

        @pl.when(k == 0)
        def _():
            acc_ref[...] = part

        @pl.when(k > 0)
        def _():
            acc_ref[...] += part

        @pl.when(k == nk - 1)
        def _():
            pltpu.sync_copy(acc_ref, o_ref)

    return pl.pallas_call(
        body, name="mlp_dn", grid=(nk,),
        in_specs=[pl.BlockSpec((TP, tk), lambda k: (0, k)),
                  pl.BlockSpec((1, D, tk), lambda k: (k // per, 0, k % per)), ANY],
        out_specs=ANY,
        out_shape=_sds((TP, D), F32),
        scratch_shapes=[pltpu.VMEM((TP, D), F32)],
        compiler_params=_cp(("arbitrary",)),
    )(dz, w1g, _dep(dep))


def _mixed_grad(dh1b, wout):
    tn = 512

    def body(d_ref, w_ref, o_ref):
        o_ref[...] = _dot_nt(d_ref[...], w_ref[...])

    return pl.pallas_call(
        body, name="mixed_grad", grid=(D // tn,),
        in_specs=[VMEM_FULL, pl.BlockSpec((tn, D), lambda j: (j, 0))],
        out_specs=pl.BlockSpec((TP, tn), lambda j: (0, j)),
        out_shape=_sds((TP, D), F32),
        compiler_params=_cp(("arbitrary",)),
    )(dh1b, wout)


def _grad_wout(og, op, dh1b):
    tm = 512

    def body(og_ref, op_ref, d_ref, o_ref):
        j = pl.program_id(0)

        @pl.when(j < 2)
        def _():
            o_ref[0] = _dot_tn(og_ref[...], d_ref[...])

        @pl.when(j >= 2)
        def _():
            o_ref[0] = _dot_tn(op_ref[...], d_ref[...])

    return pl.pallas_call(
        body, name="grad_wout", grid=(N_CHIP,),
        in_specs=[pl.BlockSpec((TP, tm), lambda j: (0, jnp.minimum(j, 1))),
                  pl.BlockSpec((TP, tm), lambda j: (0, jnp.maximum(j - 2, 0))), VMEM_FULL],
        out_specs=pl.BlockSpec((1, tm, D), lambda j: (j, 0, 0)),
        out_shape=_sds((N_CHIP, tm, D), F32),
        compiler_params=_cp(("arbitrary",)),
    )(og, op, dh1b)


def _in_grad(dq, dk, dv, dr, dglr, dpu, wg, dep=None):
    def body(dq_ref, dk_ref, dv_ref, dr_ref, dg_ref, dpu_ref, w_ref, dep_ref, o_ref):
        dv, dr, dg = dv_ref[...], dr_ref[...], dg_ref[...]
        head, tail = slice(0, GW), slice(GW, PAD_IN)
        o_ref[...] = (_dot_nt(dq_ref[...], w_ref[0, :, 0:KW]) + _dot_nt(dk_ref[...], w_ref[0, :, KW:GW])
                      + _dot_nt(dv[:, 0:128], w_ref[0, :, tail])
                      + _dot_nt(dv, w_ref[1, :, head]) + _dot_nt(dr[:, 0:128], w_ref[1, :, tail])
                      + _dot_nt(dr, w_ref[2, :, head]) + _dot_nt(dg, w_ref[2, :, tail])
                      + _dot_nt(dpu_ref[...], w_ref[3, :, head]) + _dot_nt(dg, w_ref[3, :, tail]))

    tn = 512
    return pl.pallas_call(
        body, name="in_grad", grid=(D // tn,),
        in_specs=[VMEM_FULL] * 6 + [pl.BlockSpec((N_CHIP, tn, PAD_IN), lambda j: (0, j, 0)), ANY],
        out_specs=pl.BlockSpec((TP, tn), lambda j: (0, j)),
        out_shape=_sds((TP, D), F32),
        compiler_params=_cp(("arbitrary",)),
    )(dq, dk, dv, dr, dglr, dpu, wg, _dep(dep))


def _grad_win(u, dq, dk, dv, dr, dglr, dpu, dep=None):
    tm = 512

    def body(u_ref, dq_hbm, dk_hbm, dv_hbm, dr_hbm, dg_hbm, dpu_hbm, dep_ref, o_ref, dp_ref, sem):
        k, m = pl.program_id(0), pl.program_id(1)
        head, tail = slice(0, GW), slice(GW, PAD_IN)
        pieces = [[(dq_hbm, slice(0, KW)), (dk_hbm, slice(KW, GW)), (dv_hbm.at[:, 0:128], tail)],
                  [(dv_hbm, head), (dr_hbm.at[:, 0:128], tail)],
                  [(dr_hbm, head), (dg_hbm, tail)],
                  [(dpu_hbm, head), (dg_hbm, tail)]]

        def copies(kk):
            return [pltpu.make_async_copy(src, dp_ref.at[kk % 2, :, cols], sem.at[kk % 2, i])
                    for i, (src, cols) in enumerate(pieces[kk])]

        @pl.when((k == 0) & (m == 0))
        def _():
            for cp in copies(0):
                cp.start()

        for kk in range(N_CHIP):
            @pl.when((k == kk) & (m == 0))
            def _(kk=kk):
                for cp in copies(kk):
                    cp.wait()
                if kk + 1 < N_CHIP:
                    for cp in copies(kk + 1):
                        cp.start()

        g = _dot_tn(u_ref[...], dp_ref[k % 2])
        lane = lax.broadcasted_iota(jnp.int32, (tm, PAD_IN), 1)
        for kk in range(N_CHIP):
            @pl.when(k == kk)
            def _(kk=kk):
                if kk == 0:
                    nat = g
                elif kk < 3:
                    nat = pltpu.roll(g, PAD_IN - 4 * kk, 1)
                else:
                    nat = jnp.where(lane < 4, pltpu.roll(g, PAD_IN - (GW + 12), 1), pltpu.roll(g, 4, 1))
                o_ref[0] = nat[:, 0:SHARD_IN]

    return pl.pallas_call(
        body, name="grad_win", grid=(N_CHIP, D // tm),
        in_specs=[pl.BlockSpec((TP, tm), lambda k, m: (0, m))] + [ANY] * 7,
        out_specs=pl.BlockSpec((1, tm, SHARD_IN), lambda k, m: (k, m, 0)),
        out_shape=_sds((N_CHIP, D, SHARD_IN), F32),
        scratch_shapes=[pltpu.VMEM((2, TP, PAD_IN), BF16), pltpu.SemaphoreType.DMA((2, 3))],
        compiler_params=_cp(("arbitrary", "arbitrary")),
    )(u, dq, dk, dv, dr, dglr, dpu, _dep(dep))


def _split3(x):
    hi = x.astype(BF16)
    r1 = x - hi.astype(F32)
    mid = r1.astype(BF16)
    lo = (r1 - mid.astype(F32)).astype(BF16)
    return hi, mid, lo


def _tri_sum(tri, x):
    hi, mid, lo = _split3(x)
    return _dot(tri, hi) + _dot(tri, mid) + _dot(tri, lo)


def _gla_common(n, glr, gw2, gb):
    rows = n * CH + lax.broadcasted_iota(jnp.int32, (CH, 1), 0)
    valid = (rows >= ROW_LO) & (rows < ROW_HI)
    g_raw = _dot(glr.astype(BF16), gw2.astype(BF16)) + gb
    logsig = jnp.minimum(g_raw, 0.0) - jnp.log(1.0 + jnp.exp(-jnp.abs(g_raw)))
    logg = jnp.where(valid, logsig * (1.0 / TAU), 0.0)
    ci = lax.broadcasted_iota(jnp.int32, (CH, CH), 0)
    si = lax.broadcasted_iota(jnp.int32, (CH, CH), 1)
    lower = ci >= si
    G = _tri_sum(lower.astype(BF16), logg)
    Gl = G[CH - 1:CH, :]
    return valid, g_raw, lower, G, Gl


def _p_specs(index):
    def spec(width, shard, col):
        return pl.BlockSpec((1, CH, width), lambda s: (shard, index(s), col))

    return [spec(KW, 0, 0), spec(KW, 0, 1), spec(GW, 1, 0), spec(128, 0, 8), spec(GW, 2, 0), spec(128, 1, 8),
            spec(128, 2, 8), spec(128, 3, 8)]


def _p_load(q_ref, k_ref, vm_ref, vh_ref, rm_ref, rh_ref, ga_ref, gb_ref):
    def joined(main, head):
        return jnp.concatenate([main[:, 0:128] + head, main[:, 128:]], axis=1)

    return q_ref[0], k_ref[0], joined(vm_ref[0], vh_ref[0]), joined(rm_ref[0], rh_ref[0]), ga_ref[0] + gb_ref[0]


def _gla_fwd(P, gw2, gb, gnw, dep=None):
    scale = DK ** -0.5

    def body(p0, p1, p2, p3, p4, p5, p6, p7, gw2_ref, gb_ref, gnw_ref, dep_ref, o_ref, og_ref, sp_ref, st_ref):
        n = pl.program_id(0)

        @pl.when(n == 0)
        def _():
            st_ref[...] = jnp.zeros_like(st_ref)

        q_all, k_all, v_all, r_all, glr = _p_load(p0, p1, p2, p3, p4, p5, p6, p7)
        _, _, lower, G, Gl = _gla_common(n, glr, gw2_ref[...], gb_ref[...])
        eG = jnp.exp(G)
        eN = jnp.exp(-G)
        eE = jnp.exp(Gl - G)
        dec = jnp.exp(Gl)
        gnw_v = gnw_ref[...]
        for h in range(HEADS):
            ks = slice(h * DK, (h + 1) * DK)
            vs = slice(h * DV, (h + 1) * DV)
            kh = k_all[:, ks]
            vh = v_all[:, vs].astype(BF16)
            qd = ((q_all[:, ks] * scale) * eG[:, ks]).astype(BF16)
            ki = (kh * eN[:, ks]).astype(BF16)
            ke = (kh * eE[:, ks]).astype(BF16)
            st = st_ref[h]
            a = jnp.where(lower, _dot_nt(qd, ki), 0.0).astype(BF16)
            o = _dot(a, vh) + _dot_nt(qd, st.astype(BF16))
            sp_ref[0, h] = st
            st_ref[h] = st * dec[:, ks] + _dot_tn(vh, ke)
            o_ref[:, vs] = o
            rs = lax.rsqrt(jnp.mean(o * o, axis=-1, keepdims=True) + EPS)
            rv = r_all[:, vs]
            gate = rv / (1.0 + jnp.exp(-rv))
            og_ref[:, vs] = (((o * rs) * gnw_v) * gate).astype(BF16)

    rv_ = pl.BlockSpec((CH, GW), lambda n: (n, 0))

    def full(shape):
        return pl.BlockSpec(shape, lambda n: tuple(0 for _ in shape))

    return pl.pallas_call(
        body, name="gla_fwd", grid=(NCH,),
        in_specs=_p_specs(lambda n: n) + [full((128, KW)), full((1, KW)), full((1, DV)), ANY],
        out_specs=[rv_, rv_, pl.BlockSpec((1, HEADS, DV, DK), lambda n: (n, 0, 0, 0))],
        out_shape=[_sds((TP, GW), F32), _sds((TP, GW), BF16), _sds((NCH, HEADS, DV, DK), F32)],
        scratch_shapes=[pltpu.VMEM((HEADS, DV, DK), F32)],
        compiler_params=_cp(("arbitrary",)),
    )(*([P] * 8), gw2, gb, gnw, _dep(dep))


def _gla_bwd(dog, o, P, gw2, gb, gnw, sp, dep=None):
    scale = DK ** -0.5

    def body(dog_ref, o_ref, p0, p1, p2, p3, p4, p5, p6, p7, gw2_ref, gb_ref, gnw_ref, sp_ref, dep_ref,
             dq_ref, dk_ref, dv_ref, dr_ref, dglr_ref, dgw2_ref, dgb_ref, dgnw_ref, ds_ref):
        step = pl.program_id(0)
        n = NCH - 1 - step

        @pl.when(step == 0)
        def _():
            ds_ref[...] = jnp.zeros_like(ds_ref)
            dgw2_ref[...] = jnp.zeros_like(dgw2_ref)
            dgb_ref[...] = jnp.zeros_like(dgb_ref)
            dgnw_ref[...] = jnp.zeros_like(dgnw_ref)

        q_all, k_all, v_all, r_all, glr_v = _p_load(p0, p1, p2, p3, p4, p5, p6, p7)
        gw2_b = gw2_ref[...].astype(BF16)
        valid, g_raw, lower, G, Gl = _gla_common(n, glr_v, gw2_ref[...], gb_ref[...])
        upper = lax.broadcasted_iota(jnp.int32, (CH, CH), 0) <= lax.broadcasted_iota(jnp.int32, (CH, CH), 1)
        eG = jnp.exp(G)
        eN = jnp.exp(-G)
        eE = jnp.exp(Gl - G)
        dec = jnp.exp(Gl)
        gnw_v = gnw_ref[...]
        last = lax.broadcasted_iota(jnp.int32, (CH, 1), 0) == CH - 1
        dgnw_acc = jnp.zeros((1, DV), F32)
        dG_parts = []
        for h in range(HEADS):
            ks = slice(h * DK, (h + 1) * DK)
            vs = slice(h * DV, (h + 1) * DV)
            oh = o_ref[:, vs]
            rv = r_all[:, vs]
            dg = dog_ref[:, vs]
            sig = 1.0 / (1.0 + jnp.exp(-rv))
            gate = rv * sig
            rs = lax.rsqrt(jnp.mean(oh * oh, axis=-1, keepdims=True) + EPS)
            ohat = oh * rs
            dr_ref[:, vs] = ((dg * (ohat * gnw_v)) * (sig * (1.0 + rv * (1.0 - sig)))).astype(BF16)
            don = dg * gate
            dgnw_acc = dgnw_acc + jnp.sum(don * ohat, axis=0, keepdims=True)
            gxn = don * gnw_v
            do = (rs * (gxn - ohat * jnp.mean(gxn * ohat, axis=-1, keepdims=True))).astype(BF16)
            kh = k_all[:, ks]
            vh = v_all[:, vs].astype(BF16)
            qd_f = (q_all[:, ks] * scale) * eG[:, ks]
            ki_f = kh * eN[:, ks]
            ke_f = kh * eE[:, ks]
            qd, ki, ke = qd_f.astype(BF16), ki_f.astype(BF16), ke_f.astype(BF16)
            spt = sp_ref[0, h]
            dst = ds_ref[h]
            dst_b = dst.astype(BF16)
            a_t = jnp.where(upper, _dot_nt(ki, qd), 0.0).astype(BF16)
            da = jnp.where(lower, _dot_nt(do, vh), 0.0).astype(BF16)
            da_t = jnp.where(upper, _dot_nt(vh, do), 0.0).astype(BF16)
            dv_ref[:, vs] = (_dot(a_t, do) + _dot_nt(ke, dst_b)).astype(BF16)
            dqd = _dot(da, ki) + _dot(do, spt.astype(BF16))
            dki = _dot(da_t, qd)
            dke = _dot(vh, dst_b)
            ddec = jnp.sum(spt * dst, axis=0, keepdims=True)
            ds_ref[h] = dst * dec[:, ks] + _dot_tn(do, qd)
            dq_ref[:, ks] = ((dqd * eG[:, ks]) * scale).astype(BF16)
            dk_ref[:, ks] = (dki * eN[:, ks] + dke * eE[:, ks]).astype(BF16)
            dke_ke = dke * ke_f
            dG = dqd * qd_f - dki * ki_f - dke_ke
            dGl = jnp.sum(dke_ke, axis=0, keepdims=True) + ddec * dec[:, ks]
            dG_parts.append(dG + jnp.where(last, dGl, 0.0))
        dgnw_ref[...] += dgnw_acc
        dG_all = jnp.concatenate(dG_parts, axis=1)
        dlogg = jnp.where(valid, _tri_sum(upper.astype(BF16), dG_all), 0.0)
        dg_raw = (dlogg * (1.0 / TAU)) * (1.0 / (1.0 + jnp.exp(g_raw)))
        dgb_ref[...] += jnp.sum(dg_raw, axis=0, keepdims=True)
        dg_b = dg_raw.astype(BF16)
        dgw2_ref[...] += _dot_tn(glr_v.astype(BF16), dg_b)
        dglr_ref[...] = _dot_nt(dg_b, gw2_b).astype(BF16)

    def back(s):
        return NCH - 1 - s

    rk = pl.BlockSpec((CH, KW), lambda s: (back(s), 0))
    rv_ = pl.BlockSpec((CH, GW), lambda s: (back(s), 0))
    rg = pl.BlockSpec((CH, 128), lambda s: (back(s), 0))

    def full(shape):
        return pl.BlockSpec(shape, lambda s: tuple(0 for _ in shape))

    return pl.pallas_call(
        body, name="gla_bwd", grid=(NCH,),
        in_specs=[rv_, rv_] + _p_specs(back) + [full((128, KW)), full((1, KW)), full((1, DV)),
                  pl.BlockSpec((1, HEADS, DV, DK), lambda s: (back(s), 0, 0, 0)), ANY],
        out_specs=[rk, rk, rv_, rv_, rg, full((128, KW)), full((1, KW)), full((1, DV))],
        out_shape=[_sds((TP, KW), BF16), _sds((TP, KW), BF16), _sds((TP, GW), BF16), _sds((TP, GW), BF16),
                   _sds((TP, 128), BF16), _sds((128, KW), F32), _sds((1, KW), F32), _sds((1, DV), F32)],
        scratch_shapes=[pltpu.VMEM((HEADS, DV, DK), F32)],
        compiler_params=_cp(("arbitrary",)),
    )(dog, o, *([P] * 8), gw2, gb, gnw, sp, _dep(dep))


POOL_TR = 128
HALO = 16


def _pool_counts(base, nrows):
    rows = base + lax.broadcasted_iota(jnp.int32, (nrows, 1), 0)
    valid = (rows >= ROW_LO) & (rows < ROW_HI)
    t1 = (rows - ROW_LO + 1).astype(F32)
    cnts = [jnp.clip(t1, 1.0, float(w)) for w in WINDOWS]
    return valid, cnts


def _pool_fwd(P, pw, ps):
    def body(cur_ref, prev_ref, pw_ref, ps_ref, y_ref, op_ref):
        i = pl.program_id(0)
        cur = cur_ref[0]
        full = jnp.concatenate([prev_ref[0], cur], axis=0)
        s2 = full + pltpu.roll(full, 1, 0)
        s4 = s2 + pltpu.roll(s2, 2, 0)
        s8 = s4 + pltpu.roll(s4, 4, 0)
        s16 = s8 + pltpu.roll(s8, 8, 0)
        valid, cnts = _pool_counts(i * POOL_TR, POOL_TR)
        for g, s in enumerate((s2, s4, s8, s16)):
            cs = slice(g * GC, (g + 1) * GC)
            y = s[HALO:, cs] / cnts[g] - cur[:, cs]
            yb = jnp.where(valid, y, 0.0).astype(BF16)
            y_ref[:, cs] = yb
            op_ref[:, cs] = (_dot(yb, pw_ref[g].astype(BF16)) * ps_ref[:, cs]).astype(BF16)

    row = pl.BlockSpec((POOL_TR, PW), lambda i: (i, 0))
    per = POOL_TR // HALO
    return pl.pallas_call(
        body, name="pool_fwd", grid=(TP // POOL_TR,),
        in_specs=[pl.BlockSpec((1, POOL_TR, PW), lambda i: (3, i, 0)),
                  pl.BlockSpec((1, HALO, PW), lambda i: (3, jnp.maximum(i * per - 1, 0), 0)),
                  pl.BlockSpec((4, GC, GC), lambda i: (0, 0, 0)), pl.BlockSpec((1, PW), lambda i: (0, 0))],
        out_specs=[row, row],
        out_shape=[_sds((TP, PW), BF16), _sds((TP, PW), BF16)],
        compiler_params=_cp(("arbitrary",)),
    )(P, P, pw, ps)


def _pool_bwd(dop, y, pw, ps, dep=None):
    nblk = TP // HALO

    def body(cur_ref, nxt_ref, y_ref, pw_ref, ps_ref, dep_ref, dpu_ref, dpw_ref, dps_ref):
        i = pl.program_id(0)

        @pl.when(i == 0)
        def _():
            dpw_ref[...] = jnp.zeros_like(dpw_ref)
            dps_ref[...] = jnp.zeros_like(dps_ref)

        n_all = POOL_TR + HALO
        dcur = cur_ref[...]
        dall = jnp.concatenate([dcur, nxt_ref[...]], axis=0)
        valid, cnts = _pool_counts(i * POOL_TR, n_all)
        for g in range(4):
            cs = slice(g * GC, (g + 1) * GC)
            pwb = pw_ref[g].astype(BF16)
            yb = y_ref[:, cs]
            dyw = (dall[:, cs] * ps_ref[:, cs]).astype(BF16)
            dps_ref[:, cs] += jnp.sum(dcur[:, cs] * _dot(yb, pwb), axis=0, keepdims=True)
            dpw_ref[g] += _dot_tn(yb, dyw[0:POOL_TR, :])
            dyv = jnp.where(valid, _dot_nt(dyw, pwb), 0.0)
            e = dyv / cnts[g]
            w = WINDOWS[g]
            sh = 1
            while sh < w:
                e = e + pltpu.roll(e, n_all - sh, 0)
                sh *= 2
            dpu_ref[:, cs] = (e[0:POOL_TR, :] - dyv[0:POOL_TR, :]).astype(BF16)

    row = pl.BlockSpec((POOL_TR, PW), lambda i: (i, 0))
    per = POOL_TR // HALO
    return pl.pallas_call(
        body, name="pool_bwd", grid=(TP // POOL_TR,),
        in_specs=[pl.BlockSpec((POOL_TR, PW), lambda i: (i, 1)),
                  pl.BlockSpec((HALO, PW), lambda i: (jnp.minimum(i * per + per, nblk - 1), 1)),
                  row, pl.BlockSpec((4, GC, GC), lambda i: (0, 0, 0)), pl.BlockSpec((1, PW), lambda i: (0, 0)), ANY],
        out_specs=[row, pl.BlockSpec((4, GC, GC), lambda i: (0, 0, 0)), pl.BlockSpec((1, PW), lambda i: (0, 0))],
        out_shape=[_sds((TP, PW), BF16), _sds((4, GC, GC), F32), _sds((1, PW), F32)],
        compiler_params=_cp(("arbitrary",)),
    )(dop, dop, y, pw, ps, _dep(dep))


def _place():
    x, y, c = lax.axis_index("x"), lax.axis_index("y"), lax.axis_index("c")
    chips = [(1 - x, y), (x, 1 - y), (1 - x, 1 - y)]
    return x, y, c, chips


HBM = pl.BlockSpec(memory_space=pltpu.HBM)
SEM = pl.BlockSpec(memory_space=pltpu.SEMAPHORE)
EFFECT = pltpu.SideEffectType.DATAFLOW_SIDE_EFFECTING


def _cast_into(w, place, cols_out, name, dep=None):
    rows, cols = w.shape
    tr = 256

    def body(p_ref, w_ref, dep_ref, o_ref):
        if cols_out != cols:
            o_ref[0] = jnp.zeros((tr, cols_out), BF16)
            o_ref[0, :, 0:cols] = w_ref[...].astype(BF16)
        else:
            o_ref[0] = w_ref[...].astype(BF16)

    grid_spec = pltpu.PrefetchScalarGridSpec(
        num_scalar_prefetch=1, grid=(rows // tr,),
        in_specs=[pl.BlockSpec((tr, cols), lambda i, p: (i, 0)), ANY],
        out_specs=pl.BlockSpec((1, tr, cols_out), lambda i, p: (p[0], i, 0)))
    return pl.pallas_call(
        body, name=name, grid_spec=grid_spec,
        out_shape=_sds((N_CHIP, rows, cols_out), BF16),
        compiler_params=_cp(("arbitrary",)),
    )(place, w, _dep(dep))


def _cast_win(w, place, dep=None):
    rows, cols = w.shape
    tr = 256

    def body(p_ref, w_ref, dep_ref, o_ref, t_ref):
        t_ref[...] = jnp.zeros_like(t_ref)
        t_ref[:, 0:cols] = w_ref[...]
        t = t_ref[...]
        lane = lax.broadcasted_iota(jnp.int32, (tr, PAD_IN), 1)
        for kk in range(N_CHIP):
            @pl.when(p_ref[0] == kk)
            def _(kk=kk):
                if kk == 0:
                    placed = t
                elif kk < 3:
                    placed = pltpu.roll(t, 4 * kk, 1)
                else:
                    pool = pltpu.roll(t, PAD_IN - 4, 1)
                    gate = pltpu.roll(t, GW + 12, 1)
                    placed = jnp.where(lane < GW, pool, jnp.where((lane >= GW + 12) & (lane < GW + 16), gate, 0.0))
                o_ref[0] = placed.astype(BF16)

    grid_spec = pltpu.PrefetchScalarGridSpec(
        num_scalar_prefetch=1, grid=(rows // tr,),
        in_specs=[pl.BlockSpec((tr, cols), lambda i, p: (i, 0)), ANY],
        out_specs=pl.BlockSpec((1, tr, PAD_IN), lambda i, p: (p[0], i, 0)),
        scratch_shapes=[pltpu.VMEM((tr, PAD_IN), F32)])
    return pl.pallas_call(
        body, name="cast_win", grid_spec=grid_spec,
        out_shape=_sds((N_CHIP, rows, PAD_IN), BF16),
        compiler_params=_cp(("arbitrary",)),
    )(place, w, _dep(dep))


def _half_rows(ref, k, which):
    h = ref.shape[1] // 2
    return ref.at[k, pl.ds(pl.multiple_of(which * h, 8), h), :]


def _sent_rows(ref, k, which, whole):
    return ref.at[k] if whole else _half_rows(ref, k, which)


def _gather_start(ws, name, whole=None):
    n = len(ws)
    whole = whole or [False] * n

    def body(*refs):
        ins = refs[:n]
        ssems = refs[n:2 * n]
        rsems = refs[2 * n:3 * n]
        token = refs[4 * n]
        x, y, c, chips = _place()
        me = 2 * x + y
        for w in range(n):
            blk = _sent_rows(ins[w], me, c, whole[w])
            for j, chip in enumerate(chips):
                pltpu.make_async_remote_copy(src_ref=blk, dst_ref=blk, send_sem=ssems[w].at[j], recv_sem=rsems[w].at[j],
                                             device_id=(*chip, c), device_id_type=MESH).start()
        token[...] = jnp.zeros_like(token)

    sem3 = pltpu.SemaphoreType.DMA((3,))
    outs = pl.pallas_call(
        body, name=name,
        out_shape=tuple([sem3] * (2 * n) + [pltpu.HBM(w.shape, w.dtype) for w in ws] + [_sds((8, 128), F32)]),
        in_specs=(HBM,) * n, out_specs=(SEM,) * (2 * n) + (HBM,) * n + (VMEM_FULL,),
        input_output_aliases={w: 2 * n + w for w in range(n)},
        compiler_params=pltpu.CompilerParams(has_side_effects=EFFECT),
    )(*[pltpu.with_memory_space_constraint(w, pltpu.HBM) for w in ws])
    return outs[:n], outs[n:2 * n], outs[2 * n:3 * n], outs[3 * n]


def _gather_wait(w, ssem, rsem, after, name, whole=False):
    def body(w_ref, ssem_ref, rsem_ref, after_ref, out_ref):
        x, y, c, chips = _place()
        me = 2 * x + y
        mine = _sent_rows(w_ref, me, c, whole)
        for j, (cx, cy) in enumerate(chips):
            cp = pltpu.make_async_remote_copy(src_ref=mine, dst_ref=_sent_rows(w_ref, 2 * cx + cy, c, whole),
                                              send_sem=ssem_ref.at[j], recv_sem=rsem_ref.at[j],
                                              device_id=(cx, cy, c), device_id_type=MESH)
            cp.wait_send()
            cp.wait_recv()

    return pl.pallas_call(
        body, name=name, out_shape=pltpu.HBM(w.shape, w.dtype),
        in_specs=(HBM, SEM, SEM, ANY), out_specs=HBM, input_output_aliases={0: 0},
        compiler_params=pltpu.CompilerParams(has_side_effects=EFFECT),
    )(w, ssem, rsem, after)


def _gather_copies(ref, kind, ssem, rsem):
    x, y, c, _ = _place()
    xn, yn, sib = (1 - x, y, c), (x, 1 - y, c), (x, y, 1 - c)
    kx, ky, kd = 2 * (1 - x) + y, 2 * x + (1 - y), 2 * (1 - x) + (1 - y)
    half = ref.shape[1] // 2
    quarter = half // 2

    def piece(k, q):
        return ref.at[k, pl.ds(pl.multiple_of(c * half + q * quarter, 8), quarter), :]

    if kind == "d":
        blk = _half_rows(ref, 2 * x + y, c)
        pairs = [(blk, xn), (blk, yn)]
    elif kind == "r":
        pairs = [(piece(ky, 1), xn), (piece(kx, 0), yn)]
    elif kind == "fx":
        pairs = [(_half_rows(ref, kx, c), sib), (_half_rows(ref, ky, c), sib)]
    else:
        pairs = [(_half_rows(ref, kd, c), sib)]
    return [pltpu.make_async_remote_copy(src_ref=blk, dst_ref=blk, send_sem=ssem.at[i], recv_sem=rsem.at[i],
                                         device_id=to, device_id_type=MESH) for i, (blk, to) in enumerate(pairs)]


def _gather_step(name, arrs, waits, starts, sems_in=(), after=()):
    n, nw, ns = len(arrs), len(waits), len(starts)
    after = [a for a in after if a is not None] or [_dep(None)]

    def body(*refs):
        a_in = refs[:n]
        s_in = refs[n:n + 2 * nw]
        outs = refs[n + 2 * nw + len(after):]
        s_out = outs[:2 * ns]
        for i, (ai, kind) in enumerate(waits):
            for cp in _gather_copies(a_in[ai], kind, s_in[2 * i], s_in[2 * i + 1]):
                cp.wait_send()
                cp.wait_recv()
        for i, (ai, kind) in enumerate(starts):
            for cp in _gather_copies(a_in[ai], kind, s_out[2 * i], s_out[2 * i + 1]):
                cp.start()
        if ns:
            token = outs[2 * ns + n]
            token[...] = jnp.zeros_like(token)

    sem2 = pltpu.SemaphoreType.DMA((2,))
    flat_in = [s for pair in sems_in for s in pair]
    arrs = [pltpu.with_memory_space_constraint(a, pltpu.HBM) for a in arrs]
    outs = pl.pallas_call(
        body, name=name,
        out_shape=tuple([sem2] * (2 * ns) + [pltpu.HBM(a.shape, a.dtype) for a in arrs]
                        + ([_sds((8, 128), F32)] if ns else [])),
        in_specs=(HBM,) * n + (SEM,) * (2 * nw) + (ANY,) * len(after),
        out_specs=(SEM,) * (2 * ns) + (HBM,) * n + ((VMEM_FULL,) if ns else ()),
        input_output_aliases={i: 2 * ns + i for i in range(n)},
        compiler_params=pltpu.CompilerParams(has_side_effects=EFFECT),
    )(*arrs, *flat_in, *after)
    sems = [(outs[2 * i], outs[2 * i + 1]) for i in range(ns)]
    return sems, list(outs[2 * ns:2 * ns + n]), (outs[2 * ns + n] if ns else None)


def _rs_start(sb, name, after=None):
    _, half, cols = sb.shape

    def body(sb_ref, land_ref, after_ref, ssem, rsem, sb_out, land_out, token):
        x, y, c, chips = _place()
        for j, (cx, cy) in enumerate(chips):
            pltpu.make_async_remote_copy(src_ref=sb_ref.at[2 * cx + cy], dst_ref=land_ref.at[j], send_sem=ssem.at[j],
                                         recv_sem=rsem.at[j], device_id=(cx, cy, c), device_id_type=MESH).start()
        token[...] = jnp.zeros_like(token)

    sem3 = pltpu.SemaphoreType.DMA((3,))
    land = lax.empty((3, half, cols), BF16)
    return pl.pallas_call(
        body, name=name,
        out_shape=(sem3, sem3, pltpu.HBM(sb.shape, sb.dtype), pltpu.HBM(land.shape, land.dtype), _sds((8, 128), F32)),
        in_specs=(HBM, HBM, ANY), out_specs=(SEM, SEM, HBM, HBM, VMEM_FULL), input_output_aliases={0: 2, 1: 3},
        compiler_params=pltpu.CompilerParams(has_side_effects=EFFECT),
    )(pltpu.with_memory_space_constraint(sb, pltpu.HBM), pltpu.with_memory_space_constraint(land, pltpu.HBM), _dep(after))


def _rs_wait(items, after, name):
    n = len(items)

    def body(*refs):
        x, y, c, chips = _place()
        for i in range(n):
            sb_ref, land_ref, ssem_ref, rsem_ref = refs[4 * i:4 * i + 4]
            for j, (cx, cy) in enumerate(chips):
                cp = pltpu.make_async_remote_copy(src_ref=sb_ref.at[2 * cx + cy], dst_ref=land_ref.at[j],
                                                  send_sem=ssem_ref.at[j], recv_sem=rsem_ref.at[j],
                                                  device_id=(cx, cy, c), device_id_type=MESH)
                cp.wait_send()
                cp.wait_recv()

    outs = pl.pallas_call(
        body, name=name,
        out_shape=tuple(pltpu.HBM(a.shape, a.dtype) for it in items for a in it[:2]),
        in_specs=(HBM, HBM, SEM, SEM) * n + (ANY,), out_specs=(HBM,) * (2 * n),
        input_output_aliases={4 * i + k: 2 * i + k for i in range(n) for k in range(2)},
        compiler_params=pltpu.CompilerParams(has_side_effects=EFFECT),
    )(*[a for it in items for a in it], after)
    return [outs[2 * i + 1] for i in range(n)]


def _pair_copy(g_ref, land_ref, ssem, rsem):
    x, y, c, _ = _place()
    h = g_ref.shape[1] // 2
    src = g_ref.at[:, pl.ds(pl.multiple_of((1 - c) * h, 8), h), :]
    return pltpu.make_async_remote_copy(src_ref=src, dst_ref=land_ref, send_sem=ssem.at[0], recv_sem=rsem.at[0],
                                        device_id=(x, y, 1 - c), device_id_type=MESH)


def _pair_start(g, name):
    def body(g_ref, land_ref, ssem, rsem, g_out, land_out, token):
        _pair_copy(g_ref, land_ref, ssem, rsem).start()
        token[...] = jnp.zeros_like(token)

    sem1 = pltpu.SemaphoreType.DMA((1,))
    land = lax.empty((N_CHIP, g.shape[1] // 2, g.shape[2]), F32)
    return pl.pallas_call(
        body, name=name,
        out_shape=(sem1, sem1, pltpu.HBM(g.shape, g.dtype), pltpu.HBM(land.shape, land.dtype), _sds((8, 128), F32)),
        in_specs=(HBM, HBM), out_specs=(SEM, SEM, HBM, HBM, VMEM_FULL), input_output_aliases={0: 2, 1: 3},
        compiler_params=pltpu.CompilerParams(has_side_effects=EFFECT),
    )(pltpu.with_memory_space_constraint(g, pltpu.HBM), pltpu.with_memory_space_constraint(land, pltpu.HBM))


def _pair_wait(g, land, ssem, rsem, after, name):
    def body(g_ref, land_ref, ssem_ref, rsem_ref, after_ref, g_out, land_out):
        cp = _pair_copy(g_ref, land_ref, ssem_ref, rsem_ref)
        cp.wait_send()
        cp.wait_recv()

    return pl.pallas_call(
        body, name=name,
        out_shape=(pltpu.HBM(g.shape, g.dtype), pltpu.HBM(land.shape, land.dtype)),
        in_specs=(HBM, HBM, SEM, SEM, ANY), out_specs=(HBM, HBM), input_output_aliases={0: 0, 1: 1},
        compiler_params=pltpu.CompilerParams(has_side_effects=EFFECT),
    )(g, land, ssem, rsem, after)


def _pair_sum(g, rcv, place, name):
    _, rows, cols = g.shape
    half = rows // 2
    tr = 256
    nt = half // tr

    def body(p_ref, g_ref, r_ref, sb_ref, sf_ref):
        s = pl.program_id(1)
        tot = g_ref[0] + r_ref[0]
        sb_ref[0] = tot.astype(BF16)

        @pl.when(s == p_ref[0])
        def _():
            sf_ref[...] = tot

    grid_spec = pltpu.PrefetchScalarGridSpec(
        num_scalar_prefetch=1, grid=(nt, N_CHIP),
        in_specs=[pl.BlockSpec((1, tr, cols), lambda t, s, p: (s, p[1] * nt + t, 0)),
                  pl.BlockSpec((1, tr, cols), lambda t, s, p: (s, t, 0))],
        out_specs=[pl.BlockSpec((1, tr, cols), lambda t, s, p: (s, t, 0)),
                   pl.BlockSpec((tr, cols), lambda t, s, p: (t, 0))])
    return pl.pallas_call(
        body, name=name, grid_spec=grid_spec,
        out_shape=[_sds((N_CHIP, half, cols), BF16), _sds((half, cols), F32)],
        compiler_params=_cp(("arbitrary", "arbitrary")),
    )(place, g, rcv)


def _final_sum(sf, rb, place, name):
    half, cols = sf.shape
    tr = 256
    nt = half // tr

    def body(p_ref, sf_ref, r_ref, out_ref):
        acc = sf_ref[...]
        for j in range(3):
            acc = acc + r_ref[j].astype(F32)
        out_ref[...] = acc

    grid_spec = pltpu.PrefetchScalarGridSpec(
        num_scalar_prefetch=1, grid=(nt,),
        in_specs=[pl.BlockSpec((tr, cols), lambda t, p: (t, 0)), pl.BlockSpec((3, tr, cols), lambda t, p: (0, t, 0))],
        out_specs=pl.BlockSpec((tr, cols), lambda t, p: (p[1] * nt + t, 0)))
    return pl.pallas_call(
        body, name=name, grid_spec=grid_spec,
        out_shape=_sds((2 * half, cols), F32),
        compiler_params=_cp(("arbitrary",)),
    )(place, sf, rb)


def _half_copy(f_ref, which, ssem, rsem):
    x, y, c, _ = _place()
    h = f_ref.shape[0] // 2
    rows = f_ref.at[pl.ds(pl.multiple_of(which * h, 8), h), :]
    return pltpu.make_async_remote_copy(src_ref=rows, dst_ref=rows, send_sem=ssem.at[0], recv_sem=rsem.at[0],
                                        device_id=(x, y, 1 - c), device_id_type=MESH)


def _half_start(fulls, name, after=None):
    n = len(fulls)

    def body(*refs):
        for i in range(n):
            _half_copy(refs[i], lax.axis_index("c"), refs[n + 1 + 2 * i], refs[n + 2 + 2 * i]).start()
        token = refs[4 * n + 1]
        token[...] = jnp.zeros_like(token)

    sem1 = pltpu.SemaphoreType.DMA((1,))
    outs = pl.pallas_call(
        body, name=name,
        out_shape=tuple([sem1] * (2 * n) + [pltpu.HBM(f.shape, f.dtype) for f in fulls] + [_sds((8, 128), F32)]),
        in_specs=(HBM,) * n + (ANY,), out_specs=(SEM,) * (2 * n) + (HBM,) * n + (VMEM_FULL,),
        input_output_aliases={i: 2 * n + i for i in range(n)},
        compiler_params=pltpu.CompilerParams(has_side_effects=EFFECT),
    )(*[pltpu.with_memory_space_constraint(f, pltpu.HBM) for f in fulls], _dep(after))
    return [(outs[2 * i], outs[2 * i + 1], outs[2 * n + i]) for i in range(n)], outs[3 * n]


def _half_wait(items, after, name):
    n = len(items)

    def body(*refs):
        c = lax.axis_index("c")
        for i in range(n):
            ssem_ref, rsem_ref, f_ref = refs[3 * i:3 * i + 3]
            _half_copy(f_ref, c, ssem_ref, rsem_ref).wait_send()
            _half_copy(f_ref, 1 - c, ssem_ref, rsem_ref).wait_recv()

    return pl.pallas_call(
        body, name=name, out_shape=tuple(pltpu.HBM(it[2].shape, it[2].dtype) for it in items),
        in_specs=(SEM, SEM, HBM) * n + (ANY,), out_specs=(HBM,) * n,
        input_output_aliases={3 * i + 2: i for i in range(n)},
        compiler_params=pltpu.CompilerParams(has_side_effects=EFFECT),
    )(*[a for it in items for a in it], after)


def _small_copies(src_ref, land_ref, ssem, rsem, first):
    x, y, c, chips = _place()
    if first:
        return [pltpu.make_async_remote_copy(src_ref=src_ref, dst_ref=land_ref, send_sem=ssem.at[0], recv_sem=rsem.at[0],
                                             device_id=(x, y, 1 - c), device_id_type=MESH)]
    return [pltpu.make_async_remote_copy(src_ref=src_ref, dst_ref=land_ref.at[j], send_sem=ssem.at[j], recv_sem=rsem.at[j],
                                         device_id=(*chip, c), device_id_type=MESH) for j, chip in enumerate(chips)]


def _small_start(src, first, name, after=None):
    n = 1 if first else 3

    def body(src_ref, land_ref, after_ref, ssem, rsem, src_out, land_out, token):
        for cp in _small_copies(src_ref, land_ref, ssem, rsem, first):
            cp.start()
        token[...] = jnp.zeros_like(token)

    sems = pltpu.SemaphoreType.DMA((n,))
    land = lax.empty(src.shape if first else (3,) + src.shape, F32)
    return pl.pallas_call(
        body, name=name,
        out_shape=(sems, sems, pltpu.HBM(src.shape, F32), pltpu.HBM(land.shape, F32), _sds((8, 128), F32)),
        in_specs=(HBM, HBM, ANY), out_specs=(SEM, SEM, HBM, HBM, VMEM_FULL), input_output_aliases={0: 2, 1: 3},
        compiler_params=pltpu.CompilerParams(has_side_effects=EFFECT),
    )(pltpu.with_memory_space_constraint(src, pltpu.HBM), pltpu.with_memory_space_constraint(land, pltpu.HBM), _dep(after))


def _small_wait(src, land, ssem, rsem, first, after, name):
    def body(src_ref, land_ref, ssem_ref, rsem_ref, after_ref, src_out, land_out):
        for cp in _small_copies(src_ref, land_ref, ssem_ref, rsem_ref, first):
            cp.wait_send()
            cp.wait_recv()

    return pl.pallas_call(
        body, name=name,
        out_shape=(pltpu.HBM(src.shape, F32), pltpu.HBM(land.shape, F32)),
        in_specs=(HBM, HBM, SEM, SEM, ANY), out_specs=(HBM, HBM), input_output_aliases={0: 0, 1: 1},
        compiler_params=pltpu.CompilerParams(has_side_effects=EFFECT),
    )(src, land, ssem, rsem, after)


def _small_pair_sum(vec, got):
    def body(v_ref, g_ref, o_ref):
        o_ref[...] = v_ref[...] + g_ref[...]

    return pl.pallas_call(body, name="small_pair_sum", in_specs=[VMEM_FULL] * 2, out_specs=VMEM_FULL,
                          out_shape=_sds(vec.shape, F32), compiler_params=_cp())(vec, got)


def _small_chip_sum(pair, got, place):
    def body(p_ref, pair_ref, got_ref, o_ref):
        acc = None
        for kk in range(N_CHIP):
            d = jnp.bitwise_xor(p_ref[0], kk)
            t = jnp.where(d == 0, pair_ref[...], jnp.where(d == 2, got_ref[0], jnp.where(d == 1, got_ref[1], got_ref[2])))
            acc = t if acc is None else acc + t
        o_ref[...] = acc

    grid_spec = pltpu.PrefetchScalarGridSpec(
        num_scalar_prefetch=1, grid=(1,),
        in_specs=[pl.BlockSpec(pair.shape, lambda i, p: (0, 0)), pl.BlockSpec(got.shape, lambda i, p: (0, 0, 0))],
        out_specs=pl.BlockSpec(pair.shape, lambda i, p: (0, 0)))
    return pl.pallas_call(body, name="small_chip_sum", grid_spec=grid_spec, out_shape=_sds(pair.shape, F32),
                          compiler_params=_cp(("arbitrary",)))(place, pair, got)


def _adam_math(w, g, m, v):
    m = B1 * m + (1.0 - B1) * g
    v = B2 * v + (1.0 - B2) * (g * g)
    m_hat = m / (1.0 - B1 ** STEP)
    v_hat = v / (1.0 - B2 ** STEP)
    delta = -LR * (m_hat / (jnp.sqrt(v_hat) + AEPS) + WD * w)
    return delta, m, v


def _adam_big(w, g, m, v, name, dep=None):
    rows, cols = w.shape
    tr = 128

    def body(w_ref, g_ref, m_ref, v_ref, dep_ref, go_ref, d_ref, nm_ref, nv_ref):
        g = g_ref[...]
        d, nm, nv = _adam_math(w_ref[...], g, m_ref[...], v_ref[...])
        go_ref[...] = g
        d_ref[...] = d
        nm_ref[...] = nm
        nv_ref[...] = nv

    blk = pl.BlockSpec((tr, cols), lambda i: (i, 0))
    return pl.pallas_call(
        body, name=name, grid=(rows // tr,),
        in_specs=[blk] * 4 + [ANY], out_specs=[blk] * 4, out_shape=[_sds((rows, cols), F32)] * 4,
        compiler_params=_cp(("arbitrary",)),
    )(w, g, m, v, _dep(dep))


def _adam_small(ws, gs, ms, vs, dep=None):
    n = len(ws)

    def body(*refs):
        for i in range(n):
            d, nm, nv = _adam_math(refs[i][...], refs[n + i][...], refs[2 * n + i][...], refs[3 * n + i][...])
            refs[4 * n + 1 + i][...] = d
            refs[5 * n + 1 + i][...] = nm
            refs[6 * n + 1 + i][...] = nv

    shapes = [_sds(w.shape, F32) for w in ws]
    outs = pl.pallas_call(
        body, name="adam_small",
        in_specs=[VMEM_FULL] * (4 * n) + [ANY], out_specs=[VMEM_FULL] * (3 * n), out_shape=shapes * 3,
        compiler_params=_cp(),
    )(*ws, *gs, *ms, *vs, _dep(dep))
    return outs[:n], outs[n:2 * n], outs[2 * n:]


def _pad_rows8(a):
    flat = a.reshape(-1, 128)
    pad = (-flat.shape[0]) % 8
    if pad:
        flat = jnp.concatenate([flat, jnp.zeros((pad, 128), F32)], axis=0)
    return flat


def kernel(x, meta_tokens, norm1_w, w_in, gate_w2, gate_b, gla_norm_w, pool_w, pool_scale, w_out, norm2_w, mlp_w1, mlp_w2, final_norm_w, loss_target, m_meta_tokens, m_norm1_w, m_w_in, m_gate_w2, m_gate_b, m_gla_norm_w, m_pool_w, m_pool_scale, m_w_out, m_norm2_w, m_mlp_w1, m_mlp_w2, m_final_norm_w, v_meta_tokens, v_norm1_w, v_w_in, v_gate_w2, v_gate_b, v_gla_norm_w, v_pool_w, v_pool_scale, v_w_out, v_norm2_w, v_mlp_w1, v_mlp_w2, v_final_norm_w):
    cx, cy, cc = lax.axis_index("x"), lax.axis_index("y"), lax.axis_index("c")
    me = (2 * cx + cy).astype(jnp.int32)

    place = jnp.stack([me, cc.astype(jnp.int32)])
    fw = final_norm_w.reshape(1, D)

    mine = jnp.concatenate([meta_tokens.reshape(64, 128), gate_w2[0], pool_w[0].reshape(512, 128)], axis=0)
    small = lax.dynamic_update_slice(jnp.zeros((N_CHIP, 592, 128), F32), mine[None], (me, 0, 0))
    (s_sm,), (r_sm,), (f_sm,), tok = _gather_start([small], "gather_start_small", [True])
    (sem_win_d,), (win,), tok = _gather_step("gather_start_win", [_cast_win(w_in[0], place, tok)], [], [(0, "d")])
    wout, w1, w2 = (_cast_into(w_out[0], place, D, "cast_wout", tok), _cast_into(mlp_w1[0], place, D, "cast_w1", tok),
                    _cast_into(mlp_w2[0], place, D, "cast_w2", tok))
    small = _gather_wait(f_sm, s_sm, r_sm, w2, "gather_wait_small", True)
    metaF = jnp.concatenate([small[k, 0:64].reshape(N_META, 512) for k in range(N_CHIP)], axis=1)
    gw2F = jnp.concatenate([small[k, 64:80] for k in range(N_CHIP)], axis=1)
    pwF = jnp.concatenate([small[k, 80:592].reshape(4, 64, GC) for k in range(N_CHIP)], axis=1)

    fly = {"win": win, "wout": wout, "w1": w1, "w2": w2}
    sems = {"win_d": sem_win_d}

    def step(name, names, waits, starts, after):
        at = {nm: i for i, nm in enumerate(names)}
        new, arrs, tok = _gather_step(name, [fly[nm] for nm in names], [(at[nm], k) for nm, k in waits],
                                      [(at[nm], k) for nm, k in starts], [sems[nm + "_" + k] for nm, k in waits], after)
        fly.update(zip(names, arrs))
        sems.update({nm + "_" + k: s for (nm, k), s in zip(starts, new)})
        return tok

    def get_win(after):
        tok = step("gather_relay_win", ["win", "wout", "w1"], [("win", "d")],
                   [("win", "r"), ("win", "fx"), ("wout", "d"), ("w1", "d")], [after])
        tok = step("gather_land_win", ["win"], [("win", "r")], [("win", "fd")], [tok, m_w_in[0]])
        step("gather_wait_win", ["win"], [("win", "fx"), ("win", "fd")], [], [tok])
        return fly["win"]

    def relay_mid(after):
        tok = step("gather_relay_mid", ["wout", "w1", "w2"], [("wout", "d"), ("w1", "d")],
                   [("wout", "r"), ("w1", "r"), ("w2", "d"), ("wout", "fx"), ("w1", "fx")], [after, v_w_in[0]])
        return step("gather_land_wout", ["wout"], [("wout", "r")], [("wout", "fd")], [tok])

    def get_wout(after):
        step("gather_wait_wout", ["wout"], [("wout", "fx"), ("wout", "fd")], [], [after])
        tok = step("gather_land_w1", ["w1"], [("w1", "r")], [("w1", "fd")], [fly["wout"]])
        return fly["wout"].reshape(D, D), tok

    def get_w1(after):
        step("gather_wait_w1", ["w1"], [("w1", "fx"), ("w1", "fd")], [], [after])
        tok = step("gather_relay_w2", ["w2"], [("w2", "d")], [("w2", "r"), ("w2", "fx")], [fly["w1"]])
        return fly["w1"], tok

    def get_w2(after):
        tok = step("gather_land_w2", ["w2"], [("w2", "r")], [("w2", "fd")], [after])
        step("gather_wait_w2", ["w2"], [("w2", "fx"), ("w2", "fd")], [], [tok])
        return fly["w2"].reshape(DFF, D)

    pairs, pending = {}, {}

    def grad_start(nm, g):
        ssem, rsem, g_thru, land, token = _pair_start(g, "pair_start_" + nm)
        pairs[nm] = (ssem, rsem, g_thru, land)
        return token

    def grad_finish(nm, after):
        ssem, rsem, g_thru, land = pairs[nm]
        g, rcv = _pair_wait(g_thru, land, ssem, rsem, after, "pair_wait_" + nm)
        sb, sf = _pair_sum(g, rcv, place, "pair_sum_" + nm)
        if nm == "win":
            pending[nm] = (sf, sb)
            return sf
        ssem, rsem, sb_thru, land, token = _rs_start(sb, "rs_start_" + nm)
        pending[nm] = (sf, ssem, rsem, sb_thru, land)
        return token

    (grad_x, loss8, d_n1w, d_gb, d_gnw, d_ps, d_n2w, d_fw, d_meta, d_gw2, d_pw) = _local_step(
        x[0], loss_target[0], get_win, relay_mid, get_wout, get_w1, get_w2, metaF, gw2F, pwF,
        norm1_w, gate_b, gla_norm_w, pool_scale, norm2_w, fw, grad_start, grad_finish)
    return _reduce_and_update(
        me, place, pending, grad_x, loss8, d_n1w, d_gb, d_gnw, d_ps, d_n2w, d_fw, d_meta, d_gw2, d_pw,
        meta_tokens, norm1_w, w_in, gate_w2, gate_b, gla_norm_w, pool_w, pool_scale, w_out, norm2_w, mlp_w1, mlp_w2, fw,
        m_meta_tokens, m_norm1_w, m_w_in, m_gate_w2, m_gate_b, m_gla_norm_w, m_pool_w, m_pool_scale, m_w_out, m_norm2_w,
        m_mlp_w1, m_mlp_w2, m_final_norm_w, v_meta_tokens, v_norm1_w, v_w_in, v_gate_w2, v_gate_b, v_gla_norm_w, v_pool_w,
        v_pool_scale, v_w_out, v_norm2_w, v_mlp_w1, v_mlp_w2, v_final_norm_w)


def _local_step(x, target, get_win, relay_mid, get_wout, get_w1, get_w2, metaF, gw2F, pwF, norm1_w, gate_b, gla_norm_w,
                pool_scale, norm2_w, fw, grad_start, grad_finish):
    h0, u = _embed_norm(x, metaF, norm1_w)
    Win = get_win(u)
    P = _in_proj(u, Win)
    gw2p = jnp.pad(gw2F, ((0, 128 - RANK), (0, 0)))
    yb, op = _pool_fwd(P, pwF, pool_scale)
    o, og, sp = _gla_fwd(P, gw2p, gate_b, gla_norm_w, relay_mid(op))
    Wout, tok = get_wout(og)
    h1 = _out_proj(og, op, Wout, h0, tok)
    n2 = _norm_rows(h1, norm2_w, "norm2")
    W1, tok = get_w1(n2)
    zr, a = _mlp_up(n2, W1, () if tok is None else (tok,))
    W2 = get_w2(a)
    h2 = _mlp_down(a, W2, h1)

    dh2, dh2b, d_fw, loss8 = _loss_head(h2, target, fw)
    tok = grad_start("w2", _grad_w2(a, dh2b).reshape(N_CHIP, D, D))
    dz = _mlp_dz(dh2b, W2, zr, tok)
    tok = grad_finish("w2", dz)
    tok = grad_start("w1", _grad_w1(n2, dz, tok))
    dn2 = _mlp_dn(dz, W1, tok)
    tok = grad_finish("w1", dn2)
    dh1, dh1b, d_n2w = _norm_bwd(dn2, h1, dh2, norm2_w, "norm2_bwd", tok)
    dmixed = _mixed_grad(dh1b, Wout)
    tok = grad_start("wout", _grad_wout(og, op, dh1b))
    dpu, d_pw, d_ps = _pool_bwd(dmixed, yb, pwF, pool_scale, tok)
    dq, dk, dv, dr, dglr, d_gw2p, d_gb, d_gnw = _gla_bwd(dmixed, o, P, gw2p, gate_b, gla_norm_w, sp, tok)
    d_gw2 = d_gw2p[0:RANK]
    tok = grad_finish("wout", dq)
    tok = grad_start("win", _grad_win(u, dq, dk, dv, dr, dglr, dpu, tok))
    du = _in_grad(dq, dk, dv, dr, dglr, dpu, Win, tok)
    tok = grad_finish("win", du)
    grad_x, d_meta, d_n1w = _input_grad(du, h0, dh1, norm1_w, tok)
    return grad_x, loss8, d_n1w, d_gb, d_gnw, d_ps, d_n2w, d_fw, d_meta, d_gw2, d_pw


def _reduce_and_update(me, place, pending, grad_x, loss8, d_n1w, d_gb, d_gnw, d_ps, d_n2w, d_fw, d_meta, d_gw2, d_pw,
                       meta_tokens, norm1_w, w_in, gate_w2, gate_b, gla_norm_w, pool_w, pool_scale, w_out, norm2_w,
                       mlp_w1, mlp_w2, fw, m_meta_tokens, m_norm1_w, m_w_in, m_gate_w2, m_gate_b, m_gla_norm_w, m_pool_w,
                       m_pool_scale, m_w_out, m_norm2_w, m_mlp_w1, m_mlp_w2, m_final_norm_w, v_meta_tokens, v_norm1_w, v_w_in,
                       v_gate_w2, v_gate_b, v_gla_norm_w, v_pool_w, v_pool_scale, v_w_out, v_norm2_w, v_mlp_w1, v_mlp_w2,
                       v_final_norm_w):
    parts = [loss8, d_n1w, d_gb, d_gnw, d_ps, d_n2w, d_fw, d_meta, d_gw2, d_pw]
    packed = [_pad_rows8(p) for p in parts]
    sizes = [p.shape[0] for p in packed]
    vec = jnp.concatenate(packed, axis=0)

    big, halves = {}, {}
    params = {"w2": (mlp_w2, m_mlp_w2, v_mlp_w2), "w1": (mlp_w1, m_mlp_w1, v_mlp_w1), "wout": (w_out, m_w_out, v_w_out),
              "win": (w_in, m_w_in, v_w_in)}

    def reduce_(names, after, tag):
        items = [(pending[nm][3], pending[nm][4], pending[nm][1], pending[nm][2]) for nm in names]
        landed = _rs_wait(items, after, "rs_wait_" + tag)
        fulls = [_final_sum(pending[nm][0], rb, place, "final_sum_" + nm) for nm, rb in zip(names, landed)]
        sent, token = _half_start(fulls, "half_start_" + tag)
        halves.update(zip(names, sent))
        return token

    def update(names, after, tag):
        fulls = _half_wait([halves[nm] for nm in names], after, "half_wait_" + tag)
        tok = None
        for nm, full in zip(names, fulls):
            w, m, v = params[nm]
            big[nm] = _adam_big(w[0], full, m[0], v[0], "adam_" + nm, tok)
            tok = big[nm][3]
        return tok

    first = ["w2", "w1", "wout"]
    s1, r1, vec, land1, tok = _small_start(vec, True, "small_start_pair")
    tok = reduce_(first, tok, "mlp_wout")
    vec, got = _small_wait(vec, land1, s1, r1, True, tok, "small_wait_pair")
    pair = _small_pair_sum(vec, got)
    s2, r2, pair, land2, tok = _small_start(pair, False, "small_start_chips")
    sf, sb = pending["win"]
    ssem, rsem, sb_thru, land, tok = _rs_start(sb, "rs_start_win", tok)
    pending["win"] = (sf, ssem, rsem, sb_thru, land)
    tok = update(first, tok, "mlp_wout")
    pair, got = _small_wait(pair, land2, s2, r2, False, tok, "small_wait_chips")
    red = _small_chip_sum(pair, got, place)
    tok = reduce_(["win"], red, "win")
    after = update(["win"], tok, "win")
    offs = [0]
    for s in sizes:
        offs.append(offs[-1] + s)

    def take(i, shape):
        n = 1
        for d in shape:
            n *= d
        return red[offs[i]:offs[i] + n // 128].reshape(shape)

    loss = red[0, 0]
    G_n1w = take(1, (1, D))
    G_gb = take(2, (1, KW))
    G_gnw = take(3, (1, DV))
    G_ps = take(4, (1, PW))
    G_n2w = take(5, (1, D))
    G_fw = take(6, (1, D))
    G_meta = lax.dynamic_slice(take(7, (N_META, D)), (0, me * 512), (N_META, 512))
    G_gw2 = lax.dynamic_slice(take(8, (RANK, KW)), (0, me * 128), (RANK, 128))
    G_pw = lax.dynamic_slice(take(9, (4, GC, GC)), (0, me * 64, 0), (4, 64, GC))

    G_win, d_win, nm_win, nv_win = big["win"]
    G_wout, d_wout, nm_wout, nv_wout = big["wout"]
    G_w1, d_w1, nm_w1, nv_w1 = big["w1"]
    G_w2, d_w2, nm_w2, nv_w2 = big["w2"]
    ws = [meta_tokens, norm1_w, gate_w2[0], gate_b, gla_norm_w, pool_w[0], pool_scale, norm2_w, fw]
    gs = [G_meta, G_n1w, G_gw2, G_gb, G_gnw, G_pw, G_ps, G_n2w, G_fw]
    ms = [m_meta_tokens, m_norm1_w, m_gate_w2[0], m_gate_b, m_gla_norm_w, m_pool_w[0], m_pool_scale, m_norm2_w,
          m_final_norm_w.reshape(1, D)]
    vs = [v_meta_tokens, v_norm1_w, v_gate_w2[0], v_gate_b, v_gla_norm_w, v_pool_w[0], v_pool_scale, v_norm2_w,
          v_final_norm_w.reshape(1, D)]
    ds, nms, nvs = _adam_small(ws, gs, ms, vs, after)

    def assemble(small, win_, wout_, w1_, w2_):
        meta_, n1_, gw2_, gb_, gnw_, pw_, ps_, n2_, fw_ = small
        return (meta_, n1_, win_[None], gw2_[None], gb_, gnw_, pw_[None], ps_, wout_[None], n2_, w1_[None], w2_[None],
                fw_.reshape(D))

    grads_out = assemble(gs, G_win, G_wout, G_w1, G_w2)
    deltas = assemble(ds, d_win, d_wout, d_w1, d_w2)
    new_m = assemble(nms, nm_win, nm_wout, nm_w1, nm_w2)
    new_v = assemble(nvs, nv_win, nv_wout, nv_w1, nv_w2)
    return (loss, grad_x[None], *grads_out, *deltas, *new_m, *new_v)
```

```python
import functools

import jax
import jax.numpy as jnp
from jax import lax
from jax.experimental import pallas as pl
from jax.experimental.pallas import tpu as pltpu

F32 = jnp.float32
BF16 = jnp.bfloat16

D = 2048
SEQ = 2048
N_META = 16
CH = 64
TP = 2176
NCH = TP // CH
ROW_LO = 112
X_LO = 128
ROW_HI = TP
XT = 128
NXT = TP // XT
HEADS = 4
DK = 128
DV = 256
KW = HEADS * DK
GW = HEADS * DV
RANK = 16
TAU = 16.0
WINDOWS = (2, 4, 8, 16)
PW = 1024
GC = 256
DFF = 8192
EPS = 1e-6
SHARD_IN = 1028
PAD_IN = 1152
N_CHIP = 4

LR = 0.001
B1 = 0.9
B2 = 0.999
AEPS = 1e-08
WD = 0.01
STEP = 10

VMEM_LIMIT = 60 * 1024 * 1024
ANY = pl.BlockSpec(memory_space=pl.ANY)
VMEM_FULL = pl.BlockSpec(memory_space=pltpu.VMEM)
MESH = pl.DeviceIdType.MESH


def _cp(sem=None):
    if sem is None:
        return pltpu.CompilerParams(vmem_limit_bytes=VMEM_LIMIT)
    return pltpu.CompilerParams(dimension_semantics=sem, vmem_limit_bytes=VMEM_LIMIT)


def _dot(a, b):
    return jnp.dot(a, b, preferred_element_type=F32)


def _dot_nt(a, b):
    return lax.dot_general(a, b, (((1,), (1,)), ((), ())), preferred_element_type=F32)


def _dot_tn(a, b):
    return lax.dot_general(a, b, (((0,), (0,)), ((), ())), preferred_element_type=F32)


def _sds(shape, dtype):
    return jax.ShapeDtypeStruct(shape, dtype)


def _embed_norm(x, meta_full, w, dep=None):
    def body(x_ref, meta_ref, w_ref, dep_ref, h_ref, u_ref):
        i = pl.program_id(0)

        @pl.when(i == 0)
        def _():
            h_ref[...] = jnp.zeros_like(h_ref)
            h_ref[ROW_LO:X_LO, :] = meta_ref[...]

        @pl.when(i >= 1)
        def _():
            h_ref[...] = x_ref[...]

        h = h_ref[...]
        r = lax.rsqrt(jnp.mean(h * h, axis=-1, keepdims=True) + EPS)
        u_ref[...] = ((h * r) * w_ref[...]).astype(BF16)

    return pl.pallas_call(
        body, name="embed_norm1", grid=(NXT,),
        in_specs=[pl.BlockSpec((XT, D), lambda i: (jnp.maximum(i - 1, 0), 0)),
                  pl.BlockSpec((N_META, D), lambda i: (0, 0)),
                  pl.BlockSpec((1, D), lambda i: (0, 0)), ANY],
        out_specs=[pl.BlockSpec((XT, D), lambda i: (i, 0)), pl.BlockSpec((XT, D), lambda i: (i, 0))],
        out_shape=[_sds((TP, D), F32), _sds((TP, D), BF16)],
        compiler_params=_cp(("arbitrary",)),
    )(x, meta_full, w, _dep(dep))


def _norm_rows(h, w, name):
    tr = 272

    def body(h_ref, w_ref, o_ref):
        hv = h_ref[...]
        r = lax.rsqrt(jnp.mean(hv * hv, axis=-1, keepdims=True) + EPS)
        o_ref[...] = ((hv * r) * w_ref[...]).astype(BF16)

    return pl.pallas_call(
        body, name=name, grid=(TP // tr,),
        in_specs=[pl.BlockSpec((tr, D), lambda i: (i, 0)), pl.BlockSpec((1, D), lambda i: (0, 0))],
        out_specs=pl.BlockSpec((tr, D), lambda i: (i, 0)),
        out_shape=_sds((TP, D), BF16),
        compiler_params=_cp(("arbitrary",)),
    )(h, w)


def _loss_head(h2, target, fw):
    def body(h_ref, t_ref, w_ref, dh_ref, dhb_ref, dw_ref, loss_ref):
        i = pl.program_id(0)

        @pl.when(i == 0)
        def _():
            dw_ref[...] = jnp.zeros_like(dw_ref)
            loss_ref[...] = jnp.zeros_like(loss_ref)

        h = h_ref[...]
        w = w_ref[...]
        r = lax.rsqrt(jnp.mean(h * h, axis=-1, keepdims=True) + EPS)
        xh = h * r
        y = xh * w
        is_x = (i >= 1).astype(F32)
        diff = (y - t_ref[...]) * is_x
        loss_ref[...] += jnp.sum(diff * diff) * (0.5 / D)
        dy = diff * (1.0 / D)
        dw_ref[...] += jnp.sum(dy * xh, axis=0, keepdims=True)
        gx = dy * w
        dh = r * (gx - xh * jnp.mean(gx * xh, axis=-1, keepdims=True))
        dh_ref[...] = dh
        dhb_ref[...] = dh.astype(BF16)

    return pl.pallas_call(
        body, name="loss_head", grid=(NXT,),
        in_specs=[pl.BlockSpec((XT, D), lambda i: (i, 0)),
                  pl.BlockSpec((XT, D), lambda i: (jnp.maximum(i - 1, 0), 0)),
                  pl.BlockSpec((1, D), lambda i: (0, 0))],
        out_specs=[pl.BlockSpec((XT, D), lambda i: (i, 0)), pl.BlockSpec((XT, D), lambda i: (i, 0)),
                   pl.BlockSpec((1, D), lambda i: (0, 0)), pl.BlockSpec((8, 128), lambda i: (0, 0))],
        out_shape=[_sds((TP, D), F32), _sds((TP, D), BF16), _sds((1, D), F32), _sds((8, 128), F32)],
        compiler_params=_cp(("arbitrary",)),
    )(h2, target, fw)


def _norm_bwd(dn, h, dres, w, name, dep=None):
    tr = 272

    def body(dn_ref, h_ref, dres_ref, w_ref, dep_ref, o_ref, ob_ref, dw_ref):
        @pl.when(pl.program_id(0) == 0)
        def _():
            dw_ref[...] = jnp.zeros_like(dw_ref)

        hv = h_ref[...]
        dnv = dn_ref[...]
        r = lax.rsqrt(jnp.mean(hv * hv, axis=-1, keepdims=True) + EPS)
        xh = hv * r
        dw_ref[...] += jnp.sum(dnv * xh, axis=0, keepdims=True)
        gx = dnv * w_ref[...]
        dh = dres_ref[...] + r * (gx - xh * jnp.mean(gx * xh, axis=-1, keepdims=True))
        o_ref[...] = dh
        ob_ref[...] = dh.astype(BF16)

    row = pl.BlockSpec((tr, D), lambda i: (i, 0))
    vec = pl.BlockSpec((1, D), lambda i: (0, 0))
    return pl.pallas_call(
        body, name=name, grid=(TP // tr,),
        in_specs=[row, row, row, vec, ANY], out_specs=[row, row, vec],
        out_shape=[_sds((TP, D), F32), _sds((TP, D), BF16), _sds((1, D), F32)],
        compiler_params=_cp(("arbitrary",)),
    )(dn, h, dres, w, _dep(dep))


def _input_grad(du, h0, dh1, w, dep=None):
    def body(du_ref, h_ref, dres_ref, w_ref, dep_ref, gx_ref, gm_ref, dw_ref):
        i = pl.program_id(0)

        @pl.when(i == 0)
        def _():
            dw_ref[...] = jnp.zeros_like(dw_ref)

        hv = h_ref[...]
        dnv = du_ref[...]
        r = lax.rsqrt(jnp.mean(hv * hv, axis=-1, keepdims=True) + EPS)
        xh = hv * r
        dw_ref[...] += jnp.sum(dnv * xh, axis=0, keepdims=True)
        g = dnv * w_ref[...]
        dh = dres_ref[...] + r * (g - xh * jnp.mean(g * xh, axis=-1, keepdims=True))

        @pl.when(i == 0)
        def _():
            gm_ref[...] = dh[ROW_LO:X_LO, :]

        @pl.when(i >= 1)
        def _():
            gx_ref[...] = dh

    row = pl.BlockSpec((XT, D), lambda i: (i, 0))
    vec = pl.BlockSpec((1, D), lambda i: (0, 0))
    return pl.pallas_call(
        body, name="input_grad", grid=(NXT,),
        in_specs=[row, row, row, vec, ANY],
        out_specs=[pl.BlockSpec((XT, D), lambda i: (jnp.maximum(i - 1, 0), 0)),
                   pl.BlockSpec((N_META, D), lambda i: (0, 0)), vec],
        out_shape=[_sds((SEQ, D), F32), _sds((N_META, D), F32), _sds((1, D), F32)],
        compiler_params=_cp(("arbitrary",)),
    )(du, h0, dh1, w, _dep(dep))


def _in_proj(u, wg):
    def body(u_ref, w_ref, o_ref):
        o_ref[0] = _dot(u_ref[...], w_ref[0])

    return pl.pallas_call(
        body, name="in_proj", grid=(N_CHIP,),
        in_specs=[VMEM_FULL, pl.BlockSpec((1, D, PAD_IN), lambda k: (k, 0, 0))],
        out_specs=pl.BlockSpec((1, TP, PAD_IN), lambda k: (k, 0, 0)),
        out_shape=_sds((N_CHIP, TP, PAD_IN), F32),
        compiler_params=_cp(("arbitrary",)),
    )(u, wg)


def _out_proj(og, op, wout, h0, dep=None):
    tn = 512

    def body(og_ref, op_ref, w_ref, h_ref, dep_ref, o_ref):
        acc = _dot(og_ref[...], w_ref[0:GW, :]) + _dot(op_ref[...], w_ref[GW:D, :])
        o_ref[...] = h_ref[...] + acc

    return pl.pallas_call(
        body, name="out_proj", grid=(D // tn,),
        in_specs=[VMEM_FULL, VMEM_FULL, pl.BlockSpec((D, tn), lambda j: (0, j)),
                  pl.BlockSpec((TP, tn), lambda j: (0, j)), ANY],
        out_specs=pl.BlockSpec((TP, tn), lambda j: (0, j)),
        out_shape=_sds((TP, D), F32),
        compiler_params=_cp(("arbitrary",)),
    )(og, op, wout, h0, _dep(dep))


def _mlp_up(n2, w1g, part, prev=None, dep=None):
    tn = 1024
    per = D // tn

    def body(n_ref, w_ref, dep_ref, *rest):
        zr_ref, a_ref, token = rest[-3:]
        z = jnp.maximum(_dot(n_ref[...], w_ref[0]), 0.0)
        zr_ref[...] = z.astype(BF16)
        a_ref[...] = (z * z).astype(BF16)
        token[...] = jnp.zeros_like(token)

    col = pl.BlockSpec((TP, tn), lambda k, j: (0, (2 * part + k) * per + j))
    return pl.pallas_call(
        body, name="mlp_up_%d" % part, grid=(N_CHIP // 2, per),
        in_specs=[VMEM_FULL, pl.BlockSpec((1, D, tn), lambda k, j: (2 * part + k, 0, j)), ANY] + ([ANY, ANY] if prev else []),
        out_specs=[col, col, pl.BlockSpec((8, 128), lambda k, j: (0, 0))],
        out_shape=[_sds((TP, DFF), BF16), _sds((TP, DFF), BF16), _sds((8, 128), F32)],
        input_output_aliases={3: 0, 4: 1} if prev else {},
        compiler_params=_cp(("arbitrary", "arbitrary")),
    )(n2, w1g, _dep(dep), *(prev or ()))


def _mlp_down(a, w2, h1):
    tk = 1024
    nk = DFF // tk

    def body(a_ref, w_ref, h_ref, o_ref, acc_ref):
        k = pl.program_id(0)

        @pl.when(k == 0)
        def _():
            pltpu.sync_copy(h_ref, acc_ref)

        acc_ref[...] += _dot(a_ref[...], w_ref[...])

        @pl.when(k == nk - 1)
        def _():
            pltpu.sync_copy(acc_ref, o_ref)

    return pl.pallas_call(
        body, name="mlp_down", grid=(nk,),
        in_specs=[pl.BlockSpec((TP, tk), lambda k: (0, k)), pl.BlockSpec((tk, D), lambda k: (k, 0)), ANY],
        out_specs=ANY,
        out_shape=_sds((TP, D), F32),
        scratch_shapes=[pltpu.VMEM((TP, D), F32)],
        compiler_params=_cp(("arbitrary",)),
    )(a, w2, h1)


def _mlp_dz(dh2b, w2, zr, dep=None):
    tn = 1024

    def body(d_ref, w_ref, z_ref, dep_ref, o_ref):
        da = _dot_nt(d_ref[...], w_ref[...])
        o_ref[...] = (da * (2.0 * z_ref[...].astype(F32))).astype(BF16)

    col = pl.BlockSpec((TP, tn), lambda j: (0, j))
    return pl.pallas_call(
        body, name="mlp_dz", grid=(DFF // tn,),
        in_specs=[VMEM_FULL, pl.BlockSpec((tn, D), lambda j: (j, 0)), col, ANY],
        out_specs=col,
        out_shape=_sds((TP, DFF), BF16),
        compiler_params=_cp(("arbitrary",)),
    )(dh2b, w2, zr, _dep(dep))


def _grad_w2(a, dh2b):
    tm = 1024

    def body(a_ref, d_ref, o_ref):
        o_ref[...] = _dot_tn(a_ref[...], d_ref[...])

    return pl.pallas_call(
        body, name="grad_w2", grid=(DFF // tm,),
        in_specs=[pl.BlockSpec((TP, tm), lambda j: (0, j)), VMEM_FULL],
        out_specs=pl.BlockSpec((tm, D), lambda j: (j, 0)),
        out_shape=_sds((DFF, D), F32),
        compiler_params=_cp(("arbitrary",)),
    )(a, dh2b)


def _dep(token):
    return jnp.zeros((8, 128), F32) if token is None else token


def _grad_w1(n2, dz, dep=None):
    tn = 1024
    per = D // tn

    def body(n_ref, d_ref, dep_ref, o_ref):
        o_ref[0] = _dot_tn(n_ref[...], d_ref[...])

    return pl.pallas_call(
        body, name="grad_w1", grid=(N_CHIP, per),
        in_specs=[VMEM_FULL, pl.BlockSpec((TP, tn), lambda k, j: (0, k * per + j)), ANY],
        out_specs=pl.BlockSpec((1, D, tn), lambda k, j: (k, 0, j)),
        out_shape=_sds((N_CHIP, D, D), F32),
        compiler_params=_cp(("arbitrary", "arbitrary")),
    )(n2, dz, _dep(dep))


def _mlp_dn(dz, w1g, dep=None):
    tk = 1024
    per = D // tk
    nk = DFF // tk

    def body(d_ref, w_ref, dep_ref, o_ref, acc_ref):
        k = pl.program_id(0)
        part = _dot_nt(d_ref[...], w_ref[0])

        @pl.when(k == 0)
        def _():
            acc_ref[...] = part

        @pl.when(k > 0)
        def _():
            acc_ref[...] += part

        @pl.when(k == nk - 1)
        def _():
            pltpu.sync_copy(acc_ref, o_ref)

    return pl.pallas_call(
        body, name="mlp_dn", grid=(nk,),
        in_specs=[pl.BlockSpec((TP, tk), lambda k: (0, k)),
                  pl.BlockSpec((1, D, tk), lambda k: (k // per, 0, k % per)), ANY],
        out_specs=ANY,
        out_shape=_sds((TP, D), F32),
        scratch_shapes=[pltpu.VMEM((TP, D), F32)],
        compiler_params=_cp(("arbitrary",)),
    )(dz, w1g, _dep(dep))


def _mixed_grad(dh1b, wout):
    tn = 512

    def body(d_ref, w_ref, o_ref):
        o_ref[...] = _dot_nt(d_ref[...], w_ref[...])

    return pl.pallas_call(
        body, name="mixed_grad", grid=(D // tn,),
        in_specs=[VMEM_FULL, pl.BlockSpec((tn, D), lambda j: (j, 0))],
        out_specs=pl.BlockSpec((TP, tn), lambda j: (0, j)),
        out_shape=_sds((TP, D), F32),
        compiler_params=_cp(("arbitrary",)),
    )(dh1b, wout)


def _grad_wout(og, op, dh1b):
    tm = 512

    def body(og_ref, op_ref, d_ref, o_ref):
        j = pl.program_id(0)

        @pl.when(j < 2)
        def _():
            o_ref[0] = _dot_tn(og_ref[...], d_ref[...])

        @pl.when(j >= 2)
        def _():
            o_ref[0] = _dot_tn(op_ref[...], d_ref[...])

    return pl.pallas_call(
        body, name="grad_wout", grid=(N_CHIP,),
        in_specs=[pl.BlockSpec((TP, tm), lambda j: (0, jnp.minimum(j, 1))),
                  pl.BlockSpec((TP, tm), lambda j: (0, jnp.maximum(j - 2, 0))), VMEM_FULL],
        out_specs=pl.BlockSpec((1, tm, D), lambda j: (j, 0, 0)),
        out_shape=_sds((N_CHIP, tm, D), F32),
        compiler_params=_cp(("arbitrary",)),
    )(og, op, dh1b)


def _in_grad(dq, dk, dv, dr, dglr, dpu, wg, dep=None):
    def body(dq_ref, dk_ref, dv_ref, dr_ref, dg_ref, dpu_ref, w_ref, dep_ref, o_ref):
        dv, dr, dg = dv_ref[...], dr_ref[...], dg_ref[...]
        head, tail = slice(0, GW), slice(GW, PAD_IN)
        o_ref[...] = (_dot_nt(dq_ref[...], w_ref[0, :, 0:KW]) + _dot_nt(dk_ref[...], w_ref[0, :, KW:GW])
                      + _dot_nt(dv[:, 0:128], w_ref[0, :, tail])
                      + _dot_nt(dv, w_ref[1, :, head]) + _dot_nt(dr[:, 0:128], w_ref[1, :, tail])
                      + _dot_nt(dr, w_ref[2, :, head]) + _dot_nt(dg, w_ref[2, :, tail])
                      + _dot_nt(dpu_ref[...], w_ref[3, :, head]) + _dot_nt(dg, w_ref[3, :, tail]))

    tn = 512
    return pl.pallas_call(
        body, name="in_grad", grid=(D // tn,),
        in_specs=[VMEM_FULL] * 6 + [pl.BlockSpec((N_CHIP, tn, PAD_IN), lambda j: (0, j, 0)), ANY],
        out_specs=pl.BlockSpec((TP, tn), lambda j: (0, j)),
        out_shape=_sds((TP, D), F32),
        compiler_params=_cp(("arbitrary",)),
    )(dq, dk, dv, dr, dglr, dpu, wg, _dep(dep))


def _grad_win(u, dq, dk, dv, dr, dglr, dpu, dep=None):
    tm = 512

    def body(u_ref, dq_hbm, dk_hbm, dv_hbm, dr_hbm, dg_hbm, dpu_hbm, dep_ref, o_ref, dp_ref, sem):
        k, m = pl.program_id(0), pl.program_id(1)
        head, tail = slice(0, GW), slice(GW, PAD_IN)
        pieces = [[(dq_hbm, slice(0, KW)), (dk_hbm, slice(KW, GW)), (dv_hbm.at[:, 0:128], tail)],
                  [(dv_hbm, head), (dr_hbm.at[:, 0:128], tail)],
                  [(dr_hbm, head), (dg_hbm, tail)],
                  [(dpu_hbm, head), (dg_hbm, tail)]]

        def copies(kk):
            return [pltpu.make_async_copy(src, dp_ref.at[kk % 2, :, cols], sem.at[kk % 2, i])
                    for i, (src, cols) in enumerate(pieces[kk])]

        @pl.when((k == 0) & (m == 0))
        def _():
            for cp in copies(0):
                cp.start()

        for kk in range(N_CHIP):
            @pl.when((k == kk) & (m == 0))
            def _(kk=kk):
                for cp in copies(kk):
                    cp.wait()
                if kk + 1 < N_CHIP:
                    for cp in copies(kk + 1):
                        cp.start()

        g = _dot_tn(u_ref[...], dp_ref[k % 2])
        lane = lax.broadcasted_iota(jnp.int32, (tm, PAD_IN), 1)
        for kk in range(N_CHIP):
            @pl.when(k == kk)
            def _(kk=kk):
                if kk == 0:
                    nat = g
                elif kk < 3:
                    nat = pltpu.roll(g, PAD_IN - 4 * kk, 1)
                else:
                    nat = jnp.where(lane < 4, pltpu.roll(g, PAD_IN - (GW + 12), 1), pltpu.roll(g, 4, 1))
                o_ref[0] = nat[:, 0:SHARD_IN]

    return pl.pallas_call(
        body, name="grad_win", grid=(N_CHIP, D // tm),
        in_specs=[pl.BlockSpec((TP, tm), lambda k, m: (0, m))] + [ANY] * 7,
        out_specs=pl.BlockSpec((1, tm, SHARD_IN), lambda k, m: (k, m, 0)),
        out_shape=_sds((N_CHIP, D, SHARD_IN), F32),
        scratch_shapes=[pltpu.VMEM((2, TP, PAD_IN), BF16), pltpu.SemaphoreType.DMA((2, 3))],
        compiler_params=_cp(("arbitrary", "arbitrary")),
    )(u, dq, dk, dv, dr, dglr, dpu, _dep(dep))


def _split3(x):
    hi = x.astype(BF16)
    r1 = x - hi.astype(F32)
    mid = r1.astype(BF16)
    lo = (r1 - mid.astype(F32)).astype(BF16)
    return hi, mid, lo


def _tri_sum(tri, x):
    hi, mid, lo = _split3(x)
    return _dot(tri, hi) + _dot(tri, mid) + _dot(tri, lo)


def _gla_common(n, glr, gw2, gb):
    rows = n * CH + lax.broadcasted_iota(jnp.int32, (CH, 1), 0)
    valid = (rows >= ROW_LO) & (rows < ROW_HI)
    g_raw = _dot(glr.astype(BF16), gw2.astype(BF16)) + gb
    logsig = jnp.minimum(g_raw, 0.0) - jnp.log(1.0 + jnp.exp(-jnp.abs(g_raw)))
    logg = jnp.where(valid, logsig * (1.0 / TAU), 0.0)
    ci = lax.broadcasted_iota(jnp.int32, (CH, CH), 0)
    si = lax.broadcasted_iota(jnp.int32, (CH, CH), 1)
    lower = ci >= si
    G = _tri_sum(lower.astype(BF16), logg)
    Gl = G[CH - 1:CH, :]
    return valid, g_raw, lower, G, Gl


def _p_specs(index):
    def spec(width, shard, col):
        return pl.BlockSpec((1, CH, width), lambda s: (shard, index(s), col))

    return [spec(KW, 0, 0), spec(KW, 0, 1), spec(GW, 1, 0), spec(128, 0, 8), spec(GW, 2, 0), spec(128, 1, 8),
            spec(128, 2, 8), spec(128, 3, 8)]


def _p_load(q_ref, k_ref, vm_ref, vh_ref, rm_ref, rh_ref, ga_ref, gb_ref):
    def joined(main, head):
        return jnp.concatenate([main[:, 0:128] + head, main[:, 128:]], axis=1)

    return q_ref[0], k_ref[0], joined(vm_ref[0], vh_ref[0]), joined(rm_ref[0], rh_ref[0]), ga_ref[0] + gb_ref[0]


def _gla_fwd(P, gw2, gb, gnw, dep=None):
    scale = DK ** -0.5

    def body(p0, p1, p2, p3, p4, p5, p6, p7, gw2_ref, gb_ref, gnw_ref, dep_ref, o_ref, og_ref, sp_ref, st_ref):
        n = pl.program_id(0)

        @pl.when(n == 0)
        def _():
            st_ref[...] = jnp.zeros_like(st_ref)

        q_all, k_all, v_all, r_all, glr = _p_load(p0, p1, p2, p3, p4, p5, p6, p7)
        _, _, lower, G, Gl = _gla_common(n, glr, gw2_ref[...], gb_ref[...])
        eG = jnp.exp(G)
        eN = jnp.exp(-G)
        eE = jnp.exp(Gl - G)
        dec = jnp.exp(Gl)
        gnw_v = gnw_ref[...]
        for h in range(HEADS):
            ks = slice(h * DK, (h + 1) * DK)
            vs = slice(h * DV, (h + 1) * DV)
            kh = k_all[:, ks]
            vh = v_all[:, vs].astype(BF16)
            qd = ((q_all[:, ks] * scale) * eG[:, ks]).astype(BF16)
            ki = (kh * eN[:, ks]).astype(BF16)
            ke = (kh * eE[:, ks]).astype(BF16)
            st = st_ref[h]
            a = jnp.where(lower, _dot_nt(qd, ki), 0.0).astype(BF16)
            o = _dot(a, vh) + _dot_nt(qd, st.astype(BF16))
            sp_ref[0, h] = st
            st_ref[h] = st * dec[:, ks] + _dot_tn(vh, ke)
            o_ref[:, vs] = o
            rs = lax.rsqrt(jnp.mean(o * o, axis=-1, keepdims=True) + EPS)
            rv = r_all[:, vs]
            gate = rv / (1.0 + jnp.exp(-rv))
            og_ref[:, vs] = (((o * rs) * gnw_v) * gate).astype(BF16)

    rv_ = pl.BlockSpec((CH, GW), lambda n: (n, 0))

    def full(shape):
        return pl.BlockSpec(shape, lambda n: tuple(0 for _ in shape))

    return pl.pallas_call(
        body, name="gla_fwd", grid=(NCH,),
        in_specs=_p_specs(lambda n: n) + [full((128, KW)), full((1, KW)), full((1, DV)), ANY],
        out_specs=[rv_, rv_, pl.BlockSpec((1, HEADS, DV, DK), lambda n: (n, 0, 0, 0))],
        out_shape=[_sds((TP, GW), F32), _sds((TP, GW), BF16), _sds((NCH, HEADS, DV, DK), F32)],
        scratch_shapes=[pltpu.VMEM((HEADS, DV, DK), F32)],
        compiler_params=_cp(("arbitrary",)),
    )(*([P] * 8), gw2, gb, gnw, _dep(dep))


def _gla_bwd(dog, o, P, gw2, gb, gnw, sp, dep=None):
    scale = DK ** -0.5

    def body(dog_ref, o_ref, p0, p1, p2, p3, p4, p5, p6, p7, gw2_ref, gb_ref, gnw_ref, sp_ref, dep_ref,
             dq_ref, dk_ref, dv_ref, dr_ref, dglr_ref, dgw2_ref, dgb_ref, dgnw_ref, ds_ref):
        step = pl.program_id(0)
        n = NCH - 1 - step

        @pl.when(step == 0)
        def _():
            ds_ref[...] = jnp.zeros_like(ds_ref)
            dgw2_ref[...] = jnp.zeros_like(dgw2_ref)
            dgb_ref[...] = jnp.zeros_like(dgb_ref)
            dgnw_ref[...] = jnp.zeros_like(dgnw_ref)

        q_all, k_all, v_all, r_all, glr_v = _p_load(p0, p1, p2, p3, p4, p5, p6, p7)
        gw2_b = gw2_ref[...].astype(BF16)
        valid, g_raw, lower, G, Gl = _gla_common(n, glr_v, gw2_ref[...], gb_ref[...])
        upper = lax.broadcasted_iota(jnp.int32, (CH, CH), 0) <= lax.broadcasted_iota(jnp.int32, (CH, CH), 1)
        eG = jnp.exp(G)
        eN = jnp.exp(-G)
        eE = jnp.exp(Gl - G)
        dec = jnp.exp(Gl)
        gnw_v = gnw_ref[...]
        last = lax.broadcasted_iota(jnp.int32, (CH, 1), 0) == CH - 1
        dgnw_acc = jnp.zeros((1, DV), F32)
        dG_parts = []
        for h in range(HEADS):
            ks = slice(h * DK, (h + 1) * DK)
            vs = slice(h * DV, (h + 1) * DV)
            oh = o_ref[:, vs]
            rv = r_all[:, vs]
            dg = dog_ref[:, vs]
            sig = 1.0 / (1.0 + jnp.exp(-rv))
            gate = rv * sig
            rs = lax.rsqrt(jnp.mean(oh * oh, axis=-1, keepdims=True) + EPS)
            ohat = oh * rs
            dr_ref[:, vs] = ((dg * (ohat * gnw_v)) * (sig * (1.0 + rv * (1.0 - sig)))).astype(BF16)
            don = dg * gate
            dgnw_acc = dgnw_acc + jnp.sum(don * ohat, axis=0, keepdims=True)
            gxn = don * gnw_v
            do = (rs * (gxn - ohat * jnp.mean(gxn * ohat, axis=-1, keepdims=True))).astype(BF16)
            kh = k_all[:, ks]
            vh = v_all[:, vs].astype(BF16)
            qd_f = (q_all[:, ks] * scale) * eG[:, ks]
            ki_f = kh * eN[:, ks]
            ke_f = kh * eE[:, ks]
            qd, ki, ke = qd_f.astype(BF16), ki_f.astype(BF16), ke_f.astype(BF16)
            spt = sp_ref[0, h]
            dst = ds_ref[h]
            dst_b = dst.astype(BF16)
            a_t = jnp.where(upper, _dot_nt(ki, qd), 0.0).astype(BF16)
            da = jnp.where(lower, _dot_nt(do, vh), 0.0).astype(BF16)
            da_t = jnp.where(upper, _dot_nt(vh, do), 0.0).astype(BF16)
            dv_ref[:, vs] = (_dot(a_t, do) + _dot_nt(ke, dst_b)).astype(BF16)
            dqd = _dot(da, ki) + _dot(do, spt.astype(BF16))
            dki = _dot(da_t, qd)
            dke = _dot(vh, dst_b)
            ddec = jnp.sum(spt * dst, axis=0, keepdims=True)
            ds_ref[h] = dst * dec[:, ks] + _dot_tn(do, qd)
            dq_ref[:, ks] = ((dqd * eG[:, ks]) * scale).astype(BF16)
            dk_ref[:, ks] = (dki * eN[:, ks] + dke * eE[:, ks]).astype(BF16)
            dke_ke = dke * ke_f
            dG = dqd * qd_f - dki * ki_f - dke_ke
            dGl = jnp.sum(dke_ke, axis=0, keepdims=True) + ddec * dec[:, ks]
            dG_parts.append(dG + jnp.where(last, dGl, 0.0))
        dgnw_ref[...] += dgnw_acc
        dG_all = jnp.concatenate(dG_parts, axis=1)
        dlogg = jnp.where(valid, _tri_sum(upper.astype(BF16), dG_all), 0.0)
        dg_raw = (dlogg * (1.0 / TAU)) * (1.0 / (1.0 + jnp.exp(g_raw)))
        dgb_ref[...] += jnp.sum(dg_raw, axis=0, keepdims=True)
        dg_b = dg_raw.astype(BF16)
        dgw2_ref[...] += _dot_tn(glr_v.astype(BF16), dg_b)
        dglr_ref[...] = _dot_nt(dg_b, gw2_b).astype(BF16)

    def back(s):
        return NCH - 1 - s

    rk = pl.BlockSpec((CH, KW), lambda s: (back(s), 0))
    rv_ = pl.BlockSpec((CH, GW), lambda s: (back(s), 0))
    rg = pl.BlockSpec((CH, 128), lambda s: (back(s), 0))

    def full(shape):
        return pl.BlockSpec(shape, lambda s: tuple(0 for _ in shape))

    return pl.pallas_call(
        body, name="gla_bwd", grid=(NCH,),
        in_specs=[rv_, rv_] + _p_specs(back) + [full((128, KW)), full((1, KW)), full((1, DV)),
                  pl.BlockSpec((1, HEADS, DV, DK), lambda s: (back(s), 0, 0, 0)), ANY],
        out_specs=[rk, rk, rv_, rv_, rg, full((128, KW)), full((1, KW)), full((1, DV))],
        out_shape=[_sds((TP, KW), BF16), _sds((TP, KW), BF16), _sds((TP, GW), BF16), _sds((TP, GW), BF16),
                   _sds((TP, 128), BF16), _sds((128, KW), F32), _sds((1, KW), F32), _sds((1, DV), F32)],
        scratch_shapes=[pltpu.VMEM((HEADS, DV, DK), F32)],
        compiler_params=_cp(("arbitrary",)),
    )(dog, o, *([P] * 8), gw2, gb, gnw, sp, _dep(dep))


POOL_TR = 128
HALO = 16


def _pool_counts(base, nrows):
    rows = base + lax.broadcasted_iota(jnp.int32, (nrows, 1), 0)
    valid = (rows >= ROW_LO) & (rows < ROW_HI)
    t1 = (rows - ROW_LO + 1).astype(F32)
    cnts = [jnp.clip(t1, 1.0, float(w)) for w in WINDOWS]
    return valid, cnts


def _pool_fwd(P, pw, ps, dep=None):
    def body(cur_ref, prev_ref, pw_ref, ps_ref, dep_ref, y_ref, op_ref):
        i = pl.program_id(0)
        cur = cur_ref[0]
        full = jnp.concatenate([prev_ref[0], cur], axis=0)
        s2 = full + pltpu.roll(full, 1, 0)
        s4 = s2 + pltpu.roll(s2, 2, 0)
        s8 = s4 + pltpu.roll(s4, 4, 0)
        s16 = s8 + pltpu.roll(s8, 8, 0)
        valid, cnts = _pool_counts(i * POOL_TR, POOL_TR)
        for g, s in enumerate((s2, s4, s8, s16)):
            cs = slice(g * GC, (g + 1) * GC)
            y = s[HALO:, cs] / cnts[g] - cur[:, cs]
            yb = jnp.where(valid, y, 0.0).astype(BF16)
            y_ref[:, cs] = yb
            op_ref[:, cs] = (_dot(yb, pw_ref[g].astype(BF16)) * ps_ref[:, cs]).astype(BF16)

    row = pl.BlockSpec((POOL_TR, PW), lambda i: (i, 0))
    per = POOL_TR // HALO
    return pl.pallas_call(
        body, name="pool_fwd", grid=(TP // POOL_TR,),
        in_specs=[pl.BlockSpec((1, POOL_TR, PW), lambda i: (3, i, 0)),
                  pl.BlockSpec((1, HALO, PW), lambda i: (3, jnp.maximum(i * per - 1, 0), 0)),
                  pl.BlockSpec((4, GC, GC), lambda i: (0, 0, 0)), pl.BlockSpec((1, PW), lambda i: (0, 0)), ANY],
        out_specs=[row, row],
        out_shape=[_sds((TP, PW), BF16), _sds((TP, PW), BF16)],
        compiler_params=_cp(("arbitrary",)),
    )(P, P, pw, ps, _dep(dep))


def _pool_bwd(dop, y, pw, ps, dep=None):
    nblk = TP // HALO

    def body(cur_ref, nxt_ref, y_ref, pw_ref, ps_ref, dep_ref, dpu_ref, dpw_ref, dps_ref):
        i = pl.program_id(0)

        @pl.when(i == 0)
        def _():
            dpw_ref[...] = jnp.zeros_like(dpw_ref)
            dps_ref[...] = jnp.zeros_like(dps_ref)

        n_all = POOL_TR + HALO
        dcur = cur_ref[...]
        dall = jnp.concatenate([dcur, nxt_ref[...]], axis=0)
        valid, cnts = _pool_counts(i * POOL_TR, n_all)
        for g in range(4):
            cs = slice(g * GC, (g + 1) * GC)
            pwb = pw_ref[g].astype(BF16)
            yb = y_ref[:, cs]
            dyw = (dall[:, cs] * ps_ref[:, cs]).astype(BF16)
            dps_ref[:, cs] += jnp.sum(dcur[:, cs] * _dot(yb, pwb), axis=0, keepdims=True)
            dpw_ref[g] += _dot_tn(yb, dyw[0:POOL_TR, :])
            dyv = jnp.where(valid, _dot_nt(dyw, pwb), 0.0)
            e = dyv / cnts[g]
            w = WINDOWS[g]
            sh = 1
            while sh < w:
                e = e + pltpu.roll(e, n_all - sh, 0)
                sh *= 2
            dpu_ref[:, cs] = (e[0:POOL_TR, :] - dyv[0:POOL_TR, :]).astype(BF16)

    row = pl.BlockSpec((POOL_TR, PW), lambda i: (i, 0))
    per = POOL_TR // HALO
    return pl.pallas_call(
        body, name="pool_bwd", grid=(TP // POOL_TR,),
        in_specs=[pl.BlockSpec((POOL_TR, PW), lambda i: (i, 1)),
                  pl.BlockSpec((HALO, PW), lambda i: (jnp.minimum(i * per + per, nblk - 1), 1)),
                  row, pl.BlockSpec((4, GC, GC), lambda i: (0, 0, 0)), pl.BlockSpec((1, PW), lambda i: (0, 0)), ANY],
        out_specs=[row, pl.BlockSpec((4, GC, GC), lambda i: (0, 0, 0)), pl.BlockSpec((1, PW), lambda i: (0, 0))],
        out_shape=[_sds((TP, PW), BF16), _sds((4, GC, GC), F32), _sds((1, PW), F32)],
        compiler_params=_cp(("arbitrary",)),
    )(dop, dop, y, pw, ps, _dep(dep))


def _place():
    x, y, c = lax.axis_index("x"), lax.axis_index("y"), lax.axis_index("c")
    chips = [(1 - x, y), (x, 1 - y), (1 - x, 1 - y)]
    return x, y, c, chips


HBM = pl.BlockSpec(memory_space=pltpu.HBM)
SEM = pl.BlockSpec(memory_space=pltpu.SEMAPHORE)
EFFECT = pltpu.SideEffectType.DATAFLOW_SIDE_EFFECTING


def _cast_into(w, place, cols_out, name, dep=None):
    rows, cols = w.shape
    tr = 256

    def body(p_ref, w_ref, dep_ref, o_ref):
        if cols_out != cols:
            o_ref[0] = jnp.zeros((tr, cols_out), BF16)
            o_ref[0, :, 0:cols] = w_ref[...].astype(BF16)
        else:
            o_ref[0] = w_ref[...].astype(BF16)

    grid_spec = pltpu.PrefetchScalarGridSpec(
        num_scalar_prefetch=1, grid=(rows // tr,),
        in_specs=[pl.BlockSpec((tr, cols), lambda i, p: (i, 0)), ANY],
        out_specs=pl.BlockSpec((1, tr, cols_out), lambda i, p: (p[0], i, 0)))
    return pl.pallas_call(
        body, name=name, grid_spec=grid_spec,
        out_shape=_sds((N_CHIP, rows, cols_out), BF16),
        compiler_params=_cp(("arbitrary",)),
    )(place, w, _dep(dep))


def _cast_win(w, place, dep=None):
    rows, cols = w.shape
    tr = 256

    def body(p_ref, w_ref, dep_ref, o_ref, t_ref):
        t_ref[...] = jnp.zeros_like(t_ref)
        t_ref[:, 0:cols] = w_ref[...]
        t = t_ref[...]
        lane = lax.broadcasted_iota(jnp.int32, (tr, PAD_IN), 1)
        for kk in range(N_CHIP):
            @pl.when(p_ref[0] == kk)
            def _(kk=kk):
                if kk == 0:
                    placed = t
                elif kk < 3:
                    placed = pltpu.roll(t, 4 * kk, 1)
                else:
                    pool = pltpu.roll(t, PAD_IN - 4, 1)
                    gate = pltpu.roll(t, GW + 12, 1)
                    placed = jnp.where(lane < GW, pool, jnp.where((lane >= GW + 12) & (lane < GW + 16), gate, 0.0))
                o_ref[0] = placed.astype(BF16)

    grid_spec = pltpu.PrefetchScalarGridSpec(
        num_scalar_prefetch=1, grid=(rows // tr,),
        in_specs=[pl.BlockSpec((tr, cols), lambda i, p: (i, 0)), ANY],
        out_specs=pl.BlockSpec((1, tr, PAD_IN), lambda i, p: (p[0], i, 0)),
        scratch_shapes=[pltpu.VMEM((tr, PAD_IN), F32)])
    return pl.pallas_call(
        body, name="cast_win", grid_spec=grid_spec,
        out_shape=_sds((N_CHIP, rows, PAD_IN), BF16),
        compiler_params=_cp(("arbitrary",)),
    )(place, w, _dep(dep))


def _half_rows(ref, k, which):
    h = ref.shape[1] // 2
    return ref.at[k, pl.ds(pl.multiple_of(which * h, 8), h), :]


def _sent_rows(ref, k, which, whole):
    return ref.at[k] if whole else _half_rows(ref, k, which)


def _gather_start(ws, name, whole=None):
    n = len(ws)
    whole = whole or [False] * n

    def body(*refs):
        ins = refs[:n]
        ssems = refs[n:2 * n]
        rsems = refs[2 * n:3 * n]
        token = refs[4 * n]
        x, y, c, chips = _place()
        me = 2 * x + y
        for w in range(n):
            blk = _sent_rows(ins[w], me, c, whole[w])
            for j, chip in enumerate(chips):
                pltpu.make_async_remote_copy(src_ref=blk, dst_ref=blk, send_sem=ssems[w].at[j], recv_sem=rsems[w].at[j],
                                             device_id=(*chip, c), device_id_type=MESH).start()
        token[...] = jnp.zeros_like(token)

    sem3 = pltpu.SemaphoreType.DMA((3,))
    outs = pl.pallas_call(
        body, name=name,
        out_shape=tuple([sem3] * (2 * n) + [pltpu.HBM(w.shape, w.dtype) for w in ws] + [_sds((8, 128), F32)]),
        in_specs=(HBM,) * n, out_specs=(SEM,) * (2 * n) + (HBM,) * n + (VMEM_FULL,),
        input_output_aliases={w: 2 * n + w for w in range(n)},
        compiler_params=pltpu.CompilerParams(has_side_effects=EFFECT),
    )(*[pltpu.with_memory_space_constraint(w, pltpu.HBM) for w in ws])
    return outs[:n], outs[n:2 * n], outs[2 * n:3 * n], outs[3 * n]


def _gather_wait(w, ssem, rsem, after, name, whole=False):
    def body(w_ref, ssem_ref, rsem_ref, after_ref, out_ref):
        x, y, c, chips = _place()
        me = 2 * x + y
        mine = _sent_rows(w_ref, me, c, whole)
        for j, (cx, cy) in enumerate(chips):
            cp = pltpu.make_async_remote_copy(src_ref=mine, dst_ref=_sent_rows(w_ref, 2 * cx + cy, c, whole),
                                              send_sem=ssem_ref.at[j], recv_sem=rsem_ref.at[j],
                                              device_id=(cx, cy, c), device_id_type=MESH)
            cp.wait_send()
            cp.wait_recv()

    return pl.pallas_call(
        body, name=name, out_shape=pltpu.HBM(w.shape, w.dtype),
        in_specs=(HBM, SEM, SEM, ANY), out_specs=HBM, input_output_aliases={0: 0},
        compiler_params=pltpu.CompilerParams(has_side_effects=EFFECT),
    )(w, ssem, rsem, after)


def _gather_copies(ref, kind, ssem, rsem):
    x, y, c, _ = _place()
    xn, yn, sib = (1 - x, y, c), (x, 1 - y, c), (x, y, 1 - c)
    kx, ky, kd = 2 * (1 - x) + y, 2 * x + (1 - y), 2 * (1 - x) + (1 - y)
    half = ref.shape[1] // 2
    quarter = half // 2

    def piece(k, q):
        return ref.at[k, pl.ds(pl.multiple_of(c * half + q * quarter, 8), quarter), :]

    if kind == "d":
        blk = _half_rows(ref, 2 * x + y, c)
        pairs = [(blk, xn), (blk, yn)]
    elif kind == "r":
        pairs = [(piece(ky, 1), xn), (piece(kx, 0), yn)]
    elif kind == "fx":
        pairs = [(_half_rows(ref, kx, c), sib), (_half_rows(ref, ky, c), sib)]
    else:
        pairs = [(_half_rows(ref, kd, c), sib)]
    return [pltpu.make_async_remote_copy(src_ref=blk, dst_ref=blk, send_sem=ssem.at[i], recv_sem=rsem.at[i],
                                         device_id=to, device_id_type=MESH) for i, (blk, to) in enumerate(pairs)]


def _gather_step(name, arrs, waits, starts, sems_in=(), after=()):
    n, nw, ns = len(arrs), len(waits), len(starts)
    after = [a for a in after if a is not None] or [_dep(None)]

    def body(*refs):
        a_in = refs[:n]
        s_in = refs[n:n + 2 * nw]
        outs = refs[n + 2 * nw + len(after):]
        s_out = outs[:2 * ns]
        for i, (ai, kind) in enumerate(waits):
            for cp in _gather_copies(a_in[ai], kind, s_in[2 * i], s_in[2 * i + 1]):
                cp.wait_send()
                cp.wait_recv()
        for i, (ai, kind) in enumerate(starts):
            for cp in _gather_copies(a_in[ai], kind, s_out[2 * i], s_out[2 * i + 1]):
                cp.start()
        if ns:
            token = outs[2 * ns + n]
            token[...] = jnp.zeros_like(token)

    sem2 = pltpu.SemaphoreType.DMA((2,))
    flat_in = [s for pair in sems_in for s in pair]
    arrs = [pltpu.with_memory_space_constraint(a, pltpu.HBM) for a in arrs]
    outs = pl.pallas_call(
        body, name=name,
        out_shape=tuple([sem2] * (2 * ns) + [pltpu.HBM(a.shape, a.dtype) for a in arrs]
                        + ([_sds((8, 128), F32)] if ns else [])),
        in_specs=(HBM,) * n + (SEM,) * (2 * nw) + (ANY,) * len(after),
        out_specs=(SEM,) * (2 * ns) + (HBM,) * n + ((VMEM_FULL,) if ns else ()),
        input_output_aliases={i: 2 * ns + i for i in range(n)},
        compiler_params=pltpu.CompilerParams(has_side_effects=EFFECT),
    )(*arrs, *flat_in, *after)
    sems = [(outs[2 * i], outs[2 * i + 1]) for i in range(ns)]
    return sems, list(outs[2 * ns:2 * ns + n]), (outs[2 * ns + n] if ns else None)


def _rs_start(sb, name, after=None):
    _, half, cols = sb.shape

    def body(sb_ref, land_ref, after_ref, ssem, rsem, sb_out, land_out, token):
        x, y, c, chips = _place()
        for j, (cx, cy) in enumerate(chips):
            pltpu.make_async_remote_copy(src_ref=sb_ref.at[2 * cx + cy], dst_ref=land_ref.at[j], send_sem=ssem.at[j],
                                         recv_sem=rsem.at[j], device_id=(cx, cy, c), device_id_type=MESH).start()
        token[...] = jnp.zeros_like(token)

    sem3 = pltpu.SemaphoreType.DMA((3,))
    land = lax.empty((3, half, cols), BF16)
    return pl.pallas_call(
        body, name=name,
        out_shape=(sem3, sem3, pltpu.HBM(sb.shape, sb.dtype), pltpu.HBM(land.shape, land.dtype), _sds((8, 128), F32)),
        in_specs=(HBM, HBM, ANY), out_specs=(SEM, SEM, HBM, HBM, VMEM_FULL), input_output_aliases={0: 2, 1: 3},
        compiler_params=pltpu.CompilerParams(has_side_effects=EFFECT),
    )(pltpu.with_memory_space_constraint(sb, pltpu.HBM), pltpu.with_memory_space_constraint(land, pltpu.HBM), _dep(after))


def _rs_wait(items, after, name):
    n = len(items)

    def body(*refs):
        x, y, c, chips = _place()
        for i in range(n):
            sb_ref, land_ref, ssem_ref, rsem_ref = refs[4 * i:4 * i + 4]
            for j, (cx, cy) in enumerate(chips):
                cp = pltpu.make_async_remote_copy(src_ref=sb_ref.at[2 * cx + cy], dst_ref=land_ref.at[j],
                                                  send_sem=ssem_ref.at[j], recv_sem=rsem_ref.at[j],
                                                  device_id=(cx, cy, c), device_id_type=MESH)
                cp.wait_send()
                cp.wait_recv()

    outs = pl.pallas_call(
        body, name=name,
        out_shape=tuple(pltpu.HBM(a.shape, a.dtype) for it in items for a in it[:2]),
        in_specs=(HBM, HBM, SEM, SEM) * n + (ANY,), out_specs=(HBM,) * (2 * n),
        input_output_aliases={4 * i + k: 2 * i + k for i in range(n) for k in range(2)},
        compiler_params=pltpu.CompilerParams(has_side_effects=EFFECT),
    )(*[a for it in items for a in it], after)
    return [outs[2 * i + 1] for i in range(n)]


def _pair_copy(g_ref, land_ref, ssem, rsem):
    x, y, c, _ = _place()
    h = g_ref.shape[1] // 2
    src = g_ref.at[:, pl.ds(pl.multiple_of((1 - c) * h, 8), h), :]
    return pltpu.make_async_remote_copy(src_ref=src, dst_ref=land_ref, send_sem=ssem.at[0], recv_sem=rsem.at[0],
                                        device_id=(x, y, 1 - c), device_id_type=MESH)


def _pair_start(g, name):
    def body(g_ref, land_ref, ssem, rsem, g_out, land_out, token):
        _pair_copy(g_ref, land_ref, ssem, rsem).start()
        token[...] = jnp.zeros_like(token)

    sem1 = pltpu.SemaphoreType.DMA((1,))
    land = lax.empty((N_CHIP, g.shape[1] // 2, g.shape[2]), F32)
    return pl.pallas_call(
        body, name=name,
        out_shape=(sem1, sem1, pltpu.HBM(g.shape, g.dtype), pltpu.HBM(land.shape, land.dtype), _sds((8, 128), F32)),
        in_specs=(HBM, HBM), out_specs=(SEM, SEM, HBM, HBM, VMEM_FULL), input_output_aliases={0: 2, 1: 3},
        compiler_params=pltpu.CompilerParams(has_side_effects=EFFECT),
    )(pltpu.with_memory_space_constraint(g, pltpu.HBM), pltpu.with_memory_space_constraint(land, pltpu.HBM))


def _pair_wait(g, land, ssem, rsem, after, name):
    def body(g_ref, land_ref, ssem_ref, rsem_ref, after_ref, g_out, land_out):
        cp = _pair_copy(g_ref, land_ref, ssem_ref, rsem_ref)
        cp.wait_send()
        cp.wait_recv()

    return pl.pallas_call(
        body, name=name,
        out_shape=(pltpu.HBM(g.shape, g.dtype), pltpu.HBM(land.shape, land.dtype)),
        in_specs=(HBM, HBM, SEM, SEM, ANY), out_specs=(HBM, HBM), input_output_aliases={0: 0, 1: 1},
        compiler_params=pltpu.CompilerParams(has_side_effects=EFFECT),
    )(g, land, ssem, rsem, after)


def _pair_sum(g, rcv, place, name):
    _, rows, cols = g.shape
    half = rows // 2
    tr = 256
    nt = half // tr

    def body(p_ref, g_ref, r_ref, sb_ref, sf_ref):
        s = pl.program_id(1)
        tot = g_ref[0] + r_ref[0]
        sb_ref[0] = tot.astype(BF16)

        @pl.when(s == p_ref[0])
        def _():
            sf_ref[...] = tot

    grid_spec = pltpu.PrefetchScalarGridSpec(
        num_scalar_prefetch=1, grid=(nt, N_CHIP),
        in_specs=[pl.BlockSpec((1, tr, cols), lambda t, s, p: (s, p[1] * nt + t, 0)),
                  pl.BlockSpec((1, tr, cols), lambda t, s, p: (s, t, 0))],
        out_specs=[pl.BlockSpec((1, tr, cols), lambda t, s, p: (s, t, 0)),
                   pl.BlockSpec((tr, cols), lambda t, s, p: (t, 0))])
    return pl.pallas_call(
        body, name=name, grid_spec=grid_spec,
        out_shape=[_sds((N_CHIP, half, cols), BF16), _sds((half, cols), F32)],
        compiler_params=_cp(("arbitrary", "arbitrary")),
    )(place, g, rcv)


def _final_sum(sf, rb, place, name):
    half, cols = sf.shape
    tr = 256
    nt = half // tr

    def body(p_ref, sf_ref, r_ref, out_ref):
        acc = sf_ref[...]
        for j in range(3):
            acc = acc + r_ref[j].astype(F32)
        out_ref[...] = acc

    grid_spec = pltpu.PrefetchScalarGridSpec(
        num_scalar_prefetch=1, grid=(nt,),
        in_specs=[pl.BlockSpec((tr, cols), lambda t, p: (t, 0)), pl.BlockSpec((3, tr, cols), lambda t, p: (0, t, 0))],
        out_specs=pl.BlockSpec((tr, cols), lambda t, p: (p[1] * nt + t, 0)))
    return pl.pallas_call(
        body, name=name, grid_spec=grid_spec,
        out_shape=_sds((2 * half, cols), F32),
        compiler_params=_cp(("arbitrary",)),
    )(place, sf, rb)


def _half_copy(f_ref, which, ssem, rsem):
    x, y, c, _ = _place()
    h = f_ref.shape[0] // 2
    rows = f_ref.at[pl.ds(pl.multiple_of(which * h, 8), h), :]
    return pltpu.make_async_remote_copy(src_ref=rows, dst_ref=rows, send_sem=ssem.at[0], recv_sem=rsem.at[0],
                                        device_id=(x, y, 1 - c), device_id_type=MESH)


def _half_start(fulls, name, after=None):
    n = len(fulls)

    def body(*refs):
        for i in range(n):
            _half_copy(refs[i], lax.axis_index("c"), refs[n + 1 + 2 * i], refs[n + 2 + 2 * i]).start()
        token = refs[4 * n + 1]
        token[...] = jnp.zeros_like(token)

    sem1 = pltpu.SemaphoreType.DMA((1,))
    outs = pl.pallas_call(
        body, name=name,
        out_shape=tuple([sem1] * (2 * n) + [pltpu.HBM(f.shape, f.dtype) for f in fulls] + [_sds((8, 128), F32)]),
        in_specs=(HBM,) * n + (ANY,), out_specs=(SEM,) * (2 * n) + (HBM,) * n + (VMEM_FULL,),
        input_output_aliases={i: 2 * n + i for i in range(n)},
        compiler_params=pltpu.CompilerParams(has_side_effects=EFFECT),
    )(*[pltpu.with_memory_space_constraint(f, pltpu.HBM) for f in fulls], _dep(after))
    return [(outs[2 * i], outs[2 * i + 1], outs[2 * n + i]) for i in range(n)], outs[3 * n]


def _half_wait(items, after, name):
    n = len(items)

    def body(*refs):
        c = lax.axis_index("c")
        for i in range(n):
            ssem_ref, rsem_ref, f_ref = refs[3 * i:3 * i + 3]
            _half_copy(f_ref, c, ssem_ref, rsem_ref).wait_send()
            _half_copy(f_ref, 1 - c, ssem_ref, rsem_ref).wait_recv()

    return pl.pallas_call(
        body, name=name, out_shape=tuple(pltpu.HBM(it[2].shape, it[2].dtype) for it in items),
        in_specs=(SEM, SEM, HBM) * n + (ANY,), out_specs=(HBM,) * n,
        input_output_aliases={3 * i + 2: i for i in range(n)},
        compiler_params=pltpu.CompilerParams(has_side_effects=EFFECT),
    )(*[a for it in items for a in it], after)


def _small_copies(src_ref, land_ref, ssem, rsem, first):
    x, y, c, chips = _place()
    if first:
        return [pltpu.make_async_remote_copy(src_ref=src_ref, dst_ref=land_ref, send_sem=ssem.at[0], recv_sem=rsem.at[0],
                                             device_id=(x, y, 1 - c), device_id_type=MESH)]
    return [pltpu.make_async_remote_copy(src_ref=src_ref, dst_ref=land_ref.at[j], send_sem=ssem.at[j], recv_sem=rsem.at[j],
                                         device_id=(*chip, c), device_id_type=MESH) for j, chip in enumerate(chips)]


def _small_start(src, first, name, after=None):
    n = 1 if first else 3

    def body(src_ref, land_ref, after_ref, ssem, rsem, src_out, land_out, token):
        for cp in _small_copies(src_ref, land_ref, ssem, rsem, first):
            cp.start()
        token[...] = jnp.zeros_like(token)

    sems = pltpu.SemaphoreType.DMA((n,))
    land = lax.empty(src.shape if first else (3,) + src.shape, F32)
    return pl.pallas_call(
        body, name=name,
        out_shape=(sems, sems, pltpu.HBM(src.shape, F32), pltpu.HBM(land.shape, F32), _sds((8, 128), F32)),
        in_specs=(HBM, HBM, ANY), out_specs=(SEM, SEM, HBM, HBM, VMEM_FULL), input_output_aliases={0: 2, 1: 3},
        compiler_params=pltpu.CompilerParams(has_side_effects=EFFECT),
    )(pltpu.with_memory_space_constraint(src, pltpu.HBM), pltpu.with_memory_space_constraint(land, pltpu.HBM), _dep(after))


def _small_wait(src, land, ssem, rsem, first, after, name):
    def body(src_ref, land_ref, ssem_ref, rsem_ref, after_ref, src_out, land_out):
        for cp in _small_copies(src_ref, land_ref, ssem_ref, rsem_ref, first):
            cp.wait_send()
            cp.wait_recv()

    return pl.pallas_call(
        body, name=name,
        out_shape=(pltpu.HBM(src.shape, F32), pltpu.HBM(land.shape, F32)),
        in_specs=(HBM, HBM, SEM, SEM, ANY), out_specs=(HBM, HBM), input_output_aliases={0: 0, 1: 1},
        compiler_params=pltpu.CompilerParams(has_side_effects=EFFECT),
    )(src, land, ssem, rsem, after)


def _small_pair_sum(vec, got):
    def body(v_ref, g_ref, o_ref):
        o_ref[...] = v_ref[...] + g_ref[...]

    return pl.pallas_call(body, name="small_pair_sum", in_specs=[VMEM_FULL] * 2, out_specs=VMEM_FULL,
                          out_shape=_sds(vec.shape, F32), compiler_params=_cp())(vec, got)


def _small_chip_sum(pair, got, place):
    def body(p_ref, pair_ref, got_ref, o_ref):
        acc = None
        for kk in range(N_CHIP):
            d = jnp.bitwise_xor(p_ref[0], kk)
            t = jnp.where(d == 0, pair_ref[...], jnp.where(d == 2, got_ref[0], jnp.where(d == 1, got_ref[1], got_ref[2])))
            acc = t if acc is None else acc + t
        o_ref[...] = acc

    grid_spec = pltpu.PrefetchScalarGridSpec(
        num_scalar_prefetch=1, grid=(1,),
        in_specs=[pl.BlockSpec(pair.shape, lambda i, p: (0, 0)), pl.BlockSpec(got.shape, lambda i, p: (0, 0, 0))],
        out_specs=pl.BlockSpec(pair.shape, lambda i, p: (0, 0)))
    return pl.pallas_call(body, name="small_chip_sum", grid_spec=grid_spec, out_shape=_sds(pair.shape, F32),
                          compiler_params=_cp(("arbitrary",)))(place, pair, got)


def _adam_math(w, g, m, v):
    m = B1 * m + (1.0 - B1) * g
    v = B2 * v + (1.0 - B2) * (g * g)
    m_hat = m / (1.0 - B1 ** STEP)
    v_hat = v / (1.0 - B2 ** STEP)
    delta = -LR * (m_hat / (jnp.sqrt(v_hat) + AEPS) + WD * w)
    return delta, m, v


def _adam_big(w, g, m, v, name, dep=None):
    rows, cols = w.shape
    tr = 128

    def body(w_ref, g_ref, m_ref, v_ref, dep_ref, go_ref, d_ref, nm_ref, nv_ref):
        g = g_ref[...]
        d, nm, nv = _adam_math(w_ref[...], g, m_ref[...], v_ref[...])
        go_ref[...] = g
        d_ref[...] = d
        nm_ref[...] = nm
        nv_ref[...] = nv

    blk = pl.BlockSpec((tr, cols), lambda i: (i, 0))
    return pl.pallas_call(
        body, name=name, grid=(rows // tr,),
        in_specs=[blk] * 4 + [ANY], out_specs=[blk] * 4, out_shape=[_sds((rows, cols), F32)] * 4,
        compiler_params=_cp(("arbitrary",)),
    )(w, g, m, v, _dep(dep))


def _adam_small(ws, gs, ms, vs, dep=None):
    n = len(ws)

    def body(*refs):
        for i in range(n):
            d, nm, nv = _adam_math(refs[i][...], refs[n + i][...], refs[2 * n + i][...], refs[3 * n + i][...])
            refs[4 * n + 1 + i][...] = d
            refs[5 * n + 1 + i][...] = nm
            refs[6 * n + 1 + i][...] = nv

    shapes = [_sds(w.shape, F32) for w in ws]
    outs = pl.pallas_call(
        body, name="adam_small",
        in_specs=[VMEM_FULL] * (4 * n) + [ANY], out_specs=[VMEM_FULL] * (3 * n), out_shape=shapes * 3,
        compiler_params=_cp(),
    )(*ws, *gs, *ms, *vs, _dep(dep))
    return outs[:n], outs[n:2 * n], outs[2 * n:]


def _pad_rows8(a):
    flat = a.reshape(-1, 128)
    pad = (-flat.shape[0]) % 8
    if pad:
        flat = jnp.concatenate([flat, jnp.zeros((pad, 128), F32)], axis=0)
    return flat


def kernel(x, meta_tokens, norm1_w, w_in, gate_w2, gate_b, gla_norm_w, pool_w, pool_scale, w_out, norm2_w, mlp_w1, mlp_w2, final_norm_w, loss_target, m_meta_tokens, m_norm1_w, m_w_in, m_gate_w2, m_gate_b, m_gla_norm_w, m_pool_w, m_pool_scale, m_w_out, m_norm2_w, m_mlp_w1, m_mlp_w2, m_final_norm_w, v_meta_tokens, v_norm1_w, v_w_in, v_gate_w2, v_gate_b, v_gla_norm_w, v_pool_w, v_pool_scale, v_w_out, v_norm2_w, v_mlp_w1, v_mlp_w2, v_final_norm_w):
    cx, cy, cc = lax.axis_index("x"), lax.axis_index("y"), lax.axis_index("c")
    me = (2 * cx + cy).astype(jnp.int32)

    place = jnp.stack([me, cc.astype(jnp.int32)])
    fw = final_norm_w.reshape(1, D)

    mine = jnp.concatenate([meta_tokens.reshape(64, 128), gate_w2[0], pool_w[0].reshape(512, 128)], axis=0)
    small = lax.dynamic_update_slice(jnp.zeros((N_CHIP, 592, 128), F32), mine[None], (me, 0, 0))
    (s_sm,), (r_sm,), (f_sm,), tok = _gather_start([small], "gather_start_small", [True])
    (sem_win_d,), (win,), tok = _gather_step("gather_start_win", [_cast_win(w_in[0], place, tok)], [], [(0, "d")])
    wout, w1, w2 = (_cast_into(w_out[0], place, D, "cast_wout", tok), _cast_into(mlp_w1[0], place, D, "cast_w1", tok),
                    _cast_into(mlp_w2[0], place, D, "cast_w2", tok))
    small = _gather_wait(f_sm, s_sm, r_sm, w2, "gather_wait_small", True)
    metaF = jnp.concatenate([small[k, 0:64].reshape(N_META, 512) for k in range(N_CHIP)], axis=1)
    gw2F = jnp.concatenate([small[k, 64:80] for k in range(N_CHIP)], axis=1)
    pwF = jnp.concatenate([small[k, 80:592].reshape(4, 64, GC) for k in range(N_CHIP)], axis=1)

    fly = {"win": win, "wout": wout, "w1": w1, "w2": w2}
    sems = {"win_d": sem_win_d}

    def step(name, names, waits, starts, after):
        at = {nm: i for i, nm in enumerate(names)}
        new, arrs, tok = _gather_step(name, [fly[nm] for nm in names], [(at[nm], k) for nm, k in waits],
                                      [(at[nm], k) for nm, k in starts], [sems[nm + "_" + k] for nm, k in waits], after)
        fly.update(zip(names, arrs))
        sems.update({nm + "_" + k: s for (nm, k), s in zip(starts, new)})
        return tok

    def relay_first():
        return step("gather_relay_win", ["win", "wout", "w1"], [("win", "d")],
                    [("win", "r"), ("win", "fx"), ("wout", "d"), ("w1", "d")], [])

    def get_win(after):
        tok = step("gather_land_win", ["win"], [("win", "r")], [("win", "fd")], [after, m_w_in[0]])
        step("gather_wait_win", ["win"], [("win", "fx"), ("win", "fd")], [], [tok])
        return fly["win"]

    def relay_mid(after):
        return step("gather_relay_mid", ["wout", "w1", "w2"], [("wout", "d"), ("w1", "d")],
                    [("wout", "r"), ("w1", "r"), ("w2", "d"), ("wout", "fx"), ("w1", "fx")], [after, v_w_in[0]])

    def land_wout(after):
        return step("gather_land_wout", ["wout"], [("wout", "r")], [("wout", "fd")], [after])

    def get_wout(after):
        step("gather_wait_wout", ["wout"], [("wout", "fx"), ("wout", "fd")], [], [after])
        tok = step("gather_land_w1", ["w1"], [("w1", "r")], [("w1", "fd")], [fly["wout"]])
        return fly["wout"].reshape(D, D), tok

    def get_w1(after):
        step("gather_wait_w1", ["w1"], [("w1", "fx"), ("w1", "fd")], [], [after])
        return fly["w1"]

    def relay_last(after):
        return step("gather_relay_w2", ["w2"], [("w2", "d")], [("w2", "r"), ("w2", "fx")], [after])

    def get_w2(after):
        tok = step("gather_land_w2", ["w2"], [("w2", "r")], [("w2", "fd")], [after])
        step("gather_wait_w2", ["w2"], [("w2", "fx"), ("w2", "fd")], [], [tok])
        return fly["w2"].reshape(DFF, D)

    pairs, pending = {}, {}

    def grad_start(nm, g):
        ssem, rsem, g_thru, land, token = _pair_start(g, "pair_start_" + nm)
        pairs[nm] = (ssem, rsem, g_thru, land)
        return token

    def grad_finish(nm, after):
        ssem, rsem, g_thru, land = pairs[nm]
        g, rcv = _pair_wait(g_thru, land, ssem, rsem, after, "pair_wait_" + nm)
        sb, sf = _pair_sum(g, rcv, place, "pair_sum_" + nm)
        if nm == "win":
            pending[nm] = (sf, sb)
            return sf
        ssem, rsem, sb_thru, land, token = _rs_start(sb, "rs_start_" + nm)
        pending[nm] = (sf, ssem, rsem, sb_thru, land)
        return token

    (grad_x, loss8, d_n1w, d_gb, d_gnw, d_ps, d_n2w, d_fw, d_meta, d_gw2, d_pw) = _local_step(
        x[0], loss_target[0], dict(relay_first=relay_first, win=get_win, relay_mid=relay_mid, land_wout=land_wout,
                                   wout=get_wout, w1=get_w1, relay_last=relay_last, w2=get_w2),
        metaF, gw2F, pwF, norm1_w, gate_b, gla_norm_w, pool_scale, norm2_w, fw, grad_start, grad_finish)
    return _reduce_and_update(
        me, place, pending, grad_x, loss8, d_n1w, d_gb, d_gnw, d_ps, d_n2w, d_fw, d_meta, d_gw2, d_pw,
        meta_tokens, norm1_w, w_in, gate_w2, gate_b, gla_norm_w, pool_w, pool_scale, w_out, norm2_w, mlp_w1, mlp_w2, fw,
        m_meta_tokens, m_norm1_w, m_w_in, m_gate_w2, m_gate_b, m_gla_norm_w, m_pool_w, m_pool_scale, m_w_out, m_norm2_w,
        m_mlp_w1, m_mlp_w2, m_final_norm_w, v_meta_tokens, v_norm1_w, v_w_in, v_gate_w2, v_gate_b, v_gla_norm_w, v_pool_w,
        v_pool_scale, v_w_out, v_norm2_w, v_mlp_w1, v_mlp_w2, v_final_norm_w)


def _local_step(x, target, gather, metaF, gw2F, pwF, norm1_w, gate_b, gla_norm_w, pool_scale, norm2_w, fw, grad_start,
                grad_finish):
    h0, u = _embed_norm(x, metaF, norm1_w, gather["relay_first"]())
    Win = gather["win"](u)
    P = _in_proj(u, Win)
    gw2p = jnp.pad(gw2F, ((0, 128 - RANK), (0, 0)))
    yb, op = _pool_fwd(P, pwF, pool_scale, gather["relay_mid"](P))
    o, og, sp = _gla_fwd(P, gw2p, gate_b, gla_norm_w, gather["land_wout"](op))
    Wout, tok = gather["wout"](og)
    h1 = _out_proj(og, op, Wout, h0, tok)
    n2 = _norm_rows(h1, norm2_w, "norm2")
    W1 = gather["w1"](n2)
    zr, a, tok = _mlp_up(n2, W1, 0)
    zr, a, _ = _mlp_up(n2, W1, 1, (zr, a), gather["relay_last"](tok))
    W2 = gather["w2"](a)
    h2 = _mlp_down(a, W2, h1)

    dh2, dh2b, d_fw, loss8 = _loss_head(h2, target, fw)
    tok = grad_start("w2", _grad_w2(a, dh2b).reshape(N_CHIP, D, D))
    dz = _mlp_dz(dh2b, W2, zr, tok)
    tok = grad_finish("w2", dz)
    tok = grad_start("w1", _grad_w1(n2, dz, tok))
    dn2 = _mlp_dn(dz, W1, tok)
    tok = grad_finish("w1", dn2)
    dh1, dh1b, d_n2w = _norm_bwd(dn2, h1, dh2, norm2_w, "norm2_bwd", tok)
    dmixed = _mixed_grad(dh1b, Wout)
    tok = grad_start("wout", _grad_wout(og, op, dh1b))
    dpu, d_pw, d_ps = _pool_bwd(dmixed, yb, pwF, pool_scale, tok)
    dq, dk, dv, dr, dglr, d_gw2p, d_gb, d_gnw = _gla_bwd(dmixed, o, P, gw2p, gate_b, gla_norm_w, sp, tok)
    d_gw2 = d_gw2p[0:RANK]
    tok = grad_finish("wout", dq)
    tok = grad_start("win", _grad_win(u, dq, dk, dv, dr, dglr, dpu, tok))
    du = _in_grad(dq, dk, dv, dr, dglr, dpu, Win, tok)
    tok = grad_finish("win", du)
    grad_x, d_meta, d_n1w = _input_grad(du, h0, dh1, norm1_w, tok)
    return grad_x, loss8, d_n1w, d_gb, d_gnw, d_ps, d_n2w, d_fw, d_meta, d_gw2, d_pw


def _reduce_and_update(me, place, pending, grad_x, loss8, d_n1w, d_gb, d_gnw, d_ps, d_n2w, d_fw, d_meta, d_gw2, d_pw,
                       meta_tokens, norm1_w, w_in, gate_w2, gate_b, gla_norm_w, pool_w, pool_scale, w_out, norm2_w,
                       mlp_w1, mlp_w2, fw, m_meta_tokens, m_norm1_w, m_w_in, m_gate_w2, m_gate_b, m_gla_norm_w, m_pool_w,
                       m_pool_scale, m_w_out, m_norm2_w, m_mlp_w1, m_mlp_w2, m_final_norm_w, v_meta_tokens, v_norm1_w, v_w_in,
                       v_gate_w2, v_gate_b, v_gla_norm_w, v_pool_w, v_pool_scale, v_w_out, v_norm2_w, v_mlp_w1, v_mlp_w2,
                       v_final_norm_w):
    parts = [loss8, d_n1w, d_gb, d_gnw, d_ps, d_n2w, d_fw, d_meta, d_gw2, d_pw]
    packed = [_pad_rows8(p) for p in parts]
    sizes = [p.shape[0] for p in packed]
    vec = jnp.concatenate(packed, axis=0)

    big, halves = {}, {}
    params = {"w2": (mlp_w2, m_mlp_w2, v_mlp_w2), "w1": (mlp_w1, m_mlp_w1, v_mlp_w1), "wout": (w_out, m_w_out, v_w_out),
              "win": (w_in, m_w_in, v_w_in)}

    def reduce_(names, after, tag):
        items = [(pending[nm][3], pending[nm][4], pending[nm][1], pending[nm][2]) for nm in names]
        landed = _rs_wait(items, after, "rs_wait_" + tag)
        fulls = [_final_sum(pending[nm][0], rb, place, "final_sum_" + nm) for nm, rb in zip(names, landed)]
        sent, token = _half_start(fulls, "half_start_" + tag)
        halves.update(zip(names, sent))
        return token

    def update(names, after, tag):
        fulls = _half_wait([halves[nm] for nm in names], after, "half_wait_" + tag)
        tok = None
        for nm, full in zip(names, fulls):
            w, m, v = params[nm]
            big[nm] = _adam_big(w[0], full, m[0], v[0], "adam_" + nm, tok)
            tok = big[nm][3]
        return tok

    first = ["w2", "w1", "wout"]
    s1, r1, vec, land1, tok = _small_start(vec, True, "small_start_pair")
    tok = reduce_(first, tok, "mlp_wout")
    vec, got = _small_wait(vec, land1, s1, r1, True, tok, "small_wait_pair")
    pair = _small_pair_sum(vec, got)
    s2, r2, pair, land2, tok = _small_start(pair, False, "small_start_chips")
    sf, sb = pending["win"]
    ssem, rsem, sb_thru, land, tok = _rs_start(sb, "rs_start_win", tok)
    pending["win"] = (sf, ssem, rsem, sb_thru, land)
    tok = update(first, tok, "mlp_wout")
    pair, got = _small_wait(pair, land2, s2, r2, False, tok, "small_wait_chips")
    red = _small_chip_sum(pair, got, place)
    tok = reduce_(["win"], red, "win")
    after = update(["win"], tok, "win")
    offs = [0]
    for s in sizes:
        offs.append(offs[-1] + s)

    def take(i, shape):
        n = 1
        for d in shape:
            n *= d
        return red[offs[i]:offs[i] + n // 128].reshape(shape)

    loss = red[0, 0]
    G_n1w = take(1, (1, D))
    G_gb = take(2, (1, KW))
    G_gnw = take(3, (1, DV))
    G_ps = take(4, (1, PW))
    G_n2w = take(5, (1, D))
    G_fw = take(6, (1, D))
    G_meta = lax.dynamic_slice(take(7, (N_META, D)), (0, me * 512), (N_META, 512))
    G_gw2 = lax.dynamic_slice(take(8, (RANK, KW)), (0, me * 128), (RANK, 128))
    G_pw = lax.dynamic_slice(take(9, (4, GC, GC)), (0, me * 64, 0), (4, 64, GC))

    G_win, d_win, nm_win, nv_win = big["win"]
    G_wout, d_wout, nm_wout, nv_wout = big["wout"]
    G_w1, d_w1, nm_w1, nv_w1 = big["w1"]
    G_w2, d_w2, nm_w2, nv_w2 = big["w2"]
    ws = [meta_tokens, norm1_w, gate_w2[0], gate_b, gla_norm_w, pool_w[0], pool_scale, norm2_w, fw]
    gs = [G_meta, G_n1w, G_gw2, G_gb, G_gnw, G_pw, G_ps, G_n2w, G_fw]
    ms = [m_meta_tokens, m_norm1_w, m_gate_w2[0], m_gate_b, m_gla_norm_w, m_pool_w[0], m_pool_scale, m_norm2_w,
          m_final_norm_w.reshape(1, D)]
    vs = [v_meta_tokens, v_norm1_w, v_gate_w2[0], v_gate_b, v_gla_norm_w, v_pool_w[0], v_pool_scale, v_norm2_w,
          v_final_norm_w.reshape(1, D)]
    ds, nms, nvs = _adam_small(ws, gs, ms, vs, after)

    def assemble(small, win_, wout_, w1_, w2_):
        meta_, n1_, gw2_, gb_, gnw_, pw_, ps_, n2_, fw_ = small
        return (meta_, n1_, win_[None], gw2_[None], gb_, gnw_, pw_[None], ps_, wout_[None], n2_, w1_[None], w2_[None],
                fw_.reshape(D))

    grads_out = assemble(gs, G_win, G_wout, G_w1, G_w2)
    deltas = assemble(ds, d_win, d_wout, d_w1, d_w2)
    new_m = assemble(nms, nm_win, nm_wout, nm_w1, nm_w2)
    new_v = assemble(nvs, nv_win, nv_wout, nv_w1, nv_w2)
    return (loss, grad_x[None], *grads_out, *deltas, *new_m, *new_v)
```

```python
import functools

import jax
import jax.numpy as jnp
from jax import lax
from jax.experimental import pallas as pl
from jax.experimental.pallas import tpu as pltpu

F32 = jnp.float32
BF16 = jnp.bfloat16

D = 2048
SEQ = 2048
N_META = 16
CH = 64
TP = 2176
NCH = TP // CH
ROW_LO = 112
X_LO = 128
ROW_HI = TP
XT = 128
NXT = TP // XT
HEADS = 4
DK = 128
DV = 256
KW = HEADS * DK
GW = HEADS * DV
RANK = 16
TAU = 16.0
WINDOWS = (2, 4, 8, 16)
PW = 1024
GC = 256
DFF = 8192
EPS = 1e-6
SHARD_IN = 1028
PAD_IN = 1152
N_CHIP = 4

LR = 0.001
B1 = 0.9
B2 = 0.999
AEPS = 1e-08
WD = 0.01
STEP = 10

VMEM_LIMIT = 60 * 1024 * 1024
ANY = pl.BlockSpec(memory_space=pl.ANY)
VMEM_FULL = pl.BlockSpec(memory_space=pltpu.VMEM)
MESH = pl.DeviceIdType.MESH


def _cp(sem=None):
    if sem is None:
        return pltpu.CompilerParams(vmem_limit_bytes=VMEM_LIMIT)
    return pltpu.CompilerParams(dimension_semantics=sem, vmem_limit_bytes=VMEM_LIMIT)


def _dot(a, b):
    return jnp.dot(a, b, preferred_element_type=F32)


def _dot_nt(a, b):
    return lax.dot_general(a, b, (((1,), (1,)), ((), ())), preferred_element_type=F32)


def _dot_tn(a, b):
    return lax.dot_general(a, b, (((0,), (0,)), ((), ())), preferred_element_type=F32)


def _sds(shape, dtype):
    return jax.ShapeDtypeStruct(shape, dtype)


def _embed_norm(x, meta_full, w, dep=None):
    def body(x_ref, meta_ref, w_ref, dep_ref, h_ref, u_ref):
        i = pl.program_id(0)

        @pl.when(i == 0)
        def _():
            h_ref[...] = jnp.zeros_like(h_ref)
            h_ref[ROW_LO:X_LO, :] = meta_ref[...]

        @pl.when(i >= 1)
        def _():
            h_ref[...] = x_ref[...]

        h = h_ref[...]
        r = lax.rsqrt(jnp.mean(h * h, axis=-1, keepdims=True) + EPS)
        u_ref[...] = ((h * r) * w_ref[...]).astype(BF16)

    return pl.pallas_call(
        body, name="embed_norm1", grid=(NXT,),
        in_specs=[pl.BlockSpec((XT, D), lambda i: (jnp.maximum(i - 1, 0), 0)),
                  pl.BlockSpec((N_META, D), lambda i: (0, 0)),
                  pl.BlockSpec((1, D), lambda i: (0, 0)), ANY],
        out_specs=[pl.BlockSpec((XT, D), lambda i: (i, 0)), pl.BlockSpec((XT, D), lambda i: (i, 0))],
        out_shape=[_sds((TP, D), F32), _sds((TP, D), BF16)],
        compiler_params=_cp(("arbitrary",)),
    )(x, meta_full, w, _dep(dep))


def _norm_rows(h, w, name):
    tr = 272

    def body(h_ref, w_ref, o_ref):
        hv = h_ref[...]
        r = lax.rsqrt(jnp.mean(hv * hv, axis=-1, keepdims=True) + EPS)
        o_ref[...] = ((hv * r) * w_ref[...]).astype(BF16)

    return pl.pallas_call(
        body, name=name, grid=(TP // tr,),
        in_specs=[pl.BlockSpec((tr, D), lambda i: (i, 0)), pl.BlockSpec((1, D), lambda i: (0, 0))],
        out_specs=pl.BlockSpec((tr, D), lambda i: (i, 0)),
        out_shape=_sds((TP, D), BF16),
        compiler_params=_cp(("arbitrary",)),
    )(h, w)


def _loss_head(h2, target, fw):
    def body(h_ref, t_ref, w_ref, dh_ref, dhb_ref, dw_ref, loss_ref):
        i = pl.program_id(0)

        @pl.when(i == 0)
        def _():
            dw_ref[...] = jnp.zeros_like(dw_ref)
            loss_ref[...] = jnp.zeros_like(loss_ref)

        h = h_ref[...]
        w = w_ref[...]
        r = lax.rsqrt(jnp.mean(h * h, axis=-1, keepdims=True) + EPS)
        xh = h * r
        y = xh * w
        is_x = (i >= 1).astype(F32)
        diff = (y - t_ref[...]) * is_x
        loss_ref[...] += jnp.sum(diff * diff) * (0.5 / D)
        dy = diff * (1.0 / D)
        dw_ref[...] += jnp.sum(dy * xh, axis=0, keepdims=True)
        gx = dy * w
        dh = r * (gx - xh * jnp.mean(gx * xh, axis=-1, keepdims=True))
        dh_ref[...] = dh
        dhb_ref[...] = dh.astype(BF16)

    return pl.pallas_call(
        body, name="loss_head", grid=(NXT,),
        in_specs=[pl.BlockSpec((XT, D), lambda i: (i, 0)),
                  pl.BlockSpec((XT, D), lambda i: (jnp.maximum(i - 1, 0), 0)),
                  pl.BlockSpec((1, D), lambda i: (0, 0))],
        out_specs=[pl.BlockSpec((XT, D), lambda i: (i, 0)), pl.BlockSpec((XT, D), lambda i: (i, 0)),
                   pl.BlockSpec((1, D), lambda i: (0, 0)), pl.BlockSpec((8, 128), lambda i: (0, 0))],
        out_shape=[_sds((TP, D), F32), _sds((TP, D), BF16), _sds((1, D), F32), _sds((8, 128), F32)],
        compiler_params=_cp(("arbitrary",)),
    )(h2, target, fw)


def _norm_bwd(dn, h, dres, w, name, dep=None):
    tr = 272

    def body(dn_ref, h_ref, dres_ref, w_ref, dep_ref, o_ref, ob_ref, dw_ref):
        @pl.when(pl.program_id(0) == 0)
        def _():
            dw_ref[...] = jnp.zeros_like(dw_ref)

        hv = h_ref[...]
        dnv = dn_ref[...]
        r = lax.rsqrt(jnp.mean(hv * hv, axis=-1, keepdims=True) + EPS)
        xh = hv * r
        dw_ref[...] += jnp.sum(dnv * xh, axis=0, keepdims=True)
        gx = dnv * w_ref[...]
        dh = dres_ref[...] + r * (gx - xh * jnp.mean(gx * xh, axis=-1, keepdims=True))
        o_ref[...] = dh
        ob_ref[...] = dh.astype(BF16)

    row = pl.BlockSpec((tr, D), lambda i: (i, 0))
    vec = pl.BlockSpec((1, D), lambda i: (0, 0))
    return pl.pallas_call(
        body, name=name, grid=(TP // tr,),
        in_specs=[row, row, row, vec, ANY], out_specs=[row, row, vec],
        out_shape=[_sds((TP, D), F32), _sds((TP, D), BF16), _sds((1, D), F32)],
        compiler_params=_cp(("arbitrary",)),
    )(dn, h, dres, w, _dep(dep))


def _input_grad(du, h0, dh1, w, dep=None):
    def body(du_ref, h_ref, dres_ref, w_ref, dep_ref, gx_ref, gm_ref, dw_ref):
        i = pl.program_id(0)

        @pl.when(i == 0)
        def _():
            dw_ref[...] = jnp.zeros_like(dw_ref)

        hv = h_ref[...]
        dnv = du_ref[...]
        r = lax.rsqrt(jnp.mean(hv * hv, axis=-1, keepdims=True) + EPS)
        xh = hv * r
        dw_ref[...] += jnp.sum(dnv * xh, axis=0, keepdims=True)
        g = dnv * w_ref[...]
        dh = dres_ref[...] + r * (g - xh * jnp.mean(g * xh, axis=-1, keepdims=True))

        @pl.when(i == 0)
        def _():
            gm_ref[...] = dh[ROW_LO:X_LO, :]

        @pl.when(i >= 1)
        def _():
            gx_ref[...] = dh

    row = pl.BlockSpec((XT, D), lambda i: (i, 0))
    vec = pl.BlockSpec((1, D), lambda i: (0, 0))
    return pl.pallas_call(
        body, name="input_grad", grid=(NXT,),
        in_specs=[row, row, row, vec, ANY],
        out_specs=[pl.BlockSpec((XT, D), lambda i: (jnp.maximum(i - 1, 0), 0)),
                   pl.BlockSpec((N_META, D), lambda i: (0, 0)), vec],
        out_shape=[_sds((SEQ, D), F32), _sds((N_META, D), F32), _sds((1, D), F32)],
        compiler_params=_cp(("arbitrary",)),
    )(du, h0, dh1, w, _dep(dep))


def _in_proj(u, wg):
    def body(u_ref, w_ref, o_ref):
        o_ref[0] = _dot(u_ref[...], w_ref[0])

    return pl.pallas_call(
        body, name="in_proj", grid=(N_CHIP,),
        in_specs=[VMEM_FULL, pl.BlockSpec((1, D, PAD_IN), lambda k: (k, 0, 0))],
        out_specs=pl.BlockSpec((1, TP, PAD_IN), lambda k: (k, 0, 0)),
        out_shape=_sds((N_CHIP, TP, PAD_IN), F32),
        compiler_params=_cp(("arbitrary",)),
    )(u, wg)


def _out_proj(og, op, wout, h0, dep=None):
    tn = 512

    def body(og_ref, op_ref, w_ref, h_ref, dep_ref, o_ref):
        acc = _dot(og_ref[...], w_ref[0:GW, :]) + _dot(op_ref[...], w_ref[GW:D, :])
        o_ref[...] = h_ref[...] + acc

    return pl.pallas_call(
        body, name="out_proj", grid=(D // tn,),
        in_specs=[VMEM_FULL, VMEM_FULL, pl.BlockSpec((D, tn), lambda j: (0, j)),
                  pl.BlockSpec((TP, tn), lambda j: (0, j)), ANY],
        out_specs=pl.BlockSpec((TP, tn), lambda j: (0, j)),
        out_shape=_sds((TP, D), F32),
        compiler_params=_cp(("arbitrary",)),
    )(og, op, wout, h0, _dep(dep))


def _mlp_up(n2, w1g, part, prev=None, dep=None):
    tn = 1024
    per = D // tn

    def body(n_ref, w_ref, dep_ref, *rest):
        zr_ref, a_ref, token = rest[-3:]
        z = jnp.maximum(_dot(n_ref[...], w_ref[0]), 0.0)
        zr_ref[...] = z.astype(BF16)
        a_ref[...] = (z * z).astype(BF16)
        token[...] = jnp.zeros_like(token)

    col = pl.BlockSpec((TP, tn), lambda k, j: (0, (2 * part + k) * per + j))
    return pl.pallas_call(
        body, name="mlp_up_%d" % part, grid=(N_CHIP // 2, per),
        in_specs=[VMEM_FULL, pl.BlockSpec((1, D, tn), lambda k, j: (2 * part + k, 0, j)), ANY] + ([ANY, ANY] if prev else []),
        out_specs=[col, col, pl.BlockSpec((8, 128), lambda k, j: (0, 0))],
        out_shape=[_sds((TP, DFF), BF16), _sds((TP, DFF), BF16), _sds((8, 128), F32)],
        input_output_aliases={3: 0, 4: 1} if prev else {},
        compiler_params=_cp(("arbitrary", "arbitrary")),
    )(n2, w1g, _dep(dep), *(prev or ()))


def _mlp_down(a, w2, h1):
    tk = 1024
    nk = DFF // tk

    def body(a_ref, w_ref, h_ref, o_ref, acc_ref):
        k = pl.program_id(0)

        @pl.when(k == 0)
        def _():
            pltpu.sync_copy(h_ref, acc_ref)

        acc_ref[...] += _dot(a_ref[...], w_ref[...])

        @pl.when(k == nk - 1)
        def _():
            pltpu.sync_copy(acc_ref, o_ref)

    return pl.pallas_call(
        body, name="mlp_down", grid=(nk,),
        in_specs=[pl.BlockSpec((TP, tk), lambda k: (0, k)), pl.BlockSpec((tk, D), lambda k: (k, 0)), ANY],
        out_specs=ANY,
        out_shape=_sds((TP, D), F32),
        scratch_shapes=[pltpu.VMEM((TP, D), F32)],
        compiler_params=_cp(("arbitrary",)),
    )(a, w2, h1)


def _mlp_dz(dh2b, w2, zr, dep=None):
    tn = 1024

    def body(d_ref, w_ref, z_ref, dep_ref, o_ref):
        da = _dot_nt(d_ref[...], w_ref[...])
        o_ref[...] = (da * (2.0 * z_ref[...].astype(F32))).astype(BF16)

    col = pl.BlockSpec((TP, tn), lambda j: (0, j))
    return pl.pallas_call(
        body, name="mlp_dz", grid=(DFF // tn,),
        in_specs=[VMEM_FULL, pl.BlockSpec((tn, D), lambda j: (j, 0)), col, ANY],
        out_specs=col,
        out_shape=_sds((TP, DFF), BF16),
        compiler_params=_cp(("arbitrary",)),
    )(dh2b, w2, zr, _dep(dep))


def _grad_w2(a, dh2b):
    tm = 1024

    def body(a_ref, d_ref, o_ref):
        o_ref[...] = _dot_tn(a_ref[...], d_ref[...])

    return pl.pallas_call(
        body, name="grad_w2", grid=(DFF // tm,),
        in_specs=[pl.BlockSpec((TP, tm), lambda j: (0, j)), VMEM_FULL],
        out_specs=pl.BlockSpec((tm, D), lambda j: (j, 0)),
        out_shape=_sds((DFF, D), F32),
        compiler_params=_cp(("arbitrary",)),
    )(a, dh2b)


def _dep(token):
    return jnp.zeros((8, 128), F32) if token is None else token


def _grad_w1(n2, dz, dep=None):
    tn = 1024
    per = D // tn

    def body(n_ref, d_ref, dep_ref, o_ref):
        o_ref[0] = _dot_tn(n_ref[...], d_ref[...])

    return pl.pallas_call(
        body, name="grad_w1", grid=(N_CHIP, per),
        in_specs=[VMEM_FULL, pl.BlockSpec((TP, tn), lambda k, j: (0, k * per + j)), ANY],
        out_specs=pl.BlockSpec((1, D, tn), lambda k, j: (k, 0, j)),
        out_shape=_sds((N_CHIP, D, D), F32),
        compiler_params=_cp(("arbitrary", "arbitrary")),
    )(n2, dz, _dep(dep))


def _mlp_dn(dz, w1g, dep=None):
    tk = 1024
    per = D // tk
    nk = DFF // tk

    def body(d_ref, w_ref, dep_ref, o_ref, acc_ref):
        k = pl.program_id(0)
        part = _dot_nt(d_ref[...], w_ref[0])

        @pl.when(k == 0)
        def _():
            acc_ref[...] = part

        @pl.when(k > 0)
        def _():
            acc_ref[...] += part

        @pl.when(k == nk - 1)
        def _():
            pltpu.sync_copy(acc_ref, o_ref)

    return pl.pallas_call(
        body, name="mlp_dn", grid=(nk,),
        in_specs=[pl.BlockSpec((TP, tk), lambda k: (0, k)),
                  pl.BlockSpec((1, D, tk), lambda k: (k // per, 0, k % per)), ANY],
        out_specs=ANY,
        out_shape=_sds((TP, D), F32),
        scratch_shapes=[pltpu.VMEM((TP, D), F32)],
        compiler_params=_cp(("arbitrary",)),
    )(dz, w1g, _dep(dep))


def _mixed_grad(dh1b, wout):
    tn = 512

    def body(d_ref, w_ref, o_ref):
        o_ref[...] = _dot_nt(d_ref[...], w_ref[...])

    return pl.pallas_call(
        body, name="mixed_grad", grid=(D // tn,),
        in_specs=[VMEM_FULL, pl.BlockSpec((tn, D), lambda j: (j, 0))],
        out_specs=pl.BlockSpec((TP, tn), lambda j: (0, j)),
        out_shape=_sds((TP, D), F32),
        compiler_params=_cp(("arbitrary",)),
    )(dh1b, wout)


def _grad_wout(og, op, dh1b):
    tm = 512

    def body(og_ref, op_ref, d_ref, o_ref):
        j = pl.program_id(0)

        @pl.when(j < 2)
        def _():
            o_ref[0] = _dot_tn(og_ref[...], d_ref[...])

        @pl.when(j >= 2)
        def _():
            o_ref[0] = _dot_tn(op_ref[...], d_ref[...])

    return pl.pallas_call(
        body, name="grad_wout", grid=(N_CHIP,),
        in_specs=[pl.BlockSpec((TP, tm), lambda j: (0, jnp.minimum(j, 1))),
                  pl.BlockSpec((TP, tm), lambda j: (0, jnp.maximum(j - 2, 0))), VMEM_FULL],
        out_specs=pl.BlockSpec((1, tm, D), lambda j: (j, 0, 0)),
        out_shape=_sds((N_CHIP, tm, D), F32),
        compiler_params=_cp(("arbitrary",)),
    )(og, op, dh1b)


def _in_grad(dq, dk, dv, dr, dglr, dpu, wg, dep=None):
    def body(dq_ref, dk_ref, dv_ref, dr_ref, dg_ref, dpu_ref, w_ref, dep_ref, o_ref):
        dv, dr, dg = dv_ref[...], dr_ref[...], dg_ref[...]
        head, tail = slice(0, GW), slice(GW, PAD_IN)
        o_ref[...] = (_dot_nt(dq_ref[...], w_ref[0, :, 0:KW]) + _dot_nt(dk_ref[...], w_ref[0, :, KW:GW])
                      + _dot_nt(dv[:, 0:128], w_ref[0, :, tail])
                      + _dot_nt(dv, w_ref[1, :, head]) + _dot_nt(dr[:, 0:128], w_ref[1, :, tail])
                      + _dot_nt(dr, w_ref[2, :, head]) + _dot_nt(dg, w_ref[2, :, tail])
                      + _dot_nt(dpu_ref[...], w_ref[3, :, head]) + _dot_nt(dg, w_ref[3, :, tail]))

    tn = 512
    return pl.pallas_call(
        body, name="in_grad", grid=(D // tn,),
        in_specs=[VMEM_FULL] * 6 + [pl.BlockSpec((N_CHIP, tn, PAD_IN), lambda j: (0, j, 0)), ANY],
        out_specs=pl.BlockSpec((TP, tn), lambda j: (0, j)),
        out_shape=_sds((TP, D), F32),
        compiler_params=_cp(("arbitrary",)),
    )(dq, dk, dv, dr, dglr, dpu, wg, _dep(dep))


def _grad_win(u, dq, dk, dv, dr, dglr, dpu, dep=None):
    tm = 512

    def body(u_ref, dq_hbm, dk_hbm, dv_hbm, dr_hbm, dg_hbm, dpu_hbm, dep_ref, o_ref, dp_ref, sem):
        k, m = pl.program_id(0), pl.program_id(1)
        head, tail = slice(0, GW), slice(GW, PAD_IN)
        pieces = [[(dq_hbm, slice(0, KW)), (dk_hbm, slice(KW, GW)), (dv_hbm.at[:, 0:128], tail)],
                  [(dv_hbm, head), (dr_hbm.at[:, 0:128], tail)],
                  [(dr_hbm, head), (dg_hbm, tail)],
                  [(dpu_hbm, head), (dg_hbm, tail)]]

        def copies(kk):
            return [pltpu.make_async_copy(src, dp_ref.at[kk % 2, :, cols], sem.at[kk % 2, i])
                    for i, (src, cols) in enumerate(pieces[kk])]

        @pl.when((k == 0) & (m == 0))
        def _():
            for cp in copies(0):
                cp.start()

        for kk in range(N_CHIP):
            @pl.when((k == kk) & (m == 0))
            def _(kk=kk):
                for cp in copies(kk):
                    cp.wait()
                if kk + 1 < N_CHIP:
                    for cp in copies(kk + 1):
                        cp.start()

        g = _dot_tn(u_ref[...], dp_ref[k % 2])
        lane = lax.broadcasted_iota(jnp.int32, (tm, PAD_IN), 1)
        for kk in range(N_CHIP):
            @pl.when(k == kk)
            def _(kk=kk):
                if kk == 0:
                    nat = g
                elif kk < 3:
                    nat = pltpu.roll(g, PAD_IN - 4 * kk, 1)
                else:
                    nat = jnp.where(lane < 4, pltpu.roll(g, PAD_IN - (GW + 12), 1), pltpu.roll(g, 4, 1))
                o_ref[0] = nat[:, 0:SHARD_IN]

    return pl.pallas_call(
        body, name="grad_win", grid=(N_CHIP, D // tm),
        in_specs=[pl.BlockSpec((TP, tm), lambda k, m: (0, m))] + [ANY] * 7,
        out_specs=pl.BlockSpec((1, tm, SHARD_IN), lambda k, m: (k, m, 0)),
        out_shape=_sds((N_CHIP, D, SHARD_IN), F32),
        scratch_shapes=[pltpu.VMEM((2, TP, PAD_IN), BF16), pltpu.SemaphoreType.DMA((2, 3))],
        compiler_params=_cp(("arbitrary", "arbitrary")),
    )(u, dq, dk, dv, dr, dglr, dpu, _dep(dep))


def _split3(x):
    hi = x.astype(BF16)
    r1 = x - hi.astype(F32)
    mid = r1.astype(BF16)
    lo = (r1 - mid.astype(F32)).astype(BF16)
    return hi, mid, lo


def _tri_sum(tri, x):
    hi, mid, lo = _split3(x)
    return _dot(tri, hi) + _dot(tri, mid) + _dot(tri, lo)


def _gla_common(n, glr, gw2, gb):
    rows = n * CH + lax.broadcasted_iota(jnp.int32, (CH, 1), 0)
    valid = (rows >= ROW_LO) & (rows < ROW_HI)
    g_raw = _dot(glr.astype(BF16), gw2.astype(BF16)) + gb
    logsig = jnp.minimum(g_raw, 0.0) - jnp.log(1.0 + jnp.exp(-jnp.abs(g_raw)))
    logg = jnp.where(valid, logsig * (1.0 / TAU), 0.0)
    ci = lax.broadcasted_iota(jnp.int32, (CH, CH), 0)
    si = lax.broadcasted_iota(jnp.int32, (CH, CH), 1)
    lower = ci >= si
    G = _tri_sum(lower.astype(BF16), logg)
    Gl = G[CH - 1:CH, :]
    return valid, g_raw, lower, G, Gl


def _p_specs(index):
    def spec(width, shard, col):
        return pl.BlockSpec((1, CH, width), lambda s: (shard, index(s), col))

    return [spec(KW, 0, 0), spec(KW, 0, 1), spec(GW, 1, 0), spec(128, 0, 8), spec(GW, 2, 0), spec(128, 1, 8),
            spec(128, 2, 8), spec(128, 3, 8)]


def _p_load(q_ref, k_ref, vm_ref, vh_ref, rm_ref, rh_ref, ga_ref, gb_ref):
    def joined(main, head):
        return jnp.concatenate([main[:, 0:128] + head, main[:, 128:]], axis=1)

    return q_ref[0], k_ref[0], joined(vm_ref[0], vh_ref[0]), joined(rm_ref[0], rh_ref[0]), ga_ref[0] + gb_ref[0]


def _gla_fwd(P, gw2, gb, gnw, dep=None):
    scale = DK ** -0.5

    def body(p0, p1, p2, p3, p4, p5, p6, p7, gw2_ref, gb_ref, gnw_ref, dep_ref, o_ref, og_ref, sp_ref, st_ref):
        n = pl.program_id(0)

        @pl.when(n == 0)
        def _():
            st_ref[...] = jnp.zeros_like(st_ref)

        q_all, k_all, v_all, r_all, glr = _p_load(p0, p1, p2, p3, p4, p5, p6, p7)
        _, _, lower, G, Gl = _gla_common(n, glr, gw2_ref[...], gb_ref[...])
        eG = jnp.exp(G)
        eN = jnp.exp(-G)
        eE = jnp.exp(Gl - G)
        dec = jnp.exp(Gl)
        gnw_v = gnw_ref[...]
        for h in range(HEADS):
            ks = slice(h * DK, (h + 1) * DK)
            vs = slice(h * DV, (h + 1) * DV)
            kh = k_all[:, ks]
            vh = v_all[:, vs].astype(BF16)
            qd = ((q_all[:, ks] * scale) * eG[:, ks]).astype(BF16)
            ki = (kh * eN[:, ks]).astype(BF16)
            ke = (kh * eE[:, ks]).astype(BF16)
            st = st_ref[h]
            a = jnp.where(lower, _dot_nt(qd, ki), 0.0).astype(BF16)
            o = _dot(a, vh) + _dot_nt(qd, st.astype(BF16))
            sp_ref[0, h] = st
            st_ref[h] = st * dec[:, ks] + _dot_tn(vh, ke)
            o_ref[:, vs] = o
            rs = lax.rsqrt(jnp.mean(o * o, axis=-1, keepdims=True) + EPS)
            rv = r_all[:, vs]
            gate = rv / (1.0 + jnp.exp(-rv))
            og_ref[:, vs] = (((o * rs) * gnw_v) * gate).astype(BF16)

    rv_ = pl.BlockSpec((CH, GW), lambda n: (n, 0))

    def full(shape):
        return pl.BlockSpec(shape, lambda n: tuple(0 for _ in shape))

    return pl.pallas_call(
        body, name="gla_fwd", grid=(NCH,),
        in_specs=_p_specs(lambda n: n) + [full((128, KW)), full((1, KW)), full((1, DV)), ANY],
        out_specs=[rv_, rv_, pl.BlockSpec((1, HEADS, DV, DK), lambda n: (n, 0, 0, 0))],
        out_shape=[_sds((TP, GW), F32), _sds((TP, GW), BF16), _sds((NCH, HEADS, DV, DK), F32)],
        scratch_shapes=[pltpu.VMEM((HEADS, DV, DK), F32)],
        compiler_params=_cp(("arbitrary",)),
    )(*([P] * 8), gw2, gb, gnw, _dep(dep))


def _gla_bwd(dog, o, P, gw2, gb, gnw, sp, dep=None):
    scale = DK ** -0.5

    def body(dog_ref, o_ref, p0, p1, p2, p3, p4, p5, p6, p7, gw2_ref, gb_ref, gnw_ref, sp_ref, dep_ref,
             dq_ref, dk_ref, dv_ref, dr_ref, dglr_ref, dgw2_ref, dgb_ref, dgnw_ref, ds_ref):
        step = pl.program_id(0)
        n = NCH - 1 - step

        @pl.when(step == 0)
        def _():
            ds_ref[...] = jnp.zeros_like(ds_ref)
            dgw2_ref[...] = jnp.zeros_like(dgw2_ref)
            dgb_ref[...] = jnp.zeros_like(dgb_ref)
            dgnw_ref[...] = jnp.zeros_like(dgnw_ref)

        q_all, k_all, v_all, r_all, glr_v = _p_load(p0, p1, p2, p3, p4, p5, p6, p7)
        gw2_b = gw2_ref[...].astype(BF16)
        valid, g_raw, lower, G, Gl = _gla_common(n, glr_v, gw2_ref[...], gb_ref[...])
        upper = lax.broadcasted_iota(jnp.int32, (CH, CH), 0) <= lax.broadcasted_iota(jnp.int32, (CH, CH), 1)
        eG = jnp.exp(G)
        eN = jnp.exp(-G)
        eE = jnp.exp(Gl - G)
        dec = jnp.exp(Gl)
        gnw_v = gnw_ref[...]
        last = lax.broadcasted_iota(jnp.int32, (CH, 1), 0) == CH - 1
        dgnw_acc = jnp.zeros((1, DV), F32)
        dG_parts = []
        for h in range(HEADS):
            ks = slice(h * DK, (h + 1) * DK)
            vs = slice(h * DV, (h + 1) * DV)
            oh = o_ref[:, vs]
            rv = r_all[:, vs]
            dg = dog_ref[:, vs]
            sig = 1.0 / (1.0 + jnp.exp(-rv))
            gate = rv * sig
            rs = lax.rsqrt(jnp.mean(oh * oh, axis=-1, keepdims=True) + EPS)
            ohat = oh * rs
            dr_ref[:, vs] = ((dg * (ohat * gnw_v)) * (sig * (1.0 + rv * (1.0 - sig)))).astype(BF16)
            don = dg * gate
            dgnw_acc = dgnw_acc + jnp.sum(don * ohat, axis=0, keepdims=True)
            gxn = don * gnw_v
            do = (rs * (gxn - ohat * jnp.mean(gxn * ohat, axis=-1, keepdims=True))).astype(BF16)
            kh = k_all[:, ks]
            vh = v_all[:, vs].astype(BF16)
            qd_f = (q_all[:, ks] * scale) * eG[:, ks]
            ki_f = kh * eN[:, ks]
            ke_f = kh * eE[:, ks]
            qd, ki, ke = qd_f.astype(BF16), ki_f.astype(BF16), ke_f.astype(BF16)
            spt = sp_ref[0, h]
            dst = ds_ref[h]
            dst_b = dst.astype(BF16)
            a_t = jnp.where(upper, _dot_nt(ki, qd), 0.0).astype(BF16)
            da = jnp.where(lower, _dot_nt(do, vh), 0.0).astype(BF16)
            da_t = jnp.where(upper, _dot_nt(vh, do), 0.0).astype(BF16)
            dv_ref[:, vs] = (_dot(a_t, do) + _dot_nt(ke, dst_b)).astype(BF16)
            dqd = _dot(da, ki) + _dot(do, spt.astype(BF16))
            dki = _dot(da_t, qd)
            dke = _dot(vh, dst_b)
            ddec = jnp.sum(spt * dst, axis=0, keepdims=True)
            ds_ref[h] = dst * dec[:, ks] + _dot_tn(do, qd)
            dq_ref[:, ks] = ((dqd * eG[:, ks]) * scale).astype(BF16)
            dk_ref[:, ks] = (dki * eN[:, ks] + dke * eE[:, ks]).astype(BF16)
            dke_ke = dke * ke_f
            dG = dqd * qd_f - dki * ki_f - dke_ke
            dGl = jnp.sum(dke_ke, axis=0, keepdims=True) + ddec * dec[:, ks]
            dG_parts.append(dG + jnp.where(last, dGl, 0.0))
        dgnw_ref[...] += dgnw_acc
        dG_all = jnp.concatenate(dG_parts, axis=1)
        dlogg = jnp.where(valid, _tri_sum(upper.astype(BF16), dG_all), 0.0)
        dg_raw = (dlogg * (1.0 / TAU)) * (1.0 / (1.0 + jnp.exp(g_raw)))
        dgb_ref[...] += jnp.sum(dg_raw, axis=0, keepdims=True)
        dg_b = dg_raw.astype(BF16)
        dgw2_ref[...] += _dot_tn(glr_v.astype(BF16), dg_b)
        dglr_ref[...] = _dot_nt(dg_b, gw2_b).astype(BF16)

    def back(s):
        return NCH - 1 - s

    rk = pl.BlockSpec((CH, KW), lambda s: (back(s), 0))
    rv_ = pl.BlockSpec((CH, GW), lambda s: (back(s), 0))
    rg = pl.BlockSpec((CH, 128), lambda s: (back(s), 0))

    def full(shape):
        return pl.BlockSpec(shape, lambda s: tuple(0 for _ in shape))

    return pl.pallas_call(
        body, name="gla_bwd", grid=(NCH,),
        in_specs=[rv_, rv_] + _p_specs(back) + [full((128, KW)), full((1, KW)), full((1, DV)),
                  pl.BlockSpec((1, HEADS, DV, DK), lambda s: (back(s), 0, 0, 0)), ANY],
        out_specs=[rk, rk, rv_, rv_, rg, full((128, KW)), full((1, KW)), full((1, DV))],
        out_shape=[_sds((TP, KW), BF16), _sds((TP, KW), BF16), _sds((TP, GW), BF16), _sds((TP, GW), BF16),
                   _sds((TP, 128), BF16), _sds((128, KW), F32), _sds((1, KW), F32), _sds((1, DV), F32)],
        scratch_shapes=[pltpu.VMEM((HEADS, DV, DK), F32)],
        compiler_params=_cp(("arbitrary",)),
    )(dog, o, *([P] * 8), gw2, gb, gnw, sp, _dep(dep))


POOL_TR = 128
HALO = 16


def _pool_counts(base, nrows):
    rows = base + lax.broadcasted_iota(jnp.int32, (nrows, 1), 0)
    valid = (rows >= ROW_LO) & (rows < ROW_HI)
    t1 = (rows - ROW_LO + 1).astype(F32)
    cnts = [jnp.clip(t1, 1.0, float(w)) for w in WINDOWS]
    return valid, cnts


def _pool_fwd(P, pw, ps, dep=None):
    def body(cur_ref, prev_ref, pw_ref, ps_ref, dep_ref, y_ref, op_ref):
        i = pl.program_id(0)
        cur = cur_ref[0]
        full = jnp.concatenate([prev_ref[0], cur], axis=0)
        s2 = full + pltpu.roll(full, 1, 0)
        s4 = s2 + pltpu.roll(s2, 2, 0)
        s8 = s4 + pltpu.roll(s4, 4, 0)
        s16 = s8 + pltpu.roll(s8, 8, 0)
        valid, cnts = _pool_counts(i * POOL_TR, POOL_TR)
        for g, s in enumerate((s2, s4, s8, s16)):
            cs = slice(g * GC, (g + 1) * GC)
            y = s[HALO:, cs] / cnts[g] - cur[:, cs]
            yb = jnp.where(valid, y, 0.0).astype(BF16)
            y_ref[:, cs] = yb
            op_ref[:, cs] = (_dot(yb, pw_ref[g].astype(BF16)) * ps_ref[:, cs]).astype(BF16)

    row = pl.BlockSpec((POOL_TR, PW), lambda i: (i, 0))
    per = POOL_TR // HALO
    return pl.pallas_call(
        body, name="pool_fwd", grid=(TP // POOL_TR,),
        in_specs=[pl.BlockSpec((1, POOL_TR, PW), lambda i: (3, i, 0)),
                  pl.BlockSpec((1, HALO, PW), lambda i: (3, jnp.maximum(i * per - 1, 0), 0)),
                  pl.BlockSpec((4, GC, GC), lambda i: (0, 0, 0)), pl.BlockSpec((1, PW), lambda i: (0, 0)), ANY],
        out_specs=[row, row],
        out_shape=[_sds((TP, PW), BF16), _sds((TP, PW), BF16)],
        compiler_params=_cp(("arbitrary",)),
    )(P, P, pw, ps, _dep(dep))


def _pool_bwd(dop, y, pw, ps, dep=None):
    nblk = TP // HALO

    def body(cur_ref, nxt_ref, y_ref, pw_ref, ps_ref, dep_ref, dpu_ref, dpw_ref, dps_ref):
        i = pl.program_id(0)

        @pl.when(i == 0)
        def _():
            dpw_ref[...] = jnp.zeros_like(dpw_ref)
            dps_ref[...] = jnp.zeros_like(dps_ref)

        n_all = POOL_TR + HALO
        dcur = cur_ref[...]
        dall = jnp.concatenate([dcur, nxt_ref[...]], axis=0)
        valid, cnts = _pool_counts(i * POOL_TR, n_all)
        for g in range(4):
            cs = slice(g * GC, (g + 1) * GC)
            pwb = pw_ref[g].astype(BF16)
            yb = y_ref[:, cs]
            dyw = (dall[:, cs] * ps_ref[:, cs]).astype(BF16)
            dps_ref[:, cs] += jnp.sum(dcur[:, cs] * _dot(yb, pwb), axis=0, keepdims=True)
            dpw_ref[g] += _dot_tn(yb, dyw[0:POOL_TR, :])
            dyv = jnp.where(valid, _dot_nt(dyw, pwb), 0.0)
            e = dyv / cnts[g]
            w = WINDOWS[g]
            sh = 1
            while sh < w:
                e = e + pltpu.roll(e, n_all - sh, 0)
                sh *= 2
            dpu_ref[:, cs] = (e[0:POOL_TR, :] - dyv[0:POOL_TR, :]).astype(BF16)

    row = pl.BlockSpec((POOL_TR, PW), lambda i: (i, 0))
    per = POOL_TR // HALO
    return pl.pallas_call(
        body, name="pool_bwd", grid=(TP // POOL_TR,),
        in_specs=[pl.BlockSpec((POOL_TR, PW), lambda i: (i, 1)),
                  pl.BlockSpec((HALO, PW), lambda i: (jnp.minimum(i * per + per, nblk - 1), 1)),
                  row, pl.BlockSpec((4, GC, GC), lambda i: (0, 0, 0)), pl.BlockSpec((1, PW), lambda i: (0, 0)), ANY],
        out_specs=[row, pl.BlockSpec((4, GC, GC), lambda i: (0, 0, 0)), pl.BlockSpec((1, PW), lambda i: (0, 0))],
        out_shape=[_sds((TP, PW), BF16), _sds((4, GC, GC), F32), _sds((1, PW), F32)],
        compiler_params=_cp(("arbitrary",)),
    )(dop, dop, y, pw, ps, _dep(dep))


def _place():
    x, y, c = lax.axis_index("x"), lax.axis_index("y"), lax.axis_index("c")
    chips = [(1 - x, y), (x, 1 - y), (1 - x, 1 - y)]
    return x, y, c, chips


HBM = pl.BlockSpec(memory_space=pltpu.HBM)
SEM = pl.BlockSpec(memory_space=pltpu.SEMAPHORE)
EFFECT = pltpu.SideEffectType.DATAFLOW_SIDE_EFFECTING


def _cast_into(w, place, cols_out, name, dep=None):
    rows, cols = w.shape
    tr = 256

    def body(p_ref, w_ref, dep_ref, o_ref):
        if cols_out != cols:
            o_ref[0] = jnp.zeros((tr, cols_out), BF16)
            o_ref[0, :, 0:cols] = w_ref[...].astype(BF16)
        else:
            o_ref[0] = w_ref[...].astype(BF16)

    grid_spec = pltpu.PrefetchScalarGridSpec(
        num_scalar_prefetch=1, grid=(rows // tr,),
        in_specs=[pl.BlockSpec((tr, cols), lambda i, p: (i, 0)), ANY],
        out_specs=pl.BlockSpec((1, tr, cols_out), lambda i, p: (p[0], i, 0)))
    return pl.pallas_call(
        body, name=name, grid_spec=grid_spec,
        out_shape=_sds((N_CHIP, rows, cols_out), BF16),
        compiler_params=_cp(("arbitrary",)),
    )(place, w, _dep(dep))


def _cast_win(w, place, dep=None):
    rows, cols = w.shape
    tr = 256

    def body(p_ref, w_ref, dep_ref, o_ref, t_ref):
        t_ref[...] = jnp.zeros_like(t_ref)
        t_ref[:, 0:cols] = w_ref[...]
        t = t_ref[...]
        lane = lax.broadcasted_iota(jnp.int32, (tr, PAD_IN), 1)
        for kk in range(N_CHIP):
            @pl.when(p_ref[0] == kk)
            def _(kk=kk):
                if kk == 0:
                    placed = t
                elif kk < 3:
                    placed = pltpu.roll(t, 4 * kk, 1)
                else:
                    pool = pltpu.roll(t, PAD_IN - 4, 1)
                    gate = pltpu.roll(t, GW + 12, 1)
                    placed = jnp.where(lane < GW, pool, jnp.where((lane >= GW + 12) & (lane < GW + 16), gate, 0.0))
                o_ref[0] = placed.astype(BF16)

    grid_spec = pltpu.PrefetchScalarGridSpec(
        num_scalar_prefetch=1, grid=(rows // tr,),
        in_specs=[pl.BlockSpec((tr, cols), lambda i, p: (i, 0)), ANY],
        out_specs=pl.BlockSpec((1, tr, PAD_IN), lambda i, p: (p[0], i, 0)),
        scratch_shapes=[pltpu.VMEM((tr, PAD_IN), F32)])
    return pl.pallas_call(
        body, name="cast_win", grid_spec=grid_spec,
        out_shape=_sds((N_CHIP, rows, PAD_IN), BF16),
        compiler_params=_cp(("arbitrary",)),
    )(place, w, _dep(dep))


def _half_rows(ref, k, which):
    h = ref.shape[1] // 2
    return ref.at[k, pl.ds(pl.multiple_of(which * h, 8), h), :]


def _sent_rows(ref, k, which, whole):
    return ref.at[k] if whole else _half_rows(ref, k, which)


def _gather_start(ws, name, whole=None):
    n = len(ws)
    whole = whole or [False] * n

    def body(*refs):
        ins = refs[:n]
        ssems = refs[n:2 * n]
        rsems = refs[2 * n:3 * n]
        token = refs[4 * n]
        x, y, c, chips = _place()
        me = 2 * x + y
        for w in range(n):
            blk = _sent_rows(ins[w], me, c, whole[w])
            for j, chip in enumerate(chips):
                pltpu.make_async_remote_copy(src_ref=blk, dst_ref=blk, send_sem=ssems[w].at[j], recv_sem=rsems[w].at[j],
                                             device_id=(*chip, c), device_id_type=MESH).start()
        token[...] = jnp.zeros_like(token)

    sem3 = pltpu.SemaphoreType.DMA((3,))
    outs = pl.pallas_call(
        body, name=name,
        out_shape=tuple([sem3] * (2 * n) + [pltpu.HBM(w.shape, w.dtype) for w in ws] + [_sds((8, 128), F32)]),
        in_specs=(HBM,) * n, out_specs=(SEM,) * (2 * n) + (HBM,) * n + (VMEM_FULL,),
        input_output_aliases={w: 2 * n + w for w in range(n)},
        compiler_params=pltpu.CompilerParams(has_side_effects=EFFECT),
    )(*[pltpu.with_memory_space_constraint(w, pltpu.HBM) for w in ws])
    return outs[:n], outs[n:2 * n], outs[2 * n:3 * n], outs[3 * n]


def _gather_wait(w, ssem, rsem, after, name, whole=False):
    def body(w_ref, ssem_ref, rsem_ref, after_ref, out_ref):
        x, y, c, chips = _place()
        me = 2 * x + y
        mine = _sent_rows(w_ref, me, c, whole)
        for j, (cx, cy) in enumerate(chips):
            cp = pltpu.make_async_remote_copy(src_ref=mine, dst_ref=_sent_rows(w_ref, 2 * cx + cy, c, whole),
                                              send_sem=ssem_ref.at[j], recv_sem=rsem_ref.at[j],
                                              device_id=(cx, cy, c), device_id_type=MESH)
            cp.wait_send()
            cp.wait_recv()

    return pl.pallas_call(
        body, name=name, out_shape=pltpu.HBM(w.shape, w.dtype),
        in_specs=(HBM, SEM, SEM, ANY), out_specs=HBM, input_output_aliases={0: 0},
        compiler_params=pltpu.CompilerParams(has_side_effects=EFFECT),
    )(w, ssem, rsem, after)


def _gather_copies(ref, kind, ssem, rsem):
    x, y, c, _ = _place()
    xn, yn, sib = (1 - x, y, c), (x, 1 - y, c), (x, y, 1 - c)
    kx, ky, kd = 2 * (1 - x) + y, 2 * x + (1 - y), 2 * (1 - x) + (1 - y)
    half = ref.shape[1] // 2
    quarter = half // 2

    def piece(k, q):
        return ref.at[k, pl.ds(pl.multiple_of(c * half + q * quarter, 8), quarter), :]

    if kind == "d":
        blk = _half_rows(ref, 2 * x + y, c)
        pairs = [(blk, xn), (blk, yn)]
    elif kind == "r":
        pairs = [(piece(ky, 1), xn), (piece(kx, 0), yn)]
    elif kind == "fx":
        pairs = [(_half_rows(ref, kx, c), sib), (_half_rows(ref, ky, c), sib)]
    else:
        pairs = [(_half_rows(ref, kd, c), sib)]
    return [pltpu.make_async_remote_copy(src_ref=blk, dst_ref=blk, send_sem=ssem.at[i], recv_sem=rsem.at[i],
                                         device_id=to, device_id_type=MESH) for i, (blk, to) in enumerate(pairs)]


def _gather_step(name, arrs, waits, starts, sems_in=(), after=()):
    n, nw, ns = len(arrs), len(waits), len(starts)
    after = [a for a in after if a is not None] or [_dep(None)]

    def body(*refs):
        a_in = refs[:n]
        s_in = refs[n:n + 2 * nw]
        outs = refs[n + 2 * nw + len(after):]
        s_out = outs[:2 * ns]
        for i, (ai, kind) in enumerate(waits):
            for cp in _gather_copies(a_in[ai], kind, s_in[2 * i], s_in[2 * i + 1]):
                cp.wait_send()
                cp.wait_recv()
        for i, (ai, kind) in enumerate(starts):
            for cp in _gather_copies(a_in[ai], kind, s_out[2 * i], s_out[2 * i + 1]):
                cp.start()
        if ns:
            token = outs[2 * ns + n]
            token[...] = jnp.zeros_like(token)

    sem2 = pltpu.SemaphoreType.DMA((2,))
    flat_in = [s for pair in sems_in for s in pair]
    arrs = [pltpu.with_memory_space_constraint(a, pltpu.HBM) for a in arrs]
    outs = pl.pallas_call(
        body, name=name,
        out_shape=tuple([sem2] * (2 * ns) + [pltpu.HBM(a.shape, a.dtype) for a in arrs]
                        + ([_sds((8, 128), F32)] if ns else [])),
        in_specs=(HBM,) * n + (SEM,) * (2 * nw) + (ANY,) * len(after),
        out_specs=(SEM,) * (2 * ns) + (HBM,) * n + ((VMEM_FULL,) if ns else ()),
        input_output_aliases={i: 2 * ns + i for i in range(n)},
        compiler_params=pltpu.CompilerParams(has_side_effects=EFFECT),
    )(*arrs, *flat_in, *after)
    sems = [(outs[2 * i], outs[2 * i + 1]) for i in range(ns)]
    return sems, list(outs[2 * ns:2 * ns + n]), (outs[2 * ns + n] if ns else None)


def _rs_start(sb, name, after=None):
    _, half, cols = sb.shape

    def body(sb_ref, land_ref, after_ref, ssem, rsem, sb_out, land_out, token):
        x, y, c, chips = _place()
        for j, (cx, cy) in enumerate(chips):
            pltpu.make_async_remote_copy(src_ref=sb_ref.at[2 * cx + cy], dst_ref=land_ref.at[j], send_sem=ssem.at[j],
                                         recv_sem=rsem.at[j], device_id=(cx, cy, c), device_id_type=MESH).start()
        token[...] = jnp.zeros_like(token)

    sem3 = pltpu.SemaphoreType.DMA((3,))
    land = lax.empty((3, half, cols), BF16)
    return pl.pallas_call(
        body, name=name,
        out_shape=(sem3, sem3, pltpu.HBM(sb.shape, sb.dtype), pltpu.HBM(land.shape, land.dtype), _sds((8, 128), F32)),
        in_specs=(HBM, HBM, ANY), out_specs=(SEM, SEM, HBM, HBM, VMEM_FULL), input_output_aliases={0: 2, 1: 3},
        compiler_params=pltpu.CompilerParams(has_side_effects=EFFECT),
    )(pltpu.with_memory_space_constraint(sb, pltpu.HBM), pltpu.with_memory_space_constraint(land, pltpu.HBM), _dep(after))


def _rs_wait(items, after, name):
    n = len(items)

    def body(*refs):
        x, y, c, chips = _place()
        for i in range(n):
            sb_ref, land_ref, ssem_ref, rsem_ref = refs[4 * i:4 * i + 4]
            for j, (cx, cy) in enumerate(chips):
                cp = pltpu.make_async_remote_copy(src_ref=sb_ref.at[2 * cx + cy], dst_ref=land_ref.at[j],
                                                  send_sem=ssem_ref.at[j], recv_sem=rsem_ref.at[j],
                                                  device_id=(cx, cy, c), device_id_type=MESH)
                cp.wait_send()
                cp.wait_recv()

    outs = pl.pallas_call(
        body, name=name,
        out_shape=tuple(pltpu.HBM(a.shape, a.dtype) for it in items for a in it[:2]),
        in_specs=(HBM, HBM, SEM, SEM) * n + (ANY,), out_specs=(HBM,) * (2 * n),
        input_output_aliases={4 * i + k: 2 * i + k for i in range(n) for k in range(2)},
        compiler_params=pltpu.CompilerParams(has_side_effects=EFFECT),
    )(*[a for it in items for a in it], after)
    return [outs[2 * i + 1] for i in range(n)]


def _pair_copy(g_ref, land_ref, ssem, rsem):
    x, y, c, _ = _place()
    h = g_ref.shape[1] // 2
    src = g_ref.at[:, pl.ds(pl.multiple_of((1 - c) * h, 8), h), :]
    return pltpu.make_async_remote_copy(src_ref=src, dst_ref=land_ref, send_sem=ssem.at[0], recv_sem=rsem.at[0],
                                        device_id=(x, y, 1 - c), device_id_type=MESH)


def _pair_start(g, name):
    def body(g_ref, land_ref, ssem, rsem, g_out, land_out, token):
        _pair_copy(g_ref, land_ref, ssem, rsem).start()
        token[...] = jnp.zeros_like(token)

    sem1 = pltpu.SemaphoreType.DMA((1,))
    land = lax.empty((N_CHIP, g.shape[1] // 2, g.shape[2]), F32)
    return pl.pallas_call(
        body, name=name,
        out_shape=(sem1, sem1, pltpu.HBM(g.shape, g.dtype), pltpu.HBM(land.shape, land.dtype), _sds((8, 128), F32)),
        in_specs=(HBM, HBM), out_specs=(SEM, SEM, HBM, HBM, VMEM_FULL), input_output_aliases={0: 2, 1: 3},
        compiler_params=pltpu.CompilerParams(has_side_effects=EFFECT),
    )(pltpu.with_memory_space_constraint(g, pltpu.HBM), pltpu.with_memory_space_constraint(land, pltpu.HBM))


def _pair_wait(g, land, ssem, rsem, after, name):
    def body(g_ref, land_ref, ssem_ref, rsem_ref, after_ref, g_out, land_out):
        cp = _pair_copy(g_ref, land_ref, ssem_ref, rsem_ref)
        cp.wait_send()
        cp.wait_recv()

    return pl.pallas_call(
        body, name=name,
        out_shape=(pltpu.HBM(g.shape, g.dtype), pltpu.HBM(land.shape, land.dtype)),
        in_specs=(HBM, HBM, SEM, SEM, ANY), out_specs=(HBM, HBM), input_output_aliases={0: 0, 1: 1},
        compiler_params=pltpu.CompilerParams(has_side_effects=EFFECT),
    )(g, land, ssem, rsem, after)


def _pair_sum(g, rcv, place, name):
    _, rows, cols = g.shape
    half = rows // 2
    tr = 256
    nt = half // tr

    def body(p_ref, g_ref, r_ref, sb_ref, sf_ref):
        s = pl.program_id(1)
        tot = g_ref[0] + r_ref[0]
        sb_ref[0] = tot.astype(BF16)

        @pl.when(s == p_ref[0])
        def _():
            sf_ref[...] = tot

    grid_spec = pltpu.PrefetchScalarGridSpec(
        num_scalar_prefetch=1, grid=(nt, N_CHIP),
        in_specs=[pl.BlockSpec((1, tr, cols), lambda t, s, p: (s, p[1] * nt + t, 0)),
                  pl.BlockSpec((1, tr, cols), lambda t, s, p: (s, t, 0))],
        out_specs=[pl.BlockSpec((1, tr, cols), lambda t, s, p: (s, t, 0)),
                   pl.BlockSpec((tr, cols), lambda t, s, p: (t, 0))])
    return pl.pallas_call(
        body, name=name, grid_spec=grid_spec,
        out_shape=[_sds((N_CHIP, half, cols), BF16), _sds((half, cols), F32)],
        compiler_params=_cp(("arbitrary", "arbitrary")),
    )(place, g, rcv)


def _final_sum(sf, rb, place, name):
    half, cols = sf.shape
    tr = 256
    nt = half // tr

    def body(p_ref, sf_ref, r_ref, out_ref):
        acc = sf_ref[...]
        for j in range(3):
            acc = acc + r_ref[j].astype(F32)
        out_ref[...] = acc

    grid_spec = pltpu.PrefetchScalarGridSpec(
        num_scalar_prefetch=1, grid=(nt,),
        in_specs=[pl.BlockSpec((tr, cols), lambda t, p: (t, 0)), pl.BlockSpec((3, tr, cols), lambda t, p: (0, t, 0))],
        out_specs=pl.BlockSpec((tr, cols), lambda t, p: (p[1] * nt + t, 0)))
    return pl.pallas_call(
        body, name=name, grid_spec=grid_spec,
        out_shape=_sds((2 * half, cols), F32),
        compiler_params=_cp(("arbitrary",)),
    )(place, sf, rb)


def _half_copy(f_ref, which, ssem, rsem):
    x, y, c, _ = _place()
    h = f_ref.shape[0] // 2
    rows = f_ref.at[pl.ds(pl.multiple_of(which * h, 8), h), :]
    return pltpu.make_async_remote_copy(src_ref=rows, dst_ref=rows, send_sem=ssem.at[0], recv_sem=rsem.at[0],
                                        device_id=(x, y, 1 - c), device_id_type=MESH)


def _half_start(fulls, name, after=None):
    n = len(fulls)

    def body(*refs):
        for i in range(n):
            _half_copy(refs[i], lax.axis_index("c"), refs[n + 1 + 2 * i], refs[n + 2 + 2 * i]).start()
        token = refs[4 * n + 1]
        token[...] = jnp.zeros_like(token)

    sem1 = pltpu.SemaphoreType.DMA((1,))
    outs = pl.pallas_call(
        body, name=name,
        out_shape=tuple([sem1] * (2 * n) + [pltpu.HBM(f.shape, f.dtype) for f in fulls] + [_sds((8, 128), F32)]),
        in_specs=(HBM,) * n + (ANY,), out_specs=(SEM,) * (2 * n) + (HBM,) * n + (VMEM_FULL,),
        input_output_aliases={i: 2 * n + i for i in range(n)},
        compiler_params=pltpu.CompilerParams(has_side_effects=EFFECT),
    )(*[pltpu.with_memory_space_constraint(f, pltpu.HBM) for f in fulls], _dep(after))
    return [(outs[2 * i], outs[2 * i + 1], outs[2 * n + i]) for i in range(n)], outs[3 * n]


def _half_wait(items, after, name):
    n = len(items)

    def body(*refs):
        c = lax.axis_index("c")
        for i in range(n):
            ssem_ref, rsem_ref, f_ref = refs[3 * i:3 * i + 3]
            _half_copy(f_ref, c, ssem_ref, rsem_ref).wait_send()
            _half_copy(f_ref, 1 - c, ssem_ref, rsem_ref).wait_recv()

    return pl.pallas_call(
        body, name=name, out_shape=tuple(pltpu.HBM(it[2].shape, it[2].dtype) for it in items),
        in_specs=(SEM, SEM, HBM) * n + (ANY,) * len(after), out_specs=(HBM,) * n,
        input_output_aliases={3 * i + 2: i for i in range(n)},
        compiler_params=pltpu.CompilerParams(has_side_effects=EFFECT),
    )(*[a for it in items for a in it], *after)


def _small_copies(src_ref, land_ref, ssem, rsem, first):
    x, y, c, chips = _place()
    if first:
        return [pltpu.make_async_remote_copy(src_ref=src_ref, dst_ref=land_ref, send_sem=ssem.at[0], recv_sem=rsem.at[0],
                                             device_id=(x, y, 1 - c), device_id_type=MESH)]
    return [pltpu.make_async_remote_copy(src_ref=src_ref, dst_ref=land_ref.at[j], send_sem=ssem.at[j], recv_sem=rsem.at[j],
                                         device_id=(*chip, c), device_id_type=MESH) for j, chip in enumerate(chips)]


def _small_start(src, first, name, after=None):
    n = 1 if first else 3

    def body(src_ref, land_ref, after_ref, ssem, rsem, src_out, land_out, token):
        for cp in _small_copies(src_ref, land_ref, ssem, rsem, first):
            cp.start()
        token[...] = jnp.zeros_like(token)

    sems = pltpu.SemaphoreType.DMA((n,))
    land = lax.empty(src.shape if first else (3,) + src.shape, F32)
    return pl.pallas_call(
        body, name=name,
        out_shape=(sems, sems, pltpu.HBM(src.shape, F32), pltpu.HBM(land.shape, F32), _sds((8, 128), F32)),
        in_specs=(HBM, HBM, ANY), out_specs=(SEM, SEM, HBM, HBM, VMEM_FULL), input_output_aliases={0: 2, 1: 3},
        compiler_params=pltpu.CompilerParams(has_side_effects=EFFECT),
    )(pltpu.with_memory_space_constraint(src, pltpu.HBM), pltpu.with_memory_space_constraint(land, pltpu.HBM), _dep(after))


def _small_wait(src, land, ssem, rsem, first, after, name):
    def body(src_ref, land_ref, ssem_ref, rsem_ref, after_ref, src_out, land_out):
        for cp in _small_copies(src_ref, land_ref, ssem_ref, rsem_ref, first):
            cp.wait_send()
            cp.wait_recv()

    return pl.pallas_call(
        body, name=name,
        out_shape=(pltpu.HBM(src.shape, F32), pltpu.HBM(land.shape, F32)),
        in_specs=(HBM, HBM, SEM, SEM, ANY), out_specs=(HBM, HBM), input_output_aliases={0: 0, 1: 1},
        compiler_params=pltpu.CompilerParams(has_side_effects=EFFECT),
    )(src, land, ssem, rsem, after)


def _small_pair_sum(vec, got):
    def body(v_ref, g_ref, o_ref):
        o_ref[...] = v_ref[...] + g_ref[...]

    return pl.pallas_call(body, name="small_pair_sum", in_specs=[VMEM_FULL] * 2, out_specs=VMEM_FULL,
                          out_shape=_sds(vec.shape, F32), compiler_params=_cp())(vec, got)


def _small_chip_sum(pair, got, place):
    def body(p_ref, pair_ref, got_ref, o_ref):
        acc = None
        for kk in range(N_CHIP):
            d = jnp.bitwise_xor(p_ref[0], kk)
            t = jnp.where(d == 0, pair_ref[...], jnp.where(d == 2, got_ref[0], jnp.where(d == 1, got_ref[1], got_ref[2])))
            acc = t if acc is None else acc + t
        o_ref[...] = acc

    grid_spec = pltpu.PrefetchScalarGridSpec(
        num_scalar_prefetch=1, grid=(1,),
        in_specs=[pl.BlockSpec(pair.shape, lambda i, p: (0, 0)), pl.BlockSpec(got.shape, lambda i, p: (0, 0, 0))],
        out_specs=pl.BlockSpec(pair.shape, lambda i, p: (0, 0)))
    return pl.pallas_call(body, name="small_chip_sum", grid_spec=grid_spec, out_shape=_sds(pair.shape, F32),
                          compiler_params=_cp(("arbitrary",)))(place, pair, got)


def _adam_math(w, g, m, v):
    m = B1 * m + (1.0 - B1) * g
    v = B2 * v + (1.0 - B2) * (g * g)
    m_hat = m / (1.0 - B1 ** STEP)
    v_hat = v / (1.0 - B2 ** STEP)
    delta = -LR * (m_hat / (jnp.sqrt(v_hat) + AEPS) + WD * w)
    return delta, m, v


def _adam_big(w, g, m, v, name, dep=None):
    rows, cols = w.shape
    tr = 128

    def body(w_ref, g_ref, m_ref, v_ref, dep_ref, go_ref, d_ref, nm_ref, nv_ref):
        g = g_ref[...]
        d, nm, nv = _adam_math(w_ref[...], g, m_ref[...], v_ref[...])
        go_ref[...] = g
        d_ref[...] = d
        nm_ref[...] = nm
        nv_ref[...] = nv

    blk = pl.BlockSpec((tr, cols), lambda i: (i, 0))
    return pl.pallas_call(
        body, name=name, grid=(rows // tr,),
        in_specs=[blk] * 4 + [ANY], out_specs=[blk] * 4, out_shape=[_sds((rows, cols), F32)] * 4,
        compiler_params=_cp(("arbitrary",)),
    )(w, g, m, v, _dep(dep))


def _adam_small(ws, gs, ms, vs, dep=None):
    n = len(ws)

    def body(*refs):
        for i in range(n):
            d, nm, nv = _adam_math(refs[i][...], refs[n + i][...], refs[2 * n + i][...], refs[3 * n + i][...])
            refs[4 * n + 1 + i][...] = d
            refs[5 * n + 1 + i][...] = nm
            refs[6 * n + 1 + i][...] = nv

    shapes = [_sds(w.shape, F32) for w in ws]
    outs = pl.pallas_call(
        body, name="adam_small",
        in_specs=[VMEM_FULL] * (4 * n) + [ANY], out_specs=[VMEM_FULL] * (3 * n), out_shape=shapes * 3,
        compiler_params=_cp(),
    )(*ws, *gs, *ms, *vs, _dep(dep))
    return outs[:n], outs[n:2 * n], outs[2 * n:]


def _pad_rows8(a):
    flat = a.reshape(-1, 128)
    pad = (-flat.shape[0]) % 8
    if pad:
        flat = jnp.concatenate([flat, jnp.zeros((pad, 128), F32)], axis=0)
    return flat


def kernel(x, meta_tokens, norm1_w, w_in, gate_w2, gate_b, gla_norm_w, pool_w, pool_scale, w_out, norm2_w, mlp_w1, mlp_w2, final_norm_w, loss_target, m_meta_tokens, m_norm1_w, m_w_in, m_gate_w2, m_gate_b, m_gla_norm_w, m_pool_w, m_pool_scale, m_w_out, m_norm2_w, m_mlp_w1, m_mlp_w2, m_final_norm_w, v_meta_tokens, v_norm1_w, v_w_in, v_gate_w2, v_gate_b, v_gla_norm_w, v_pool_w, v_pool_scale, v_w_out, v_norm2_w, v_mlp_w1, v_mlp_w2, v_final_norm_w):
    cx, cy, cc = lax.axis_index("x"), lax.axis_index("y"), lax.axis_index("c")
    me = (2 * cx + cy).astype(jnp.int32)

    place = jnp.stack([me, cc.astype(jnp.int32)])
    fw = final_norm_w.reshape(1, D)

    mine = jnp.concatenate([meta_tokens.reshape(64, 128), gate_w2[0], pool_w[0].reshape(512, 128)], axis=0)
    small = lax.dynamic_update_slice(jnp.zeros((N_CHIP, 592, 128), F32), mine[None], (me, 0, 0))
    (s_sm,), (r_sm,), (f_sm,), tok = _gather_start([small], "gather_start_small", [True])
    (sem_win_d,), (win,), tok = _gather_step("gather_start_win", [_cast_win(w_in[0], place, tok)], [], [(0, "d")])
    wout, w1, w2 = (_cast_into(w_out[0], place, D, "cast_wout", tok), _cast_into(mlp_w1[0], place, D, "cast_w1", tok),
                    _cast_into(mlp_w2[0], place, D, "cast_w2", tok))
    small = _gather_wait(f_sm, s_sm, r_sm, w2, "gather_wait_small", True)
    metaF = jnp.concatenate([small[k, 0:64].reshape(N_META, 512) for k in range(N_CHIP)], axis=1)
    gw2F = jnp.concatenate([small[k, 64:80] for k in range(N_CHIP)], axis=1)
    pwF = jnp.concatenate([small[k, 80:592].reshape(4, 64, GC) for k in range(N_CHIP)], axis=1)

    fly = {"win": win, "wout": wout, "w1": w1, "w2": w2}
    sems = {"win_d": sem_win_d}

    def step(name, names, waits, starts, after):
        at = {nm: i for i, nm in enumerate(names)}
        new, arrs, tok = _gather_step(name, [fly[nm] for nm in names], [(at[nm], k) for nm, k in waits],
                                      [(at[nm], k) for nm, k in starts], [sems[nm + "_" + k] for nm, k in waits], after)
        fly.update(zip(names, arrs))
        sems.update({nm + "_" + k: s for (nm, k), s in zip(starts, new)})
        return tok

    def relay_first():
        return step("gather_relay_win", ["win", "wout", "w1"], [("win", "d")],
                    [("win", "r"), ("win", "fx"), ("wout", "d"), ("w1", "d")], [])

    adam_win = {}

    def get_win(after):
        tok = step("gather_land_win", ["win"], [("win", "r")], [("win", "fd")], [after])
        adam_win["m"] = _late(m_w_in[0], tok)
        step("gather_wait_win", ["win"], [("win", "fx"), ("win", "fd")], [], [tok])
        return fly["win"]

    def relay_mid(after):
        return step("gather_relay_mid", ["wout", "w1", "w2"], [("wout", "d"), ("w1", "d")],
                    [("wout", "r"), ("w1", "r"), ("w2", "d"), ("wout", "fx"), ("w1", "fx")], [after, adam_win["m"]])

    def land_wout(after):
        return step("gather_land_wout", ["wout"], [("wout", "r")], [("wout", "fd")], [after])

    def get_wout(after):
        step("gather_wait_wout", ["wout"], [("wout", "fx"), ("wout", "fd")], [], [after])
        tok = step("gather_land_w1", ["w1"], [("w1", "r")], [("w1", "fd")], [fly["wout"]])
        return fly["wout"].reshape(D, D), tok

    def get_w1(after):
        step("gather_wait_w1", ["w1"], [("w1", "fx"), ("w1", "fd")], [], [after])
        return fly["w1"]

    def relay_last(after):
        return step("gather_relay_w2", ["w2"], [("w2", "d")], [("w2", "r"), ("w2", "fx")], [after])

    def get_w2(after):
        tok = step("gather_land_w2", ["w2"], [("w2", "r")], [("w2", "fd")], [after])
        step("gather_wait_w2", ["w2"], [("w2", "fx"), ("w2", "fd")], [], [tok])
        return fly["w2"].reshape(DFF, D)

    pairs, pending = {}, {}

    def grad_start(nm, g):
        ssem, rsem, g_thru, land, token = _pair_start(g, "pair_start_" + nm)
        pairs[nm] = (ssem, rsem, g_thru, land)
        return token

    def grad_finish(nm, after):
        ssem, rsem, g_thru, land = pairs[nm]
        g, rcv = _pair_wait(g_thru, land, ssem, rsem, after, "pair_wait_" + nm)
        sb, sf = _pair_sum(g, rcv, place, "pair_sum_" + nm)
        if nm == "win":
            pending[nm] = (sf, sb)
            return sf
        ssem, rsem, sb_thru, land, token = _rs_start(sb, "rs_start_" + nm)
        pending[nm] = (sf, ssem, rsem, sb_thru, land)
        return token

    (grad_x, loss8, d_n1w, d_gb, d_gnw, d_ps, d_n2w, d_fw, d_meta, d_gw2, d_pw) = _local_step(
        x[0], loss_target[0], dict(relay_first=relay_first, win=get_win, relay_mid=relay_mid, land_wout=land_wout,
                                   wout=get_wout, w1=get_w1, relay_last=relay_last, w2=get_w2),
        metaF, gw2F, pwF, norm1_w, gate_b, gla_norm_w, pool_scale, norm2_w, fw, grad_start, grad_finish)
    return _reduce_and_update(
        me, place, pending, adam_win, grad_x, loss8, d_n1w, d_gb, d_gnw, d_ps, d_n2w, d_fw, d_meta, d_gw2, d_pw,
        meta_tokens, norm1_w, w_in, gate_w2, gate_b, gla_norm_w, pool_w, pool_scale, w_out, norm2_w, mlp_w1, mlp_w2, fw,
        m_meta_tokens, m_norm1_w, m_w_in, m_gate_w2, m_gate_b, m_gla_norm_w, m_pool_w, m_pool_scale, m_w_out, m_norm2_w,
        m_mlp_w1, m_mlp_w2, m_final_norm_w, v_meta_tokens, v_norm1_w, v_w_in, v_gate_w2, v_gate_b, v_gla_norm_w, v_pool_w,
        v_pool_scale, v_w_out, v_norm2_w, v_mlp_w1, v_mlp_w2, v_final_norm_w)


def _late(x, token):
    return x * (1.0 + token[0, 0])


def _local_step(x, target, gather, metaF, gw2F, pwF, norm1_w, gate_b, gla_norm_w, pool_scale, norm2_w, fw, grad_start,
                grad_finish):
    h0, u = _embed_norm(x, metaF, norm1_w, gather["relay_first"]())
    Win = gather["win"](u)
    P = _in_proj(u, Win)
    gw2p = jnp.pad(gw2F, ((0, 128 - RANK), (0, 0)))
    yb, op = _pool_fwd(P, pwF, pool_scale, gather["relay_mid"](P))
    o, og, sp = _gla_fwd(P, gw2p, gate_b, gla_norm_w, gather["land_wout"](op))
    Wout, tok = gather["wout"](og)
    h1 = _out_proj(og, op, Wout, h0, tok)
    n2 = _norm_rows(h1, norm2_w, "norm2")
    W1 = gather["w1"](n2)
    zr, a, tok = _mlp_up(n2, W1, 0)
    zr, a, _ = _mlp_up(n2, W1, 1, (zr, a), gather["relay_last"](tok))
    W2 = gather["w2"](a)
    h2 = _mlp_down(a, W2, h1)

    dh2, dh2b, d_fw, loss8 = _loss_head(h2, target, fw)
    tok = grad_start("w2", _grad_w2(a, dh2b).reshape(N_CHIP, D, D))
    dz = _mlp_dz(dh2b, W2, zr, tok)
    tok = grad_finish("w2", dz)
    tok = grad_start("w1", _grad_w1(n2, dz, tok))
    dn2 = _mlp_dn(dz, W1, tok)
    tok = grad_finish("w1", dn2)
    dh1, dh1b, d_n2w = _norm_bwd(dn2, h1, dh2, norm2_w, "norm2_bwd", tok)
    dmixed = _mixed_grad(dh1b, Wout)
    tok = grad_start("wout", _grad_wout(og, op, dh1b))
    dpu, d_pw, d_ps = _pool_bwd(dmixed, yb, pwF, pool_scale, tok)
    dq, dk, dv, dr, dglr, d_gw2p, d_gb, d_gnw = _gla_bwd(dmixed, o, P, gw2p, gate_b, gla_norm_w, sp, tok)
    d_gw2 = d_gw2p[0:RANK]
    tok = grad_finish("wout", dq)
    tok = grad_start("win", _grad_win(u, dq, dk, dv, dr, dglr, dpu, tok))
    du = _in_grad(dq, dk, dv, dr, dglr, dpu, Win, tok)
    tok = grad_finish("win", du)
    grad_x, d_meta, d_n1w = _input_grad(du, h0, dh1, norm1_w, tok)
    return grad_x, loss8, d_n1w, d_gb, d_gnw, d_ps, d_n2w, d_fw, d_meta, d_gw2, d_pw


def _reduce_and_update(me, place, pending, adam_win, grad_x, loss8, d_n1w, d_gb, d_gnw, d_ps, d_n2w, d_fw, d_meta, d_gw2, d_pw,
                       meta_tokens, norm1_w, w_in, gate_w2, gate_b, gla_norm_w, pool_w, pool_scale, w_out, norm2_w,
                       mlp_w1, mlp_w2, fw, m_meta_tokens, m_norm1_w, m_w_in, m_gate_w2, m_gate_b, m_gla_norm_w, m_pool_w,
                       m_pool_scale, m_w_out, m_norm2_w, m_mlp_w1, m_mlp_w2, m_final_norm_w, v_meta_tokens, v_norm1_w, v_w_in,
                       v_gate_w2, v_gate_b, v_gla_norm_w, v_pool_w, v_pool_scale, v_w_out, v_norm2_w, v_mlp_w1, v_mlp_w2,
                       v_final_norm_w):
    parts = [loss8, d_n1w, d_gb, d_gnw, d_ps, d_n2w, d_fw, d_meta, d_gw2, d_pw]
    packed = [_pad_rows8(p) for p in parts]
    sizes = [p.shape[0] for p in packed]
    vec = jnp.concatenate(packed, axis=0)

    big, halves = {}, {}
    params = {"w2": (mlp_w2[0], m_mlp_w2[0], v_mlp_w2[0]), "w1": (mlp_w1[0], m_mlp_w1[0], v_mlp_w1[0]),
              "wout": (w_out[0], m_w_out[0], v_w_out[0])}

    def reduce_(names, after, tag):
        items = [(pending[nm][3], pending[nm][4], pending[nm][1], pending[nm][2]) for nm in names]
        landed = _rs_wait(items, after, "rs_wait_" + tag)
        fulls = [_final_sum(pending[nm][0], rb, place, "final_sum_" + nm) for nm, rb in zip(names, landed)]
        sent, token = _half_start(fulls, "half_start_" + tag)
        halves.update(zip(names, sent))
        if "win" not in params:
            params["win"] = (w_in[0], adam_win["m"], _late(v_w_in[0], token))
        return token

    def update(names, after, tag):
        fulls = _half_wait([halves[nm] for nm in names], [after, params["win"][2]], "half_wait_" + tag)
        tok = None
        for nm, full in zip(names, fulls):
            w, m, v = params[nm]
            big[nm] = _adam_big(w, full, m, v, "adam_" + nm, tok)
            tok = big[nm][3]
        return tok

    first = ["w2", "w1", "wout"]
    s1, r1, vec, land1, tok = _small_start(vec, True, "small_start_pair")
    tok = reduce_(first, tok, "mlp_wout")
    vec, got = _small_wait(vec, land1, s1, r1, True, tok, "small_wait_pair")
    pair = _small_pair_sum(vec, got)
    s2, r2, pair, land2, tok = _small_start(pair, False, "small_start_chips")
    sf, sb = pending["win"]
    ssem, rsem, sb_thru, land, tok = _rs_start(sb, "rs_start_win", tok)
    pending["win"] = (sf, ssem, rsem, sb_thru, land)
    tok = update(first, tok, "mlp_wout")
    pair, got = _small_wait(pair, land2, s2, r2, False, tok, "small_wait_chips")
    red = _small_chip_sum(pair, got, place)
    tok = reduce_(["win"], red, "win")
    after = update(["win"], tok, "win")
    offs = [0]
    for s in sizes:
        offs.append(offs[-1] + s)

    def take(i, shape):
        n = 1
        for d in shape:
            n *= d
        return red[offs[i]:offs[i] + n // 128].reshape(shape)

    loss = red[0, 0]
    G_n1w = take(1, (1, D))
    G_gb = take(2, (1, KW))
    G_gnw = take(3, (1, DV))
    G_ps = take(4, (1, PW))
    G_n2w = take(5, (1, D))
    G_fw = take(6, (1, D))
    G_meta = lax.dynamic_slice(take(7, (N_META, D)), (0, me * 512), (N_META, 512))
    G_gw2 = lax.dynamic_slice(take(8, (RANK, KW)), (0, me * 128), (RANK, 128))
    G_pw = lax.dynamic_slice(take(9, (4, GC, GC)), (0, me * 64, 0), (4, 64, GC))

    G_win, d_win, nm_win, nv_win = big["win"]
    G_wout, d_wout, nm_wout, nv_wout = big["wout"]
    G_w1, d_w1, nm_w1, nv_w1 = big["w1"]
    G_w2, d_w2, nm_w2, nv_w2 = big["w2"]
    ws = [meta_tokens, norm1_w, gate_w2[0], gate_b, gla_norm_w, pool_w[0], pool_scale, norm2_w, fw]
    gs = [G_meta, G_n1w, G_gw2, G_gb, G_gnw, G_pw, G_ps, G_n2w, G_fw]
    ms = [m_meta_tokens, m_norm1_w, m_gate_w2[0], m_gate_b, m_gla_norm_w, m_pool_w[0], m_pool_scale, m_norm2_w,
          m_final_norm_w.reshape(1, D)]
    vs = [v_meta_tokens, v_norm1_w, v_gate_w2[0], v_gate_b, v_gla_norm_w, v_pool_w[0], v_pool_scale, v_norm2_w,
          v_final_norm_w.reshape(1, D)]
    ds, nms, nvs = _adam_small(ws, gs, ms, vs, after)

    def assemble(small, win_, wout_, w1_, w2_):
        meta_, n1_, gw2_, gb_, gnw_, pw_, ps_, n2_, fw_ = small
        return (meta_, n1_, win_[None], gw2_[None], gb_, gnw_, pw_[None], ps_, wout_[None], n2_, w1_[None], w2_[None],
                fw_.reshape(D))

    grads_out = assemble(gs, G_win, G_wout, G_w1, G_w2)
    deltas = assemble(ds, d_win, d_wout, d_w1, d_w2)
    new_m = assemble(nms, nm_win, nm_wout, nm_w1, nm_w2)
    new_v = assemble(nvs, nv_win, nv_wout, nv_w1, nv_w2)
    return (loss, grad_x[None], *grads_out, *deltas, *new_m, *new_v)
```

```python
import functools

import jax
import jax.numpy as jnp
from jax import lax
from jax.experimental import pallas as pl
from jax.experimental.pallas import tpu as pltpu

F32 = jnp.float32
BF16 = jnp.bfloat16

D = 2048
SEQ = 2048
N_META = 16
CH = 64
TP = 2176
NCH = TP // CH
ROW_LO = 112
X_LO = 128
ROW_HI = TP
XT = 128
NXT = TP // XT
HEADS = 4
DK = 128
DV = 256
KW = HEADS * DK
GW = HEADS * DV
RANK = 16
TAU = 16.0
WINDOWS = (2, 4, 8, 16)
PW = 1024
GC = 256
DFF = 8192
EPS = 1e-6
SHARD_IN = 1028
PAD_IN = 1152
N_CHIP = 4

LR = 0.001
B1 = 0.9
B2 = 0.999
AEPS = 1e-08
WD = 0.01
STEP = 10

VMEM_LIMIT = 60 * 1024 * 1024
ANY = pl.BlockSpec(memory_space=pl.ANY)
VMEM_FULL = pl.BlockSpec(memory_space=pltpu.VMEM)
MESH = pl.DeviceIdType.MESH


def _cp(sem=None):
    if sem is None:
        return pltpu.CompilerParams(vmem_limit_bytes=VMEM_LIMIT)
    return pltpu.CompilerParams(dimension_semantics=sem, vmem_limit_bytes=VMEM_LIMIT)


def _dot(a, b):
    return jnp.dot(a, b, preferred_element_type=F32)


def _dot_nt(a, b):
    return lax.dot_general(a, b, (((1,), (1,)), ((), ())), preferred_element_type=F32)


def _dot_tn(a, b):
    return lax.dot_general(a, b, (((0,), (0,)), ((), ())), preferred_element_type=F32)


def _sds(shape, dtype):
    return jax.ShapeDtypeStruct(shape, dtype)


def _embed_norm(x, meta_full, w, dep=None):
    def body(x_ref, meta_ref, w_ref, dep_ref, h_ref, u_ref):
        i = pl.program_id(0)

        @pl.when(i == 0)
        def _():
            h_ref[...] = jnp.zeros_like(h_ref)
            h_ref[ROW_LO:X_LO, :] = meta_ref[...]

        @pl.when(i >= 1)
        def _():
            h_ref[...] = x_ref[...]

        h = h_ref[...]
        r = lax.rsqrt(jnp.mean(h * h, axis=-1, keepdims=True) + EPS)
        u_ref[...] = ((h * r) * w_ref[...]).astype(BF16)

    return pl.pallas_call(
        body, name="embed_norm1", grid=(NXT,),
        in_specs=[pl.BlockSpec((XT, D), lambda i: (jnp.maximum(i - 1, 0), 0)),
                  pl.BlockSpec((N_META, D), lambda i: (0, 0)),
                  pl.BlockSpec((1, D), lambda i: (0, 0)), ANY],
        out_specs=[pl.BlockSpec((XT, D), lambda i: (i, 0)), pl.BlockSpec((XT, D), lambda i: (i, 0))],
        out_shape=[_sds((TP, D), F32), _sds((TP, D), BF16)],
        compiler_params=_cp(("arbitrary",)),
    )(x, meta_full, w, _dep(dep))


def _norm_rows(h, w, name):
    tr = 272

    def body(h_ref, w_ref, o_ref):
        hv = h_ref[...]
        r = lax.rsqrt(jnp.mean(hv * hv, axis=-1, keepdims=True) + EPS)
        o_ref[...] = ((hv * r) * w_ref[...]).astype(BF16)

    return pl.pallas_call(
        body, name=name, grid=(TP // tr,),
        in_specs=[pl.BlockSpec((tr, D), lambda i: (i, 0)), pl.BlockSpec((1, D), lambda i: (0, 0))],
        out_specs=pl.BlockSpec((tr, D), lambda i: (i, 0)),
        out_shape=_sds((TP, D), BF16),
        compiler_params=_cp(("arbitrary",)),
    )(h, w)


def _loss_head(h2, target, fw):
    def body(h_ref, t_ref, w_ref, dh_ref, dhb_ref, dw_ref, loss_ref):
        i = pl.program_id(0)

        @pl.when(i == 0)
        def _():
            dw_ref[...] = jnp.zeros_like(dw_ref)
            loss_ref[...] = jnp.zeros_like(loss_ref)

        h = h_ref[...]
        w = w_ref[...]
        r = lax.rsqrt(jnp.mean(h * h, axis=-1, keepdims=True) + EPS)
        xh = h * r
        y = xh * w
        is_x = (i >= 1).astype(F32)
        diff = (y - t_ref[...]) * is_x
        loss_ref[...] += jnp.sum(diff * diff) * (0.5 / D)
        dy = diff * (1.0 / D)
        dw_ref[...] += jnp.sum(dy * xh, axis=0, keepdims=True)
        gx = dy * w
        dh = r * (gx - xh * jnp.mean(gx * xh, axis=-1, keepdims=True))
        dh_ref[...] = dh
        dhb_ref[...] = dh.astype(BF16)

    return pl.pallas_call(
        body, name="loss_head", grid=(NXT,),
        in_specs=[pl.BlockSpec((XT, D), lambda i: (i, 0)),
                  pl.BlockSpec((XT, D), lambda i: (jnp.maximum(i - 1, 0), 0)),
                  pl.BlockSpec((1, D), lambda i: (0, 0))],
        out_specs=[pl.BlockSpec((XT, D), lambda i: (i, 0)), pl.BlockSpec((XT, D), lambda i: (i, 0)),
                   pl.BlockSpec((1, D), lambda i: (0, 0)), pl.BlockSpec((8, 128), lambda i: (0, 0))],
        out_shape=[_sds((TP, D), F32), _sds((TP, D), BF16), _sds((1, D), F32), _sds((8, 128), F32)],
        compiler_params=_cp(("arbitrary",)),
    )(h2, target, fw)


def _norm_bwd(dn, h, dres, w, name, dep=None):
    tr = 272

    def body(dn_ref, h_ref, dres_ref, w_ref, dep_ref, o_ref, ob_ref, dw_ref):
        @pl.when(pl.program_id(0) == 0)
        def _():
            dw_ref[...] = jnp.zeros_like(dw_ref)

        hv = h_ref[...]
        dnv = dn_ref[...]
        r = lax.rsqrt(jnp.mean(hv * hv, axis=-1, keepdims=True) + EPS)
        xh = hv * r
        dw_ref[...] += jnp.sum(dnv * xh, axis=0, keepdims=True)
        gx = dnv * w_ref[...]
        dh = dres_ref[...] + r * (gx - xh * jnp.mean(gx * xh, axis=-1, keepdims=True))
        o_ref[...] = dh
        ob_ref[...] = dh.astype(BF16)

    row = pl.BlockSpec((tr, D), lambda i: (i, 0))
    vec = pl.BlockSpec((1, D), lambda i: (0, 0))
    return pl.pallas_call(
        body, name=name, grid=(TP // tr,),
        in_specs=[row, row, row, vec, ANY], out_specs=[row, row, vec],
        out_shape=[_sds((TP, D), F32), _sds((TP, D), BF16), _sds((1, D), F32)],
        compiler_params=_cp(("arbitrary",)),
    )(dn, h, dres, w, _dep(dep))


def _input_grad(du, h0, dh1, w, dep=None):
    def body(du_ref, h_ref, dres_ref, w_ref, dep_ref, gx_ref, gm_ref, dw_ref):
        i = pl.program_id(0)

        @pl.when(i == 0)
        def _():
            dw_ref[...] = jnp.zeros_like(dw_ref)

        hv = h_ref[...]
        dnv = du_ref[...]
        r = lax.rsqrt(jnp.mean(hv * hv, axis=-1, keepdims=True) + EPS)
        xh = hv * r
        dw_ref[...] += jnp.sum(dnv * xh, axis=0, keepdims=True)
        g = dnv * w_ref[...]
        dh = dres_ref[...] + r * (g - xh * jnp.mean(g * xh, axis=-1, keepdims=True))

        @pl.when(i == 0)
        def _():
            gm_ref[...] = dh[ROW_LO:X_LO, :]

        @pl.when(i >= 1)
        def _():
            gx_ref[...] = dh

    row = pl.BlockSpec((XT, D), lambda i: (i, 0))
    vec = pl.BlockSpec((1, D), lambda i: (0, 0))
    return pl.pallas_call(
        body, name="input_grad", grid=(NXT,),
        in_specs=[row, row, row, vec, ANY],
        out_specs=[pl.BlockSpec((XT, D), lambda i: (jnp.maximum(i - 1, 0), 0)),
                   pl.BlockSpec((N_META, D), lambda i: (0, 0)), vec],
        out_shape=[_sds((SEQ, D), F32), _sds((N_META, D), F32), _sds((1, D), F32)],
        compiler_params=_cp(("arbitrary",)),
    )(du, h0, dh1, w, _dep(dep))


def _in_proj(u, wg):
    def body(u_ref, w_ref, o_ref):
        o_ref[0] = _dot(u_ref[...], w_ref[0])

    return pl.pallas_call(
        body, name="in_proj", grid=(N_CHIP,),
        in_specs=[VMEM_FULL, pl.BlockSpec((1, D, PAD_IN), lambda k: (k, 0, 0))],
        out_specs=pl.BlockSpec((1, TP, PAD_IN), lambda k: (k, 0, 0)),
        out_shape=_sds((N_CHIP, TP, PAD_IN), F32),
        compiler_params=_cp(("arbitrary",)),
    )(u, wg)


def _out_proj(og, op, wout, h0, dep=None):
    tn = 512

    def body(og_ref, op_ref, w_ref, h_ref, dep_ref, o_ref):
        acc = _dot(og_ref[...], w_ref[0:GW, :]) + _dot(op_ref[...], w_ref[GW:D, :])
        o_ref[...] = h_ref[...] + acc

    return pl.pallas_call(
        body, name="out_proj", grid=(D // tn,),
        in_specs=[VMEM_FULL, VMEM_FULL, pl.BlockSpec((D, tn), lambda j: (0, j)),
                  pl.BlockSpec((TP, tn), lambda j: (0, j)), ANY],
        out_specs=pl.BlockSpec((TP, tn), lambda j: (0, j)),
        out_shape=_sds((TP, D), F32),
        compiler_params=_cp(("arbitrary",)),
    )(og, op, wout, h0, _dep(dep))


def _mlp_up(n2, w1g, part, prev=None, dep=None):
    tn = 1024
    per = D // tn

    def body(n_ref, w_ref, dep_ref, *rest):
        zr_ref, a_ref, token = rest[-3:]
        z = jnp.maximum(_dot(n_ref[...], w_ref[0]), 0.0)
        zr_ref[...] = z.astype(BF16)
        a_ref[...] = (z * z).astype(BF16)
        token[...] = jnp.zeros_like(token)

    col = pl.BlockSpec((TP, tn), lambda k, j: (0, (2 * part + k) * per + j))
    return pl.pallas_call(
        body, name="mlp_up_%d" % part, grid=(N_CHIP // 2, per),
        in_specs=[VMEM_FULL, pl.BlockSpec((1, D, tn), lambda k, j: (2 * part + k, 0, j)), ANY] + ([ANY, ANY] if prev else []),
        out_specs=[col, col, pl.BlockSpec((8, 128), lambda k, j: (0, 0))],
        out_shape=[_sds((TP, DFF), BF16), _sds((TP, DFF), BF16), _sds((8, 128), F32)],
        input_output_aliases={3: 0, 4: 1} if prev else {},
        compiler_params=_cp(("arbitrary", "arbitrary")),
    )(n2, w1g, _dep(dep), *(prev or ()))


def _mlp_down(a, w2, h1):
    tk = 1024
    nk = DFF // tk

    def body(a_ref, w_ref, h_ref, o_ref, acc_ref):
        k = pl.program_id(0)

        @pl.when(k == 0)
        def _():
            pltpu.sync_copy(h_ref, acc_ref)

        acc_ref[...] += _dot(a_ref[...], w_ref[...])

        @pl.when(k == nk - 1)
        def _():
            pltpu.sync_copy(acc_ref, o_ref)

    return pl.pallas_call(
        body, name="mlp_down", grid=(nk,),
        in_specs=[pl.BlockSpec((TP, tk), lambda k: (0, k)), pl.BlockSpec((tk, D), lambda k: (k, 0)), ANY],
        out_specs=ANY,
        out_shape=_sds((TP, D), F32),
        scratch_shapes=[pltpu.VMEM((TP, D), F32)],
        compiler_params=_cp(("arbitrary",)),
    )(a, w2, h1)


def _mlp_dz(dh2b, w2, zr, dep=None):
    tn = 1024

    def body(d_ref, w_ref, z_ref, dep_ref, o_ref):
        da = _dot_nt(d_ref[...], w_ref[...])
        o_ref[...] = (da * (2.0 * z_ref[...].astype(F32))).astype(BF16)

    col = pl.BlockSpec((TP, tn), lambda j: (0, j))
    return pl.pallas_call(
        body, name="mlp_dz", grid=(DFF // tn,),
        in_specs=[VMEM_FULL, pl.BlockSpec((tn, D), lambda j: (j, 0)), col, ANY],
        out_specs=col,
        out_shape=_sds((TP, DFF), BF16),
        compiler_params=_cp(("arbitrary",)),
    )(dh2b, w2, zr, _dep(dep))


def _grad_w2(a, dh2b):
    tm = 1024

    def body(a_ref, d_ref, o_ref):
        o_ref[...] = _dot_tn(a_ref[...], d_ref[...])

    return pl.pallas_call(
        body, name="grad_w2", grid=(DFF // tm,),
        in_specs=[pl.BlockSpec((TP, tm), lambda j: (0, j)), VMEM_FULL],
        out_specs=pl.BlockSpec((tm, D), lambda j: (j, 0)),
        out_shape=_sds((DFF, D), F32),
        compiler_params=_cp(("arbitrary",)),
    )(a, dh2b)


def _dep(token):
    return jnp.zeros((8, 128), F32) if token is None else token


def _grad_w1(n2, dz, dep=None):
    tn = 1024
    per = D // tn

    def body(n_ref, d_ref, dep_ref, o_ref):
        o_ref[0] = _dot_tn(n_ref[...], d_ref[...])

    return pl.pallas_call(
        body, name="grad_w1", grid=(N_CHIP, per),
        in_specs=[VMEM_FULL, pl.BlockSpec((TP, tn), lambda k, j: (0, k * per + j)), ANY],
        out_specs=pl.BlockSpec((1, D, tn), lambda k, j: (k, 0, j)),
        out_shape=_sds((N_CHIP, D, D), F32),
        compiler_params=_cp(("arbitrary", "arbitrary")),
    )(n2, dz, _dep(dep))


def _mlp_dn(dz, w1g, dep=None):
    tk = 1024
    per = D // tk
    nk = DFF // tk

    def body(d_ref, w_ref, dep_ref, o_ref, acc_ref):
        k = pl.program_id(0)
        part = _dot_nt(d_ref[...], w_ref[0])

        @pl.when(k == 0)
        def _():
            acc_ref[...] = part

        @pl.when(k > 0)
        def _():
            acc_ref[...] += part

        @pl.when(k == nk - 1)
        def _():
            pltpu.sync_copy(acc_ref, o_ref)

    return pl.pallas_call(
        body, name="mlp_dn", grid=(nk,),
        in_specs=[pl.BlockSpec((TP, tk), lambda k: (0, k)),
                  pl.BlockSpec((1, D, tk), lambda k: (k // per, 0, k % per)), ANY],
        out_specs=ANY,
        out_shape=_sds((TP, D), F32),
        scratch_shapes=[pltpu.VMEM((TP, D), F32)],
        compiler_params=_cp(("arbitrary",)),
    )(dz, w1g, _dep(dep))


def _mixed_grad(dh1b, wout):
    tn = 512

    def body(d_ref, w_ref, o_ref):
        o_ref[...] = _dot_nt(d_ref[...], w_ref[...])

    return pl.pallas_call(
        body, name="mixed_grad", grid=(D // tn,),
        in_specs=[VMEM_FULL, pl.BlockSpec((tn, D), lambda j: (j, 0))],
        out_specs=pl.BlockSpec((TP, tn), lambda j: (0, j)),
        out_shape=_sds((TP, D), F32),
        compiler_params=_cp(("arbitrary",)),
    )(dh1b, wout)


def _grad_wout(og, op, dh1b):
    tm = 512

    def body(og_ref, op_ref, d_ref, o_ref):
        j = pl.program_id(0)

        @pl.when(j < 2)
        def _():
            o_ref[0] = _dot_tn(og_ref[...], d_ref[...])

        @pl.when(j >= 2)
        def _():
            o_ref[0] = _dot_tn(op_ref[...], d_ref[...])

    return pl.pallas_call(
        body, name="grad_wout", grid=(N_CHIP,),
        in_specs=[pl.BlockSpec((TP, tm), lambda j: (0, jnp.minimum(j, 1))),
                  pl.BlockSpec((TP, tm), lambda j: (0, jnp.maximum(j - 2, 0))), VMEM_FULL],
        out_specs=pl.BlockSpec((1, tm, D), lambda j: (j, 0, 0)),
        out_shape=_sds((N_CHIP, tm, D), F32),
        compiler_params=_cp(("arbitrary",)),
    )(og, op, dh1b)


def _in_grad(dq, dk, dv, dr, dglr, dpu, wg, dep=None):
    def body(dq_ref, dk_ref, dv_ref, dr_ref, dg_ref, dpu_ref, w_ref, dep_ref, o_ref):
        dv, dr, dg = dv_ref[...], dr_ref[...], dg_ref[...]
        head, tail = slice(0, GW), slice(GW, PAD_IN)
        o_ref[...] = (_dot_nt(dq_ref[...], w_ref[0, :, 0:KW]) + _dot_nt(dk_ref[...], w_ref[0, :, KW:GW])
                      + _dot_nt(dv[:, 0:128], w_ref[0, :, tail])
                      + _dot_nt(dv, w_ref[1, :, head]) + _dot_nt(dr[:, 0:128], w_ref[1, :, tail])
                      + _dot_nt(dr, w_ref[2, :, head]) + _dot_nt(dg, w_ref[2, :, tail])
                      + _dot_nt(dpu_ref[...], w_ref[3, :, head]) + _dot_nt(dg, w_ref[3, :, tail]))

    tn = 512
    return pl.pallas_call(
        body, name="in_grad", grid=(D // tn,),
        in_specs=[VMEM_FULL] * 6 + [pl.BlockSpec((N_CHIP, tn, PAD_IN), lambda j: (0, j, 0)), ANY],
        out_specs=pl.BlockSpec((TP, tn), lambda j: (0, j)),
        out_shape=_sds((TP, D), F32),
        compiler_params=_cp(("arbitrary",)),
    )(dq, dk, dv, dr, dglr, dpu, wg, _dep(dep))


def _grad_win(u, dq, dk, dv, dr, dglr, dpu, dep=None):
    tm = 512

    def body(u_ref, dq_hbm, dk_hbm, dv_hbm, dr_hbm, dg_hbm, dpu_hbm, dep_ref, o_ref, dp_ref, sem):
        k, m = pl.program_id(0), pl.program_id(1)
        head, tail = slice(0, GW), slice(GW, PAD_IN)
        pieces = [[(dq_hbm, slice(0, KW)), (dk_hbm, slice(KW, GW)), (dv_hbm.at[:, 0:128], tail)],
                  [(dv_hbm, head), (dr_hbm.at[:, 0:128], tail)],
                  [(dr_hbm, head), (dg_hbm, tail)],
                  [(dpu_hbm, head), (dg_hbm, tail)]]

        def copies(kk):
            return [pltpu.make_async_copy(src, dp_ref.at[kk % 2, :, cols], sem.at[kk % 2, i])
                    for i, (src, cols) in enumerate(pieces[kk])]

        @pl.when((k == 0) & (m == 0))
        def _():
            for cp in copies(0):
                cp.start()

        for kk in range(N_CHIP):
            @pl.when((k == kk) & (m == 0))
            def _(kk=kk):
                for cp in copies(kk):
                    cp.wait()
                if kk + 1 < N_CHIP:
                    for cp in copies(kk + 1):
                        cp.start()

        g = _dot_tn(u_ref[...], dp_ref[k % 2])
        lane = lax.broadcasted_iota(jnp.int32, (tm, PAD_IN), 1)
        for kk in range(N_CHIP):
            @pl.when(k == kk)
            def _(kk=kk):
                if kk == 0:
                    nat = g
                elif kk < 3:
                    nat = pltpu.roll(g, PAD_IN - 4 * kk, 1)
                else:
                    nat = jnp.where(lane < 4, pltpu.roll(g, PAD_IN - (GW + 12), 1), pltpu.roll(g, 4, 1))
                o_ref[0] = nat[:, 0:SHARD_IN]

    return pl.pallas_call(
        body, name="grad_win", grid=(N_CHIP, D // tm),
        in_specs=[pl.BlockSpec((TP, tm), lambda k, m: (0, m))] + [ANY] * 7,
        out_specs=pl.BlockSpec((1, tm, SHARD_IN), lambda k, m: (k, m, 0)),
        out_shape=_sds((N_CHIP, D, SHARD_IN), F32),
        scratch_shapes=[pltpu.VMEM((2, TP, PAD_IN), BF16), pltpu.SemaphoreType.DMA((2, 3))],
        compiler_params=_cp(("arbitrary", "arbitrary")),
    )(u, dq, dk, dv, dr, dglr, dpu, _dep(dep))


def _split3(x):
    hi = x.astype(BF16)
    r1 = x - hi.astype(F32)
    mid = r1.astype(BF16)
    lo = (r1 - mid.astype(F32)).astype(BF16)
    return hi, mid, lo


def _tri_sum(tri, x):
    hi, mid, lo = _split3(x)
    return _dot(tri, hi) + _dot(tri, mid) + _dot(tri, lo)


def _gla_common(n, glr, gw2, gb):
    rows = n * CH + lax.broadcasted_iota(jnp.int32, (CH, 1), 0)
    valid = (rows >= ROW_LO) & (rows < ROW_HI)
    g_raw = _dot(glr.astype(BF16), gw2.astype(BF16)) + gb
    logsig = jnp.minimum(g_raw, 0.0) - jnp.log(1.0 + jnp.exp(-jnp.abs(g_raw)))
    logg = jnp.where(valid, logsig * (1.0 / TAU), 0.0)
    ci = lax.broadcasted_iota(jnp.int32, (CH, CH), 0)
    si = lax.broadcasted_iota(jnp.int32, (CH, CH), 1)
    lower = ci >= si
    G = _tri_sum(lower.astype(BF16), logg)
    Gl = G[CH - 1:CH, :]
    return valid, g_raw, lower, G, Gl


def _p_specs(index):
    def spec(width, shard, col):
        return pl.BlockSpec((1, CH, width), lambda s: (shard, index(s), col))

    return [spec(KW, 0, 0), spec(KW, 0, 1), spec(GW, 1, 0), spec(128, 0, 8), spec(GW, 2, 0), spec(128, 1, 8),
            spec(128, 2, 8), spec(128, 3, 8)]


def _p_load(q_ref, k_ref, vm_ref, vh_ref, rm_ref, rh_ref, ga_ref, gb_ref):
    def joined(main, head):
        return jnp.concatenate([main[:, 0:128] + head, main[:, 128:]], axis=1)

    return q_ref[0], k_ref[0], joined(vm_ref[0], vh_ref[0]), joined(rm_ref[0], rh_ref[0]), ga_ref[0] + gb_ref[0]


def _gla_fwd(P, gw2, gb, gnw, dep=None):
    scale = DK ** -0.5

    def body(p0, p1, p2, p3, p4, p5, p6, p7, gw2_ref, gb_ref, gnw_ref, dep_ref, o_ref, og_ref, sp_ref, st_ref):
        n = pl.program_id(0)

        @pl.when(n == 0)
        def _():
            st_ref[...] = jnp.zeros_like(st_ref)

        q_all, k_all, v_all, r_all, glr = _p_load(p0, p1, p2, p3, p4, p5, p6, p7)
        _, _, lower, G, Gl = _gla_common(n, glr, gw2_ref[...], gb_ref[...])
        eG = jnp.exp(G)
        eN = jnp.exp(-G)
        eE = jnp.exp(Gl - G)
        dec = jnp.exp(Gl)
        gnw_v = gnw_ref[...]
        for h in range(HEADS):
            ks = slice(h * DK, (h + 1) * DK)
            vs = slice(h * DV, (h + 1) * DV)
            kh = k_all[:, ks]
            vh = v_all[:, vs].astype(BF16)
            qd = ((q_all[:, ks] * scale) * eG[:, ks]).astype(BF16)
            ki = (kh * eN[:, ks]).astype(BF16)
            ke = (kh * eE[:, ks]).astype(BF16)
            st = st_ref[h]
            a = jnp.where(lower, _dot_nt(qd, ki), 0.0).astype(BF16)
            o = _dot(a, vh) + _dot_nt(qd, st.astype(BF16))
            sp_ref[0, h] = st
            st_ref[h] = st * dec[:, ks] + _dot_tn(vh, ke)
            o_ref[:, vs] = o
            rs = lax.rsqrt(jnp.mean(o * o, axis=-1, keepdims=True) + EPS)
            rv = r_all[:, vs]
            gate = rv / (1.0 + jnp.exp(-rv))
            og_ref[:, vs] = (((o * rs) * gnw_v) * gate).astype(BF16)

    rv_ = pl.BlockSpec((CH, GW), lambda n: (n, 0))

    def full(shape):
        return pl.BlockSpec(shape, lambda n: tuple(0 for _ in shape))

    return pl.pallas_call(
        body, name="gla_fwd", grid=(NCH,),
        in_specs=_p_specs(lambda n: n) + [full((128, KW)), full((1, KW)), full((1, DV)), ANY],
        out_specs=[rv_, rv_, pl.BlockSpec((1, HEADS, DV, DK), lambda n: (n, 0, 0, 0))],
        out_shape=[_sds((TP, GW), F32), _sds((TP, GW), BF16), _sds((NCH, HEADS, DV, DK), F32)],
        scratch_shapes=[pltpu.VMEM((HEADS, DV, DK), F32)],
        compiler_params=_cp(("arbitrary",)),
    )(*([P] * 8), gw2, gb, gnw, _dep(dep))


def _gla_bwd(dog, o, P, gw2, gb, gnw, sp, dep=None):
    scale = DK ** -0.5

    def body(dog_ref, o_ref, p0, p1, p2, p3, p4, p5, p6, p7, gw2_ref, gb_ref, gnw_ref, sp_ref, dep_ref,
             dq_ref, dk_ref, dv_ref, dr_ref, dglr_ref, dgw2_ref, dgb_ref, dgnw_ref, ds_ref):
        step = pl.program_id(0)
        n = NCH - 1 - step

        @pl.when(step == 0)
        def _():
            ds_ref[...] = jnp.zeros_like(ds_ref)
            dgw2_ref[...] = jnp.zeros_like(dgw2_ref)
            dgb_ref[...] = jnp.zeros_like(dgb_ref)
            dgnw_ref[...] = jnp.zeros_like(dgnw_ref)

        q_all, k_all, v_all, r_all, glr_v = _p_load(p0, p1, p2, p3, p4, p5, p6, p7)
        gw2_b = gw2_ref[...].astype(BF16)
        valid, g_raw, lower, G, Gl = _gla_common(n, glr_v, gw2_ref[...], gb_ref[...])
        upper = lax.broadcasted_iota(jnp.int32, (CH, CH), 0) <= lax.broadcasted_iota(jnp.int32, (CH, CH), 1)
        eG = jnp.exp(G)
        eN = jnp.exp(-G)
        eE = jnp.exp(Gl - G)
        dec = jnp.exp(Gl)
        gnw_v = gnw_ref[...]
        last = lax.broadcasted_iota(jnp.int32, (CH, 1), 0) == CH - 1
        dgnw_acc = jnp.zeros((1, DV), F32)
        dG_parts = []
        for h in range(HEADS):
            ks = slice(h * DK, (h + 1) * DK)
            vs = slice(h * DV, (h + 1) * DV)
            oh = o_ref[:, vs]
            rv = r_all[:, vs]
            dg = dog_ref[:, vs]
            sig = 1.0 / (1.0 + jnp.exp(-rv))
            gate = rv * sig
            rs = lax.rsqrt(jnp.mean(oh * oh, axis=-1, keepdims=True) + EPS)
            ohat = oh * rs
            dr_ref[:, vs] = ((dg * (ohat * gnw_v)) * (sig * (1.0 + rv * (1.0 - sig)))).astype(BF16)
            don = dg * gate
            dgnw_acc = dgnw_acc + jnp.sum(don * ohat, axis=0, keepdims=True)
            gxn = don * gnw_v
            do = (rs * (gxn - ohat * jnp.mean(gxn * ohat, axis=-1, keepdims=True))).astype(BF16)
            kh = k_all[:, ks]
            vh = v_all[:, vs].astype(BF16)
            qd_f = (q_all[:, ks] * scale) * eG[:, ks]
            ki_f = kh * eN[:, ks]
            ke_f = kh * eE[:, ks]
            qd, ki, ke = qd_f.astype(BF16), ki_f.astype(BF16), ke_f.astype(BF16)
            spt = sp_ref[0, h]
            dst = ds_ref[h]
            dst_b = dst.astype(BF16)
            a_t = jnp.where(upper, _dot_nt(ki, qd), 0.0).astype(BF16)
            da = jnp.where(lower, _dot_nt(do, vh), 0.0).astype(BF16)
            da_t = jnp.where(upper, _dot_nt(vh, do), 0.0).astype(BF16)
            dv_ref[:, vs] = (_dot(a_t, do) + _dot_nt(ke, dst_b)).astype(BF16)
            dqd = _dot(da, ki) + _dot(do, spt.astype(BF16))
            dki = _dot(da_t, qd)
            dke = _dot(vh, dst_b)
            ddec = jnp.sum(spt * dst, axis=0, keepdims=True)
            ds_ref[h] = dst * dec[:, ks] + _dot_tn(do, qd)
            dq_ref[:, ks] = ((dqd * eG[:, ks]) * scale).astype(BF16)
            dk_ref[:, ks] = (dki * eN[:, ks] + dke * eE[:, ks]).astype(BF16)
            dke_ke = dke * ke_f
            dG = dqd * qd_f - dki * ki_f - dke_ke
            dGl = jnp.sum(dke_ke, axis=0, keepdims=True) + ddec * dec[:, ks]
            dG_parts.append(dG + jnp.where(last, dGl, 0.0))
        dgnw_ref[...] += dgnw_acc
        dG_all = jnp.concatenate(dG_parts, axis=1)
        dlogg = jnp.where(valid, _tri_sum(upper.astype(BF16), dG_all), 0.0)
        dg_raw = (dlogg * (1.0 / TAU)) * (1.0 / (1.0 + jnp.exp(g_raw)))
        dgb_ref[...] += jnp.sum(dg_raw, axis=0, keepdims=True)
        dg_b = dg_raw.astype(BF16)
        dgw2_ref[...] += _dot_tn(glr_v.astype(BF16), dg_b)
        dglr_ref[...] = _dot_nt(dg_b, gw2_b).astype(BF16)

    def back(s):
        return NCH - 1 - s

    rk = pl.BlockSpec((CH, KW), lambda s: (back(s), 0))
    rv_ = pl.BlockSpec((CH, GW), lambda s: (back(s), 0))
    rg = pl.BlockSpec((CH, 128), lambda s: (back(s), 0))

    def full(shape):
        return pl.BlockSpec(shape, lambda s: tuple(0 for _ in shape))

    return pl.pallas_call(
        body, name="gla_bwd", grid=(NCH,),
        in_specs=[rv_, rv_] + _p_specs(back) + [full((128, KW)), full((1, KW)), full((1, DV)),
                  pl.BlockSpec((1, HEADS, DV, DK), lambda s: (back(s), 0, 0, 0)), ANY],
        out_specs=[rk, rk, rv_, rv_, rg, full((128, KW)), full((1, KW)), full((1, DV))],
        out_shape=[_sds((TP, KW), BF16), _sds((TP, KW), BF16), _sds((TP, GW), BF16), _sds((TP, GW), BF16),
                   _sds((TP, 128), BF16), _sds((128, KW), F32), _sds((1, KW), F32), _sds((1, DV), F32)],
        scratch_shapes=[pltpu.VMEM((HEADS, DV, DK), F32)],
        compiler_params=_cp(("arbitrary",)),
    )(dog, o, *([P] * 8), gw2, gb, gnw, sp, _dep(dep))


POOL_TR = 128
HALO = 16


def _pool_counts(base, nrows):
    rows = base + lax.broadcasted_iota(jnp.int32, (nrows, 1), 0)
    valid = (rows >= ROW_LO) & (rows < ROW_HI)
    t1 = (rows - ROW_LO + 1).astype(F32)
    cnts = [jnp.clip(t1, 1.0, float(w)) for w in WINDOWS]
    return valid, cnts


def _pool_fwd(P, pw, ps, dep=None):
    def body(cur_ref, prev_ref, pw_ref, ps_ref, dep_ref, y_ref, op_ref):
        i = pl.program_id(0)
        cur = cur_ref[0]
        full = jnp.concatenate([prev_ref[0], cur], axis=0)
        s2 = full + pltpu.roll(full, 1, 0)
        s4 = s2 + pltpu.roll(s2, 2, 0)
        s8 = s4 + pltpu.roll(s4, 4, 0)
        s16 = s8 + pltpu.roll(s8, 8, 0)
        valid, cnts = _pool_counts(i * POOL_TR, POOL_TR)
        for g, s in enumerate((s2, s4, s8, s16)):
            cs = slice(g * GC, (g + 1) * GC)
            y = s[HALO:, cs] / cnts[g] - cur[:, cs]
            yb = jnp.where(valid, y, 0.0).astype(BF16)
            y_ref[:, cs] = yb
            op_ref[:, cs] = (_dot(yb, pw_ref[g].astype(BF16)) * ps_ref[:, cs]).astype(BF16)

    row = pl.BlockSpec((POOL_TR, PW), lambda i: (i, 0))
    per = POOL_TR // HALO
    return pl.pallas_call(
        body, name="pool_fwd", grid=(TP // POOL_TR,),
        in_specs=[pl.BlockSpec((1, POOL_TR, PW), lambda i: (3, i, 0)),
                  pl.BlockSpec((1, HALO, PW), lambda i: (3, jnp.maximum(i * per - 1, 0), 0)),
                  pl.BlockSpec((4, GC, GC), lambda i: (0, 0, 0)), pl.BlockSpec((1, PW), lambda i: (0, 0)), ANY],
        out_specs=[row, row],
        out_shape=[_sds((TP, PW), BF16), _sds((TP, PW), BF16)],
        compiler_params=_cp(("arbitrary",)),
    )(P, P, pw, ps, _dep(dep))


def _pool_bwd(dop, y, pw, ps, dep=None):
    nblk = TP // HALO

    def body(cur_ref, nxt_ref, y_ref, pw_ref, ps_ref, dep_ref, dpu_ref, dpw_ref, dps_ref):
        i = pl.program_id(0)

        @pl.when(i == 0)
        def _():
            dpw_ref[...] = jnp.zeros_like(dpw_ref)
            dps_ref[...] = jnp.zeros_like(dps_ref)

        n_all = POOL_TR + HALO
        dcur = cur_ref[...]
        dall = jnp.concatenate([dcur, nxt_ref[...]], axis=0)
        valid, cnts = _pool_counts(i * POOL_TR, n_all)
        for g in range(4):
            cs = slice(g * GC, (g + 1) * GC)
            pwb = pw_ref[g].astype(BF16)
            yb = y_ref[:, cs]
            dyw = (dall[:, cs] * ps_ref[:, cs]).astype(BF16)
            dps_ref[:, cs] += jnp.sum(dcur[:, cs] * _dot(yb, pwb), axis=0, keepdims=True)
            dpw_ref[g] += _dot_tn(yb, dyw[0:POOL_TR, :])
            dyv = jnp.where(valid, _dot_nt(dyw, pwb), 0.0)
            e = dyv / cnts[g]
            w = WINDOWS[g]
            sh = 1
            while sh < w:
                e = e + pltpu.roll(e, n_all - sh, 0)
                sh *= 2
            dpu_ref[:, cs] = (e[0:POOL_TR, :] - dyv[0:POOL_TR, :]).astype(BF16)

    row = pl.BlockSpec((POOL_TR, PW), lambda i: (i, 0))
    per = POOL_TR // HALO
    return pl.pallas_call(
        body, name="pool_bwd", grid=(TP // POOL_TR,),
        in_specs=[pl.BlockSpec((POOL_TR, PW), lambda i: (i, 1)),
                  pl.BlockSpec((HALO, PW), lambda i: (jnp.minimum(i * per + per, nblk - 1), 1)),
                  row, pl.BlockSpec((4, GC, GC), lambda i: (0, 0, 0)), pl.BlockSpec((1, PW), lambda i: (0, 0)), ANY],
        out_specs=[row, pl.BlockSpec((4, GC, GC), lambda i: (0, 0, 0)), pl.BlockSpec((1, PW), lambda i: (0, 0))],
        out_shape=[_sds((TP, PW), BF16), _sds((4, GC, GC), F32), _sds((1, PW), F32)],
        compiler_params=_cp(("arbitrary",)),
    )(dop, dop, y, pw, ps, _dep(dep))


def _place():
    x, y, c = lax.axis_index("x"), lax.axis_index("y"), lax.axis_index("c")
    chips = [(1 - x, y), (x, 1 - y), (1 - x, 1 - y)]
    return x, y, c, chips


HBM = pl.BlockSpec(memory_space=pltpu.HBM)
SEM = pl.BlockSpec(memory_space=pltpu.SEMAPHORE)
EFFECT = pltpu.SideEffectType.DATAFLOW_SIDE_EFFECTING


def _cast_into(w, place, cols_out, name, dep=None):
    rows, cols = w.shape
    tr = 256

    def body(p_ref, w_ref, dep_ref, o_ref):
        if cols_out != cols:
            o_ref[0] = jnp.zeros((tr, cols_out), BF16)
            o_ref[0, :, 0:cols] = w_ref[...].astype(BF16)
        else:
            o_ref[0] = w_ref[...].astype(BF16)

    grid_spec = pltpu.PrefetchScalarGridSpec(
        num_scalar_prefetch=1, grid=(rows // tr,),
        in_specs=[pl.BlockSpec((tr, cols), lambda i, p: (i, 0)), ANY],
        out_specs=pl.BlockSpec((1, tr, cols_out), lambda i, p: (p[0], i, 0)))
    return pl.pallas_call(
        body, name=name, grid_spec=grid_spec,
        out_shape=_sds((N_CHIP, rows, cols_out), BF16),
        compiler_params=_cp(("arbitrary",)),
    )(place, w, _dep(dep))


def _cast_win(w, place, dep=None):
    rows, cols = w.shape
    tr = 256

    def body(p_ref, w_ref, dep_ref, o_ref, t_ref):
        t_ref[...] = jnp.zeros_like(t_ref)
        t_ref[:, 0:cols] = w_ref[...]
        t = t_ref[...]
        lane = lax.broadcasted_iota(jnp.int32, (tr, PAD_IN), 1)
        for kk in range(N_CHIP):
            @pl.when(p_ref[0] == kk)
            def _(kk=kk):
                if kk == 0:
                    placed = t
                elif kk < 3:
                    placed = pltpu.roll(t, 4 * kk, 1)
                else:
                    pool = pltpu.roll(t, PAD_IN - 4, 1)
                    gate = pltpu.roll(t, GW + 12, 1)
                    placed = jnp.where(lane < GW, pool, jnp.where((lane >= GW + 12) & (lane < GW + 16), gate, 0.0))
                o_ref[0] = placed.astype(BF16)

    grid_spec = pltpu.PrefetchScalarGridSpec(
        num_scalar_prefetch=1, grid=(rows // tr,),
        in_specs=[pl.BlockSpec((tr, cols), lambda i, p: (i, 0)), ANY],
        out_specs=pl.BlockSpec((1, tr, PAD_IN), lambda i, p: (p[0], i, 0)),
        scratch_shapes=[pltpu.VMEM((tr, PAD_IN), F32)])
    return pl.pallas_call(
        body, name="cast_win", grid_spec=grid_spec,
        out_shape=_sds((N_CHIP, rows, PAD_IN), BF16),
        compiler_params=_cp(("arbitrary",)),
    )(place, w, _dep(dep))


def _half_rows(ref, k, which):
    h = ref.shape[1] // 2
    return ref.at[k, pl.ds(pl.multiple_of(which * h, 8), h), :]


def _sent_rows(ref, k, which, whole):
    return ref.at[k] if whole else _half_rows(ref, k, which)


def _gather_start(ws, name, whole=None):
    n = len(ws)
    whole = whole or [False] * n

    def body(*refs):
        ins = refs[:n]
        ssems = refs[n:2 * n]
        rsems = refs[2 * n:3 * n]
        token = refs[4 * n]
        x, y, c, chips = _place()
        me = 2 * x + y
        for w in range(n):
            blk = _sent_rows(ins[w], me, c, whole[w])
            for j, chip in enumerate(chips):
                pltpu.make_async_remote_copy(src_ref=blk, dst_ref=blk, send_sem=ssems[w].at[j], recv_sem=rsems[w].at[j],
                                             device_id=(*chip, c), device_id_type=MESH).start()
        token[...] = jnp.zeros_like(token)

    sem3 = pltpu.SemaphoreType.DMA((3,))
    outs = pl.pallas_call(
        body, name=name,
        out_shape=tuple([sem3] * (2 * n) + [pltpu.HBM(w.shape, w.dtype) for w in ws] + [_sds((8, 128), F32)]),
        in_specs=(HBM,) * n, out_specs=(SEM,) * (2 * n) + (HBM,) * n + (VMEM_FULL,),
        input_output_aliases={w: 2 * n + w for w in range(n)},
        compiler_params=pltpu.CompilerParams(has_side_effects=EFFECT),
    )(*[pltpu.with_memory_space_constraint(w, pltpu.HBM) for w in ws])
    return outs[:n], outs[n:2 * n], outs[2 * n:3 * n], outs[3 * n]


def _gather_wait(w, ssem, rsem, after, name, whole=False):
    def body(w_ref, ssem_ref, rsem_ref, after_ref, out_ref):
        x, y, c, chips = _place()
        me = 2 * x + y
        mine = _sent_rows(w_ref, me, c, whole)
        for j, (cx, cy) in enumerate(chips):
            cp = pltpu.make_async_remote_copy(src_ref=mine, dst_ref=_sent_rows(w_ref, 2 * cx + cy, c, whole),
                                              send_sem=ssem_ref.at[j], recv_sem=rsem_ref.at[j],
                                              device_id=(cx, cy, c), device_id_type=MESH)
            cp.wait_send()
            cp.wait_recv()

    return pl.pallas_call(
        body, name=name, out_shape=pltpu.HBM(w.shape, w.dtype),
        in_specs=(HBM, SEM, SEM, ANY), out_specs=HBM, input_output_aliases={0: 0},
        compiler_params=pltpu.CompilerParams(has_side_effects=EFFECT),
    )(w, ssem, rsem, after)


def _gather_copies(ref, kind, ssem, rsem):
    x, y, c, _ = _place()
    xn, yn, sib = (1 - x, y, c), (x, 1 - y, c), (x, y, 1 - c)
    kx, ky, kd = 2 * (1 - x) + y, 2 * x + (1 - y), 2 * (1 - x) + (1 - y)
    half = ref.shape[1] // 2
    quarter = half // 2

    def piece(k, q):
        return ref.at[k, pl.ds(pl.multiple_of(c * half + q * quarter, 8), quarter), :]

    if kind == "d":
        blk = _half_rows(ref, 2 * x + y, c)
        pairs = [(blk, xn), (blk, yn)]
    elif kind == "r":
        pairs = [(piece(ky, 1), xn), (piece(kx, 0), yn)]
    elif kind == "fx":
        pairs = [(_half_rows(ref, kx, c), sib), (_half_rows(ref, ky, c), sib)]
    else:
        pairs = [(_half_rows(ref, kd, c), sib)]
    return [pltpu.make_async_remote_copy(src_ref=blk, dst_ref=blk, send_sem=ssem.at[i], recv_sem=rsem.at[i],
                                         device_id=to, device_id_type=MESH) for i, (blk, to) in enumerate(pairs)]


def _gather_step(name, arrs, waits, starts, sems_in=(), after=()):
    n, nw, ns = len(arrs), len(waits), len(starts)
    after = [a for a in after if a is not None] or [_dep(None)]

    def body(*refs):
        a_in = refs[:n]
        s_in = refs[n:n + 2 * nw]
        outs = refs[n + 2 * nw + len(after):]
        s_out = outs[:2 * ns]
        for i, (ai, kind) in enumerate(waits):
            for cp in _gather_copies(a_in[ai], kind, s_in[2 * i], s_in[2 * i + 1]):
                cp.wait_send()
                cp.wait_recv()
        for i, (ai, kind) in enumerate(starts):
            for cp in _gather_copies(a_in[ai], kind, s_out[2 * i], s_out[2 * i + 1]):
                cp.start()
        if ns:
            token = outs[2 * ns + n]
            token[...] = jnp.zeros_like(token)

    sem2 = pltpu.SemaphoreType.DMA((2,))
    flat_in = [s for pair in sems_in for s in pair]
    arrs = [pltpu.with_memory_space_constraint(a, pltpu.HBM) for a in arrs]
    outs = pl.pallas_call(
        body, name=name,
        out_shape=tuple([sem2] * (2 * ns) + [pltpu.HBM(a.shape, a.dtype) for a in arrs]
                        + ([_sds((8, 128), F32)] if ns else [])),
        in_specs=(HBM,) * n + (SEM,) * (2 * nw) + (ANY,) * len(after),
        out_specs=(SEM,) * (2 * ns) + (HBM,) * n + ((VMEM_FULL,) if ns else ()),
        input_output_aliases={i: 2 * ns + i for i in range(n)},
        compiler_params=pltpu.CompilerParams(has_side_effects=EFFECT),
    )(*arrs, *flat_in, *after)
    sems = [(outs[2 * i], outs[2 * i + 1]) for i in range(ns)]
    return sems, list(outs[2 * ns:2 * ns + n]), (outs[2 * ns + n] if ns else None)


def _rs_start(sb, name, after=None):
    _, half, cols = sb.shape

    def body(sb_ref, land_ref, after_ref, ssem, rsem, sb_out, land_out, token):
        x, y, c, chips = _place()
        for j, (cx, cy) in enumerate(chips):
            pltpu.make_async_remote_copy(src_ref=sb_ref.at[2 * cx + cy], dst_ref=land_ref.at[j], send_sem=ssem.at[j],
                                         recv_sem=rsem.at[j], device_id=(cx, cy, c), device_id_type=MESH).start()
        token[...] = jnp.zeros_like(token)

    sem3 = pltpu.SemaphoreType.DMA((3,))
    land = lax.empty((3, half, cols), BF16)
    return pl.pallas_call(
        body, name=name,
        out_shape=(sem3, sem3, pltpu.HBM(sb.shape, sb.dtype), pltpu.HBM(land.shape, land.dtype), _sds((8, 128), F32)),
        in_specs=(HBM, HBM, ANY), out_specs=(SEM, SEM, HBM, HBM, VMEM_FULL), input_output_aliases={0: 2, 1: 3},
        compiler_params=pltpu.CompilerParams(has_side_effects=EFFECT),
    )(pltpu.with_memory_space_constraint(sb, pltpu.HBM), pltpu.with_memory_space_constraint(land, pltpu.HBM), _dep(after))


def _rs_wait(items, after, name):
    n = len(items)

    def body(*refs):
        x, y, c, chips = _place()
        for i in range(n):
            sb_ref, land_ref, ssem_ref, rsem_ref = refs[4 * i:4 * i + 4]
            for j, (cx, cy) in enumerate(chips):
                cp = pltpu.make_async_remote_copy(src_ref=sb_ref.at[2 * cx + cy], dst_ref=land_ref.at[j],
                                                  send_sem=ssem_ref.at[j], recv_sem=rsem_ref.at[j],
                                                  device_id=(cx, cy, c), device_id_type=MESH)
                cp.wait_send()
                cp.wait_recv()

    outs = pl.pallas_call(
        body, name=name,
        out_shape=tuple(pltpu.HBM(a.shape, a.dtype) for it in items for a in it[:2]),
        in_specs=(HBM, HBM, SEM, SEM) * n + (ANY,), out_specs=(HBM,) * (2 * n),
        input_output_aliases={4 * i + k: 2 * i + k for i in range(n) for k in range(2)},
        compiler_params=pltpu.CompilerParams(has_side_effects=EFFECT),
    )(*[a for it in items for a in it], after)
    return [outs[2 * i + 1] for i in range(n)]


def _pair_copy(g_ref, land_ref, ssem, rsem):
    x, y, c, _ = _place()
    h = g_ref.shape[1] // 2
    src = g_ref.at[:, pl.ds(pl.multiple_of((1 - c) * h, 8), h), :]
    return pltpu.make_async_remote_copy(src_ref=src, dst_ref=land_ref, send_sem=ssem.at[0], recv_sem=rsem.at[0],
                                        device_id=(x, y, 1 - c), device_id_type=MESH)


def _pair_start(g, name):
    def body(g_ref, land_ref, ssem, rsem, g_out, land_out, token):
        _pair_copy(g_ref, land_ref, ssem, rsem).start()
        token[...] = jnp.zeros_like(token)

    sem1 = pltpu.SemaphoreType.DMA((1,))
    land = lax.empty((N_CHIP, g.shape[1] // 2, g.shape[2]), F32)
    return pl.pallas_call(
        body, name=name,
        out_shape=(sem1, sem1, pltpu.HBM(g.shape, g.dtype), pltpu.HBM(land.shape, land.dtype), _sds((8, 128), F32)),
        in_specs=(HBM, HBM), out_specs=(SEM, SEM, HBM, HBM, VMEM_FULL), input_output_aliases={0: 2, 1: 3},
        compiler_params=pltpu.CompilerParams(has_side_effects=EFFECT),
    )(pltpu.with_memory_space_constraint(g, pltpu.HBM), pltpu.with_memory_space_constraint(land, pltpu.HBM))


def _pair_wait(g, land, ssem, rsem, after, name):
    def body(g_ref, land_ref, ssem_ref, rsem_ref, after_ref, g_out, land_out):
        cp = _pair_copy(g_ref, land_ref, ssem_ref, rsem_ref)
        cp.wait_send()
        cp.wait_recv()

    return pl.pallas_call(
        body, name=name,
        out_shape=(pltpu.HBM(g.shape, g.dtype), pltpu.HBM(land.shape, land.dtype)),
        in_specs=(HBM, HBM, SEM, SEM, ANY), out_specs=(HBM, HBM), input_output_aliases={0: 0, 1: 1},
        compiler_params=pltpu.CompilerParams(has_side_effects=EFFECT),
    )(g, land, ssem, rsem, after)


def _pair_sum(g, rcv, place, name):
    _, rows, cols = g.shape
    half = rows // 2
    tr = 256
    nt = half // tr

    def body(p_ref, g_ref, r_ref, sb_ref, sf_ref):
        s = pl.program_id(1)
        tot = g_ref[0] + r_ref[0]
        sb_ref[0] = tot.astype(BF16)

        @pl.when(s == p_ref[0])
        def _():
            sf_ref[...] = tot

    grid_spec = pltpu.PrefetchScalarGridSpec(
        num_scalar_prefetch=1, grid=(nt, N_CHIP),
        in_specs=[pl.BlockSpec((1, tr, cols), lambda t, s, p: (s, p[1] * nt + t, 0)),
                  pl.BlockSpec((1, tr, cols), lambda t, s, p: (s, t, 0))],
        out_specs=[pl.BlockSpec((1, tr, cols), lambda t, s, p: (s, t, 0)),
                   pl.BlockSpec((tr, cols), lambda t, s, p: (t, 0))])
    return pl.pallas_call(
        body, name=name, grid_spec=grid_spec,
        out_shape=[_sds((N_CHIP, half, cols), BF16), _sds((half, cols), F32)],
        compiler_params=_cp(("arbitrary", "arbitrary")),
    )(place, g, rcv)


def _final_sum(sf, rb, place, name):
    half, cols = sf.shape
    tr = 256
    nt = half // tr

    def body(p_ref, sf_ref, r_ref, out_ref):
        acc = sf_ref[...]
        for j in range(3):
            acc = acc + r_ref[j].astype(F32)
        out_ref[...] = acc

    grid_spec = pltpu.PrefetchScalarGridSpec(
        num_scalar_prefetch=1, grid=(nt,),
        in_specs=[pl.BlockSpec((tr, cols), lambda t, p: (t, 0)), pl.BlockSpec((3, tr, cols), lambda t, p: (0, t, 0))],
        out_specs=pl.BlockSpec((tr, cols), lambda t, p: (p[1] * nt + t, 0)))
    return pl.pallas_call(
        body, name=name, grid_spec=grid_spec,
        out_shape=_sds((2 * half, cols), F32),
        compiler_params=_cp(("arbitrary",)),
    )(place, sf, rb)


def _half_copy(f_ref, which, ssem, rsem):
    x, y, c, _ = _place()
    h = f_ref.shape[0] // 2
    rows = f_ref.at[pl.ds(pl.multiple_of(which * h, 8), h), :]
    return pltpu.make_async_remote_copy(src_ref=rows, dst_ref=rows, send_sem=ssem.at[0], recv_sem=rsem.at[0],
                                        device_id=(x, y, 1 - c), device_id_type=MESH)


def _half_start(fulls, name, after=None):
    n = len(fulls)

    def body(*refs):
        for i in range(n):
            _half_copy(refs[i], lax.axis_index("c"), refs[n + 1 + 2 * i], refs[n + 2 + 2 * i]).start()
        token = refs[4 * n + 1]
        token[...] = jnp.zeros_like(token)

    sem1 = pltpu.SemaphoreType.DMA((1,))
    outs = pl.pallas_call(
        body, name=name,
        out_shape=tuple([sem1] * (2 * n) + [pltpu.HBM(f.shape, f.dtype) for f in fulls] + [_sds((8, 128), F32)]),
        in_specs=(HBM,) * n + (ANY,), out_specs=(SEM,) * (2 * n) + (HBM,) * n + (VMEM_FULL,),
        input_output_aliases={i: 2 * n + i for i in range(n)},
        compiler_params=pltpu.CompilerParams(has_side_effects=EFFECT),
    )(*[pltpu.with_memory_space_constraint(f, pltpu.HBM) for f in fulls], _dep(after))
    return [(outs[2 * i], outs[2 * i + 1], outs[2 * n + i]) for i in range(n)], outs[3 * n]


def _half_wait(items, after, name):
    n = len(items)

    def body(*refs):
        c = lax.axis_index("c")
        for i in range(n):
            ssem_ref, rsem_ref, f_ref = refs[3 * i:3 * i + 3]
            _half_copy(f_ref, c, ssem_ref, rsem_ref).wait_send()
            _half_copy(f_ref, 1 - c, ssem_ref, rsem_ref).wait_recv()

    return pl.pallas_call(
        body, name=name, out_shape=tuple(pltpu.HBM(it[2].shape, it[2].dtype) for it in items),
        in_specs=(SEM, SEM, HBM) * n + (ANY,) * len(after), out_specs=(HBM,) * n,
        input_output_aliases={3 * i + 2: i for i in range(n)},
        compiler_params=pltpu.CompilerParams(has_side_effects=EFFECT),
    )(*[a for it in items for a in it], *after)


def _small_copies(src_ref, land_ref, ssem, rsem, first):
    x, y, c, chips = _place()
    if first:
        return [pltpu.make_async_remote_copy(src_ref=src_ref, dst_ref=land_ref, send_sem=ssem.at[0], recv_sem=rsem.at[0],
                                             device_id=(x, y, 1 - c), device_id_type=MESH)]
    return [pltpu.make_async_remote_copy(src_ref=src_ref, dst_ref=land_ref.at[j], send_sem=ssem.at[j], recv_sem=rsem.at[j],
                                         device_id=(*chip, c), device_id_type=MESH) for j, chip in enumerate(chips)]


def _small_start(src, first, name, after=None):
    n = 1 if first else 3

    def body(src_ref, land_ref, after_ref, ssem, rsem, src_out, land_out, token):
        for cp in _small_copies(src_ref, land_ref, ssem, rsem, first):
            cp.start()
        token[...] = jnp.zeros_like(token)

    sems = pltpu.SemaphoreType.DMA((n,))
    land = lax.empty(src.shape if first else (3,) + src.shape, F32)
    return pl.pallas_call(
        body, name=name,
        out_shape=(sems, sems, pltpu.HBM(src.shape, F32), pltpu.HBM(land.shape, F32), _sds((8, 128), F32)),
        in_specs=(HBM, HBM, ANY), out_specs=(SEM, SEM, HBM, HBM, VMEM_FULL), input_output_aliases={0: 2, 1: 3},
        compiler_params=pltpu.CompilerParams(has_side_effects=EFFECT),
    )(pltpu.with_memory_space_constraint(src, pltpu.HBM), pltpu.with_memory_space_constraint(land, pltpu.HBM), _dep(after))


def _small_wait(src, land, ssem, rsem, first, after, name):
    def body(src_ref, land_ref, ssem_ref, rsem_ref, after_ref, src_out, land_out):
        for cp in _small_copies(src_ref, land_ref, ssem_ref, rsem_ref, first):
            cp.wait_send()
            cp.wait_recv()

    return pl.pallas_call(
        body, name=name,
        out_shape=(pltpu.HBM(src.shape, F32), pltpu.HBM(land.shape, F32)),
        in_specs=(HBM, HBM, SEM, SEM, ANY), out_specs=(HBM, HBM), input_output_aliases={0: 0, 1: 1},
        compiler_params=pltpu.CompilerParams(has_side_effects=EFFECT),
    )(src, land, ssem, rsem, after)


def _small_pair_sum(vec, got):
    def body(v_ref, g_ref, o_ref):
        o_ref[...] = v_ref[...] + g_ref[...]

    return pl.pallas_call(body, name="small_pair_sum", in_specs=[VMEM_FULL] * 2, out_specs=VMEM_FULL,
                          out_shape=_sds(vec.shape, F32), compiler_params=_cp())(vec, got)


def _small_chip_sum(pair, got, place):
    def body(p_ref, pair_ref, got_ref, o_ref):
        acc = None
        for kk in range(N_CHIP):
            d = jnp.bitwise_xor(p_ref[0], kk)
            t = jnp.where(d == 0, pair_ref[...], jnp.where(d == 2, got_ref[0], jnp.where(d == 1, got_ref[1], got_ref[2])))
            acc = t if acc is None else acc + t
        o_ref[...] = acc

    grid_spec = pltpu.PrefetchScalarGridSpec(
        num_scalar_prefetch=1, grid=(1,),
        in_specs=[pl.BlockSpec(pair.shape, lambda i, p: (0, 0)), pl.BlockSpec(got.shape, lambda i, p: (0, 0, 0))],
        out_specs=pl.BlockSpec(pair.shape, lambda i, p: (0, 0)))
    return pl.pallas_call(body, name="small_chip_sum", grid_spec=grid_spec, out_shape=_sds(pair.shape, F32),
                          compiler_params=_cp(("arbitrary",)))(place, pair, got)


def _adam_math(w, g, m, v):
    m = B1 * m + (1.0 - B1) * g
    v = B2 * v + (1.0 - B2) * (g * g)
    m_hat = m / (1.0 - B1 ** STEP)
    v_hat = v / (1.0 - B2 ** STEP)
    delta = -LR * (m_hat / (jnp.sqrt(v_hat) + AEPS) + WD * w)
    return delta, m, v


def _adam_big(w, g, m, v, name, dep=None):
    rows, cols = w.shape
    tr = 128

    def body(w_ref, g_ref, m_ref, v_ref, dep_ref, go_ref, d_ref, nm_ref, nv_ref):
        g = g_ref[...]
        d, nm, nv = _adam_math(w_ref[...], g, m_ref[...], v_ref[...])
        go_ref[...] = g
        d_ref[...] = d
        nm_ref[...] = nm
        nv_ref[...] = nv

    blk = pl.BlockSpec((tr, cols), lambda i: (i, 0))
    return pl.pallas_call(
        body, name=name, grid=(rows // tr,),
        in_specs=[blk] * 4 + [ANY], out_specs=[blk] * 4, out_shape=[_sds((rows, cols), F32)] * 4,
        compiler_params=_cp(("arbitrary",)),
    )(w, g, m, v, _dep(dep))


def _adam_small(ws, gs, ms, vs, dep=None):
    n = len(ws)

    def body(*refs):
        for i in range(n):
            d, nm, nv = _adam_math(refs[i][...], refs[n + i][...], refs[2 * n + i][...], refs[3 * n + i][...])
            refs[4 * n + 1 + i][...] = d
            refs[5 * n + 1 + i][...] = nm
            refs[6 * n + 1 + i][...] = nv

    shapes = [_sds(w.shape, F32) for w in ws]
    outs = pl.pallas_call(
        body, name="adam_small",
        in_specs=[VMEM_FULL] * (4 * n) + [ANY], out_specs=[VMEM_FULL] * (3 * n), out_shape=shapes * 3,
        compiler_params=_cp(),
    )(*ws, *gs, *ms, *vs, _dep(dep))
    return outs[:n], outs[n:2 * n], outs[2 * n:]


def _pad_rows8(a):
    flat = a.reshape(-1, 128)
    pad = (-flat.shape[0]) % 8
    if pad:
        flat = jnp.concatenate([flat, jnp.zeros((pad, 128), F32)], axis=0)
    return flat


def kernel(x, meta_tokens, norm1_w, w_in, gate_w2, gate_b, gla_norm_w, pool_w, pool_scale, w_out, norm2_w, mlp_w1, mlp_w2, final_norm_w, loss_target, m_meta_tokens, m_norm1_w, m_w_in, m_gate_w2, m_gate_b, m_gla_norm_w, m_pool_w, m_pool_scale, m_w_out, m_norm2_w, m_mlp_w1, m_mlp_w2, m_final_norm_w, v_meta_tokens, v_norm1_w, v_w_in, v_gate_w2, v_gate_b, v_gla_norm_w, v_pool_w, v_pool_scale, v_w_out, v_norm2_w, v_mlp_w1, v_mlp_w2, v_final_norm_w):
    cx, cy, cc = lax.axis_index("x"), lax.axis_index("y"), lax.axis_index("c")
    me = (2 * cx + cy).astype(jnp.int32)

    place = jnp.stack([me, cc.astype(jnp.int32)])
    fw = final_norm_w.reshape(1, D)

    mine = jnp.concatenate([meta_tokens.reshape(64, 128), gate_w2[0], pool_w[0].reshape(512, 128)], axis=0)
    small = lax.dynamic_update_slice(jnp.zeros((N_CHIP, 592, 128), F32), mine[None], (me, 0, 0))
    (s_sm,), (r_sm,), (f_sm,), tok = _gather_start([small], "gather_start_small", [True])
    (sem_win_d,), (win,), tok = _gather_step("gather_start_win", [_cast_win(w_in[0], place, tok)], [], [(0, "d")])
    wout, w1, w2 = (_cast_into(w_out[0], place, D, "cast_wout", tok), _cast_into(mlp_w1[0], place, D, "cast_w1", tok),
                    _cast_into(mlp_w2[0], place, D, "cast_w2", tok))
    small = _gather_wait(f_sm, s_sm, r_sm, w2, "gather_wait_small", True)
    metaF = jnp.concatenate([small[k, 0:64].reshape(N_META, 512) for k in range(N_CHIP)], axis=1)
    gw2F = jnp.concatenate([small[k, 64:80] for k in range(N_CHIP)], axis=1)
    pwF = jnp.concatenate([small[k, 80:592].reshape(4, 64, GC) for k in range(N_CHIP)], axis=1)

    fly = {"win": win, "wout": wout, "w1": w1, "w2": w2}
    sems = {"win_d": sem_win_d}

    def step(name, names, waits, starts, after):
        at = {nm: i for i, nm in enumerate(names)}
        new, arrs, tok = _gather_step(name, [fly[nm] for nm in names], [(at[nm], k) for nm, k in waits],
                                      [(at[nm], k) for nm, k in starts], [sems[nm + "_" + k] for nm, k in waits], after)
        fly.update(zip(names, arrs))
        sems.update({nm + "_" + k: s for (nm, k), s in zip(starts, new)})
        return tok

    def relay_first():
        return step("gather_relay_win", ["win", "wout", "w1"], [("win", "d")],
                    [("win", "r"), ("win", "fx"), ("wout", "d"), ("w1", "d")], [])

    def get_win(after):
        tok = step("gather_land_win", ["win"], [("win", "r")], [("win", "fd")], [after, m_w_in[0]])
        step("gather_wait_win", ["win"], [("win", "fx"), ("win", "fd")], [], [tok])
        return fly["win"]

    def relay_mid(after):
        return step("gather_relay_mid", ["wout", "w1", "w2"], [("wout", "d"), ("w1", "d")],
                    [("wout", "r"), ("w1", "r"), ("w2", "d"), ("wout", "fx"), ("w1", "fx")], [after, v_w_in[0]])

    def land_wout(after):
        return step("gather_land_wout", ["wout"], [("wout", "r")], [("wout", "fd")], [after])

    def get_wout(after):
        step("gather_wait_wout", ["wout"], [("wout", "fx"), ("wout", "fd")], [], [after])
        tok = step("gather_land_w1", ["w1"], [("w1", "r")], [("w1", "fd")], [fly["wout"]])
        return fly["wout"].reshape(D, D), tok

    def get_w1(after):
        step("gather_wait_w1", ["w1"], [("w1", "fx"), ("w1", "fd")], [], [after])
        return fly["w1"]

    def relay_last(after):
        return step("gather_relay_w2", ["w2"], [("w2", "d")], [("w2", "r"), ("w2", "fx")], [after])

    def get_w2(after):
        tok = step("gather_land_w2", ["w2"], [("w2", "r")], [("w2", "fd")], [after])
        step("gather_wait_w2", ["w2"], [("w2", "fx"), ("w2", "fd")], [], [tok])
        return fly["w2"].reshape(DFF, D)

    pairs, pending = {}, {}

    halves = {}

    def reduce_(names, after, tag):
        items = [(pending[nm][3], pending[nm][4], pending[nm][1], pending[nm][2]) for nm in names]
        landed = _rs_wait(items, after, "rs_wait_" + tag)
        fulls = [_final_sum(pending[nm][0], rb, place, "final_sum_" + nm) for nm, rb in zip(names, landed)]
        sent, token = _half_start(fulls, "half_start_" + tag)
        halves.update(zip(names, sent))
        return token

    def grad_start(nm, g):
        ssem, rsem, g_thru, land, token = _pair_start(g, "pair_start_" + nm)
        pairs[nm] = (ssem, rsem, g_thru, land)
        if nm == "win":
            token = reduce_(["w2", "w1", "wout"], token, "mlp_wout")
        return token

    def grad_finish(nm, after):
        ssem, rsem, g_thru, land = pairs[nm]
        g, rcv = _pair_wait(g_thru, land, ssem, rsem, after, "pair_wait_" + nm)
        sb, sf = _pair_sum(g, rcv, place, "pair_sum_" + nm)
        ssem, rsem, sb_thru, land, token = _rs_start(sb, "rs_start_" + nm)
        pending[nm] = (sf, ssem, rsem, sb_thru, land)
        return token

    (grad_x, loss8, d_n1w, d_gb, d_gnw, d_ps, d_n2w, d_fw, d_meta, d_gw2, d_pw) = _local_step(
        x[0], loss_target[0], dict(relay_first=relay_first, win=get_win, relay_mid=relay_mid, land_wout=land_wout,
                                   wout=get_wout, w1=get_w1, relay_last=relay_last, w2=get_w2),
        metaF, gw2F, pwF, norm1_w, gate_b, gla_norm_w, pool_scale, norm2_w, fw, grad_start, grad_finish)
    return _reduce_and_update(
        me, place, pending, halves, reduce_, grad_x, loss8, d_n1w, d_gb, d_gnw, d_ps, d_n2w, d_fw, d_meta, d_gw2, d_pw,
        meta_tokens, norm1_w, w_in, gate_w2, gate_b, gla_norm_w, pool_w, pool_scale, w_out, norm2_w, mlp_w1, mlp_w2, fw,
        m_meta_tokens, m_norm1_w, m_w_in, m_gate_w2, m_gate_b, m_gla_norm_w, m_pool_w, m_pool_scale, m_w_out, m_norm2_w,
        m_mlp_w1, m_mlp_w2, m_final_norm_w, v_meta_tokens, v_norm1_w, v_w_in, v_gate_w2, v_gate_b, v_gla_norm_w, v_pool_w,
        v_pool_scale, v_w_out, v_norm2_w, v_mlp_w1, v_mlp_w2, v_final_norm_w)


def _local_step(x, target, gather, metaF, gw2F, pwF, norm1_w, gate_b, gla_norm_w, pool_scale, norm2_w, fw, grad_start,
                grad_finish):
    h0, u = _embed_norm(x, metaF, norm1_w, gather["relay_first"]())
    Win = gather["win"](u)
    P = _in_proj(u, Win)
    gw2p = jnp.pad(gw2F, ((0, 128 - RANK), (0, 0)))
    yb, op = _pool_fwd(P, pwF, pool_scale, gather["relay_mid"](P))
    o, og, sp = _gla_fwd(P, gw2p, gate_b, gla_norm_w, gather["land_wout"](op))
    Wout, tok = gather["wout"](og)
    h1 = _out_proj(og, op, Wout, h0, tok)
    n2 = _norm_rows(h1, norm2_w, "norm2")
    W1 = gather["w1"](n2)
    zr, a, tok = _mlp_up(n2, W1, 0)
    zr, a, _ = _mlp_up(n2, W1, 1, (zr, a), gather["relay_last"](tok))
    W2 = gather["w2"](a)
    h2 = _mlp_down(a, W2, h1)

    dh2, dh2b, d_fw, loss8 = _loss_head(h2, target, fw)
    tok = grad_start("w2", _grad_w2(a, dh2b).reshape(N_CHIP, D, D))
    dz = _mlp_dz(dh2b, W2, zr, tok)
    tok = grad_finish("w2", dz)
    tok = grad_start("w1", _grad_w1(n2, dz, tok))
    dn2 = _mlp_dn(dz, W1, tok)
    tok = grad_finish("w1", dn2)
    dh1, dh1b, d_n2w = _norm_bwd(dn2, h1, dh2, norm2_w, "norm2_bwd", tok)
    dmixed = _mixed_grad(dh1b, Wout)
    tok = grad_start("wout", _grad_wout(og, op, dh1b))
    dpu, d_pw, d_ps = _pool_bwd(dmixed, yb, pwF, pool_scale, tok)
    dq, dk, dv, dr, dglr, d_gw2p, d_gb, d_gnw = _gla_bwd(dmixed, o, P, gw2p, gate_b, gla_norm_w, sp, tok)
    d_gw2 = d_gw2p[0:RANK]
    tok = grad_finish("wout", dq)
    tok = grad_start("win", _grad_win(u, dq, dk, dv, dr, dglr, dpu, tok))
    du = _in_grad(dq, dk, dv, dr, dglr, dpu, Win, tok)
    tok = grad_finish("win", du)
    grad_x, d_meta, d_n1w = _input_grad(du, h0, dh1, norm1_w, tok)
    return grad_x, loss8, d_n1w, d_gb, d_gnw, d_ps, d_n2w, d_fw, d_meta, d_gw2, d_pw


def _reduce_and_update(me, place, pending, halves, reduce_, grad_x, loss8, d_n1w, d_gb, d_gnw, d_ps, d_n2w, d_fw, d_meta, d_gw2,
                       d_pw,
                       meta_tokens, norm1_w, w_in, gate_w2, gate_b, gla_norm_w, pool_w, pool_scale, w_out, norm2_w,
                       mlp_w1, mlp_w2, fw, m_meta_tokens, m_norm1_w, m_w_in, m_gate_w2, m_gate_b, m_gla_norm_w, m_pool_w,
                       m_pool_scale, m_w_out, m_norm2_w, m_mlp_w1, m_mlp_w2, m_final_norm_w, v_meta_tokens, v_norm1_w, v_w_in,
                       v_gate_w2, v_gate_b, v_gla_norm_w, v_pool_w, v_pool_scale, v_w_out, v_norm2_w, v_mlp_w1, v_mlp_w2,
                       v_final_norm_w):
    parts = [loss8, d_n1w, d_gb, d_gnw, d_ps, d_n2w, d_fw, d_meta, d_gw2, d_pw]
    packed = [_pad_rows8(p) for p in parts]
    sizes = [p.shape[0] for p in packed]
    vec = jnp.concatenate(packed, axis=0)

    big = {}
    params = {"w2": (mlp_w2[0], m_mlp_w2[0], v_mlp_w2[0]), "w1": (mlp_w1[0], m_mlp_w1[0], v_mlp_w1[0]),
              "wout": (w_out[0], m_w_out[0], v_w_out[0]), "win": (w_in[0], m_w_in[0], v_w_in[0])}

    def update(names, after, tag):
        fulls = _half_wait([halves[nm] for nm in names], [after], "half_wait_" + tag)
        tok = None
        for nm, full in zip(names, fulls):
            w, m, v = params[nm]
            big[nm] = _adam_big(w, full, m, v, "adam_" + nm, tok)
            tok = big[nm][3]
        return tok

    s1, r1, vec, land1, tok = _small_start(vec, True, "small_start_pair")
    tok = update(["w2"], tok, "w2")
    vec, got = _small_wait(vec, land1, s1, r1, True, tok, "small_wait_pair")
    pair = _small_pair_sum(vec, got)
    s2, r2, pair, land2, tok = _small_start(pair, False, "small_start_chips")
    tok = update(["w1", "wout"], tok, "w1_wout")
    pair, got = _small_wait(pair, land2, s2, r2, False, tok, "small_wait_chips")
    red = _small_chip_sum(pair, got, place)
    tok = reduce_(["win"], red, "win")
    after = update(["win"], tok, "win")
    offs = [0]
    for s in sizes:
        offs.append(offs[-1] + s)

    def take(i, shape):
        n = 1
        for d in shape:
            n *= d
        return red[offs[i]:offs[i] + n // 128].reshape(shape)

    loss = red[0, 0]
    G_n1w = take(1, (1, D))
    G_gb = take(2, (1, KW))
    G_gnw = take(3, (1, DV))
    G_ps = take(4, (1, PW))
    G_n2w = take(5, (1, D))
    G_fw = take(6, (1, D))
    G_meta = lax.dynamic_slice(take(7, (N_META, D)), (0, me * 512), (N_META, 512))
    G_gw2 = lax.dynamic_slice(take(8, (RANK, KW)), (0, me * 128), (RANK, 128))
    G_pw = lax.dynamic_slice(take(9, (4, GC, GC)), (0, me * 64, 0), (4, 64, GC))

    G_win, d_win, nm_win, nv_win = big["win"]
    G_wout, d_wout, nm_wout, nv_wout = big["wout"]
    G_w1, d_w1, nm_w1, nv_w1 = big["w1"]
    G_w2, d_w2, nm_w2, nv_w2 = big["w2"]
    ws = [meta_tokens, norm1_w, gate_w2[0], gate_b, gla_norm_w, pool_w[0], pool_scale, norm2_w, fw]
    gs = [G_meta, G_n1w, G_gw2, G_gb, G_gnw, G_pw, G_ps, G_n2w, G_fw]
    ms = [m_meta_tokens, m_norm1_w, m_gate_w2[0], m_gate_b, m_gla_norm_w, m_pool_w[0], m_pool_scale, m_norm2_w,
          m_final_norm_w.reshape(1, D)]
    vs = [v_meta_tokens, v_norm1_w, v_gate_w2[0], v_gate_b, v_gla_norm_w, v_pool_w[0], v_pool_scale, v_norm2_w,
          v_final_norm_w.reshape(1, D)]
    ds, nms, nvs = _adam_small(ws, gs, ms, vs, after)

    def assemble(small, win_, wout_, w1_, w2_):
        meta_, n1_, gw2_, gb_, gnw_, pw_, ps_, n2_, fw_ = small
        return (meta_, n1_, win_[None], gw2_[None], gb_, gnw_, pw_[None], ps_, wout_[None], n2_, w1_[None], w2_[None],
                fw_.reshape(D))

    grads_out = assemble(gs, G_win, G_wout, G_w1, G_w2)
    deltas = assemble(ds, d_win, d_wout, d_w1, d_w2)
    new_m = assemble(nms, nm_win, nm_wout, nm_w1, nm_w2)
    new_v = assemble(nvs, nv_win, nv_wout, nv_w1, nv_w2)
    return (loss, grad_x[None], *grads_out, *deltas, *new_m, *new_v)
```

```python
import functools

import jax
import jax.numpy as jnp
from jax import lax
from jax.experimental import pallas as pl
from jax.experimental.pallas import tpu as pltpu

F32 = jnp.float32
BF16 = jnp.bfloat16

D = 2048
SEQ = 2048
N_META = 16
CH = 64
TP = 2176
NCH = TP // CH
ROW_LO = 112
X_LO = 128
ROW_HI = TP
XT = 128
NXT = TP // XT
HEADS = 4
DK = 128
DV = 256
KW = HEADS * DK
GW = HEADS * DV
RANK = 16
TAU = 16.0
WINDOWS = (2, 4, 8, 16)
PW = 1024
GC = 256
DFF = 8192
EPS = 1e-6
SHARD_IN = 1028
PAD_IN = 1152
N_CHIP = 4

LR = 0.001
B1 = 0.9
B2 = 0.999
AEPS = 1e-08
WD = 0.01
STEP = 10

VMEM_LIMIT = 60 * 1024 * 1024
ANY = pl.BlockSpec(memory_space=pl.ANY)
VMEM_FULL = pl.BlockSpec(memory_space=pltpu.VMEM)
MESH = pl.DeviceIdType.MESH


def _cp(sem=None):
    if sem is None:
        return pltpu.CompilerParams(vmem_limit_bytes=VMEM_LIMIT)
    return pltpu.CompilerParams(dimension_semantics=sem, vmem_limit_bytes=VMEM_LIMIT)


def _dot(a, b):
    return jnp.dot(a, b, preferred_element_type=F32)


def _dot_nt(a, b):
    return lax.dot_general(a, b, (((1,), (1,)), ((), ())), preferred_element_type=F32)


def _dot_tn(a, b):
    return lax.dot_general(a, b, (((0,), (0,)), ((), ())), preferred_element_type=F32)


def _sds(shape, dtype):
    return jax.ShapeDtypeStruct(shape, dtype)


def _embed_norm(x, meta_full, w, dep=None):
    def body(x_ref, meta_ref, w_ref, dep_ref, h_ref, u_ref):
        i = pl.program_id(0)

        @pl.when(i == 0)
        def _():
            h_ref[...] = jnp.zeros_like(h_ref)
            h_ref[ROW_LO:X_LO, :] = meta_ref[...]

        @pl.when(i >= 1)
        def _():
            h_ref[...] = x_ref[...]

        h = h_ref[...]
        r = lax.rsqrt(jnp.mean(h * h, axis=-1, keepdims=True) + EPS)
        u_ref[...] = ((h * r) * w_ref[...]).astype(BF16)

    return pl.pallas_call(
        body, name="embed_norm1", grid=(NXT,),
        in_specs=[pl.BlockSpec((XT, D), lambda i: (jnp.maximum(i - 1, 0), 0)),
                  pl.BlockSpec((N_META, D), lambda i: (0, 0)),
                  pl.BlockSpec((1, D), lambda i: (0, 0)), ANY],
        out_specs=[pl.BlockSpec((XT, D), lambda i: (i, 0)), pl.BlockSpec((XT, D), lambda i: (i, 0))],
        out_shape=[_sds((TP, D), F32), _sds((TP, D), BF16)],
        compiler_params=_cp(("arbitrary",)),
    )(x, meta_full, w, _dep(dep))


def _norm_rows(h, w, name):
    tr = 272

    def body(h_ref, w_ref, o_ref):
        hv = h_ref[...]
        r = lax.rsqrt(jnp.mean(hv * hv, axis=-1, keepdims=True) + EPS)
        o_ref[...] = ((hv * r) * w_ref[...]).astype(BF16)

    return pl.pallas_call(
        body, name=name, grid=(TP // tr,),
        in_specs=[pl.BlockSpec((tr, D), lambda i: (i, 0)), pl.BlockSpec((1, D), lambda i: (0, 0))],
        out_specs=pl.BlockSpec((tr, D), lambda i: (i, 0)),
        out_shape=_sds((TP, D), BF16),
        compiler_params=_cp(("arbitrary",)),
    )(h, w)


def _loss_head(h2, target, fw):
    def body(h_ref, t_ref, w_ref, dh_ref, dhb_ref, dw_ref, loss_ref):
        i = pl.program_id(0)

        @pl.when(i == 0)
        def _():
            dw_ref[...] = jnp.zeros_like(dw_ref)
            loss_ref[...] = jnp.zeros_like(loss_ref)

        h = h_ref[...]
        w = w_ref[...]
        r = lax.rsqrt(jnp.mean(h * h, axis=-1, keepdims=True) + EPS)
        xh = h * r
        y = xh * w
        is_x = (i >= 1).astype(F32)
        diff = (y - t_ref[...]) * is_x
        loss_ref[...] += jnp.sum(diff * diff) * (0.5 / D)
        dy = diff * (1.0 / D)
        dw_ref[...] += jnp.sum(dy * xh, axis=0, keepdims=True)
        gx = dy * w
        dh = r * (gx - xh * jnp.mean(gx * xh, axis=-1, keepdims=True))
        dh_ref[...] = dh
        dhb_ref[...] = dh.astype(BF16)

    return pl.pallas_call(
        body, name="loss_head", grid=(NXT,),
        in_specs=[pl.BlockSpec((XT, D), lambda i: (i, 0)),
                  pl.BlockSpec((XT, D), lambda i: (jnp.maximum(i - 1, 0), 0)),
                  pl.BlockSpec((1, D), lambda i: (0, 0))],
        out_specs=[pl.BlockSpec((XT, D), lambda i: (i, 0)), pl.BlockSpec((XT, D), lambda i: (i, 0)),
                   pl.BlockSpec((1, D), lambda i: (0, 0)), pl.BlockSpec((8, 128), lambda i: (0, 0))],
        out_shape=[_sds((TP, D), F32), _sds((TP, D), BF16), _sds((1, D), F32), _sds((8, 128), F32)],
        compiler_params=_cp(("arbitrary",)),
    )(h2, target, fw)


def _norm_bwd(dn, h, dres, w, name, dep=None):
    tr = 272

    def body(dn_ref, h_ref, dres_ref, w_ref, dep_ref, o_ref, ob_ref, dw_ref):
        @pl.when(pl.program_id(0) == 0)
        def _():
            dw_ref[...] = jnp.zeros_like(dw_ref)

        hv = h_ref[...]
        dnv = dn_ref[...]
        r = lax.rsqrt(jnp.mean(hv * hv, axis=-1, keepdims=True) + EPS)
        xh = hv * r
        dw_ref[...] += jnp.sum(dnv * xh, axis=0, keepdims=True)
        gx = dnv * w_ref[...]
        dh = dres_ref[...] + r * (gx - xh * jnp.mean(gx * xh, axis=-1, keepdims=True))
        o_ref[...] = dh
        ob_ref[...] = dh.astype(BF16)

    row = pl.BlockSpec((tr, D), lambda i: (i, 0))
    vec = pl.BlockSpec((1, D), lambda i: (0, 0))
    return pl.pallas_call(
        body, name=name, grid=(TP // tr,),
        in_specs=[row, row, row, vec, ANY], out_specs=[row, row, vec],
        out_shape=[_sds((TP, D), F32), _sds((TP, D), BF16), _sds((1, D), F32)],
        compiler_params=_cp(("arbitrary",)),
    )(dn, h, dres, w, _dep(dep))


def _input_grad(du, h0, dh1, w, dep=None):
    def body(du_ref, h_ref, dres_ref, w_ref, dep_ref, gx_ref, gm_ref, dw_ref):
        i = pl.program_id(0)

        @pl.when(i == 0)
        def _():
            dw_ref[...] = jnp.zeros_like(dw_ref)

        hv = h_ref[...]
        dnv = du_ref[...]
        r = lax.rsqrt(jnp.mean(hv * hv, axis=-1, keepdims=True) + EPS)
        xh = hv * r
        dw_ref[...] += jnp.sum(dnv * xh, axis=0, keepdims=True)
        g = dnv * w_ref[...]
        dh = dres_ref[...] + r * (g - xh * jnp.mean(g * xh, axis=-1, keepdims=True))

        @pl.when(i == 0)
        def _():
            gm_ref[...] = dh[ROW_LO:X_LO, :]

        @pl.when(i >= 1)
        def _():
            gx_ref[...] = dh

    row = pl.BlockSpec((XT, D), lambda i: (i, 0))
    vec = pl.BlockSpec((1, D), lambda i: (0, 0))
    return pl.pallas_call(
        body, name="input_grad", grid=(NXT,),
        in_specs=[row, row, row, vec, ANY],
        out_specs=[pl.BlockSpec((XT, D), lambda i: (jnp.maximum(i - 1, 0), 0)),
                   pl.BlockSpec((N_META, D), lambda i: (0, 0)), vec],
        out_shape=[_sds((SEQ, D), F32), _sds((N_META, D), F32), _sds((1, D), F32)],
        compiler_params=_cp(("arbitrary",)),
    )(du, h0, dh1, w, _dep(dep))


def _in_proj(u, wg):
    def body(u_ref, w_ref, o_ref):
        o_ref[0] = _dot(u_ref[...], w_ref[0])

    return pl.pallas_call(
        body, name="in_proj", grid=(N_CHIP,),
        in_specs=[VMEM_FULL, pl.BlockSpec((1, D, PAD_IN), lambda k: (k, 0, 0))],
        out_specs=pl.BlockSpec((1, TP, PAD_IN), lambda k: (k, 0, 0)),
        out_shape=_sds((N_CHIP, TP, PAD_IN), F32),
        compiler_params=_cp(("arbitrary",)),
    )(u, wg)


def _out_proj(og, op, wout, h0, dep=None):
    tn = 512

    def body(og_ref, op_ref, w_ref, h_ref, dep_ref, o_ref):
        acc = _dot(og_ref[...], w_ref[0:GW, :]) + _dot(op_ref[...], w_ref[GW:D, :])
        o_ref[...] = h_ref[...] + acc

    return pl.pallas_call(
        body, name="out_proj", grid=(D // tn,),
        in_specs=[VMEM_FULL, VMEM_FULL, pl.BlockSpec((D, tn), lambda j: (0, j)),
                  pl.BlockSpec((TP, tn), lambda j: (0, j)), ANY],
        out_specs=pl.BlockSpec((TP, tn), lambda j: (0, j)),
        out_shape=_sds((TP, D), F32),
        compiler_params=_cp(("arbitrary",)),
    )(og, op, wout, h0, _dep(dep))


def _mlp_up(n2, w1g, part, prev=None, dep=None):
    tn = 1024
    per = D // tn

    def body(n_ref, w_ref, dep_ref, *rest):
        zr_ref, a_ref, token = rest[-3:]
        z = jnp.maximum(_dot(n_ref[...], w_ref[0]), 0.0)
        zr_ref[...] = z.astype(BF16)
        a_ref[...] = (z * z).astype(BF16)
        token[...] = jnp.zeros_like(token)

    col = pl.BlockSpec((TP, tn), lambda k, j: (0, (2 * part + k) * per + j))
    return pl.pallas_call(
        body, name="mlp_up_%d" % part, grid=(N_CHIP // 2, per),
        in_specs=[VMEM_FULL, pl.BlockSpec((1, D, tn), lambda k, j: (2 * part + k, 0, j)), ANY] + ([ANY, ANY] if prev else []),
        out_specs=[col, col, pl.BlockSpec((8, 128), lambda k, j: (0, 0))],
        out_shape=[_sds((TP, DFF), BF16), _sds((TP, DFF), BF16), _sds((8, 128), F32)],
        input_output_aliases={3: 0, 4: 1} if prev else {},
        compiler_params=_cp(("arbitrary", "arbitrary")),
    )(n2, w1g, _dep(dep), *(prev or ()))


def _mlp_down(a, w2, h1):
    tk = 1024
    nk = DFF // tk

    def body(a_ref, w_ref, h_ref, o_ref, acc_ref):
        k = pl.program_id(0)

        @pl.when(k == 0)
        def _():
            pltpu.sync_copy(h_ref, acc_ref)

        acc_ref[...] += _dot(a_ref[...], w_ref[...])

        @pl.when(k == nk - 1)
        def _():
            pltpu.sync_copy(acc_ref, o_ref)

    return pl.pallas_call(
        body, name="mlp_down", grid=(nk,),
        in_specs=[pl.BlockSpec((TP, tk), lambda k: (0, k)), pl.BlockSpec((tk, D), lambda k: (k, 0)), ANY],
        out_specs=ANY,
        out_shape=_sds((TP, D), F32),
        scratch_shapes=[pltpu.VMEM((TP, D), F32)],
        compiler_params=_cp(("arbitrary",)),
    )(a, w2, h1)


def _mlp_dz(dh2b, w2, zr, dep=None):
    tn = 1024

    def body(d_ref, w_ref, z_ref, dep_ref, o_ref):
        da = _dot_nt(d_ref[...], w_ref[...])
        o_ref[...] = (da * (2.0 * z_ref[...].astype(F32))).astype(BF16)

    col = pl.BlockSpec((TP, tn), lambda j: (0, j))
    return pl.pallas_call(
        body, name="mlp_dz", grid=(DFF // tn,),
        in_specs=[VMEM_FULL, pl.BlockSpec((tn, D), lambda j: (j, 0)), col, ANY],
        out_specs=col,
        out_shape=_sds((TP, DFF), BF16),
        compiler_params=_cp(("arbitrary",)),
    )(dh2b, w2, zr, _dep(dep))


def _grad_w2(a, dh2b):
    tm = 1024

    def body(a_ref, d_ref, o_ref):
        o_ref[...] = _dot_tn(a_ref[...], d_ref[...])

    return pl.pallas_call(
        body, name="grad_w2", grid=(DFF // tm,),
        in_specs=[pl.BlockSpec((TP, tm), lambda j: (0, j)), VMEM_FULL],
        out_specs=pl.BlockSpec((tm, D), lambda j: (j, 0)),
        out_shape=_sds((DFF, D), F32),
        compiler_params=_cp(("arbitrary",)),
    )(a, dh2b)


def _dep(token):
    return jnp.zeros((8, 128), F32) if token is None else token


def _grad_w1(n2, dz, dep=None):
    tn = 1024
    per = D // tn

    def body(n_ref, d_ref, dep_ref, o_ref):
        o_ref[0] = _dot_tn(n_ref[...], d_ref[...])

    return pl.pallas_call(
        body, name="grad_w1", grid=(N_CHIP, per),
        in_specs=[VMEM_FULL, pl.BlockSpec((TP, tn), lambda k, j: (0, k * per + j)), ANY],
        out_specs=pl.BlockSpec((1, D, tn), lambda k, j: (k, 0, j)),
        out_shape=_sds((N_CHIP, D, D), F32),
        compiler_params=_cp(("arbitrary", "arbitrary")),
    )(n2, dz, _dep(dep))


def _mlp_dn(dz, w1g, dep=None):
    tk = 1024
    per = D // tk
    nk = DFF // tk

    def body(d_ref, w_ref, dep_ref, o_ref, acc_ref):
        k = pl.program_id(0)
        part = _dot_nt(d_ref[...], w_ref[0])

        @pl.when(k == 0)
        def _():
            acc_ref[...] = part

        @pl.when(k > 0)
        def _():
            acc_ref[...] += part

        @pl.when(k == nk - 1)
        def _():
            pltpu.sync_copy(acc_ref, o_ref)

    return pl.pallas_call(
        body, name="mlp_dn", grid=(nk,),
        in_specs=[pl.BlockSpec((TP, tk), lambda k: (0, k)),
                  pl.BlockSpec((1, D, tk), lambda k: (k // per, 0, k % per)), ANY],
        out_specs=ANY,
        out_shape=_sds((TP, D), F32),
        scratch_shapes=[pltpu.VMEM((TP, D), F32)],
        compiler_params=_cp(("arbitrary",)),
    )(dz, w1g, _dep(dep))


def _mixed_grad(dh1b, wout):
    tn = 512

    def body(d_ref, w_ref, o_ref):
        o_ref[...] = _dot_nt(d_ref[...], w_ref[...])

    return pl.pallas_call(
        body, name="mixed_grad", grid=(D // tn,),
        in_specs=[VMEM_FULL, pl.BlockSpec((tn, D), lambda j: (j, 0))],
        out_specs=pl.BlockSpec((TP, tn), lambda j: (0, j)),
        out_shape=_sds((TP, D), F32),
        compiler_params=_cp(("arbitrary",)),
    )(dh1b, wout)


def _grad_wout(og, op, dh1b):
    tm = 512

    def body(og_ref, op_ref, d_ref, o_ref):
        j = pl.program_id(0)

        @pl.when(j < 2)
        def _():
            o_ref[0] = _dot_tn(og_ref[...], d_ref[...])

        @pl.when(j >= 2)
        def _():
            o_ref[0] = _dot_tn(op_ref[...], d_ref[...])

    return pl.pallas_call(
        body, name="grad_wout", grid=(N_CHIP,),
        in_specs=[pl.BlockSpec((TP, tm), lambda j: (0, jnp.minimum(j, 1))),
                  pl.BlockSpec((TP, tm), lambda j: (0, jnp.maximum(j - 2, 0))), VMEM_FULL],
        out_specs=pl.BlockSpec((1, tm, D), lambda j: (j, 0, 0)),
        out_shape=_sds((N_CHIP, tm, D), F32),
        compiler_params=_cp(("arbitrary",)),
    )(og, op, dh1b)


def _in_grad(dq, dk, dv, dr, dglr, dpu, wg, dep=None):
    def body(dq_ref, dk_ref, dv_ref, dr_ref, dg_ref, dpu_ref, w_ref, dep_ref, o_ref):
        dv, dr, dg = dv_ref[...], dr_ref[...], dg_ref[...]
        head, tail = slice(0, GW), slice(GW, PAD_IN)
        o_ref[...] = (_dot_nt(dq_ref[...], w_ref[0, :, 0:KW]) + _dot_nt(dk_ref[...], w_ref[0, :, KW:GW])
                      + _dot_nt(dv[:, 0:128], w_ref[0, :, tail])
                      + _dot_nt(dv, w_ref[1, :, head]) + _dot_nt(dr[:, 0:128], w_ref[1, :, tail])
                      + _dot_nt(dr, w_ref[2, :, head]) + _dot_nt(dg, w_ref[2, :, tail])
                      + _dot_nt(dpu_ref[...], w_ref[3, :, head]) + _dot_nt(dg, w_ref[3, :, tail]))

    tn = 512
    return pl.pallas_call(
        body, name="in_grad", grid=(D // tn,),
        in_specs=[VMEM_FULL] * 6 + [pl.BlockSpec((N_CHIP, tn, PAD_IN), lambda j: (0, j, 0)), ANY],
        out_specs=pl.BlockSpec((TP, tn), lambda j: (0, j)),
        out_shape=_sds((TP, D), F32),
        compiler_params=_cp(("arbitrary",)),
    )(dq, dk, dv, dr, dglr, dpu, wg, _dep(dep))


def _grad_win(u, dq, dk, dv, dr, dglr, dpu, dep=None):
    tm = 512

    def body(u_ref, dq_hbm, dk_hbm, dv_hbm, dr_hbm, dg_hbm, dpu_hbm, dep_ref, o_ref, dp_ref, sem):
        k, m = pl.program_id(0), pl.program_id(1)
        head, tail = slice(0, GW), slice(GW, PAD_IN)
        pieces = [[(dq_hbm, slice(0, KW)), (dk_hbm, slice(KW, GW)), (dv_hbm.at[:, 0:128], tail)],
                  [(dv_hbm, head), (dr_hbm.at[:, 0:128], tail)],
                  [(dr_hbm, head), (dg_hbm, tail)],
                  [(dpu_hbm, head), (dg_hbm, tail)]]

        def copies(kk):
            return [pltpu.make_async_copy(src, dp_ref.at[kk % 2, :, cols], sem.at[kk % 2, i])
                    for i, (src, cols) in enumerate(pieces[kk])]

        @pl.when((k == 0) & (m == 0))
        def _():
            for cp in copies(0):
                cp.start()

        for kk in range(N_CHIP):
            @pl.when((k == kk) & (m == 0))
            def _(kk=kk):
                for cp in copies(kk):
                    cp.wait()
                if kk + 1 < N_CHIP:
                    for cp in copies(kk + 1):
                        cp.start()

        g = _dot_tn(u_ref[...], dp_ref[k % 2])
        lane = lax.broadcasted_iota(jnp.int32, (tm, PAD_IN), 1)
        for kk in range(N_CHIP):
            @pl.when(k == kk)
            def _(kk=kk):
                if kk == 0:
                    nat = g
                elif kk < 3:
                    nat = pltpu.roll(g, PAD_IN - 4 * kk, 1)
                else:
                    nat = jnp.where(lane < 4, pltpu.roll(g, PAD_IN - (GW + 12), 1), pltpu.roll(g, 4, 1))
                o_ref[0] = nat[:, 0:SHARD_IN]

    return pl.pallas_call(
        body, name="grad_win", grid=(N_CHIP, D // tm),
        in_specs=[pl.BlockSpec((TP, tm), lambda k, m: (0, m))] + [ANY] * 7,
        out_specs=pl.BlockSpec((1, tm, SHARD_IN), lambda k, m: (k, m, 0)),
        out_shape=_sds((N_CHIP, D, SHARD_IN), F32),
        scratch_shapes=[pltpu.VMEM((2, TP, PAD_IN), BF16), pltpu.SemaphoreType.DMA((2, 3))],
        compiler_params=_cp(("arbitrary", "arbitrary")),
    )(u, dq, dk, dv, dr, dglr, dpu, _dep(dep))


def _split3(x):
    hi = x.astype(BF16)
    r1 = x - hi.astype(F32)
    mid = r1.astype(BF16)
    lo = (r1 - mid.astype(F32)).astype(BF16)
    return hi, mid, lo


def _tri_sum(tri, x):
    hi, mid, lo = _split3(x)
    return _dot(tri, hi) + _dot(tri, mid) + _dot(tri, lo)


def _gla_common(n, glr, gw2, gb):
    rows = n * CH + lax.broadcasted_iota(jnp.int32, (CH, 1), 0)
    valid = (rows >= ROW_LO) & (rows < ROW_HI)
    g_raw = _dot(glr.astype(BF16), gw2.astype(BF16)) + gb
    logsig = jnp.minimum(g_raw, 0.0) - jnp.log(1.0 + jnp.exp(-jnp.abs(g_raw)))
    logg = jnp.where(valid, logsig * (1.0 / TAU), 0.0)
    ci = lax.broadcasted_iota(jnp.int32, (CH, CH), 0)
    si = lax.broadcasted_iota(jnp.int32, (CH, CH), 1)
    lower = ci >= si
    G = _tri_sum(lower.astype(BF16), logg)
    Gl = G[CH - 1:CH, :]
    return valid, g_raw, lower, G, Gl


def _p_specs(index):
    def spec(width, shard, col):
        return pl.BlockSpec((1, CH, width), lambda s: (shard, index(s), col))

    return [spec(KW, 0, 0), spec(KW, 0, 1), spec(GW, 1, 0), spec(128, 0, 8), spec(GW, 2, 0), spec(128, 1, 8),
            spec(128, 2, 8), spec(128, 3, 8)]


def _p_load(q_ref, k_ref, vm_ref, vh_ref, rm_ref, rh_ref, ga_ref, gb_ref):
    def joined(main, head):
        return jnp.concatenate([main[:, 0:128] + head, main[:, 128:]], axis=1)

    return q_ref[0], k_ref[0], joined(vm_ref[0], vh_ref[0]), joined(rm_ref[0], rh_ref[0]), ga_ref[0] + gb_ref[0]


def _gla_fwd(P, gw2, gb, gnw, dep=None):
    scale = DK ** -0.5

    def body(p0, p1, p2, p3, p4, p5, p6, p7, gw2_ref, gb_ref, gnw_ref, dep_ref, o_ref, og_ref, sp_ref, st_ref):
        n = pl.program_id(0)

        @pl.when(n == 0)
        def _():
            st_ref[...] = jnp.zeros_like(st_ref)

        q_all, k_all, v_all, r_all, glr = _p_load(p0, p1, p2, p3, p4, p5, p6, p7)
        _, _, lower, G, Gl = _gla_common(n, glr, gw2_ref[...], gb_ref[...])
        eG = jnp.exp(G)
        eN = jnp.exp(-G)
        eE = jnp.exp(Gl - G)
        dec = jnp.exp(Gl)
        gnw_v = gnw_ref[...]
        for h in range(HEADS):
            ks = slice(h * DK, (h + 1) * DK)
            vs = slice(h * DV, (h + 1) * DV)
            kh = k_all[:, ks]
            vh = v_all[:, vs].astype(BF16)
            qd = ((q_all[:, ks] * scale) * eG[:, ks]).astype(BF16)
            ki = (kh * eN[:, ks]).astype(BF16)
            ke = (kh * eE[:, ks]).astype(BF16)
            st = st_ref[h]
            a = jnp.where(lower, _dot_nt(qd, ki), 0.0).astype(BF16)
            o = _dot(a, vh) + _dot_nt(qd, st.astype(BF16))
            sp_ref[0, h] = st
            st_ref[h] = st * dec[:, ks] + _dot_tn(vh, ke)
            o_ref[:, vs] = o
            rs = lax.rsqrt(jnp.mean(o * o, axis=-1, keepdims=True) + EPS)
            rv = r_all[:, vs]
            gate = rv / (1.0 + jnp.exp(-rv))
            og_ref[:, vs] = (((o * rs) * gnw_v) * gate).astype(BF16)

    rv_ = pl.BlockSpec((CH, GW), lambda n: (n, 0))

    def full(shape):
        return pl.BlockSpec(shape, lambda n: tuple(0 for _ in shape))

    return pl.pallas_call(
        body, name="gla_fwd", grid=(NCH,),
        in_specs=_p_specs(lambda n: n) + [full((128, KW)), full((1, KW)), full((1, DV)), ANY],
        out_specs=[rv_, rv_, pl.BlockSpec((1, HEADS, DV, DK), lambda n: (n, 0, 0, 0))],
        out_shape=[_sds((TP, GW), F32), _sds((TP, GW), BF16), _sds((NCH, HEADS, DV, DK), F32)],
        scratch_shapes=[pltpu.VMEM((HEADS, DV, DK), F32)],
        compiler_params=_cp(("arbitrary",)),
    )(*([P] * 8), gw2, gb, gnw, _dep(dep))


def _gla_bwd(dog, o, P, gw2, gb, gnw, sp, dep=None):
    scale = DK ** -0.5

    def body(dog_ref, o_ref, p0, p1, p2, p3, p4, p5, p6, p7, gw2_ref, gb_ref, gnw_ref, sp_ref, dep_ref,
             dq_ref, dk_ref, dv_ref, dr_ref, dglr_ref, dgw2_ref, dgb_ref, dgnw_ref, ds_ref):
        step = pl.program_id(0)
        n = NCH - 1 - step

        @pl.when(step == 0)
        def _():
            ds_ref[...] = jnp.zeros_like(ds_ref)
            dgw2_ref[...] = jnp.zeros_like(dgw2_ref)
            dgb_ref[...] = jnp.zeros_like(dgb_ref)
            dgnw_ref[...] = jnp.zeros_like(dgnw_ref)

        q_all, k_all, v_all, r_all, glr_v = _p_load(p0, p1, p2, p3, p4, p5, p6, p7)
        gw2_b = gw2_ref[...].astype(BF16)
        valid, g_raw, lower, G, Gl = _gla_common(n, glr_v, gw2_ref[...], gb_ref[...])
        upper = lax.broadcasted_iota(jnp.int32, (CH, CH), 0) <= lax.broadcasted_iota(jnp.int32, (CH, CH), 1)
        eG = jnp.exp(G)
        eN = jnp.exp(-G)
        eE = jnp.exp(Gl - G)
        dec = jnp.exp(Gl)
        gnw_v = gnw_ref[...]
        last = lax.broadcasted_iota(jnp.int32, (CH, 1), 0) == CH - 1
        dgnw_acc = jnp.zeros((1, DV), F32)
        dG_parts = []
        for h in range(HEADS):
            ks = slice(h * DK, (h + 1) * DK)
            vs = slice(h * DV, (h + 1) * DV)
            oh = o_ref[:, vs]
            rv = r_all[:, vs]
            dg = dog_ref[:, vs]
            sig = 1.0 / (1.0 + jnp.exp(-rv))
            gate = rv * sig
            rs = lax.rsqrt(jnp.mean(oh * oh, axis=-1, keepdims=True) + EPS)
            ohat = oh * rs
            dr_ref[:, vs] = ((dg * (ohat * gnw_v)) * (sig * (1.0 + rv * (1.0 - sig)))).astype(BF16)
            don = dg * gate
            dgnw_acc = dgnw_acc + jnp.sum(don * ohat, axis=0, keepdims=True)
            gxn = don * gnw_v
            do = (rs * (gxn - ohat * jnp.mean(gxn * ohat, axis=-1, keepdims=True))).astype(BF16)
            kh = k_all[:, ks]
            vh = v_all[:, vs].astype(BF16)
            qd_f = (q_all[:, ks] * scale) * eG[:, ks]
            ki_f = kh * eN[:, ks]
            ke_f = kh * eE[:, ks]
            qd, ki, ke = qd_f.astype(BF16), ki_f.astype(BF16), ke_f.astype(BF16)
            spt = sp_ref[0, h]
            dst = ds_ref[h]
            dst_b = dst.astype(BF16)
            a_t = jnp.where(upper, _dot_nt(ki, qd), 0.0).astype(BF16)
            da = jnp.where(lower, _dot_nt(do, vh), 0.0).astype(BF16)
            da_t = jnp.where(upper, _dot_nt(vh, do), 0.0).astype(BF16)
            dv_ref[:, vs] = (_dot(a_t, do) + _dot_nt(ke, dst_b)).astype(BF16)
            dqd = _dot(da, ki) + _dot(do, spt.astype(BF16))
            dki = _dot(da_t, qd)
            dke = _dot(vh, dst_b)
            ddec = jnp.sum(spt * dst, axis=0, keepdims=True)
            ds_ref[h] = dst * dec[:, ks] + _dot_tn(do, qd)
            dq_ref[:, ks] = ((dqd * eG[:, ks]) * scale).astype(BF16)
            dk_ref[:, ks] = (dki * eN[:, ks] + dke * eE[:, ks]).astype(BF16)
            dke_ke = dke * ke_f
            dG = dqd * qd_f - dki * ki_f - dke_ke
            dGl = jnp.sum(dke_ke, axis=0, keepdims=True) + ddec * dec[:, ks]
            dG_parts.append(dG + jnp.where(last, dGl, 0.0))
        dgnw_ref[...] += dgnw_acc
        dG_all = jnp.concatenate(dG_parts, axis=1)
        dlogg = jnp.where(valid, _tri_sum(upper.astype(BF16), dG_all), 0.0)
        dg_raw = (dlogg * (1.0 / TAU)) * (1.0 / (1.0 + jnp.exp(g_raw)))
        dgb_ref[...] += jnp.sum(dg_raw, axis=0, keepdims=True)
        dg_b = dg_raw.astype(BF16)
        dgw2_ref[...] += _dot_tn(glr_v.astype(BF16), dg_b)
        dglr_ref[...] = _dot_nt(dg_b, gw2_b).astype(BF16)

    def back(s):
        return NCH - 1 - s

    rk = pl.BlockSpec((CH, KW), lambda s: (back(s), 0))
    rv_ = pl.BlockSpec((CH, GW), lambda s: (back(s), 0))
    rg = pl.BlockSpec((CH, 128), lambda s: (back(s), 0))

    def full(shape):
        return pl.BlockSpec(shape, lambda s: tuple(0 for _ in shape))

    return pl.pallas_call(
        body, name="gla_bwd", grid=(NCH,),
        in_specs=[rv_, rv_] + _p_specs(back) + [full((128, KW)), full((1, KW)), full((1, DV)),
                  pl.BlockSpec((1, HEADS, DV, DK), lambda s: (back(s), 0, 0, 0)), ANY],
        out_specs=[rk, rk, rv_, rv_, rg, full((128, KW)), full((1, KW)), full((1, DV))],
        out_shape=[_sds((TP, KW), BF16), _sds((TP, KW), BF16), _sds((TP, GW), BF16), _sds((TP, GW), BF16),
                   _sds((TP, 128), BF16), _sds((128, KW), F32), _sds((1, KW), F32), _sds((1, DV), F32)],
        scratch_shapes=[pltpu.VMEM((HEADS, DV, DK), F32)],
        compiler_params=_cp(("arbitrary",)),
    )(dog, o, *([P] * 8), gw2, gb, gnw, sp, _dep(dep))


POOL_TR = 128
HALO = 16


def _pool_counts(base, nrows):
    rows = base + lax.broadcasted_iota(jnp.int32, (nrows, 1), 0)
    valid = (rows >= ROW_LO) & (rows < ROW_HI)
    t1 = (rows - ROW_LO + 1).astype(F32)
    cnts = [jnp.clip(t1, 1.0, float(w)) for w in WINDOWS]
    return valid, cnts


def _pool_fwd(P, pw, ps, dep=None):
    def body(cur_ref, prev_ref, pw_ref, ps_ref, dep_ref, y_ref, op_ref):
        i = pl.program_id(0)
        cur = cur_ref[0]
        full = jnp.concatenate([prev_ref[0], cur], axis=0)
        s2 = full + pltpu.roll(full, 1, 0)
        s4 = s2 + pltpu.roll(s2, 2, 0)
        s8 = s4 + pltpu.roll(s4, 4, 0)
        s16 = s8 + pltpu.roll(s8, 8, 0)
        valid, cnts = _pool_counts(i * POOL_TR, POOL_TR)
        for g, s in enumerate((s2, s4, s8, s16)):
            cs = slice(g * GC, (g + 1) * GC)
            y = s[HALO:, cs] / cnts[g] - cur[:, cs]
            yb = jnp.where(valid, y, 0.0).astype(BF16)
            y_ref[:, cs] = yb
            op_ref[:, cs] = (_dot(yb, pw_ref[g].astype(BF16)) * ps_ref[:, cs]).astype(BF16)

    row = pl.BlockSpec((POOL_TR, PW), lambda i: (i, 0))
    per = POOL_TR // HALO
    return pl.pallas_call(
        body, name="pool_fwd", grid=(TP // POOL_TR,),
        in_specs=[pl.BlockSpec((1, POOL_TR, PW), lambda i: (3, i, 0)),
                  pl.BlockSpec((1, HALO, PW), lambda i: (3, jnp.maximum(i * per - 1, 0), 0)),
                  pl.BlockSpec((4, GC, GC), lambda i: (0, 0, 0)), pl.BlockSpec((1, PW), lambda i: (0, 0)), ANY],
        out_specs=[row, row],
        out_shape=[_sds((TP, PW), BF16), _sds((TP, PW), BF16)],
        compiler_params=_cp(("arbitrary",)),
    )(P, P, pw, ps, _dep(dep))


def _pool_bwd(dop, y, pw, ps, dep=None):
    nblk = TP // HALO

    def body(cur_ref, nxt_ref, y_ref, pw_ref, ps_ref, dep_ref, dpu_ref, dpw_ref, dps_ref):
        i = pl.program_id(0)

        @pl.when(i == 0)
        def _():
            dpw_ref[...] = jnp.zeros_like(dpw_ref)
            dps_ref[...] = jnp.zeros_like(dps_ref)

        n_all = POOL_TR + HALO
        dcur = cur_ref[...]
        dall = jnp.concatenate([dcur, nxt_ref[...]], axis=0)
        valid, cnts = _pool_counts(i * POOL_TR, n_all)
        for g in range(4):
            cs = slice(g * GC, (g + 1) * GC)
            pwb = pw_ref[g].astype(BF16)
            yb = y_ref[:, cs]
            dyw = (dall[:, cs] * ps_ref[:, cs]).astype(BF16)
            dps_ref[:, cs] += jnp.sum(dcur[:, cs] * _dot(yb, pwb), axis=0, keepdims=True)
            dpw_ref[g] += _dot_tn(yb, dyw[0:POOL_TR, :])
            dyv = jnp.where(valid, _dot_nt(dyw, pwb), 0.0)
            e = dyv / cnts[g]
            w = WINDOWS[g]
            sh = 1
            while sh < w:
                e = e + pltpu.roll(e, n_all - sh, 0)
                sh *= 2
            dpu_ref[:, cs] = (e[0:POOL_TR, :] - dyv[0:POOL_TR, :]).astype(BF16)

    row = pl.BlockSpec((POOL_TR, PW), lambda i: (i, 0))
    per = POOL_TR // HALO
    return pl.pallas_call(
        body, name="pool_bwd", grid=(TP // POOL_TR,),
        in_specs=[pl.BlockSpec((POOL_TR, PW), lambda i: (i, 1)),
                  pl.BlockSpec((HALO, PW), lambda i: (jnp.minimum(i * per + per, nblk - 1), 1)),
                  row, pl.BlockSpec((4, GC, GC), lambda i: (0, 0, 0)), pl.BlockSpec((1, PW), lambda i: (0, 0)), ANY],
        out_specs=[row, pl.BlockSpec((4, GC, GC), lambda i: (0, 0, 0)), pl.BlockSpec((1, PW), lambda i: (0, 0))],
        out_shape=[_sds((TP, PW), BF16), _sds((4, GC, GC), F32), _sds((1, PW), F32)],
        compiler_params=_cp(("arbitrary",)),
    )(dop, dop, y, pw, ps, _dep(dep))


def _place():
    x, y, c = lax.axis_index("x"), lax.axis_index("y"), lax.axis_index("c")
    chips = [(1 - x, y), (x, 1 - y), (1 - x, 1 - y)]
    return x, y, c, chips


HBM = pl.BlockSpec(memory_space=pltpu.HBM)
SEM = pl.BlockSpec(memory_space=pltpu.SEMAPHORE)
EFFECT = pltpu.SideEffectType.DATAFLOW_SIDE_EFFECTING


def _cast_into(w, place, cols_out, name, dep=None):
    rows, cols = w.shape
    tr = 256

    def body(p_ref, w_ref, dep_ref, o_ref):
        if cols_out != cols:
            o_ref[0] = jnp.zeros((tr, cols_out), BF16)
            o_ref[0, :, 0:cols] = w_ref[...].astype(BF16)
        else:
            o_ref[0] = w_ref[...].astype(BF16)

    grid_spec = pltpu.PrefetchScalarGridSpec(
        num_scalar_prefetch=1, grid=(rows // tr,),
        in_specs=[pl.BlockSpec((tr, cols), lambda i, p: (i, 0)), ANY],
        out_specs=pl.BlockSpec((1, tr, cols_out), lambda i, p: (p[0], i, 0)))
    return pl.pallas_call(
        body, name=name, grid_spec=grid_spec,
        out_shape=_sds((N_CHIP, rows, cols_out), BF16),
        compiler_params=_cp(("arbitrary",)),
    )(place, w, _dep(dep))


def _cast_win(w, place, dep=None):
    rows, cols = w.shape
    tr = 256

    def body(p_ref, w_ref, dep_ref, o_ref, t_ref):
        t_ref[...] = jnp.zeros_like(t_ref)
        t_ref[:, 0:cols] = w_ref[...]
        t = t_ref[...]
        lane = lax.broadcasted_iota(jnp.int32, (tr, PAD_IN), 1)
        for kk in range(N_CHIP):
            @pl.when(p_ref[0] == kk)
            def _(kk=kk):
                if kk == 0:
                    placed = t
                elif kk < 3:
                    placed = pltpu.roll(t, 4 * kk, 1)
                else:
                    pool = pltpu.roll(t, PAD_IN - 4, 1)
                    gate = pltpu.roll(t, GW + 12, 1)
                    placed = jnp.where(lane < GW, pool, jnp.where((lane >= GW + 12) & (lane < GW + 16), gate, 0.0))
                o_ref[0] = placed.astype(BF16)

    grid_spec = pltpu.PrefetchScalarGridSpec(
        num_scalar_prefetch=1, grid=(rows // tr,),
        in_specs=[pl.BlockSpec((tr, cols), lambda i, p: (i, 0)), ANY],
        out_specs=pl.BlockSpec((1, tr, PAD_IN), lambda i, p: (p[0], i, 0)),
        scratch_shapes=[pltpu.VMEM((tr, PAD_IN), F32)])
    return pl.pallas_call(
        body, name="cast_win", grid_spec=grid_spec,
        out_shape=_sds((N_CHIP, rows, PAD_IN), BF16),
        compiler_params=_cp(("arbitrary",)),
    )(place, w, _dep(dep))


def _half_rows(ref, k, which):
    h = ref.shape[1] // 2
    return ref.at[k, pl.ds(pl.multiple_of(which * h, 8), h), :]


def _sent_rows(ref, k, which, whole):
    return ref.at[k] if whole else _half_rows(ref, k, which)


def _gather_start(ws, name, whole=None):
    n = len(ws)
    whole = whole or [False] * n

    def body(*refs):
        ins = refs[:n]
        ssems = refs[n:2 * n]
        rsems = refs[2 * n:3 * n]
        token = refs[4 * n]
        x, y, c, chips = _place()
        me = 2 * x + y
        for w in range(n):
            blk = _sent_rows(ins[w], me, c, whole[w])
            for j, chip in enumerate(chips):
                pltpu.make_async_remote_copy(src_ref=blk, dst_ref=blk, send_sem=ssems[w].at[j], recv_sem=rsems[w].at[j],
                                             device_id=(*chip, c), device_id_type=MESH).start()
        token[...] = jnp.zeros_like(token)

    sem3 = pltpu.SemaphoreType.DMA((3,))
    outs = pl.pallas_call(
        body, name=name,
        out_shape=tuple([sem3] * (2 * n) + [pltpu.HBM(w.shape, w.dtype) for w in ws] + [_sds((8, 128), F32)]),
        in_specs=(HBM,) * n, out_specs=(SEM,) * (2 * n) + (HBM,) * n + (VMEM_FULL,),
        input_output_aliases={w: 2 * n + w for w in range(n)},
        compiler_params=pltpu.CompilerParams(has_side_effects=EFFECT),
    )(*[pltpu.with_memory_space_constraint(w, pltpu.HBM) for w in ws])
    return outs[:n], outs[n:2 * n], outs[2 * n:3 * n], outs[3 * n]


def _gather_wait(w, ssem, rsem, after, name, whole=False):
    def body(w_ref, ssem_ref, rsem_ref, after_ref, out_ref):
        x, y, c, chips = _place()
        me = 2 * x + y
        mine = _sent_rows(w_ref, me, c, whole)
        for j, (cx, cy) in enumerate(chips):
            cp = pltpu.make_async_remote_copy(src_ref=mine, dst_ref=_sent_rows(w_ref, 2 * cx + cy, c, whole),
                                              send_sem=ssem_ref.at[j], recv_sem=rsem_ref.at[j],
                                              device_id=(cx, cy, c), device_id_type=MESH)
            cp.wait_send()
            cp.wait_recv()

    return pl.pallas_call(
        body, name=name, out_shape=pltpu.HBM(w.shape, w.dtype),
        in_specs=(HBM, SEM, SEM, ANY), out_specs=HBM, input_output_aliases={0: 0},
        compiler_params=pltpu.CompilerParams(has_side_effects=EFFECT),
    )(w, ssem, rsem, after)


def _gather_copies(ref, kind, ssem, rsem):
    x, y, c, _ = _place()
    xn, yn, sib = (1 - x, y, c), (x, 1 - y, c), (x, y, 1 - c)
    kx, ky, kd = 2 * (1 - x) + y, 2 * x + (1 - y), 2 * (1 - x) + (1 - y)
    half = ref.shape[1] // 2
    quarter = half // 2

    def piece(k, q):
        return ref.at[k, pl.ds(pl.multiple_of(c * half + q * quarter, 8), quarter), :]

    if kind == "d":
        blk = _half_rows(ref, 2 * x + y, c)
        pairs = [(blk, xn), (blk, yn)]
    elif kind == "r":
        pairs = [(piece(ky, 1), xn), (piece(kx, 0), yn)]
    elif kind == "fx":
        pairs = [(_half_rows(ref, kx, c), sib), (_half_rows(ref, ky, c), sib)]
    else:
        pairs = [(_half_rows(ref, kd, c), sib)]
    return [pltpu.make_async_remote_copy(src_ref=blk, dst_ref=blk, send_sem=ssem.at[i], recv_sem=rsem.at[i],
                                         device_id=to, device_id_type=MESH) for i, (blk, to) in enumerate(pairs)]


def _gather_step(name, arrs, waits, starts, sems_in=(), after=()):
    n, nw, ns = len(arrs), len(waits), len(starts)
    after = [a for a in after if a is not None] or [_dep(None)]

    def body(*refs):
        a_in = refs[:n]
        s_in = refs[n:n + 2 * nw]
        outs = refs[n + 2 * nw + len(after):]
        s_out = outs[:2 * ns]
        for i, (ai, kind) in enumerate(waits):
            for cp in _gather_copies(a_in[ai], kind, s_in[2 * i], s_in[2 * i + 1]):
                cp.wait_send()
                cp.wait_recv()
        for i, (ai, kind) in enumerate(starts):
            for cp in _gather_copies(a_in[ai], kind, s_out[2 * i], s_out[2 * i + 1]):
                cp.start()
        if ns:
            token = outs[2 * ns + n]
            token[...] = jnp.zeros_like(token)

    sem2 = pltpu.SemaphoreType.DMA((2,))
    flat_in = [s for pair in sems_in for s in pair]
    arrs = [pltpu.with_memory_space_constraint(a, pltpu.HBM) for a in arrs]
    outs = pl.pallas_call(
        body, name=name,
        out_shape=tuple([sem2] * (2 * ns) + [pltpu.HBM(a.shape, a.dtype) for a in arrs]
                        + ([_sds((8, 128), F32)] if ns else [])),
        in_specs=(HBM,) * n + (SEM,) * (2 * nw) + (ANY,) * len(after),
        out_specs=(SEM,) * (2 * ns) + (HBM,) * n + ((VMEM_FULL,) if ns else ()),
        input_output_aliases={i: 2 * ns + i for i in range(n)},
        compiler_params=pltpu.CompilerParams(has_side_effects=EFFECT),
    )(*arrs, *flat_in, *after)
    sems = [(outs[2 * i], outs[2 * i + 1]) for i in range(ns)]
    return sems, list(outs[2 * ns:2 * ns + n]), (outs[2 * ns + n] if ns else None)


def _rs_start(sb, name, after=None):
    _, half, cols = sb.shape

    def body(sb_ref, land_ref, after_ref, ssem, rsem, sb_out, land_out, token):
        x, y, c, chips = _place()
        for j, (cx, cy) in enumerate(chips):
            pltpu.make_async_remote_copy(src_ref=sb_ref.at[2 * cx + cy], dst_ref=land_ref.at[j], send_sem=ssem.at[j],
                                         recv_sem=rsem.at[j], device_id=(cx, cy, c), device_id_type=MESH).start()
        token[...] = jnp.zeros_like(token)

    sem3 = pltpu.SemaphoreType.DMA((3,))
    land = lax.empty((3, half, cols), BF16)
    return pl.pallas_call(
        body, name=name,
        out_shape=(sem3, sem3, pltpu.HBM(sb.shape, sb.dtype), pltpu.HBM(land.shape, land.dtype), _sds((8, 128), F32)),
        in_specs=(HBM, HBM, ANY), out_specs=(SEM, SEM, HBM, HBM, VMEM_FULL), input_output_aliases={0: 2, 1: 3},
        compiler_params=pltpu.CompilerParams(has_side_effects=EFFECT),
    )(pltpu.with_memory_space_constraint(sb, pltpu.HBM), pltpu.with_memory_space_constraint(land, pltpu.HBM), _dep(after))


def _rs_wait(items, after, name):
    n = len(items)

    def body(*refs):
        x, y, c, chips = _place()
        for i in range(n):
            sb_ref, land_ref, ssem_ref, rsem_ref = refs[4 * i:4 * i + 4]
            for j, (cx, cy) in enumerate(chips):
                cp = pltpu.make_async_remote_copy(src_ref=sb_ref.at[2 * cx + cy], dst_ref=land_ref.at[j],
                                                  send_sem=ssem_ref.at[j], recv_sem=rsem_ref.at[j],
                                                  device_id=(cx, cy, c), device_id_type=MESH)
                cp.wait_send()
                cp.wait_recv()

    outs = pl.pallas_call(
        body, name=name,
        out_shape=tuple(pltpu.HBM(a.shape, a.dtype) for it in items for a in it[:2]),
        in_specs=(HBM, HBM, SEM, SEM) * n + (ANY,), out_specs=(HBM,) * (2 * n),
        input_output_aliases={4 * i + k: 2 * i + k for i in range(n) for k in range(2)},
        compiler_params=pltpu.CompilerParams(has_side_effects=EFFECT),
    )(*[a for it in items for a in it], after)
    return [outs[2 * i + 1] for i in range(n)]


def _pair_copy(g_ref, land_ref, ssem, rsem):
    x, y, c, _ = _place()
    h = g_ref.shape[1] // 2
    src = g_ref.at[:, pl.ds(pl.multiple_of((1 - c) * h, 8), h), :]
    return pltpu.make_async_remote_copy(src_ref=src, dst_ref=land_ref, send_sem=ssem.at[0], recv_sem=rsem.at[0],
                                        device_id=(x, y, 1 - c), device_id_type=MESH)


def _pair_start(g, name):
    def body(g_ref, land_ref, ssem, rsem, g_out, land_out, token):
        _pair_copy(g_ref, land_ref, ssem, rsem).start()
        token[...] = jnp.zeros_like(token)

    sem1 = pltpu.SemaphoreType.DMA((1,))
    land = lax.empty((N_CHIP, g.shape[1] // 2, g.shape[2]), F32)
    return pl.pallas_call(
        body, name=name,
        out_shape=(sem1, sem1, pltpu.HBM(g.shape, g.dtype), pltpu.HBM(land.shape, land.dtype), _sds((8, 128), F32)),
        in_specs=(HBM, HBM), out_specs=(SEM, SEM, HBM, HBM, VMEM_FULL), input_output_aliases={0: 2, 1: 3},
        compiler_params=pltpu.CompilerParams(has_side_effects=EFFECT),
    )(pltpu.with_memory_space_constraint(g, pltpu.HBM), pltpu.with_memory_space_constraint(land, pltpu.HBM))


def _pair_wait(g, land, ssem, rsem, after, name):
    def body(g_ref, land_ref, ssem_ref, rsem_ref, after_ref, g_out, land_out):
        cp = _pair_copy(g_ref, land_ref, ssem_ref, rsem_ref)
        cp.wait_send()
        cp.wait_recv()

    return pl.pallas_call(
        body, name=name,
        out_shape=(pltpu.HBM(g.shape, g.dtype), pltpu.HBM(land.shape, land.dtype)),
        in_specs=(HBM, HBM, SEM, SEM, ANY), out_specs=(HBM, HBM), input_output_aliases={0: 0, 1: 1},
        compiler_params=pltpu.CompilerParams(has_side_effects=EFFECT),
    )(g, land, ssem, rsem, after)


def _pair_sum(g, rcv, place, name):
    _, rows, cols = g.shape
    half = rows // 2
    tr = 256
    nt = half // tr

    def body(p_ref, g_ref, r_ref, sb_ref, sf_ref):
        s = pl.program_id(1)
        tot = g_ref[0] + r_ref[0]
        sb_ref[0] = tot.astype(BF16)

        @pl.when(s == p_ref[0])
        def _():
            sf_ref[...] = tot

    grid_spec = pltpu.PrefetchScalarGridSpec(
        num_scalar_prefetch=1, grid=(nt, N_CHIP),
        in_specs=[pl.BlockSpec((1, tr, cols), lambda t, s, p: (s, p[1] * nt + t, 0)),
                  pl.BlockSpec((1, tr, cols), lambda t, s, p: (s, t, 0))],
        out_specs=[pl.BlockSpec((1, tr, cols), lambda t, s, p: (s, t, 0)),
                   pl.BlockSpec((tr, cols), lambda t, s, p: (t, 0))])
    return pl.pallas_call(
        body, name=name, grid_spec=grid_spec,
        out_shape=[_sds((N_CHIP, half, cols), BF16), _sds((half, cols), F32)],
        compiler_params=_cp(("arbitrary", "arbitrary")),
    )(place, g, rcv)


def _final_sum(sf, rb, place, name):
    half, cols = sf.shape
    tr = 256
    nt = half // tr

    def body(p_ref, sf_ref, r_ref, out_ref):
        acc = sf_ref[...]
        for j in range(3):
            acc = acc + r_ref[j].astype(F32)
        out_ref[...] = acc

    grid_spec = pltpu.PrefetchScalarGridSpec(
        num_scalar_prefetch=1, grid=(nt,),
        in_specs=[pl.BlockSpec((tr, cols), lambda t, p: (t, 0)), pl.BlockSpec((3, tr, cols), lambda t, p: (0, t, 0))],
        out_specs=pl.BlockSpec((tr, cols), lambda t, p: (p[1] * nt + t, 0)))
    return pl.pallas_call(
        body, name=name, grid_spec=grid_spec,
        out_shape=_sds((2 * half, cols), F32),
        compiler_params=_cp(("arbitrary",)),
    )(place, sf, rb)


def _half_copy(f_ref, which, ssem, rsem):
    x, y, c, _ = _place()
    h = f_ref.shape[0] // 2
    rows = f_ref.at[pl.ds(pl.multiple_of(which * h, 8), h), :]
    return pltpu.make_async_remote_copy(src_ref=rows, dst_ref=rows, send_sem=ssem.at[0], recv_sem=rsem.at[0],
                                        device_id=(x, y, 1 - c), device_id_type=MESH)


def _half_start(fulls, name, after=None):
    n = len(fulls)

    def body(*refs):
        for i in range(n):
            _half_copy(refs[i], lax.axis_index("c"), refs[n + 1 + 2 * i], refs[n + 2 + 2 * i]).start()
        token = refs[4 * n + 1]
        token[...] = jnp.zeros_like(token)

    sem1 = pltpu.SemaphoreType.DMA((1,))
    outs = pl.pallas_call(
        body, name=name,
        out_shape=tuple([sem1] * (2 * n) + [pltpu.HBM(f.shape, f.dtype) for f in fulls] + [_sds((8, 128), F32)]),
        in_specs=(HBM,) * n + (ANY,), out_specs=(SEM,) * (2 * n) + (HBM,) * n + (VMEM_FULL,),
        input_output_aliases={i: 2 * n + i for i in range(n)},
        compiler_params=pltpu.CompilerParams(has_side_effects=EFFECT),
    )(*[pltpu.with_memory_space_constraint(f, pltpu.HBM) for f in fulls], _dep(after))
    return [(outs[2 * i], outs[2 * i + 1], outs[2 * n + i]) for i in range(n)], outs[3 * n]


def _half_wait(items, after, name):
    n = len(items)

    def body(*refs):
        c = lax.axis_index("c")
        for i in range(n):
            ssem_ref, rsem_ref, f_ref = refs[3 * i:3 * i + 3]
            _half_copy(f_ref, c, ssem_ref, rsem_ref).wait_send()
            _half_copy(f_ref, 1 - c, ssem_ref, rsem_ref).wait_recv()

    return pl.pallas_call(
        body, name=name, out_shape=tuple(pltpu.HBM(it[2].shape, it[2].dtype) for it in items),
        in_specs=(SEM, SEM, HBM) * n + (ANY,) * len(after), out_specs=(HBM,) * n,
        input_output_aliases={3 * i + 2: i for i in range(n)},
        compiler_params=pltpu.CompilerParams(has_side_effects=EFFECT),
    )(*[a for it in items for a in it], *after)


def _small_copies(src_ref, land_ref, ssem, rsem, first):
    x, y, c, chips = _place()
    if first:
        return [pltpu.make_async_remote_copy(src_ref=src_ref, dst_ref=land_ref, send_sem=ssem.at[0], recv_sem=rsem.at[0],
                                             device_id=(x, y, 1 - c), device_id_type=MESH)]
    return [pltpu.make_async_remote_copy(src_ref=src_ref, dst_ref=land_ref.at[j], send_sem=ssem.at[j], recv_sem=rsem.at[j],
                                         device_id=(*chip, c), device_id_type=MESH) for j, chip in enumerate(chips)]


def _small_start(src, first, name, after=None):
    n = 1 if first else 3

    def body(src_ref, land_ref, after_ref, ssem, rsem, src_out, land_out, token):
        for cp in _small_copies(src_ref, land_ref, ssem, rsem, first):
            cp.start()
        token[...] = jnp.zeros_like(token)

    sems = pltpu.SemaphoreType.DMA((n,))
    land = lax.empty(src.shape if first else (3,) + src.shape, F32)
    return pl.pallas_call(
        body, name=name,
        out_shape=(sems, sems, pltpu.HBM(src.shape, F32), pltpu.HBM(land.shape, F32), _sds((8, 128), F32)),
        in_specs=(HBM, HBM, ANY), out_specs=(SEM, SEM, HBM, HBM, VMEM_FULL), input_output_aliases={0: 2, 1: 3},
        compiler_params=pltpu.CompilerParams(has_side_effects=EFFECT),
    )(pltpu.with_memory_space_constraint(src, pltpu.HBM), pltpu.with_memory_space_constraint(land, pltpu.HBM), _dep(after))


def _small_wait(src, land, ssem, rsem, first, after, name):
    def body(src_ref, land_ref, ssem_ref, rsem_ref, after_ref, src_out, land_out):
        for cp in _small_copies(src_ref, land_ref, ssem_ref, rsem_ref, first):
            cp.wait_send()
            cp.wait_recv()

    return pl.pallas_call(
        body, name=name,
        out_shape=(pltpu.HBM(src.shape, F32), pltpu.HBM(land.shape, F32)),
        in_specs=(HBM, HBM, SEM, SEM, ANY), out_specs=(HBM, HBM), input_output_aliases={0: 0, 1: 1},
        compiler_params=pltpu.CompilerParams(has_side_effects=EFFECT),
    )(src, land, ssem, rsem, after)


def _small_pair_sum(vec, got):
    def body(v_ref, g_ref, o_ref):
        o_ref[...] = v_ref[...] + g_ref[...]

    return pl.pallas_call(body, name="small_pair_sum", in_specs=[VMEM_FULL] * 2, out_specs=VMEM_FULL,
                          out_shape=_sds(vec.shape, F32), compiler_params=_cp())(vec, got)


def _small_chip_sum(pair, got, place):
    def body(p_ref, pair_ref, got_ref, o_ref):
        acc = None
        for kk in range(N_CHIP):
            d = jnp.bitwise_xor(p_ref[0], kk)
            t = jnp.where(d == 0, pair_ref[...], jnp.where(d == 2, got_ref[0], jnp.where(d == 1, got_ref[1], got_ref[2])))
            acc = t if acc is None else acc + t
        o_ref[...] = acc

    grid_spec = pltpu.PrefetchScalarGridSpec(
        num_scalar_prefetch=1, grid=(1,),
        in_specs=[pl.BlockSpec(pair.shape, lambda i, p: (0, 0)), pl.BlockSpec(got.shape, lambda i, p: (0, 0, 0))],
        out_specs=pl.BlockSpec(pair.shape, lambda i, p: (0, 0)))
    return pl.pallas_call(body, name="small_chip_sum", grid_spec=grid_spec, out_shape=_sds(pair.shape, F32),
                          compiler_params=_cp(("arbitrary",)))(place, pair, got)


def _adam_math(w, g, m, v):
    m = B1 * m + (1.0 - B1) * g
    v = B2 * v + (1.0 - B2) * (g * g)
    m_hat = m / (1.0 - B1 ** STEP)
    v_hat = v / (1.0 - B2 ** STEP)
    delta = -LR * (m_hat / (jnp.sqrt(v_hat) + AEPS) + WD * w)
    return delta, m, v


def _adam_big(w, g, m, v, name, dep=None):
    rows, cols = w.shape
    tr = 128

    def body(w_ref, g_ref, m_ref, v_ref, dep_ref, go_ref, d_ref, nm_ref, nv_ref):
        g = g_ref[...]
        d, nm, nv = _adam_math(w_ref[...], g, m_ref[...], v_ref[...])
        go_ref[...] = g
        d_ref[...] = d
        nm_ref[...] = nm
        nv_ref[...] = nv

    blk = pl.BlockSpec((tr, cols), lambda i: (i, 0))
    return pl.pallas_call(
        body, name=name, grid=(rows // tr,),
        in_specs=[blk] * 4 + [ANY], out_specs=[blk] * 4, out_shape=[_sds((rows, cols), F32)] * 4,
        compiler_params=_cp(("arbitrary",)),
    )(w, g, m, v, _dep(dep))


def _adam_small(ws, gs, ms, vs, dep=None):
    n = len(ws)

    def body(*refs):
        for i in range(n):
            d, nm, nv = _adam_math(refs[i][...], refs[n + i][...], refs[2 * n + i][...], refs[3 * n + i][...])
            refs[4 * n + 1 + i][...] = d
            refs[5 * n + 1 + i][...] = nm
            refs[6 * n + 1 + i][...] = nv

    shapes = [_sds(w.shape, F32) for w in ws]
    outs = pl.pallas_call(
        body, name="adam_small",
        in_specs=[VMEM_FULL] * (4 * n) + [ANY], out_specs=[VMEM_FULL] * (3 * n), out_shape=shapes * 3,
        compiler_params=_cp(),
    )(*ws, *gs, *ms, *vs, _dep(dep))
    return outs[:n], outs[n:2 * n], outs[2 * n:]


def _pad_rows8(a):
    flat = a.reshape(-1, 128)
    pad = (-flat.shape[0]) % 8
    if pad:
        flat = jnp.concatenate([flat, jnp.zeros((pad, 128), F32)], axis=0)
    return flat


def kernel(x, meta_tokens, norm1_w, w_in, gate_w2, gate_b, gla_norm_w, pool_w, pool_scale, w_out, norm2_w, mlp_w1, mlp_w2, final_norm_w, loss_target, m_meta_tokens, m_norm1_w, m_w_in, m_gate_w2, m_gate_b, m_gla_norm_w, m_pool_w, m_pool_scale, m_w_out, m_norm2_w, m_mlp_w1, m_mlp_w2, m_final_norm_w, v_meta_tokens, v_norm1_w, v_w_in, v_gate_w2, v_gate_b, v_gla_norm_w, v_pool_w, v_pool_scale, v_w_out, v_norm2_w, v_mlp_w1, v_mlp_w2, v_final_norm_w):
    cx, cy, cc = lax.axis_index("x"), lax.axis_index("y"), lax.axis_index("c")
    me = (2 * cx + cy).astype(jnp.int32)

    place = jnp.stack([me, cc.astype(jnp.int32)])
    fw = final_norm_w.reshape(1, D)

    mine = jnp.concatenate([meta_tokens.reshape(64, 128), gate_w2[0], pool_w[0].reshape(512, 128)], axis=0)
    small = lax.dynamic_update_slice(jnp.zeros((N_CHIP, 592, 128), F32), mine[None], (me, 0, 0))
    (s_sm,), (r_sm,), (f_sm,), tok = _gather_start([small], "gather_start_small", [True])
    (sem_win_d,), (win,), tok = _gather_step("gather_start_win", [_cast_win(w_in[0], place, tok)], [], [(0, "d")])
    wout, w1, w2 = (_cast_into(w_out[0], place, D, "cast_wout", tok), _cast_into(mlp_w1[0], place, D, "cast_w1", tok),
                    _cast_into(mlp_w2[0], place, D, "cast_w2", tok))
    small = _gather_wait(f_sm, s_sm, r_sm, w2, "gather_wait_small", True)
    metaF = jnp.concatenate([small[k, 0:64].reshape(N_META, 512) for k in range(N_CHIP)], axis=1)
    gw2F = jnp.concatenate([small[k, 64:80] for k in range(N_CHIP)], axis=1)
    pwF = jnp.concatenate([small[k, 80:592].reshape(4, 64, GC) for k in range(N_CHIP)], axis=1)

    fly = {"win": win, "wout": wout, "w1": w1, "w2": w2}
    sems = {"win_d": sem_win_d}

    def step(name, names, waits, starts, after):
        at = {nm: i for i, nm in enumerate(names)}
        new, arrs, tok = _gather_step(name, [fly[nm] for nm in names], [(at[nm], k) for nm, k in waits],
                                      [(at[nm], k) for nm, k in starts], [sems[nm + "_" + k] for nm, k in waits], after)
        fly.update(zip(names, arrs))
        sems.update({nm + "_" + k: s for (nm, k), s in zip(starts, new)})
        return tok

    def relay_first():
        return step("gather_relay_win", ["win", "wout", "w1"], [("win", "d")],
                    [("win", "r"), ("win", "fx"), ("wout", "d"), ("w1", "d")], [v_w_in[0]])

    def get_win(after):
        tok = step("gather_land_win", ["win"], [("win", "r")], [("win", "fd")], [after])
        step("gather_wait_win", ["win"], [("win", "fx"), ("win", "fd")], [], [tok])
        return fly["win"]

    def relay_mid(after):
        return step("gather_relay_mid", ["wout", "w1", "w2"], [("wout", "d"), ("w1", "d")],
                    [("wout", "r"), ("w1", "r"), ("w2", "d"), ("wout", "fx"), ("w1", "fx")], [after, m_w_in[0]])

    def land_wout(after):
        return step("gather_land_wout", ["wout"], [("wout", "r")], [("wout", "fd")], [after])

    def get_wout(after):
        step("gather_wait_wout", ["wout"], [("wout", "fx"), ("wout", "fd")], [], [after])
        tok = step("gather_land_w1", ["w1"], [("w1", "r")], [("w1", "fd")], [fly["wout"]])
        return fly["wout"].reshape(D, D), tok

    def get_w1(after):
        step("gather_wait_w1", ["w1"], [("w1", "fx"), ("w1", "fd")], [], [after])
        return fly["w1"]

    def relay_last(after):
        return step("gather_relay_w2", ["w2"], [("w2", "d")], [("w2", "r"), ("w2", "fx")], [after])

    def get_w2(after):
        tok = step("gather_land_w2", ["w2"], [("w2", "r")], [("w2", "fd")], [after])
        step("gather_wait_w2", ["w2"], [("w2", "fx"), ("w2", "fd")], [], [tok])
        return fly["w2"].reshape(DFF, D)

    pairs, pending = {}, {}

    halves = {}

    def reduce_(names, after, tag):
        items = [(pending[nm][3], pending[nm][4], pending[nm][1], pending[nm][2]) for nm in names]
        landed = _rs_wait(items, after, "rs_wait_" + tag)
        fulls = [_final_sum(pending[nm][0], rb, place, "final_sum_" + nm) for nm, rb in zip(names, landed)]
        sent, token = _half_start(fulls, "half_start_" + tag)
        halves.update(zip(names, sent))
        return token

    def grad_start(nm, g):
        ssem, rsem, g_thru, land, token = _pair_start(g, "pair_start_" + nm)
        pairs[nm] = (ssem, rsem, g_thru, land)
        if nm == "win":
            token = reduce_(["w2", "w1", "wout"], token, "mlp_wout")
        return token

    def grad_finish(nm, after):
        ssem, rsem, g_thru, land = pairs[nm]
        g, rcv = _pair_wait(g_thru, land, ssem, rsem, after, "pair_wait_" + nm)
        sb, sf = _pair_sum(g, rcv, place, "pair_sum_" + nm)
        ssem, rsem, sb_thru, land, token = _rs_start(sb, "rs_start_" + nm)
        pending[nm] = (sf, ssem, rsem, sb_thru, land)
        return token

    (grad_x, loss8, d_n1w, d_gb, d_gnw, d_ps, d_n2w, d_fw, d_meta, d_gw2, d_pw) = _local_step(
        x[0], loss_target[0], dict(relay_first=relay_first, win=get_win, relay_mid=relay_mid, land_wout=land_wout,
                                   wout=get_wout, w1=get_w1, relay_last=relay_last, w2=get_w2),
        metaF, gw2F, pwF, norm1_w, gate_b, gla_norm_w, pool_scale, norm2_w, fw, grad_start, grad_finish)
    return _reduce_and_update(
        me, place, pending, halves, reduce_, grad_x, loss8, d_n1w, d_gb, d_gnw, d_ps, d_n2w, d_fw, d_meta, d_gw2, d_pw,
        meta_tokens, norm1_w, w_in, gate_w2, gate_b, gla_norm_w, pool_w, pool_scale, w_out, norm2_w, mlp_w1, mlp_w2, fw,
        m_meta_tokens, m_norm1_w, m_w_in, m_gate_w2, m_gate_b, m_gla_norm_w, m_pool_w, m_pool_scale, m_w_out, m_norm2_w,
        m_mlp_w1, m_mlp_w2, m_final_norm_w, v_meta_tokens, v_norm1_w, v_w_in, v_gate_w2, v_gate_b, v_gla_norm_w, v_pool_w,
        v_pool_scale, v_w_out, v_norm2_w, v_mlp_w1, v_mlp_w2, v_final_norm_w)


def _local_step(x, target, gather, metaF, gw2F, pwF, norm1_w, gate_b, gla_norm_w, pool_scale, norm2_w, fw, grad_start,
                grad_finish):
    h0, u = _embed_norm(x, metaF, norm1_w, gather["relay_first"]())
    Win = gather["win"](u)
    P = _in_proj(u, Win)
    gw2p = jnp.pad(gw2F, ((0, 128 - RANK), (0, 0)))
    yb, op = _pool_fwd(P, pwF, pool_scale, gather["relay_mid"](P))
    o, og, sp = _gla_fwd(P, gw2p, gate_b, gla_norm_w, gather["land_wout"](op))
    Wout, tok = gather["wout"](og)
    h1 = _out_proj(og, op, Wout, h0, tok)
    n2 = _norm_rows(h1, norm2_w, "norm2")
    W1 = gather["w1"](n2)
    zr, a, tok = _mlp_up(n2, W1, 0)
    zr, a, _ = _mlp_up(n2, W1, 1, (zr, a), gather["relay_last"](tok))
    W2 = gather["w2"](a)
    h2 = _mlp_down(a, W2, h1)

    dh2, dh2b, d_fw, loss8 = _loss_head(h2, target, fw)
    tok = grad_start("w2", _grad_w2(a, dh2b).reshape(N_CHIP, D, D))
    dz = _mlp_dz(dh2b, W2, zr, tok)
    tok = grad_finish("w2", dz)
    tok = grad_start("w1", _grad_w1(n2, dz, tok))
    dn2 = _mlp_dn(dz, W1, tok)
    tok = grad_finish("w1", dn2)
    dh1, dh1b, d_n2w = _norm_bwd(dn2, h1, dh2, norm2_w, "norm2_bwd", tok)
    dmixed = _mixed_grad(dh1b, Wout)
    tok = grad_start("wout", _grad_wout(og, op, dh1b))
    dpu, d_pw, d_ps = _pool_bwd(dmixed, yb, pwF, pool_scale, tok)
    dq, dk, dv, dr, dglr, d_gw2p, d_gb, d_gnw = _gla_bwd(dmixed, o, P, gw2p, gate_b, gla_norm_w, sp, tok)
    d_gw2 = d_gw2p[0:RANK]
    tok = grad_finish("wout", dq)
    tok = grad_start("win", _grad_win(u, dq, dk, dv, dr, dglr, dpu, tok))
    du = _in_grad(dq, dk, dv, dr, dglr, dpu, Win, tok)
    tok = grad_finish("win", du)
    grad_x, d_meta, d_n1w = _input_grad(du, h0, dh1, norm1_w, tok)
    return grad_x, loss8, d_n1w, d_gb, d_gnw, d_ps, d_n2w, d_fw, d_meta, d_gw2, d_pw


def _reduce_and_update(me, place, pending, halves, reduce_, grad_x, loss8, d_n1w, d_gb, d_gnw, d_ps, d_n2w, d_fw, d_meta, d_gw2,
                       d_pw,
                       meta_tokens, norm1_w, w_in, gate_w2, gate_b, gla_norm_w, pool_w, pool_scale, w_out, norm2_w,
                       mlp_w1, mlp_w2, fw, m_meta_tokens, m_norm1_w, m_w_in, m_gate_w2, m_gate_b, m_gla_norm_w, m_pool_w,
                       m_pool_scale, m_w_out, m_norm2_w, m_mlp_w1, m_mlp_w2, m_final_norm_w, v_meta_tokens, v_norm1_w, v_w_in,
                       v_gate_w2, v_gate_b, v_gla_norm_w, v_pool_w, v_pool_scale, v_w_out, v_norm2_w, v_mlp_w1, v_mlp_w2,
                       v_final_norm_w):
    parts = [loss8, d_n1w, d_gb, d_gnw, d_ps, d_n2w, d_fw, d_meta, d_gw2, d_pw]
    packed = [_pad_rows8(p) for p in parts]
    sizes = [p.shape[0] for p in packed]
    vec = jnp.concatenate(packed, axis=0)

    big = {}
    params = {"w2": (mlp_w2[0], m_mlp_w2[0], v_mlp_w2[0]), "w1": (mlp_w1[0], m_mlp_w1[0], v_mlp_w1[0]),
              "wout": (w_out[0], m_w_out[0], v_w_out[0]), "win": (w_in[0], m_w_in[0], v_w_in[0])}

    def update(names, after, tag):
        fulls = _half_wait([halves[nm] for nm in names], [after], "half_wait_" + tag)
        tok = None
        for nm, full in zip(names, fulls):
            w, m, v = params[nm]
            big[nm] = _adam_big(w, full, m, v, "adam_" + nm, tok)
            tok = big[nm][3]
        return tok

    s1, r1, vec, land1, tok = _small_start(vec, True, "small_start_pair")
    tok = update(["w2"], tok, "w2")
    vec, got = _small_wait(vec, land1, s1, r1, True, tok, "small_wait_pair")
    pair = _small_pair_sum(vec, got)
    s2, r2, pair, land2, tok = _small_start(pair, False, "small_start_chips")
    tok = reduce_(["win"], tok, "win")
    tok = update(["w1"], tok, "w1")
    tok = update(["win"], tok, "win")
    pair, got = _small_wait(pair, land2, s2, r2, False, tok, "small_wait_chips")
    red = _small_chip_sum(pair, got, place)
    after = update(["wout"], red, "wout")
    offs = [0]
    for s in sizes:
        offs.append(offs[-1] + s)

    def take(i, shape):
        n = 1
        for d in shape:
            n *= d
        return red[offs[i]:offs[i] + n // 128].reshape(shape)

    loss = red[0, 0]
    G_n1w = take(1, (1, D))
    G_gb = take(2, (1, KW))
    G_gnw = take(3, (1, DV))
    G_ps = take(4, (1, PW))
    G_n2w = take(5, (1, D))
    G_fw = take(6, (1, D))
    G_meta = lax.dynamic_slice(take(7, (N_META, D)), (0, me * 512), (N_META, 512))
    G_gw2 = lax.dynamic_slice(take(8, (RANK, KW)), (0, me * 128), (RANK, 128))
    G_pw = lax.dynamic_slice(take(9, (4, GC, GC)), (0, me * 64, 0), (4, 64, GC))

    G_win, d_win, nm_win, nv_win = big["win"]
    G_wout, d_wout, nm_wout, nv_wout = big["wout"]
    G_w1, d_w1, nm_w1, nv_w1 = big["w1"]
    G_w2, d_w2, nm_w2, nv_w2 = big["w2"]
    ws = [meta_tokens, norm1_w, gate_w2[0], gate_b, gla_norm_w, pool_w[0], pool_scale, norm2_w, fw]
    gs = [G_meta, G_n1w, G_gw2, G_gb, G_gnw, G_pw, G_ps, G_n2w, G_fw]
    ms = [m_meta_tokens, m_norm1_w, m_gate_w2[0], m_gate_b, m_gla_norm_w, m_pool_w[0], m_pool_scale, m_norm2_w,
          m_final_norm_w.reshape(1, D)]
    vs = [v_meta_tokens, v_norm1_w, v_gate_w2[0], v_gate_b, v_gla_norm_w, v_pool_w[0], v_pool_scale, v_norm2_w,
          v_final_norm_w.reshape(1, D)]
    ds, nms, nvs = _adam_small(ws, gs, ms, vs, after)

    def assemble(small, win_, wout_, w1_, w2_):
        meta_, n1_, gw2_, gb_, gnw_, pw_, ps_, n2_, fw_ = small
        return (meta_, n1_, win_[None], gw2_[None], gb_, gnw_, pw_[None], ps_, wout_[None], n2_, w1_[None], w2_[None],
                fw_.reshape(D))

    grads_out = assemble(gs, G_win, G_wout, G_w1, G_w2)
    deltas = assemble(ds, d_win, d_wout, d_w1, d_w2)
    new_m = assemble(nms, nm_win, nm_wout, nm_w1, nm_w2)
    new_v = assemble(nvs, nv_win, nv_wout, nv_w1, nv_w2)
    return (loss, grad_x[None], *grads_out, *deltas, *new_m, *new_v)
```

```python
import functools

import jax
import jax.numpy as jnp
from jax import lax
from jax.experimental import pallas as pl
from jax.experimental.pallas import tpu as pltpu

F32 = jnp.float32
BF16 = jnp.bfloat16

D = 2048
SEQ = 2048
N_META = 16
CH = 64
TP = 2176
NCH = TP // CH
ROW_LO = 112
X_LO = 128
ROW_HI = TP
XT = 128
NXT = TP // XT
HEADS = 4
DK = 128
DV = 256
KW = HEADS * DK
GW = HEADS * DV
RANK = 16
TAU = 16.0
WINDOWS = (2, 4, 8, 16)
PW = 1024
GC = 256
DFF = 8192
EPS = 1e-6
SHARD_IN = 1028
PAD_IN = 1152
N_CHIP = 4

LR = 0.001
B1 = 0.9
B2 = 0.999
AEPS = 1e-08
WD = 0.01
STEP = 10

VMEM_LIMIT = 60 * 1024 * 1024
ANY = pl.BlockSpec(memory_space=pl.ANY)
VMEM_FULL = pl.BlockSpec(memory_space=pltpu.VMEM)
MESH = pl.DeviceIdType.MESH


def _cp(sem=None):
    if sem is None:
        return pltpu.CompilerParams(vmem_limit_bytes=VMEM_LIMIT)
    return pltpu.CompilerParams(dimension_semantics=sem, vmem_limit_bytes=VMEM_LIMIT)


def _dot(a, b):
    return jnp.dot(a, b, preferred_element_type=F32)


def _dot_nt(a, b):
    return lax.dot_general(a, b, (((1,), (1,)), ((), ())), preferred_element_type=F32)


def _dot_tn(a, b):
    return lax.dot_general(a, b, (((0,), (0,)), ((), ())), preferred_element_type=F32)


def _sds(shape, dtype):
    return jax.ShapeDtypeStruct(shape, dtype)


def _embed_norm(x, meta_full, w, dep=None):
    def body(x_ref, meta_ref, w_ref, dep_ref, h_ref, u_ref):
        i = pl.program_id(0)

        @pl.when(i == 0)
        def _():
            h_ref[...] = jnp.zeros_like(h_ref)
            h_ref[ROW_LO:X_LO, :] = meta_ref[...]

        @pl.when(i >= 1)
        def _():
            h_ref[...] = x_ref[...]

        h = h_ref[...]
        r = lax.rsqrt(jnp.mean(h * h, axis=-1, keepdims=True) + EPS)
        u_ref[...] = ((h * r) * w_ref[...]).astype(BF16)

    return pl.pallas_call(
        body, name="embed_norm1", grid=(NXT,),
        in_specs=[pl.BlockSpec((XT, D), lambda i: (jnp.maximum(i - 1, 0), 0)),
                  pl.BlockSpec((N_META, D), lambda i: (0, 0)),
                  pl.BlockSpec((1, D), lambda i: (0, 0)), ANY],
        out_specs=[pl.BlockSpec((XT, D), lambda i: (i, 0)), pl.BlockSpec((XT, D), lambda i: (i, 0))],
        out_shape=[_sds((TP, D), F32), _sds((TP, D), BF16)],
        compiler_params=_cp(("arbitrary",)),
    )(x, meta_full, w, _dep(dep))


def _norm_rows(h, w, name):
    tr = 272

    def body(h_ref, w_ref, o_ref):
        hv = h_ref[...]
        r = lax.rsqrt(jnp.mean(hv * hv, axis=-1, keepdims=True) + EPS)
        o_ref[...] = ((hv * r) * w_ref[...]).astype(BF16)

    return pl.pallas_call(
        body, name=name, grid=(TP // tr,),
        in_specs=[pl.BlockSpec((tr, D), lambda i: (i, 0)), pl.BlockSpec((1, D), lambda i: (0, 0))],
        out_specs=pl.BlockSpec((tr, D), lambda i: (i, 0)),
        out_shape=_sds((TP, D), BF16),
        compiler_params=_cp(("arbitrary",)),
    )(h, w)


def _loss_head(h2, target, fw):
    def body(h_ref, t_ref, w_ref, dh_ref, dhb_ref, dw_ref, loss_ref):
        i = pl.program_id(0)

        @pl.when(i == 0)
        def _():
            dw_ref[...] = jnp.zeros_like(dw_ref)
            loss_ref[...] = jnp.zeros_like(loss_ref)

        h = h_ref[...]
        w = w_ref[...]
        r = lax.rsqrt(jnp.mean(h * h, axis=-1, keepdims=True) + EPS)
        xh = h * r
        y = xh * w
        is_x = (i >= 1).astype(F32)
        diff = (y - t_ref[...]) * is_x
        loss_ref[...] += jnp.sum(diff * diff) * (0.5 / D)
        dy = diff * (1.0 / D)
        dw_ref[...] += jnp.sum(dy * xh, axis=0, keepdims=True)
        gx = dy * w
        dh = r * (gx - xh * jnp.mean(gx * xh, axis=-1, keepdims=True))
        dh_ref[...] = dh
        dhb_ref[...] = dh.astype(BF16)

    return pl.pallas_call(
        body, name="loss_head", grid=(NXT,),
        in_specs=[pl.BlockSpec((XT, D), lambda i: (i, 0)),
                  pl.BlockSpec((XT, D), lambda i: (jnp.maximum(i - 1, 0), 0)),
                  pl.BlockSpec((1, D), lambda i: (0, 0))],
        out_specs=[pl.BlockSpec((XT, D), lambda i: (i, 0)), pl.BlockSpec((XT, D), lambda i: (i, 0)),
                   pl.BlockSpec((1, D), lambda i: (0, 0)), pl.BlockSpec((8, 128), lambda i: (0, 0))],
        out_shape=[_sds((TP, D), F32), _sds((TP, D), BF16), _sds((1, D), F32), _sds((8, 128), F32)],
        compiler_params=_cp(("arbitrary",)),
    )(h2, target, fw)


def _norm_bwd(dn, h, dres, w, name, dep=None):
    tr = 272

    def body(dn_ref, h_ref, dres_ref, w_ref, dep_ref, o_ref, ob_ref, dw_ref):
        @pl.when(pl.program_id(0) == 0)
        def _():
            dw_ref[...] = jnp.zeros_like(dw_ref)

        hv = h_ref[...]
        dnv = dn_ref[...]
        r = lax.rsqrt(jnp.mean(hv * hv, axis=-1, keepdims=True) + EPS)
        xh = hv * r
        dw_ref[...] += jnp.sum(dnv * xh, axis=0, keepdims=True)
        gx = dnv * w_ref[...]
        dh = dres_ref[...] + r * (gx - xh * jnp.mean(gx * xh, axis=-1, keepdims=True))
        o_ref[...] = dh
        ob_ref[...] = dh.astype(BF16)

    row = pl.BlockSpec((tr, D), lambda i: (i, 0))
    vec = pl.BlockSpec((1, D), lambda i: (0, 0))
    return pl.pallas_call(
        body, name=name, grid=(TP // tr,),
        in_specs=[row, row, row, vec, ANY], out_specs=[row, row, vec],
        out_shape=[_sds((TP, D), F32), _sds((TP, D), BF16), _sds((1, D), F32)],
        compiler_params=_cp(("arbitrary",)),
    )(dn, h, dres, w, _dep(dep))


def _input_grad(du, h0, dh1, w, dep=None):
    def body(du_ref, h_ref, dres_ref, w_ref, dep_ref, gx_ref, gm_ref, dw_ref):
        i = pl.program_id(0)

        @pl.when(i == 0)
        def _():
            dw_ref[...] = jnp.zeros_like(dw_ref)

        hv = h_ref[...]
        dnv = du_ref[...]
        r = lax.rsqrt(jnp.mean(hv * hv, axis=-1, keepdims=True) + EPS)
        xh = hv * r
        dw_ref[...] += jnp.sum(dnv * xh, axis=0, keepdims=True)
        g = dnv * w_ref[...]
        dh = dres_ref[...] + r * (g - xh * jnp.mean(g * xh, axis=-1, keepdims=True))

        @pl.when(i == 0)
        def _():
            gm_ref[...] = dh[ROW_LO:X_LO, :]

        @pl.when(i >= 1)
        def _():
            gx_ref[...] = dh

    row = pl.BlockSpec((XT, D), lambda i: (i, 0))
    vec = pl.BlockSpec((1, D), lambda i: (0, 0))
    return pl.pallas_call(
        body, name="input_grad", grid=(NXT,),
        in_specs=[row, row, row, vec, ANY],
        out_specs=[pl.BlockSpec((XT, D), lambda i: (jnp.maximum(i - 1, 0), 0)),
                   pl.BlockSpec((N_META, D), lambda i: (0, 0)), vec],
        out_shape=[_sds((SEQ, D), F32), _sds((N_META, D), F32), _sds((1, D), F32)],
        compiler_params=_cp(("arbitrary",)),
    )(du, h0, dh1, w, _dep(dep))


def _in_proj(u, wg):
    def body(u_ref, w_ref, o_ref):
        o_ref[0] = _dot(u_ref[...], w_ref[0])

    return pl.pallas_call(
        body, name="in_proj", grid=(N_CHIP,),
        in_specs=[VMEM_FULL, pl.BlockSpec((1, D, PAD_IN), lambda k: (k, 0, 0))],
        out_specs=pl.BlockSpec((1, TP, PAD_IN), lambda k: (k, 0, 0)),
        out_shape=_sds((N_CHIP, TP, PAD_IN), F32),
        compiler_params=_cp(("arbitrary",)),
    )(u, wg)


def _out_proj(og, op, wout, h0, dep=None):
    tn = 512

    def body(og_ref, op_ref, w_ref, h_ref, dep_ref, o_ref):
        acc = _dot(og_ref[...], w_ref[0:GW, :]) + _dot(op_ref[...], w_ref[GW:D, :])
        o_ref[...] = h_ref[...] + acc

    return pl.pallas_call(
        body, name="out_proj", grid=(D // tn,),
        in_specs=[VMEM_FULL, VMEM_FULL, pl.BlockSpec((D, tn), lambda j: (0, j)),
                  pl.BlockSpec((TP, tn), lambda j: (0, j)), ANY],
        out_specs=pl.BlockSpec((TP, tn), lambda j: (0, j)),
        out_shape=_sds((TP, D), F32),
        compiler_params=_cp(("arbitrary",)),
    )(og, op, wout, h0, _dep(dep))


def _mlp_up(n2, w1g, part, prev=None, dep=None):
    tn = 1024
    per = D // tn

    def body(n_ref, w_ref, dep_ref, *rest):
        zr_ref, a_ref, token = rest[-3:]
        z = jnp.maximum(_dot(n_ref[...], w_ref[0]), 0.0)
        zr_ref[...] = z.astype(BF16)
        a_ref[...] = (z * z).astype(BF16)
        token[...] = jnp.zeros_like(token)

    col = pl.BlockSpec((TP, tn), lambda k, j: (0, (2 * part + k) * per + j))
    return pl.pallas_call(
        body, name="mlp_up_%d" % part, grid=(N_CHIP // 2, per),
        in_specs=[VMEM_FULL, pl.BlockSpec((1, D, tn), lambda k, j: (2 * part + k, 0, j)), ANY] + ([ANY, ANY] if prev else []),
        out_specs=[col, col, pl.BlockSpec((8, 128), lambda k, j: (0, 0))],
        out_shape=[_sds((TP, DFF), BF16), _sds((TP, DFF), BF16), _sds((8, 128), F32)],
        input_output_aliases={3: 0, 4: 1} if prev else {},
        compiler_params=_cp(("arbitrary", "arbitrary")),
    )(n2, w1g, _dep(dep), *(prev or ()))


def _mlp_down(a, w2, h1):
    tk = 1024
    nk = DFF // tk

    def body(a_ref, w_ref, h_ref, o_ref, acc_ref):
        k = pl.program_id(0)

        @pl.when(k == 0)
        def _():
            pltpu.sync_copy(h_ref, acc_ref)

        acc_ref[...] += _dot(a_ref[...], w_ref[...])

        @pl.when(k == nk - 1)
        def _():
            pltpu.sync_copy(acc_ref, o_ref)

    return pl.pallas_call(
        body, name="mlp_down", grid=(nk,),
        in_specs=[pl.BlockSpec((TP, tk), lambda k: (0, k)), pl.BlockSpec((tk, D), lambda k: (k, 0)), ANY],
        out_specs=ANY,
        out_shape=_sds((TP, D), F32),
        scratch_shapes=[pltpu.VMEM((TP, D), F32)],
        compiler_params=_cp(("arbitrary",)),
    )(a, w2, h1)


def _mlp_dz(dh2b, w2, zr, dep=None):
    tn = 1024

    def body(d_ref, w_ref, z_ref, dep_ref, o_ref):
        da = _dot_nt(d_ref[...], w_ref[...])
        o_ref[...] = (da * (2.0 * z_ref[...].astype(F32))).astype(BF16)

    col = pl.BlockSpec((TP, tn), lambda j: (0, j))
    return pl.pallas_call(
        body, name="mlp_dz", grid=(DFF // tn,),
        in_specs=[VMEM_FULL, pl.BlockSpec((tn, D), lambda j: (j, 0)), col, ANY],
        out_specs=col,
        out_shape=_sds((TP, DFF), BF16),
        compiler_params=_cp(("arbitrary",)),
    )(dh2b, w2, zr, _dep(dep))


def _grad_w2(a, dh2b):
    tm = 1024

    def body(a_ref, d_ref, o_ref):
        o_ref[...] = _dot_tn(a_ref[...], d_ref[...])

    return pl.pallas_call(
        body, name="grad_w2", grid=(DFF // tm,),
        in_specs=[pl.BlockSpec((TP, tm), lambda j: (0, j)), VMEM_FULL],
        out_specs=pl.BlockSpec((tm, D), lambda j: (j, 0)),
        out_shape=_sds((DFF, D), F32),
        compiler_params=_cp(("arbitrary",)),
    )(a, dh2b)


def _dep(token):
    return jnp.zeros((8, 128), F32) if token is None else token


def _grad_w1(n2, dz, dep=None):
    tn = 1024
    per = D // tn

    def body(n_ref, d_ref, dep_ref, o_ref):
        o_ref[0] = _dot_tn(n_ref[...], d_ref[...])

    return pl.pallas_call(
        body, name="grad_w1", grid=(N_CHIP, per),
        in_specs=[VMEM_FULL, pl.BlockSpec((TP, tn), lambda k, j: (0, k * per + j)), ANY],
        out_specs=pl.BlockSpec((1, D, tn), lambda k, j: (k, 0, j)),
        out_shape=_sds((N_CHIP, D, D), F32),
        compiler_params=_cp(("arbitrary", "arbitrary")),
    )(n2, dz, _dep(dep))


def _mlp_dn(dz, w1g, dep=None):
    tk = 1024
    per = D // tk
    nk = DFF // tk

    def body(d_ref, w_ref, dep_ref, o_ref, acc_ref):
        k = pl.program_id(0)
        part = _dot_nt(d_ref[...], w_ref[0])

        @pl.when(k == 0)
        def _():
            acc_ref[...] = part

        @pl.when(k > 0)
        def _():
            acc_ref[...] += part

        @pl.when(k == nk - 1)
        def _():
            pltpu.sync_copy(acc_ref, o_ref)

    return pl.pallas_call(
        body, name="mlp_dn", grid=(nk,),
        in_specs=[pl.BlockSpec((TP, tk), lambda k: (0, k)),
                  pl.BlockSpec((1, D, tk), lambda k: (k // per, 0, k % per)), ANY],
        out_specs=ANY,
        out_shape=_sds((TP, D), F32),
        scratch_shapes=[pltpu.VMEM((TP, D), F32)],
        compiler_params=_cp(("arbitrary",)),
    )(dz, w1g, _dep(dep))


def _mixed_grad(dh1b, wout):
    tn = 512

    def body(d_ref, w_ref, o_ref):
        o_ref[...] = _dot_nt(d_ref[...], w_ref[...])

    return pl.pallas_call(
        body, name="mixed_grad", grid=(D // tn,),
        in_specs=[VMEM_FULL, pl.BlockSpec((tn, D), lambda j: (j, 0))],
        out_specs=pl.BlockSpec((TP, tn), lambda j: (0, j)),
        out_shape=_sds((TP, D), F32),
        compiler_params=_cp(("arbitrary",)),
    )(dh1b, wout)


def _grad_wout(og, op, dh1b):
    tm = 512

    def body(og_ref, op_ref, d_ref, o_ref):
        j = pl.program_id(0)

        @pl.when(j < 2)
        def _():
            o_ref[0] = _dot_tn(og_ref[...], d_ref[...])

        @pl.when(j >= 2)
        def _():
            o_ref[0] = _dot_tn(op_ref[...], d_ref[...])

    return pl.pallas_call(
        body, name="grad_wout", grid=(N_CHIP,),
        in_specs=[pl.BlockSpec((TP, tm), lambda j: (0, jnp.minimum(j, 1))),
                  pl.BlockSpec((TP, tm), lambda j: (0, jnp.maximum(j - 2, 0))), VMEM_FULL],
        out_specs=pl.BlockSpec((1, tm, D), lambda j: (j, 0, 0)),
        out_shape=_sds((N_CHIP, tm, D), F32),
        compiler_params=_cp(("arbitrary",)),
    )(og, op, dh1b)


def _in_grad(dq, dk, dv, dr, dglr, dpu, wg, dep=None):
    def body(dq_ref, dk_ref, dv_ref, dr_ref, dg_ref, dpu_ref, w_ref, dep_ref, o_ref):
        dv, dr, dg = dv_ref[...], dr_ref[...], dg_ref[...]
        head, tail = slice(0, GW), slice(GW, PAD_IN)
        o_ref[...] = (_dot_nt(dq_ref[...], w_ref[0, :, 0:KW]) + _dot_nt(dk_ref[...], w_ref[0, :, KW:GW])
                      + _dot_nt(dv[:, 0:128], w_ref[0, :, tail])
                      + _dot_nt(dv, w_ref[1, :, head]) + _dot_nt(dr[:, 0:128], w_ref[1, :, tail])
                      + _dot_nt(dr, w_ref[2, :, head]) + _dot_nt(dg, w_ref[2, :, tail])
                      + _dot_nt(dpu_ref[...], w_ref[3, :, head]) + _dot_nt(dg, w_ref[3, :, tail]))

    tn = 512
    return pl.pallas_call(
        body, name="in_grad", grid=(D // tn,),
        in_specs=[VMEM_FULL] * 6 + [pl.BlockSpec((N_CHIP, tn, PAD_IN), lambda j: (0, j, 0)), ANY],
        out_specs=pl.BlockSpec((TP, tn), lambda j: (0, j)),
        out_shape=_sds((TP, D), F32),
        compiler_params=_cp(("arbitrary",)),
    )(dq, dk, dv, dr, dglr, dpu, wg, _dep(dep))


def _grad_win(u, dq, dk, dv, dr, dglr, dpu, dep=None):
    tm = 512

    def body(u_ref, dq_hbm, dk_hbm, dv_hbm, dr_hbm, dg_hbm, dpu_hbm, dep_ref, o_ref, dp_ref, sem):
        k, m = pl.program_id(0), pl.program_id(1)
        head, tail = slice(0, GW), slice(GW, PAD_IN)
        pieces = [[(dq_hbm, slice(0, KW)), (dk_hbm, slice(KW, GW)), (dv_hbm.at[:, 0:128], tail)],
                  [(dv_hbm, head), (dr_hbm.at[:, 0:128], tail)],
                  [(dr_hbm, head), (dg_hbm, tail)],
                  [(dpu_hbm, head), (dg_hbm, tail)]]

        def copies(kk):
            return [pltpu.make_async_copy(src, dp_ref.at[kk % 2, :, cols], sem.at[kk % 2, i])
                    for i, (src, cols) in enumerate(pieces[kk])]

        @pl.when((k == 0) & (m == 0))
        def _():
            for cp in copies(0):
                cp.start()

        for kk in range(N_CHIP):
            @pl.when((k == kk) & (m == 0))
            def _(kk=kk):
                for cp in copies(kk):
                    cp.wait()
                if kk + 1 < N_CHIP:
                    for cp in copies(kk + 1):
                        cp.start()

        g = _dot_tn(u_ref[...], dp_ref[k % 2])
        lane = lax.broadcasted_iota(jnp.int32, (tm, PAD_IN), 1)
        for kk in range(N_CHIP):
            @pl.when(k == kk)
            def _(kk=kk):
                if kk == 0:
                    nat = g
                elif kk < 3:
                    nat = pltpu.roll(g, PAD_IN - 4 * kk, 1)
                else:
                    nat = jnp.where(lane < 4, pltpu.roll(g, PAD_IN - (GW + 12), 1), pltpu.roll(g, 4, 1))
                o_ref[0] = nat[:, 0:SHARD_IN]

    return pl.pallas_call(
        body, name="grad_win", grid=(N_CHIP, D // tm),
        in_specs=[pl.BlockSpec((TP, tm), lambda k, m: (0, m))] + [ANY] * 7,
        out_specs=pl.BlockSpec((1, tm, SHARD_IN), lambda k, m: (k, m, 0)),
        out_shape=_sds((N_CHIP, D, SHARD_IN), F32),
        scratch_shapes=[pltpu.VMEM((2, TP, PAD_IN), BF16), pltpu.SemaphoreType.DMA((2, 3))],
        compiler_params=_cp(("arbitrary", "arbitrary")),
    )(u, dq, dk, dv, dr, dglr, dpu, _dep(dep))


def _split3(x):
    hi = x.astype(BF16)
    r1 = x - hi.astype(F32)
    mid = r1.astype(BF16)
    lo = (r1 - mid.astype(F32)).astype(BF16)
    return hi, mid, lo


def _tri_sum(tri, x):
    hi, mid, lo = _split3(x)
    return _dot(tri, hi) + _dot(tri, mid) + _dot(tri, lo)


def _gla_common(n, glr, gw2, gb):
    rows = n * CH + lax.broadcasted_iota(jnp.int32, (CH, 1), 0)
    valid = (rows >= ROW_LO) & (rows < ROW_HI)
    g_raw = _dot(glr.astype(BF16), gw2.astype(BF16)) + gb
    logsig = jnp.minimum(g_raw, 0.0) - jnp.log(1.0 + jnp.exp(-jnp.abs(g_raw)))
    logg = jnp.where(valid, logsig * (1.0 / TAU), 0.0)
    ci = lax.broadcasted_iota(jnp.int32, (CH, CH), 0)
    si = lax.broadcasted_iota(jnp.int32, (CH, CH), 1)
    lower = ci >= si
    G = _tri_sum(lower.astype(BF16), logg)
    Gl = G[CH - 1:CH, :]
    return valid, g_raw, lower, G, Gl


def _p_specs(index):
    def spec(width, shard, col):
        return pl.BlockSpec((1, CH, width), lambda s: (shard, index(s), col))

    return [spec(KW, 0, 0), spec(KW, 0, 1), spec(GW, 1, 0), spec(128, 0, 8), spec(GW, 2, 0), spec(128, 1, 8),
            spec(128, 2, 8), spec(128, 3, 8)]


def _p_load(q_ref, k_ref, vm_ref, vh_ref, rm_ref, rh_ref, ga_ref, gb_ref):
    def joined(main, head):
        return jnp.concatenate([main[:, 0:128] + head, main[:, 128:]], axis=1)

    return q_ref[0], k_ref[0], joined(vm_ref[0], vh_ref[0]), joined(rm_ref[0], rh_ref[0]), ga_ref[0] + gb_ref[0]


def _gla_fwd(P, gw2, gb, gnw, dep=None):
    scale = DK ** -0.5

    def body(p0, p1, p2, p3, p4, p5, p6, p7, gw2_ref, gb_ref, gnw_ref, dep_ref, o_ref, og_ref, sp_ref, st_ref):
        n = pl.program_id(0)

        @pl.when(n == 0)
        def _():
            st_ref[...] = jnp.zeros_like(st_ref)

        q_all, k_all, v_all, r_all, glr = _p_load(p0, p1, p2, p3, p4, p5, p6, p7)
        _, _, lower, G, Gl = _gla_common(n, glr, gw2_ref[...], gb_ref[...])
        eG = jnp.exp(G)
        eN = jnp.exp(-G)
        eE = jnp.exp(Gl - G)
        dec = jnp.exp(Gl)
        gnw_v = gnw_ref[...]
        for h in range(HEADS):
            ks = slice(h * DK, (h + 1) * DK)
            vs = slice(h * DV, (h + 1) * DV)
            kh = k_all[:, ks]
            vh = v_all[:, vs].astype(BF16)
            qd = ((q_all[:, ks] * scale) * eG[:, ks]).astype(BF16)
            ki = (kh * eN[:, ks]).astype(BF16)
            ke = (kh * eE[:, ks]).astype(BF16)
            st = st_ref[h]
            a = jnp.where(lower, _dot_nt(qd, ki), 0.0).astype(BF16)
            o = _dot(a, vh) + _dot_nt(qd, st.astype(BF16))
            sp_ref[0, h] = st
            st_ref[h] = st * dec[:, ks] + _dot_tn(vh, ke)
            o_ref[:, vs] = o
            rs = lax.rsqrt(jnp.mean(o * o, axis=-1, keepdims=True) + EPS)
            rv = r_all[:, vs]
            gate = rv / (1.0 + jnp.exp(-rv))
            og_ref[:, vs] = (((o * rs) * gnw_v) * gate).astype(BF16)

    rv_ = pl.BlockSpec((CH, GW), lambda n: (n, 0))

    def full(shape):
        return pl.BlockSpec(shape, lambda n: tuple(0 for _ in shape))

    return pl.pallas_call(
        body, name="gla_fwd", grid=(NCH,),
        in_specs=_p_specs(lambda n: n) + [full((128, KW)), full((1, KW)), full((1, DV)), ANY],
        out_specs=[rv_, rv_, pl.BlockSpec((1, HEADS, DV, DK), lambda n: (n, 0, 0, 0))],
        out_shape=[_sds((TP, GW), F32), _sds((TP, GW), BF16), _sds((NCH, HEADS, DV, DK), F32)],
        scratch_shapes=[pltpu.VMEM((HEADS, DV, DK), F32)],
        compiler_params=_cp(("arbitrary",)),
    )(*([P] * 8), gw2, gb, gnw, _dep(dep))


def _gla_bwd(dog, o, P, gw2, gb, gnw, sp, dep=None):
    scale = DK ** -0.5

    def body(dog_ref, o_ref, p0, p1, p2, p3, p4, p5, p6, p7, gw2_ref, gb_ref, gnw_ref, sp_ref, dep_ref,
             dq_ref, dk_ref, dv_ref, dr_ref, dglr_ref, dgw2_ref, dgb_ref, dgnw_ref, ds_ref):
        step = pl.program_id(0)
        n = NCH - 1 - step

        @pl.when(step == 0)
        def _():
            ds_ref[...] = jnp.zeros_like(ds_ref)
            dgw2_ref[...] = jnp.zeros_like(dgw2_ref)
            dgb_ref[...] = jnp.zeros_like(dgb_ref)
            dgnw_ref[...] = jnp.zeros_like(dgnw_ref)

        q_all, k_all, v_all, r_all, glr_v = _p_load(p0, p1, p2, p3, p4, p5, p6, p7)
        gw2_b = gw2_ref[...].astype(BF16)
        valid, g_raw, lower, G, Gl = _gla_common(n, glr_v, gw2_ref[...], gb_ref[...])
        upper = lax.broadcasted_iota(jnp.int32, (CH, CH), 0) <= lax.broadcasted_iota(jnp.int32, (CH, CH), 1)
        eG = jnp.exp(G)
        eN = jnp.exp(-G)
        eE = jnp.exp(Gl - G)
        dec = jnp.exp(Gl)
        gnw_v = gnw_ref[...]
        last = lax.broadcasted_iota(jnp.int32, (CH, 1), 0) == CH - 1
        dgnw_acc = jnp.zeros((1, DV), F32)
        dG_parts = []
        for h in range(HEADS):
            ks = slice(h * DK, (h + 1) * DK)
            vs = slice(h * DV, (h + 1) * DV)
            oh = o_ref[:, vs]
            rv = r_all[:, vs]
            dg = dog_ref[:, vs]
            sig = 1.0 / (1.0 + jnp.exp(-rv))
            gate = rv * sig
            rs = lax.rsqrt(jnp.mean(oh * oh, axis=-1, keepdims=True) + EPS)
            ohat = oh * rs
            dr_ref[:, vs] = ((dg * (ohat * gnw_v)) * (sig * (1.0 + rv * (1.0 - sig)))).astype(BF16)
            don = dg * gate
            dgnw_acc = dgnw_acc + jnp.sum(don * ohat, axis=0, keepdims=True)
            gxn = don * gnw_v
            do = (rs * (gxn - ohat * jnp.mean(gxn * ohat, axis=-1, keepdims=True))).astype(BF16)
            kh = k_all[:, ks]
            vh = v_all[:, vs].astype(BF16)
            qd_f = (q_all[:, ks] * scale) * eG[:, ks]
            ki_f = kh * eN[:, ks]
            ke_f = kh * eE[:, ks]
            qd, ki, ke = qd_f.astype(BF16), ki_f.astype(BF16), ke_f.astype(BF16)
            spt = sp_ref[0, h]
            dst = ds_ref[h]
            dst_b = dst.astype(BF16)
            a_t = jnp.where(upper, _dot_nt(ki, qd), 0.0).astype(BF16)
            da = jnp.where(lower, _dot_nt(do, vh), 0.0).astype(BF16)
            da_t = jnp.where(upper, _dot_nt(vh, do), 0.0).astype(BF16)
            dv_ref[:, vs] = (_dot(a_t, do) + _dot_nt(ke, dst_b)).astype(BF16)
            dqd = _dot(da, ki) + _dot(do, spt.astype(BF16))
            dki = _dot(da_t, qd)
            dke = _dot(vh, dst_b)
            ddec = jnp.sum(spt * dst, axis=0, keepdims=True)
            ds_ref[h] = dst * dec[:, ks] + _dot_tn(do, qd)
            dq_ref[:, ks] = ((dqd * eG[:, ks]) * scale).astype(BF16)
            dk_ref[:, ks] = (dki * eN[:, ks] + dke * eE[:, ks]).astype(BF16)
            dke_ke = dke * ke_f
            dG = dqd * qd_f - dki * ki_f - dke_ke
            dGl = jnp.sum(dke_ke, axis=0, keepdims=True) + ddec * dec[:, ks]
            dG_parts.append(dG + jnp.where(last, dGl, 0.0))
        dgnw_ref[...] += dgnw_acc
        dG_all = jnp.concatenate(dG_parts, axis=1)
        dlogg = jnp.where(valid, _tri_sum(upper.astype(BF16), dG_all), 0.0)
        dg_raw = (dlogg * (1.0 / TAU)) * (1.0 / (1.0 + jnp.exp(g_raw)))
        dgb_ref[...] += jnp.sum(dg_raw, axis=0, keepdims=True)
        dg_b = dg_raw.astype(BF16)
        dgw2_ref[...] += _dot_tn(glr_v.astype(BF16), dg_b)
        dglr_ref[...] = _dot_nt(dg_b, gw2_b).astype(BF16)

    def back(s):
        return NCH - 1 - s

    rk = pl.BlockSpec((CH, KW), lambda s: (back(s), 0))
    rv_ = pl.BlockSpec((CH, GW), lambda s: (back(s), 0))
    rg = pl.BlockSpec((CH, 128), lambda s: (back(s), 0))

    def full(shape):
        return pl.BlockSpec(shape, lambda s: tuple(0 for _ in shape))

    return pl.pallas_call(
        body, name="gla_bwd", grid=(NCH,),
        in_specs=[rv_, rv_] + _p_specs(back) + [full((128, KW)), full((1, KW)), full((1, DV)),
                  pl.BlockSpec((1, HEADS, DV, DK), lambda s: (back(s), 0, 0, 0)), ANY],
        out_specs=[rk, rk, rv_, rv_, rg, full((128, KW)), full((1, KW)), full((1, DV))],
        out_shape=[_sds((TP, KW), BF16), _sds((TP, KW), BF16), _sds((TP, GW), BF16), _sds((TP, GW), BF16),
                   _sds((TP, 128), BF16), _sds((128, KW), F32), _sds((1, KW), F32), _sds((1, DV), F32)],
        scratch_shapes=[pltpu.VMEM((HEADS, DV, DK), F32)],
        compiler_params=_cp(("arbitrary",)),
    )(dog, o, *([P] * 8), gw2, gb, gnw, sp, _dep(dep))


POOL_TR = 128
HALO = 16


def _pool_counts(base, nrows):
    rows = base + lax.broadcasted_iota(jnp.int32, (nrows, 1), 0)
    valid = (rows >= ROW_LO) & (rows < ROW_HI)
    t1 = (rows - ROW_LO + 1).astype(F32)
    cnts = [jnp.clip(t1, 1.0, float(w)) for w in WINDOWS]
    return valid, cnts


def _pool_fwd(P, pw, ps, dep=None):
    def body(cur_ref, prev_ref, pw_ref, ps_ref, dep_ref, y_ref, op_ref):
        i = pl.program_id(0)
        cur = cur_ref[0]
        full = jnp.concatenate([prev_ref[0], cur], axis=0)
        s2 = full + pltpu.roll(full, 1, 0)
        s4 = s2 + pltpu.roll(s2, 2, 0)
        s8 = s4 + pltpu.roll(s4, 4, 0)
        s16 = s8 + pltpu.roll(s8, 8, 0)
        valid, cnts = _pool_counts(i * POOL_TR, POOL_TR)
        for g, s in enumerate((s2, s4, s8, s16)):
            cs = slice(g * GC, (g + 1) * GC)
            y = s[HALO:, cs] / cnts[g] - cur[:, cs]
            yb = jnp.where(valid, y, 0.0).astype(BF16)
            y_ref[:, cs] = yb
            op_ref[:, cs] = (_dot(yb, pw_ref[g].astype(BF16)) * ps_ref[:, cs]).astype(BF16)

    row = pl.BlockSpec((POOL_TR, PW), lambda i: (i, 0))
    per = POOL_TR // HALO
    return pl.pallas_call(
        body, name="pool_fwd", grid=(TP // POOL_TR,),
        in_specs=[pl.BlockSpec((1, POOL_TR, PW), lambda i: (3, i, 0)),
                  pl.BlockSpec((1, HALO, PW), lambda i: (3, jnp.maximum(i * per - 1, 0), 0)),
                  pl.BlockSpec((4, GC, GC), lambda i: (0, 0, 0)), pl.BlockSpec((1, PW), lambda i: (0, 0)), ANY],
        out_specs=[row, row],
        out_shape=[_sds((TP, PW), BF16), _sds((TP, PW), BF16)],
        compiler_params=_cp(("arbitrary",)),
    )(P, P, pw, ps, _dep(dep))


def _pool_bwd(dop, y, pw, ps, dep=None):
    nblk = TP // HALO

    def body(cur_ref, nxt_ref, y_ref, pw_ref, ps_ref, dep_ref, dpu_ref, dpw_ref, dps_ref):
        i = pl.program_id(0)

        @pl.when(i == 0)
        def _():
            dpw_ref[...] = jnp.zeros_like(dpw_ref)
            dps_ref[...] = jnp.zeros_like(dps_ref)

        n_all = POOL_TR + HALO
        dcur = cur_ref[...]
        dall = jnp.concatenate([dcur, nxt_ref[...]], axis=0)
        valid, cnts = _pool_counts(i * POOL_TR, n_all)
        for g in range(4):
            cs = slice(g * GC, (g + 1) * GC)
            pwb = pw_ref[g].astype(BF16)
            yb = y_ref[:, cs]
            dyw = (dall[:, cs] * ps_ref[:, cs]).astype(BF16)
            dps_ref[:, cs] += jnp.sum(dcur[:, cs] * _dot(yb, pwb), axis=0, keepdims=True)
            dpw_ref[g] += _dot_tn(yb, dyw[0:POOL_TR, :])
            dyv = jnp.where(valid, _dot_nt(dyw, pwb), 0.0)
            e = dyv / cnts[g]
            w = WINDOWS[g]
            sh = 1
            while sh < w:
                e = e + pltpu.roll(e, n_all - sh, 0)
                sh *= 2
            dpu_ref[:, cs] = (e[0:POOL_TR, :] - dyv[0:POOL_TR, :]).astype(BF16)

    row = pl.BlockSpec((POOL_TR, PW), lambda i: (i, 0))
    per = POOL_TR // HALO
    return pl.pallas_call(
        body, name="pool_bwd", grid=(TP // POOL_TR,),
        in_specs=[pl.BlockSpec((POOL_TR, PW), lambda i: (i, 1)),
                  pl.BlockSpec((HALO, PW), lambda i: (jnp.minimum(i * per + per, nblk - 1), 1)),
                  row, pl.BlockSpec((4, GC, GC), lambda i: (0, 0, 0)), pl.BlockSpec((1, PW), lambda i: (0, 0)), ANY],
        out_specs=[row, pl.BlockSpec((4, GC, GC), lambda i: (0, 0, 0)), pl.BlockSpec((1, PW), lambda i: (0, 0))],
        out_shape=[_sds((TP, PW), BF16), _sds((4, GC, GC), F32), _sds((1, PW), F32)],
        compiler_params=_cp(("arbitrary",)),
    )(dop, dop, y, pw, ps, _dep(dep))


def _place():
    x, y, c = lax.axis_index("x"), lax.axis_index("y"), lax.axis_index("c")
    chips = [(1 - x, y), (x, 1 - y), (1 - x, 1 - y)]
    return x, y, c, chips


HBM = pl.BlockSpec(memory_space=pltpu.HBM)
SEM = pl.BlockSpec(memory_space=pltpu.SEMAPHORE)
EFFECT = pltpu.SideEffectType.DATAFLOW_SIDE_EFFECTING


def _cast_into(w, place, cols_out, name, dep=None):
    rows, cols = w.shape
    tr = 256

    def body(p_ref, w_ref, dep_ref, o_ref):
        if cols_out != cols:
            o_ref[0] = jnp.zeros((tr, cols_out), BF16)
            o_ref[0, :, 0:cols] = w_ref[...].astype(BF16)
        else:
            o_ref[0] = w_ref[...].astype(BF16)

    grid_spec = pltpu.PrefetchScalarGridSpec(
        num_scalar_prefetch=1, grid=(rows // tr,),
        in_specs=[pl.BlockSpec((tr, cols), lambda i, p: (i, 0)), ANY],
        out_specs=pl.BlockSpec((1, tr, cols_out), lambda i, p: (p[0], i, 0)))
    return pl.pallas_call(
        body, name=name, grid_spec=grid_spec,
        out_shape=_sds((N_CHIP, rows, cols_out), BF16),
        compiler_params=_cp(("arbitrary",)),
    )(place, w, _dep(dep))


def _cast_win(w, place, dep=None):
    rows, cols = w.shape
    tr = 256

    def body(p_ref, w_ref, dep_ref, o_ref, t_ref):
        t_ref[...] = jnp.zeros_like(t_ref)
        t_ref[:, 0:cols] = w_ref[...]
        t = t_ref[...]
        lane = lax.broadcasted_iota(jnp.int32, (tr, PAD_IN), 1)
        for kk in range(N_CHIP):
            @pl.when(p_ref[0] == kk)
            def _(kk=kk):
                if kk == 0:
                    placed = t
                elif kk < 3:
                    placed = pltpu.roll(t, 4 * kk, 1)
                else:
                    pool = pltpu.roll(t, PAD_IN - 4, 1)
                    gate = pltpu.roll(t, GW + 12, 1)
                    placed = jnp.where(lane < GW, pool, jnp.where((lane >= GW + 12) & (lane < GW + 16), gate, 0.0))
                o_ref[0] = placed.astype(BF16)

    grid_spec = pltpu.PrefetchScalarGridSpec(
        num_scalar_prefetch=1, grid=(rows // tr,),
        in_specs=[pl.BlockSpec((tr, cols), lambda i, p: (i, 0)), ANY],
        out_specs=pl.BlockSpec((1, tr, PAD_IN), lambda i, p: (p[0], i, 0)),
        scratch_shapes=[pltpu.VMEM((tr, PAD_IN), F32)])
    return pl.pallas_call(
        body, name="cast_win", grid_spec=grid_spec,
        out_shape=_sds((N_CHIP, rows, PAD_IN), BF16),
        compiler_params=_cp(("arbitrary",)),
    )(place, w, _dep(dep))


def _half_rows(ref, k, which):
    h = ref.shape[1] // 2
    return ref.at[k, pl.ds(pl.multiple_of(which * h, 8), h), :]


def _sent_rows(ref, k, which, whole):
    return ref.at[k] if whole else _half_rows(ref, k, which)


def _gather_start(ws, name, whole=None):
    n = len(ws)
    whole = whole or [False] * n

    def body(*refs):
        ins = refs[:n]
        ssems = refs[n:2 * n]
        rsems = refs[2 * n:3 * n]
        token = refs[4 * n]
        x, y, c, chips = _place()
        me = 2 * x + y
        for w in range(n):
            blk = _sent_rows(ins[w], me, c, whole[w])
            for j, chip in enumerate(chips):
                pltpu.make_async_remote_copy(src_ref=blk, dst_ref=blk, send_sem=ssems[w].at[j], recv_sem=rsems[w].at[j],
                                             device_id=(*chip, c), device_id_type=MESH).start()
        token[...] = jnp.zeros_like(token)

    sem3 = pltpu.SemaphoreType.DMA((3,))
    outs = pl.pallas_call(
        body, name=name,
        out_shape=tuple([sem3] * (2 * n) + [pltpu.HBM(w.shape, w.dtype) for w in ws] + [_sds((8, 128), F32)]),
        in_specs=(HBM,) * n, out_specs=(SEM,) * (2 * n) + (HBM,) * n + (VMEM_FULL,),
        input_output_aliases={w: 2 * n + w for w in range(n)},
        compiler_params=pltpu.CompilerParams(has_side_effects=EFFECT),
    )(*[pltpu.with_memory_space_constraint(w, pltpu.HBM) for w in ws])
    return outs[:n], outs[n:2 * n], outs[2 * n:3 * n], outs[3 * n]


def _gather_wait(w, ssem, rsem, after, name, whole=False):
    def body(w_ref, ssem_ref, rsem_ref, after_ref, out_ref):
        x, y, c, chips = _place()
        me = 2 * x + y
        mine = _sent_rows(w_ref, me, c, whole)
        for j, (cx, cy) in enumerate(chips):
            cp = pltpu.make_async_remote_copy(src_ref=mine, dst_ref=_sent_rows(w_ref, 2 * cx + cy, c, whole),
                                              send_sem=ssem_ref.at[j], recv_sem=rsem_ref.at[j],
                                              device_id=(cx, cy, c), device_id_type=MESH)
            cp.wait_send()
            cp.wait_recv()

    return pl.pallas_call(
        body, name=name, out_shape=pltpu.HBM(w.shape, w.dtype),
        in_specs=(HBM, SEM, SEM, ANY), out_specs=HBM, input_output_aliases={0: 0},
        compiler_params=pltpu.CompilerParams(has_side_effects=EFFECT),
    )(w, ssem, rsem, after)


def _gather_copies(ref, kind, ssem, rsem):
    x, y, c, _ = _place()
    xn, yn, sib = (1 - x, y, c), (x, 1 - y, c), (x, y, 1 - c)
    kx, ky, kd = 2 * (1 - x) + y, 2 * x + (1 - y), 2 * (1 - x) + (1 - y)
    half = ref.shape[1] // 2
    quarter = half // 2

    def piece(k, q):
        return ref.at[k, pl.ds(pl.multiple_of(c * half + q * quarter, 8), quarter), :]

    if kind == "d":
        blk = _half_rows(ref, 2 * x + y, c)
        pairs = [(blk, xn), (blk, yn)]
    elif kind == "r":
        pairs = [(piece(ky, 1), xn), (piece(kx, 0), yn)]
    elif kind == "fx":
        pairs = [(_half_rows(ref, kx, c), sib), (_half_rows(ref, ky, c), sib)]
    else:
        pairs = [(_half_rows(ref, kd, c), sib)]
    return [pltpu.make_async_remote_copy(src_ref=blk, dst_ref=blk, send_sem=ssem.at[i], recv_sem=rsem.at[i],
                                         device_id=to, device_id_type=MESH) for i, (blk, to) in enumerate(pairs)]


def _gather_step(name, arrs, waits, starts, sems_in=(), after=()):
    n, nw, ns = len(arrs), len(waits), len(starts)
    after = [a for a in after if a is not None] or [_dep(None)]

    def body(*refs):
        a_in = refs[:n]
        s_in = refs[n:n + 2 * nw]
        outs = refs[n + 2 * nw + len(after):]
        s_out = outs[:2 * ns]
        for i, (ai, kind) in enumerate(waits):
            for cp in _gather_copies(a_in[ai], kind, s_in[2 * i], s_in[2 * i + 1]):
                cp.wait_send()
                cp.wait_recv()
        for i, (ai, kind) in enumerate(starts):
            for cp in _gather_copies(a_in[ai], kind, s_out[2 * i], s_out[2 * i + 1]):
                cp.start()
        if ns:
            token = outs[2 * ns + n]
            token[...] = jnp.zeros_like(token)

    sem2 = pltpu.SemaphoreType.DMA((2,))
    flat_in = [s for pair in sems_in for s in pair]
    arrs = [pltpu.with_memory_space_constraint(a, pltpu.HBM) for a in arrs]
    outs = pl.pallas_call(
        body, name=name,
        out_shape=tuple([sem2] * (2 * ns) + [pltpu.HBM(a.shape, a.dtype) for a in arrs]
                        + ([_sds((8, 128), F32)] if ns else [])),
        in_specs=(HBM,) * n + (SEM,) * (2 * nw) + (ANY,) * len(after),
        out_specs=(SEM,) * (2 * ns) + (HBM,) * n + ((VMEM_FULL,) if ns else ()),
        input_output_aliases={i: 2 * ns + i for i in range(n)},
        compiler_params=pltpu.CompilerParams(has_side_effects=EFFECT),
    )(*arrs, *flat_in, *after)
    sems = [(outs[2 * i], outs[2 * i + 1]) for i in range(ns)]
    return sems, list(outs[2 * ns:2 * ns + n]), (outs[2 * ns + n] if ns else None)


def _rs_start(sb, name, after=None):
    _, half, cols = sb.shape

    def body(sb_ref, land_ref, after_ref, ssem, rsem, sb_out, land_out, token):
        x, y, c, chips = _place()
        for j, (cx, cy) in enumerate(chips):
            pltpu.make_async_remote_copy(src_ref=sb_ref.at[2 * cx + cy], dst_ref=land_ref.at[j], send_sem=ssem.at[j],
                                         recv_sem=rsem.at[j], device_id=(cx, cy, c), device_id_type=MESH).start()
        token[...] = jnp.zeros_like(token)

    sem3 = pltpu.SemaphoreType.DMA((3,))
    land = lax.empty((3, half, cols), BF16)
    return pl.pallas_call(
        body, name=name,
        out_shape=(sem3, sem3, pltpu.HBM(sb.shape, sb.dtype), pltpu.HBM(land.shape, land.dtype), _sds((8, 128), F32)),
        in_specs=(HBM, HBM, ANY), out_specs=(SEM, SEM, HBM, HBM, VMEM_FULL), input_output_aliases={0: 2, 1: 3},
        compiler_params=pltpu.CompilerParams(has_side_effects=EFFECT),
    )(pltpu.with_memory_space_constraint(sb, pltpu.HBM), pltpu.with_memory_space_constraint(land, pltpu.HBM), _dep(after))


def _rs_wait(items, after, name):
    n = len(items)

    def body(*refs):
        x, y, c, chips = _place()
        for i in range(n):
            sb_ref, land_ref, ssem_ref, rsem_ref = refs[4 * i:4 * i + 4]
            for j, (cx, cy) in enumerate(chips):
                cp = pltpu.make_async_remote_copy(src_ref=sb_ref.at[2 * cx + cy], dst_ref=land_ref.at[j],
                                                  send_sem=ssem_ref.at[j], recv_sem=rsem_ref.at[j],
                                                  device_id=(cx, cy, c), device_id_type=MESH)
                cp.wait_send()
                cp.wait_recv()

    outs = pl.pallas_call(
        body, name=name,
        out_shape=tuple(pltpu.HBM(a.shape, a.dtype) for it in items for a in it[:2]),
        in_specs=(HBM, HBM, SEM, SEM) * n + (ANY,), out_specs=(HBM,) * (2 * n),
        input_output_aliases={4 * i + k: 2 * i + k for i in range(n) for k in range(2)},
        compiler_params=pltpu.CompilerParams(has_side_effects=EFFECT),
    )(*[a for it in items for a in it], after)
    return [outs[2 * i + 1] for i in range(n)]


def _pair_copy(g_ref, land_ref, ssem, rsem):
    x, y, c, _ = _place()
    h = g_ref.shape[1] // 2
    src = g_ref.at[:, pl.ds(pl.multiple_of((1 - c) * h, 8), h), :]
    return pltpu.make_async_remote_copy(src_ref=src, dst_ref=land_ref, send_sem=ssem.at[0], recv_sem=rsem.at[0],
                                        device_id=(x, y, 1 - c), device_id_type=MESH)


def _pair_start(g, name):
    def body(g_ref, land_ref, ssem, rsem, g_out, land_out, token):
        _pair_copy(g_ref, land_ref, ssem, rsem).start()
        token[...] = jnp.zeros_like(token)

    sem1 = pltpu.SemaphoreType.DMA((1,))
    land = lax.empty((N_CHIP, g.shape[1] // 2, g.shape[2]), F32)
    return pl.pallas_call(
        body, name=name,
        out_shape=(sem1, sem1, pltpu.HBM(g.shape, g.dtype), pltpu.HBM(land.shape, land.dtype), _sds((8, 128), F32)),
        in_specs=(HBM, HBM), out_specs=(SEM, SEM, HBM, HBM, VMEM_FULL), input_output_aliases={0: 2, 1: 3},
        compiler_params=pltpu.CompilerParams(has_side_effects=EFFECT),
    )(pltpu.with_memory_space_constraint(g, pltpu.HBM), pltpu.with_memory_space_constraint(land, pltpu.HBM))


def _pair_wait(g, land, ssem, rsem, after, name):
    def body(g_ref, land_ref, ssem_ref, rsem_ref, after_ref, g_out, land_out):
        cp = _pair_copy(g_ref, land_ref, ssem_ref, rsem_ref)
        cp.wait_send()
        cp.wait_recv()

    return pl.pallas_call(
        body, name=name,
        out_shape=(pltpu.HBM(g.shape, g.dtype), pltpu.HBM(land.shape, land.dtype)),
        in_specs=(HBM, HBM, SEM, SEM, ANY), out_specs=(HBM, HBM), input_output_aliases={0: 0, 1: 1},
        compiler_params=pltpu.CompilerParams(has_side_effects=EFFECT),
    )(g, land, ssem, rsem, after)


def _pair_sum(g, rcv, place, name):
    _, rows, cols = g.shape
    half = rows // 2
    tr = 256
    nt = half // tr

    def body(p_ref, g_ref, r_ref, sb_ref, sf_ref):
        s = pl.program_id(1)
        tot = g_ref[0] + r_ref[0]
        sb_ref[0] = tot.astype(BF16)

        @pl.when(s == p_ref[0])
        def _():
            sf_ref[...] = tot

    grid_spec = pltpu.PrefetchScalarGridSpec(
        num_scalar_prefetch=1, grid=(nt, N_CHIP),
        in_specs=[pl.BlockSpec((1, tr, cols), lambda t, s, p: (s, p[1] * nt + t, 0)),
                  pl.BlockSpec((1, tr, cols), lambda t, s, p: (s, t, 0))],
        out_specs=[pl.BlockSpec((1, tr, cols), lambda t, s, p: (s, t, 0)),
                   pl.BlockSpec((tr, cols), lambda t, s, p: (t, 0))])
    return pl.pallas_call(
        body, name=name, grid_spec=grid_spec,
        out_shape=[_sds((N_CHIP, half, cols), BF16), _sds((half, cols), F32)],
        compiler_params=_cp(("arbitrary", "arbitrary")),
    )(place, g, rcv)


def _final_sum(sf, rb, place, name):
    half, cols = sf.shape
    tr = 256
    nt = half // tr

    def body(p_ref, sf_ref, r_ref, out_ref):
        acc = sf_ref[...]
        for j in range(3):
            acc = acc + r_ref[j].astype(F32)
        out_ref[...] = acc

    grid_spec = pltpu.PrefetchScalarGridSpec(
        num_scalar_prefetch=1, grid=(nt,),
        in_specs=[pl.BlockSpec((tr, cols), lambda t, p: (t, 0)), pl.BlockSpec((3, tr, cols), lambda t, p: (0, t, 0))],
        out_specs=pl.BlockSpec((tr, cols), lambda t, p: (p[1] * nt + t, 0)))
    return pl.pallas_call(
        body, name=name, grid_spec=grid_spec,
        out_shape=_sds((2 * half, cols), F32),
        compiler_params=_cp(("arbitrary",)),
    )(place, sf, rb)


def _half_copy(f_ref, which, ssem, rsem):
    x, y, c, _ = _place()
    h = f_ref.shape[0] // 2
    rows = f_ref.at[pl.ds(pl.multiple_of(which * h, 8), h), :]
    return pltpu.make_async_remote_copy(src_ref=rows, dst_ref=rows, send_sem=ssem.at[0], recv_sem=rsem.at[0],
                                        device_id=(x, y, 1 - c), device_id_type=MESH)


def _half_start(fulls, name, after=None):
    n = len(fulls)

    def body(*refs):
        for i in range(n):
            _half_copy(refs[i], lax.axis_index("c"), refs[n + 1 + 2 * i], refs[n + 2 + 2 * i]).start()
        token = refs[4 * n + 1]
        token[...] = jnp.zeros_like(token)

    sem1 = pltpu.SemaphoreType.DMA((1,))
    outs = pl.pallas_call(
        body, name=name,
        out_shape=tuple([sem1] * (2 * n) + [pltpu.HBM(f.shape, f.dtype) for f in fulls] + [_sds((8, 128), F32)]),
        in_specs=(HBM,) * n + (ANY,), out_specs=(SEM,) * (2 * n) + (HBM,) * n + (VMEM_FULL,),
        input_output_aliases={i: 2 * n + i for i in range(n)},
        compiler_params=pltpu.CompilerParams(has_side_effects=EFFECT),
    )(*[pltpu.with_memory_space_constraint(f, pltpu.HBM) for f in fulls], _dep(after))
    return [(outs[2 * i], outs[2 * i + 1], outs[2 * n + i]) for i in range(n)], outs[3 * n]


def _half_wait(items, after, name):
    n = len(items)

    def body(*refs):
        c = lax.axis_index("c")
        for i in range(n):
            ssem_ref, rsem_ref, f_ref = refs[3 * i:3 * i + 3]
            _half_copy(f_ref, c, ssem_ref, rsem_ref).wait_send()
            _half_copy(f_ref, 1 - c, ssem_ref, rsem_ref).wait_recv()

    return pl.pallas_call(
        body, name=name, out_shape=tuple(pltpu.HBM(it[2].shape, it[2].dtype) for it in items),
        in_specs=(SEM, SEM, HBM) * n + (ANY,) * len(after), out_specs=(HBM,) * n,
        input_output_aliases={3 * i + 2: i for i in range(n)},
        compiler_params=pltpu.CompilerParams(has_side_effects=EFFECT),
    )(*[a for it in items for a in it], *after)


def _small_copies(src_ref, land_ref, ssem, rsem, first):
    x, y, c, chips = _place()
    if first:
        return [pltpu.make_async_remote_copy(src_ref=src_ref, dst_ref=land_ref, send_sem=ssem.at[0], recv_sem=rsem.at[0],
                                             device_id=(x, y, 1 - c), device_id_type=MESH)]
    return [pltpu.make_async_remote_copy(src_ref=src_ref, dst_ref=land_ref.at[j], send_sem=ssem.at[j], recv_sem=rsem.at[j],
                                         device_id=(*chip, c), device_id_type=MESH) for j, chip in enumerate(chips)]


def _small_start(src, first, name, after=None):
    n = 1 if first else 3

    def body(src_ref, land_ref, after_ref, ssem, rsem, src_out, land_out, token):
        for cp in _small_copies(src_ref, land_ref, ssem, rsem, first):
            cp.start()
        token[...] = jnp.zeros_like(token)

    sems = pltpu.SemaphoreType.DMA((n,))
    land = lax.empty(src.shape if first else (3,) + src.shape, F32)
    return pl.pallas_call(
        body, name=name,
        out_shape=(sems, sems, pltpu.HBM(src.shape, F32), pltpu.HBM(land.shape, F32), _sds((8, 128), F32)),
        in_specs=(HBM, HBM, ANY), out_specs=(SEM, SEM, HBM, HBM, VMEM_FULL), input_output_aliases={0: 2, 1: 3},
        compiler_params=pltpu.CompilerParams(has_side_effects=EFFECT),
    )(pltpu.with_memory_space_constraint(src, pltpu.HBM), pltpu.with_memory_space_constraint(land, pltpu.HBM), _dep(after))


def _small_wait(src, land, ssem, rsem, first, after, name):
    def body(src_ref, land_ref, ssem_ref, rsem_ref, after_ref, src_out, land_out):
        for cp in _small_copies(src_ref, land_ref, ssem_ref, rsem_ref, first):
            cp.wait_send()
            cp.wait_recv()

    return pl.pallas_call(
        body, name=name,
        out_shape=(pltpu.HBM(src.shape, F32), pltpu.HBM(land.shape, F32)),
        in_specs=(HBM, HBM, SEM, SEM, ANY), out_specs=(HBM, HBM), input_output_aliases={0: 0, 1: 1},
        compiler_params=pltpu.CompilerParams(has_side_effects=EFFECT),
    )(src, land, ssem, rsem, after)


def _small_pair_sum(vec, got):
    def body(v_ref, g_ref, o_ref):
        o_ref[...] = v_ref[...] + g_ref[...]

    return pl.pallas_call(body, name="small_pair_sum", in_specs=[VMEM_FULL] * 2, out_specs=VMEM_FULL,
                          out_shape=_sds(vec.shape, F32), compiler_params=_cp())(vec, got)


def _small_chip_sum(pair, got, place):
    def body(p_ref, pair_ref, got_ref, o_ref):
        acc = None
        for kk in range(N_CHIP):
            d = jnp.bitwise_xor(p_ref[0], kk)
            t = jnp.where(d == 0, pair_ref[...], jnp.where(d == 2, got_ref[0], jnp.where(d == 1, got_ref[1], got_ref[2])))
            acc = t if acc is None else acc + t
        o_ref[...] = acc

    grid_spec = pltpu.PrefetchScalarGridSpec(
        num_scalar_prefetch=1, grid=(1,),
        in_specs=[pl.BlockSpec(pair.shape, lambda i, p: (0, 0)), pl.BlockSpec(got.shape, lambda i, p: (0, 0, 0))],
        out_specs=pl.BlockSpec(pair.shape, lambda i, p: (0, 0)))
    return pl.pallas_call(body, name="small_chip_sum", grid_spec=grid_spec, out_shape=_sds(pair.shape, F32),
                          compiler_params=_cp(("arbitrary",)))(place, pair, got)


def _adam_math(w, g, m, v):
    m = B1 * m + (1.0 - B1) * g
    v = B2 * v + (1.0 - B2) * (g * g)
    m_hat = m / (1.0 - B1 ** STEP)
    v_hat = v / (1.0 - B2 ** STEP)
    delta = -LR * (m_hat / (jnp.sqrt(v_hat) + AEPS) + WD * w)
    return delta, m, v


def _adam_big(w, g, m, v, name, dep=None):
    rows, cols = w.shape
    tr = 128

    def body(w_ref, g_ref, m_ref, v_ref, dep_ref, go_ref, d_ref, nm_ref, nv_ref):
        g = g_ref[...]
        d, nm, nv = _adam_math(w_ref[...], g, m_ref[...], v_ref[...])
        go_ref[...] = g
        d_ref[...] = d
        nm_ref[...] = nm
        nv_ref[...] = nv

    blk = pl.BlockSpec((tr, cols), lambda i: (i, 0))
    return pl.pallas_call(
        body, name=name, grid=(rows // tr,),
        in_specs=[blk] * 4 + [ANY], out_specs=[blk] * 4, out_shape=[_sds((rows, cols), F32)] * 4,
        compiler_params=_cp(("arbitrary",)),
    )(w, g, m, v, _dep(dep))


def _adam_small(ws, gs, ms, vs, dep=None):
    n = len(ws)

    def body(*refs):
        for i in range(n):
            d, nm, nv = _adam_math(refs[i][...], refs[n + i][...], refs[2 * n + i][...], refs[3 * n + i][...])
            refs[4 * n + 1 + i][...] = d
            refs[5 * n + 1 + i][...] = nm
            refs[6 * n + 1 + i][...] = nv

    shapes = [_sds(w.shape, F32) for w in ws]
    outs = pl.pallas_call(
        body, name="adam_small",
        in_specs=[VMEM_FULL] * (4 * n) + [ANY], out_specs=[VMEM_FULL] * (3 * n), out_shape=shapes * 3,
        compiler_params=_cp(),
    )(*ws, *gs, *ms, *vs, _dep(dep))
    return outs[:n], outs[n:2 * n], outs[2 * n:]


def _pad_rows8(a):
    flat = a.reshape(-1, 128)
    pad = (-flat.shape[0]) % 8
    if pad:
        flat = jnp.concatenate([flat, jnp.zeros((pad, 128), F32)], axis=0)
    return flat


def kernel(x, meta_tokens, norm1_w, w_in, gate_w2, gate_b, gla_norm_w, pool_w, pool_scale, w_out, norm2_w, mlp_w1, mlp_w2, final_norm_w, loss_target, m_meta_tokens, m_norm1_w, m_w_in, m_gate_w2, m_gate_b, m_gla_norm_w, m_pool_w, m_pool_scale, m_w_out, m_norm2_w, m_mlp_w1, m_mlp_w2, m_final_norm_w, v_meta_tokens, v_norm1_w, v_w_in, v_gate_w2, v_gate_b, v_gla_norm_w, v_pool_w, v_pool_scale, v_w_out, v_norm2_w, v_mlp_w1, v_mlp_w2, v_final_norm_w):
    cx, cy, cc = lax.axis_index("x"), lax.axis_index("y"), lax.axis_index("c")
    me = (2 * cx + cy).astype(jnp.int32)

    place = jnp.stack([me, cc.astype(jnp.int32)])
    fw = final_norm_w.reshape(1, D)

    mine = jnp.concatenate([meta_tokens.reshape(64, 128), gate_w2[0], pool_w[0].reshape(512, 128)], axis=0)
    small = lax.dynamic_update_slice(jnp.zeros((N_CHIP, 592, 128), F32), mine[None], (me, 0, 0))
    (s_sm,), (r_sm,), (f_sm,), tok = _gather_start([small], "gather_start_small", [True])
    (sem_win_d,), (win,), tok = _gather_step("gather_start_win", [_cast_win(w_in[0], place, tok)], [], [(0, "d")])
    wout, w1, w2 = (_cast_into(w_out[0], place, D, "cast_wout", tok), _cast_into(mlp_w1[0], place, D, "cast_w1", tok),
                    _cast_into(mlp_w2[0], place, D, "cast_w2", tok))
    small = _gather_wait(f_sm, s_sm, r_sm, w2, "gather_wait_small", True)
    metaF = jnp.concatenate([small[k, 0:64].reshape(N_META, 512) for k in range(N_CHIP)], axis=1)
    gw2F = jnp.concatenate([small[k, 64:80] for k in range(N_CHIP)], axis=1)
    pwF = jnp.concatenate([small[k, 80:592].reshape(4, 64, GC) for k in range(N_CHIP)], axis=1)

    fly = {"win": win, "wout": wout, "w1": w1, "w2": w2}
    sems = {"win_d": sem_win_d}

    def step(name, names, waits, starts, after):
        at = {nm: i for i, nm in enumerate(names)}
        new, arrs, tok = _gather_step(name, [fly[nm] for nm in names], [(at[nm], k) for nm, k in waits],
                                      [(at[nm], k) for nm, k in starts], [sems[nm + "_" + k] for nm, k in waits], after)
        fly.update(zip(names, arrs))
        sems.update({nm + "_" + k: s for (nm, k), s in zip(starts, new)})
        return tok

    def relay_first():
        return step("gather_relay_win", ["win", "wout", "w1"], [("win", "d")],
                    [("win", "r"), ("win", "fx"), ("wout", "d"), ("w1", "d")], [v_w_in[0]])

    def get_win(after):
        tok = step("gather_land_win", ["win"], [("win", "r")], [("win", "fd")], [after])
        step("gather_wait_win", ["win"], [("win", "fx"), ("win", "fd")], [], [tok])
        return fly["win"]

    def relay_mid(after):
        return step("gather_relay_mid", ["wout"], [("wout", "d")], [("wout", "r"), ("wout", "fx")], [after, m_w_in[0]])

    def land_wout(after):
        return step("gather_land_wout", ["wout", "w1", "w2"], [("wout", "r"), ("w1", "d")],
                    [("wout", "fd"), ("w1", "r"), ("w1", "fx"), ("w2", "d")], [after])

    def get_wout(after):
        step("gather_wait_wout", ["wout"], [("wout", "fx"), ("wout", "fd")], [], [after])
        tok = step("gather_land_w1", ["w1"], [("w1", "r")], [("w1", "fd")], [fly["wout"]])
        return fly["wout"].reshape(D, D), tok

    def get_w1(after):
        step("gather_wait_w1", ["w1"], [("w1", "fx"), ("w1", "fd")], [], [after])
        return fly["w1"]

    def relay_last(after):
        return step("gather_relay_w2", ["w2"], [("w2", "d")], [("w2", "r"), ("w2", "fx")], [after])

    def get_w2(after):
        tok = step("gather_land_w2", ["w2"], [("w2", "r")], [("w2", "fd")], [after])
        step("gather_wait_w2", ["w2"], [("w2", "fx"), ("w2", "fd")], [], [tok])
        return fly["w2"].reshape(DFF, D)

    pairs, pending = {}, {}

    halves = {}

    def reduce_(names, after, tag):
        items = [(pending[nm][3], pending[nm][4], pending[nm][1], pending[nm][2]) for nm in names]
        landed = _rs_wait(items, after, "rs_wait_" + tag)
        fulls = [_final_sum(pending[nm][0], rb, place, "final_sum_" + nm) for nm, rb in zip(names, landed)]
        sent, token = _half_start(fulls, "half_start_" + tag)
        halves.update(zip(names, sent))
        return token

    def grad_start(nm, g):
        ssem, rsem, g_thru, land, token = _pair_start(g, "pair_start_" + nm)
        pairs[nm] = (ssem, rsem, g_thru, land)
        if nm == "win":
            token = reduce_(["w2", "w1", "wout"], token, "mlp_wout")
        return token

    def grad_finish(nm, after):
        ssem, rsem, g_thru, land = pairs[nm]
        g, rcv = _pair_wait(g_thru, land, ssem, rsem, after, "pair_wait_" + nm)
        sb, sf = _pair_sum(g, rcv, place, "pair_sum_" + nm)
        ssem, rsem, sb_thru, land, token = _rs_start(sb, "rs_start_" + nm)
        pending[nm] = (sf, ssem, rsem, sb_thru, land)
        return token

    (grad_x, loss8, d_n1w, d_gb, d_gnw, d_ps, d_n2w, d_fw, d_meta, d_gw2, d_pw) = _local_step(
        x[0], loss_target[0], dict(relay_first=relay_first, win=get_win, relay_mid=relay_mid, land_wout=land_wout,
                                   wout=get_wout, w1=get_w1, relay_last=relay_last, w2=get_w2),
        metaF, gw2F, pwF, norm1_w, gate_b, gla_norm_w, pool_scale, norm2_w, fw, grad_start, grad_finish)
    return _reduce_and_update(
        me, place, pending, halves, reduce_, grad_x, loss8, d_n1w, d_gb, d_gnw, d_ps, d_n2w, d_fw, d_meta, d_gw2, d_pw,
        meta_tokens, norm1_w, w_in, gate_w2, gate_b, gla_norm_w, pool_w, pool_scale, w_out, norm2_w, mlp_w1, mlp_w2, fw,
        m_meta_tokens, m_norm1_w, m_w_in, m_gate_w2, m_gate_b, m_gla_norm_w, m_pool_w, m_pool_scale, m_w_out, m_norm2_w,
        m_mlp_w1, m_mlp_w2, m_final_norm_w, v_meta_tokens, v_norm1_w, v_w_in, v_gate_w2, v_gate_b, v_gla_norm_w, v_pool_w,
        v_pool_scale, v_w_out, v_norm2_w, v_mlp_w1, v_mlp_w2, v_final_norm_w)


def _local_step(x, target, gather, metaF, gw2F, pwF, norm1_w, gate_b, gla_norm_w, pool_scale, norm2_w, fw, grad_start,
                grad_finish):
    h0, u = _embed_norm(x, metaF, norm1_w, gather["relay_first"]())
    Win = gather["win"](u)
    P = _in_proj(u, Win)
    gw2p = jnp.pad(gw2F, ((0, 128 - RANK), (0, 0)))
    yb, op = _pool_fwd(P, pwF, pool_scale, gather["relay_mid"](P))
    o, og, sp = _gla_fwd(P, gw2p, gate_b, gla_norm_w, gather["land_wout"](op))
    Wout, tok = gather["wout"](og)
    h1 = _out_proj(og, op, Wout, h0, tok)
    n2 = _norm_rows(h1, norm2_w, "norm2")
    W1 = gather["w1"](n2)
    zr, a, tok = _mlp_up(n2, W1, 0)
    zr, a, _ = _mlp_up(n2, W1, 1, (zr, a), gather["relay_last"](tok))
    W2 = gather["w2"](a)
    h2 = _mlp_down(a, W2, h1)

    dh2, dh2b, d_fw, loss8 = _loss_head(h2, target, fw)
    tok = grad_start("w2", _grad_w2(a, dh2b).reshape(N_CHIP, D, D))
    dz = _mlp_dz(dh2b, W2, zr, tok)
    tok = grad_finish("w2", dz)
    tok = grad_start("w1", _grad_w1(n2, dz, tok))
    dn2 = _mlp_dn(dz, W1, tok)
    tok = grad_finish("w1", dn2)
    dh1, dh1b, d_n2w = _norm_bwd(dn2, h1, dh2, norm2_w, "norm2_bwd", tok)
    dmixed = _mixed_grad(dh1b, Wout)
    tok = grad_start("wout", _grad_wout(og, op, dh1b))
    dpu, d_pw, d_ps = _pool_bwd(dmixed, yb, pwF, pool_scale, tok)
    dq, dk, dv, dr, dglr, d_gw2p, d_gb, d_gnw = _gla_bwd(dmixed, o, P, gw2p, gate_b, gla_norm_w, sp, tok)
    d_gw2 = d_gw2p[0:RANK]
    tok = grad_finish("wout", dq)
    tok = grad_start("win", _grad_win(u, dq, dk, dv, dr, dglr, dpu, tok))
    du = _in_grad(dq, dk, dv, dr, dglr, dpu, Win, tok)
    tok = grad_finish("win", du)
    grad_x, d_meta, d_n1w = _input_grad(du, h0, dh1, norm1_w, tok)
    return grad_x, loss8, d_n1w, d_gb, d_gnw, d_ps, d_n2w, d_fw, d_meta, d_gw2, d_pw


def _reduce_and_update(me, place, pending, halves, reduce_, grad_x, loss8, d_n1w, d_gb, d_gnw, d_ps, d_n2w, d_fw, d_meta, d_gw2,
                       d_pw,
                       meta_tokens, norm1_w, w_in, gate_w2, gate_b, gla_norm_w, pool_w, pool_scale, w_out, norm2_w,
                       mlp_w1, mlp_w2, fw, m_meta_tokens, m_norm1_w, m_w_in, m_gate_w2, m_gate_b, m_gla_norm_w, m_pool_w,
                       m_pool_scale, m_w_out, m_norm2_w, m_mlp_w1, m_mlp_w2, m_final_norm_w, v_meta_tokens, v_norm1_w, v_w_in,
                       v_gate_w2, v_gate_b, v_gla_norm_w, v_pool_w, v_pool_scale, v_w_out, v_norm2_w, v_mlp_w1, v_mlp_w2,
                       v_final_norm_w):
    parts = [loss8, d_n1w, d_gb, d_gnw, d_ps, d_n2w, d_fw, d_meta, d_gw2, d_pw]
    packed = [_pad_rows8(p) for p in parts]
    sizes = [p.shape[0] for p in packed]
    vec = jnp.concatenate(packed, axis=0)

    big = {}
    params = {"w2": (mlp_w2[0], m_mlp_w2[0], v_mlp_w2[0]), "w1": (mlp_w1[0], m_mlp_w1[0], v_mlp_w1[0]),
              "wout": (w_out[0], m_w_out[0], v_w_out[0]), "win": (w_in[0], m_w_in[0], v_w_in[0])}

    def update(names, after, tag):
        fulls = _half_wait([halves[nm] for nm in names], [after], "half_wait_" + tag)
        tok = None
        for nm, full in zip(names, fulls):
            w, m, v = params[nm]
            big[nm] = _adam_big(w, full, m, v, "adam_" + nm, tok)
            tok = big[nm][3]
        return tok

    s1, r1, vec, land1, tok = _small_start(vec, True, "small_start_pair")
    tok = update(["w2"], tok, "w2")
    vec, got = _small_wait(vec, land1, s1, r1, True, tok, "small_wait_pair")
    pair = _small_pair_sum(vec, got)
    s2, r2, pair, land2, tok = _small_start(pair, False, "small_start_chips")
    tok = reduce_(["win"], tok, "win")
    tok = update(["w1"], tok, "w1")
    tok = update(["win"], tok, "win")
    pair, got = _small_wait(pair, land2, s2, r2, False, tok, "small_wait_chips")
    red = _small_chip_sum(pair, got, place)
    after = update(["wout"], red, "wout")
    offs = [0]
    for s in sizes:
        offs.append(offs[-1] + s)

    def take(i, shape):
        n = 1
        for d in shape:
            n *= d
        return red[offs[i]:offs[i] + n // 128].reshape(shape)

    loss = red[0, 0]
    G_n1w = take(1, (1, D))
    G_gb = take(2, (1, KW))
    G_gnw = take(3, (1, DV))
    G_ps = take(4, (1, PW))
    G_n2w = take(5, (1, D))
    G_fw = take(6, (1, D))
    G_meta = lax.dynamic_slice(take(7, (N_META, D)), (0, me * 512), (N_META, 512))
    G_gw2 = lax.dynamic_slice(take(8, (RANK, KW)), (0, me * 128), (RANK, 128))
    G_pw = lax.dynamic_slice(take(9, (4, GC, GC)), (0, me * 64, 0), (4, 64, GC))

    G_win, d_win, nm_win, nv_win = big["win"]
    G_wout, d_wout, nm_wout, nv_wout = big["wout"]
    G_w1, d_w1, nm_w1, nv_w1 = big["w1"]
    G_w2, d_w2, nm_w2, nv_w2 = big["w2"]
    ws = [meta_tokens, norm1_w, gate_w2[0], gate_b, gla_norm_w, pool_w[0], pool_scale, norm2_w, fw]
    gs = [G_meta, G_n1w, G_gw2, G_gb, G_gnw, G_pw, G_ps, G_n2w, G_fw]
    ms = [m_meta_tokens, m_norm1_w, m_gate_w2[0], m_gate_b, m_gla_norm_w, m_pool_w[0], m_pool_scale, m_norm2_w,
          m_final_norm_w.reshape(1, D)]
    vs = [v_meta_tokens, v_norm1_w, v_gate_w2[0], v_gate_b, v_gla_norm_w, v_pool_w[0], v_pool_scale, v_norm2_w,
          v_final_norm_w.reshape(1, D)]
    ds, nms, nvs = _adam_small(ws, gs, ms, vs, after)

    def assemble(small, win_, wout_, w1_, w2_):
        meta_, n1_, gw2_, gb_, gnw_, pw_, ps_, n2_, fw_ = small
        return (meta_, n1_, win_[None], gw2_[None], gb_, gnw_, pw_[None], ps_, wout_[None], n2_, w1_[None], w2_[None],
                fw_.reshape(D))

    grads_out = assemble(gs, G_win, G_wout, G_w1, G_w2)
    deltas = assemble(ds, d_win, d_wout, d_w1, d_w2)
    new_m = assemble(nms, nm_win, nm_wout, nm_w1, nm_w2)
    new_v = assemble(nvs, nv_win, nv_wout, nv_w1, nv_w2)
    return (loss, grad_x[None], *grads_out, *deltas, *new_m, *new_v)
```

```python
import functools

import jax
import jax.numpy as jnp
from jax import lax
from jax.experimental import pallas as pl
from jax.experimental.pallas import tpu as pltpu

F32 = jnp.float32
BF16 = jnp.bfloat16

D = 2048
SEQ = 2048
N_META = 16
CH = 64
TP = 2176
NCH = TP // CH
ROW_LO = 112
X_LO = 128
ROW_HI = TP
XT = 128
NXT = TP // XT
HEADS = 4
DK = 128
DV = 256
KW = HEADS * DK
GW = HEADS * DV
RANK = 16
TAU = 16.0
WINDOWS = (2, 4, 8, 16)
PW = 1024
GC = 256
DFF = 8192
EPS = 1e-6
SHARD_IN = 1028
PAD_IN = 1152
N_CHIP = 4

LR = 0.001
B1 = 0.9
B2 = 0.999
AEPS = 1e-08
WD = 0.01
STEP = 10

VMEM_LIMIT = 60 * 1024 * 1024
ANY = pl.BlockSpec(memory_space=pl.ANY)
VMEM_FULL = pl.BlockSpec(memory_space=pltpu.VMEM)
MESH = pl.DeviceIdType.MESH


def _cp(sem=None):
    if sem is None:
        return pltpu.CompilerParams(vmem_limit_bytes=VMEM_LIMIT)
    return pltpu.CompilerParams(dimension_semantics=sem, vmem_limit_bytes=VMEM_LIMIT)


def _dot(a, b):
    return jnp.dot(a, b, preferred_element_type=F32)


def _dot_nt(a, b):
    return lax.dot_general(a, b, (((1,), (1,)), ((), ())), preferred_element_type=F32)


def _dot_tn(a, b):
    return lax.dot_general(a, b, (((0,), (0,)), ((), ())), preferred_element_type=F32)


def _sds(shape, dtype):
    return jax.ShapeDtypeStruct(shape, dtype)


def _embed_norm(x, meta_full, w, dep=None):
    def body(x_ref, meta_ref, w_ref, dep_ref, h_ref, u_ref):
        i = pl.program_id(0)

        @pl.when(i == 0)
        def _():
            h_ref[...] = jnp.zeros_like(h_ref)
            h_ref[ROW_LO:X_LO, :] = meta_ref[...]

        @pl.when(i >= 1)
        def _():
            h_ref[...] = x_ref[...]

        h = h_ref[...]
        r = lax.rsqrt(jnp.mean(h * h, axis=-1, keepdims=True) + EPS)
        u_ref[...] = ((h * r) * w_ref[...]).astype(BF16)

    return pl.pallas_call(
        body, name="embed_norm1", grid=(NXT,),
        in_specs=[pl.BlockSpec((XT, D), lambda i: (jnp.maximum(i - 1, 0), 0)),
                  pl.BlockSpec((N_META, D), lambda i: (0, 0)),
                  pl.BlockSpec((1, D), lambda i: (0, 0)), ANY],
        out_specs=[pl.BlockSpec((XT, D), lambda i: (i, 0)), pl.BlockSpec((XT, D), lambda i: (i, 0))],
        out_shape=[_sds((TP, D), F32), _sds((TP, D), BF16)],
        compiler_params=_cp(("arbitrary",)),
    )(x, meta_full, w, _dep(dep))


def _norm_rows(h, w, name):
    tr = 272

    def body(h_ref, w_ref, o_ref):
        hv = h_ref[...]
        r = lax.rsqrt(jnp.mean(hv * hv, axis=-1, keepdims=True) + EPS)
        o_ref[...] = ((hv * r) * w_ref[...]).astype(BF16)

    return pl.pallas_call(
        body, name=name, grid=(TP // tr,),
        in_specs=[pl.BlockSpec((tr, D), lambda i: (i, 0)), pl.BlockSpec((1, D), lambda i: (0, 0))],
        out_specs=pl.BlockSpec((tr, D), lambda i: (i, 0)),
        out_shape=_sds((TP, D), BF16),
        compiler_params=_cp(("arbitrary",)),
    )(h, w)


def _loss_head(h2, target, fw):
    def body(h_ref, t_ref, w_ref, dh_ref, dhb_ref, dw_ref, loss_ref):
        i = pl.program_id(0)

        @pl.when(i == 0)
        def _():
            dw_ref[...] = jnp.zeros_like(dw_ref)
            loss_ref[...] = jnp.zeros_like(loss_ref)

        h = h_ref[...]
        w = w_ref[...]
        r = lax.rsqrt(jnp.mean(h * h, axis=-1, keepdims=True) + EPS)
        xh = h * r
        y = xh * w
        is_x = (i >= 1).astype(F32)
        diff = (y - t_ref[...]) * is_x
        loss_ref[...] += jnp.sum(diff * diff) * (0.5 / D)
        dy = diff * (1.0 / D)
        dw_ref[...] += jnp.sum(dy * xh, axis=0, keepdims=True)
        gx = dy * w
        dh = r * (gx - xh * jnp.mean(gx * xh, axis=-1, keepdims=True))
        dh_ref[...] = dh
        dhb_ref[...] = dh.astype(BF16)

    return pl.pallas_call(
        body, name="loss_head", grid=(NXT,),
        in_specs=[pl.BlockSpec((XT, D), lambda i: (i, 0)),
                  pl.BlockSpec((XT, D), lambda i: (jnp.maximum(i - 1, 0), 0)),
                  pl.BlockSpec((1, D), lambda i: (0, 0))],
        out_specs=[pl.BlockSpec((XT, D), lambda i: (i, 0)), pl.BlockSpec((XT, D), lambda i: (i, 0)),
                   pl.BlockSpec((1, D), lambda i: (0, 0)), pl.BlockSpec((8, 128), lambda i: (0, 0))],
        out_shape=[_sds((TP, D), F32), _sds((TP, D), BF16), _sds((1, D), F32), _sds((8, 128), F32)],
        compiler_params=_cp(("arbitrary",)),
    )(h2, target, fw)


def _norm_bwd(dn, h, dres, w, name, dep=None):
    tr = 272

    def body(dn_ref, h_ref, dres_ref, w_ref, dep_ref, o_ref, ob_ref, dw_ref):
        @pl.when(pl.program_id(0) == 0)
        def _():
            dw_ref[...] = jnp.zeros_like(dw_ref)

        hv = h_ref[...]
        dnv = dn_ref[...]
        r = lax.rsqrt(jnp.mean(hv * hv, axis=-1, keepdims=True) + EPS)
        xh = hv * r
        dw_ref[...] += jnp.sum(dnv * xh, axis=0, keepdims=True)
        gx = dnv * w_ref[...]
        dh = dres_ref[...] + r * (gx - xh * jnp.mean(gx * xh, axis=-1, keepdims=True))
        o_ref[...] = dh
        ob_ref[...] = dh.astype(BF16)

    row = pl.BlockSpec((tr, D), lambda i: (i, 0))
    vec = pl.BlockSpec((1, D), lambda i: (0, 0))
    return pl.pallas_call(
        body, name=name, grid=(TP // tr,),
        in_specs=[row, row, row, vec, ANY], out_specs=[row, row, vec],
        out_shape=[_sds((TP, D), F32), _sds((TP, D), BF16), _sds((1, D), F32)],
        compiler_params=_cp(("arbitrary",)),
    )(dn, h, dres, w, _dep(dep))


def _input_grad(du, h0, dh1, w, dep=None):
    def body(du_ref, h_ref, dres_ref, w_ref, dep_ref, gx_ref, gm_ref, dw_ref):
        i = pl.program_id(0)

        @pl.when(i == 0)
        def _():
            dw_ref[...] = jnp.zeros_like(dw_ref)

        hv = h_ref[...]
        dnv = du_ref[...]
        r = lax.rsqrt(jnp.mean(hv * hv, axis=-1, keepdims=True) + EPS)
        xh = hv * r
        dw_ref[...] += jnp.sum(dnv * xh, axis=0, keepdims=True)
        g = dnv * w_ref[...]
        dh = dres_ref[...] + r * (g - xh * jnp.mean(g * xh, axis=-1, keepdims=True))

        @pl.when(i == 0)
        def _():
            gm_ref[...] = dh[ROW_LO:X_LO, :]

        @pl.when(i >= 1)
        def _():
            gx_ref[...] = dh

    row = pl.BlockSpec((XT, D), lambda i: (i, 0))
    vec = pl.BlockSpec((1, D), lambda i: (0, 0))
    return pl.pallas_call(
        body, name="input_grad", grid=(NXT,),
        in_specs=[row, row, row, vec, ANY],
        out_specs=[pl.BlockSpec((XT, D), lambda i: (jnp.maximum(i - 1, 0), 0)),
                   pl.BlockSpec((N_META, D), lambda i: (0, 0)), vec],
        out_shape=[_sds((SEQ, D), F32), _sds((N_META, D), F32), _sds((1, D), F32)],
        compiler_params=_cp(("arbitrary",)),
    )(du, h0, dh1, w, _dep(dep))


def _in_proj(u, wg):
    def body(u_ref, w_ref, o_ref):
        o_ref[0] = _dot(u_ref[...], w_ref[0])

    return pl.pallas_call(
        body, name="in_proj", grid=(N_CHIP,),
        in_specs=[VMEM_FULL, pl.BlockSpec((1, D, PAD_IN), lambda k: (k, 0, 0))],
        out_specs=pl.BlockSpec((1, TP, PAD_IN), lambda k: (k, 0, 0)),
        out_shape=_sds((N_CHIP, TP, PAD_IN), F32),
        compiler_params=_cp(("arbitrary",)),
    )(u, wg)


def _out_proj(og, op, wout, h0, dep=None):
    tn = 512

    def body(og_ref, op_ref, w_ref, h_ref, dep_ref, o_ref):
        acc = _dot(og_ref[...], w_ref[0:GW, :]) + _dot(op_ref[...], w_ref[GW:D, :])
        o_ref[...] = h_ref[...] + acc

    return pl.pallas_call(
        body, name="out_proj", grid=(D // tn,),
        in_specs=[VMEM_FULL, VMEM_FULL, pl.BlockSpec((D, tn), lambda j: (0, j)),
                  pl.BlockSpec((TP, tn), lambda j: (0, j)), ANY],
        out_specs=pl.BlockSpec((TP, tn), lambda j: (0, j)),
        out_shape=_sds((TP, D), F32),
        compiler_params=_cp(("arbitrary",)),
    )(og, op, wout, h0, _dep(dep))


def _mlp_up(n2, w1g, part, prev=None, dep=None):
    tn = 1024
    per = D // tn

    def body(n_ref, w_ref, dep_ref, *rest):
        zr_ref, a_ref, token = rest[-3:]
        z = jnp.maximum(_dot(n_ref[...], w_ref[0]), 0.0)
        zr_ref[...] = z.astype(BF16)
        a_ref[...] = (z * z).astype(BF16)
        token[...] = jnp.zeros_like(token)

    col = pl.BlockSpec((TP, tn), lambda k, j: (0, (2 * part + k) * per + j))
    return pl.pallas_call(
        body, name="mlp_up_%d" % part, grid=(N_CHIP // 2, per),
        in_specs=[VMEM_FULL, pl.BlockSpec((1, D, tn), lambda k, j: (2 * part + k, 0, j)), ANY] + ([ANY, ANY] if prev else []),
        out_specs=[col, col, pl.BlockSpec((8, 128), lambda k, j: (0, 0))],
        out_shape=[_sds((TP, DFF), BF16), _sds((TP, DFF), BF16), _sds((8, 128), F32)],
        input_output_aliases={3: 0, 4: 1} if prev else {},
        compiler_params=_cp(("arbitrary", "arbitrary")),
    )(n2, w1g, _dep(dep), *(prev or ()))


def _mlp_down(a, w2, h1):
    tk = 1024
    nk = DFF // tk

    def body(a_ref, w_ref, h_ref, o_ref, acc_ref):
        k = pl.program_id(0)

        @pl.when(k == 0)
        def _():
            pltpu.sync_copy(h_ref, acc_ref)

        acc_ref[...] += _dot(a_ref[...], w_ref[...])

        @pl.when(k == nk - 1)
        def _():
            pltpu.sync_copy(acc_ref, o_ref)

    return pl.pallas_call(
        body, name="mlp_down", grid=(nk,),
        in_specs=[pl.BlockSpec((TP, tk), lambda k: (0, k)), pl.BlockSpec((tk, D), lambda k: (k, 0)), ANY],
        out_specs=ANY,
        out_shape=_sds((TP, D), F32),
        scratch_shapes=[pltpu.VMEM((TP, D), F32)],
        compiler_params=_cp(("arbitrary",)),
    )(a, w2, h1)


def _mlp_dz(dh2b, w2, zr, dep=None):
    tn = 1024

    def body(d_ref, w_ref, z_ref, dep_ref, o_ref):
        da = _dot_nt(d_ref[...], w_ref[...])
        o_ref[...] = (da * (2.0 * z_ref[...].astype(F32))).astype(BF16)

    col = pl.BlockSpec((TP, tn), lambda j: (0, j))
    return pl.pallas_call(
        body, name="mlp_dz", grid=(DFF // tn,),
        in_specs=[VMEM_FULL, pl.BlockSpec((tn, D), lambda j: (j, 0)), col, ANY],
        out_specs=col,
        out_shape=_sds((TP, DFF), BF16),
        compiler_params=_cp(("arbitrary",)),
    )(dh2b, w2, zr, _dep(dep))


def _grad_w2(a, dh2b):
    tm = 1024

    def body(a_ref, d_ref, o_ref):
        o_ref[...] = _dot_tn(a_ref[...], d_ref[...])

    return pl.pallas_call(
        body, name="grad_w2", grid=(DFF // tm,),
        in_specs=[pl.BlockSpec((TP, tm), lambda j: (0, j)), VMEM_FULL],
        out_specs=pl.BlockSpec((tm, D), lambda j: (j, 0)),
        out_shape=_sds((DFF, D), F32),
        compiler_params=_cp(("arbitrary",)),
    )(a, dh2b)


def _dep(token):
    return jnp.zeros((8, 128), F32) if token is None else token


def _grad_w1(n2, dz, dep=None):
    tn = 1024
    per = D // tn

    def body(n_ref, d_ref, dep_ref, o_ref):
        o_ref[0] = _dot_tn(n_ref[...], d_ref[...])

    return pl.pallas_call(
        body, name="grad_w1", grid=(N_CHIP, per),
        in_specs=[VMEM_FULL, pl.BlockSpec((TP, tn), lambda k, j: (0, k * per + j)), ANY],
        out_specs=pl.BlockSpec((1, D, tn), lambda k, j: (k, 0, j)),
        out_shape=_sds((N_CHIP, D, D), F32),
        compiler_params=_cp(("arbitrary", "arbitrary")),
    )(n2, dz, _dep(dep))


def _mlp_dn(dz, w1g, dep=None):
    tk = 1024
    per = D // tk
    nk = DFF // tk

    def body(d_ref, w_ref, dep_ref, o_ref, acc_ref):
        k = pl.program_id(0)
        part = _dot_nt(d_ref[...], w_ref[0])

        @pl.when(k == 0)
        def _():
            acc_ref[...] = part

        @pl.when(k > 0)
        def _():
            acc_ref[...] += part

        @pl.when(k == nk - 1)
        def _():
            pltpu.sync_copy(acc_ref, o_ref)

    return pl.pallas_call(
        body, name="mlp_dn", grid=(nk,),
        in_specs=[pl.BlockSpec((TP, tk), lambda k: (0, k)),
                  pl.BlockSpec((1, D, tk), lambda k: (k // per, 0, k % per)), ANY],
        out_specs=ANY,
        out_shape=_sds((TP, D), F32),
        scratch_shapes=[pltpu.VMEM((TP, D), F32)],
        compiler_params=_cp(("arbitrary",)),
    )(dz, w1g, _dep(dep))


def _mixed_grad(dh1b, wout):
    tn = 512

    def body(d_ref, w_ref, o_ref):
        o_ref[...] = _dot_nt(d_ref[...], w_ref[...])

    return pl.pallas_call(
        body, name="mixed_grad", grid=(D // tn,),
        in_specs=[VMEM_FULL, pl.BlockSpec((tn, D), lambda j: (j, 0))],
        out_specs=pl.BlockSpec((TP, tn), lambda j: (0, j)),
        out_shape=_sds((TP, D), F32),
        compiler_params=_cp(("arbitrary",)),
    )(dh1b, wout)


def _grad_wout(og, op, dh1b):
    tm = 512

    def body(og_ref, op_ref, d_ref, o_ref):
        j = pl.program_id(0)

        @pl.when(j < 2)
        def _():
            o_ref[0] = _dot_tn(og_ref[...], d_ref[...])

        @pl.when(j >= 2)
        def _():
            o_ref[0] = _dot_tn(op_ref[...], d_ref[...])

    return pl.pallas_call(
        body, name="grad_wout", grid=(N_CHIP,),
        in_specs=[pl.BlockSpec((TP, tm), lambda j: (0, jnp.minimum(j, 1))),
                  pl.BlockSpec((TP, tm), lambda j: (0, jnp.maximum(j - 2, 0))), VMEM_FULL],
        out_specs=pl.BlockSpec((1, tm, D), lambda j: (j, 0, 0)),
        out_shape=_sds((N_CHIP, tm, D), F32),
        compiler_params=_cp(("arbitrary",)),
    )(og, op, dh1b)


def _in_grad(dq, dk, dv, dr, dglr, dpu, wg, dep=None):
    def body(dq_ref, dk_ref, dv_ref, dr_ref, dg_ref, dpu_ref, w_ref, dep_ref, o_ref):
        dv, dr, dg = dv_ref[...], dr_ref[...], dg_ref[...]
        head, tail = slice(0, GW), slice(GW, PAD_IN)
        o_ref[...] = (_dot_nt(dq_ref[...], w_ref[0, :, 0:KW]) + _dot_nt(dk_ref[...], w_ref[0, :, KW:GW])
                      + _dot_nt(dv[:, 0:128], w_ref[0, :, tail])
                      + _dot_nt(dv, w_ref[1, :, head]) + _dot_nt(dr[:, 0:128], w_ref[1, :, tail])
                      + _dot_nt(dr, w_ref[2, :, head]) + _dot_nt(dg, w_ref[2, :, tail])
                      + _dot_nt(dpu_ref[...], w_ref[3, :, head]) + _dot_nt(dg, w_ref[3, :, tail]))

    tn = 512
    return pl.pallas_call(
        body, name="in_grad", grid=(D // tn,),
        in_specs=[VMEM_FULL] * 6 + [pl.BlockSpec((N_CHIP, tn, PAD_IN), lambda j: (0, j, 0)), ANY],
        out_specs=pl.BlockSpec((TP, tn), lambda j: (0, j)),
        out_shape=_sds((TP, D), F32),
        compiler_params=_cp(("arbitrary",)),
    )(dq, dk, dv, dr, dglr, dpu, wg, _dep(dep))


def _grad_win(u, dq, dk, dv, dr, dglr, dpu, dep=None):
    tm = 512

    def body(u_ref, dq_hbm, dk_hbm, dv_hbm, dr_hbm, dg_hbm, dpu_hbm, dep_ref, o_ref, dp_ref, sem):
        k, m = pl.program_id(0), pl.program_id(1)
        head, tail = slice(0, GW), slice(GW, PAD_IN)
        pieces = [[(dq_hbm, slice(0, KW)), (dk_hbm, slice(KW, GW)), (dv_hbm.at[:, 0:128], tail)],
                  [(dv_hbm, head), (dr_hbm.at[:, 0:128], tail)],
                  [(dr_hbm, head), (dg_hbm, tail)],
                  [(dpu_hbm, head), (dg_hbm, tail)]]

        def copies(kk):
            return [pltpu.make_async_copy(src, dp_ref.at[kk % 2, :, cols], sem.at[kk % 2, i])
                    for i, (src, cols) in enumerate(pieces[kk])]

        @pl.when((k == 0) & (m == 0))
        def _():
            for cp in copies(0):
                cp.start()

        for kk in range(N_CHIP):
            @pl.when((k == kk) & (m == 0))
            def _(kk=kk):
                for cp in copies(kk):
                    cp.wait()
                if kk + 1 < N_CHIP:
                    for cp in copies(kk + 1):
                        cp.start()

        g = _dot_tn(u_ref[...], dp_ref[k % 2])
        lane = lax.broadcasted_iota(jnp.int32, (tm, PAD_IN), 1)
        for kk in range(N_CHIP):
            @pl.when(k == kk)
            def _(kk=kk):
                if kk == 0:
                    nat = g
                elif kk < 3:
                    nat = pltpu.roll(g, PAD_IN - 4 * kk, 1)
                else:
                    nat = jnp.where(lane < 4, pltpu.roll(g, PAD_IN - (GW + 12), 1), pltpu.roll(g, 4, 1))
                o_ref[0] = nat[:, 0:SHARD_IN]

    return pl.pallas_call(
        body, name="grad_win", grid=(N_CHIP, D // tm),
        in_specs=[pl.BlockSpec((TP, tm), lambda k, m: (0, m))] + [ANY] * 7,
        out_specs=pl.BlockSpec((1, tm, SHARD_IN), lambda k, m: (k, m, 0)),
        out_shape=_sds((N_CHIP, D, SHARD_IN), F32),
        scratch_shapes=[pltpu.VMEM((2, TP, PAD_IN), BF16), pltpu.SemaphoreType.DMA((2, 3))],
        compiler_params=_cp(("arbitrary", "arbitrary")),
    )(u, dq, dk, dv, dr, dglr, dpu, _dep(dep))


def _split3(x):
    hi = x.astype(BF16)
    r1 = x - hi.astype(F32)
    mid = r1.astype(BF16)
    lo = (r1 - mid.astype(F32)).astype(BF16)
    return hi, mid, lo


def _tri_sum(tri, x):
    hi, mid, lo = _split3(x)
    return _dot(tri, hi) + _dot(tri, mid) + _dot(tri, lo)


def _gla_common(n, glr, gw2, gb):
    rows = n * CH + lax.broadcasted_iota(jnp.int32, (CH, 1), 0)
    valid = (rows >= ROW_LO) & (rows < ROW_HI)
    g_raw = _dot(glr.astype(BF16), gw2.astype(BF16)) + gb
    logsig = jnp.minimum(g_raw, 0.0) - jnp.log(1.0 + jnp.exp(-jnp.abs(g_raw)))
    logg = jnp.where(valid, logsig * (1.0 / TAU), 0.0)
    ci = lax.broadcasted_iota(jnp.int32, (CH, CH), 0)
    si = lax.broadcasted_iota(jnp.int32, (CH, CH), 1)
    lower = ci >= si
    G = _tri_sum(lower.astype(BF16), logg)
    Gl = G[CH - 1:CH, :]
    return valid, g_raw, lower, G, Gl


GSUB = 2


def _p_specs(index):
    def spec(width, shard, col):
        return pl.BlockSpec((1, GSUB * CH, width), lambda s: (shard, index(s), col))

    return [spec(KW, 0, 0), spec(KW, 0, 1), spec(GW, 1, 0), spec(128, 0, 8), spec(GW, 2, 0), spec(128, 1, 8),
            spec(128, 2, 8), spec(128, 3, 8)]


def _p_load(q_ref, k_ref, vm_ref, vh_ref, rm_ref, rh_ref, ga_ref, gb_ref):
    def joined(main, head):
        return jnp.concatenate([main[:, 0:128] + head, main[:, 128:]], axis=1)

    return q_ref[0], k_ref[0], joined(vm_ref[0], vh_ref[0]), joined(rm_ref[0], rh_ref[0]), ga_ref[0] + gb_ref[0]


def _gla_fwd(P, gw2, gb, gnw, dep=None):
    scale = DK ** -0.5

    def body(p0, p1, p2, p3, p4, p5, p6, p7, gw2_ref, gb_ref, gnw_ref, dep_ref, o_ref, og_ref, sp_ref, st_ref):
        n = pl.program_id(0)

        @pl.when(n == 0)
        def _():
            st_ref[...] = jnp.zeros_like(st_ref)

        q_blk, k_blk, v_blk, r_blk, glr_blk = _p_load(p0, p1, p2, p3, p4, p5, p6, p7)
        gnw_v = gnw_ref[...]
        for sub in range(GSUB):
            rows = slice(sub * CH, (sub + 1) * CH)
            _, _, lower, G, Gl = _gla_common(GSUB * n + sub, glr_blk[rows], gw2_ref[...], gb_ref[...])
            eG = jnp.exp(G)
            eN = jnp.exp(-G)
            eE = jnp.exp(Gl - G)
            dec = jnp.exp(Gl)
            for h in range(HEADS):
                ks = slice(h * DK, (h + 1) * DK)
                vs = slice(h * DV, (h + 1) * DV)
                kh = k_blk[rows, ks]
                vh = v_blk[rows, vs].astype(BF16)
                qd = ((q_blk[rows, ks] * scale) * eG[:, ks]).astype(BF16)
                ki = (kh * eN[:, ks]).astype(BF16)
                ke = (kh * eE[:, ks]).astype(BF16)
                st = st_ref[h]
                a = jnp.where(lower, _dot_nt(qd, ki), 0.0).astype(BF16)
                o = _dot(a, vh) + _dot_nt(qd, st.astype(BF16))
                sp_ref[sub, h] = st
                st_ref[h] = st * dec[:, ks] + _dot_tn(vh, ke)
                o_ref[rows, vs] = o
                rs = lax.rsqrt(jnp.mean(o * o, axis=-1, keepdims=True) + EPS)
                rv = r_blk[rows, vs]
                gate = rv / (1.0 + jnp.exp(-rv))
                og_ref[rows, vs] = (((o * rs) * gnw_v) * gate).astype(BF16)

    rv_ = pl.BlockSpec((GSUB * CH, GW), lambda n: (n, 0))

    def full(shape):
        return pl.BlockSpec(shape, lambda n: tuple(0 for _ in shape))

    return pl.pallas_call(
        body, name="gla_fwd", grid=(NCH // GSUB,),
        in_specs=_p_specs(lambda n: n) + [full((128, KW)), full((1, KW)), full((1, DV)), ANY],
        out_specs=[rv_, rv_, pl.BlockSpec((GSUB, HEADS, DV, DK), lambda n: (n, 0, 0, 0))],
        out_shape=[_sds((TP, GW), F32), _sds((TP, GW), BF16), _sds((NCH, HEADS, DV, DK), F32)],
        scratch_shapes=[pltpu.VMEM((HEADS, DV, DK), F32)],
        compiler_params=_cp(("arbitrary",)),
    )(*([P] * 8), gw2, gb, gnw, _dep(dep))


def _gla_bwd(dog, o, P, gw2, gb, gnw, sp, dep=None):
    scale = DK ** -0.5

    def body(dog_ref, o_ref, p0, p1, p2, p3, p4, p5, p6, p7, gw2_ref, gb_ref, gnw_ref, sp_ref, dep_ref,
             dq_ref, dk_ref, dv_ref, dr_ref, dglr_ref, dgw2_ref, dgb_ref, dgnw_ref, ds_ref):
        step = pl.program_id(0)
        blk = NCH // GSUB - 1 - step

        @pl.when(step == 0)
        def _():
            ds_ref[...] = jnp.zeros_like(ds_ref)
            dgw2_ref[...] = jnp.zeros_like(dgw2_ref)
            dgb_ref[...] = jnp.zeros_like(dgb_ref)
            dgnw_ref[...] = jnp.zeros_like(dgnw_ref)

        q_blk, k_blk, v_blk, r_blk, glr_blk = _p_load(p0, p1, p2, p3, p4, p5, p6, p7)
        gw2_b = gw2_ref[...].astype(BF16)
        upper = lax.broadcasted_iota(jnp.int32, (CH, CH), 0) <= lax.broadcasted_iota(jnp.int32, (CH, CH), 1)
        gnw_v = gnw_ref[...]
        last = lax.broadcasted_iota(jnp.int32, (CH, 1), 0) == CH - 1
        for sub in reversed(range(GSUB)):
            rows = slice(sub * CH, (sub + 1) * CH)
            glr_v = glr_blk[rows]
            valid, g_raw, lower, G, Gl = _gla_common(GSUB * blk + sub, glr_v, gw2_ref[...], gb_ref[...])
            eG = jnp.exp(G)
            eN = jnp.exp(-G)
            eE = jnp.exp(Gl - G)
            dec = jnp.exp(Gl)
            dgnw_acc = jnp.zeros((1, DV), F32)
            dG_parts = []
            for h in range(HEADS):
                ks = slice(h * DK, (h + 1) * DK)
                vs = slice(h * DV, (h + 1) * DV)
                oh = o_ref[rows, vs]
                rv = r_blk[rows, vs]
                dg = dog_ref[rows, vs]
                sig = 1.0 / (1.0 + jnp.exp(-rv))
                gate = rv * sig
                rs = lax.rsqrt(jnp.mean(oh * oh, axis=-1, keepdims=True) + EPS)
                ohat = oh * rs
                dr_ref[rows, vs] = ((dg * (ohat * gnw_v)) * (sig * (1.0 + rv * (1.0 - sig)))).astype(BF16)
                don = dg * gate
                dgnw_acc = dgnw_acc + jnp.sum(don * ohat, axis=0, keepdims=True)
                gxn = don * gnw_v
                do = (rs * (gxn - ohat * jnp.mean(gxn * ohat, axis=-1, keepdims=True))).astype(BF16)
                kh = k_blk[rows, ks]
                vh = v_blk[rows, vs].astype(BF16)
                qd_f = (q_blk[rows, ks] * scale) * eG[:, ks]
                ki_f = kh * eN[:, ks]
                ke_f = kh * eE[:, ks]
                qd, ki, ke = qd_f.astype(BF16), ki_f.astype(BF16), ke_f.astype(BF16)
                spt = sp_ref[sub, h]
                dst = ds_ref[h]
                dst_b = dst.astype(BF16)
                a_t = jnp.where(upper, _dot_nt(ki, qd), 0.0).astype(BF16)
                da = jnp.where(lower, _dot_nt(do, vh), 0.0).astype(BF16)
                da_t = jnp.where(upper, _dot_nt(vh, do), 0.0).astype(BF16)
                dv_ref[rows, vs] = (_dot(a_t, do) + _dot_nt(ke, dst_b)).astype(BF16)
                dqd = _dot(da, ki) + _dot(do, spt.astype(BF16))
                dki = _dot(da_t, qd)
                dke = _dot(vh, dst_b)
                ddec = jnp.sum(spt * dst, axis=0, keepdims=True)
                ds_ref[h] = dst * dec[:, ks] + _dot_tn(do, qd)
                dq_ref[rows, ks] = ((dqd * eG[:, ks]) * scale).astype(BF16)
                dk_ref[rows, ks] = (dki * eN[:, ks] + dke * eE[:, ks]).astype(BF16)
                dke_ke = dke * ke_f
                dG = dqd * qd_f - dki * ki_f - dke_ke
                dGl = jnp.sum(dke_ke, axis=0, keepdims=True) + ddec * dec[:, ks]
                dG_parts.append(dG + jnp.where(last, dGl, 0.0))
            dgnw_ref[...] += dgnw_acc
            dG_all = jnp.concatenate(dG_parts, axis=1)
            dlogg = jnp.where(valid, _tri_sum(upper.astype(BF16), dG_all), 0.0)
            dg_raw = (dlogg * (1.0 / TAU)) * (1.0 / (1.0 + jnp.exp(g_raw)))
            dgb_ref[...] += jnp.sum(dg_raw, axis=0, keepdims=True)
            dg_b = dg_raw.astype(BF16)
            dgw2_ref[...] += _dot_tn(glr_v.astype(BF16), dg_b)
            dglr_ref[rows, :] = _dot_nt(dg_b, gw2_b).astype(BF16)

    def back(s):
        return NCH // GSUB - 1 - s

    rk = pl.BlockSpec((GSUB * CH, KW), lambda s: (back(s), 0))
    rv_ = pl.BlockSpec((GSUB * CH, GW), lambda s: (back(s), 0))
    rg = pl.BlockSpec((GSUB * CH, 128), lambda s: (back(s), 0))

    def full(shape):
        return pl.BlockSpec(shape, lambda s: tuple(0 for _ in shape))

    return pl.pallas_call(
        body, name="gla_bwd", grid=(NCH // GSUB,),
        in_specs=[rv_, rv_] + _p_specs(back) + [full((128, KW)), full((1, KW)), full((1, DV)),
                  pl.BlockSpec((GSUB, HEADS, DV, DK), lambda s: (back(s), 0, 0, 0)), ANY],
        out_specs=[rk, rk, rv_, rv_, rg, full((128, KW)), full((1, KW)), full((1, DV))],
        out_shape=[_sds((TP, KW), BF16), _sds((TP, KW), BF16), _sds((TP, GW), BF16), _sds((TP, GW), BF16),
                   _sds((TP, 128), BF16), _sds((128, KW), F32), _sds((1, KW), F32), _sds((1, DV), F32)],
        scratch_shapes=[pltpu.VMEM((HEADS, DV, DK), F32)],
        compiler_params=_cp(("arbitrary",)),
    )(dog, o, *([P] * 8), gw2, gb, gnw, sp, _dep(dep))


POOL_TR = 128
HALO = 16


def _pool_counts(base, nrows):
    rows = base + lax.broadcasted_iota(jnp.int32, (nrows, 1), 0)
    valid = (rows >= ROW_LO) & (rows < ROW_HI)
    t1 = (rows - ROW_LO + 1).astype(F32)
    cnts = [jnp.clip(t1, 1.0, float(w)) for w in WINDOWS]
    return valid, cnts


def _pool_fwd(P, pw, ps, dep=None):
    def body(cur_ref, prev_ref, pw_ref, ps_ref, dep_ref, y_ref, op_ref):
        i = pl.program_id(0)
        cur = cur_ref[0]
        full = jnp.concatenate([prev_ref[0], cur], axis=0)
        s2 = full + pltpu.roll(full, 1, 0)
        s4 = s2 + pltpu.roll(s2, 2, 0)
        s8 = s4 + pltpu.roll(s4, 4, 0)
        s16 = s8 + pltpu.roll(s8, 8, 0)
        valid, cnts = _pool_counts(i * POOL_TR, POOL_TR)
        for g, s in enumerate((s2, s4, s8, s16)):
            cs = slice(g * GC, (g + 1) * GC)
            y = s[HALO:, cs] / cnts[g] - cur[:, cs]
            yb = jnp.where(valid, y, 0.0).astype(BF16)
            y_ref[:, cs] = yb
            op_ref[:, cs] = (_dot(yb, pw_ref[g].astype(BF16)) * ps_ref[:, cs]).astype(BF16)

    row = pl.BlockSpec((POOL_TR, PW), lambda i: (i, 0))
    per = POOL_TR // HALO
    return pl.pallas_call(
        body, name="pool_fwd", grid=(TP // POOL_TR,),
        in_specs=[pl.BlockSpec((1, POOL_TR, PW), lambda i: (3, i, 0)),
                  pl.BlockSpec((1, HALO, PW), lambda i: (3, jnp.maximum(i * per - 1, 0), 0)),
                  pl.BlockSpec((4, GC, GC), lambda i: (0, 0, 0)), pl.BlockSpec((1, PW), lambda i: (0, 0)), ANY],
        out_specs=[row, row],
        out_shape=[_sds((TP, PW), BF16), _sds((TP, PW), BF16)],
        compiler_params=_cp(("arbitrary",)),
    )(P, P, pw, ps, _dep(dep))


def _pool_bwd(dop, y, pw, ps, dep=None):
    nblk = TP // HALO

    def body(cur_ref, nxt_ref, y_ref, pw_ref, ps_ref, dep_ref, dpu_ref, dpw_ref, dps_ref):
        i = pl.program_id(0)

        @pl.when(i == 0)
        def _():
            dpw_ref[...] = jnp.zeros_like(dpw_ref)
            dps_ref[...] = jnp.zeros_like(dps_ref)

        n_all = POOL_TR + HALO
        dcur = cur_ref[...]
        dall = jnp.concatenate([dcur, nxt_ref[...]], axis=0)
        valid, cnts = _pool_counts(i * POOL_TR, n_all)
        for g in range(4):
            cs = slice(g * GC, (g + 1) * GC)
            pwb = pw_ref[g].astype(BF16)
            yb = y_ref[:, cs]
            dyw = (dall[:, cs] * ps_ref[:, cs]).astype(BF16)
            dps_ref[:, cs] += jnp.sum(dcur[:, cs] * _dot(yb, pwb), axis=0, keepdims=True)
            dpw_ref[g] += _dot_tn(yb, dyw[0:POOL_TR, :])
            dyv = jnp.where(valid, _dot_nt(dyw, pwb), 0.0)
            e = dyv / cnts[g]
            w = WINDOWS[g]
            sh = 1
            while sh < w:
                e = e + pltpu.roll(e, n_all - sh, 0)
                sh *= 2
            dpu_ref[:, cs] = (e[0:POOL_TR, :] - dyv[0:POOL_TR, :]).astype(BF16)

    row = pl.BlockSpec((POOL_TR, PW), lambda i: (i, 0))
    per = POOL_TR // HALO
    return pl.pallas_call(
        body, name="pool_bwd", grid=(TP // POOL_TR,),
        in_specs=[pl.BlockSpec((POOL_TR, PW), lambda i: (i, 1)),
                  pl.BlockSpec((HALO, PW), lambda i: (jnp.minimum(i * per + per, nblk - 1), 1)),
                  row, pl.BlockSpec((4, GC, GC), lambda i: (0, 0, 0)), pl.BlockSpec((1, PW), lambda i: (0, 0)), ANY],
        out_specs=[row, pl.BlockSpec((4, GC, GC), lambda i: (0, 0, 0)), pl.BlockSpec((1, PW), lambda i: (0, 0))],
        out_shape=[_sds((TP, PW), BF16), _sds((4, GC, GC), F32), _sds((1, PW), F32)],
        compiler_params=_cp(("arbitrary",)),
    )(dop, dop, y, pw, ps, _dep(dep))


def _place():
    x, y, c = lax.axis_index("x"), lax.axis_index("y"), lax.axis_index("c")
    chips = [(1 - x, y), (x, 1 - y), (1 - x, 1 - y)]
    return x, y, c, chips


HBM = pl.BlockSpec(memory_space=pltpu.HBM)
SEM = pl.BlockSpec(memory_space=pltpu.SEMAPHORE)
EFFECT = pltpu.SideEffectType.DATAFLOW_SIDE_EFFECTING


def _cast_into(w, place, cols_out, name, dep=None):
    rows, cols = w.shape
    tr = 256

    def body(p_ref, w_ref, dep_ref, o_ref):
        if cols_out != cols:
            o_ref[0] = jnp.zeros((tr, cols_out), BF16)
            o_ref[0, :, 0:cols] = w_ref[...].astype(BF16)
        else:
            o_ref[0] = w_ref[...].astype(BF16)

    grid_spec = pltpu.PrefetchScalarGridSpec(
        num_scalar_prefetch=1, grid=(rows // tr,),
        in_specs=[pl.BlockSpec((tr, cols), lambda i, p: (i, 0)), ANY],
        out_specs=pl.BlockSpec((1, tr, cols_out), lambda i, p: (p[0], i, 0)))
    return pl.pallas_call(
        body, name=name, grid_spec=grid_spec,
        out_shape=_sds((N_CHIP, rows, cols_out), BF16),
        compiler_params=_cp(("arbitrary",)),
    )(place, w, _dep(dep))


def _cast_win(w, place, dep=None):
    rows, cols = w.shape
    tr = 256

    def body(p_ref, w_ref, dep_ref, o_ref, t_ref):
        t_ref[...] = jnp.zeros_like(t_ref)
        t_ref[:, 0:cols] = w_ref[...]
        t = t_ref[...]
        lane = lax.broadcasted_iota(jnp.int32, (tr, PAD_IN), 1)
        for kk in range(N_CHIP):
            @pl.when(p_ref[0] == kk)
            def _(kk=kk):
                if kk == 0:
                    placed = t
                elif kk < 3:
                    placed = pltpu.roll(t, 4 * kk, 1)
                else:
                    pool = pltpu.roll(t, PAD_IN - 4, 1)
                    gate = pltpu.roll(t, GW + 12, 1)
                    placed = jnp.where(lane < GW, pool, jnp.where((lane >= GW + 12) & (lane < GW + 16), gate, 0.0))
                o_ref[0] = placed.astype(BF16)

    grid_spec = pltpu.PrefetchScalarGridSpec(
        num_scalar_prefetch=1, grid=(rows // tr,),
        in_specs=[pl.BlockSpec((tr, cols), lambda i, p: (i, 0)), ANY],
        out_specs=pl.BlockSpec((1, tr, PAD_IN), lambda i, p: (p[0], i, 0)),
        scratch_shapes=[pltpu.VMEM((tr, PAD_IN), F32)])
    return pl.pallas_call(
        body, name="cast_win", grid_spec=grid_spec,
        out_shape=_sds((N_CHIP, rows, PAD_IN), BF16),
        compiler_params=_cp(("arbitrary",)),
    )(place, w, _dep(dep))


def _half_rows(ref, k, which):
    h = ref.shape[1] // 2
    return ref.at[k, pl.ds(pl.multiple_of(which * h, 8), h), :]


def _sent_rows(ref, k, which, whole):
    return ref.at[k] if whole else _half_rows(ref, k, which)


def _gather_start(ws, name, whole=None):
    n = len(ws)
    whole = whole or [False] * n

    def body(*refs):
        ins = refs[:n]
        ssems = refs[n:2 * n]
        rsems = refs[2 * n:3 * n]
        token = refs[4 * n]
        x, y, c, chips = _place()
        me = 2 * x + y
        for w in range(n):
            blk = _sent_rows(ins[w], me, c, whole[w])
            for j, chip in enumerate(chips):
                pltpu.make_async_remote_copy(src_ref=blk, dst_ref=blk, send_sem=ssems[w].at[j], recv_sem=rsems[w].at[j],
                                             device_id=(*chip, c), device_id_type=MESH).start()
        token[...] = jnp.zeros_like(token)

    sem3 = pltpu.SemaphoreType.DMA((3,))
    outs = pl.pallas_call(
        body, name=name,
        out_shape=tuple([sem3] * (2 * n) + [pltpu.HBM(w.shape, w.dtype) for w in ws] + [_sds((8, 128), F32)]),
        in_specs=(HBM,) * n, out_specs=(SEM,) * (2 * n) + (HBM,) * n + (VMEM_FULL,),
        input_output_aliases={w: 2 * n + w for w in range(n)},
        compiler_params=pltpu.CompilerParams(has_side_effects=EFFECT),
    )(*[pltpu.with_memory_space_constraint(w, pltpu.HBM) for w in ws])
    return outs[:n], outs[n:2 * n], outs[2 * n:3 * n], outs[3 * n]


def _gather_wait(w, ssem, rsem, after, name, whole=False):
    def body(w_ref, ssem_ref, rsem_ref, after_ref, out_ref):
        x, y, c, chips = _place()
        me = 2 * x + y
        mine = _sent_rows(w_ref, me, c, whole)
        for j, (cx, cy) in enumerate(chips):
            cp = pltpu.make_async_remote_copy(src_ref=mine, dst_ref=_sent_rows(w_ref, 2 * cx + cy, c, whole),
                                              send_sem=ssem_ref.at[j], recv_sem=rsem_ref.at[j],
                                              device_id=(cx, cy, c), device_id_type=MESH)
            cp.wait_send()
            cp.wait_recv()

    return pl.pallas_call(
        body, name=name, out_shape=pltpu.HBM(w.shape, w.dtype),
        in_specs=(HBM, SEM, SEM, ANY), out_specs=HBM, input_output_aliases={0: 0},
        compiler_params=pltpu.CompilerParams(has_side_effects=EFFECT),
    )(w, ssem, rsem, after)


def _gather_copies(ref, kind, ssem, rsem):
    x, y, c, _ = _place()
    xn, yn, sib = (1 - x, y, c), (x, 1 - y, c), (x, y, 1 - c)
    kx, ky, kd = 2 * (1 - x) + y, 2 * x + (1 - y), 2 * (1 - x) + (1 - y)
    half = ref.shape[1] // 2
    quarter = half // 2

    def piece(k, q):
        return ref.at[k, pl.ds(pl.multiple_of(c * half + q * quarter, 8), quarter), :]

    if kind == "d":
        blk = _half_rows(ref, 2 * x + y, c)
        pairs = [(blk, xn), (blk, yn)]
    elif kind == "r":
        pairs = [(piece(ky, 1), xn), (piece(kx, 0), yn)]
    elif kind == "fx":
        pairs = [(_half_rows(ref, kx, c), sib), (_half_rows(ref, ky, c), sib)]
    else:
        pairs = [(_half_rows(ref, kd, c), sib)]
    return [pltpu.make_async_remote_copy(src_ref=blk, dst_ref=blk, send_sem=ssem.at[i], recv_sem=rsem.at[i],
                                         device_id=to, device_id_type=MESH) for i, (blk, to) in enumerate(pairs)]


def _gather_step(name, arrs, waits, starts, sems_in=(), after=()):
    n, nw, ns = len(arrs), len(waits), len(starts)
    after = [a for a in after if a is not None] or [_dep(None)]

    def body(*refs):
        a_in = refs[:n]
        s_in = refs[n:n + 2 * nw]
        outs = refs[n + 2 * nw + len(after):]
        s_out = outs[:2 * ns]
        for i, (ai, kind) in enumerate(waits):
            for cp in _gather_copies(a_in[ai], kind, s_in[2 * i], s_in[2 * i + 1]):
                cp.wait_send()
                cp.wait_recv()
        for i, (ai, kind) in enumerate(starts):
            for cp in _gather_copies(a_in[ai], kind, s_out[2 * i], s_out[2 * i + 1]):
                cp.start()
        if ns:
            token = outs[2 * ns + n]
            token[...] = jnp.zeros_like(token)

    sem2 = pltpu.SemaphoreType.DMA((2,))
    flat_in = [s for pair in sems_in for s in pair]
    arrs = [pltpu.with_memory_space_constraint(a, pltpu.HBM) for a in arrs]
    outs = pl.pallas_call(
        body, name=name,
        out_shape=tuple([sem2] * (2 * ns) + [pltpu.HBM(a.shape, a.dtype) for a in arrs]
                        + ([_sds((8, 128), F32)] if ns else [])),
        in_specs=(HBM,) * n + (SEM,) * (2 * nw) + (ANY,) * len(after),
        out_specs=(SEM,) * (2 * ns) + (HBM,) * n + ((VMEM_FULL,) if ns else ()),
        input_output_aliases={i: 2 * ns + i for i in range(n)},
        compiler_params=pltpu.CompilerParams(has_side_effects=EFFECT),
    )(*arrs, *flat_in, *after)
    sems = [(outs[2 * i], outs[2 * i + 1]) for i in range(ns)]
    return sems, list(outs[2 * ns:2 * ns + n]), (outs[2 * ns + n] if ns else None)


def _rs_start(sb, name, after=None):
    _, half, cols = sb.shape

    def body(sb_ref, land_ref, after_ref, ssem, rsem, sb_out, land_out, token):
        x, y, c, chips = _place()
        for j, (cx, cy) in enumerate(chips):
            pltpu.make_async_remote_copy(src_ref=sb_ref.at[2 * cx + cy], dst_ref=land_ref.at[j], send_sem=ssem.at[j],
                                         recv_sem=rsem.at[j], device_id=(cx, cy, c), device_id_type=MESH).start()
        token[...] = jnp.zeros_like(token)

    sem3 = pltpu.SemaphoreType.DMA((3,))
    land = lax.empty((3, half, cols), BF16)
    return pl.pallas_call(
        body, name=name,
        out_shape=(sem3, sem3, pltpu.HBM(sb.shape, sb.dtype), pltpu.HBM(land.shape, land.dtype), _sds((8, 128), F32)),
        in_specs=(HBM, HBM, ANY), out_specs=(SEM, SEM, HBM, HBM, VMEM_FULL), input_output_aliases={0: 2, 1: 3},
        compiler_params=pltpu.CompilerParams(has_side_effects=EFFECT),
    )(pltpu.with_memory_space_constraint(sb, pltpu.HBM), pltpu.with_memory_space_constraint(land, pltpu.HBM), _dep(after))


def _rs_wait(items, after, name):
    n = len(items)

    def body(*refs):
        x, y, c, chips = _place()
        for i in range(n):
            sb_ref, land_ref, ssem_ref, rsem_ref = refs[4 * i:4 * i + 4]
            for j, (cx, cy) in enumerate(chips):
                cp = pltpu.make_async_remote_copy(src_ref=sb_ref.at[2 * cx + cy], dst_ref=land_ref.at[j],
                                                  send_sem=ssem_ref.at[j], recv_sem=rsem_ref.at[j],
                                                  device_id=(cx, cy, c), device_id_type=MESH)
                cp.wait_send()
                cp.wait_recv()

    outs = pl.pallas_call(
        body, name=name,
        out_shape=tuple(pltpu.HBM(a.shape, a.dtype) for it in items for a in it[:2]),
        in_specs=(HBM, HBM, SEM, SEM) * n + (ANY,), out_specs=(HBM,) * (2 * n),
        input_output_aliases={4 * i + k: 2 * i + k for i in range(n) for k in range(2)},
        compiler_params=pltpu.CompilerParams(has_side_effects=EFFECT),
    )(*[a for it in items for a in it], after)
    return [outs[2 * i + 1] for i in range(n)]


def _pair_copy(g_ref, land_ref, ssem, rsem):
    x, y, c, _ = _place()
    h = g_ref.shape[1] // 2
    src = g_ref.at[:, pl.ds(pl.multiple_of((1 - c) * h, 8), h), :]
    return pltpu.make_async_remote_copy(src_ref=src, dst_ref=land_ref, send_sem=ssem.at[0], recv_sem=rsem.at[0],
                                        device_id=(x, y, 1 - c), device_id_type=MESH)


def _pair_start(g, name):
    def body(g_ref, land_ref, ssem, rsem, g_out, land_out, token):
        _pair_copy(g_ref, land_ref, ssem, rsem).start()
        token[...] = jnp.zeros_like(token)

    sem1 = pltpu.SemaphoreType.DMA((1,))
    land = lax.empty((N_CHIP, g.shape[1] // 2, g.shape[2]), F32)
    return pl.pallas_call(
        body, name=name,
        out_shape=(sem1, sem1, pltpu.HBM(g.shape, g.dtype), pltpu.HBM(land.shape, land.dtype), _sds((8, 128), F32)),
        in_specs=(HBM, HBM), out_specs=(SEM, SEM, HBM, HBM, VMEM_FULL), input_output_aliases={0: 2, 1: 3},
        compiler_params=pltpu.CompilerParams(has_side_effects=EFFECT),
    )(pltpu.with_memory_space_constraint(g, pltpu.HBM), pltpu.with_memory_space_constraint(land, pltpu.HBM))


def _pair_wait(g, land, ssem, rsem, after, name):
    def body(g_ref, land_ref, ssem_ref, rsem_ref, after_ref, g_out, land_out):
        cp = _pair_copy(g_ref, land_ref, ssem_ref, rsem_ref)
        cp.wait_send()
        cp.wait_recv()

    return pl.pallas_call(
        body, name=name,
        out_shape=(pltpu.HBM(g.shape, g.dtype), pltpu.HBM(land.shape, land.dtype)),
        in_specs=(HBM, HBM, SEM, SEM, ANY), out_specs=(HBM, HBM), input_output_aliases={0: 0, 1: 1},
        compiler_params=pltpu.CompilerParams(has_side_effects=EFFECT),
    )(g, land, ssem, rsem, after)


def _pair_sum(g, rcv, place, name):
    _, rows, cols = g.shape
    half = rows // 2
    tr = 256
    nt = half // tr

    def body(p_ref, g_ref, r_ref, sb_ref, sf_ref):
        s = pl.program_id(1)
        tot = g_ref[0] + r_ref[0]
        sb_ref[0] = tot.astype(BF16)

        @pl.when(s == p_ref[0])
        def _():
            sf_ref[...] = tot

    grid_spec = pltpu.PrefetchScalarGridSpec(
        num_scalar_prefetch=1, grid=(nt, N_CHIP),
        in_specs=[pl.BlockSpec((1, tr, cols), lambda t, s, p: (s, p[1] * nt + t, 0)),
                  pl.BlockSpec((1, tr, cols), lambda t, s, p: (s, t, 0))],
        out_specs=[pl.BlockSpec((1, tr, cols), lambda t, s, p: (s, t, 0)),
                   pl.BlockSpec((tr, cols), lambda t, s, p: (t, 0))])
    return pl.pallas_call(
        body, name=name, grid_spec=grid_spec,
        out_shape=[_sds((N_CHIP, half, cols), BF16), _sds((half, cols), F32)],
        compiler_params=_cp(("arbitrary", "arbitrary")),
    )(place, g, rcv)


def _final_sum(sf, rb, place, name):
    half, cols = sf.shape
    tr = 256
    nt = half // tr

    def body(p_ref, sf_ref, r_ref, out_ref):
        acc = sf_ref[...]
        for j in range(3):
            acc = acc + r_ref[j].astype(F32)
        out_ref[...] = acc

    grid_spec = pltpu.PrefetchScalarGridSpec(
        num_scalar_prefetch=1, grid=(nt,),
        in_specs=[pl.BlockSpec((tr, cols), lambda t, p: (t, 0)), pl.BlockSpec((3, tr, cols), lambda t, p: (0, t, 0))],
        out_specs=pl.BlockSpec((tr, cols), lambda t, p: (p[1] * nt + t, 0)))
    return pl.pallas_call(
        body, name=name, grid_spec=grid_spec,
        out_shape=_sds((2 * half, cols), F32),
        compiler_params=_cp(("arbitrary",)),
    )(place, sf, rb)


def _half_copy(f_ref, which, ssem, rsem):
    x, y, c, _ = _place()
    h = f_ref.shape[0] // 2
    rows = f_ref.at[pl.ds(pl.multiple_of(which * h, 8), h), :]
    return pltpu.make_async_remote_copy(src_ref=rows, dst_ref=rows, send_sem=ssem.at[0], recv_sem=rsem.at[0],
                                        device_id=(x, y, 1 - c), device_id_type=MESH)


def _half_start(fulls, name, after=None):
    n = len(fulls)

    def body(*refs):
        for i in range(n):
            _half_copy(refs[i], lax.axis_index("c"), refs[n + 1 + 2 * i], refs[n + 2 + 2 * i]).start()
        token = refs[4 * n + 1]
        token[...] = jnp.zeros_like(token)

    sem1 = pltpu.SemaphoreType.DMA((1,))
    outs = pl.pallas_call(
        body, name=name,
        out_shape=tuple([sem1] * (2 * n) + [pltpu.HBM(f.shape, f.dtype) for f in fulls] + [_sds((8, 128), F32)]),
        in_specs=(HBM,) * n + (ANY,), out_specs=(SEM,) * (2 * n) + (HBM,) * n + (VMEM_FULL,),
        input_output_aliases={i: 2 * n + i for i in range(n)},
        compiler_params=pltpu.CompilerParams(has_side_effects=EFFECT),
    )(*[pltpu.with_memory_space_constraint(f, pltpu.HBM) for f in fulls], _dep(after))
    return [(outs[2 * i], outs[2 * i + 1], outs[2 * n + i]) for i in range(n)], outs[3 * n]


def _half_wait(items, after, name):
    n = len(items)

    def body(*refs):
        c = lax.axis_index("c")
        for i in range(n):
            ssem_ref, rsem_ref, f_ref = refs[3 * i:3 * i + 3]
            _half_copy(f_ref, c, ssem_ref, rsem_ref).wait_send()
            _half_copy(f_ref, 1 - c, ssem_ref, rsem_ref).wait_recv()

    return pl.pallas_call(
        body, name=name, out_shape=tuple(pltpu.HBM(it[2].shape, it[2].dtype) for it in items),
        in_specs=(SEM, SEM, HBM) * n + (ANY,) * len(after), out_specs=(HBM,) * n,
        input_output_aliases={3 * i + 2: i for i in range(n)},
        compiler_params=pltpu.CompilerParams(has_side_effects=EFFECT),
    )(*[a for it in items for a in it], *after)


def _small_copies(src_ref, land_ref, ssem, rsem, first):
    x, y, c, chips = _place()
    if first:
        return [pltpu.make_async_remote_copy(src_ref=src_ref, dst_ref=land_ref, send_sem=ssem.at[0], recv_sem=rsem.at[0],
                                             device_id=(x, y, 1 - c), device_id_type=MESH)]
    return [pltpu.make_async_remote_copy(src_ref=src_ref, dst_ref=land_ref.at[j], send_sem=ssem.at[j], recv_sem=rsem.at[j],
                                         device_id=(*chip, c), device_id_type=MESH) for j, chip in enumerate(chips)]


def _small_start(src, first, name, after=None):
    n = 1 if first else 3

    def body(src_ref, land_ref, after_ref, ssem, rsem, src_out, land_out, token):
        for cp in _small_copies(src_ref, land_ref, ssem, rsem, first):
            cp.start()
        token[...] = jnp.zeros_like(token)

    sems = pltpu.SemaphoreType.DMA((n,))
    land = lax.empty(src.shape if first else (3,) + src.shape, F32)
    return pl.pallas_call(
        body, name=name,
        out_shape=(sems, sems, pltpu.HBM(src.shape, F32), pltpu.HBM(land.shape, F32), _sds((8, 128), F32)),
        in_specs=(HBM, HBM, ANY), out_specs=(SEM, SEM, HBM, HBM, VMEM_FULL), input_output_aliases={0: 2, 1: 3},
        compiler_params=pltpu.CompilerParams(has_side_effects=EFFECT),
    )(pltpu.with_memory_space_constraint(src, pltpu.HBM), pltpu.with_memory_space_constraint(land, pltpu.HBM), _dep(after))


def _small_wait(src, land, ssem, rsem, first, after, name):
    def body(src_ref, land_ref, ssem_ref, rsem_ref, after_ref, src_out, land_out):
        for cp in _small_copies(src_ref, land_ref, ssem_ref, rsem_ref, first):
            cp.wait_send()
            cp.wait_recv()

    return pl.pallas_call(
        body, name=name,
        out_shape=(pltpu.HBM(src.shape, F32), pltpu.HBM(land.shape, F32)),
        in_specs=(HBM, HBM, SEM, SEM, ANY), out_specs=(HBM, HBM), input_output_aliases={0: 0, 1: 1},
        compiler_params=pltpu.CompilerParams(has_side_effects=EFFECT),
    )(src, land, ssem, rsem, after)


def _small_pair_sum(vec, got):
    def body(v_ref, g_ref, o_ref):
        o_ref[...] = v_ref[...] + g_ref[...]

    return pl.pallas_call(body, name="small_pair_sum", in_specs=[VMEM_FULL] * 2, out_specs=VMEM_FULL,
                          out_shape=_sds(vec.shape, F32), compiler_params=_cp())(vec, got)


def _small_chip_sum(pair, got, place):
    def body(p_ref, pair_ref, got_ref, o_ref):
        acc = None
        for kk in range(N_CHIP):
            d = jnp.bitwise_xor(p_ref[0], kk)
            t = jnp.where(d == 0, pair_ref[...], jnp.where(d == 2, got_ref[0], jnp.where(d == 1, got_ref[1], got_ref[2])))
            acc = t if acc is None else acc + t
        o_ref[...] = acc

    grid_spec = pltpu.PrefetchScalarGridSpec(
        num_scalar_prefetch=1, grid=(1,),
        in_specs=[pl.BlockSpec(pair.shape, lambda i, p: (0, 0)), pl.BlockSpec(got.shape, lambda i, p: (0, 0, 0))],
        out_specs=pl.BlockSpec(pair.shape, lambda i, p: (0, 0)))
    return pl.pallas_call(body, name="small_chip_sum", grid_spec=grid_spec, out_shape=_sds(pair.shape, F32),
                          compiler_params=_cp(("arbitrary",)))(place, pair, got)


def _adam_math(w, g, m, v):
    m = B1 * m + (1.0 - B1) * g
    v = B2 * v + (1.0 - B2) * (g * g)
    m_hat = m / (1.0 - B1 ** STEP)
    v_hat = v / (1.0 - B2 ** STEP)
    delta = -LR * (m_hat / (jnp.sqrt(v_hat) + AEPS) + WD * w)
    return delta, m, v


def _adam_big(w, g, m, v, name, dep=None):
    rows, cols = w.shape
    tr = 128

    def body(w_ref, g_ref, m_ref, v_ref, dep_ref, go_ref, d_ref, nm_ref, nv_ref):
        g = g_ref[...]
        d, nm, nv = _adam_math(w_ref[...], g, m_ref[...], v_ref[...])
        go_ref[...] = g
        d_ref[...] = d
        nm_ref[...] = nm
        nv_ref[...] = nv

    blk = pl.BlockSpec((tr, cols), lambda i: (i, 0))
    return pl.pallas_call(
        body, name=name, grid=(rows // tr,),
        in_specs=[blk] * 4 + [ANY], out_specs=[blk] * 4, out_shape=[_sds((rows, cols), F32)] * 4,
        compiler_params=_cp(("arbitrary",)),
    )(w, g, m, v, _dep(dep))


def _adam_small(ws, gs, ms, vs, dep=None):
    n = len(ws)

    def body(*refs):
        for i in range(n):
            d, nm, nv = _adam_math(refs[i][...], refs[n + i][...], refs[2 * n + i][...], refs[3 * n + i][...])
            refs[4 * n + 1 + i][...] = d
            refs[5 * n + 1 + i][...] = nm
            refs[6 * n + 1 + i][...] = nv

    shapes = [_sds(w.shape, F32) for w in ws]
    outs = pl.pallas_call(
        body, name="adam_small",
        in_specs=[VMEM_FULL] * (4 * n) + [ANY], out_specs=[VMEM_FULL] * (3 * n), out_shape=shapes * 3,
        compiler_params=_cp(),
    )(*ws, *gs, *ms, *vs, _dep(dep))
    return outs[:n], outs[n:2 * n], outs[2 * n:]


def _pad_rows8(a):
    flat = a.reshape(-1, 128)
    pad = (-flat.shape[0]) % 8
    if pad:
        flat = jnp.concatenate([flat, jnp.zeros((pad, 128), F32)], axis=0)
    return flat


def kernel(x, meta_tokens, norm1_w, w_in, gate_w2, gate_b, gla_norm_w, pool_w, pool_scale, w_out, norm2_w, mlp_w1, mlp_w2, final_norm_w, loss_target, m_meta_tokens, m_norm1_w, m_w_in, m_gate_w2, m_gate_b, m_gla_norm_w, m_pool_w, m_pool_scale, m_w_out, m_norm2_w, m_mlp_w1, m_mlp_w2, m_final_norm_w, v_meta_tokens, v_norm1_w, v_w_in, v_gate_w2, v_gate_b, v_gla_norm_w, v_pool_w, v_pool_scale, v_w_out, v_norm2_w, v_mlp_w1, v_mlp_w2, v_final_norm_w):
    cx, cy, cc = lax.axis_index("x"), lax.axis_index("y"), lax.axis_index("c")
    me = (2 * cx + cy).astype(jnp.int32)

    place = jnp.stack([me, cc.astype(jnp.int32)])
    fw = final_norm_w.reshape(1, D)

    mine = jnp.concatenate([meta_tokens.reshape(64, 128), gate_w2[0], pool_w[0].reshape(512, 128)], axis=0)
    small = lax.dynamic_update_slice(jnp.zeros((N_CHIP, 592, 128), F32), mine[None], (me, 0, 0))
    (s_sm,), (r_sm,), (f_sm,), tok = _gather_start([small], "gather_start_small", [True])
    (sem_win_d,), (win,), tok = _gather_step("gather_start_win", [_cast_win(w_in[0], place, tok)], [], [(0, "d")])
    wout, w1, w2 = (_cast_into(w_out[0], place, D, "cast_wout", tok), _cast_into(mlp_w1[0], place, D, "cast_w1", tok),
                    _cast_into(mlp_w2[0], place, D, "cast_w2", tok))
    small = _gather_wait(f_sm, s_sm, r_sm, w2, "gather_wait_small", True)
    metaF = jnp.concatenate([small[k, 0:64].reshape(N_META, 512) for k in range(N_CHIP)], axis=1)
    gw2F = jnp.concatenate([small[k, 64:80] for k in range(N_CHIP)], axis=1)
    pwF = jnp.concatenate([small[k, 80:592].reshape(4, 64, GC) for k in range(N_CHIP)], axis=1)

    fly = {"win": win, "wout": wout, "w1": w1, "w2": w2}
    sems = {"win_d": sem_win_d}

    def step(name, names, waits, starts, after):
        at = {nm: i for i, nm in enumerate(names)}
        new, arrs, tok = _gather_step(name, [fly[nm] for nm in names], [(at[nm], k) for nm, k in waits],
                                      [(at[nm], k) for nm, k in starts], [sems[nm + "_" + k] for nm, k in waits], after)
        fly.update(zip(names, arrs))
        sems.update({nm + "_" + k: s for (nm, k), s in zip(starts, new)})
        return tok

    def relay_first():
        return step("gather_relay_win", ["win", "wout", "w1"], [("win", "d")],
                    [("win", "r"), ("win", "fx"), ("wout", "d"), ("w1", "d")], [v_w_in[0]])

    def get_win(after):
        tok = step("gather_land_win", ["win"], [("win", "r")], [("win", "fd")], [after])
        step("gather_wait_win", ["win"], [("win", "fx"), ("win", "fd")], [], [tok])
        return fly["win"]

    def relay_mid(after):
        return step("gather_relay_mid", ["wout"], [("wout", "d")], [("wout", "r"), ("wout", "fx")], [after, m_w_in[0]])

    def land_wout(after):
        return step("gather_land_wout", ["wout", "w1", "w2"], [("wout", "r"), ("w1", "d")],
                    [("wout", "fd"), ("w1", "r"), ("w1", "fx"), ("w2", "d")], [after])

    def get_wout(after):
        step("gather_wait_wout", ["wout"], [("wout", "fx"), ("wout", "fd")], [], [after])
        tok = step("gather_land_w1", ["w1"], [("w1", "r")], [("w1", "fd")], [fly["wout"]])
        return fly["wout"].reshape(D, D), tok

    def get_w1(after):
        step("gather_wait_w1", ["w1"], [("w1", "fx"), ("w1", "fd")], [], [after])
        return fly["w1"]

    def relay_last(after):
        return step("gather_relay_w2", ["w2"], [("w2", "d")], [("w2", "r"), ("w2", "fx")], [after])

    def get_w2(after):
        tok = step("gather_land_w2", ["w2"], [("w2", "r")], [("w2", "fd")], [after])
        step("gather_wait_w2", ["w2"], [("w2", "fx"), ("w2", "fd")], [], [tok])
        return fly["w2"].reshape(DFF, D)

    pairs, pending = {}, {}

    halves = {}

    def reduce_(names, after, tag):
        items = [(pending[nm][3], pending[nm][4], pending[nm][1], pending[nm][2]) for nm in names]
        landed = _rs_wait(items, after, "rs_wait_" + tag)
        fulls = [_final_sum(pending[nm][0], rb, place, "final_sum_" + nm) for nm, rb in zip(names, landed)]
        sent, token = _half_start(fulls, "half_start_" + tag)
        halves.update(zip(names, sent))
        return token

    def grad_start(nm, g):
        ssem, rsem, g_thru, land, token = _pair_start(g, "pair_start_" + nm)
        pairs[nm] = (ssem, rsem, g_thru, land)
        if nm == "win":
            token = reduce_(["w2", "w1", "wout"], token, "mlp_wout")
        return token

    def grad_finish(nm, after):
        ssem, rsem, g_thru, land = pairs[nm]
        g, rcv = _pair_wait(g_thru, land, ssem, rsem, after, "pair_wait_" + nm)
        sb, sf = _pair_sum(g, rcv, place, "pair_sum_" + nm)
        ssem, rsem, sb_thru, land, token = _rs_start(sb, "rs_start_" + nm)
        pending[nm] = (sf, ssem, rsem, sb_thru, land)
        return token

    (grad_x, loss8, d_n1w, d_gb, d_gnw, d_ps, d_n2w, d_fw, d_meta, d_gw2, d_pw) = _local_step(
        x[0], loss_target[0], dict(relay_first=relay_first, win=get_win, relay_mid=relay_mid, land_wout=land_wout,
                                   wout=get_wout, w1=get_w1, relay_last=relay_last, w2=get_w2),
        metaF, gw2F, pwF, norm1_w, gate_b, gla_norm_w, pool_scale, norm2_w, fw, grad_start, grad_finish)
    return _reduce_and_update(
        me, place, pending, halves, reduce_, grad_x, loss8, d_n1w, d_gb, d_gnw, d_ps, d_n2w, d_fw, d_meta, d_gw2, d_pw,
        meta_tokens, norm1_w, w_in, gate_w2, gate_b, gla_norm_w, pool_w, pool_scale, w_out, norm2_w, mlp_w1, mlp_w2, fw,
        m_meta_tokens, m_norm1_w, m_w_in, m_gate_w2, m_gate_b, m_gla_norm_w, m_pool_w, m_pool_scale, m_w_out, m_norm2_w,
        m_mlp_w1, m_mlp_w2, m_final_norm_w, v_meta_tokens, v_norm1_w, v_w_in, v_gate_w2, v_gate_b, v_gla_norm_w, v_pool_w,
        v_pool_scale, v_w_out, v_norm2_w, v_mlp_w1, v_mlp_w2, v_final_norm_w)


def _local_step(x, target, gather, metaF, gw2F, pwF, norm1_w, gate_b, gla_norm_w, pool_scale, norm2_w, fw, grad_start,
                grad_finish):
    h0, u = _embed_norm(x, metaF, norm1_w, gather["relay_first"]())
    Win = gather["win"](u)
    P = _in_proj(u, Win)
    gw2p = jnp.pad(gw2F, ((0, 128 - RANK), (0, 0)))
    yb, op = _pool_fwd(P, pwF, pool_scale, gather["relay_mid"](P))
    o, og, sp = _gla_fwd(P, gw2p, gate_b, gla_norm_w, gather["land_wout"](op))
    Wout, tok = gather["wout"](og)
    h1 = _out_proj(og, op, Wout, h0, tok)
    n2 = _norm_rows(h1, norm2_w, "norm2")
    W1 = gather["w1"](n2)
    zr, a, tok = _mlp_up(n2, W1, 0)
    zr, a, _ = _mlp_up(n2, W1, 1, (zr, a), gather["relay_last"](tok))
    W2 = gather["w2"](a)
    h2 = _mlp_down(a, W2, h1)

    dh2, dh2b, d_fw, loss8 = _loss_head(h2, target, fw)
    tok = grad_start("w2", _grad_w2(a, dh2b).reshape(N_CHIP, D, D))
    dz = _mlp_dz(dh2b, W2, zr, tok)
    tok = grad_finish("w2", dz)
    tok = grad_start("w1", _grad_w1(n2, dz, tok))
    dn2 = _mlp_dn(dz, W1, tok)
    tok = grad_finish("w1", dn2)
    dh1, dh1b, d_n2w = _norm_bwd(dn2, h1, dh2, norm2_w, "norm2_bwd", tok)
    dmixed = _mixed_grad(dh1b, Wout)
    tok = grad_start("wout", _grad_wout(og, op, dh1b))
    dpu, d_pw, d_ps = _pool_bwd(dmixed, yb, pwF, pool_scale, tok)
    dq, dk, dv, dr, dglr, d_gw2p, d_gb, d_gnw = _gla_bwd(dmixed, o, P, gw2p, gate_b, gla_norm_w, sp, tok)
    d_gw2 = d_gw2p[0:RANK]
    tok = grad_finish("wout", dq)
    tok = grad_start("win", _grad_win(u, dq, dk, dv, dr, dglr, dpu, tok))
    du = _in_grad(dq, dk, dv, dr, dglr, dpu, Win, tok)
    tok = grad_finish("win", du)
    grad_x, d_meta, d_n1w = _input_grad(du, h0, dh1, norm1_w, tok)
    return grad_x, loss8, d_n1w, d_gb, d_gnw, d_ps, d_n2w, d_fw, d_meta, d_gw2, d_pw


def _reduce_and_update(me, place, pending, halves, reduce_, grad_x, loss8, d_n1w, d_gb, d_gnw, d_ps, d_n2w, d_fw, d_meta, d_gw2,
                       d_pw,
                       meta_tokens, norm1_w, w_in, gate_w2, gate_b, gla_norm_w, pool_w, pool_scale, w_out, norm2_w,
                       mlp_w1, mlp_w2, fw, m_meta_tokens, m_norm1_w, m_w_in, m_gate_w2, m_gate_b, m_gla_norm_w, m_pool_w,
                       m_pool_scale, m_w_out, m_norm2_w, m_mlp_w1, m_mlp_w2, m_final_norm_w, v_meta_tokens, v_norm1_w, v_w_in,
                       v_gate_w2, v_gate_b, v_gla_norm_w, v_pool_w, v_pool_scale, v_w_out, v_norm2_w, v_mlp_w1, v_mlp_w2,
                       v_final_norm_w):
    parts = [loss8, d_n1w, d_gb, d_gnw, d_ps, d_n2w, d_fw, d_meta, d_gw2, d_pw]
    packed = [_pad_rows8(p) for p in parts]
    sizes = [p.shape[0] for p in packed]
    vec = jnp.concatenate(packed, axis=0)

    big = {}
    params = {"w2": (mlp_w2[0], m_mlp_w2[0], v_mlp_w2[0]), "w1": (mlp_w1[0], m_mlp_w1[0], v_mlp_w1[0]),
              "wout": (w_out[0], m_w_out[0], v_w_out[0]), "win": (w_in[0], m_w_in[0], v_w_in[0])}

    def update(names, after, tag):
        fulls = _half_wait([halves[nm] for nm in names], [after], "half_wait_" + tag)
        tok = None
        for nm, full in zip(names, fulls):
            w, m, v = params[nm]
            big[nm] = _adam_big(w, full, m, v, "adam_" + nm, tok)
            tok = big[nm][3]
        return tok

    s1, r1, vec, land1, tok = _small_start(vec, True, "small_start_pair")
    tok = update(["w2"], tok, "w2")
    vec, got = _small_wait(vec, land1, s1, r1, True, tok, "small_wait_pair")
    pair = _small_pair_sum(vec, got)
    s2, r2, pair, land2, tok = _small_start(pair, False, "small_start_chips")
    tok = reduce_(["win"], tok, "win")
    tok = update(["w1"], tok, "w1")
    tok = update(["win"], tok, "win")
    pair, got = _small_wait(pair, land2, s2, r2, False, tok, "small_wait_chips")
    red = _small_chip_sum(pair, got, place)
    after = update(["wout"], red, "wout")
    offs = [0]
    for s in sizes:
        offs.append(offs[-1] + s)

    def take(i, shape):
        n = 1
        for d in shape:
            n *= d
        return red[offs[i]:offs[i] + n // 128].reshape(shape)

    loss = red[0, 0]
    G_n1w = take(1, (1, D))
    G_gb = take(2, (1, KW))
    G_gnw = take(3, (1, DV))
    G_ps = take(4, (1, PW))
    G_n2w = take(5, (1, D))
    G_fw = take(6, (1, D))
    G_meta = lax.dynamic_slice(take(7, (N_META, D)), (0, me * 512), (N_META, 512))
    G_gw2 = lax.dynamic_slice(take(8, (RANK, KW)), (0, me * 128), (RANK, 128))
    G_pw = lax.dynamic_slice(take(9, (4, GC, GC)), (0, me * 64, 0), (4, 64, GC))

    G_win, d_win, nm_win, nv_win = big["win"]
    G_wout, d_wout, nm_wout, nv_wout = big["wout"]
    G_w1, d_w1, nm_w1, nv_w1 = big["w1"]
    G_w2, d_w2, nm_w2, nv_w2 = big["w2"]
    ws = [meta_tokens, norm1_w, gate_w2[0], gate_b, gla_norm_w, pool_w[0], pool_scale, norm2_w, fw]
    gs = [G_meta, G_n1w, G_gw2, G_gb, G_gnw, G_pw, G_ps, G_n2w, G_fw]
    ms = [m_meta_tokens, m_norm1_w, m_gate_w2[0], m_gate_b, m_gla_norm_w, m_pool_w[0], m_pool_scale, m_norm2_w,
          m_final_norm_w.reshape(1, D)]
    vs = [v_meta_tokens, v_norm1_w, v_gate_w2[0], v_gate_b, v_gla_norm_w, v_pool_w[0], v_pool_scale, v_norm2_w,
          v_final_norm_w.reshape(1, D)]
    ds, nms, nvs = _adam_small(ws, gs, ms, vs, after)

    def assemble(small, win_, wout_, w1_, w2_):
        meta_, n1_, gw2_, gb_, gnw_, pw_, ps_, n2_, fw_ = small
        return (meta_, n1_, win_[None], gw2_[None], gb_, gnw_, pw_[None], ps_, wout_[None], n2_, w1_[None], w2_[None],
                fw_.reshape(D))

    grads_out = assemble(gs, G_win, G_wout, G_w1, G_w2)
    deltas = assemble(ds, d_win, d_wout, d_w1, d_w2)
    new_m = assemble(nms, nm_win, nm_wout, nm_w1, nm_w2)
    new_v = assemble(nvs, nv_win, nv_wout, nv_w1, nv_w2)
    return (loss, grad_x[None], *grads_out, *deltas, *new_m, *new_v)
```

```python
import functools

import jax
import jax.numpy as jnp
from jax import lax
from jax.experimental import pallas as pl
from jax.experimental.pallas import tpu as pltpu

F32 = jnp.float32
BF16 = jnp.bfloat16

D = 2048
SEQ = 2048
N_META = 16
CH = 64
TP = 2176
NCH = TP // CH
ROW_LO = 112
X_LO = 128
ROW_HI = TP
XT = 128
NXT = TP // XT
HEADS = 4
DK = 128
DV = 256
KW = HEADS * DK
GW = HEADS * DV
RANK = 16
TAU = 16.0
WINDOWS = (2, 4, 8, 16)
PW = 1024
GC = 256
DFF = 8192
EPS = 1e-6
SHARD_IN = 1028
PAD_IN = 1152
N_CHIP = 4

LR = 0.001
B1 = 0.9
B2 = 0.999
AEPS = 1e-08
WD = 0.01
STEP = 10

VMEM_LIMIT = 60 * 1024 * 1024
ANY = pl.BlockSpec(memory_space=pl.ANY)
VMEM_FULL = pl.BlockSpec(memory_space=pltpu.VMEM)
MESH = pl.DeviceIdType.MESH


def _cp(sem=None):
    if sem is None:
        return pltpu.CompilerParams(vmem_limit_bytes=VMEM_LIMIT)
    return pltpu.CompilerParams(dimension_semantics=sem, vmem_limit_bytes=VMEM_LIMIT)


def _dot(a, b):
    return jnp.dot(a, b, preferred_element_type=F32)


def _dot_nt(a, b):
    return lax.dot_general(a, b, (((1,), (1,)), ((), ())), preferred_element_type=F32)


def _dot_tn(a, b):
    return lax.dot_general(a, b, (((0,), (0,)), ((), ())), preferred_element_type=F32)


def _sds(shape, dtype):
    return jax.ShapeDtypeStruct(shape, dtype)


def _embed_norm(x, meta_full, w, dep=None):
    def body(x_ref, meta_ref, w_ref, dep_ref, h_ref, u_ref):
        i = pl.program_id(0)

        @pl.when(i == 0)
        def _():
            h_ref[...] = jnp.zeros_like(h_ref)
            h_ref[ROW_LO:X_LO, :] = meta_ref[...]

        @pl.when(i >= 1)
        def _():
            h_ref[...] = x_ref[...]

        h = h_ref[...]
        r = lax.rsqrt(jnp.mean(h * h, axis=-1, keepdims=True) + EPS)
        u_ref[...] = ((h * r) * w_ref[...]).astype(BF16)

    return pl.pallas_call(
        body, name="embed_norm1", grid=(NXT,),
        in_specs=[pl.BlockSpec((XT, D), lambda i: (jnp.maximum(i - 1, 0), 0)),
                  pl.BlockSpec((N_META, D), lambda i: (0, 0)),
                  pl.BlockSpec((1, D), lambda i: (0, 0)), ANY],
        out_specs=[pl.BlockSpec((XT, D), lambda i: (i, 0)), pl.BlockSpec((XT, D), lambda i: (i, 0))],
        out_shape=[_sds((TP, D), F32), _sds((TP, D), BF16)],
        compiler_params=_cp(("arbitrary",)),
    )(x, meta_full, w, _dep(dep))


def _norm_rows(h, w, name):
    tr = 272

    def body(h_ref, w_ref, o_ref):
        hv = h_ref[...]
        r = lax.rsqrt(jnp.mean(hv * hv, axis=-1, keepdims=True) + EPS)
        o_ref[...] = ((hv * r) * w_ref[...]).astype(BF16)

    return pl.pallas_call(
        body, name=name, grid=(TP // tr,),
        in_specs=[pl.BlockSpec((tr, D), lambda i: (i, 0)), pl.BlockSpec((1, D), lambda i: (0, 0))],
        out_specs=pl.BlockSpec((tr, D), lambda i: (i, 0)),
        out_shape=_sds((TP, D), BF16),
        compiler_params=_cp(("arbitrary",)),
    )(h, w)


def _loss_head(h2, target, fw):
    def body(h_ref, t_ref, w_ref, dh_ref, dhb_ref, dw_ref, loss_ref):
        i = pl.program_id(0)

        @pl.when(i == 0)
        def _():
            dw_ref[...] = jnp.zeros_like(dw_ref)
            loss_ref[...] = jnp.zeros_like(loss_ref)

        h = h_ref[...]
        w = w_ref[...]
        r = lax.rsqrt(jnp.mean(h * h, axis=-1, keepdims=True) + EPS)
        xh = h * r
        y = xh * w
        is_x = (i >= 1).astype(F32)
        diff = (y - t_ref[...]) * is_x
        loss_ref[...] += jnp.sum(diff * diff) * (0.5 / D)
        dy = diff * (1.0 / D)
        dw_ref[...] += jnp.sum(dy * xh, axis=0, keepdims=True)
        gx = dy * w
        dh = r * (gx - xh * jnp.mean(gx * xh, axis=-1, keepdims=True))
        dh_ref[...] = dh
        dhb_ref[...] = dh.astype(BF16)

    return pl.pallas_call(
        body, name="loss_head", grid=(NXT,),
        in_specs=[pl.BlockSpec((XT, D), lambda i: (i, 0)),
                  pl.BlockSpec((XT, D), lambda i: (jnp.maximum(i - 1, 0), 0)),
                  pl.BlockSpec((1, D), lambda i: (0, 0))],
        out_specs=[pl.BlockSpec((XT, D), lambda i: (i, 0)), pl.BlockSpec((XT, D), lambda i: (i, 0)),
                   pl.BlockSpec((1, D), lambda i: (0, 0)), pl.BlockSpec((8, 128), lambda i: (0, 0))],
        out_shape=[_sds((TP, D), F32), _sds((TP, D), BF16), _sds((1, D), F32), _sds((8, 128), F32)],
        compiler_params=_cp(("arbitrary",)),
    )(h2, target, fw)


def _norm_bwd(dn, h, dres, w, name, dep=None):
    tr = 272

    def body(dn_ref, h_ref, dres_ref, w_ref, dep_ref, o_ref, ob_ref, dw_ref):
        @pl.when(pl.program_id(0) == 0)
        def _():
            dw_ref[...] = jnp.zeros_like(dw_ref)

        hv = h_ref[...]
        dnv = dn_ref[...]
        r = lax.rsqrt(jnp.mean(hv * hv, axis=-1, keepdims=True) + EPS)
        xh = hv * r
        dw_ref[...] += jnp.sum(dnv * xh, axis=0, keepdims=True)
        gx = dnv * w_ref[...]
        dh = dres_ref[...] + r * (gx - xh * jnp.mean(gx * xh, axis=-1, keepdims=True))
        o_ref[...] = dh
        ob_ref[...] = dh.astype(BF16)

    row = pl.BlockSpec((tr, D), lambda i: (i, 0))
    vec = pl.BlockSpec((1, D), lambda i: (0, 0))
    return pl.pallas_call(
        body, name=name, grid=(TP // tr,),
        in_specs=[row, row, row, vec, ANY], out_specs=[row, row, vec],
        out_shape=[_sds((TP, D), F32), _sds((TP, D), BF16), _sds((1, D), F32)],
        compiler_params=_cp(("arbitrary",)),
    )(dn, h, dres, w, _dep(dep))


def _input_grad(du, h0, dh1, w, dep=None):
    def body(du_ref, h_ref, dres_ref, w_ref, dep_ref, gx_ref, gm_ref, dw_ref):
        i = pl.program_id(0)

        @pl.when(i == 0)
        def _():
            dw_ref[...] = jnp.zeros_like(dw_ref)

        hv = h_ref[...]
        dnv = du_ref[...]
        r = lax.rsqrt(jnp.mean(hv * hv, axis=-1, keepdims=True) + EPS)
        xh = hv * r
        dw_ref[...] += jnp.sum(dnv * xh, axis=0, keepdims=True)
        g = dnv * w_ref[...]
        dh = dres_ref[...] + r * (g - xh * jnp.mean(g * xh, axis=-1, keepdims=True))

        @pl.when(i == 0)
        def _():
            gm_ref[...] = dh[ROW_LO:X_LO, :]

        @pl.when(i >= 1)
        def _():
            gx_ref[...] = dh

    row = pl.BlockSpec((XT, D), lambda i: (i, 0))
    vec = pl.BlockSpec((1, D), lambda i: (0, 0))
    return pl.pallas_call(
        body, name="input_grad", grid=(NXT,),
        in_specs=[row, row, row, vec, ANY],
        out_specs=[pl.BlockSpec((XT, D), lambda i: (jnp.maximum(i - 1, 0), 0)),
                   pl.BlockSpec((N_META, D), lambda i: (0, 0)), vec],
        out_shape=[_sds((SEQ, D), F32), _sds((N_META, D), F32), _sds((1, D), F32)],
        compiler_params=_cp(("arbitrary",)),
    )(du, h0, dh1, w, _dep(dep))


def _in_proj(u, wg):
    def body(u_ref, w_ref, o_ref):
        o_ref[0] = _dot(u_ref[...], w_ref[0])

    return pl.pallas_call(
        body, name="in_proj", grid=(N_CHIP,),
        in_specs=[VMEM_FULL, pl.BlockSpec((1, D, PAD_IN), lambda k: (k, 0, 0))],
        out_specs=pl.BlockSpec((1, TP, PAD_IN), lambda k: (k, 0, 0)),
        out_shape=_sds((N_CHIP, TP, PAD_IN), F32),
        compiler_params=_cp(("arbitrary",)),
    )(u, wg)


def _out_proj(og, op, wout, h0, dep=None):
    tn = 512

    def body(og_ref, op_ref, w_ref, h_ref, dep_ref, o_ref):
        acc = _dot(og_ref[...], w_ref[0:GW, :]) + _dot(op_ref[...], w_ref[GW:D, :])
        o_ref[...] = h_ref[...] + acc

    return pl.pallas_call(
        body, name="out_proj", grid=(D // tn,),
        in_specs=[VMEM_FULL, VMEM_FULL, pl.BlockSpec((D, tn), lambda j: (0, j)),
                  pl.BlockSpec((TP, tn), lambda j: (0, j)), ANY],
        out_specs=pl.BlockSpec((TP, tn), lambda j: (0, j)),
        out_shape=_sds((TP, D), F32),
        compiler_params=_cp(("arbitrary",)),
    )(og, op, wout, h0, _dep(dep))


def _mlp_up(n2, w1g, part, prev=None, dep=None):
    tn = 1024
    per = D // tn

    def body(n_ref, w_ref, dep_ref, *rest):
        zr_ref, a_ref, token = rest[-3:]
        z = jnp.maximum(_dot(n_ref[...], w_ref[0]), 0.0)
        zr_ref[...] = z.astype(BF16)
        a_ref[...] = (z * z).astype(BF16)
        token[...] = jnp.zeros_like(token)

    col = pl.BlockSpec((TP, tn), lambda k, j: (0, (2 * part + k) * per + j))
    return pl.pallas_call(
        body, name="mlp_up_%d" % part, grid=(N_CHIP // 2, per),
        in_specs=[VMEM_FULL, pl.BlockSpec((1, D, tn), lambda k, j: (2 * part + k, 0, j)), ANY] + ([ANY, ANY] if prev else []),
        out_specs=[col, col, pl.BlockSpec((8, 128), lambda k, j: (0, 0))],
        out_shape=[_sds((TP, DFF), BF16), _sds((TP, DFF), BF16), _sds((8, 128), F32)],
        input_output_aliases={3: 0, 4: 1} if prev else {},
        compiler_params=_cp(("arbitrary", "arbitrary")),
    )(n2, w1g, _dep(dep), *(prev or ()))


def _mlp_down(a, w2, h1):
    tk = 1024
    nk = DFF // tk

    def body(a_ref, w_ref, h_ref, o_ref, acc_ref):
        k = pl.program_id(0)

        @pl.when(k == 0)
        def _():
            pltpu.sync_copy(h_ref, acc_ref)

        acc_ref[...] += _dot(a_ref[...], w_ref[...])

        @pl.when(k == nk - 1)
        def _():
            pltpu.sync_copy(acc_ref, o_ref)

    return pl.pallas_call(
        body, name="mlp_down", grid=(nk,),
        in_specs=[pl.BlockSpec((TP, tk), lambda k: (0, k)), pl.BlockSpec((tk, D), lambda k: (k, 0)), ANY],
        out_specs=ANY,
        out_shape=_sds((TP, D), F32),
        scratch_shapes=[pltpu.VMEM((TP, D), F32)],
        compiler_params=_cp(("arbitrary",)),
    )(a, w2, h1)


def _mlp_dz(dh2b, w2, zr, dep=None):
    tn = 1024

    def body(d_ref, w_ref, z_ref, dep_ref, o_ref):
        da = _dot_nt(d_ref[...], w_ref[...])
        o_ref[...] = (da * (2.0 * z_ref[...].astype(F32))).astype(BF16)

    col = pl.BlockSpec((TP, tn), lambda j: (0, j))
    return pl.pallas_call(
        body, name="mlp_dz", grid=(DFF // tn,),
        in_specs=[VMEM_FULL, pl.BlockSpec((tn, D), lambda j: (j, 0)), col, ANY],
        out_specs=col,
        out_shape=_sds((TP, DFF), BF16),
        compiler_params=_cp(("arbitrary",)),
    )(dh2b, w2, zr, _dep(dep))


def _grad_w2(a, dh2b):
    tm = 1024

    def body(a_ref, d_ref, o_ref):
        o_ref[...] = _dot_tn(a_ref[...], d_ref[...])

    return pl.pallas_call(
        body, name="grad_w2", grid=(DFF // tm,),
        in_specs=[pl.BlockSpec((TP, tm), lambda j: (0, j)), VMEM_FULL],
        out_specs=pl.BlockSpec((tm, D), lambda j: (j, 0)),
        out_shape=_sds((DFF, D), F32),
        compiler_params=_cp(("arbitrary",)),
    )(a, dh2b)


def _dep(token):
    return jnp.zeros((8, 128), F32) if token is None else token


def _grad_w1(n2, dz, dep=None):
    tn = 1024
    per = D // tn

    def body(n_ref, d_ref, dep_ref, o_ref):
        o_ref[0] = _dot_tn(n_ref[...], d_ref[...])

    return pl.pallas_call(
        body, name="grad_w1", grid=(N_CHIP, per),
        in_specs=[VMEM_FULL, pl.BlockSpec((TP, tn), lambda k, j: (0, k * per + j)), ANY],
        out_specs=pl.BlockSpec((1, D, tn), lambda k, j: (k, 0, j)),
        out_shape=_sds((N_CHIP, D, D), F32),
        compiler_params=_cp(("arbitrary", "arbitrary")),
    )(n2, dz, _dep(dep))


def _mlp_dn(dz, w1g, dep=None):
    tk = 1024
    per = D // tk
    nk = DFF // tk

    def body(d_ref, w_ref, dep_ref, o_ref, acc_ref):
        k = pl.program_id(0)
        part = _dot_nt(d_ref[...], w_ref[0])

        @pl.when(k == 0)
        def _():
            acc_ref[...] = part

        @pl.when(k > 0)
        def _():
            acc_ref[...] += part

        @pl.when(k == nk - 1)
        def _():
            pltpu.sync_copy(acc_ref, o_ref)

    return pl.pallas_call(
        body, name="mlp_dn", grid=(nk,),
        in_specs=[pl.BlockSpec((TP, tk), lambda k: (0, k)),
                  pl.BlockSpec((1, D, tk), lambda k: (k // per, 0, k % per)), ANY],
        out_specs=ANY,
        out_shape=_sds((TP, D), F32),
        scratch_shapes=[pltpu.VMEM((TP, D), F32)],
        compiler_params=_cp(("arbitrary",)),
    )(dz, w1g, _dep(dep))


def _mixed_grad(dh1b, wout):
    tn = 512

    def body(d_ref, w_ref, o_ref):
        o_ref[...] = _dot_nt(d_ref[...], w_ref[...])

    return pl.pallas_call(
        body, name="mixed_grad", grid=(D // tn,),
        in_specs=[VMEM_FULL, pl.BlockSpec((tn, D), lambda j: (j, 0))],
        out_specs=pl.BlockSpec((TP, tn), lambda j: (0, j)),
        out_shape=_sds((TP, D), F32),
        compiler_params=_cp(("arbitrary",)),
    )(dh1b, wout)


def _grad_wout(og, op, dh1b):
    tm = 512

    def body(og_ref, op_ref, d_ref, o_ref):
        j = pl.program_id(0)

        @pl.when(j < 2)
        def _():
            o_ref[0] = _dot_tn(og_ref[...], d_ref[...])

        @pl.when(j >= 2)
        def _():
            o_ref[0] = _dot_tn(op_ref[...], d_ref[...])

    return pl.pallas_call(
        body, name="grad_wout", grid=(N_CHIP,),
        in_specs=[pl.BlockSpec((TP, tm), lambda j: (0, jnp.minimum(j, 1))),
                  pl.BlockSpec((TP, tm), lambda j: (0, jnp.maximum(j - 2, 0))), VMEM_FULL],
        out_specs=pl.BlockSpec((1, tm, D), lambda j: (j, 0, 0)),
        out_shape=_sds((N_CHIP, tm, D), F32),
        compiler_params=_cp(("arbitrary",)),
    )(og, op, dh1b)


def _in_grad(dq, dk, dv, dr, dglr, dpu, wg, dep=None):
    def body(dq_ref, dk_ref, dv_ref, dr_ref, dg_ref, dpu_ref, w_ref, dep_ref, o_ref):
        dv, dr, dg = dv_ref[...], dr_ref[...], dg_ref[...]
        head, tail = slice(0, GW), slice(GW, PAD_IN)
        o_ref[...] = (_dot_nt(dq_ref[...], w_ref[0, :, 0:KW]) + _dot_nt(dk_ref[...], w_ref[0, :, KW:GW])
                      + _dot_nt(dv[:, 0:128], w_ref[0, :, tail])
                      + _dot_nt(dv, w_ref[1, :, head]) + _dot_nt(dr[:, 0:128], w_ref[1, :, tail])
                      + _dot_nt(dr, w_ref[2, :, head]) + _dot_nt(dg, w_ref[2, :, tail])
                      + _dot_nt(dpu_ref[...], w_ref[3, :, head]) + _dot_nt(dg, w_ref[3, :, tail]))

    tn = 512
    return pl.pallas_call(
        body, name="in_grad", grid=(D // tn,),
        in_specs=[VMEM_FULL] * 6 + [pl.BlockSpec((N_CHIP, tn, PAD_IN), lambda j: (0, j, 0)), ANY],
        out_specs=pl.BlockSpec((TP, tn), lambda j: (0, j)),
        out_shape=_sds((TP, D), F32),
        compiler_params=_cp(("arbitrary",)),
    )(dq, dk, dv, dr, dglr, dpu, wg, _dep(dep))


def _grad_win(u, dq, dk, dv, dr, dglr, dpu, dep=None):
    tm = 512

    def body(u_ref, dq_hbm, dk_hbm, dv_hbm, dr_hbm, dg_hbm, dpu_hbm, dep_ref, o_ref, dp_ref, sem):
        k, m = pl.program_id(0), pl.program_id(1)
        head, tail = slice(0, GW), slice(GW, PAD_IN)
        pieces = [[(dq_hbm, slice(0, KW)), (dk_hbm, slice(KW, GW)), (dv_hbm.at[:, 0:128], tail)],
                  [(dv_hbm, head), (dr_hbm.at[:, 0:128], tail)],
                  [(dr_hbm, head), (dg_hbm, tail)],
                  [(dpu_hbm, head), (dg_hbm, tail)]]

        def copies(kk):
            return [pltpu.make_async_copy(src, dp_ref.at[kk % 2, :, cols], sem.at[kk % 2, i])
                    for i, (src, cols) in enumerate(pieces[kk])]

        @pl.when((k == 0) & (m == 0))
        def _():
            for cp in copies(0):
                cp.start()

        for kk in range(N_CHIP):
            @pl.when((k == kk) & (m == 0))
            def _(kk=kk):
                for cp in copies(kk):
                    cp.wait()
                if kk + 1 < N_CHIP:
                    for cp in copies(kk + 1):
                        cp.start()

        g = _dot_tn(u_ref[...], dp_ref[k % 2])
        lane = lax.broadcasted_iota(jnp.int32, (tm, PAD_IN), 1)
        for kk in range(N_CHIP):
            @pl.when(k == kk)
            def _(kk=kk):
                if kk == 0:
                    nat = g
                elif kk < 3:
                    nat = pltpu.roll(g, PAD_IN - 4 * kk, 1)
                else:
                    nat = jnp.where(lane < 4, pltpu.roll(g, PAD_IN - (GW + 12), 1), pltpu.roll(g, 4, 1))
                o_ref[0] = nat[:, 0:SHARD_IN]

    return pl.pallas_call(
        body, name="grad_win", grid=(N_CHIP, D // tm),
        in_specs=[pl.BlockSpec((TP, tm), lambda k, m: (0, m))] + [ANY] * 7,
        out_specs=pl.BlockSpec((1, tm, SHARD_IN), lambda k, m: (k, m, 0)),
        out_shape=_sds((N_CHIP, D, SHARD_IN), F32),
        scratch_shapes=[pltpu.VMEM((2, TP, PAD_IN), BF16), pltpu.SemaphoreType.DMA((2, 3))],
        compiler_params=_cp(("arbitrary", "arbitrary")),
    )(u, dq, dk, dv, dr, dglr, dpu, _dep(dep))


def _split3(x):
    hi = x.astype(BF16)
    r1 = x - hi.astype(F32)
    mid = r1.astype(BF16)
    lo = (r1 - mid.astype(F32)).astype(BF16)
    return hi, mid, lo


def _tri_sum(tri, x):
    hi, mid, lo = _split3(x)
    return _dot(tri, hi) + _dot(tri, mid) + _dot(tri, lo)


def _gla_common(n, glr, gw2, gb):
    rows = n * CH + lax.broadcasted_iota(jnp.int32, (CH, 1), 0)
    valid = (rows >= ROW_LO) & (rows < ROW_HI)
    g_raw = _dot(glr.astype(BF16), gw2.astype(BF16)) + gb
    logsig = jnp.minimum(g_raw, 0.0) - jnp.log(1.0 + jnp.exp(-jnp.abs(g_raw)))
    logg = jnp.where(valid, logsig * (1.0 / TAU), 0.0)
    ci = lax.broadcasted_iota(jnp.int32, (CH, CH), 0)
    si = lax.broadcasted_iota(jnp.int32, (CH, CH), 1)
    lower = ci >= si
    G = _tri_sum(lower.astype(BF16), logg)
    Gl = G[CH - 1:CH, :]
    return valid, g_raw, lower, G, Gl


GSUB = 2


def _p_specs(index):
    def spec(width, shard, col):
        return pl.BlockSpec((1, GSUB * CH, width), lambda s: (shard, index(s), col))

    return [spec(KW, 0, 0), spec(KW, 0, 1), spec(GW, 1, 0), spec(128, 0, 8), spec(GW, 2, 0), spec(128, 1, 8),
            spec(128, 2, 8), spec(128, 3, 8)]


def _p_load(q_ref, k_ref, vm_ref, vh_ref, rm_ref, rh_ref, ga_ref, gb_ref):
    def joined(main, head):
        return jnp.concatenate([main[:, 0:128] + head, main[:, 128:]], axis=1)

    return q_ref[0], k_ref[0], joined(vm_ref[0], vh_ref[0]), joined(rm_ref[0], rh_ref[0]), ga_ref[0] + gb_ref[0]


def _gla_fwd(P, gw2, gb, gnw, dep=None):
    scale = DK ** -0.5

    def body(p0, p1, p2, p3, p4, p5, p6, p7, gw2_ref, gb_ref, gnw_ref, dep_ref, o_ref, og_ref, sp_ref, st_ref):
        n = pl.program_id(0)

        @pl.when(n == 0)
        def _():
            st_ref[...] = jnp.zeros_like(st_ref)

        q_blk, k_blk, v_blk, r_blk, glr_blk = _p_load(p0, p1, p2, p3, p4, p5, p6, p7)
        gnw_v = gnw_ref[...]
        for sub in range(GSUB):
            rows = slice(sub * CH, (sub + 1) * CH)
            _, _, lower, G, Gl = _gla_common(GSUB * n + sub, glr_blk[rows], gw2_ref[...], gb_ref[...])
            eG = jnp.exp(G)
            eN = jnp.exp(-G)
            eE = jnp.exp(Gl - G)
            dec = jnp.exp(Gl)
            for h in range(HEADS):
                ks = slice(h * DK, (h + 1) * DK)
                vs = slice(h * DV, (h + 1) * DV)
                kh = k_blk[rows, ks]
                vh = v_blk[rows, vs].astype(BF16)
                qd = ((q_blk[rows, ks] * scale) * eG[:, ks]).astype(BF16)
                ki = (kh * eN[:, ks]).astype(BF16)
                ke = (kh * eE[:, ks]).astype(BF16)
                st = st_ref[h]
                a = jnp.where(lower, _dot_nt(qd, ki), 0.0).astype(BF16)
                o = _dot(a, vh) + _dot_nt(qd, st.astype(BF16))
                sp_ref[sub, h] = st
                st_ref[h] = st * dec[:, ks] + _dot_tn(vh, ke)
                o_ref[rows, vs] = o
                rs = lax.rsqrt(jnp.mean(o * o, axis=-1, keepdims=True) + EPS)
                rv = r_blk[rows, vs]
                gate = rv / (1.0 + jnp.exp(-rv))
                og_ref[rows, vs] = (((o * rs) * gnw_v) * gate).astype(BF16)

    rv_ = pl.BlockSpec((GSUB * CH, GW), lambda n: (n, 0))

    def full(shape):
        return pl.BlockSpec(shape, lambda n: tuple(0 for _ in shape))

    return pl.pallas_call(
        body, name="gla_fwd", grid=(NCH // GSUB,),
        in_specs=_p_specs(lambda n: n) + [full((128, KW)), full((1, KW)), full((1, DV)), ANY],
        out_specs=[rv_, rv_, pl.BlockSpec((GSUB, HEADS, DV, DK), lambda n: (n, 0, 0, 0))],
        out_shape=[_sds((TP, GW), F32), _sds((TP, GW), BF16), _sds((NCH, HEADS, DV, DK), F32)],
        scratch_shapes=[pltpu.VMEM((HEADS, DV, DK), F32)],
        compiler_params=_cp(("arbitrary",)),
    )(*([P] * 8), gw2, gb, gnw, _dep(dep))


def _gla_bwd(dog, o, P, gw2, gb, gnw, sp, dep=None):
    scale = DK ** -0.5

    def body(dog_ref, o_ref, p0, p1, p2, p3, p4, p5, p6, p7, gw2_ref, gb_ref, gnw_ref, sp_ref, dep_ref,
             dq_ref, dk_ref, dv_ref, dr_ref, dglr_ref, dgw2_ref, dgb_ref, dgnw_ref, ds_ref):
        step = pl.program_id(0)
        blk = NCH // GSUB - 1 - step

        @pl.when(step == 0)
        def _():
            ds_ref[...] = jnp.zeros_like(ds_ref)
            dgw2_ref[...] = jnp.zeros_like(dgw2_ref)
            dgb_ref[...] = jnp.zeros_like(dgb_ref)
            dgnw_ref[...] = jnp.zeros_like(dgnw_ref)

        q_blk, k_blk, v_blk, r_blk, glr_blk = _p_load(p0, p1, p2, p3, p4, p5, p6, p7)
        gw2_b = gw2_ref[...].astype(BF16)
        upper = lax.broadcasted_iota(jnp.int32, (CH, CH), 0) <= lax.broadcasted_iota(jnp.int32, (CH, CH), 1)
        gnw_v = gnw_ref[...]
        last = lax.broadcasted_iota(jnp.int32, (CH, 1), 0) == CH - 1
        for sub in reversed(range(GSUB)):
            rows = slice(sub * CH, (sub + 1) * CH)
            glr_v = glr_blk[rows]
            valid, g_raw, lower, G, Gl = _gla_common(GSUB * blk + sub, glr_v, gw2_ref[...], gb_ref[...])
            eG = jnp.exp(G)
            eN = jnp.exp(-G)
            eE = jnp.exp(Gl - G)
            dec = jnp.exp(Gl)
            dgnw_acc = jnp.zeros((1, DV), F32)
            dG_parts = []
            for h in range(HEADS):
                ks = slice(h * DK, (h + 1) * DK)
                vs = slice(h * DV, (h + 1) * DV)
                oh = o_ref[rows, vs]
                rv = r_blk[rows, vs]
                dg = dog_ref[rows, vs]
                sig = 1.0 / (1.0 + jnp.exp(-rv))
                gate = rv * sig
                rs = lax.rsqrt(jnp.mean(oh * oh, axis=-1, keepdims=True) + EPS)
                ohat = oh * rs
                dr_ref[rows, vs] = ((dg * (ohat * gnw_v)) * (sig * (1.0 + rv * (1.0 - sig)))).astype(BF16)
                don = dg * gate
                dgnw_acc = dgnw_acc + jnp.sum(don * ohat, axis=0, keepdims=True)
                gxn = don * gnw_v
                do = (rs * (gxn - ohat * jnp.mean(gxn * ohat, axis=-1, keepdims=True))).astype(BF16)
                kh = k_blk[rows, ks]
                vh = v_blk[rows, vs].astype(BF16)
                qd_f = (q_blk[rows, ks] * scale) * eG[:, ks]
                ki_f = kh * eN[:, ks]
                ke_f = kh * eE[:, ks]
                qd, ki, ke = qd_f.astype(BF16), ki_f.astype(BF16), ke_f.astype(BF16)
                spt = sp_ref[sub, h]
                dst = ds_ref[h]
                dst_b = dst.astype(BF16)
                a_t = jnp.where(upper, _dot_nt(ki, qd), 0.0).astype(BF16)
                da = jnp.where(lower, _dot_nt(do, vh), 0.0).astype(BF16)
                da_t = jnp.where(upper, _dot_nt(vh, do), 0.0).astype(BF16)
                dv_ref[rows, vs] = (_dot(a_t, do) + _dot_nt(ke, dst_b)).astype(BF16)
                dqd = _dot(da, ki) + _dot(do, spt.astype(BF16))
                dki = _dot(da_t, qd)
                dke = _dot(vh, dst_b)
                ddec = jnp.sum(spt * dst, axis=0, keepdims=True)
                ds_ref[h] = dst * dec[:, ks] + _dot_tn(do, qd)
                dq_ref[rows, ks] = ((dqd * eG[:, ks]) * scale).astype(BF16)
                dk_ref[rows, ks] = (dki * eN[:, ks] + dke * eE[:, ks]).astype(BF16)
                dke_ke = dke * ke_f
                dG = dqd * qd_f - dki * ki_f - dke_ke
                dGl = jnp.sum(dke_ke, axis=0, keepdims=True) + ddec * dec[:, ks]
                dG_parts.append(dG + jnp.where(last, dGl, 0.0))
            dgnw_ref[...] += dgnw_acc
            dG_all = jnp.concatenate(dG_parts, axis=1)
            dlogg = jnp.where(valid, _tri_sum(upper.astype(BF16), dG_all), 0.0)
            dg_raw = (dlogg * (1.0 / TAU)) * (1.0 / (1.0 + jnp.exp(g_raw)))
            dgb_ref[...] += jnp.sum(dg_raw, axis=0, keepdims=True)
            dg_b = dg_raw.astype(BF16)
            dgw2_ref[...] += _dot_tn(glr_v.astype(BF16), dg_b)
            dglr_ref[rows, :] = _dot_nt(dg_b, gw2_b).astype(BF16)

    def back(s):
        return NCH // GSUB - 1 - s

    rk = pl.BlockSpec((GSUB * CH, KW), lambda s: (back(s), 0))
    rv_ = pl.BlockSpec((GSUB * CH, GW), lambda s: (back(s), 0))
    rg = pl.BlockSpec((GSUB * CH, 128), lambda s: (back(s), 0))

    def full(shape):
        return pl.BlockSpec(shape, lambda s: tuple(0 for _ in shape))

    return pl.pallas_call(
        body, name="gla_bwd", grid=(NCH // GSUB,),
        in_specs=[rv_, rv_] + _p_specs(back) + [full((128, KW)), full((1, KW)), full((1, DV)),
                  pl.BlockSpec((GSUB, HEADS, DV, DK), lambda s: (back(s), 0, 0, 0)), ANY],
        out_specs=[rk, rk, rv_, rv_, rg, full((128, KW)), full((1, KW)), full((1, DV))],
        out_shape=[_sds((TP, KW), BF16), _sds((TP, KW), BF16), _sds((TP, GW), BF16), _sds((TP, GW), BF16),
                   _sds((TP, 128), BF16), _sds((128, KW), F32), _sds((1, KW), F32), _sds((1, DV), F32)],
        scratch_shapes=[pltpu.VMEM((HEADS, DV, DK), F32)],
        compiler_params=_cp(("arbitrary",)),
    )(dog, o, *([P] * 8), gw2, gb, gnw, sp, _dep(dep))


POOL_TR = 128
HALO = 16


def _pool_counts(base, nrows):
    rows = base + lax.broadcasted_iota(jnp.int32, (nrows, 1), 0)
    valid = (rows >= ROW_LO) & (rows < ROW_HI)
    t1 = (rows - ROW_LO + 1).astype(F32)
    cnts = [jnp.clip(t1, 1.0, float(w)) for w in WINDOWS]
    return valid, cnts


def _pool_fwd(P, pw, ps, dep=None):
    def body(cur_ref, prev_ref, pw_ref, ps_ref, dep_ref, y_ref, op_ref):
        i = pl.program_id(0)
        cur = cur_ref[0]
        full = jnp.concatenate([prev_ref[0], cur], axis=0)
        s2 = full + pltpu.roll(full, 1, 0)
        s4 = s2 + pltpu.roll(s2, 2, 0)
        s8 = s4 + pltpu.roll(s4, 4, 0)
        s16 = s8 + pltpu.roll(s8, 8, 0)
        valid, cnts = _pool_counts(i * POOL_TR, POOL_TR)
        for g, s in enumerate((s2, s4, s8, s16)):
            cs = slice(g * GC, (g + 1) * GC)
            y = s[HALO:, cs] / cnts[g] - cur[:, cs]
            yb = jnp.where(valid, y, 0.0).astype(BF16)
            y_ref[:, cs] = yb
            op_ref[:, cs] = (_dot(yb, pw_ref[g].astype(BF16)) * ps_ref[:, cs]).astype(BF16)

    row = pl.BlockSpec((POOL_TR, PW), lambda i: (i, 0))
    per = POOL_TR // HALO
    return pl.pallas_call(
        body, name="pool_fwd", grid=(TP // POOL_TR,),
        in_specs=[pl.BlockSpec((1, POOL_TR, PW), lambda i: (3, i, 0)),
                  pl.BlockSpec((1, HALO, PW), lambda i: (3, jnp.maximum(i * per - 1, 0), 0)),
                  pl.BlockSpec((4, GC, GC), lambda i: (0, 0, 0)), pl.BlockSpec((1, PW), lambda i: (0, 0)), ANY],
        out_specs=[row, row],
        out_shape=[_sds((TP, PW), BF16), _sds((TP, PW), BF16)],
        compiler_params=_cp(("arbitrary",)),
    )(P, P, pw, ps, _dep(dep))


def _pool_bwd(dop, y, pw, ps, dep=None):
    nblk = TP // HALO

    def body(cur_ref, nxt_ref, y_ref, pw_ref, ps_ref, dep_ref, dpu_ref, dpw_ref, dps_ref):
        i = pl.program_id(0)

        @pl.when(i == 0)
        def _():
            dpw_ref[...] = jnp.zeros_like(dpw_ref)
            dps_ref[...] = jnp.zeros_like(dps_ref)

        n_all = POOL_TR + HALO
        dcur = cur_ref[...]
        dall = jnp.concatenate([dcur, nxt_ref[...]], axis=0)
        valid, cnts = _pool_counts(i * POOL_TR, n_all)
        for g in range(4):
            cs = slice(g * GC, (g + 1) * GC)
            pwb = pw_ref[g].astype(BF16)
            yb = y_ref[:, cs]
            dyw = (dall[:, cs] * ps_ref[:, cs]).astype(BF16)
            dps_ref[:, cs] += jnp.sum(dcur[:, cs] * _dot(yb, pwb), axis=0, keepdims=True)
            dpw_ref[g] += _dot_tn(yb, dyw[0:POOL_TR, :])
            dyv = jnp.where(valid, _dot_nt(dyw, pwb), 0.0)
            e = dyv / cnts[g]
            w = WINDOWS[g]
            sh = 1
            while sh < w:
                e = e + pltpu.roll(e, n_all - sh, 0)
                sh *= 2
            dpu_ref[:, cs] = (e[0:POOL_TR, :] - dyv[0:POOL_TR, :]).astype(BF16)

    row = pl.BlockSpec((POOL_TR, PW), lambda i: (i, 0))
    per = POOL_TR // HALO
    return pl.pallas_call(
        body, name="pool_bwd", grid=(TP // POOL_TR,),
        in_specs=[pl.BlockSpec((POOL_TR, PW), lambda i: (i, 1)),
                  pl.BlockSpec((HALO, PW), lambda i: (jnp.minimum(i * per + per, nblk - 1), 1)),
                  row, pl.BlockSpec((4, GC, GC), lambda i: (0, 0, 0)), pl.BlockSpec((1, PW), lambda i: (0, 0)), ANY],
        out_specs=[row, pl.BlockSpec((4, GC, GC), lambda i: (0, 0, 0)), pl.BlockSpec((1, PW), lambda i: (0, 0))],
        out_shape=[_sds((TP, PW), BF16), _sds((4, GC, GC), F32), _sds((1, PW), F32)],
        compiler_params=_cp(("arbitrary",)),
    )(dop, dop, y, pw, ps, _dep(dep))


def _place():
    x, y, c = lax.axis_index("x"), lax.axis_index("y"), lax.axis_index("c")
    chips = [(1 - x, y), (x, 1 - y), (1 - x, 1 - y)]
    return x, y, c, chips


HBM = pl.BlockSpec(memory_space=pltpu.HBM)
SEM = pl.BlockSpec(memory_space=pltpu.SEMAPHORE)
EFFECT = pltpu.SideEffectType.DATAFLOW_SIDE_EFFECTING


def _cast_into(w, place, cols_out, name, dep=None):
    rows, cols = w.shape
    tr = 256

    def body(p_ref, w_ref, dep_ref, o_ref):
        if cols_out != cols:
            o_ref[0] = jnp.zeros((tr, cols_out), BF16)
            o_ref[0, :, 0:cols] = w_ref[...].astype(BF16)
        else:
            o_ref[0] = w_ref[...].astype(BF16)

    grid_spec = pltpu.PrefetchScalarGridSpec(
        num_scalar_prefetch=1, grid=(rows // tr,),
        in_specs=[pl.BlockSpec((tr, cols), lambda i, p: (i, 0)), ANY],
        out_specs=pl.BlockSpec((1, tr, cols_out), lambda i, p: (p[0], i, 0)))
    return pl.pallas_call(
        body, name=name, grid_spec=grid_spec,
        out_shape=_sds((N_CHIP, rows, cols_out), BF16),
        compiler_params=_cp(("arbitrary",)),
    )(place, w, _dep(dep))


def _cast_win(w, place, dep=None):
    rows, cols = w.shape
    tr = 256

    def body(p_ref, w_ref, dep_ref, o_ref, t_ref):
        t_ref[...] = jnp.zeros_like(t_ref)
        t_ref[:, 0:cols] = w_ref[...]
        t = t_ref[...]
        lane = lax.broadcasted_iota(jnp.int32, (tr, PAD_IN), 1)
        for kk in range(N_CHIP):
            @pl.when(p_ref[0] == kk)
            def _(kk=kk):
                if kk == 0:
                    placed = t
                elif kk < 3:
                    placed = pltpu.roll(t, 4 * kk, 1)
                else:
                    pool = pltpu.roll(t, PAD_IN - 4, 1)
                    gate = pltpu.roll(t, GW + 12, 1)
                    placed = jnp.where(lane < GW, pool, jnp.where((lane >= GW + 12) & (lane < GW + 16), gate, 0.0))
                o_ref[0] = placed.astype(BF16)

    grid_spec = pltpu.PrefetchScalarGridSpec(
        num_scalar_prefetch=1, grid=(rows // tr,),
        in_specs=[pl.BlockSpec((tr, cols), lambda i, p: (i, 0)), ANY],
        out_specs=pl.BlockSpec((1, tr, PAD_IN), lambda i, p: (p[0], i, 0)),
        scratch_shapes=[pltpu.VMEM((tr, PAD_IN), F32)])
    return pl.pallas_call(
        body, name="cast_win", grid_spec=grid_spec,
        out_shape=_sds((N_CHIP, rows, PAD_IN), BF16),
        compiler_params=_cp(("arbitrary",)),
    )(place, w, _dep(dep))


def _half_rows(ref, k, which):
    h = ref.shape[1] // 2
    return ref.at[k, pl.ds(pl.multiple_of(which * h, 8), h), :]


def _sent_rows(ref, k, which, whole):
    return ref.at[k] if whole else _half_rows(ref, k, which)


def _gather_start(ws, name, whole=None):
    n = len(ws)
    whole = whole or [False] * n

    def body(*refs):
        ins = refs[:n]
        ssems = refs[n:2 * n]
        rsems = refs[2 * n:3 * n]
        token = refs[4 * n]
        x, y, c, chips = _place()
        me = 2 * x + y
        for w in range(n):
            blk = _sent_rows(ins[w], me, c, whole[w])
            for j, chip in enumerate(chips):
                pltpu.make_async_remote_copy(src_ref=blk, dst_ref=blk, send_sem=ssems[w].at[j], recv_sem=rsems[w].at[j],
                                             device_id=(*chip, c), device_id_type=MESH).start()
        token[...] = jnp.zeros_like(token)

    sem3 = pltpu.SemaphoreType.DMA((3,))
    outs = pl.pallas_call(
        body, name=name,
        out_shape=tuple([sem3] * (2 * n) + [pltpu.HBM(w.shape, w.dtype) for w in ws] + [_sds((8, 128), F32)]),
        in_specs=(HBM,) * n, out_specs=(SEM,) * (2 * n) + (HBM,) * n + (VMEM_FULL,),
        input_output_aliases={w: 2 * n + w for w in range(n)},
        compiler_params=pltpu.CompilerParams(has_side_effects=EFFECT),
    )(*[pltpu.with_memory_space_constraint(w, pltpu.HBM) for w in ws])
    return outs[:n], outs[n:2 * n], outs[2 * n:3 * n], outs[3 * n]


def _gather_wait(w, ssem, rsem, after, name, whole=False):
    def body(w_ref, ssem_ref, rsem_ref, after_ref, out_ref):
        x, y, c, chips = _place()
        me = 2 * x + y
        mine = _sent_rows(w_ref, me, c, whole)
        for j, (cx, cy) in enumerate(chips):
            cp = pltpu.make_async_remote_copy(src_ref=mine, dst_ref=_sent_rows(w_ref, 2 * cx + cy, c, whole),
                                              send_sem=ssem_ref.at[j], recv_sem=rsem_ref.at[j],
                                              device_id=(cx, cy, c), device_id_type=MESH)
            cp.wait_send()
            cp.wait_recv()

    return pl.pallas_call(
        body, name=name, out_shape=pltpu.HBM(w.shape, w.dtype),
        in_specs=(HBM, SEM, SEM, ANY), out_specs=HBM, input_output_aliases={0: 0},
        compiler_params=pltpu.CompilerParams(has_side_effects=EFFECT),
    )(w, ssem, rsem, after)


def _gather_copies(ref, kind, ssem, rsem):
    x, y, c, _ = _place()
    xn, yn, sib = (1 - x, y, c), (x, 1 - y, c), (x, y, 1 - c)
    kx, ky, kd = 2 * (1 - x) + y, 2 * x + (1 - y), 2 * (1 - x) + (1 - y)
    half = ref.shape[1] // 2
    quarter = half // 2

    def piece(k, q):
        return ref.at[k, pl.ds(pl.multiple_of(c * half + q * quarter, 8), quarter), :]

    if kind == "d":
        blk = _half_rows(ref, 2 * x + y, c)
        pairs = [(blk, xn), (blk, yn)]
    elif kind == "r":
        pairs = [(piece(ky, 1), xn), (piece(kx, 0), yn)]
    elif kind == "fx":
        pairs = [(_half_rows(ref, kx, c), sib), (_half_rows(ref, ky, c), sib)]
    else:
        pairs = [(_half_rows(ref, kd, c), sib)]
    return [pltpu.make_async_remote_copy(src_ref=blk, dst_ref=blk, send_sem=ssem.at[i], recv_sem=rsem.at[i],
                                         device_id=to, device_id_type=MESH) for i, (blk, to) in enumerate(pairs)]


def _gather_step(name, arrs, waits, starts, sems_in=(), after=()):
    n, nw, ns = len(arrs), len(waits), len(starts)
    after = [a for a in after if a is not None] or [_dep(None)]

    def body(*refs):
        a_in = refs[:n]
        s_in = refs[n:n + 2 * nw]
        outs = refs[n + 2 * nw + len(after):]
        s_out = outs[:2 * ns]
        for i, (ai, kind) in enumerate(waits):
            for cp in _gather_copies(a_in[ai], kind, s_in[2 * i], s_in[2 * i + 1]):
                cp.wait_send()
                cp.wait_recv()
        for i, (ai, kind) in enumerate(starts):
            for cp in _gather_copies(a_in[ai], kind, s_out[2 * i], s_out[2 * i + 1]):
                cp.start()
        if ns:
            token = outs[2 * ns + n]
            token[...] = jnp.zeros_like(token)

    sem2 = pltpu.SemaphoreType.DMA((2,))
    flat_in = [s for pair in sems_in for s in pair]
    arrs = [pltpu.with_memory_space_constraint(a, pltpu.HBM) for a in arrs]
    outs = pl.pallas_call(
        body, name=name,
        out_shape=tuple([sem2] * (2 * ns) + [pltpu.HBM(a.shape, a.dtype) for a in arrs]
                        + ([_sds((8, 128), F32)] if ns else [])),
        in_specs=(HBM,) * n + (SEM,) * (2 * nw) + (ANY,) * len(after),
        out_specs=(SEM,) * (2 * ns) + (HBM,) * n + ((VMEM_FULL,) if ns else ()),
        input_output_aliases={i: 2 * ns + i for i in range(n)},
        compiler_params=pltpu.CompilerParams(has_side_effects=EFFECT),
    )(*arrs, *flat_in, *after)
    sems = [(outs[2 * i], outs[2 * i + 1]) for i in range(ns)]
    return sems, list(outs[2 * ns:2 * ns + n]), (outs[2 * ns + n] if ns else None)


def _rs_start(sb, name, after=None):
    _, half, cols = sb.shape

    def body(sb_ref, land_ref, after_ref, ssem, rsem, sb_out, land_out, token):
        x, y, c, chips = _place()
        for j, (cx, cy) in enumerate(chips):
            pltpu.make_async_remote_copy(src_ref=sb_ref.at[2 * cx + cy], dst_ref=land_ref.at[j], send_sem=ssem.at[j],
                                         recv_sem=rsem.at[j], device_id=(cx, cy, c), device_id_type=MESH).start()
        token[...] = jnp.zeros_like(token)

    sem3 = pltpu.SemaphoreType.DMA((3,))
    land = lax.empty((3, half, cols), BF16)
    return pl.pallas_call(
        body, name=name,
        out_shape=(sem3, sem3, pltpu.HBM(sb.shape, sb.dtype), pltpu.HBM(land.shape, land.dtype), _sds((8, 128), F32)),
        in_specs=(HBM, HBM, ANY), out_specs=(SEM, SEM, HBM, HBM, VMEM_FULL), input_output_aliases={0: 2, 1: 3},
        compiler_params=pltpu.CompilerParams(has_side_effects=EFFECT),
    )(pltpu.with_memory_space_constraint(sb, pltpu.HBM), pltpu.with_memory_space_constraint(land, pltpu.HBM), _dep(after))


def _rs_wait(items, after, name):
    n = len(items)

    def body(*refs):
        x, y, c, chips = _place()
        for i in range(n):
            sb_ref, land_ref, ssem_ref, rsem_ref = refs[4 * i:4 * i + 4]
            for j, (cx, cy) in enumerate(chips):
                cp = pltpu.make_async_remote_copy(src_ref=sb_ref.at[2 * cx + cy], dst_ref=land_ref.at[j],
                                                  send_sem=ssem_ref.at[j], recv_sem=rsem_ref.at[j],
                                                  device_id=(cx, cy, c), device_id_type=MESH)
                cp.wait_send()
                cp.wait_recv()

    outs = pl.pallas_call(
        body, name=name,
        out_shape=tuple(pltpu.HBM(a.shape, a.dtype) for it in items for a in it[:2]),
        in_specs=(HBM, HBM, SEM, SEM) * n + (ANY,), out_specs=(HBM,) * (2 * n),
        input_output_aliases={4 * i + k: 2 * i + k for i in range(n) for k in range(2)},
        compiler_params=pltpu.CompilerParams(has_side_effects=EFFECT),
    )(*[a for it in items for a in it], after)
    return [outs[2 * i + 1] for i in range(n)]


def _pair_copy(g_ref, land_ref, ssem, rsem):
    x, y, c, _ = _place()
    h = g_ref.shape[1] // 2
    src = g_ref.at[:, pl.ds(pl.multiple_of((1 - c) * h, 8), h), :]
    return pltpu.make_async_remote_copy(src_ref=src, dst_ref=land_ref, send_sem=ssem.at[0], recv_sem=rsem.at[0],
                                        device_id=(x, y, 1 - c), device_id_type=MESH)


def _pair_start(g, name):
    def body(g_ref, land_ref, ssem, rsem, g_out, land_out, token):
        _pair_copy(g_ref, land_ref, ssem, rsem).start()
        token[...] = jnp.zeros_like(token)

    sem1 = pltpu.SemaphoreType.DMA((1,))
    land = lax.empty((N_CHIP, g.shape[1] // 2, g.shape[2]), F32)
    return pl.pallas_call(
        body, name=name,
        out_shape=(sem1, sem1, pltpu.HBM(g.shape, g.dtype), pltpu.HBM(land.shape, land.dtype), _sds((8, 128), F32)),
        in_specs=(HBM, HBM), out_specs=(SEM, SEM, HBM, HBM, VMEM_FULL), input_output_aliases={0: 2, 1: 3},
        compiler_params=pltpu.CompilerParams(has_side_effects=EFFECT),
    )(pltpu.with_memory_space_constraint(g, pltpu.HBM), pltpu.with_memory_space_constraint(land, pltpu.HBM))


def _pair_wait(g, land, ssem, rsem, after, name):
    def body(g_ref, land_ref, ssem_ref, rsem_ref, after_ref, g_out, land_out):
        cp = _pair_copy(g_ref, land_ref, ssem_ref, rsem_ref)
        cp.wait_send()
        cp.wait_recv()

    return pl.pallas_call(
        body, name=name,
        out_shape=(pltpu.HBM(g.shape, g.dtype), pltpu.HBM(land.shape, land.dtype)),
        in_specs=(HBM, HBM, SEM, SEM, ANY), out_specs=(HBM, HBM), input_output_aliases={0: 0, 1: 1},
        compiler_params=pltpu.CompilerParams(has_side_effects=EFFECT),
    )(g, land, ssem, rsem, after)


def _pair_sum(g, rcv, place, name):
    _, rows, cols = g.shape
    half = rows // 2
    tr = min(512, half)
    nt = half // tr

    def body(p_ref, g_ref, r_ref, sb_ref, sf_ref):
        s = pl.program_id(1)
        tot = g_ref[0] + r_ref[0]
        sb_ref[0] = tot.astype(BF16)

        @pl.when(s == p_ref[0])
        def _():
            sf_ref[...] = tot

    grid_spec = pltpu.PrefetchScalarGridSpec(
        num_scalar_prefetch=1, grid=(nt, N_CHIP),
        in_specs=[pl.BlockSpec((1, tr, cols), lambda t, s, p: (s, p[1] * nt + t, 0)),
                  pl.BlockSpec((1, tr, cols), lambda t, s, p: (s, t, 0))],
        out_specs=[pl.BlockSpec((1, tr, cols), lambda t, s, p: (s, t, 0)),
                   pl.BlockSpec((tr, cols), lambda t, s, p: (t, 0))])
    return pl.pallas_call(
        body, name=name, grid_spec=grid_spec,
        out_shape=[_sds((N_CHIP, half, cols), BF16), _sds((half, cols), F32)],
        compiler_params=_cp(("arbitrary", "arbitrary")),
    )(place, g, rcv)


def _final_sum(sf, rb, place, name):
    half, cols = sf.shape
    tr = min(512, half)
    nt = half // tr

    def body(p_ref, sf_ref, r_ref, out_ref):
        acc = sf_ref[...]
        for j in range(3):
            acc = acc + r_ref[j].astype(F32)
        out_ref[...] = acc

    grid_spec = pltpu.PrefetchScalarGridSpec(
        num_scalar_prefetch=1, grid=(nt,),
        in_specs=[pl.BlockSpec((tr, cols), lambda t, p: (t, 0)), pl.BlockSpec((3, tr, cols), lambda t, p: (0, t, 0))],
        out_specs=pl.BlockSpec((tr, cols), lambda t, p: (p[1] * nt + t, 0)))
    return pl.pallas_call(
        body, name=name, grid_spec=grid_spec,
        out_shape=_sds((2 * half, cols), F32),
        compiler_params=_cp(("arbitrary",)),
    )(place, sf, rb)


def _half_copy(f_ref, which, ssem, rsem):
    x, y, c, _ = _place()
    h = f_ref.shape[0] // 2
    rows = f_ref.at[pl.ds(pl.multiple_of(which * h, 8), h), :]
    return pltpu.make_async_remote_copy(src_ref=rows, dst_ref=rows, send_sem=ssem.at[0], recv_sem=rsem.at[0],
                                        device_id=(x, y, 1 - c), device_id_type=MESH)


def _half_start(fulls, name, after=None):
    n = len(fulls)

    def body(*refs):
        for i in range(n):
            _half_copy(refs[i], lax.axis_index("c"), refs[n + 1 + 2 * i], refs[n + 2 + 2 * i]).start()
        token = refs[4 * n + 1]
        token[...] = jnp.zeros_like(token)

    sem1 = pltpu.SemaphoreType.DMA((1,))
    outs = pl.pallas_call(
        body, name=name,
        out_shape=tuple([sem1] * (2 * n) + [pltpu.HBM(f.shape, f.dtype) for f in fulls] + [_sds((8, 128), F32)]),
        in_specs=(HBM,) * n + (ANY,), out_specs=(SEM,) * (2 * n) + (HBM,) * n + (VMEM_FULL,),
        input_output_aliases={i: 2 * n + i for i in range(n)},
        compiler_params=pltpu.CompilerParams(has_side_effects=EFFECT),
    )(*[pltpu.with_memory_space_constraint(f, pltpu.HBM) for f in fulls], _dep(after))
    return [(outs[2 * i], outs[2 * i + 1], outs[2 * n + i]) for i in range(n)], outs[3 * n]


def _half_wait(items, after, name):
    n = len(items)

    def body(*refs):
        c = lax.axis_index("c")
        for i in range(n):
            ssem_ref, rsem_ref, f_ref = refs[3 * i:3 * i + 3]
            _half_copy(f_ref, c, ssem_ref, rsem_ref).wait_send()
            _half_copy(f_ref, 1 - c, ssem_ref, rsem_ref).wait_recv()

    return pl.pallas_call(
        body, name=name, out_shape=tuple(pltpu.HBM(it[2].shape, it[2].dtype) for it in items),
        in_specs=(SEM, SEM, HBM) * n + (ANY,) * len(after), out_specs=(HBM,) * n,
        input_output_aliases={3 * i + 2: i for i in range(n)},
        compiler_params=pltpu.CompilerParams(has_side_effects=EFFECT),
    )(*[a for it in items for a in it], *after)


def _small_copies(src_ref, land_ref, ssem, rsem, first):
    x, y, c, chips = _place()
    if first:
        return [pltpu.make_async_remote_copy(src_ref=src_ref, dst_ref=land_ref, send_sem=ssem.at[0], recv_sem=rsem.at[0],
                                             device_id=(x, y, 1 - c), device_id_type=MESH)]
    return [pltpu.make_async_remote_copy(src_ref=src_ref, dst_ref=land_ref.at[j], send_sem=ssem.at[j], recv_sem=rsem.at[j],
                                         device_id=(*chip, c), device_id_type=MESH) for j, chip in enumerate(chips)]


def _small_start(src, first, name, after=None):
    n = 1 if first else 3

    def body(src_ref, land_ref, after_ref, ssem, rsem, src_out, land_out, token):
        for cp in _small_copies(src_ref, land_ref, ssem, rsem, first):
            cp.start()
        token[...] = jnp.zeros_like(token)

    sems = pltpu.SemaphoreType.DMA((n,))
    land = lax.empty(src.shape if first else (3,) + src.shape, F32)
    return pl.pallas_call(
        body, name=name,
        out_shape=(sems, sems, pltpu.HBM(src.shape, F32), pltpu.HBM(land.shape, F32), _sds((8, 128), F32)),
        in_specs=(HBM, HBM, ANY), out_specs=(SEM, SEM, HBM, HBM, VMEM_FULL), input_output_aliases={0: 2, 1: 3},
        compiler_params=pltpu.CompilerParams(has_side_effects=EFFECT),
    )(pltpu.with_memory_space_constraint(src, pltpu.HBM), pltpu.with_memory_space_constraint(land, pltpu.HBM), _dep(after))


def _small_wait(src, land, ssem, rsem, first, after, name):
    def body(src_ref, land_ref, ssem_ref, rsem_ref, after_ref, src_out, land_out):
        for cp in _small_copies(src_ref, land_ref, ssem_ref, rsem_ref, first):
            cp.wait_send()
            cp.wait_recv()

    return pl.pallas_call(
        body, name=name,
        out_shape=(pltpu.HBM(src.shape, F32), pltpu.HBM(land.shape, F32)),
        in_specs=(HBM, HBM, SEM, SEM, ANY), out_specs=(HBM, HBM), input_output_aliases={0: 0, 1: 1},
        compiler_params=pltpu.CompilerParams(has_side_effects=EFFECT),
    )(src, land, ssem, rsem, after)


def _small_pair_sum(vec, got):
    def body(v_ref, g_ref, o_ref):
        o_ref[...] = v_ref[...] + g_ref[...]

    return pl.pallas_call(body, name="small_pair_sum", in_specs=[VMEM_FULL] * 2, out_specs=VMEM_FULL,
                          out_shape=_sds(vec.shape, F32), compiler_params=_cp())(vec, got)


def _small_chip_sum(pair, got, place):
    def body(p_ref, pair_ref, got_ref, o_ref):
        acc = None
        for kk in range(N_CHIP):
            d = jnp.bitwise_xor(p_ref[0], kk)
            t = jnp.where(d == 0, pair_ref[...], jnp.where(d == 2, got_ref[0], jnp.where(d == 1, got_ref[1], got_ref[2])))
            acc = t if acc is None else acc + t
        o_ref[...] = acc

    grid_spec = pltpu.PrefetchScalarGridSpec(
        num_scalar_prefetch=1, grid=(1,),
        in_specs=[pl.BlockSpec(pair.shape, lambda i, p: (0, 0)), pl.BlockSpec(got.shape, lambda i, p: (0, 0, 0))],
        out_specs=pl.BlockSpec(pair.shape, lambda i, p: (0, 0)))
    return pl.pallas_call(body, name="small_chip_sum", grid_spec=grid_spec, out_shape=_sds(pair.shape, F32),
                          compiler_params=_cp(("arbitrary",)))(place, pair, got)


def _adam_math(w, g, m, v):
    m = B1 * m + (1.0 - B1) * g
    v = B2 * v + (1.0 - B2) * (g * g)
    m_hat = m / (1.0 - B1 ** STEP)
    v_hat = v / (1.0 - B2 ** STEP)
    delta = -LR * (m_hat / (jnp.sqrt(v_hat) + AEPS) + WD * w)
    return delta, m, v


def _adam_big(w, g, m, v, name, dep=None):
    rows, cols = w.shape
    tr = 256

    def body(w_ref, g_ref, m_ref, v_ref, dep_ref, go_ref, d_ref, nm_ref, nv_ref):
        g = g_ref[...]
        d, nm, nv = _adam_math(w_ref[...], g, m_ref[...], v_ref[...])
        go_ref[...] = g
        d_ref[...] = d
        nm_ref[...] = nm
        nv_ref[...] = nv

    blk = pl.BlockSpec((tr, cols), lambda i: (i, 0))
    return pl.pallas_call(
        body, name=name, grid=(rows // tr,),
        in_specs=[blk] * 4 + [ANY], out_specs=[blk] * 4, out_shape=[_sds((rows, cols), F32)] * 4,
        compiler_params=_cp(("arbitrary",)),
    )(w, g, m, v, _dep(dep))


def _adam_small(ws, gs, ms, vs, dep=None):
    n = len(ws)

    def body(*refs):
        for i in range(n):
            d, nm, nv = _adam_math(refs[i][...], refs[n + i][...], refs[2 * n + i][...], refs[3 * n + i][...])
            refs[4 * n + 1 + i][...] = d
            refs[5 * n + 1 + i][...] = nm
            refs[6 * n + 1 + i][...] = nv

    shapes = [_sds(w.shape, F32) for w in ws]
    outs = pl.pallas_call(
        body, name="adam_small",
        in_specs=[VMEM_FULL] * (4 * n) + [ANY], out_specs=[VMEM_FULL] * (3 * n), out_shape=shapes * 3,
        compiler_params=_cp(),
    )(*ws, *gs, *ms, *vs, _dep(dep))
    return outs[:n], outs[n:2 * n], outs[2 * n:]


def _pad_rows8(a):
    flat = a.reshape(-1, 128)
    pad = (-flat.shape[0]) % 8
    if pad:
        flat = jnp.concatenate([flat, jnp.zeros((pad, 128), F32)], axis=0)
    return flat


def kernel(x, meta_tokens, norm1_w, w_in, gate_w2, gate_b, gla_norm_w, pool_w, pool_scale, w_out, norm2_w, mlp_w1, mlp_w2, final_norm_w, loss_target, m_meta_tokens, m_norm1_w, m_w_in, m_gate_w2, m_gate_b, m_gla_norm_w, m_pool_w, m_pool_scale, m_w_out, m_norm2_w, m_mlp_w1, m_mlp_w2, m_final_norm_w, v_meta_tokens, v_norm1_w, v_w_in, v_gate_w2, v_gate_b, v_gla_norm_w, v_pool_w, v_pool_scale, v_w_out, v_norm2_w, v_mlp_w1, v_mlp_w2, v_final_norm_w):
    cx, cy, cc = lax.axis_index("x"), lax.axis_index("y"), lax.axis_index("c")
    me = (2 * cx + cy).astype(jnp.int32)

    place = jnp.stack([me, cc.astype(jnp.int32)])
    fw = final_norm_w.reshape(1, D)

    mine = jnp.concatenate([meta_tokens.reshape(64, 128), gate_w2[0], pool_w[0].reshape(512, 128)], axis=0)
    small = lax.dynamic_update_slice(jnp.zeros((N_CHIP, 592, 128), F32), mine[None], (me, 0, 0))
    (s_sm,), (r_sm,), (f_sm,), tok = _gather_start([small], "gather_start_small", [True])
    (sem_win_d,), (win,), tok = _gather_step("gather_start_win", [_cast_win(w_in[0], place, tok)], [], [(0, "d")])
    wout, w1, w2 = (_cast_into(w_out[0], place, D, "cast_wout", tok), _cast_into(mlp_w1[0], place, D, "cast_w1", tok),
                    _cast_into(mlp_w2[0], place, D, "cast_w2", tok))
    small = _gather_wait(f_sm, s_sm, r_sm, w2, "gather_wait_small", True)
    metaF = jnp.concatenate([small[k, 0:64].reshape(N_META, 512) for k in range(N_CHIP)], axis=1)
    gw2F = jnp.concatenate([small[k, 64:80] for k in range(N_CHIP)], axis=1)
    pwF = jnp.concatenate([small[k, 80:592].reshape(4, 64, GC) for k in range(N_CHIP)], axis=1)

    fly = {"win": win, "wout": wout, "w1": w1, "w2": w2}
    sems = {"win_d": sem_win_d}

    def step(name, names, waits, starts, after):
        at = {nm: i for i, nm in enumerate(names)}
        new, arrs, tok = _gather_step(name, [fly[nm] for nm in names], [(at[nm], k) for nm, k in waits],
                                      [(at[nm], k) for nm, k in starts], [sems[nm + "_" + k] for nm, k in waits], after)
        fly.update(zip(names, arrs))
        sems.update({nm + "_" + k: s for (nm, k), s in zip(starts, new)})
        return tok

    def relay_first():
        return step("gather_relay_win", ["win", "wout", "w1"], [("win", "d")],
                    [("win", "r"), ("win", "fx"), ("wout", "d"), ("w1", "d")], [v_w_in[0]])

    def get_win(after):
        tok = step("gather_land_win", ["win"], [("win", "r")], [("win", "fd")], [after])
        step("gather_wait_win", ["win"], [("win", "fx"), ("win", "fd")], [], [tok])
        return fly["win"]

    def relay_mid(after):
        return step("gather_relay_mid", ["wout"], [("wout", "d")], [("wout", "r"), ("wout", "fx")], [after, m_w_in[0]])

    def land_wout(after):
        return step("gather_land_wout", ["wout", "w1", "w2"], [("wout", "r"), ("w1", "d")],
                    [("wout", "fd"), ("w1", "r"), ("w1", "fx"), ("w2", "d")], [after])

    def get_wout(after):
        step("gather_wait_wout", ["wout"], [("wout", "fx"), ("wout", "fd")], [], [after])
        tok = step("gather_land_w1", ["w1"], [("w1", "r")], [("w1", "fd")], [fly["wout"]])
        return fly["wout"].reshape(D, D), tok

    def get_w1(after):
        step("gather_wait_w1", ["w1"], [("w1", "fx"), ("w1", "fd")], [], [after])
        return fly["w1"]

    def relay_last(after):
        return step("gather_relay_w2", ["w2"], [("w2", "d")], [("w2", "r"), ("w2", "fx")], [after])

    def get_w2(after):
        tok = step("gather_land_w2", ["w2"], [("w2", "r")], [("w2", "fd")], [after])
        step("gather_wait_w2", ["w2"], [("w2", "fx"), ("w2", "fd")], [], [tok])
        return fly["w2"].reshape(DFF, D)

    pairs, pending = {}, {}

    halves = {}

    def reduce_(names, after, tag):
        items = [(pending[nm][3], pending[nm][4], pending[nm][1], pending[nm][2]) for nm in names]
        landed = _rs_wait(items, after, "rs_wait_" + tag)
        fulls = [_final_sum(pending[nm][0], rb, place, "final_sum_" + nm) for nm, rb in zip(names, landed)]
        sent, token = _half_start(fulls, "half_start_" + tag)
        halves.update(zip(names, sent))
        return token

    def grad_start(nm, g):
        ssem, rsem, g_thru, land, token = _pair_start(g, "pair_start_" + nm)
        pairs[nm] = (ssem, rsem, g_thru, land)
        if nm == "win":
            token = reduce_(["w2", "w1", "wout"], token, "mlp_wout")
        return token

    def grad_finish(nm, after):
        ssem, rsem, g_thru, land = pairs[nm]
        g, rcv = _pair_wait(g_thru, land, ssem, rsem, after, "pair_wait_" + nm)
        sb, sf = _pair_sum(g, rcv, place, "pair_sum_" + nm)
        ssem, rsem, sb_thru, land, token = _rs_start(sb, "rs_start_" + nm)
        pending[nm] = (sf, ssem, rsem, sb_thru, land)
        return token

    (grad_x, loss8, d_n1w, d_gb, d_gnw, d_ps, d_n2w, d_fw, d_meta, d_gw2, d_pw) = _local_step(
        x[0], loss_target[0], dict(relay_first=relay_first, win=get_win, relay_mid=relay_mid, land_wout=land_wout,
                                   wout=get_wout, w1=get_w1, relay_last=relay_last, w2=get_w2),
        metaF, gw2F, pwF, norm1_w, gate_b, gla_norm_w, pool_scale, norm2_w, fw, grad_start, grad_finish)
    return _reduce_and_update(
        me, place, pending, halves, reduce_, grad_x, loss8, d_n1w, d_gb, d_gnw, d_ps, d_n2w, d_fw, d_meta, d_gw2, d_pw,
        meta_tokens, norm1_w, w_in, gate_w2, gate_b, gla_norm_w, pool_w, pool_scale, w_out, norm2_w, mlp_w1, mlp_w2, fw,
        m_meta_tokens, m_norm1_w, m_w_in, m_gate_w2, m_gate_b, m_gla_norm_w, m_pool_w, m_pool_scale, m_w_out, m_norm2_w,
        m_mlp_w1, m_mlp_w2, m_final_norm_w, v_meta_tokens, v_norm1_w, v_w_in, v_gate_w2, v_gate_b, v_gla_norm_w, v_pool_w,
        v_pool_scale, v_w_out, v_norm2_w, v_mlp_w1, v_mlp_w2, v_final_norm_w)


def _local_step(x, target, gather, metaF, gw2F, pwF, norm1_w, gate_b, gla_norm_w, pool_scale, norm2_w, fw, grad_start,
                grad_finish):
    h0, u = _embed_norm(x, metaF, norm1_w, gather["relay_first"]())
    Win = gather["win"](u)
    P = _in_proj(u, Win)
    gw2p = jnp.pad(gw2F, ((0, 128 - RANK), (0, 0)))
    yb, op = _pool_fwd(P, pwF, pool_scale, gather["relay_mid"](P))
    o, og, sp = _gla_fwd(P, gw2p, gate_b, gla_norm_w, gather["land_wout"](op))
    Wout, tok = gather["wout"](og)
    h1 = _out_proj(og, op, Wout, h0, tok)
    n2 = _norm_rows(h1, norm2_w, "norm2")
    W1 = gather["w1"](n2)
    zr, a, tok = _mlp_up(n2, W1, 0)
    zr, a, _ = _mlp_up(n2, W1, 1, (zr, a), gather["relay_last"](tok))
    W2 = gather["w2"](a)
    h2 = _mlp_down(a, W2, h1)

    dh2, dh2b, d_fw, loss8 = _loss_head(h2, target, fw)
    tok = grad_start("w2", _grad_w2(a, dh2b).reshape(N_CHIP, D, D))
    dz = _mlp_dz(dh2b, W2, zr, tok)
    tok = grad_finish("w2", dz)
    tok = grad_start("w1", _grad_w1(n2, dz, tok))
    dn2 = _mlp_dn(dz, W1, tok)
    tok = grad_finish("w1", dn2)
    dh1, dh1b, d_n2w = _norm_bwd(dn2, h1, dh2, norm2_w, "norm2_bwd", tok)
    dmixed = _mixed_grad(dh1b, Wout)
    tok = grad_start("wout", _grad_wout(og, op, dh1b))
    dpu, d_pw, d_ps = _pool_bwd(dmixed, yb, pwF, pool_scale, tok)
    dq, dk, dv, dr, dglr, d_gw2p, d_gb, d_gnw = _gla_bwd(dmixed, o, P, gw2p, gate_b, gla_norm_w, sp, tok)
    d_gw2 = d_gw2p[0:RANK]
    tok = grad_finish("wout", dq)
    tok = grad_start("win", _grad_win(u, dq, dk, dv, dr, dglr, dpu, tok))
    du = _in_grad(dq, dk, dv, dr, dglr, dpu, Win, tok)
    tok = grad_finish("win", du)
    grad_x, d_meta, d_n1w = _input_grad(du, h0, dh1, norm1_w, tok)
    return grad_x, loss8, d_n1w, d_gb, d_gnw, d_ps, d_n2w, d_fw, d_meta, d_gw2, d_pw


def _reduce_and_update(me, place, pending, halves, reduce_, grad_x, loss8, d_n1w, d_gb, d_gnw, d_ps, d_n2w, d_fw, d_meta, d_gw2,
                       d_pw,
                       meta_tokens, norm1_w, w_in, gate_w2, gate_b, gla_norm_w, pool_w, pool_scale, w_out, norm2_w,
                       mlp_w1, mlp_w2, fw, m_meta_tokens, m_norm1_w, m_w_in, m_gate_w2, m_gate_b, m_gla_norm_w, m_pool_w,
                       m_pool_scale, m_w_out, m_norm2_w, m_mlp_w1, m_mlp_w2, m_final_norm_w, v_meta_tokens, v_norm1_w, v_w_in,
                       v_gate_w2, v_gate_b, v_gla_norm_w, v_pool_w, v_pool_scale, v_w_out, v_norm2_w, v_mlp_w1, v_mlp_w2,
                       v_final_norm_w):
    parts = [loss8, d_n1w, d_gb, d_gnw, d_ps, d_n2w, d_fw, d_meta, d_gw2, d_pw]
    packed = [_pad_rows8(p) for p in parts]
    sizes = [p.shape[0] for p in packed]
    vec = jnp.concatenate(packed, axis=0)

    big = {}
    params = {"w2": (mlp_w2[0], m_mlp_w2[0], v_mlp_w2[0]), "w1": (mlp_w1[0], m_mlp_w1[0], v_mlp_w1[0]),
              "wout": (w_out[0], m_w_out[0], v_w_out[0]), "win": (w_in[0], m_w_in[0], v_w_in[0])}

    def update(names, after, tag):
        fulls = _half_wait([halves[nm] for nm in names], [after], "half_wait_" + tag)
        tok = None
        for nm, full in zip(names, fulls):
            w, m, v = params[nm]
            big[nm] = _adam_big(w, full, m, v, "adam_" + nm, tok)
            tok = big[nm][3]
        return tok

    s1, r1, vec, land1, tok = _small_start(vec, True, "small_start_pair")
    tok = update(["w2"], tok, "w2")
    vec, got = _small_wait(vec, land1, s1, r1, True, tok, "small_wait_pair")
    pair = _small_pair_sum(vec, got)
    s2, r2, pair, land2, tok = _small_start(pair, False, "small_start_chips")
    tok = reduce_(["win"], tok, "win")
    tok = update(["w1"], tok, "w1")
    tok = update(["win"], tok, "win")
    pair, got = _small_wait(pair, land2, s2, r2, False, tok, "small_wait_chips")
    red = _small_chip_sum(pair, got, place)
    after = update(["wout"], red, "wout")
    offs = [0]
    for s in sizes:
        offs.append(offs[-1] + s)

    def take(i, shape):
        n = 1
        for d in shape:
            n *= d
        return red[offs[i]:offs[i] + n // 128].reshape(shape)

    loss = red[0, 0]
    G_n1w = take(1, (1, D))
    G_gb = take(2, (1, KW))
    G_gnw = take(3, (1, DV))
    G_ps = take(4, (1, PW))
    G_n2w = take(5, (1, D))
    G_fw = take(6, (1, D))
    G_meta = lax.dynamic_slice(take(7, (N_META, D)), (0, me * 512), (N_META, 512))
    G_gw2 = lax.dynamic_slice(take(8, (RANK, KW)), (0, me * 128), (RANK, 128))
    G_pw = lax.dynamic_slice(take(9, (4, GC, GC)), (0, me * 64, 0), (4, 64, GC))

    G_win, d_win, nm_win, nv_win = big["win"]
    G_wout, d_wout, nm_wout, nv_wout = big["wout"]
    G_w1, d_w1, nm_w1, nv_w1 = big["w1"]
    G_w2, d_w2, nm_w2, nv_w2 = big["w2"]
    ws = [meta_tokens, norm1_w, gate_w2[0], gate_b, gla_norm_w, pool_w[0], pool_scale, norm2_w, fw]
    gs = [G_meta, G_n1w, G_gw2, G_gb, G_gnw, G_pw, G_ps, G_n2w, G_fw]
    ms = [m_meta_tokens, m_norm1_w, m_gate_w2[0], m_gate_b, m_gla_norm_w, m_pool_w[0], m_pool_scale, m_norm2_w,
          m_final_norm_w.reshape(1, D)]
    vs = [v_meta_tokens, v_norm1_w, v_gate_w2[0], v_gate_b, v_gla_norm_w, v_pool_w[0], v_pool_scale, v_norm2_w,
          v_final_norm_w.reshape(1, D)]
    ds, nms, nvs = _adam_small(ws, gs, ms, vs, after)

    def assemble(small, win_, wout_, w1_, w2_):
        meta_, n1_, gw2_, gb_, gnw_, pw_, ps_, n2_, fw_ = small
        return (meta_, n1_, win_[None], gw2_[None], gb_, gnw_, pw_[None], ps_, wout_[None], n2_, w1_[None], w2_[None],
                fw_.reshape(D))

    grads_out = assemble(gs, G_win, G_wout, G_w1, G_w2)
    deltas = assemble(ds, d_win, d_wout, d_w1, d_w2)
    new_m = assemble(nms, nm_win, nm_wout, nm_w1, nm_w2)
    new_v = assemble(nvs, nv_win, nv_wout, nv_w1, nv_w2)
    return (loss, grad_x[None], *grads_out, *deltas, *new_m, *new_v)
```

```python
import functools

import jax
import jax.numpy as jnp
from jax import lax
from jax.experimental import pallas as pl
from jax.experimental.pallas import tpu as pltpu

F32 = jnp.float32
BF16 = jnp.bfloat16

D = 2048
SEQ = 2048
N_META = 16
CH = 64
TP = 2176
NCH = TP // CH
ROW_LO = 112
X_LO = 128
ROW_HI = TP
XT = 128
NXT = TP // XT
HEADS = 4
DK = 128
DV = 256
KW = HEADS * DK
GW = HEADS * DV
RANK = 16
TAU = 16.0
WINDOWS = (2, 4, 8, 16)
PW = 1024
GC = 256
DFF = 8192
EPS = 1e-6
SHARD_IN = 1028
PAD_IN = 1152
N_CHIP = 4

LR = 0.001
B1 = 0.9
B2 = 0.999
AEPS = 1e-08
WD = 0.01
STEP = 10

VMEM_LIMIT = 60 * 1024 * 1024
ANY = pl.BlockSpec(memory_space=pl.ANY)
VMEM_FULL = pl.BlockSpec(memory_space=pltpu.VMEM)
MESH = pl.DeviceIdType.MESH


def _cp(sem=None):
    if sem is None:
        return pltpu.CompilerParams(vmem_limit_bytes=VMEM_LIMIT)
    return pltpu.CompilerParams(dimension_semantics=sem, vmem_limit_bytes=VMEM_LIMIT)


def _dot(a, b):
    return jnp.dot(a, b, preferred_element_type=F32)


def _dot_nt(a, b):
    return lax.dot_general(a, b, (((1,), (1,)), ((), ())), preferred_element_type=F32)


def _dot_tn(a, b):
    return lax.dot_general(a, b, (((0,), (0,)), ((), ())), preferred_element_type=F32)


def _sds(shape, dtype):
    return jax.ShapeDtypeStruct(shape, dtype)


def _embed_norm(x, meta_full, w, dep=None):
    def body(x_ref, meta_ref, w_ref, dep_ref, h_ref, u_ref):
        i = pl.program_id(0)

        @pl.when(i == 0)
        def _():
            h_ref[...] = jnp.zeros_like(h_ref)
            h_ref[ROW_LO:X_LO, :] = meta_ref[...]

        @pl.when(i >= 1)
        def _():
            h_ref[...] = x_ref[...]

        h = h_ref[...]
        r = lax.rsqrt(jnp.mean(h * h, axis=-1, keepdims=True) + EPS)
        u_ref[...] = ((h * r) * w_ref[...]).astype(BF16)

    return pl.pallas_call(
        body, name="embed_norm1", grid=(NXT,),
        in_specs=[pl.BlockSpec((XT, D), lambda i: (jnp.maximum(i - 1, 0), 0)),
                  pl.BlockSpec((N_META, D), lambda i: (0, 0)),
                  pl.BlockSpec((1, D), lambda i: (0, 0)), ANY],
        out_specs=[pl.BlockSpec((XT, D), lambda i: (i, 0)), pl.BlockSpec((XT, D), lambda i: (i, 0))],
        out_shape=[_sds((TP, D), F32), _sds((TP, D), BF16)],
        compiler_params=_cp(("arbitrary",)),
    )(x, meta_full, w, _dep(dep))


def _norm_rows(h, w, name):
    tr = 272

    def body(h_ref, w_ref, o_ref):
        hv = h_ref[...]
        r = lax.rsqrt(jnp.mean(hv * hv, axis=-1, keepdims=True) + EPS)
        o_ref[...] = ((hv * r) * w_ref[...]).astype(BF16)

    return pl.pallas_call(
        body, name=name, grid=(TP // tr,),
        in_specs=[pl.BlockSpec((tr, D), lambda i: (i, 0)), pl.BlockSpec((1, D), lambda i: (0, 0))],
        out_specs=pl.BlockSpec((tr, D), lambda i: (i, 0)),
        out_shape=_sds((TP, D), BF16),
        compiler_params=_cp(("arbitrary",)),
    )(h, w)


def _loss_head(h2, target, fw):
    def body(h_ref, t_ref, w_ref, dh_ref, dhb_ref, dw_ref, loss_ref):
        i = pl.program_id(0)

        @pl.when(i == 0)
        def _():
            dw_ref[...] = jnp.zeros_like(dw_ref)
            loss_ref[...] = jnp.zeros_like(loss_ref)

        h = h_ref[...]
        w = w_ref[...]
        r = lax.rsqrt(jnp.mean(h * h, axis=-1, keepdims=True) + EPS)
        xh = h * r
        y = xh * w
        is_x = (i >= 1).astype(F32)
        diff = (y - t_ref[...]) * is_x
        loss_ref[...] += jnp.sum(diff * diff) * (0.5 / D)
        dy = diff * (1.0 / D)
        dw_ref[...] += jnp.sum(dy * xh, axis=0, keepdims=True)
        gx = dy * w
        dh = r * (gx - xh * jnp.mean(gx * xh, axis=-1, keepdims=True))
        dh_ref[...] = dh
        dhb_ref[...] = dh.astype(BF16)

    return pl.pallas_call(
        body, name="loss_head", grid=(NXT,),
        in_specs=[pl.BlockSpec((XT, D), lambda i: (i, 0)),
                  pl.BlockSpec((XT, D), lambda i: (jnp.maximum(i - 1, 0), 0)),
                  pl.BlockSpec((1, D), lambda i: (0, 0))],
        out_specs=[pl.BlockSpec((XT, D), lambda i: (i, 0)), pl.BlockSpec((XT, D), lambda i: (i, 0)),
                   pl.BlockSpec((1, D), lambda i: (0, 0)), pl.BlockSpec((8, 128), lambda i: (0, 0))],
        out_shape=[_sds((TP, D), F32), _sds((TP, D), BF16), _sds((1, D), F32), _sds((8, 128), F32)],
        compiler_params=_cp(("arbitrary",)),
    )(h2, target, fw)


def _norm_bwd(dn, h, dres, w, name, dep=None):
    tr = 272

    def body(dn_ref, h_ref, dres_ref, w_ref, dep_ref, o_ref, ob_ref, dw_ref):
        @pl.when(pl.program_id(0) == 0)
        def _():
            dw_ref[...] = jnp.zeros_like(dw_ref)

        hv = h_ref[...]
        dnv = dn_ref[...]
        r = lax.rsqrt(jnp.mean(hv * hv, axis=-1, keepdims=True) + EPS)
        xh = hv * r
        dw_ref[...] += jnp.sum(dnv * xh, axis=0, keepdims=True)
        gx = dnv * w_ref[...]
        dh = dres_ref[...] + r * (gx - xh * jnp.mean(gx * xh, axis=-1, keepdims=True))
        o_ref[...] = dh
        ob_ref[...] = dh.astype(BF16)

    row = pl.BlockSpec((tr, D), lambda i: (i, 0))
    vec = pl.BlockSpec((1, D), lambda i: (0, 0))
    return pl.pallas_call(
        body, name=name, grid=(TP // tr,),
        in_specs=[row, row, row, vec, ANY], out_specs=[row, row, vec],
        out_shape=[_sds((TP, D), F32), _sds((TP, D), BF16), _sds((1, D), F32)],
        compiler_params=_cp(("arbitrary",)),
    )(dn, h, dres, w, _dep(dep))


def _input_grad(du, h0, dh1, w, dep=None):
    def body(du_ref, h_ref, dres_ref, w_ref, dep_ref, gx_ref, gm_ref, dw_ref):
        i = pl.program_id(0)

        @pl.when(i == 0)
        def _():
            dw_ref[...] = jnp.zeros_like(dw_ref)

        hv = h_ref[...]
        dnv = du_ref[...]
        r = lax.rsqrt(jnp.mean(hv * hv, axis=-1, keepdims=True) + EPS)
        xh = hv * r
        dw_ref[...] += jnp.sum(dnv * xh, axis=0, keepdims=True)
        g = dnv * w_ref[...]
        dh = dres_ref[...] + r * (g - xh * jnp.mean(g * xh, axis=-1, keepdims=True))

        @pl.when(i == 0)
        def _():
            gm_ref[...] = dh[ROW_LO:X_LO, :]

        @pl.when(i >= 1)
        def _():
            gx_ref[...] = dh

    row = pl.BlockSpec((XT, D), lambda i: (i, 0))
    vec = pl.BlockSpec((1, D), lambda i: (0, 0))
    return pl.pallas_call(
        body, name="input_grad", grid=(NXT,),
        in_specs=[row, row, row, vec, ANY],
        out_specs=[pl.BlockSpec((XT, D), lambda i: (jnp.maximum(i - 1, 0), 0)),
                   pl.BlockSpec((N_META, D), lambda i: (0, 0)), vec],
        out_shape=[_sds((SEQ, D), F32), _sds((N_META, D), F32), _sds((1, D), F32)],
        compiler_params=_cp(("arbitrary",)),
    )(du, h0, dh1, w, _dep(dep))


def _in_proj(u, wg):
    def body(u_ref, w_ref, o_ref):
        o_ref[0] = _dot(u_ref[...], w_ref[0])

    return pl.pallas_call(
        body, name="in_proj", grid=(N_CHIP,),
        in_specs=[VMEM_FULL, pl.BlockSpec((1, D, PAD_IN), lambda k: (k, 0, 0))],
        out_specs=pl.BlockSpec((1, TP, PAD_IN), lambda k: (k, 0, 0)),
        out_shape=_sds((N_CHIP, TP, PAD_IN), F32),
        compiler_params=_cp(("arbitrary",)),
    )(u, wg)


def _out_proj(og, op, wout, h0, dep=None):
    tn = 512

    def body(og_ref, op_ref, w_ref, h_ref, dep_ref, o_ref):
        acc = _dot(og_ref[...], w_ref[0:GW, :]) + _dot(op_ref[...], w_ref[GW:D, :])
        o_ref[...] = h_ref[...] + acc

    return pl.pallas_call(
        body, name="out_proj", grid=(D // tn,),
        in_specs=[VMEM_FULL, VMEM_FULL, pl.BlockSpec((D, tn), lambda j: (0, j)),
                  pl.BlockSpec((TP, tn), lambda j: (0, j)), ANY],
        out_specs=pl.BlockSpec((TP, tn), lambda j: (0, j)),
        out_shape=_sds((TP, D), F32),
        compiler_params=_cp(("arbitrary",)),
    )(og, op, wout, h0, _dep(dep))


def _mlp_up(n2, w1g, part, prev=None, dep=None):
    tn = 1024
    per = D // tn

    def body(n_ref, w_ref, dep_ref, *rest):
        zr_ref, a_ref, token = rest[-3:]
        z = jnp.maximum(_dot(n_ref[...], w_ref[0]), 0.0)
        zr_ref[...] = z.astype(BF16)
        a_ref[...] = (z * z).astype(BF16)
        token[...] = jnp.zeros_like(token)

    col = pl.BlockSpec((TP, tn), lambda k, j: (0, (2 * part + k) * per + j))
    return pl.pallas_call(
        body, name="mlp_up_%d" % part, grid=(N_CHIP // 2, per),
        in_specs=[VMEM_FULL, pl.BlockSpec((1, D, tn), lambda k, j: (2 * part + k, 0, j)), ANY] + ([ANY, ANY] if prev else []),
        out_specs=[col, col, pl.BlockSpec((8, 128), lambda k, j: (0, 0))],
        out_shape=[_sds((TP, DFF), BF16), _sds((TP, DFF), BF16), _sds((8, 128), F32)],
        input_output_aliases={3: 0, 4: 1} if prev else {},
        compiler_params=_cp(("arbitrary", "arbitrary")),
    )(n2, w1g, _dep(dep), *(prev or ()))


def _mlp_down(a, w2, h1):
    tk = 1024
    nk = DFF // tk

    def body(a_ref, w_ref, h_ref, o_ref, acc_ref):
        k = pl.program_id(0)

        @pl.when(k == 0)
        def _():
            pltpu.sync_copy(h_ref, acc_ref)

        acc_ref[...] += _dot(a_ref[...], w_ref[...])

        @pl.when(k == nk - 1)
        def _():
            pltpu.sync_copy(acc_ref, o_ref)

    return pl.pallas_call(
        body, name="mlp_down", grid=(nk,),
        in_specs=[pl.BlockSpec((TP, tk), lambda k: (0, k)), pl.BlockSpec((tk, D), lambda k: (k, 0)), ANY],
        out_specs=ANY,
        out_shape=_sds((TP, D), F32),
        scratch_shapes=[pltpu.VMEM((TP, D), F32)],
        compiler_params=_cp(("arbitrary",)),
    )(a, w2, h1)


def _mlp_dz(dh2b, w2, zr, dep=None):
    tn = 1024

    def body(d_ref, w_ref, z_ref, dep_ref, o_ref):
        da = _dot_nt(d_ref[...], w_ref[...])
        o_ref[...] = (da * (2.0 * z_ref[...].astype(F32))).astype(BF16)

    col = pl.BlockSpec((TP, tn), lambda j: (0, j))
    return pl.pallas_call(
        body, name="mlp_dz", grid=(DFF // tn,),
        in_specs=[VMEM_FULL, pl.BlockSpec((tn, D), lambda j: (j, 0)), col, ANY],
        out_specs=col,
        out_shape=_sds((TP, DFF), BF16),
        compiler_params=_cp(("arbitrary",)),
    )(dh2b, w2, zr, _dep(dep))


def _grad_w2(a, dh2b):
    tm = 1024

    def body(a_ref, d_ref, o_ref):
        o_ref[...] = _dot_tn(a_ref[...], d_ref[...])

    return pl.pallas_call(
        body, name="grad_w2", grid=(DFF // tm,),
        in_specs=[pl.BlockSpec((TP, tm), lambda j: (0, j)), VMEM_FULL],
        out_specs=pl.BlockSpec((tm, D), lambda j: (j, 0)),
        out_shape=_sds((DFF, D), F32),
        compiler_params=_cp(("arbitrary",)),
    )(a, dh2b)


def _dep(token):
    return jnp.zeros((8, 128), F32) if token is None else token


def _grad_w1(n2, dz, dep=None):
    tn = 1024
    per = D // tn

    def body(n_ref, d_ref, dep_ref, o_ref):
        o_ref[0] = _dot_tn(n_ref[...], d_ref[...])

    return pl.pallas_call(
        body, name="grad_w1", grid=(N_CHIP, per),
        in_specs=[VMEM_FULL, pl.BlockSpec((TP, tn), lambda k, j: (0, k * per + j)), ANY],
        out_specs=pl.BlockSpec((1, D, tn), lambda k, j: (k, 0, j)),
        out_shape=_sds((N_CHIP, D, D), F32),
        compiler_params=_cp(("arbitrary", "arbitrary")),
    )(n2, dz, _dep(dep))


def _mlp_dn(dz, w1g, dep=None):
    tk = 1024
    per = D // tk
    nk = DFF // tk

    def body(d_ref, w_ref, dep_ref, o_ref, acc_ref):
        k = pl.program_id(0)
        part = _dot_nt(d_ref[...], w_ref[0])

        @pl.when(k == 0)
        def _():
            acc_ref[...] = part

        @pl.when(k > 0)
        def _():
            acc_ref[...] += part

        @pl.when(k == nk - 1)
        def _():
            pltpu.sync_copy(acc_ref, o_ref)

    return pl.pallas_call(
        body, name="mlp_dn", grid=(nk,),
        in_specs=[pl.BlockSpec((TP, tk), lambda k: (0, k)),
                  pl.BlockSpec((1, D, tk), lambda k: (k // per, 0, k % per)), ANY],
        out_specs=ANY,
        out_shape=_sds((TP, D), F32),
        scratch_shapes=[pltpu.VMEM((TP, D), F32)],
        compiler_params=_cp(("arbitrary",)),
    )(dz, w1g, _dep(dep))


def _mixed_grad(dh1b, wout):
    tn = 512

    def body(d_ref, w_ref, o_ref):
        o_ref[...] = _dot_nt(d_ref[...], w_ref[...])

    return pl.pallas_call(
        body, name="mixed_grad", grid=(D // tn,),
        in_specs=[VMEM_FULL, pl.BlockSpec((tn, D), lambda j: (j, 0))],
        out_specs=pl.BlockSpec((TP, tn), lambda j: (0, j)),
        out_shape=_sds((TP, D), F32),
        compiler_params=_cp(("arbitrary",)),
    )(dh1b, wout)


def _grad_wout(og, op, dh1b):
    tm = 512

    def body(og_ref, op_ref, d_ref, o_ref):
        j = pl.program_id(0)

        @pl.when(j < 2)
        def _():
            o_ref[0] = _dot_tn(og_ref[...], d_ref[...])

        @pl.when(j >= 2)
        def _():
            o_ref[0] = _dot_tn(op_ref[...], d_ref[...])

    return pl.pallas_call(
        body, name="grad_wout", grid=(N_CHIP,),
        in_specs=[pl.BlockSpec((TP, tm), lambda j: (0, jnp.minimum(j, 1))),
                  pl.BlockSpec((TP, tm), lambda j: (0, jnp.maximum(j - 2, 0))), VMEM_FULL],
        out_specs=pl.BlockSpec((1, tm, D), lambda j: (j, 0, 0)),
        out_shape=_sds((N_CHIP, tm, D), F32),
        compiler_params=_cp(("arbitrary",)),
    )(og, op, dh1b)


def _in_grad(dq, dk, dv, dr, dglr, dpu, wg, dep=None):
    def body(dq_ref, dk_ref, dv_ref, dr_ref, dg_ref, dpu_ref, w_ref, dep_ref, o_ref):
        dv, dr, dg = dv_ref[...], dr_ref[...], dg_ref[...]
        head, tail = slice(0, GW), slice(GW, PAD_IN)
        o_ref[...] = (_dot_nt(dq_ref[...], w_ref[0, :, 0:KW]) + _dot_nt(dk_ref[...], w_ref[0, :, KW:GW])
                      + _dot_nt(dv[:, 0:128], w_ref[0, :, tail])
                      + _dot_nt(dv, w_ref[1, :, head]) + _dot_nt(dr[:, 0:128], w_ref[1, :, tail])
                      + _dot_nt(dr, w_ref[2, :, head]) + _dot_nt(dg, w_ref[2, :, tail])
                      + _dot_nt(dpu_ref[...], w_ref[3, :, head]) + _dot_nt(dg, w_ref[3, :, tail]))

    tn = 512
    return pl.pallas_call(
        body, name="in_grad", grid=(D // tn,),
        in_specs=[VMEM_FULL] * 6 + [pl.BlockSpec((N_CHIP, tn, PAD_IN), lambda j: (0, j, 0)), ANY],
        out_specs=pl.BlockSpec((TP, tn), lambda j: (0, j)),
        out_shape=_sds((TP, D), F32),
        compiler_params=_cp(("arbitrary",)),
    )(dq, dk, dv, dr, dglr, dpu, wg, _dep(dep))


def _grad_win(u, dq, dk, dv, dr, dglr, dpu, dep=None):
    tm = 512

    def body(u_ref, dq_hbm, dk_hbm, dv_hbm, dr_hbm, dg_hbm, dpu_hbm, dep_ref, o_ref, dp_ref, sem):
        k, m = pl.program_id(0), pl.program_id(1)
        head, tail = slice(0, GW), slice(GW, PAD_IN)
        pieces = [[(dq_hbm, slice(0, KW)), (dk_hbm, slice(KW, GW)), (dv_hbm.at[:, 0:128], tail)],
                  [(dv_hbm, head), (dr_hbm.at[:, 0:128], tail)],
                  [(dr_hbm, head), (dg_hbm, tail)],
                  [(dpu_hbm, head), (dg_hbm, tail)]]

        def copies(kk):
            return [pltpu.make_async_copy(src, dp_ref.at[kk % 2, :, cols], sem.at[kk % 2, i])
                    for i, (src, cols) in enumerate(pieces[kk])]

        @pl.when((k == 0) & (m == 0))
        def _():
            for cp in copies(0):
                cp.start()

        for kk in range(N_CHIP):
            @pl.when((k == kk) & (m == 0))
            def _(kk=kk):
                for cp in copies(kk):
                    cp.wait()
                if kk + 1 < N_CHIP:
                    for cp in copies(kk + 1):
                        cp.start()

        g = _dot_tn(u_ref[...], dp_ref[k % 2])
        lane = lax.broadcasted_iota(jnp.int32, (tm, PAD_IN), 1)
        for kk in range(N_CHIP):
            @pl.when(k == kk)
            def _(kk=kk):
                if kk == 0:
                    nat = g
                elif kk < 3:
                    nat = pltpu.roll(g, PAD_IN - 4 * kk, 1)
                else:
                    nat = jnp.where(lane < 4, pltpu.roll(g, PAD_IN - (GW + 12), 1), pltpu.roll(g, 4, 1))
                o_ref[0] = nat[:, 0:SHARD_IN]

    return pl.pallas_call(
        body, name="grad_win", grid=(N_CHIP, D // tm),
        in_specs=[pl.BlockSpec((TP, tm), lambda k, m: (0, m))] + [ANY] * 7,
        out_specs=pl.BlockSpec((1, tm, SHARD_IN), lambda k, m: (k, m, 0)),
        out_shape=_sds((N_CHIP, D, SHARD_IN), F32),
        scratch_shapes=[pltpu.VMEM((2, TP, PAD_IN), BF16), pltpu.SemaphoreType.DMA((2, 3))],
        compiler_params=_cp(("arbitrary", "arbitrary")),
    )(u, dq, dk, dv, dr, dglr, dpu, _dep(dep))


def _split3(x):
    hi = x.astype(BF16)
    r1 = x - hi.astype(F32)
    mid = r1.astype(BF16)
    lo = (r1 - mid.astype(F32)).astype(BF16)
    return hi, mid, lo


def _tri_sum(tri, x):
    hi, mid, lo = _split3(x)
    return _dot(tri, hi) + _dot(tri, mid) + _dot(tri, lo)


def _gla_common(n, glr, gw2, gb):
    rows = n * CH + lax.broadcasted_iota(jnp.int32, (CH, 1), 0)
    valid = (rows >= ROW_LO) & (rows < ROW_HI)
    g_raw = _dot(glr.astype(BF16), gw2.astype(BF16)) + gb
    logsig = jnp.minimum(g_raw, 0.0) - jnp.log(1.0 + jnp.exp(-jnp.abs(g_raw)))
    logg = jnp.where(valid, logsig * (1.0 / TAU), 0.0)
    ci = lax.broadcasted_iota(jnp.int32, (CH, CH), 0)
    si = lax.broadcasted_iota(jnp.int32, (CH, CH), 1)
    lower = ci >= si
    G = _tri_sum(lower.astype(BF16), logg)
    Gl = G[CH - 1:CH, :]
    return valid, g_raw, lower, G, Gl


GSUB = 2


def _p_specs(index):
    def spec(width, shard, col):
        return pl.BlockSpec((1, GSUB * CH, width), lambda s: (shard, index(s), col))

    return [spec(KW, 0, 0), spec(KW, 0, 1), spec(GW, 1, 0), spec(128, 0, 8), spec(GW, 2, 0), spec(128, 1, 8),
            spec(128, 2, 8), spec(128, 3, 8)]


def _p_load(q_ref, k_ref, vm_ref, vh_ref, rm_ref, rh_ref, ga_ref, gb_ref):
    def joined(main, head):
        return jnp.concatenate([main[:, 0:128] + head, main[:, 128:]], axis=1)

    return q_ref[0], k_ref[0], joined(vm_ref[0], vh_ref[0]), joined(rm_ref[0], rh_ref[0]), ga_ref[0] + gb_ref[0]


def _gla_fwd(P, gw2, gb, gnw, dep=None):
    scale = DK ** -0.5

    def body(p0, p1, p2, p3, p4, p5, p6, p7, gw2_ref, gb_ref, gnw_ref, dep_ref, o_ref, og_ref, sp_ref, st_ref):
        n = pl.program_id(0)

        @pl.when(n == 0)
        def _():
            st_ref[...] = jnp.zeros_like(st_ref)

        q_blk, k_blk, v_blk, r_blk, glr_blk = _p_load(p0, p1, p2, p3, p4, p5, p6, p7)
        gnw_v = gnw_ref[...]
        for sub in range(GSUB):
            rows = slice(sub * CH, (sub + 1) * CH)
            _, _, lower, G, Gl = _gla_common(GSUB * n + sub, glr_blk[rows], gw2_ref[...], gb_ref[...])
            eG = jnp.exp(G)
            eN = jnp.exp(-G)
            eE = jnp.exp(Gl - G)
            dec = jnp.exp(Gl)
            for h in range(HEADS):
                ks = slice(h * DK, (h + 1) * DK)
                vs = slice(h * DV, (h + 1) * DV)
                kh = k_blk[rows, ks]
                vh = v_blk[rows, vs].astype(BF16)
                qd = ((q_blk[rows, ks] * scale) * eG[:, ks]).astype(BF16)
                ki = (kh * eN[:, ks]).astype(BF16)
                ke = (kh * eE[:, ks]).astype(BF16)
                st = st_ref[h]
                a = jnp.where(lower, _dot_nt(qd, ki), 0.0).astype(BF16)
                o = _dot(a, vh) + _dot_nt(qd, st.astype(BF16))
                sp_ref[sub, h] = st
                st_ref[h] = st * dec[:, ks] + _dot_tn(vh, ke)
                o_ref[rows, vs] = o
                rs = lax.rsqrt(jnp.mean(o * o, axis=-1, keepdims=True) + EPS)
                rv = r_blk[rows, vs]
                gate = rv / (1.0 + jnp.exp(-rv))
                og_ref[rows, vs] = (((o * rs) * gnw_v) * gate).astype(BF16)

    rv_ = pl.BlockSpec((GSUB * CH, GW), lambda n: (n, 0))

    def full(shape):
        return pl.BlockSpec(shape, lambda n: tuple(0 for _ in shape))

    return pl.pallas_call(
        body, name="gla_fwd", grid=(NCH // GSUB,),
        in_specs=_p_specs(lambda n: n) + [full((128, KW)), full((1, KW)), full((1, DV)), ANY],
        out_specs=[rv_, rv_, pl.BlockSpec((GSUB, HEADS, DV, DK), lambda n: (n, 0, 0, 0))],
        out_shape=[_sds((TP, GW), F32), _sds((TP, GW), BF16), _sds((NCH, HEADS, DV, DK), F32)],
        scratch_shapes=[pltpu.VMEM((HEADS, DV, DK), F32)],
        compiler_params=_cp(("arbitrary",)),
    )(*([P] * 8), gw2, gb, gnw, _dep(dep))


def _gla_bwd(dog, o, P, gw2, gb, gnw, sp, dep=None):
    scale = DK ** -0.5

    def body(dog_ref, o_ref, p0, p1, p2, p3, p4, p5, p6, p7, gw2_ref, gb_ref, gnw_ref, sp_ref, dep_ref,
             dq_ref, dk_ref, dv_ref, dr_ref, dglr_ref, dgw2_ref, dgb_ref, dgnw_ref, ds_ref):
        step = pl.program_id(0)
        blk = NCH // GSUB - 1 - step

        @pl.when(step == 0)
        def _():
            ds_ref[...] = jnp.zeros_like(ds_ref)
            dgw2_ref[...] = jnp.zeros_like(dgw2_ref)
            dgb_ref[...] = jnp.zeros_like(dgb_ref)
            dgnw_ref[...] = jnp.zeros_like(dgnw_ref)

        q_blk, k_blk, v_blk, r_blk, glr_blk = _p_load(p0, p1, p2, p3, p4, p5, p6, p7)
        gw2_b = gw2_ref[...].astype(BF16)
        upper = lax.broadcasted_iota(jnp.int32, (CH, CH), 0) <= lax.broadcasted_iota(jnp.int32, (CH, CH), 1)
        gnw_v = gnw_ref[...]
        last = lax.broadcasted_iota(jnp.int32, (CH, 1), 0) == CH - 1
        for sub in reversed(range(GSUB)):
            rows = slice(sub * CH, (sub + 1) * CH)
            glr_v = glr_blk[rows]
            valid, g_raw, lower, G, Gl = _gla_common(GSUB * blk + sub, glr_v, gw2_ref[...], gb_ref[...])
            eG = jnp.exp(G)
            eN = jnp.exp(-G)
            eE = jnp.exp(Gl - G)
            dec = jnp.exp(Gl)
            dgnw_acc = jnp.zeros((1, DV), F32)
            dG_parts = []
            for h in range(HEADS):
                ks = slice(h * DK, (h + 1) * DK)
                vs = slice(h * DV, (h + 1) * DV)
                oh = o_ref[rows, vs]
                rv = r_blk[rows, vs]
                dg = dog_ref[rows, vs]
                sig = 1.0 / (1.0 + jnp.exp(-rv))
                gate = rv * sig
                rs = lax.rsqrt(jnp.mean(oh * oh, axis=-1, keepdims=True) + EPS)
                ohat = oh * rs
                dr_ref[rows, vs] = ((dg * (ohat * gnw_v)) * (sig * (1.0 + rv * (1.0 - sig)))).astype(BF16)
                don = dg * gate
                dgnw_acc = dgnw_acc + jnp.sum(don * ohat, axis=0, keepdims=True)
                gxn = don * gnw_v
                do = (rs * (gxn - ohat * jnp.mean(gxn * ohat, axis=-1, keepdims=True))).astype(BF16)
                kh = k_blk[rows, ks]
                vh = v_blk[rows, vs].astype(BF16)
                qd_f = (q_blk[rows, ks] * scale) * eG[:, ks]
                ki_f = kh * eN[:, ks]
                ke_f = kh * eE[:, ks]
                qd, ki, ke = qd_f.astype(BF16), ki_f.astype(BF16), ke_f.astype(BF16)
                spt = sp_ref[sub, h]
                dst = ds_ref[h]
                dst_b = dst.astype(BF16)
                a_t = jnp.where(upper, _dot_nt(ki, qd), 0.0).astype(BF16)
                da = jnp.where(lower, _dot_nt(do, vh), 0.0).astype(BF16)
                da_t = jnp.where(upper, _dot_nt(vh, do), 0.0).astype(BF16)
                dv_ref[rows, vs] = (_dot(a_t, do) + _dot_nt(ke, dst_b)).astype(BF16)
                dqd = _dot(da, ki) + _dot(do, spt.astype(BF16))
                dki = _dot(da_t, qd)
                dke = _dot(vh, dst_b)
                ddec = jnp.sum(spt * dst, axis=0, keepdims=True)
                ds_ref[h] = dst * dec[:, ks] + _dot_tn(do, qd)
                dq_ref[rows, ks] = ((dqd * eG[:, ks]) * scale).astype(BF16)
                dk_ref[rows, ks] = (dki * eN[:, ks] + dke * eE[:, ks]).astype(BF16)
                dke_ke = dke * ke_f
                dG = dqd * qd_f - dki * ki_f - dke_ke
                dGl = jnp.sum(dke_ke, axis=0, keepdims=True) + ddec * dec[:, ks]
                dG_parts.append(dG + jnp.where(last, dGl, 0.0))
            dgnw_ref[...] += dgnw_acc
            dG_all = jnp.concatenate(dG_parts, axis=1)
            dlogg = jnp.where(valid, _tri_sum(upper.astype(BF16), dG_all), 0.0)
            dg_raw = (dlogg * (1.0 / TAU)) * (1.0 / (1.0 + jnp.exp(g_raw)))
            dgb_ref[...] += jnp.sum(dg_raw, axis=0, keepdims=True)
            dg_b = dg_raw.astype(BF16)
            dgw2_ref[...] += _dot_tn(glr_v.astype(BF16), dg_b)
            dglr_ref[rows, :] = _dot_nt(dg_b, gw2_b).astype(BF16)

    def back(s):
        return NCH // GSUB - 1 - s

    rk = pl.BlockSpec((GSUB * CH, KW), lambda s: (back(s), 0))
    rv_ = pl.BlockSpec((GSUB * CH, GW), lambda s: (back(s), 0))
    rg = pl.BlockSpec((GSUB * CH, 128), lambda s: (back(s), 0))

    def full(shape):
        return pl.BlockSpec(shape, lambda s: tuple(0 for _ in shape))

    return pl.pallas_call(
        body, name="gla_bwd", grid=(NCH // GSUB,),
        in_specs=[rv_, rv_] + _p_specs(back) + [full((128, KW)), full((1, KW)), full((1, DV)),
                  pl.BlockSpec((GSUB, HEADS, DV, DK), lambda s: (back(s), 0, 0, 0)), ANY],
        out_specs=[rk, rk, rv_, rv_, rg, full((128, KW)), full((1, KW)), full((1, DV))],
        out_shape=[_sds((TP, KW), BF16), _sds((TP, KW), BF16), _sds((TP, GW), BF16), _sds((TP, GW), BF16),
                   _sds((TP, 128), BF16), _sds((128, KW), F32), _sds((1, KW), F32), _sds((1, DV), F32)],
        scratch_shapes=[pltpu.VMEM((HEADS, DV, DK), F32)],
        compiler_params=_cp(("arbitrary",)),
    )(dog, o, *([P] * 8), gw2, gb, gnw, sp, _dep(dep))


POOL_TR = 128
HALO = 16


def _pool_counts(base, nrows):
    rows = base + lax.broadcasted_iota(jnp.int32, (nrows, 1), 0)
    valid = (rows >= ROW_LO) & (rows < ROW_HI)
    t1 = (rows - ROW_LO + 1).astype(F32)
    cnts = [jnp.clip(t1, 1.0, float(w)) for w in WINDOWS]
    return valid, cnts


def _pool_fwd(P, pw, ps, dep=None):
    def body(cur_ref, prev_ref, pw_ref, ps_ref, dep_ref, y_ref, op_ref):
        i = pl.program_id(0)
        cur = cur_ref[0]
        full = jnp.concatenate([prev_ref[0], cur], axis=0)
        s2 = full + pltpu.roll(full, 1, 0)
        s4 = s2 + pltpu.roll(s2, 2, 0)
        s8 = s4 + pltpu.roll(s4, 4, 0)
        s16 = s8 + pltpu.roll(s8, 8, 0)
        valid, cnts = _pool_counts(i * POOL_TR, POOL_TR)
        for g, s in enumerate((s2, s4, s8, s16)):
            cs = slice(g * GC, (g + 1) * GC)
            y = s[HALO:, cs] / cnts[g] - cur[:, cs]
            yb = jnp.where(valid, y, 0.0).astype(BF16)
            y_ref[:, cs] = yb
            op_ref[:, cs] = (_dot(yb, pw_ref[g].astype(BF16)) * ps_ref[:, cs]).astype(BF16)

    row = pl.BlockSpec((POOL_TR, PW), lambda i: (i, 0))
    per = POOL_TR // HALO
    return pl.pallas_call(
        body, name="pool_fwd", grid=(TP // POOL_TR,),
        in_specs=[pl.BlockSpec((1, POOL_TR, PW), lambda i: (3, i, 0)),
                  pl.BlockSpec((1, HALO, PW), lambda i: (3, jnp.maximum(i * per - 1, 0), 0)),
                  pl.BlockSpec((4, GC, GC), lambda i: (0, 0, 0)), pl.BlockSpec((1, PW), lambda i: (0, 0)), ANY],
        out_specs=[row, row],
        out_shape=[_sds((TP, PW), BF16), _sds((TP, PW), BF16)],
        compiler_params=_cp(("arbitrary",)),
    )(P, P, pw, ps, _dep(dep))


def _pool_bwd(dop, y, pw, ps, dep=None):
    nblk = TP // HALO

    def body(cur_ref, nxt_ref, y_ref, pw_ref, ps_ref, dep_ref, dpu_ref, dpw_ref, dps_ref):
        i = pl.program_id(0)

        @pl.when(i == 0)
        def _():
            dpw_ref[...] = jnp.zeros_like(dpw_ref)
            dps_ref[...] = jnp.zeros_like(dps_ref)

        n_all = POOL_TR + HALO
        dcur = cur_ref[...]
        dall = jnp.concatenate([dcur, nxt_ref[...]], axis=0)
        valid, cnts = _pool_counts(i * POOL_TR, n_all)
        for g in range(4):
            cs = slice(g * GC, (g + 1) * GC)
            pwb = pw_ref[g].astype(BF16)
            yb = y_ref[:, cs]
            dyw = (dall[:, cs] * ps_ref[:, cs]).astype(BF16)
            dps_ref[:, cs] += jnp.sum(dcur[:, cs] * _dot(yb, pwb), axis=0, keepdims=True)
            dpw_ref[g] += _dot_tn(yb, dyw[0:POOL_TR, :])
            dyv = jnp.where(valid, _dot_nt(dyw, pwb), 0.0)
            e = dyv / cnts[g]
            w = WINDOWS[g]
            sh = 1
            while sh < w:
                e = e + pltpu.roll(e, n_all - sh, 0)
                sh *= 2
            dpu_ref[:, cs] = (e[0:POOL_TR, :] - dyv[0:POOL_TR, :]).astype(BF16)

    row = pl.BlockSpec((POOL_TR, PW), lambda i: (i, 0))
    per = POOL_TR // HALO
    return pl.pallas_call(
        body, name="pool_bwd", grid=(TP // POOL_TR,),
        in_specs=[pl.BlockSpec((POOL_TR, PW), lambda i: (i, 1)),
                  pl.BlockSpec((HALO, PW), lambda i: (jnp.minimum(i * per + per, nblk - 1), 1)),
                  row, pl.BlockSpec((4, GC, GC), lambda i: (0, 0, 0)), pl.BlockSpec((1, PW), lambda i: (0, 0)), ANY],
        out_specs=[row, pl.BlockSpec((4, GC, GC), lambda i: (0, 0, 0)), pl.BlockSpec((1, PW), lambda i: (0, 0))],
        out_shape=[_sds((TP, PW), BF16), _sds((4, GC, GC), F32), _sds((1, PW), F32)],
        compiler_params=_cp(("arbitrary",)),
    )(dop, dop, y, pw, ps, _dep(dep))


def _place():
    x, y, c = lax.axis_index("x"), lax.axis_index("y"), lax.axis_index("c")
    chips = [(1 - x, y), (x, 1 - y), (1 - x, 1 - y)]
    return x, y, c, chips


HBM = pl.BlockSpec(memory_space=pltpu.HBM)
SEM = pl.BlockSpec(memory_space=pltpu.SEMAPHORE)
EFFECT = pltpu.SideEffectType.DATAFLOW_SIDE_EFFECTING


def _cast_into(w, place, cols_out, name, dep=None):
    rows, cols = w.shape
    tr = 512

    def body(p_ref, w_ref, dep_ref, o_ref):
        if cols_out != cols:
            o_ref[0] = jnp.zeros((tr, cols_out), BF16)
            o_ref[0, :, 0:cols] = w_ref[...].astype(BF16)
        else:
            o_ref[0] = w_ref[...].astype(BF16)

    grid_spec = pltpu.PrefetchScalarGridSpec(
        num_scalar_prefetch=1, grid=(rows // tr,),
        in_specs=[pl.BlockSpec((tr, cols), lambda i, p: (i, 0)), ANY],
        out_specs=pl.BlockSpec((1, tr, cols_out), lambda i, p: (p[0], i, 0)))
    return pl.pallas_call(
        body, name=name, grid_spec=grid_spec,
        out_shape=_sds((N_CHIP, rows, cols_out), BF16),
        compiler_params=_cp(("arbitrary",)),
    )(place, w, _dep(dep))


def _cast_win(w, place, dep=None):
    rows, cols = w.shape
    tr = 512

    def body(p_ref, w_ref, dep_ref, o_ref, t_ref):
        t_ref[...] = jnp.zeros_like(t_ref)
        t_ref[:, 0:cols] = w_ref[...]
        t = t_ref[...]
        lane = lax.broadcasted_iota(jnp.int32, (tr, PAD_IN), 1)
        for kk in range(N_CHIP):
            @pl.when(p_ref[0] == kk)
            def _(kk=kk):
                if kk == 0:
                    placed = t
                elif kk < 3:
                    placed = pltpu.roll(t, 4 * kk, 1)
                else:
                    pool = pltpu.roll(t, PAD_IN - 4, 1)
                    gate = pltpu.roll(t, GW + 12, 1)
                    placed = jnp.where(lane < GW, pool, jnp.where((lane >= GW + 12) & (lane < GW + 16), gate, 0.0))
                o_ref[0] = placed.astype(BF16)

    grid_spec = pltpu.PrefetchScalarGridSpec(
        num_scalar_prefetch=1, grid=(rows // tr,),
        in_specs=[pl.BlockSpec((tr, cols), lambda i, p: (i, 0)), ANY],
        out_specs=pl.BlockSpec((1, tr, PAD_IN), lambda i, p: (p[0], i, 0)),
        scratch_shapes=[pltpu.VMEM((tr, PAD_IN), F32)])
    return pl.pallas_call(
        body, name="cast_win", grid_spec=grid_spec,
        out_shape=_sds((N_CHIP, rows, PAD_IN), BF16),
        compiler_params=_cp(("arbitrary",)),
    )(place, w, _dep(dep))


def _half_rows(ref, k, which):
    h = ref.shape[1] // 2
    return ref.at[k, pl.ds(pl.multiple_of(which * h, 8), h), :]


def _sent_rows(ref, k, which, whole):
    return ref.at[k] if whole else _half_rows(ref, k, which)


def _gather_start(ws, name, whole=None):
    n = len(ws)
    whole = whole or [False] * n

    def body(*refs):
        ins = refs[:n]
        ssems = refs[n:2 * n]
        rsems = refs[2 * n:3 * n]
        token = refs[4 * n]
        x, y, c, chips = _place()
        me = 2 * x + y
        for w in range(n):
            blk = _sent_rows(ins[w], me, c, whole[w])
            for j, chip in enumerate(chips):
                pltpu.make_async_remote_copy(src_ref=blk, dst_ref=blk, send_sem=ssems[w].at[j], recv_sem=rsems[w].at[j],
                                             device_id=(*chip, c), device_id_type=MESH).start()
        token[...] = jnp.zeros_like(token)

    sem3 = pltpu.SemaphoreType.DMA((3,))
    outs = pl.pallas_call(
        body, name=name,
        out_shape=tuple([sem3] * (2 * n) + [pltpu.HBM(w.shape, w.dtype) for w in ws] + [_sds((8, 128), F32)]),
        in_specs=(HBM,) * n, out_specs=(SEM,) * (2 * n) + (HBM,) * n + (VMEM_FULL,),
        input_output_aliases={w: 2 * n + w for w in range(n)},
        compiler_params=pltpu.CompilerParams(has_side_effects=EFFECT),
    )(*[pltpu.with_memory_space_constraint(w, pltpu.HBM) for w in ws])
    return outs[:n], outs[n:2 * n], outs[2 * n:3 * n], outs[3 * n]


def _gather_wait(w, ssem, rsem, after, name, whole=False):
    def body(w_ref, ssem_ref, rsem_ref, after_ref, out_ref):
        x, y, c, chips = _place()
        me = 2 * x + y
        mine = _sent_rows(w_ref, me, c, whole)
        for j, (cx, cy) in enumerate(chips):
            cp = pltpu.make_async_remote_copy(src_ref=mine, dst_ref=_sent_rows(w_ref, 2 * cx + cy, c, whole),
                                              send_sem=ssem_ref.at[j], recv_sem=rsem_ref.at[j],
                                              device_id=(cx, cy, c), device_id_type=MESH)
            cp.wait_send()
            cp.wait_recv()

    return pl.pallas_call(
        body, name=name, out_shape=pltpu.HBM(w.shape, w.dtype),
        in_specs=(HBM, SEM, SEM, ANY), out_specs=HBM, input_output_aliases={0: 0},
        compiler_params=pltpu.CompilerParams(has_side_effects=EFFECT),
    )(w, ssem, rsem, after)


def _gather_copies(ref, kind, ssem, rsem):
    x, y, c, _ = _place()
    xn, yn, sib = (1 - x, y, c), (x, 1 - y, c), (x, y, 1 - c)
    kx, ky, kd = 2 * (1 - x) + y, 2 * x + (1 - y), 2 * (1 - x) + (1 - y)
    half = ref.shape[1] // 2
    quarter = half // 2

    def piece(k, q):
        return ref.at[k, pl.ds(pl.multiple_of(c * half + q * quarter, 8), quarter), :]

    if kind == "d":
        blk = _half_rows(ref, 2 * x + y, c)
        pairs = [(blk, xn), (blk, yn)]
    elif kind == "r":
        pairs = [(piece(ky, 1), xn), (piece(kx, 0), yn)]
    elif kind == "fx":
        pairs = [(_half_rows(ref, kx, c), sib), (_half_rows(ref, ky, c), sib)]
    else:
        pairs = [(_half_rows(ref, kd, c), sib)]
    return [pltpu.make_async_remote_copy(src_ref=blk, dst_ref=blk, send_sem=ssem.at[i], recv_sem=rsem.at[i],
                                         device_id=to, device_id_type=MESH) for i, (blk, to) in enumerate(pairs)]


def _gather_step(name, arrs, waits, starts, sems_in=(), after=()):
    n, nw, ns = len(arrs), len(waits), len(starts)
    after = [a for a in after if a is not None] or [_dep(None)]

    def body(*refs):
        a_in = refs[:n]
        s_in = refs[n:n + 2 * nw]
        outs = refs[n + 2 * nw + len(after):]
        s_out = outs[:2 * ns]
        for i, (ai, kind) in enumerate(waits):
            for cp in _gather_copies(a_in[ai], kind, s_in[2 * i], s_in[2 * i + 1]):
                cp.wait_send()
                cp.wait_recv()
        for i, (ai, kind) in enumerate(starts):
            for cp in _gather_copies(a_in[ai], kind, s_out[2 * i], s_out[2 * i + 1]):
                cp.start()
        if ns:
            token = outs[2 * ns + n]
            token[...] = jnp.zeros_like(token)

    sem2 = pltpu.SemaphoreType.DMA((2,))
    flat_in = [s for pair in sems_in for s in pair]
    arrs = [pltpu.with_memory_space_constraint(a, pltpu.HBM) for a in arrs]
    outs = pl.pallas_call(
        body, name=name,
        out_shape=tuple([sem2] * (2 * ns) + [pltpu.HBM(a.shape, a.dtype) for a in arrs]
                        + ([_sds((8, 128), F32)] if ns else [])),
        in_specs=(HBM,) * n + (SEM,) * (2 * nw) + (ANY,) * len(after),
        out_specs=(SEM,) * (2 * ns) + (HBM,) * n + ((VMEM_FULL,) if ns else ()),
        input_output_aliases={i: 2 * ns + i for i in range(n)},
        compiler_params=pltpu.CompilerParams(has_side_effects=EFFECT),
    )(*arrs, *flat_in, *after)
    sems = [(outs[2 * i], outs[2 * i + 1]) for i in range(ns)]
    return sems, list(outs[2 * ns:2 * ns + n]), (outs[2 * ns + n] if ns else None)


def _rs_start(sb, name, after=None):
    _, half, cols = sb.shape

    def body(sb_ref, land_ref, after_ref, ssem, rsem, sb_out, land_out, token):
        x, y, c, chips = _place()
        for j, (cx, cy) in enumerate(chips):
            pltpu.make_async_remote_copy(src_ref=sb_ref.at[2 * cx + cy], dst_ref=land_ref.at[j], send_sem=ssem.at[j],
                                         recv_sem=rsem.at[j], device_id=(cx, cy, c), device_id_type=MESH).start()
        token[...] = jnp.zeros_like(token)

    sem3 = pltpu.SemaphoreType.DMA((3,))
    land = lax.empty((3, half, cols), BF16)
    return pl.pallas_call(
        body, name=name,
        out_shape=(sem3, sem3, pltpu.HBM(sb.shape, sb.dtype), pltpu.HBM(land.shape, land.dtype), _sds((8, 128), F32)),
        in_specs=(HBM, HBM, ANY), out_specs=(SEM, SEM, HBM, HBM, VMEM_FULL), input_output_aliases={0: 2, 1: 3},
        compiler_params=pltpu.CompilerParams(has_side_effects=EFFECT),
    )(pltpu.with_memory_space_constraint(sb, pltpu.HBM), pltpu.with_memory_space_constraint(land, pltpu.HBM), _dep(after))


def _rs_wait(items, after, name):
    n = len(items)

    def body(*refs):
        x, y, c, chips = _place()
        for i in range(n):
            sb_ref, land_ref, ssem_ref, rsem_ref = refs[4 * i:4 * i + 4]
            for j, (cx, cy) in enumerate(chips):
                cp = pltpu.make_async_remote_copy(src_ref=sb_ref.at[2 * cx + cy], dst_ref=land_ref.at[j],
                                                  send_sem=ssem_ref.at[j], recv_sem=rsem_ref.at[j],
                                                  device_id=(cx, cy, c), device_id_type=MESH)
                cp.wait_send()
                cp.wait_recv()

    outs = pl.pallas_call(
        body, name=name,
        out_shape=tuple(pltpu.HBM(a.shape, a.dtype) for it in items for a in it[:2]),
        in_specs=(HBM, HBM, SEM, SEM) * n + (ANY,), out_specs=(HBM,) * (2 * n),
        input_output_aliases={4 * i + k: 2 * i + k for i in range(n) for k in range(2)},
        compiler_params=pltpu.CompilerParams(has_side_effects=EFFECT),
    )(*[a for it in items for a in it], after)
    return [outs[2 * i + 1] for i in range(n)]


def _pair_copy(g_ref, land_ref, ssem, rsem):
    x, y, c, _ = _place()
    h = g_ref.shape[1] // 2
    src = g_ref.at[:, pl.ds(pl.multiple_of((1 - c) * h, 8), h), :]
    return pltpu.make_async_remote_copy(src_ref=src, dst_ref=land_ref, send_sem=ssem.at[0], recv_sem=rsem.at[0],
                                        device_id=(x, y, 1 - c), device_id_type=MESH)


def _pair_start(g, name):
    def body(g_ref, land_ref, ssem, rsem, g_out, land_out, token):
        _pair_copy(g_ref, land_ref, ssem, rsem).start()
        token[...] = jnp.zeros_like(token)

    sem1 = pltpu.SemaphoreType.DMA((1,))
    land = lax.empty((N_CHIP, g.shape[1] // 2, g.shape[2]), F32)
    return pl.pallas_call(
        body, name=name,
        out_shape=(sem1, sem1, pltpu.HBM(g.shape, g.dtype), pltpu.HBM(land.shape, land.dtype), _sds((8, 128), F32)),
        in_specs=(HBM, HBM), out_specs=(SEM, SEM, HBM, HBM, VMEM_FULL), input_output_aliases={0: 2, 1: 3},
        compiler_params=pltpu.CompilerParams(has_side_effects=EFFECT),
    )(pltpu.with_memory_space_constraint(g, pltpu.HBM), pltpu.with_memory_space_constraint(land, pltpu.HBM))


def _pair_wait(g, land, ssem, rsem, after, name):
    def body(g_ref, land_ref, ssem_ref, rsem_ref, after_ref, g_out, land_out):
        cp = _pair_copy(g_ref, land_ref, ssem_ref, rsem_ref)
        cp.wait_send()
        cp.wait_recv()

    return pl.pallas_call(
        body, name=name,
        out_shape=(pltpu.HBM(g.shape, g.dtype), pltpu.HBM(land.shape, land.dtype)),
        in_specs=(HBM, HBM, SEM, SEM, ANY), out_specs=(HBM, HBM), input_output_aliases={0: 0, 1: 1},
        compiler_params=pltpu.CompilerParams(has_side_effects=EFFECT),
    )(g, land, ssem, rsem, after)


def _pair_sum(g, rcv, place, name):
    _, rows, cols = g.shape
    half = rows // 2
    tr = min(512, half)
    nt = half // tr

    def body(p_ref, g_ref, r_ref, sb_ref, sf_ref):
        s = pl.program_id(1)
        tot = g_ref[0] + r_ref[0]
        sb_ref[0] = tot.astype(BF16)

        @pl.when(s == p_ref[0])
        def _():
            sf_ref[...] = tot

    grid_spec = pltpu.PrefetchScalarGridSpec(
        num_scalar_prefetch=1, grid=(nt, N_CHIP),
        in_specs=[pl.BlockSpec((1, tr, cols), lambda t, s, p: (s, p[1] * nt + t, 0)),
                  pl.BlockSpec((1, tr, cols), lambda t, s, p: (s, t, 0))],
        out_specs=[pl.BlockSpec((1, tr, cols), lambda t, s, p: (s, t, 0)),
                   pl.BlockSpec((tr, cols), lambda t, s, p: (t, 0))])
    return pl.pallas_call(
        body, name=name, grid_spec=grid_spec,
        out_shape=[_sds((N_CHIP, half, cols), BF16), _sds((half, cols), F32)],
        compiler_params=_cp(("arbitrary", "arbitrary")),
    )(place, g, rcv)


def _final_sum(sf, rb, place, name):
    half, cols = sf.shape
    tr = min(512, half)
    nt = half // tr

    def body(p_ref, sf_ref, r_ref, out_ref):
        acc = sf_ref[...]
        for j in range(3):
            acc = acc + r_ref[j].astype(F32)
        out_ref[...] = acc

    grid_spec = pltpu.PrefetchScalarGridSpec(
        num_scalar_prefetch=1, grid=(nt,),
        in_specs=[pl.BlockSpec((tr, cols), lambda t, p: (t, 0)), pl.BlockSpec((3, tr, cols), lambda t, p: (0, t, 0))],
        out_specs=pl.BlockSpec((tr, cols), lambda t, p: (p[1] * nt + t, 0)))
    return pl.pallas_call(
        body, name=name, grid_spec=grid_spec,
        out_shape=_sds((2 * half, cols), F32),
        compiler_params=_cp(("arbitrary",)),
    )(place, sf, rb)


def _half_copy(f_ref, which, ssem, rsem):
    x, y, c, _ = _place()
    h = f_ref.shape[0] // 2
    rows = f_ref.at[pl.ds(pl.multiple_of(which * h, 8), h), :]
    return pltpu.make_async_remote_copy(src_ref=rows, dst_ref=rows, send_sem=ssem.at[0], recv_sem=rsem.at[0],
                                        device_id=(x, y, 1 - c), device_id_type=MESH)


def _half_start(fulls, name, after=None):
    n = len(fulls)

    def body(*refs):
        for i in range(n):
            _half_copy(refs[i], lax.axis_index("c"), refs[n + 1 + 2 * i], refs[n + 2 + 2 * i]).start()
        token = refs[4 * n + 1]
        token[...] = jnp.zeros_like(token)

    sem1 = pltpu.SemaphoreType.DMA((1,))
    outs = pl.pallas_call(
        body, name=name,
        out_shape=tuple([sem1] * (2 * n) + [pltpu.HBM(f.shape, f.dtype) for f in fulls] + [_sds((8, 128), F32)]),
        in_specs=(HBM,) * n + (ANY,), out_specs=(SEM,) * (2 * n) + (HBM,) * n + (VMEM_FULL,),
        input_output_aliases={i: 2 * n + i for i in range(n)},
        compiler_params=pltpu.CompilerParams(has_side_effects=EFFECT),
    )(*[pltpu.with_memory_space_constraint(f, pltpu.HBM) for f in fulls], _dep(after))
    return [(outs[2 * i], outs[2 * i + 1], outs[2 * n + i]) for i in range(n)], outs[3 * n]


def _half_wait(items, after, name):
    n = len(items)

    def body(*refs):
        c = lax.axis_index("c")
        for i in range(n):
            ssem_ref, rsem_ref, f_ref = refs[3 * i:3 * i + 3]
            _half_copy(f_ref, c, ssem_ref, rsem_ref).wait_send()
            _half_copy(f_ref, 1 - c, ssem_ref, rsem_ref).wait_recv()

    return pl.pallas_call(
        body, name=name, out_shape=tuple(pltpu.HBM(it[2].shape, it[2].dtype) for it in items),
        in_specs=(SEM, SEM, HBM) * n + (ANY,) * len(after), out_specs=(HBM,) * n,
        input_output_aliases={3 * i + 2: i for i in range(n)},
        compiler_params=pltpu.CompilerParams(has_side_effects=EFFECT),
    )(*[a for it in items for a in it], *after)


def _small_copies(src_ref, land_ref, ssem, rsem, first):
    x, y, c, chips = _place()
    if first:
        return [pltpu.make_async_remote_copy(src_ref=src_ref, dst_ref=land_ref, send_sem=ssem.at[0], recv_sem=rsem.at[0],
                                             device_id=(x, y, 1 - c), device_id_type=MESH)]
    return [pltpu.make_async_remote_copy(src_ref=src_ref, dst_ref=land_ref.at[j], send_sem=ssem.at[j], recv_sem=rsem.at[j],
                                         device_id=(*chip, c), device_id_type=MESH) for j, chip in enumerate(chips)]


def _small_start(src, first, name, after=None):
    n = 1 if first else 3

    def body(src_ref, land_ref, after_ref, ssem, rsem, src_out, land_out, token):
        for cp in _small_copies(src_ref, land_ref, ssem, rsem, first):
            cp.start()
        token[...] = jnp.zeros_like(token)

    sems = pltpu.SemaphoreType.DMA((n,))
    land = lax.empty(src.shape if first else (3,) + src.shape, F32)
    return pl.pallas_call(
        body, name=name,
        out_shape=(sems, sems, pltpu.HBM(src.shape, F32), pltpu.HBM(land.shape, F32), _sds((8, 128), F32)),
        in_specs=(HBM, HBM, ANY), out_specs=(SEM, SEM, HBM, HBM, VMEM_FULL), input_output_aliases={0: 2, 1: 3},
        compiler_params=pltpu.CompilerParams(has_side_effects=EFFECT),
    )(pltpu.with_memory_space_constraint(src, pltpu.HBM), pltpu.with_memory_space_constraint(land, pltpu.HBM), _dep(after))


def _small_wait(src, land, ssem, rsem, first, after, name):
    def body(src_ref, land_ref, ssem_ref, rsem_ref, after_ref, src_out, land_out):
        for cp in _small_copies(src_ref, land_ref, ssem_ref, rsem_ref, first):
            cp.wait_send()
            cp.wait_recv()

    return pl.pallas_call(
        body, name=name,
        out_shape=(pltpu.HBM(src.shape, F32), pltpu.HBM(land.shape, F32)),
        in_specs=(HBM, HBM, SEM, SEM, ANY), out_specs=(HBM, HBM), input_output_aliases={0: 0, 1: 1},
        compiler_params=pltpu.CompilerParams(has_side_effects=EFFECT),
    )(src, land, ssem, rsem, after)


def _small_pair_sum(vec, got):
    def body(v_ref, g_ref, o_ref):
        o_ref[...] = v_ref[...] + g_ref[...]

    return pl.pallas_call(body, name="small_pair_sum", in_specs=[VMEM_FULL] * 2, out_specs=VMEM_FULL,
                          out_shape=_sds(vec.shape, F32), compiler_params=_cp())(vec, got)


def _small_chip_sum(pair, got, place):
    def body(p_ref, pair_ref, got_ref, o_ref):
        acc = None
        for kk in range(N_CHIP):
            d = jnp.bitwise_xor(p_ref[0], kk)
            t = jnp.where(d == 0, pair_ref[...], jnp.where(d == 2, got_ref[0], jnp.where(d == 1, got_ref[1], got_ref[2])))
            acc = t if acc is None else acc + t
        o_ref[...] = acc

    grid_spec = pltpu.PrefetchScalarGridSpec(
        num_scalar_prefetch=1, grid=(1,),
        in_specs=[pl.BlockSpec(pair.shape, lambda i, p: (0, 0)), pl.BlockSpec(got.shape, lambda i, p: (0, 0, 0))],
        out_specs=pl.BlockSpec(pair.shape, lambda i, p: (0, 0)))
    return pl.pallas_call(body, name="small_chip_sum", grid_spec=grid_spec, out_shape=_sds(pair.shape, F32),
                          compiler_params=_cp(("arbitrary",)))(place, pair, got)


def _adam_math(w, g, m, v):
    m = B1 * m + (1.0 - B1) * g
    v = B2 * v + (1.0 - B2) * (g * g)
    m_hat = m / (1.0 - B1 ** STEP)
    v_hat = v / (1.0 - B2 ** STEP)
    delta = -LR * (m_hat / (jnp.sqrt(v_hat) + AEPS) + WD * w)
    return delta, m, v


def _adam_big(w, g, m, v, name, dep=None):
    rows, cols = w.shape
    tr = 256

    def body(w_ref, g_ref, m_ref, v_ref, dep_ref, go_ref, d_ref, nm_ref, nv_ref):
        g = g_ref[...]
        d, nm, nv = _adam_math(w_ref[...], g, m_ref[...], v_ref[...])
        go_ref[...] = g
        d_ref[...] = d
        nm_ref[...] = nm
        nv_ref[...] = nv

    blk = pl.BlockSpec((tr, cols), lambda i: (i, 0))
    return pl.pallas_call(
        body, name=name, grid=(rows // tr,),
        in_specs=[blk] * 4 + [ANY], out_specs=[blk] * 4, out_shape=[_sds((rows, cols), F32)] * 4,
        compiler_params=_cp(("arbitrary",)),
    )(w, g, m, v, _dep(dep))


def _adam_small(ws, gs, ms, vs, dep=None):
    n = len(ws)

    def body(*refs):
        for i in range(n):
            d, nm, nv = _adam_math(refs[i][...], refs[n + i][...], refs[2 * n + i][...], refs[3 * n + i][...])
            refs[4 * n + 1 + i][...] = d
            refs[5 * n + 1 + i][...] = nm
            refs[6 * n + 1 + i][...] = nv

    shapes = [_sds(w.shape, F32) for w in ws]
    outs = pl.pallas_call(
        body, name="adam_small",
        in_specs=[VMEM_FULL] * (4 * n) + [ANY], out_specs=[VMEM_FULL] * (3 * n), out_shape=shapes * 3,
        compiler_params=_cp(),
    )(*ws, *gs, *ms, *vs, _dep(dep))
    return outs[:n], outs[n:2 * n], outs[2 * n:]


def _pad_rows8(a):
    flat = a.reshape(-1, 128)
    pad = (-flat.shape[0]) % 8
    if pad:
        flat = jnp.concatenate([flat, jnp.zeros((pad, 128), F32)], axis=0)
    return flat


def kernel(x, meta_tokens, norm1_w, w_in, gate_w2, gate_b, gla_norm_w, pool_w, pool_scale, w_out, norm2_w, mlp_w1, mlp_w2, final_norm_w, loss_target, m_meta_tokens, m_norm1_w, m_w_in, m_gate_w2, m_gate_b, m_gla_norm_w, m_pool_w, m_pool_scale, m_w_out, m_norm2_w, m_mlp_w1, m_mlp_w2, m_final_norm_w, v_meta_tokens, v_norm1_w, v_w_in, v_gate_w2, v_gate_b, v_gla_norm_w, v_pool_w, v_pool_scale, v_w_out, v_norm2_w, v_mlp_w1, v_mlp_w2, v_final_norm_w):
    cx, cy, cc = lax.axis_index("x"), lax.axis_index("y"), lax.axis_index("c")
    me = (2 * cx + cy).astype(jnp.int32)

    place = jnp.stack([me, cc.astype(jnp.int32)])
    fw = final_norm_w.reshape(1, D)

    mine = jnp.concatenate([meta_tokens.reshape(64, 128), gate_w2[0], pool_w[0].reshape(512, 128)], axis=0)
    small = lax.dynamic_update_slice(jnp.zeros((N_CHIP, 592, 128), F32), mine[None], (me, 0, 0))
    (s_sm,), (r_sm,), (f_sm,), tok = _gather_start([small], "gather_start_small", [True])
    (sem_win_d,), (win,), tok = _gather_step("gather_start_win", [_cast_win(w_in[0], place, tok)], [], [(0, "d")])
    wout, w1, w2 = (_cast_into(w_out[0], place, D, "cast_wout", tok), _cast_into(mlp_w1[0], place, D, "cast_w1", tok),
                    _cast_into(mlp_w2[0], place, D, "cast_w2", tok))
    small = _gather_wait(f_sm, s_sm, r_sm, w2, "gather_wait_small", True)
    metaF = jnp.concatenate([small[k, 0:64].reshape(N_META, 512) for k in range(N_CHIP)], axis=1)
    gw2F = jnp.concatenate([small[k, 64:80] for k in range(N_CHIP)], axis=1)
    pwF = jnp.concatenate([small[k, 80:592].reshape(4, 64, GC) for k in range(N_CHIP)], axis=1)

    fly = {"win": win, "wout": wout, "w1": w1, "w2": w2}
    sems = {"win_d": sem_win_d}

    def step(name, names, waits, starts, after):
        at = {nm: i for i, nm in enumerate(names)}
        new, arrs, tok = _gather_step(name, [fly[nm] for nm in names], [(at[nm], k) for nm, k in waits],
                                      [(at[nm], k) for nm, k in starts], [sems[nm + "_" + k] for nm, k in waits], after)
        fly.update(zip(names, arrs))
        sems.update({nm + "_" + k: s for (nm, k), s in zip(starts, new)})
        return tok

    def relay_first():
        return step("gather_relay_win", ["win", "wout", "w1"], [("win", "d")],
                    [("win", "r"), ("win", "fx"), ("wout", "d"), ("w1", "d")], [v_w_in[0]])

    def get_win(after):
        tok = step("gather_land_win", ["win"], [("win", "r")], [("win", "fd")], [after])
        step("gather_wait_win", ["win"], [("win", "fx"), ("win", "fd")], [], [tok])
        return fly["win"]

    def relay_mid(after):
        return step("gather_relay_mid", ["wout"], [("wout", "d")], [("wout", "r"), ("wout", "fx")], [after, m_w_in[0]])

    def land_wout(after):
        return step("gather_land_wout", ["wout", "w1", "w2"], [("wout", "r"), ("w1", "d")],
                    [("wout", "fd"), ("w1", "r"), ("w1", "fx"), ("w2", "d")], [after])

    def get_wout(after):
        step("gather_wait_wout", ["wout"], [("wout", "fx"), ("wout", "fd")], [], [after])
        tok = step("gather_land_w1", ["w1"], [("w1", "r")], [("w1", "fd")], [fly["wout"]])
        return fly["wout"].reshape(D, D), tok

    def get_w1(after):
        step("gather_wait_w1", ["w1"], [("w1", "fx"), ("w1", "fd")], [], [after])
        return fly["w1"]

    def relay_last(after):
        return step("gather_relay_w2", ["w2"], [("w2", "d")], [("w2", "r"), ("w2", "fx")], [after])

    def get_w2(after):
        tok = step("gather_land_w2", ["w2"], [("w2", "r")], [("w2", "fd")], [after])
        step("gather_wait_w2", ["w2"], [("w2", "fx"), ("w2", "fd")], [], [tok])
        return fly["w2"].reshape(DFF, D)

    pairs, pending = {}, {}

    halves = {}

    def reduce_(names, after, tag):
        items = [(pending[nm][3], pending[nm][4], pending[nm][1], pending[nm][2]) for nm in names]
        landed = _rs_wait(items, after, "rs_wait_" + tag)
        fulls = [_final_sum(pending[nm][0], rb, place, "final_sum_" + nm) for nm, rb in zip(names, landed)]
        sent, token = _half_start(fulls, "half_start_" + tag)
        halves.update(zip(names, sent))
        return token

    def grad_start(nm, g):
        ssem, rsem, g_thru, land, token = _pair_start(g, "pair_start_" + nm)
        pairs[nm] = (ssem, rsem, g_thru, land)
        if nm == "win":
            token = reduce_(["w2", "w1", "wout"], token, "mlp_wout")
        return token

    def grad_finish(nm, after):
        ssem, rsem, g_thru, land = pairs[nm]
        g, rcv = _pair_wait(g_thru, land, ssem, rsem, after, "pair_wait_" + nm)
        sb, sf = _pair_sum(g, rcv, place, "pair_sum_" + nm)
        ssem, rsem, sb_thru, land, token = _rs_start(sb, "rs_start_" + nm)
        pending[nm] = (sf, ssem, rsem, sb_thru, land)
        return token

    (grad_x, loss8, d_n1w, d_gb, d_gnw, d_ps, d_n2w, d_fw, d_meta, d_gw2, d_pw) = _local_step(
        x[0], loss_target[0], dict(relay_first=relay_first, win=get_win, relay_mid=relay_mid, land_wout=land_wout,
                                   wout=get_wout, w1=get_w1, relay_last=relay_last, w2=get_w2),
        metaF, gw2F, pwF, norm1_w, gate_b, gla_norm_w, pool_scale, norm2_w, fw, grad_start, grad_finish)
    return _reduce_and_update(
        me, place, pending, halves, reduce_, grad_x, loss8, d_n1w, d_gb, d_gnw, d_ps, d_n2w, d_fw, d_meta, d_gw2, d_pw,
        meta_tokens, norm1_w, w_in, gate_w2, gate_b, gla_norm_w, pool_w, pool_scale, w_out, norm2_w, mlp_w1, mlp_w2, fw,
        m_meta_tokens, m_norm1_w, m_w_in, m_gate_w2, m_gate_b, m_gla_norm_w, m_pool_w, m_pool_scale, m_w_out, m_norm2_w,
        m_mlp_w1, m_mlp_w2, m_final_norm_w, v_meta_tokens, v_norm1_w, v_w_in, v_gate_w2, v_gate_b, v_gla_norm_w, v_pool_w,
        v_pool_scale, v_w_out, v_norm2_w, v_mlp_w1, v_mlp_w2, v_final_norm_w)


def _local_step(x, target, gather, metaF, gw2F, pwF, norm1_w, gate_b, gla_norm_w, pool_scale, norm2_w, fw, grad_start,
                grad_finish):
    h0, u = _embed_norm(x, metaF, norm1_w, gather["relay_first"]())
    Win = gather["win"](u)
    P = _in_proj(u, Win)
    gw2p = jnp.pad(gw2F, ((0, 128 - RANK), (0, 0)))
    yb, op = _pool_fwd(P, pwF, pool_scale, gather["relay_mid"](P))
    o, og, sp = _gla_fwd(P, gw2p, gate_b, gla_norm_w, gather["land_wout"](op))
    Wout, tok = gather["wout"](og)
    h1 = _out_proj(og, op, Wout, h0, tok)
    n2 = _norm_rows(h1, norm2_w, "norm2")
    W1 = gather["w1"](n2)
    zr, a, tok = _mlp_up(n2, W1, 0)
    zr, a, _ = _mlp_up(n2, W1, 1, (zr, a), gather["relay_last"](tok))
    W2 = gather["w2"](a)
    h2 = _mlp_down(a, W2, h1)

    dh2, dh2b, d_fw, loss8 = _loss_head(h2, target, fw)
    tok = grad_start("w2", _grad_w2(a, dh2b).reshape(N_CHIP, D, D))
    dz = _mlp_dz(dh2b, W2, zr, tok)
    tok = grad_finish("w2", dz)
    tok = grad_start("w1", _grad_w1(n2, dz, tok))
    dn2 = _mlp_dn(dz, W1, tok)
    tok = grad_finish("w1", dn2)
    dh1, dh1b, d_n2w = _norm_bwd(dn2, h1, dh2, norm2_w, "norm2_bwd", tok)
    dmixed = _mixed_grad(dh1b, Wout)
    tok = grad_start("wout", _grad_wout(og, op, dh1b))
    dpu, d_pw, d_ps = _pool_bwd(dmixed, yb, pwF, pool_scale, tok)
    dq, dk, dv, dr, dglr, d_gw2p, d_gb, d_gnw = _gla_bwd(dmixed, o, P, gw2p, gate_b, gla_norm_w, sp, tok)
    d_gw2 = d_gw2p[0:RANK]
    tok = grad_finish("wout", dq)
    tok = grad_start("win", _grad_win(u, dq, dk, dv, dr, dglr, dpu, tok))
    du = _in_grad(dq, dk, dv, dr, dglr, dpu, Win, tok)
    tok = grad_finish("win", du)
    grad_x, d_meta, d_n1w = _input_grad(du, h0, dh1, norm1_w, tok)
    return grad_x, loss8, d_n1w, d_gb, d_gnw, d_ps, d_n2w, d_fw, d_meta, d_gw2, d_pw


def _reduce_and_update(me, place, pending, halves, reduce_, grad_x, loss8, d_n1w, d_gb, d_gnw, d_ps, d_n2w, d_fw, d_meta, d_gw2,
                       d_pw,
                       meta_tokens, norm1_w, w_in, gate_w2, gate_b, gla_norm_w, pool_w, pool_scale, w_out, norm2_w,
                       mlp_w1, mlp_w2, fw, m_meta_tokens, m_norm1_w, m_w_in, m_gate_w2, m_gate_b, m_gla_norm_w, m_pool_w,
                       m_pool_scale, m_w_out, m_norm2_w, m_mlp_w1, m_mlp_w2, m_final_norm_w, v_meta_tokens, v_norm1_w, v_w_in,
                       v_gate_w2, v_gate_b, v_gla_norm_w, v_pool_w, v_pool_scale, v_w_out, v_norm2_w, v_mlp_w1, v_mlp_w2,
                       v_final_norm_w):
    parts = [loss8, d_n1w, d_gb, d_gnw, d_ps, d_n2w, d_fw, d_meta, d_gw2, d_pw]
    packed = [_pad_rows8(p) for p in parts]
    sizes = [p.shape[0] for p in packed]
    vec = jnp.concatenate(packed, axis=0)

    big = {}
    params = {"w2": (mlp_w2[0], m_mlp_w2[0], v_mlp_w2[0]), "w1": (mlp_w1[0], m_mlp_w1[0], v_mlp_w1[0]),
              "wout": (w_out[0], m_w_out[0], v_w_out[0]), "win": (w_in[0], m_w_in[0], v_w_in[0])}

    def update(names, after, tag):
        fulls = _half_wait([halves[nm] for nm in names], [after], "half_wait_" + tag)
        tok = None
        for nm, full in zip(names, fulls):
            w, m, v = params[nm]
            big[nm] = _adam_big(w, full, m, v, "adam_" + nm, tok)
            tok = big[nm][3]
        return tok

    s1, r1, vec, land1, tok = _small_start(vec, True, "small_start_pair")
    tok = update(["w2"], tok, "w2")
    vec, got = _small_wait(vec, land1, s1, r1, True, tok, "small_wait_pair")
    pair = _small_pair_sum(vec, got)
    s2, r2, pair, land2, tok = _small_start(pair, False, "small_start_chips")
    tok = reduce_(["win"], tok, "win")
    tok = update(["w1"], tok, "w1")
    tok = update(["win"], tok, "win")
    pair, got = _small_wait(pair, land2, s2, r2, False, tok, "small_wait_chips")
    red = _small_chip_sum(pair, got, place)
    after = update(["wout"], red, "wout")
    offs = [0]
    for s in sizes:
        offs.append(offs[-1] + s)

    def take(i, shape):
        n = 1
        for d in shape:
            n *= d
        return red[offs[i]:offs[i] + n // 128].reshape(shape)

    loss = red[0, 0]
    G_n1w = take(1, (1, D))
    G_gb = take(2, (1, KW))
    G_gnw = take(3, (1, DV))
    G_ps = take(4, (1, PW))
    G_n2w = take(5, (1, D))
    G_fw = take(6, (1, D))
    G_meta = lax.dynamic_slice(take(7, (N_META, D)), (0, me * 512), (N_META, 512))
    G_gw2 = lax.dynamic_slice(take(8, (RANK, KW)), (0, me * 128), (RANK, 128))
    G_pw = lax.dynamic_slice(take(9, (4, GC, GC)), (0, me * 64, 0), (4, 64, GC))

    G_win, d_win, nm_win, nv_win = big["win"]
    G_wout, d_wout, nm_wout, nv_wout = big["wout"]
    G_w1, d_w1, nm_w1, nv_w1 = big["w1"]
    G_w2, d_w2, nm_w2, nv_w2 = big["w2"]
    ws = [meta_tokens, norm1_w, gate_w2[0], gate_b, gla_norm_w, pool_w[0], pool_scale, norm2_w, fw]
    gs = [G_meta, G_n1w, G_gw2, G_gb, G_gnw, G_pw, G_ps, G_n2w, G_fw]
    ms = [m_meta_tokens, m_norm1_w, m_gate_w2[0], m_gate_b, m_gla_norm_w, m_pool_w[0], m_pool_scale, m_norm2_w,
          m_final_norm_w.reshape(1, D)]
    vs = [v_meta_tokens, v_norm1_w, v_gate_w2[0], v_gate_b, v_gla_norm_w, v_pool_w[0], v_pool_scale, v_norm2_w,
          v_final_norm_w.reshape(1, D)]
    ds, nms, nvs = _adam_small(ws, gs, ms, vs, after)

    def assemble(small, win_, wout_, w1_, w2_):
        meta_, n1_, gw2_, gb_, gnw_, pw_, ps_, n2_, fw_ = small
        return (meta_, n1_, win_[None], gw2_[None], gb_, gnw_, pw_[None], ps_, wout_[None], n2_, w1_[None], w2_[None],
                fw_.reshape(D))

    grads_out = assemble(gs, G_win, G_wout, G_w1, G_w2)
    deltas = assemble(ds, d_win, d_wout, d_w1, d_w2)
    new_m = assemble(nms, nm_win, nm_wout, nm_w1, nm_w2)
    new_v = assemble(nvs, nv_win, nv_wout, nv_w1, nv_w2)
    return (loss, grad_x[None], *grads_out, *deltas, *new_m, *new_v)
```

```python
import functools

import jax
import jax.numpy as jnp
from jax import lax
from jax.experimental import pallas as pl
from jax.experimental.pallas import tpu as pltpu

F32 = jnp.float32
BF16 = jnp.bfloat16

D = 2048
SEQ = 2048
N_META = 16
CH = 64
TP = 2176
NCH = TP // CH
ROW_LO = 112
X_LO = 128
ROW_HI = TP
XT = 128
NXT = TP // XT
HEADS = 4
DK = 128
DV = 256
KW = HEADS * DK
GW = HEADS * DV
RANK = 16
TAU = 16.0
WINDOWS = (2, 4, 8, 16)
PW = 1024
GC = 256
DFF = 8192
EPS = 1e-6
SHARD_IN = 1028
PAD_IN = 1152
N_CHIP = 4

LR = 0.001
B1 = 0.9
B2 = 0.999
AEPS = 1e-08
WD = 0.01
STEP = 10

VMEM_LIMIT = 60 * 1024 * 1024
ANY = pl.BlockSpec(memory_space=pl.ANY)
VMEM_FULL = pl.BlockSpec(memory_space=pltpu.VMEM)
MESH = pl.DeviceIdType.MESH


def _cp(sem=None):
    if sem is None:
        return pltpu.CompilerParams(vmem_limit_bytes=VMEM_LIMIT)
    return pltpu.CompilerParams(dimension_semantics=sem, vmem_limit_bytes=VMEM_LIMIT)


def _dot(a, b):
    return jnp.dot(a, b, preferred_element_type=F32)


def _dot_nt(a, b):
    return lax.dot_general(a, b, (((1,), (1,)), ((), ())), preferred_element_type=F32)


def _dot_tn(a, b):
    return lax.dot_general(a, b, (((0,), (0,)), ((), ())), preferred_element_type=F32)


def _sds(shape, dtype):
    return jax.ShapeDtypeStruct(shape, dtype)


def _embed_norm(x, meta_full, w, dep=None):
    def body(x_ref, meta_ref, w_ref, dep_ref, h_ref, u_ref):
        i = pl.program_id(0)

        @pl.when(i == 0)
        def _():
            h_ref[...] = jnp.zeros_like(h_ref)
            h_ref[ROW_LO:X_LO, :] = meta_ref[...]

        @pl.when(i >= 1)
        def _():
            h_ref[...] = x_ref[...]

        h = h_ref[...]
        r = lax.rsqrt(jnp.mean(h * h, axis=-1, keepdims=True) + EPS)
        u_ref[...] = ((h * r) * w_ref[...]).astype(BF16)

    return pl.pallas_call(
        body, name="embed_norm1", grid=(NXT,),
        in_specs=[pl.BlockSpec((XT, D), lambda i: (jnp.maximum(i - 1, 0), 0)),
                  pl.BlockSpec((N_META, D), lambda i: (0, 0)),
                  pl.BlockSpec((1, D), lambda i: (0, 0)), ANY],
        out_specs=[pl.BlockSpec((XT, D), lambda i: (i, 0)), pl.BlockSpec((XT, D), lambda i: (i, 0))],
        out_shape=[_sds((TP, D), F32), _sds((TP, D), BF16)],
        compiler_params=_cp(("arbitrary",)),
    )(x, meta_full, w, _dep(dep))


def _norm_rows(h, w, name):
    tr = 272

    def body(h_ref, w_ref, o_ref):
        hv = h_ref[...]
        r = lax.rsqrt(jnp.mean(hv * hv, axis=-1, keepdims=True) + EPS)
        o_ref[...] = ((hv * r) * w_ref[...]).astype(BF16)

    return pl.pallas_call(
        body, name=name, grid=(TP // tr,),
        in_specs=[pl.BlockSpec((tr, D), lambda i: (i, 0)), pl.BlockSpec((1, D), lambda i: (0, 0))],
        out_specs=pl.BlockSpec((tr, D), lambda i: (i, 0)),
        out_shape=_sds((TP, D), BF16),
        compiler_params=_cp(("arbitrary",)),
    )(h, w)


def _loss_head(h2, target, fw):
    def body(h_ref, t_ref, w_ref, dh_ref, dhb_ref, dw_ref, loss_ref):
        i = pl.program_id(0)

        @pl.when(i == 0)
        def _():
            dw_ref[...] = jnp.zeros_like(dw_ref)
            loss_ref[...] = jnp.zeros_like(loss_ref)

        h = h_ref[...]
        w = w_ref[...]
        r = lax.rsqrt(jnp.mean(h * h, axis=-1, keepdims=True) + EPS)
        xh = h * r
        y = xh * w
        is_x = (i >= 1).astype(F32)
        diff = (y - t_ref[...]) * is_x
        loss_ref[...] += jnp.sum(diff * diff) * (0.5 / D)
        dy = diff * (1.0 / D)
        dw_ref[...] += jnp.sum(dy * xh, axis=0, keepdims=True)
        gx = dy * w
        dh = r * (gx - xh * jnp.mean(gx * xh, axis=-1, keepdims=True))
        dh_ref[...] = dh
        dhb_ref[...] = dh.astype(BF16)

    return pl.pallas_call(
        body, name="loss_head", grid=(NXT,),
        in_specs=[pl.BlockSpec((XT, D), lambda i: (i, 0)),
                  pl.BlockSpec((XT, D), lambda i: (jnp.maximum(i - 1, 0), 0)),
                  pl.BlockSpec((1, D), lambda i: (0, 0))],
        out_specs=[pl.BlockSpec((XT, D), lambda i: (i, 0)), pl.BlockSpec((XT, D), lambda i: (i, 0)),
                   pl.BlockSpec((1, D), lambda i: (0, 0)), pl.BlockSpec((8, 128), lambda i: (0, 0))],
        out_shape=[_sds((TP, D), F32), _sds((TP, D), BF16), _sds((1, D), F32), _sds((8, 128), F32)],
        compiler_params=_cp(("arbitrary",)),
    )(h2, target, fw)


def _norm_bwd(dn, h, dres, w, name, dep=None):
    tr = 272

    def body(dn_ref, h_ref, dres_ref, w_ref, dep_ref, o_ref, ob_ref, dw_ref):
        @pl.when(pl.program_id(0) == 0)
        def _():
            dw_ref[...] = jnp.zeros_like(dw_ref)

        hv = h_ref[...]
        dnv = dn_ref[...]
        r = lax.rsqrt(jnp.mean(hv * hv, axis=-1, keepdims=True) + EPS)
        xh = hv * r
        dw_ref[...] += jnp.sum(dnv * xh, axis=0, keepdims=True)
        gx = dnv * w_ref[...]
        dh = dres_ref[...] + r * (gx - xh * jnp.mean(gx * xh, axis=-1, keepdims=True))
        o_ref[...] = dh
        ob_ref[...] = dh.astype(BF16)

    row = pl.BlockSpec((tr, D), lambda i: (i, 0))
    vec = pl.BlockSpec((1, D), lambda i: (0, 0))
    return pl.pallas_call(
        body, name=name, grid=(TP // tr,),
        in_specs=[row, row, row, vec, ANY], out_specs=[row, row, vec],
        out_shape=[_sds((TP, D), F32), _sds((TP, D), BF16), _sds((1, D), F32)],
        compiler_params=_cp(("arbitrary",)),
    )(dn, h, dres, w, _dep(dep))


def _input_grad(du, h0, dh1, w, dep=None):
    def body(du_ref, h_ref, dres_ref, w_ref, dep_ref, gx_ref, gm_ref, dw_ref):
        i = pl.program_id(0)

        @pl.when(i == 0)
        def _():
            dw_ref[...] = jnp.zeros_like(dw_ref)

        hv = h_ref[...]
        dnv = du_ref[...]
        r = lax.rsqrt(jnp.mean(hv * hv, axis=-1, keepdims=True) + EPS)
        xh = hv * r
        dw_ref[...] += jnp.sum(dnv * xh, axis=0, keepdims=True)
        g = dnv * w_ref[...]
        dh = dres_ref[...] + r * (g - xh * jnp.mean(g * xh, axis=-1, keepdims=True))

        @pl.when(i == 0)
        def _():
            gm_ref[...] = dh[ROW_LO:X_LO, :]

        @pl.when(i >= 1)
        def _():
            gx_ref[...] = dh

    row = pl.BlockSpec((XT, D), lambda i: (i, 0))
    vec = pl.BlockSpec((1, D), lambda i: (0, 0))
    return pl.pallas_call(
        body, name="input_grad", grid=(NXT,),
        in_specs=[row, row, row, vec, ANY],
        out_specs=[pl.BlockSpec((XT, D), lambda i: (jnp.maximum(i - 1, 0), 0)),
                   pl.BlockSpec((N_META, D), lambda i: (0, 0)), vec],
        out_shape=[_sds((SEQ, D), F32), _sds((N_META, D), F32), _sds((1, D), F32)],
        compiler_params=_cp(("arbitrary",)),
    )(du, h0, dh1, w, _dep(dep))


def _in_proj(u, wg):
    def body(u_ref, w_ref, o_ref):
        o_ref[0] = _dot(u_ref[...], w_ref[0])

    return pl.pallas_call(
        body, name="in_proj", grid=(N_CHIP,),
        in_specs=[VMEM_FULL, pl.BlockSpec((1, D, PAD_IN), lambda k: (k, 0, 0))],
        out_specs=pl.BlockSpec((1, TP, PAD_IN), lambda k: (k, 0, 0)),
        out_shape=_sds((N_CHIP, TP, PAD_IN), F32),
        compiler_params=_cp(("arbitrary",)),
    )(u, wg)


def _out_proj(og, op, wout, h0, dep=None):
    tn = 512

    def body(og_ref, op_ref, w_ref, h_ref, dep_ref, o_ref):
        acc = _dot(og_ref[...], w_ref[0:GW, :]) + _dot(op_ref[...], w_ref[GW:D, :])
        o_ref[...] = h_ref[...] + acc

    return pl.pallas_call(
        body, name="out_proj", grid=(D // tn,),
        in_specs=[VMEM_FULL, VMEM_FULL, pl.BlockSpec((D, tn), lambda j: (0, j)),
                  pl.BlockSpec((TP, tn), lambda j: (0, j)), ANY],
        out_specs=pl.BlockSpec((TP, tn), lambda j: (0, j)),
        out_shape=_sds((TP, D), F32),
        compiler_params=_cp(("arbitrary",)),
    )(og, op, wout, h0, _dep(dep))


def _mlp_up(n2, w1g, part, prev=None, dep=None):
    tn = 1024
    per = D // tn

    def body(n_ref, w_ref, dep_ref, *rest):
        zr_ref, a_ref, token = rest[-3:]
        z = jnp.maximum(_dot(n_ref[...], w_ref[0]), 0.0)
        zr_ref[...] = z.astype(BF16)
        a_ref[...] = (z * z).astype(BF16)
        token[...] = jnp.zeros_like(token)

    col = pl.BlockSpec((TP, tn), lambda k, j: (0, (2 * part + k) * per + j))
    return pl.pallas_call(
        body, name="mlp_up_%d" % part, grid=(N_CHIP // 2, per),
        in_specs=[VMEM_FULL, pl.BlockSpec((1, D, tn), lambda k, j: (2 * part + k, 0, j)), ANY] + ([ANY, ANY] if prev else []),
        out_specs=[col, col, pl.BlockSpec((8, 128), lambda k, j: (0, 0))],
        out_shape=[_sds((TP, DFF), BF16), _sds((TP, DFF), BF16), _sds((8, 128), F32)],
        input_output_aliases={3: 0, 4: 1} if prev else {},
        compiler_params=_cp(("arbitrary", "arbitrary")),
    )(n2, w1g, _dep(dep), *(prev or ()))


def _mlp_down(a, w2, h1):
    tk = 1024
    nk = DFF // tk

    def body(a_ref, w_ref, h_ref, o_ref, acc_ref):
        k = pl.program_id(0)

        @pl.when(k == 0)
        def _():
            pltpu.sync_copy(h_ref, acc_ref)

        acc_ref[...] += _dot(a_ref[...], w_ref[...])

        @pl.when(k == nk - 1)
        def _():
            pltpu.sync_copy(acc_ref, o_ref)

    return pl.pallas_call(
        body, name="mlp_down", grid=(nk,),
        in_specs=[pl.BlockSpec((TP, tk), lambda k: (0, k)), pl.BlockSpec((tk, D), lambda k: (k, 0)), ANY],
        out_specs=ANY,
        out_shape=_sds((TP, D), F32),
        scratch_shapes=[pltpu.VMEM((TP, D), F32)],
        compiler_params=_cp(("arbitrary",)),
    )(a, w2, h1)


def _mlp_dz(dh2b, w2, zr, dep=None):
    tn = 1024

    def body(d_ref, w_ref, z_ref, dep_ref, o_ref):
        da = _dot_nt(d_ref[...], w_ref[...])
        o_ref[...] = (da * (2.0 * z_ref[...].astype(F32))).astype(BF16)

    col = pl.BlockSpec((TP, tn), lambda j: (0, j))
    return pl.pallas_call(
        body, name="mlp_dz", grid=(DFF // tn,),
        in_specs=[VMEM_FULL, pl.BlockSpec((tn, D), lambda j: (j, 0)), col, ANY],
        out_specs=col,
        out_shape=_sds((TP, DFF), BF16),
        compiler_params=_cp(("arbitrary",)),
    )(dh2b, w2, zr, _dep(dep))


def _grad_w2(a, dh2b):
    tm = 1024

    def body(a_ref, d_ref, o_ref):
        o_ref[...] = _dot_tn(a_ref[...], d_ref[...])

    return pl.pallas_call(
        body, name="grad_w2", grid=(DFF // tm,),
        in_specs=[pl.BlockSpec((TP, tm), lambda j: (0, j)), VMEM_FULL],
        out_specs=pl.BlockSpec((tm, D), lambda j: (j, 0)),
        out_shape=_sds((DFF, D), F32),
        compiler_params=_cp(("arbitrary",)),
    )(a, dh2b)


def _dep(token):
    return jnp.zeros((8, 128), F32) if token is None else token


def _grad_w1(n2, dz, dep=None):
    tn = 1024
    per = D // tn

    def body(n_ref, d_ref, dep_ref, o_ref):
        o_ref[0] = _dot_tn(n_ref[...], d_ref[...])

    return pl.pallas_call(
        body, name="grad_w1", grid=(N_CHIP, per),
        in_specs=[VMEM_FULL, pl.BlockSpec((TP, tn), lambda k, j: (0, k * per + j)), ANY],
        out_specs=pl.BlockSpec((1, D, tn), lambda k, j: (k, 0, j)),
        out_shape=_sds((N_CHIP, D, D), F32),
        compiler_params=_cp(("arbitrary", "arbitrary")),
    )(n2, dz, _dep(dep))


def _mlp_dn(dz, w1g, dep=None):
    tk = 1024
    per = D // tk
    nk = DFF // tk

    def body(d_ref, w_ref, dep_ref, o_ref, acc_ref):
        k = pl.program_id(0)
        part = _dot_nt(d_ref[...], w_ref[0])

        @pl.when(k == 0)
        def _():
            acc_ref[...] = part

        @pl.when(k > 0)
        def _():
            acc_ref[...] += part

        @pl.when(k == nk - 1)
        def _():
            pltpu.sync_copy(acc_ref, o_ref)

    return pl.pallas_call(
        body, name="mlp_dn", grid=(nk,),
        in_specs=[pl.BlockSpec((TP, tk), lambda k: (0, k)),
                  pl.BlockSpec((1, D, tk), lambda k: (k // per, 0, k % per)), ANY],
        out_specs=ANY,
        out_shape=_sds((TP, D), F32),
        scratch_shapes=[pltpu.VMEM((TP, D), F32)],
        compiler_params=_cp(("arbitrary",)),
    )(dz, w1g, _dep(dep))


def _mixed_grad(dh1b, wout):
    tn = 512

    def body(d_ref, w_ref, o_ref):
        o_ref[...] = _dot_nt(d_ref[...], w_ref[...])

    return pl.pallas_call(
        body, name="mixed_grad", grid=(D // tn,),
        in_specs=[VMEM_FULL, pl.BlockSpec((tn, D), lambda j: (j, 0))],
        out_specs=pl.BlockSpec((TP, tn), lambda j: (0, j)),
        out_shape=_sds((TP, D), F32),
        compiler_params=_cp(("arbitrary",)),
    )(dh1b, wout)


def _grad_wout(og, op, dh1b):
    tm = 512

    def body(og_ref, op_ref, d_ref, o_ref):
        j = pl.program_id(0)

        @pl.when(j < 2)
        def _():
            o_ref[0] = _dot_tn(og_ref[...], d_ref[...])

        @pl.when(j >= 2)
        def _():
            o_ref[0] = _dot_tn(op_ref[...], d_ref[...])

    return pl.pallas_call(
        body, name="grad_wout", grid=(N_CHIP,),
        in_specs=[pl.BlockSpec((TP, tm), lambda j: (0, jnp.minimum(j, 1))),
                  pl.BlockSpec((TP, tm), lambda j: (0, jnp.maximum(j - 2, 0))), VMEM_FULL],
        out_specs=pl.BlockSpec((1, tm, D), lambda j: (j, 0, 0)),
        out_shape=_sds((N_CHIP, tm, D), F32),
        compiler_params=_cp(("arbitrary",)),
    )(og, op, dh1b)


def _in_grad(dq, dk, dv, dr, dglr, dpu, wg, dep=None):
    def body(dq_ref, dk_ref, dv_ref, dr_ref, dg_ref, dpu_ref, w_ref, dep_ref, o_ref):
        dv, dr, dg = dv_ref[...], dr_ref[...], dg_ref[...]
        head, tail = slice(0, GW), slice(GW, PAD_IN)
        o_ref[...] = (_dot_nt(dq_ref[...], w_ref[0, :, 0:KW]) + _dot_nt(dk_ref[...], w_ref[0, :, KW:GW])
                      + _dot_nt(dv[:, 0:128], w_ref[0, :, tail])
                      + _dot_nt(dv, w_ref[1, :, head]) + _dot_nt(dr[:, 0:128], w_ref[1, :, tail])
                      + _dot_nt(dr, w_ref[2, :, head]) + _dot_nt(dg, w_ref[2, :, tail])
                      + _dot_nt(dpu_ref[...], w_ref[3, :, head]) + _dot_nt(dg, w_ref[3, :, tail]))

    tn = 512
    return pl.pallas_call(
        body, name="in_grad", grid=(D // tn,),
        in_specs=[VMEM_FULL] * 6 + [pl.BlockSpec((N_CHIP, tn, PAD_IN), lambda j: (0, j, 0)), ANY],
        out_specs=pl.BlockSpec((TP, tn), lambda j: (0, j)),
        out_shape=_sds((TP, D), F32),
        compiler_params=_cp(("arbitrary",)),
    )(dq, dk, dv, dr, dglr, dpu, wg, _dep(dep))


def _grad_win(u, dq, dk, dv, dr, dglr, dpu, dep=None):
    tm = 1024

    def body(u_ref, dq_hbm, dk_hbm, dv_hbm, dr_hbm, dg_hbm, dpu_hbm, dep_ref, o_ref, dp_ref, sem):
        k, m = pl.program_id(0), pl.program_id(1)
        head, tail = slice(0, GW), slice(GW, PAD_IN)
        pieces = [[(dq_hbm, slice(0, KW)), (dk_hbm, slice(KW, GW)), (dv_hbm.at[:, 0:128], tail)],
                  [(dv_hbm, head), (dr_hbm.at[:, 0:128], tail)],
                  [(dr_hbm, head), (dg_hbm, tail)],
                  [(dpu_hbm, head), (dg_hbm, tail)]]

        def copies(kk):
            return [pltpu.make_async_copy(src, dp_ref.at[kk % 2, :, cols], sem.at[kk % 2, i])
                    for i, (src, cols) in enumerate(pieces[kk])]

        @pl.when((k == 0) & (m == 0))
        def _():
            for cp in copies(0):
                cp.start()

        for kk in range(N_CHIP):
            @pl.when((k == kk) & (m == 0))
            def _(kk=kk):
                for cp in copies(kk):
                    cp.wait()
                if kk + 1 < N_CHIP:
                    for cp in copies(kk + 1):
                        cp.start()

        g = _dot_tn(u_ref[...], dp_ref[k % 2])
        lane = lax.broadcasted_iota(jnp.int32, (tm, PAD_IN), 1)
        for kk in range(N_CHIP):
            @pl.when(k == kk)
            def _(kk=kk):
                if kk == 0:
                    nat = g
                elif kk < 3:
                    nat = pltpu.roll(g, PAD_IN - 4 * kk, 1)
                else:
                    nat = jnp.where(lane < 4, pltpu.roll(g, PAD_IN - (GW + 12), 1), pltpu.roll(g, 4, 1))
                o_ref[0] = nat[:, 0:SHARD_IN]

    return pl.pallas_call(
        body, name="grad_win", grid=(N_CHIP, D // tm),
        in_specs=[pl.BlockSpec((TP, tm), lambda k, m: (0, m))] + [ANY] * 7,
        out_specs=pl.BlockSpec((1, tm, SHARD_IN), lambda k, m: (k, m, 0)),
        out_shape=_sds((N_CHIP, D, SHARD_IN), F32),
        scratch_shapes=[pltpu.VMEM((2, TP, PAD_IN), BF16), pltpu.SemaphoreType.DMA((2, 3))],
        compiler_params=_cp(("arbitrary", "arbitrary")),
    )(u, dq, dk, dv, dr, dglr, dpu, _dep(dep))


def _split3(x):
    hi = x.astype(BF16)
    r1 = x - hi.astype(F32)
    mid = r1.astype(BF16)
    lo = (r1 - mid.astype(F32)).astype(BF16)
    return hi, mid, lo


def _tri_sum(tri, x):
    hi, mid, lo = _split3(x)
    return _dot(tri, hi) + _dot(tri, mid) + _dot(tri, lo)


def _gla_common(n, glr, gw2, gb):
    rows = n * CH + lax.broadcasted_iota(jnp.int32, (CH, 1), 0)
    valid = (rows >= ROW_LO) & (rows < ROW_HI)
    g_raw = _dot(glr.astype(BF16), gw2.astype(BF16)) + gb
    logsig = jnp.minimum(g_raw, 0.0) - jnp.log(1.0 + jnp.exp(-jnp.abs(g_raw)))
    logg = jnp.where(valid, logsig * (1.0 / TAU), 0.0)
    ci = lax.broadcasted_iota(jnp.int32, (CH, CH), 0)
    si = lax.broadcasted_iota(jnp.int32, (CH, CH), 1)
    lower = ci >= si
    G = _tri_sum(lower.astype(BF16), logg)
    Gl = G[CH - 1:CH, :]
    return valid, g_raw, lower, G, Gl


GSUB = 2


def _p_specs(index):
    def spec(width, shard, col):
        return pl.BlockSpec((1, GSUB * CH, width), lambda s: (shard, index(s), col))

    return [spec(KW, 0, 0), spec(KW, 0, 1), spec(GW, 1, 0), spec(128, 0, 8), spec(GW, 2, 0), spec(128, 1, 8),
            spec(128, 2, 8), spec(128, 3, 8)]


def _p_load(q_ref, k_ref, vm_ref, vh_ref, rm_ref, rh_ref, ga_ref, gb_ref):
    def joined(main, head):
        return jnp.concatenate([main[:, 0:128] + head, main[:, 128:]], axis=1)

    return q_ref[0], k_ref[0], joined(vm_ref[0], vh_ref[0]), joined(rm_ref[0], rh_ref[0]), ga_ref[0] + gb_ref[0]


def _gla_fwd(P, gw2, gb, gnw, dep=None):
    scale = DK ** -0.5

    def body(p0, p1, p2, p3, p4, p5, p6, p7, gw2_ref, gb_ref, gnw_ref, dep_ref, o_ref, og_ref, sp_ref, st_ref):
        n = pl.program_id(0)

        @pl.when(n == 0)
        def _():
            st_ref[...] = jnp.zeros_like(st_ref)

        q_blk, k_blk, v_blk, r_blk, glr_blk = _p_load(p0, p1, p2, p3, p4, p5, p6, p7)
        gnw_v = gnw_ref[...]
        for sub in range(GSUB):
            rows = slice(sub * CH, (sub + 1) * CH)
            _, _, lower, G, Gl = _gla_common(GSUB * n + sub, glr_blk[rows], gw2_ref[...], gb_ref[...])
            eG = jnp.exp(G)
            eN = jnp.exp(-G)
            eE = jnp.exp(Gl - G)
            dec = jnp.exp(Gl)
            for h in range(HEADS):
                ks = slice(h * DK, (h + 1) * DK)
                vs = slice(h * DV, (h + 1) * DV)
                kh = k_blk[rows, ks]
                vh = v_blk[rows, vs].astype(BF16)
                qd = ((q_blk[rows, ks] * scale) * eG[:, ks]).astype(BF16)
                ki = (kh * eN[:, ks]).astype(BF16)
                ke = (kh * eE[:, ks]).astype(BF16)
                st = st_ref[h]
                a = jnp.where(lower, _dot_nt(qd, ki), 0.0).astype(BF16)
                o = _dot(a, vh) + _dot_nt(qd, st.astype(BF16))
                sp_ref[sub, h] = st
                st_ref[h] = st * dec[:, ks] + _dot_tn(vh, ke)
                o_ref[rows, vs] = o
                rs = lax.rsqrt(jnp.mean(o * o, axis=-1, keepdims=True) + EPS)
                rv = r_blk[rows, vs]
                gate = rv / (1.0 + jnp.exp(-rv))
                og_ref[rows, vs] = (((o * rs) * gnw_v) * gate).astype(BF16)

    rv_ = pl.BlockSpec((GSUB * CH, GW), lambda n: (n, 0))

    def full(shape):
        return pl.BlockSpec(shape, lambda n: tuple(0 for _ in shape))

    return pl.pallas_call(
        body, name="gla_fwd", grid=(NCH // GSUB,),
        in_specs=_p_specs(lambda n: n) + [full((128, KW)), full((1, KW)), full((1, DV)), ANY],
        out_specs=[rv_, rv_, pl.BlockSpec((GSUB, HEADS, DV, DK), lambda n: (n, 0, 0, 0))],
        out_shape=[_sds((TP, GW), F32), _sds((TP, GW), BF16), _sds((NCH, HEADS, DV, DK), F32)],
        scratch_shapes=[pltpu.VMEM((HEADS, DV, DK), F32)],
        compiler_params=_cp(("arbitrary",)),
    )(*([P] * 8), gw2, gb, gnw, _dep(dep))


def _gla_bwd(dog, o, P, gw2, gb, gnw, sp, dep=None):
    scale = DK ** -0.5

    def body(dog_ref, o_ref, p0, p1, p2, p3, p4, p5, p6, p7, gw2_ref, gb_ref, gnw_ref, sp_ref, dep_ref,
             dq_ref, dk_ref, dv_ref, dr_ref, dglr_ref, dgw2_ref, dgb_ref, dgnw_ref, ds_ref):
        step = pl.program_id(0)
        blk = NCH // GSUB - 1 - step

        @pl.when(step == 0)
        def _():
            ds_ref[...] = jnp.zeros_like(ds_ref)
            dgw2_ref[...] = jnp.zeros_like(dgw2_ref)
            dgb_ref[...] = jnp.zeros_like(dgb_ref)
            dgnw_ref[...] = jnp.zeros_like(dgnw_ref)

        q_blk, k_blk, v_blk, r_blk, glr_blk = _p_load(p0, p1, p2, p3, p4, p5, p6, p7)
        gw2_b = gw2_ref[...].astype(BF16)
        upper = lax.broadcasted_iota(jnp.int32, (CH, CH), 0) <= lax.broadcasted_iota(jnp.int32, (CH, CH), 1)
        gnw_v = gnw_ref[...]
        last = lax.broadcasted_iota(jnp.int32, (CH, 1), 0) == CH - 1
        for sub in reversed(range(GSUB)):
            rows = slice(sub * CH, (sub + 1) * CH)
            glr_v = glr_blk[rows]
            valid, g_raw, lower, G, Gl = _gla_common(GSUB * blk + sub, glr_v, gw2_ref[...], gb_ref[...])
            eG = jnp.exp(G)
            eN = jnp.exp(-G)
            eE = jnp.exp(Gl - G)
            dec = jnp.exp(Gl)
            dgnw_acc = jnp.zeros((1, DV), F32)
            dG_parts = []
            for h in range(HEADS):
                ks = slice(h * DK, (h + 1) * DK)
                vs = slice(h * DV, (h + 1) * DV)
                oh = o_ref[rows, vs]
                rv = r_blk[rows, vs]
                dg = dog_ref[rows, vs]
                sig = 1.0 / (1.0 + jnp.exp(-rv))
                gate = rv * sig
                rs = lax.rsqrt(jnp.mean(oh * oh, axis=-1, keepdims=True) + EPS)
                ohat = oh * rs
                dr_ref[rows, vs] = ((dg * (ohat * gnw_v)) * (sig * (1.0 + rv * (1.0 - sig)))).astype(BF16)
                don = dg * gate
                dgnw_acc = dgnw_acc + jnp.sum(don * ohat, axis=0, keepdims=True)
                gxn = don * gnw_v
                do = (rs * (gxn - ohat * jnp.mean(gxn * ohat, axis=-1, keepdims=True))).astype(BF16)
                kh = k_blk[rows, ks]
                vh = v_blk[rows, vs].astype(BF16)
                qd_f = (q_blk[rows, ks] * scale) * eG[:, ks]
                ki_f = kh * eN[:, ks]
                ke_f = kh * eE[:, ks]
                qd, ki, ke = qd_f.astype(BF16), ki_f.astype(BF16), ke_f.astype(BF16)
                spt = sp_ref[sub, h]
                dst = ds_ref[h]
                dst_b = dst.astype(BF16)
                a_t = jnp.where(upper, _dot_nt(ki, qd), 0.0).astype(BF16)
                da = jnp.where(lower, _dot_nt(do, vh), 0.0).astype(BF16)
                da_t = jnp.where(upper, _dot_nt(vh, do), 0.0).astype(BF16)
                dv_ref[rows, vs] = (_dot(a_t, do) + _dot_nt(ke, dst_b)).astype(BF16)
                dqd = _dot(da, ki) + _dot(do, spt.astype(BF16))
                dki = _dot(da_t, qd)
                dke = _dot(vh, dst_b)
                ddec = jnp.sum(spt * dst, axis=0, keepdims=True)
                ds_ref[h] = dst * dec[:, ks] + _dot_tn(do, qd)
                dq_ref[rows, ks] = ((dqd * eG[:, ks]) * scale).astype(BF16)
                dk_ref[rows, ks] = (dki * eN[:, ks] + dke * eE[:, ks]).astype(BF16)
                dke_ke = dke * ke_f
                dG = dqd * qd_f - dki * ki_f - dke_ke
                dGl = jnp.sum(dke_ke, axis=0, keepdims=True) + ddec * dec[:, ks]
                dG_parts.append(dG + jnp.where(last, dGl, 0.0))
            dgnw_ref[...] += dgnw_acc
            dG_all = jnp.concatenate(dG_parts, axis=1)
            dlogg = jnp.where(valid, _tri_sum(upper.astype(BF16), dG_all), 0.0)
            dg_raw = (dlogg * (1.0 / TAU)) * (1.0 / (1.0 + jnp.exp(g_raw)))
            dgb_ref[...] += jnp.sum(dg_raw, axis=0, keepdims=True)
            dg_b = dg_raw.astype(BF16)
            dgw2_ref[...] += _dot_tn(glr_v.astype(BF16), dg_b)
            dglr_ref[rows, :] = _dot_nt(dg_b, gw2_b).astype(BF16)

    def back(s):
        return NCH // GSUB - 1 - s

    rk = pl.BlockSpec((GSUB * CH, KW), lambda s: (back(s), 0))
    rv_ = pl.BlockSpec((GSUB * CH, GW), lambda s: (back(s), 0))
    rg = pl.BlockSpec((GSUB * CH, 128), lambda s: (back(s), 0))

    def full(shape):
        return pl.BlockSpec(shape, lambda s: tuple(0 for _ in shape))

    return pl.pallas_call(
        body, name="gla_bwd", grid=(NCH // GSUB,),
        in_specs=[rv_, rv_] + _p_specs(back) + [full((128, KW)), full((1, KW)), full((1, DV)),
                  pl.BlockSpec((GSUB, HEADS, DV, DK), lambda s: (back(s), 0, 0, 0)), ANY],
        out_specs=[rk, rk, rv_, rv_, rg, full((128, KW)), full((1, KW)), full((1, DV))],
        out_shape=[_sds((TP, KW), BF16), _sds((TP, KW), BF16), _sds((TP, GW), BF16), _sds((TP, GW), BF16),
                   _sds((TP, 128), BF16), _sds((128, KW), F32), _sds((1, KW), F32), _sds((1, DV), F32)],
        scratch_shapes=[pltpu.VMEM((HEADS, DV, DK), F32)],
        compiler_params=_cp(("arbitrary",)),
    )(dog, o, *([P] * 8), gw2, gb, gnw, sp, _dep(dep))


POOL_TR = 128
HALO = 16


def _pool_counts(base, nrows):
    rows = base + lax.broadcasted_iota(jnp.int32, (nrows, 1), 0)
    valid = (rows >= ROW_LO) & (rows < ROW_HI)
    t1 = (rows - ROW_LO + 1).astype(F32)
    cnts = [jnp.clip(t1, 1.0, float(w)) for w in WINDOWS]
    return valid, cnts


def _pool_fwd(P, pw, ps, dep=None):
    def body(cur_ref, prev_ref, pw_ref, ps_ref, dep_ref, y_ref, op_ref):
        i = pl.program_id(0)
        cur = cur_ref[0]
        full = jnp.concatenate([prev_ref[0], cur], axis=0)
        s2 = full + pltpu.roll(full, 1, 0)
        s4 = s2 + pltpu.roll(s2, 2, 0)
        s8 = s4 + pltpu.roll(s4, 4, 0)
        s16 = s8 + pltpu.roll(s8, 8, 0)
        valid, cnts = _pool_counts(i * POOL_TR, POOL_TR)
        for g, s in enumerate((s2, s4, s8, s16)):
            cs = slice(g * GC, (g + 1) * GC)
            y = s[HALO:, cs] / cnts[g] - cur[:, cs]
            yb = jnp.where(valid, y, 0.0).astype(BF16)
            y_ref[:, cs] = yb
            op_ref[:, cs] = (_dot(yb, pw_ref[g].astype(BF16)) * ps_ref[:, cs]).astype(BF16)

    row = pl.BlockSpec((POOL_TR, PW), lambda i: (i, 0))
    per = POOL_TR // HALO
    return pl.pallas_call(
        body, name="pool_fwd", grid=(TP // POOL_TR,),
        in_specs=[pl.BlockSpec((1, POOL_TR, PW), lambda i: (3, i, 0)),
                  pl.BlockSpec((1, HALO, PW), lambda i: (3, jnp.maximum(i * per - 1, 0), 0)),
                  pl.BlockSpec((4, GC, GC), lambda i: (0, 0, 0)), pl.BlockSpec((1, PW), lambda i: (0, 0)), ANY],
        out_specs=[row, row],
        out_shape=[_sds((TP, PW), BF16), _sds((TP, PW), BF16)],
        compiler_params=_cp(("arbitrary",)),
    )(P, P, pw, ps, _dep(dep))


def _pool_bwd(dop, y, pw, ps, dep=None):
    nblk = TP // HALO

    def body(cur_ref, nxt_ref, y_ref, pw_ref, ps_ref, dep_ref, dpu_ref, dpw_ref, dps_ref):
        i = pl.program_id(0)

        @pl.when(i == 0)
        def _():
            dpw_ref[...] = jnp.zeros_like(dpw_ref)
            dps_ref[...] = jnp.zeros_like(dps_ref)

        n_all = POOL_TR + HALO
        dcur = cur_ref[...]
        dall = jnp.concatenate([dcur, nxt_ref[...]], axis=0)
        valid, cnts = _pool_counts(i * POOL_TR, n_all)
        for g in range(4):
            cs = slice(g * GC, (g + 1) * GC)
            pwb = pw_ref[g].astype(BF16)
            yb = y_ref[:, cs]
            dyw = (dall[:, cs] * ps_ref[:, cs]).astype(BF16)
            dps_ref[:, cs] += jnp.sum(dcur[:, cs] * _dot(yb, pwb), axis=0, keepdims=True)
            dpw_ref[g] += _dot_tn(yb, dyw[0:POOL_TR, :])
            dyv = jnp.where(valid, _dot_nt(dyw, pwb), 0.0)
            e = dyv / cnts[g]
            w = WINDOWS[g]
            sh = 1
            while sh < w:
                e = e + pltpu.roll(e, n_all - sh, 0)
                sh *= 2
            dpu_ref[:, cs] = (e[0:POOL_TR, :] - dyv[0:POOL_TR, :]).astype(BF16)

    row = pl.BlockSpec((POOL_TR, PW), lambda i: (i, 0))
    per = POOL_TR // HALO
    return pl.pallas_call(
        body, name="pool_bwd", grid=(TP // POOL_TR,),
        in_specs=[pl.BlockSpec((POOL_TR, PW), lambda i: (i, 1)),
                  pl.BlockSpec((HALO, PW), lambda i: (jnp.minimum(i * per + per, nblk - 1), 1)),
                  row, pl.BlockSpec((4, GC, GC), lambda i: (0, 0, 0)), pl.BlockSpec((1, PW), lambda i: (0, 0)), ANY],
        out_specs=[row, pl.BlockSpec((4, GC, GC), lambda i: (0, 0, 0)), pl.BlockSpec((1, PW), lambda i: (0, 0))],
        out_shape=[_sds((TP, PW), BF16), _sds((4, GC, GC), F32), _sds((1, PW), F32)],
        compiler_params=_cp(("arbitrary",)),
    )(dop, dop, y, pw, ps, _dep(dep))


def _place():
    x, y, c = lax.axis_index("x"), lax.axis_index("y"), lax.axis_index("c")
    chips = [(1 - x, y), (x, 1 - y), (1 - x, 1 - y)]
    return x, y, c, chips


HBM = pl.BlockSpec(memory_space=pltpu.HBM)
SEM = pl.BlockSpec(memory_space=pltpu.SEMAPHORE)
EFFECT = pltpu.SideEffectType.DATAFLOW_SIDE_EFFECTING


def _cast_into(w, place, cols_out, name, dep=None):
    rows, cols = w.shape
    tr = 512

    def body(p_ref, w_ref, dep_ref, o_ref):
        if cols_out != cols:
            o_ref[0] = jnp.zeros((tr, cols_out), BF16)
            o_ref[0, :, 0:cols] = w_ref[...].astype(BF16)
        else:
            o_ref[0] = w_ref[...].astype(BF16)

    grid_spec = pltpu.PrefetchScalarGridSpec(
        num_scalar_prefetch=1, grid=(rows // tr,),
        in_specs=[pl.BlockSpec((tr, cols), lambda i, p: (i, 0)), ANY],
        out_specs=pl.BlockSpec((1, tr, cols_out), lambda i, p: (p[0], i, 0)))
    return pl.pallas_call(
        body, name=name, grid_spec=grid_spec,
        out_shape=_sds((N_CHIP, rows, cols_out), BF16),
        compiler_params=_cp(("arbitrary",)),
    )(place, w, _dep(dep))


def _cast_win(w, place, dep=None):
    rows, cols = w.shape
    tr = 512

    def body(p_ref, w_ref, dep_ref, o_ref, t_ref):
        t_ref[...] = jnp.zeros_like(t_ref)
        t_ref[:, 0:cols] = w_ref[...]
        t = t_ref[...]
        lane = lax.broadcasted_iota(jnp.int32, (tr, PAD_IN), 1)
        for kk in range(N_CHIP):
            @pl.when(p_ref[0] == kk)
            def _(kk=kk):
                if kk == 0:
                    placed = t
                elif kk < 3:
                    placed = pltpu.roll(t, 4 * kk, 1)
                else:
                    pool = pltpu.roll(t, PAD_IN - 4, 1)
                    gate = pltpu.roll(t, GW + 12, 1)
                    placed = jnp.where(lane < GW, pool, jnp.where((lane >= GW + 12) & (lane < GW + 16), gate, 0.0))
                o_ref[0] = placed.astype(BF16)

    grid_spec = pltpu.PrefetchScalarGridSpec(
        num_scalar_prefetch=1, grid=(rows // tr,),
        in_specs=[pl.BlockSpec((tr, cols), lambda i, p: (i, 0)), ANY],
        out_specs=pl.BlockSpec((1, tr, PAD_IN), lambda i, p: (p[0], i, 0)),
        scratch_shapes=[pltpu.VMEM((tr, PAD_IN), F32)])
    return pl.pallas_call(
        body, name="cast_win", grid_spec=grid_spec,
        out_shape=_sds((N_CHIP, rows, PAD_IN), BF16),
        compiler_params=_cp(("arbitrary",)),
    )(place, w, _dep(dep))


def _half_rows(ref, k, which):
    h = ref.shape[1] // 2
    return ref.at[k, pl.ds(pl.multiple_of(which * h, 8), h), :]


def _sent_rows(ref, k, which, whole):
    return ref.at[k] if whole else _half_rows(ref, k, which)


def _gather_start(ws, name, whole=None):
    n = len(ws)
    whole = whole or [False] * n

    def body(*refs):
        ins = refs[:n]
        ssems = refs[n:2 * n]
        rsems = refs[2 * n:3 * n]
        token = refs[4 * n]
        x, y, c, chips = _place()
        me = 2 * x + y
        for w in range(n):
            blk = _sent_rows(ins[w], me, c, whole[w])
            for j, chip in enumerate(chips):
                pltpu.make_async_remote_copy(src_ref=blk, dst_ref=blk, send_sem=ssems[w].at[j], recv_sem=rsems[w].at[j],
                                             device_id=(*chip, c), device_id_type=MESH).start()
        token[...] = jnp.zeros_like(token)

    sem3 = pltpu.SemaphoreType.DMA((3,))
    outs = pl.pallas_call(
        body, name=name,
        out_shape=tuple([sem3] * (2 * n) + [pltpu.HBM(w.shape, w.dtype) for w in ws] + [_sds((8, 128), F32)]),
        in_specs=(HBM,) * n, out_specs=(SEM,) * (2 * n) + (HBM,) * n + (VMEM_FULL,),
        input_output_aliases={w: 2 * n + w for w in range(n)},
        compiler_params=pltpu.CompilerParams(has_side_effects=EFFECT),
    )(*[pltpu.with_memory_space_constraint(w, pltpu.HBM) for w in ws])
    return outs[:n], outs[n:2 * n], outs[2 * n:3 * n], outs[3 * n]


def _gather_wait(w, ssem, rsem, after, name, whole=False):
    def body(w_ref, ssem_ref, rsem_ref, after_ref, out_ref):
        x, y, c, chips = _place()
        me = 2 * x + y
        mine = _sent_rows(w_ref, me, c, whole)
        for j, (cx, cy) in enumerate(chips):
            cp = pltpu.make_async_remote_copy(src_ref=mine, dst_ref=_sent_rows(w_ref, 2 * cx + cy, c, whole),
                                              send_sem=ssem_ref.at[j], recv_sem=rsem_ref.at[j],
                                              device_id=(cx, cy, c), device_id_type=MESH)
            cp.wait_send()
            cp.wait_recv()

    return pl.pallas_call(
        body, name=name, out_shape=pltpu.HBM(w.shape, w.dtype),
        in_specs=(HBM, SEM, SEM, ANY), out_specs=HBM, input_output_aliases={0: 0},
        compiler_params=pltpu.CompilerParams(has_side_effects=EFFECT),
    )(w, ssem, rsem, after)


def _gather_copies(ref, kind, ssem, rsem):
    x, y, c, _ = _place()
    xn, yn, sib = (1 - x, y, c), (x, 1 - y, c), (x, y, 1 - c)
    kx, ky, kd = 2 * (1 - x) + y, 2 * x + (1 - y), 2 * (1 - x) + (1 - y)
    half = ref.shape[1] // 2
    quarter = half // 2

    def piece(k, q):
        return ref.at[k, pl.ds(pl.multiple_of(c * half + q * quarter, 8), quarter), :]

    if kind == "d":
        blk = _half_rows(ref, 2 * x + y, c)
        pairs = [(blk, xn), (blk, yn)]
    elif kind == "r":
        pairs = [(piece(ky, 1), xn), (piece(kx, 0), yn)]
    elif kind == "fx":
        pairs = [(_half_rows(ref, kx, c), sib), (_half_rows(ref, ky, c), sib)]
    else:
        pairs = [(_half_rows(ref, kd, c), sib)]
    return [pltpu.make_async_remote_copy(src_ref=blk, dst_ref=blk, send_sem=ssem.at[i], recv_sem=rsem.at[i],
                                         device_id=to, device_id_type=MESH) for i, (blk, to) in enumerate(pairs)]


def _gather_step(name, arrs, waits, starts, sems_in=(), after=()):
    n, nw, ns = len(arrs), len(waits), len(starts)
    after = [a for a in after if a is not None] or [_dep(None)]

    def body(*refs):
        a_in = refs[:n]
        s_in = refs[n:n + 2 * nw]
        outs = refs[n + 2 * nw + len(after):]
        s_out = outs[:2 * ns]
        for i, (ai, kind) in enumerate(waits):
            for cp in _gather_copies(a_in[ai], kind, s_in[2 * i], s_in[2 * i + 1]):
                cp.wait_send()
                cp.wait_recv()
        for i, (ai, kind) in enumerate(starts):
            for cp in _gather_copies(a_in[ai], kind, s_out[2 * i], s_out[2 * i + 1]):
                cp.start()
        if ns:
            token = outs[2 * ns + n]
            token[...] = jnp.zeros_like(token)

    sem2 = pltpu.SemaphoreType.DMA((2,))
    flat_in = [s for pair in sems_in for s in pair]
    arrs = [pltpu.with_memory_space_constraint(a, pltpu.HBM) for a in arrs]
    outs = pl.pallas_call(
        body, name=name,
        out_shape=tuple([sem2] * (2 * ns) + [pltpu.HBM(a.shape, a.dtype) for a in arrs]
                        + ([_sds((8, 128), F32)] if ns else [])),
        in_specs=(HBM,) * n + (SEM,) * (2 * nw) + (ANY,) * len(after),
        out_specs=(SEM,) * (2 * ns) + (HBM,) * n + ((VMEM_FULL,) if ns else ()),
        input_output_aliases={i: 2 * ns + i for i in range(n)},
        compiler_params=pltpu.CompilerParams(has_side_effects=EFFECT),
    )(*arrs, *flat_in, *after)
    sems = [(outs[2 * i], outs[2 * i + 1]) for i in range(ns)]
    return sems, list(outs[2 * ns:2 * ns + n]), (outs[2 * ns + n] if ns else None)


def _rs_start(sb, name, after=None):
    _, half, cols = sb.shape

    def body(sb_ref, land_ref, after_ref, ssem, rsem, sb_out, land_out, token):
        x, y, c, chips = _place()
        for j, (cx, cy) in enumerate(chips):
            pltpu.make_async_remote_copy(src_ref=sb_ref.at[2 * cx + cy], dst_ref=land_ref.at[j], send_sem=ssem.at[j],
                                         recv_sem=rsem.at[j], device_id=(cx, cy, c), device_id_type=MESH).start()
        token[...] = jnp.zeros_like(token)

    sem3 = pltpu.SemaphoreType.DMA((3,))
    land = lax.empty((3, half, cols), BF16)
    return pl.pallas_call(
        body, name=name,
        out_shape=(sem3, sem3, pltpu.HBM(sb.shape, sb.dtype), pltpu.HBM(land.shape, land.dtype), _sds((8, 128), F32)),
        in_specs=(HBM, HBM, ANY), out_specs=(SEM, SEM, HBM, HBM, VMEM_FULL), input_output_aliases={0: 2, 1: 3},
        compiler_params=pltpu.CompilerParams(has_side_effects=EFFECT),
    )(pltpu.with_memory_space_constraint(sb, pltpu.HBM), pltpu.with_memory_space_constraint(land, pltpu.HBM), _dep(after))


def _rs_wait(items, after, name):
    n = len(items)

    def body(*refs):
        x, y, c, chips = _place()
        for i in range(n):
            sb_ref, land_ref, ssem_ref, rsem_ref = refs[4 * i:4 * i + 4]
            for j, (cx, cy) in enumerate(chips):
                cp = pltpu.make_async_remote_copy(src_ref=sb_ref.at[2 * cx + cy], dst_ref=land_ref.at[j],
                                                  send_sem=ssem_ref.at[j], recv_sem=rsem_ref.at[j],
                                                  device_id=(cx, cy, c), device_id_type=MESH)
                cp.wait_send()
                cp.wait_recv()

    outs = pl.pallas_call(
        body, name=name,
        out_shape=tuple(pltpu.HBM(a.shape, a.dtype) for it in items for a in it[:2]),
        in_specs=(HBM, HBM, SEM, SEM) * n + (ANY,), out_specs=(HBM,) * (2 * n),
        input_output_aliases={4 * i + k: 2 * i + k for i in range(n) for k in range(2)},
        compiler_params=pltpu.CompilerParams(has_side_effects=EFFECT),
    )(*[a for it in items for a in it], after)
    return [outs[2 * i + 1] for i in range(n)]


def _pair_copy(g_ref, land_ref, ssem, rsem):
    x, y, c, _ = _place()
    h = g_ref.shape[1] // 2
    src = g_ref.at[:, pl.ds(pl.multiple_of((1 - c) * h, 8), h), :]
    return pltpu.make_async_remote_copy(src_ref=src, dst_ref=land_ref, send_sem=ssem.at[0], recv_sem=rsem.at[0],
                                        device_id=(x, y, 1 - c), device_id_type=MESH)


def _pair_start(g, name):
    def body(g_ref, land_ref, ssem, rsem, g_out, land_out, token):
        _pair_copy(g_ref, land_ref, ssem, rsem).start()
        token[...] = jnp.zeros_like(token)

    sem1 = pltpu.SemaphoreType.DMA((1,))
    land = lax.empty((N_CHIP, g.shape[1] // 2, g.shape[2]), F32)
    return pl.pallas_call(
        body, name=name,
        out_shape=(sem1, sem1, pltpu.HBM(g.shape, g.dtype), pltpu.HBM(land.shape, land.dtype), _sds((8, 128), F32)),
        in_specs=(HBM, HBM), out_specs=(SEM, SEM, HBM, HBM, VMEM_FULL), input_output_aliases={0: 2, 1: 3},
        compiler_params=pltpu.CompilerParams(has_side_effects=EFFECT),
    )(pltpu.with_memory_space_constraint(g, pltpu.HBM), pltpu.with_memory_space_constraint(land, pltpu.HBM))


def _pair_wait(g, land, ssem, rsem, after, name):
    def body(g_ref, land_ref, ssem_ref, rsem_ref, after_ref, g_out, land_out):
        cp = _pair_copy(g_ref, land_ref, ssem_ref, rsem_ref)
        cp.wait_send()
        cp.wait_recv()

    return pl.pallas_call(
        body, name=name,
        out_shape=(pltpu.HBM(g.shape, g.dtype), pltpu.HBM(land.shape, land.dtype)),
        in_specs=(HBM, HBM, SEM, SEM, ANY), out_specs=(HBM, HBM), input_output_aliases={0: 0, 1: 1},
        compiler_params=pltpu.CompilerParams(has_side_effects=EFFECT),
    )(g, land, ssem, rsem, after)


def _pair_sum(g, rcv, place, name):
    _, rows, cols = g.shape
    half = rows // 2
    tr = min(512, half)
    nt = half // tr

    def body(p_ref, g_ref, r_ref, sb_ref, sf_ref):
        s = pl.program_id(1)
        tot = g_ref[0] + r_ref[0]
        sb_ref[0] = tot.astype(BF16)

        @pl.when(s == p_ref[0])
        def _():
            sf_ref[...] = tot

    grid_spec = pltpu.PrefetchScalarGridSpec(
        num_scalar_prefetch=1, grid=(nt, N_CHIP),
        in_specs=[pl.BlockSpec((1, tr, cols), lambda t, s, p: (s, p[1] * nt + t, 0)),
                  pl.BlockSpec((1, tr, cols), lambda t, s, p: (s, t, 0))],
        out_specs=[pl.BlockSpec((1, tr, cols), lambda t, s, p: (s, t, 0)),
                   pl.BlockSpec((tr, cols), lambda t, s, p: (t, 0))])
    return pl.pallas_call(
        body, name=name, grid_spec=grid_spec,
        out_shape=[_sds((N_CHIP, half, cols), BF16), _sds((half, cols), F32)],
        compiler_params=_cp(("arbitrary", "arbitrary")),
    )(place, g, rcv)


def _final_sum(sf, rb, place, name):
    half, cols = sf.shape
    tr = min(512, half)
    nt = half // tr

    def body(p_ref, sf_ref, r_ref, out_ref):
        acc = sf_ref[...]
        for j in range(3):
            acc = acc + r_ref[j].astype(F32)
        out_ref[...] = acc

    grid_spec = pltpu.PrefetchScalarGridSpec(
        num_scalar_prefetch=1, grid=(nt,),
        in_specs=[pl.BlockSpec((tr, cols), lambda t, p: (t, 0)), pl.BlockSpec((3, tr, cols), lambda t, p: (0, t, 0))],
        out_specs=pl.BlockSpec((tr, cols), lambda t, p: (p[1] * nt + t, 0)))
    return pl.pallas_call(
        body, name=name, grid_spec=grid_spec,
        out_shape=_sds((2 * half, cols), F32),
        compiler_params=_cp(("arbitrary",)),
    )(place, sf, rb)


def _half_copy(f_ref, which, ssem, rsem):
    x, y, c, _ = _place()
    h = f_ref.shape[0] // 2
    rows = f_ref.at[pl.ds(pl.multiple_of(which * h, 8), h), :]
    return pltpu.make_async_remote_copy(src_ref=rows, dst_ref=rows, send_sem=ssem.at[0], recv_sem=rsem.at[0],
                                        device_id=(x, y, 1 - c), device_id_type=MESH)


def _half_start(fulls, name, after=None):
    n = len(fulls)

    def body(*refs):
        for i in range(n):
            _half_copy(refs[i], lax.axis_index("c"), refs[n + 1 + 2 * i], refs[n + 2 + 2 * i]).start()
        token = refs[4 * n + 1]
        token[...] = jnp.zeros_like(token)

    sem1 = pltpu.SemaphoreType.DMA((1,))
    outs = pl.pallas_call(
        body, name=name,
        out_shape=tuple([sem1] * (2 * n) + [pltpu.HBM(f.shape, f.dtype) for f in fulls] + [_sds((8, 128), F32)]),
        in_specs=(HBM,) * n + (ANY,), out_specs=(SEM,) * (2 * n) + (HBM,) * n + (VMEM_FULL,),
        input_output_aliases={i: 2 * n + i for i in range(n)},
        compiler_params=pltpu.CompilerParams(has_side_effects=EFFECT),
    )(*[pltpu.with_memory_space_constraint(f, pltpu.HBM) for f in fulls], _dep(after))
    return [(outs[2 * i], outs[2 * i + 1], outs[2 * n + i]) for i in range(n)], outs[3 * n]


def _half_wait(items, after, name):
    n = len(items)

    def body(*refs):
        c = lax.axis_index("c")
        for i in range(n):
            ssem_ref, rsem_ref, f_ref = refs[3 * i:3 * i + 3]
            _half_copy(f_ref, c, ssem_ref, rsem_ref).wait_send()
            _half_copy(f_ref, 1 - c, ssem_ref, rsem_ref).wait_recv()

    return pl.pallas_call(
        body, name=name, out_shape=tuple(pltpu.HBM(it[2].shape, it[2].dtype) for it in items),
        in_specs=(SEM, SEM, HBM) * n + (ANY,) * len(after), out_specs=(HBM,) * n,
        input_output_aliases={3 * i + 2: i for i in range(n)},
        compiler_params=pltpu.CompilerParams(has_side_effects=EFFECT),
    )(*[a for it in items for a in it], *after)


def _small_copies(src_ref, land_ref, ssem, rsem, first):
    x, y, c, chips = _place()
    if first:
        return [pltpu.make_async_remote_copy(src_ref=src_ref, dst_ref=land_ref, send_sem=ssem.at[0], recv_sem=rsem.at[0],
                                             device_id=(x, y, 1 - c), device_id_type=MESH)]
    return [pltpu.make_async_remote_copy(src_ref=src_ref, dst_ref=land_ref.at[j], send_sem=ssem.at[j], recv_sem=rsem.at[j],
                                         device_id=(*chip, c), device_id_type=MESH) for j, chip in enumerate(chips)]


def _small_start(src, first, name, after=None):
    n = 1 if first else 3

    def body(src_ref, land_ref, after_ref, ssem, rsem, src_out, land_out, token):
        for cp in _small_copies(src_ref, land_ref, ssem, rsem, first):
            cp.start()
        token[...] = jnp.zeros_like(token)

    sems = pltpu.SemaphoreType.DMA((n,))
    land = lax.empty(src.shape if first else (3,) + src.shape, F32)
    return pl.pallas_call(
        body, name=name,
        out_shape=(sems, sems, pltpu.HBM(src.shape, F32), pltpu.HBM(land.shape, F32), _sds((8, 128), F32)),
        in_specs=(HBM, HBM, ANY), out_specs=(SEM, SEM, HBM, HBM, VMEM_FULL), input_output_aliases={0: 2, 1: 3},
        compiler_params=pltpu.CompilerParams(has_side_effects=EFFECT),
    )(pltpu.with_memory_space_constraint(src, pltpu.HBM), pltpu.with_memory_space_constraint(land, pltpu.HBM), _dep(after))


def _small_wait(src, land, ssem, rsem, first, after, name):
    def body(src_ref, land_ref, ssem_ref, rsem_ref, after_ref, src_out, land_out):
        for cp in _small_copies(src_ref, land_ref, ssem_ref, rsem_ref, first):
            cp.wait_send()
            cp.wait_recv()

    return pl.pallas_call(
        body, name=name,
        out_shape=(pltpu.HBM(src.shape, F32), pltpu.HBM(land.shape, F32)),
        in_specs=(HBM, HBM, SEM, SEM, ANY), out_specs=(HBM, HBM), input_output_aliases={0: 0, 1: 1},
        compiler_params=pltpu.CompilerParams(has_side_effects=EFFECT),
    )(src, land, ssem, rsem, after)


def _small_pair_sum(vec, got):
    def body(v_ref, g_ref, o_ref):
        o_ref[...] = v_ref[...] + g_ref[...]

    return pl.pallas_call(body, name="small_pair_sum", in_specs=[VMEM_FULL] * 2, out_specs=VMEM_FULL,
                          out_shape=_sds(vec.shape, F32), compiler_params=_cp())(vec, got)


def _small_chip_sum(pair, got, place):
    def body(p_ref, pair_ref, got_ref, o_ref):
        acc = None
        for kk in range(N_CHIP):
            d = jnp.bitwise_xor(p_ref[0], kk)
            t = jnp.where(d == 0, pair_ref[...], jnp.where(d == 2, got_ref[0], jnp.where(d == 1, got_ref[1], got_ref[2])))
            acc = t if acc is None else acc + t
        o_ref[...] = acc

    grid_spec = pltpu.PrefetchScalarGridSpec(
        num_scalar_prefetch=1, grid=(1,),
        in_specs=[pl.BlockSpec(pair.shape, lambda i, p: (0, 0)), pl.BlockSpec(got.shape, lambda i, p: (0, 0, 0))],
        out_specs=pl.BlockSpec(pair.shape, lambda i, p: (0, 0)))
    return pl.pallas_call(body, name="small_chip_sum", grid_spec=grid_spec, out_shape=_sds(pair.shape, F32),
                          compiler_params=_cp(("arbitrary",)))(place, pair, got)


def _adam_math(w, g, m, v):
    m = B1 * m + (1.0 - B1) * g
    v = B2 * v + (1.0 - B2) * (g * g)
    m_hat = m / (1.0 - B1 ** STEP)
    v_hat = v / (1.0 - B2 ** STEP)
    delta = -LR * (m_hat / (jnp.sqrt(v_hat) + AEPS) + WD * w)
    return delta, m, v


def _adam_big(w, g, m, v, name, dep=None):
    rows, cols = w.shape
    tr = 256

    def body(w_ref, g_ref, m_ref, v_ref, dep_ref, go_ref, d_ref, nm_ref, nv_ref):
        g = g_ref[...]
        d, nm, nv = _adam_math(w_ref[...], g, m_ref[...], v_ref[...])
        go_ref[...] = g
        d_ref[...] = d
        nm_ref[...] = nm
        nv_ref[...] = nv

    blk = pl.BlockSpec((tr, cols), lambda i: (i, 0))
    return pl.pallas_call(
        body, name=name, grid=(rows // tr,),
        in_specs=[blk] * 4 + [ANY], out_specs=[blk] * 4, out_shape=[_sds((rows, cols), F32)] * 4,
        compiler_params=_cp(("arbitrary",)),
    )(w, g, m, v, _dep(dep))


def _adam_small(ws, gs, ms, vs, dep=None):
    n = len(ws)

    def body(*refs):
        for i in range(n):
            d, nm, nv = _adam_math(refs[i][...], refs[n + i][...], refs[2 * n + i][...], refs[3 * n + i][...])
            refs[4 * n + 1 + i][...] = d
            refs[5 * n + 1 + i][...] = nm
            refs[6 * n + 1 + i][...] = nv

    shapes = [_sds(w.shape, F32) for w in ws]
    outs = pl.pallas_call(
        body, name="adam_small",
        in_specs=[VMEM_FULL] * (4 * n) + [ANY], out_specs=[VMEM_FULL] * (3 * n), out_shape=shapes * 3,
        compiler_params=_cp(),
    )(*ws, *gs, *ms, *vs, _dep(dep))
    return outs[:n], outs[n:2 * n], outs[2 * n:]


def _pad_rows8(a):
    flat = a.reshape(-1, 128)
    pad = (-flat.shape[0]) % 8
    if pad:
        flat = jnp.concatenate([flat, jnp.zeros((pad, 128), F32)], axis=0)
    return flat


def kernel(x, meta_tokens, norm1_w, w_in, gate_w2, gate_b, gla_norm_w, pool_w, pool_scale, w_out, norm2_w, mlp_w1, mlp_w2, final_norm_w, loss_target, m_meta_tokens, m_norm1_w, m_w_in, m_gate_w2, m_gate_b, m_gla_norm_w, m_pool_w, m_pool_scale, m_w_out, m_norm2_w, m_mlp_w1, m_mlp_w2, m_final_norm_w, v_meta_tokens, v_norm1_w, v_w_in, v_gate_w2, v_gate_b, v_gla_norm_w, v_pool_w, v_pool_scale, v_w_out, v_norm2_w, v_mlp_w1, v_mlp_w2, v_final_norm_w):
    cx, cy, cc = lax.axis_index("x"), lax.axis_index("y"), lax.axis_index("c")
    me = (2 * cx + cy).astype(jnp.int32)

    place = jnp.stack([me, cc.astype(jnp.int32)])
    fw = final_norm_w.reshape(1, D)

    mine = jnp.concatenate([meta_tokens.reshape(64, 128), gate_w2[0], pool_w[0].reshape(512, 128)], axis=0)
    small = lax.dynamic_update_slice(jnp.zeros((N_CHIP, 592, 128), F32), mine[None], (me, 0, 0))
    (s_sm,), (r_sm,), (f_sm,), tok = _gather_start([small], "gather_start_small", [True])
    (sem_win_d,), (win,), tok = _gather_step("gather_start_win", [_cast_win(w_in[0], place, tok)], [], [(0, "d")])
    wout, w1, w2 = (_cast_into(w_out[0], place, D, "cast_wout", tok), _cast_into(mlp_w1[0], place, D, "cast_w1", tok),
                    _cast_into(mlp_w2[0], place, D, "cast_w2", tok))
    small = _gather_wait(f_sm, s_sm, r_sm, w2, "gather_wait_small", True)
    metaF = jnp.concatenate([small[k, 0:64].reshape(N_META, 512) for k in range(N_CHIP)], axis=1)
    gw2F = jnp.concatenate([small[k, 64:80] for k in range(N_CHIP)], axis=1)
    pwF = jnp.concatenate([small[k, 80:592].reshape(4, 64, GC) for k in range(N_CHIP)], axis=1)

    fly = {"win": win, "wout": wout, "w1": w1, "w2": w2}
    sems = {"win_d": sem_win_d}

    def step(name, names, waits, starts, after):
        at = {nm: i for i, nm in enumerate(names)}
        new, arrs, tok = _gather_step(name, [fly[nm] for nm in names], [(at[nm], k) for nm, k in waits],
                                      [(at[nm], k) for nm, k in starts], [sems[nm + "_" + k] for nm, k in waits], after)
        fly.update(zip(names, arrs))
        sems.update({nm + "_" + k: s for (nm, k), s in zip(starts, new)})
        return tok

    def relay_first():
        return step("gather_relay_win", ["win", "wout", "w1"], [("win", "d")],
                    [("win", "r"), ("win", "fx"), ("wout", "d"), ("w1", "d")], [v_w_in[0]])

    def get_win(after):
        tok = step("gather_land_win", ["win"], [("win", "r")], [("win", "fd")], [after])
        step("gather_wait_win", ["win"], [("win", "fx"), ("win", "fd")], [], [tok])
        return fly["win"]

    def relay_mid(after):
        return step("gather_relay_mid", ["wout"], [("wout", "d")], [("wout", "r"), ("wout", "fx")], [after, m_w_in[0]])

    def land_wout(after):
        return step("gather_land_wout", ["wout", "w1", "w2"], [("wout", "r"), ("w1", "d")],
                    [("wout", "fd"), ("w1", "r"), ("w1", "fx"), ("w2", "d")], [after])

    def get_wout(after):
        step("gather_wait_wout", ["wout"], [("wout", "fx"), ("wout", "fd")], [], [after])
        tok = step("gather_land_w1", ["w1"], [("w1", "r")], [("w1", "fd")], [fly["wout"]])
        return fly["wout"].reshape(D, D), tok

    def get_w1(after):
        step("gather_wait_w1", ["w1"], [("w1", "fx"), ("w1", "fd")], [], [after])
        return fly["w1"]

    def relay_last(after):
        return step("gather_relay_w2", ["w2"], [("w2", "d")], [("w2", "r"), ("w2", "fx")], [after])

    def get_w2(after):
        tok = step("gather_land_w2", ["w2"], [("w2", "r")], [("w2", "fd")], [after])
        step("gather_wait_w2", ["w2"], [("w2", "fx"), ("w2", "fd")], [], [tok])
        return fly["w2"].reshape(DFF, D)

    pairs, pending = {}, {}

    halves = {}

    def reduce_(names, after, tag):
        items = [(pending[nm][3], pending[nm][4], pending[nm][1], pending[nm][2]) for nm in names]
        landed = _rs_wait(items, after, "rs_wait_" + tag)
        fulls = [_final_sum(pending[nm][0], rb, place, "final_sum_" + nm) for nm, rb in zip(names, landed)]
        sent, token = _half_start(fulls, "half_start_" + tag)
        halves.update(zip(names, sent))
        return token

    def grad_start(nm, g):
        ssem, rsem, g_thru, land, token = _pair_start(g, "pair_start_" + nm)
        pairs[nm] = (ssem, rsem, g_thru, land)
        if nm == "win":
            token = reduce_(["w2", "w1", "wout"], token, "mlp_wout")
        return token

    def grad_finish(nm, after):
        ssem, rsem, g_thru, land = pairs[nm]
        g, rcv = _pair_wait(g_thru, land, ssem, rsem, after, "pair_wait_" + nm)
        sb, sf = _pair_sum(g, rcv, place, "pair_sum_" + nm)
        ssem, rsem, sb_thru, land, token = _rs_start(sb, "rs_start_" + nm)
        pending[nm] = (sf, ssem, rsem, sb_thru, land)
        return token

    (grad_x, loss8, d_n1w, d_gb, d_gnw, d_ps, d_n2w, d_fw, d_meta, d_gw2, d_pw) = _local_step(
        x[0], loss_target[0], dict(relay_first=relay_first, win=get_win, relay_mid=relay_mid, land_wout=land_wout,
                                   wout=get_wout, w1=get_w1, relay_last=relay_last, w2=get_w2),
        metaF, gw2F, pwF, norm1_w, gate_b, gla_norm_w, pool_scale, norm2_w, fw, grad_start, grad_finish)
    return _reduce_and_update(
        me, place, pending, halves, reduce_, grad_x, loss8, d_n1w, d_gb, d_gnw, d_ps, d_n2w, d_fw, d_meta, d_gw2, d_pw,
        meta_tokens, norm1_w, w_in, gate_w2, gate_b, gla_norm_w, pool_w, pool_scale, w_out, norm2_w, mlp_w1, mlp_w2, fw,
        m_meta_tokens, m_norm1_w, m_w_in, m_gate_w2, m_gate_b, m_gla_norm_w, m_pool_w, m_pool_scale, m_w_out, m_norm2_w,
        m_mlp_w1, m_mlp_w2, m_final_norm_w, v_meta_tokens, v_norm1_w, v_w_in, v_gate_w2, v_gate_b, v_gla_norm_w, v_pool_w,
        v_pool_scale, v_w_out, v_norm2_w, v_mlp_w1, v_mlp_w2, v_final_norm_w)


def _local_step(x, target, gather, metaF, gw2F, pwF, norm1_w, gate_b, gla_norm_w, pool_scale, norm2_w, fw, grad_start,
                grad_finish):
    h0, u = _embed_norm(x, metaF, norm1_w, gather["relay_first"]())
    Win = gather["win"](u)
    P = _in_proj(u, Win)
    gw2p = jnp.pad(gw2F, ((0, 128 - RANK), (0, 0)))
    yb, op = _pool_fwd(P, pwF, pool_scale, gather["relay_mid"](P))
    o, og, sp = _gla_fwd(P, gw2p, gate_b, gla_norm_w, gather["land_wout"](op))
    Wout, tok = gather["wout"](og)
    h1 = _out_proj(og, op, Wout, h0, tok)
    n2 = _norm_rows(h1, norm2_w, "norm2")
    W1 = gather["w1"](n2)
    zr, a, tok = _mlp_up(n2, W1, 0)
    zr, a, _ = _mlp_up(n2, W1, 1, (zr, a), gather["relay_last"](tok))
    W2 = gather["w2"](a)
    h2 = _mlp_down(a, W2, h1)

    dh2, dh2b, d_fw, loss8 = _loss_head(h2, target, fw)
    tok = grad_start("w2", _grad_w2(a, dh2b).reshape(N_CHIP, D, D))
    dz = _mlp_dz(dh2b, W2, zr, tok)
    tok = grad_finish("w2", dz)
    tok = grad_start("w1", _grad_w1(n2, dz, tok))
    dn2 = _mlp_dn(dz, W1, tok)
    tok = grad_finish("w1", dn2)
    dh1, dh1b, d_n2w = _norm_bwd(dn2, h1, dh2, norm2_w, "norm2_bwd", tok)
    dmixed = _mixed_grad(dh1b, Wout)
    tok = grad_start("wout", _grad_wout(og, op, dh1b))
    dpu, d_pw, d_ps = _pool_bwd(dmixed, yb, pwF, pool_scale, tok)
    dq, dk, dv, dr, dglr, d_gw2p, d_gb, d_gnw = _gla_bwd(dmixed, o, P, gw2p, gate_b, gla_norm_w, sp, tok)
    d_gw2 = d_gw2p[0:RANK]
    tok = grad_finish("wout", dq)
    tok = grad_start("win", _grad_win(u, dq, dk, dv, dr, dglr, dpu, tok))
    du = _in_grad(dq, dk, dv, dr, dglr, dpu, Win, tok)
    tok = grad_finish("win", du)
    grad_x, d_meta, d_n1w = _input_grad(du, h0, dh1, norm1_w, tok)
    return grad_x, loss8, d_n1w, d_gb, d_gnw, d_ps, d_n2w, d_fw, d_meta, d_gw2, d_pw


def _reduce_and_update(me, place, pending, halves, reduce_, grad_x, loss8, d_n1w, d_gb, d_gnw, d_ps, d_n2w, d_fw, d_meta, d_gw2,
                       d_pw,
                       meta_tokens, norm1_w, w_in, gate_w2, gate_b, gla_norm_w, pool_w, pool_scale, w_out, norm2_w,
                       mlp_w1, mlp_w2, fw, m_meta_tokens, m_norm1_w, m_w_in, m_gate_w2, m_gate_b, m_gla_norm_w, m_pool_w,
                       m_pool_scale, m_w_out, m_norm2_w, m_mlp_w1, m_mlp_w2, m_final_norm_w, v_meta_tokens, v_norm1_w, v_w_in,
                       v_gate_w2, v_gate_b, v_gla_norm_w, v_pool_w, v_pool_scale, v_w_out, v_norm2_w, v_mlp_w1, v_mlp_w2,
                       v_final_norm_w):
    parts = [loss8, d_n1w, d_gb, d_gnw, d_ps, d_n2w, d_fw, d_meta, d_gw2, d_pw]
    packed = [_pad_rows8(p) for p in parts]
    sizes = [p.shape[0] for p in packed]
    vec = jnp.concatenate(packed, axis=0)

    big = {}
    params = {"w2": (mlp_w2[0], m_mlp_w2[0], v_mlp_w2[0]), "w1": (mlp_w1[0], m_mlp_w1[0], v_mlp_w1[0]),
              "wout": (w_out[0], m_w_out[0], v_w_out[0]), "win": (w_in[0], m_w_in[0], v_w_in[0])}

    def update(names, after, tag):
        fulls = _half_wait([halves[nm] for nm in names], [after], "half_wait_" + tag)
        tok = None
        for nm, full in zip(names, fulls):
            w, m, v = params[nm]
            big[nm] = _adam_big(w, full, m, v, "adam_" + nm, tok)
            tok = big[nm][3]
        return tok

    s1, r1, vec, land1, tok = _small_start(vec, True, "small_start_pair")
    tok = update(["w2"], tok, "w2")
    vec, got = _small_wait(vec, land1, s1, r1, True, tok, "small_wait_pair")
    pair = _small_pair_sum(vec, got)
    s2, r2, pair, land2, tok = _small_start(pair, False, "small_start_chips")
    tok = reduce_(["win"], tok, "win")
    tok = update(["w1"], tok, "w1")
    tok = update(["win"], tok, "win")
    pair, got = _small_wait(pair, land2, s2, r2, False, tok, "small_wait_chips")
    red = _small_chip_sum(pair, got, place)
    after = update(["wout"], red, "wout")
    offs = [0]
    for s in sizes:
        offs.append(offs[-1] + s)

    def take(i, shape):
        n = 1
        for d in shape:
            n *= d
        return red[offs[i]:offs[i] + n // 128].reshape(shape)

    loss = red[0, 0]
    G_n1w = take(1, (1, D))
    G_gb = take(2, (1, KW))
    G_gnw = take(3, (1, DV))
    G_ps = take(4, (1, PW))
    G_n2w = take(5, (1, D))
    G_fw = take(6, (1, D))
    G_meta = lax.dynamic_slice(take(7, (N_META, D)), (0, me * 512), (N_META, 512))
    G_gw2 = lax.dynamic_slice(take(8, (RANK, KW)), (0, me * 128), (RANK, 128))
    G_pw = lax.dynamic_slice(take(9, (4, GC, GC)), (0, me * 64, 0), (4, 64, GC))

    G_win, d_win, nm_win, nv_win = big["win"]
    G_wout, d_wout, nm_wout, nv_wout = big["wout"]
    G_w1, d_w1, nm_w1, nv_w1 = big["w1"]
    G_w2, d_w2, nm_w2, nv_w2 = big["w2"]
    ws = [meta_tokens, norm1_w, gate_w2[0], gate_b, gla_norm_w, pool_w[0], pool_scale, norm2_w, fw]
    gs = [G_meta, G_n1w, G_gw2, G_gb, G_gnw, G_pw, G_ps, G_n2w, G_fw]
    ms = [m_meta_tokens, m_norm1_w, m_gate_w2[0], m_gate_b, m_gla_norm_w, m_pool_w[0], m_pool_scale, m_norm2_w,
          m_final_norm_w.reshape(1, D)]
    vs = [v_meta_tokens, v_norm1_w, v_gate_w2[0], v_gate_b, v_gla_norm_w, v_pool_w[0], v_pool_scale, v_norm2_w,
          v_final_norm_w.reshape(1, D)]
    ds, nms, nvs = _adam_small(ws, gs, ms, vs, after)

    def assemble(small, win_, wout_, w1_, w2_):
        meta_, n1_, gw2_, gb_, gnw_, pw_, ps_, n2_, fw_ = small
        return (meta_, n1_, win_[None], gw2_[None], gb_, gnw_, pw_[None], ps_, wout_[None], n2_, w1_[None], w2_[None],
                fw_.reshape(D))

    grads_out = assemble(gs, G_win, G_wout, G_w1, G_w2)
    deltas = assemble(ds, d_win, d_wout, d_w1, d_w2)
    new_m = assemble(nms, nm_win, nm_wout, nm_w1, nm_w2)
    new_v = assemble(nvs, nv_win, nv_wout, nv_w1, nv_w2)
    return (loss, grad_x[None], *grads_out, *deltas, *new_m, *new_v)
```

```python
import functools

import jax
import jax.numpy as jnp
from jax import lax
from jax.experimental import pallas as pl
from jax.experimental.pallas import tpu as pltpu

F32 = jnp.float32
BF16 = jnp.bfloat16

D = 2048
SEQ = 2048
N_META = 16
CH = 64
TP = 2176
NCH = TP // CH
ROW_LO = 112
X_LO = 128
ROW_HI = TP
XT = 128
NXT = TP // XT
HEADS = 4
DK = 128
DV = 256
KW = HEADS * DK
GW = HEADS * DV
RANK = 16
TAU = 16.0
WINDOWS = (2, 4, 8, 16)
PW = 1024
GC = 256
DFF = 8192
EPS = 1e-6
SHARD_IN = 1028
PAD_IN = 1152
N_CHIP = 4

LR = 0.001
B1 = 0.9
B2 = 0.999
AEPS = 1e-08
WD = 0.01
STEP = 10

VMEM_LIMIT = 60 * 1024 * 1024
ANY = pl.BlockSpec(memory_space=pl.ANY)
VMEM_FULL = pl.BlockSpec(memory_space=pltpu.VMEM)
MESH = pl.DeviceIdType.MESH


def _cp(sem=None):
    if sem is None:
        return pltpu.CompilerParams(vmem_limit_bytes=VMEM_LIMIT)
    return pltpu.CompilerParams(dimension_semantics=sem, vmem_limit_bytes=VMEM_LIMIT)


def _dot(a, b):
    return jnp.dot(a, b, preferred_element_type=F32)


def _dot_nt(a, b):
    return lax.dot_general(a, b, (((1,), (1,)), ((), ())), preferred_element_type=F32)


def _dot_tn(a, b):
    return lax.dot_general(a, b, (((0,), (0,)), ((), ())), preferred_element_type=F32)


def _sds(shape, dtype):
    return jax.ShapeDtypeStruct(shape, dtype)


def _embed_norm(x, meta_full, w, dep=None):
    def body(x_ref, meta_ref, w_ref, dep_ref, h_ref, u_ref):
        i = pl.program_id(0)

        @pl.when(i == 0)
        def _():
            h_ref[...] = jnp.zeros_like(h_ref)
            h_ref[ROW_LO:X_LO, :] = meta_ref[...]

        @pl.when(i >= 1)
        def _():
            h_ref[...] = x_ref[...]

        h = h_ref[...]
        r = lax.rsqrt(jnp.mean(h * h, axis=-1, keepdims=True) + EPS)
        u_ref[...] = ((h * r) * w_ref[...]).astype(BF16)

    return pl.pallas_call(
        body, name="embed_norm1", grid=(NXT,),
        in_specs=[pl.BlockSpec((XT, D), lambda i: (jnp.maximum(i - 1, 0), 0)),
                  pl.BlockSpec((N_META, D), lambda i: (0, 0)),
                  pl.BlockSpec((1, D), lambda i: (0, 0)), ANY],
        out_specs=[pl.BlockSpec((XT, D), lambda i: (i, 0)), pl.BlockSpec((XT, D), lambda i: (i, 0))],
        out_shape=[_sds((TP, D), F32), _sds((TP, D), BF16)],
        compiler_params=_cp(("arbitrary",)),
    )(x, meta_full, w, _dep(dep))


def _norm_rows(h, w, name):
    tr = 272

    def body(h_ref, w_ref, o_ref):
        hv = h_ref[...]
        r = lax.rsqrt(jnp.mean(hv * hv, axis=-1, keepdims=True) + EPS)
        o_ref[...] = ((hv * r) * w_ref[...]).astype(BF16)

    return pl.pallas_call(
        body, name=name, grid=(TP // tr,),
        in_specs=[pl.BlockSpec((tr, D), lambda i: (i, 0)), pl.BlockSpec((1, D), lambda i: (0, 0))],
        out_specs=pl.BlockSpec((tr, D), lambda i: (i, 0)),
        out_shape=_sds((TP, D), BF16),
        compiler_params=_cp(("arbitrary",)),
    )(h, w)


def _loss_head(h2, target, fw):
    def body(h_ref, t_ref, w_ref, dhb_ref, dw_ref, loss_ref):
        i = pl.program_id(0)

        @pl.when(i == 0)
        def _():
            dw_ref[...] = jnp.zeros_like(dw_ref)
            loss_ref[...] = jnp.zeros_like(loss_ref)

        h = h_ref[...]
        w = w_ref[...]
        r = lax.rsqrt(jnp.mean(h * h, axis=-1, keepdims=True) + EPS)
        xh = h * r
        y = xh * w
        is_x = (i >= 1).astype(F32)
        diff = (y - t_ref[...]) * is_x
        loss_ref[...] += jnp.sum(diff * diff) * (0.5 / D)
        dy = diff * (1.0 / D)
        dw_ref[...] += jnp.sum(dy * xh, axis=0, keepdims=True)
        gx = dy * w
        dh = r * (gx - xh * jnp.mean(gx * xh, axis=-1, keepdims=True))
        dhb_ref[...] = dh.astype(BF16)

    return pl.pallas_call(
        body, name="loss_head", grid=(NXT,),
        in_specs=[pl.BlockSpec((XT, D), lambda i: (i, 0)),
                  pl.BlockSpec((XT, D), lambda i: (jnp.maximum(i - 1, 0), 0)),
                  pl.BlockSpec((1, D), lambda i: (0, 0))],
        out_specs=[pl.BlockSpec((XT, D), lambda i: (i, 0)),
                   pl.BlockSpec((1, D), lambda i: (0, 0)), pl.BlockSpec((8, 128), lambda i: (0, 0))],
        out_shape=[_sds((TP, D), BF16), _sds((1, D), F32), _sds((8, 128), F32)],
        compiler_params=_cp(("arbitrary",)),
    )(h2, target, fw)


def _norm_bwd(dn, h, dres, w, name, dep=None):
    tr = 272

    def body(dn_ref, h_ref, dres_ref, w_ref, dep_ref, ob_ref, dw_ref):
        @pl.when(pl.program_id(0) == 0)
        def _():
            dw_ref[...] = jnp.zeros_like(dw_ref)

        hv = h_ref[...]
        dnv = dn_ref[...]
        r = lax.rsqrt(jnp.mean(hv * hv, axis=-1, keepdims=True) + EPS)
        xh = hv * r
        dw_ref[...] += jnp.sum(dnv * xh, axis=0, keepdims=True)
        gx = dnv * w_ref[...]
        dh = dres_ref[...].astype(F32) + r * (gx - xh * jnp.mean(gx * xh, axis=-1, keepdims=True))
        ob_ref[...] = dh.astype(BF16)

    row = pl.BlockSpec((tr, D), lambda i: (i, 0))
    vec = pl.BlockSpec((1, D), lambda i: (0, 0))
    return pl.pallas_call(
        body, name=name, grid=(TP // tr,),
        in_specs=[row, row, row, vec, ANY], out_specs=[row, vec],
        out_shape=[_sds((TP, D), BF16), _sds((1, D), F32)],
        compiler_params=_cp(("arbitrary",)),
    )(dn, h, dres, w, _dep(dep))


def _input_grad(du, h0, dh1, w, dep=None):
    def body(du_ref, h_ref, dres_ref, w_ref, dep_ref, gx_ref, gm_ref, dw_ref):
        i = pl.program_id(0)

        @pl.when(i == 0)
        def _():
            dw_ref[...] = jnp.zeros_like(dw_ref)

        hv = h_ref[...]
        dnv = du_ref[...]
        r = lax.rsqrt(jnp.mean(hv * hv, axis=-1, keepdims=True) + EPS)
        xh = hv * r
        dw_ref[...] += jnp.sum(dnv * xh, axis=0, keepdims=True)
        g = dnv * w_ref[...]
        dh = dres_ref[...].astype(F32) + r * (g - xh * jnp.mean(g * xh, axis=-1, keepdims=True))

        @pl.when(i == 0)
        def _():
            gm_ref[...] = dh[ROW_LO:X_LO, :]

        @pl.when(i >= 1)
        def _():
            gx_ref[...] = dh

    row = pl.BlockSpec((XT, D), lambda i: (i, 0))
    vec = pl.BlockSpec((1, D), lambda i: (0, 0))
    return pl.pallas_call(
        body, name="input_grad", grid=(NXT,),
        in_specs=[row, row, row, vec, ANY],
        out_specs=[pl.BlockSpec((XT, D), lambda i: (jnp.maximum(i - 1, 0), 0)),
                   pl.BlockSpec((N_META, D), lambda i: (0, 0)), vec],
        out_shape=[_sds((SEQ, D), F32), _sds((N_META, D), F32), _sds((1, D), F32)],
        compiler_params=_cp(("arbitrary",)),
    )(du, h0, dh1, w, _dep(dep))


def _in_proj(u, wg):
    def body(u_ref, w_ref, o_ref):
        o_ref[0] = _dot(u_ref[...], w_ref[0])

    return pl.pallas_call(
        body, name="in_proj", grid=(N_CHIP,),
        in_specs=[VMEM_FULL, pl.BlockSpec((1, D, PAD_IN), lambda k: (k, 0, 0))],
        out_specs=pl.BlockSpec((1, TP, PAD_IN), lambda k: (k, 0, 0)),
        out_shape=_sds((N_CHIP, TP, PAD_IN), F32),
        compiler_params=_cp(("arbitrary",)),
    )(u, wg)


def _out_proj(og, op, wout, h0, dep=None):
    tn = 512

    def body(og_ref, op_ref, w_ref, h_ref, dep_ref, o_ref):
        acc = _dot(og_ref[...], w_ref[0:GW, :]) + _dot(op_ref[...], w_ref[GW:D, :])
        o_ref[...] = h_ref[...] + acc

    return pl.pallas_call(
        body, name="out_proj", grid=(D // tn,),
        in_specs=[VMEM_FULL, VMEM_FULL, pl.BlockSpec((D, tn), lambda j: (0, j)),
                  pl.BlockSpec((TP, tn), lambda j: (0, j)), ANY],
        out_specs=pl.BlockSpec((TP, tn), lambda j: (0, j)),
        out_shape=_sds((TP, D), F32),
        compiler_params=_cp(("arbitrary",)),
    )(og, op, wout, h0, _dep(dep))


def _mlp_up(n2, w1g, part, prev=None, dep=None):
    tn = 1024
    per = D // tn

    def body(n_ref, w_ref, dep_ref, *rest):
        zr_ref, a_ref, token = rest[-3:]
        z = jnp.maximum(_dot(n_ref[...], w_ref[0]), 0.0)
        zr_ref[...] = z.astype(BF16)
        a_ref[...] = (z * z).astype(BF16)
        token[...] = jnp.zeros_like(token)

    col = pl.BlockSpec((TP, tn), lambda k, j: (0, (2 * part + k) * per + j))
    return pl.pallas_call(
        body, name="mlp_up_%d" % part, grid=(N_CHIP // 2, per),
        in_specs=[VMEM_FULL, pl.BlockSpec((1, D, tn), lambda k, j: (2 * part + k, 0, j)), ANY] + ([ANY, ANY] if prev else []),
        out_specs=[col, col, pl.BlockSpec((8, 128), lambda k, j: (0, 0))],
        out_shape=[_sds((TP, DFF), BF16), _sds((TP, DFF), BF16), _sds((8, 128), F32)],
        input_output_aliases={3: 0, 4: 1} if prev else {},
        compiler_params=_cp(("arbitrary", "arbitrary")),
    )(n2, w1g, _dep(dep), *(prev or ()))


def _mlp_down(a, w2, h1):
    tk = 1024
    nk = DFF // tk

    def body(a_ref, w_ref, h_ref, o_ref, acc_ref):
        k = pl.program_id(0)

        @pl.when(k == 0)
        def _():
            pltpu.sync_copy(h_ref, acc_ref)

        acc_ref[...] += _dot(a_ref[...], w_ref[...])

        @pl.when(k == nk - 1)
        def _():
            pltpu.sync_copy(acc_ref, o_ref)

    return pl.pallas_call(
        body, name="mlp_down", grid=(nk,),
        in_specs=[pl.BlockSpec((TP, tk), lambda k: (0, k)), pl.BlockSpec((tk, D), lambda k: (k, 0)), ANY],
        out_specs=ANY,
        out_shape=_sds((TP, D), F32),
        scratch_shapes=[pltpu.VMEM((TP, D), F32)],
        compiler_params=_cp(("arbitrary",)),
    )(a, w2, h1)


def _mlp_dz(dh2b, w2, zr, dep=None):
    tn = 1024

    def body(d_ref, w_ref, z_ref, dep_ref, o_ref):
        da = _dot_nt(d_ref[...], w_ref[...])
        o_ref[...] = (da * (2.0 * z_ref[...].astype(F32))).astype(BF16)

    col = pl.BlockSpec((TP, tn), lambda j: (0, j))
    return pl.pallas_call(
        body, name="mlp_dz", grid=(DFF // tn,),
        in_specs=[VMEM_FULL, pl.BlockSpec((tn, D), lambda j: (j, 0)), col, ANY],
        out_specs=col,
        out_shape=_sds((TP, DFF), BF16),
        compiler_params=_cp(("arbitrary",)),
    )(dh2b, w2, zr, _dep(dep))


def _grad_w2(a, dh2b):
    tm = 1024

    def body(a_ref, d_ref, o_ref):
        o_ref[...] = _dot_tn(a_ref[...], d_ref[...])

    return pl.pallas_call(
        body, name="grad_w2", grid=(DFF // tm,),
        in_specs=[pl.BlockSpec((TP, tm), lambda j: (0, j)), VMEM_FULL],
        out_specs=pl.BlockSpec((tm, D), lambda j: (j, 0)),
        out_shape=_sds((DFF, D), F32),
        compiler_params=_cp(("arbitrary",)),
    )(a, dh2b)


def _dep(token):
    return jnp.zeros((8, 128), F32) if token is None else token


def _grad_w1(n2, dz, dep=None):
    tn = 1024
    per = D // tn

    def body(n_ref, d_ref, dep_ref, o_ref):
        o_ref[0] = _dot_tn(n_ref[...], d_ref[...])

    return pl.pallas_call(
        body, name="grad_w1", grid=(N_CHIP, per),
        in_specs=[VMEM_FULL, pl.BlockSpec((TP, tn), lambda k, j: (0, k * per + j)), ANY],
        out_specs=pl.BlockSpec((1, D, tn), lambda k, j: (k, 0, j)),
        out_shape=_sds((N_CHIP, D, D), F32),
        compiler_params=_cp(("arbitrary", "arbitrary")),
    )(n2, dz, _dep(dep))


def _mlp_dn(dz, w1g, dep=None):
    tk = 1024
    per = D // tk
    nk = DFF // tk

    def body(d_ref, w_ref, dep_ref, o_ref, acc_ref):
        k = pl.program_id(0)
        part = _dot_nt(d_ref[...], w_ref[0])

        @pl.when(k == 0)
        def _():
            acc_ref[...] = part

        @pl.when(k > 0)
        def _():
            acc_ref[...] += part

        @pl.when(k == nk - 1)
        def _():
            pltpu.sync_copy(acc_ref, o_ref)

    return pl.pallas_call(
        body, name="mlp_dn", grid=(nk,),
        in_specs=[pl.BlockSpec((TP, tk), lambda k: (0, k)),
                  pl.BlockSpec((1, D, tk), lambda k: (k // per, 0, k % per)), ANY],
        out_specs=ANY,
        out_shape=_sds((TP, D), F32),
        scratch_shapes=[pltpu.VMEM((TP, D), F32)],
        compiler_params=_cp(("arbitrary",)),
    )(dz, w1g, _dep(dep))


def _mixed_grad(dh1b, wout):
    tn = 512

    def body(d_ref, w_ref, o_ref):
        o_ref[...] = _dot_nt(d_ref[...], w_ref[...])

    return pl.pallas_call(
        body, name="mixed_grad", grid=(D // tn,),
        in_specs=[VMEM_FULL, pl.BlockSpec((tn, D), lambda j: (j, 0))],
        out_specs=pl.BlockSpec((TP, tn), lambda j: (0, j)),
        out_shape=_sds((TP, D), F32),
        compiler_params=_cp(("arbitrary",)),
    )(dh1b, wout)


def _grad_wout(og, op, dh1b):
    tm = 512

    def body(og_ref, op_ref, d_ref, o_ref):
        j = pl.program_id(0)

        @pl.when(j < 2)
        def _():
            o_ref[0] = _dot_tn(og_ref[...], d_ref[...])

        @pl.when(j >= 2)
        def _():
            o_ref[0] = _dot_tn(op_ref[...], d_ref[...])

    return pl.pallas_call(
        body, name="grad_wout", grid=(N_CHIP,),
        in_specs=[pl.BlockSpec((TP, tm), lambda j: (0, jnp.minimum(j, 1))),
                  pl.BlockSpec((TP, tm), lambda j: (0, jnp.maximum(j - 2, 0))), VMEM_FULL],
        out_specs=pl.BlockSpec((1, tm, D), lambda j: (j, 0, 0)),
        out_shape=_sds((N_CHIP, tm, D), F32),
        compiler_params=_cp(("arbitrary",)),
    )(og, op, dh1b)


def _in_grad(dq, dk, dv, dr, dglr, dpu, wg, dep=None):
    def body(dq_ref, dk_ref, dv_ref, dr_ref, dg_ref, dpu_ref, w_ref, dep_ref, o_ref):
        dv, dr, dg = dv_ref[...], dr_ref[...], dg_ref[...]
        head, tail = slice(0, GW), slice(GW, PAD_IN)
        o_ref[...] = (_dot_nt(dq_ref[...], w_ref[0, :, 0:KW]) + _dot_nt(dk_ref[...], w_ref[0, :, KW:GW])
                      + _dot_nt(dv[:, 0:128], w_ref[0, :, tail])
                      + _dot_nt(dv, w_ref[1, :, head]) + _dot_nt(dr[:, 0:128], w_ref[1, :, tail])
                      + _dot_nt(dr, w_ref[2, :, head]) + _dot_nt(dg, w_ref[2, :, tail])
                      + _dot_nt(dpu_ref[...], w_ref[3, :, head]) + _dot_nt(dg, w_ref[3, :, tail]))

    tn = 512
    return pl.pallas_call(
        body, name="in_grad", grid=(D // tn,),
        in_specs=[VMEM_FULL] * 6 + [pl.BlockSpec((N_CHIP, tn, PAD_IN), lambda j: (0, j, 0)), ANY],
        out_specs=pl.BlockSpec((TP, tn), lambda j: (0, j)),
        out_shape=_sds((TP, D), F32),
        compiler_params=_cp(("arbitrary",)),
    )(dq, dk, dv, dr, dglr, dpu, wg, _dep(dep))


def _grad_win(u, dq, dk, dv, dr, dglr, dpu, dep=None):
    tm = 512

    def body(u_ref, dq_hbm, dk_hbm, dv_hbm, dr_hbm, dg_hbm, dpu_hbm, dep_ref, o_ref, dp_ref, sem):
        k, m = pl.program_id(0), pl.program_id(1)
        head, tail = slice(0, GW), slice(GW, PAD_IN)
        pieces = [[(dq_hbm, slice(0, KW)), (dk_hbm, slice(KW, GW)), (dv_hbm.at[:, 0:128], tail)],
                  [(dv_hbm, head), (dr_hbm.at[:, 0:128], tail)],
                  [(dr_hbm, head), (dg_hbm, tail)],
                  [(dpu_hbm, head), (dg_hbm, tail)]]

        def copies(kk):
            return [pltpu.make_async_copy(src, dp_ref.at[kk % 2, :, cols], sem.at[kk % 2, i])
                    for i, (src, cols) in enumerate(pieces[kk])]

        @pl.when((k == 0) & (m == 0))
        def _():
            for cp in copies(0):
                cp.start()

        for kk in range(N_CHIP):
            @pl.when((k == kk) & (m == 0))
            def _(kk=kk):
                for cp in copies(kk):
                    cp.wait()
                if kk + 1 < N_CHIP:
                    for cp in copies(kk + 1):
                        cp.start()

        g = _dot_tn(u_ref[...], dp_ref[k % 2])
        lane = lax.broadcasted_iota(jnp.int32, (tm, PAD_IN), 1)
        for kk in range(N_CHIP):
            @pl.when(k == kk)
            def _(kk=kk):
                if kk == 0:
                    nat = g
                elif kk < 3:
                    nat = pltpu.roll(g, PAD_IN - 4 * kk, 1)
                else:
                    nat = jnp.where(lane < 4, pltpu.roll(g, PAD_IN - (GW + 12), 1), pltpu.roll(g, 4, 1))
                o_ref[0] = nat[:, 0:SHARD_IN]

    return pl.pallas_call(
        body, name="grad_win", grid=(N_CHIP, D // tm),
        in_specs=[pl.BlockSpec((TP, tm), lambda k, m: (0, m))] + [ANY] * 7,
        out_specs=pl.BlockSpec((1, tm, SHARD_IN), lambda k, m: (k, m, 0)),
        out_shape=_sds((N_CHIP, D, SHARD_IN), F32),
        scratch_shapes=[pltpu.VMEM((2, TP, PAD_IN), BF16), pltpu.SemaphoreType.DMA((2, 3))],
        compiler_params=_cp(("arbitrary", "arbitrary")),
    )(u, dq, dk, dv, dr, dglr, dpu, _dep(dep))


def _split3(x):
    hi = x.astype(BF16)
    r1 = x - hi.astype(F32)
    mid = r1.astype(BF16)
    lo = (r1 - mid.astype(F32)).astype(BF16)
    return hi, mid, lo


def _tri_sum(tri, x):
    hi, mid, lo = _split3(x)
    return _dot(tri, hi) + _dot(tri, mid) + _dot(tri, lo)


def _gla_common(n, glr, gw2, gb):
    rows = n * CH + lax.broadcasted_iota(jnp.int32, (CH, 1), 0)
    valid = (rows >= ROW_LO) & (rows < ROW_HI)
    g_raw = _dot(glr.astype(BF16), gw2.astype(BF16)) + gb
    logsig = jnp.minimum(g_raw, 0.0) - jnp.log(1.0 + jnp.exp(-jnp.abs(g_raw)))
    logg = jnp.where(valid, logsig * (1.0 / TAU), 0.0)
    ci = lax.broadcasted_iota(jnp.int32, (CH, CH), 0)
    si = lax.broadcasted_iota(jnp.int32, (CH, CH), 1)
    lower = ci >= si
    G = _tri_sum(lower.astype(BF16), logg)
    Gl = G[CH - 1:CH, :]
    return valid, g_raw, lower, G, Gl


GSUB = 2


def _p_specs(index):
    def spec(width, shard, col):
        return pl.BlockSpec((1, GSUB * CH, width), lambda s: (shard, index(s), col))

    return [spec(KW, 0, 0), spec(KW, 0, 1), spec(GW, 1, 0), spec(128, 0, 8), spec(GW, 2, 0), spec(128, 1, 8),
            spec(128, 2, 8), spec(128, 3, 8)]


def _p_load(q_ref, k_ref, vm_ref, vh_ref, rm_ref, rh_ref, ga_ref, gb_ref):
    def joined(main, head):
        return jnp.concatenate([main[:, 0:128] + head, main[:, 128:]], axis=1)

    return q_ref[0], k_ref[0], joined(vm_ref[0], vh_ref[0]), joined(rm_ref[0], rh_ref[0]), ga_ref[0] + gb_ref[0]


def _gla_fwd(P, gw2, gb, gnw, dep=None):
    scale = DK ** -0.5

    def body(p0, p1, p2, p3, p4, p5, p6, p7, gw2_ref, gb_ref, gnw_ref, dep_ref, o_ref, og_ref, sp_ref, st_ref):
        n = pl.program_id(0)

        @pl.when(n == 0)
        def _():
            st_ref[...] = jnp.zeros_like(st_ref)

        q_blk, k_blk, v_blk, r_blk, glr_blk = _p_load(p0, p1, p2, p3, p4, p5, p6, p7)
        gnw_v = gnw_ref[...]
        for sub in range(GSUB):
            rows = slice(sub * CH, (sub + 1) * CH)
            _, _, lower, G, Gl = _gla_common(GSUB * n + sub, glr_blk[rows], gw2_ref[...], gb_ref[...])
            eG = jnp.exp(G)
            eN = jnp.exp(-G)
            eE = jnp.exp(Gl - G)
            dec = jnp.exp(Gl)
            for h in range(HEADS):
                ks = slice(h * DK, (h + 1) * DK)
                vs = slice(h * DV, (h + 1) * DV)
                kh = k_blk[rows, ks]
                vh = v_blk[rows, vs].astype(BF16)
                qd = ((q_blk[rows, ks] * scale) * eG[:, ks]).astype(BF16)
                ki = (kh * eN[:, ks]).astype(BF16)
                ke = (kh * eE[:, ks]).astype(BF16)
                st = st_ref[h]
                a = jnp.where(lower, _dot_nt(qd, ki), 0.0).astype(BF16)
                o = _dot(a, vh) + _dot_nt(qd, st.astype(BF16))
                sp_ref[sub, h] = st
                st_ref[h] = st * dec[:, ks] + _dot_tn(vh, ke)
                o_ref[rows, vs] = o
                rs = lax.rsqrt(jnp.mean(o * o, axis=-1, keepdims=True) + EPS)
                rv = r_blk[rows, vs]
                gate = rv / (1.0 + jnp.exp(-rv))
                og_ref[rows, vs] = (((o * rs) * gnw_v) * gate).astype(BF16)

    rv_ = pl.BlockSpec((GSUB * CH, GW), lambda n: (n, 0))

    def full(shape):
        return pl.BlockSpec(shape, lambda n: tuple(0 for _ in shape))

    return pl.pallas_call(
        body, name="gla_fwd", grid=(NCH // GSUB,),
        in_specs=_p_specs(lambda n: n) + [full((128, KW)), full((1, KW)), full((1, DV)), ANY],
        out_specs=[rv_, rv_, pl.BlockSpec((GSUB, HEADS, DV, DK), lambda n: (n, 0, 0, 0))],
        out_shape=[_sds((TP, GW), F32), _sds((TP, GW), BF16), _sds((NCH, HEADS, DV, DK), F32)],
        scratch_shapes=[pltpu.VMEM((HEADS, DV, DK), F32)],
        compiler_params=_cp(("arbitrary",)),
    )(*([P] * 8), gw2, gb, gnw, _dep(dep))


def _gla_bwd(dog, o, P, gw2, gb, gnw, sp, dep=None):
    scale = DK ** -0.5

    def body(dog_ref, o_ref, p0, p1, p2, p3, p4, p5, p6, p7, gw2_ref, gb_ref, gnw_ref, sp_ref, dep_ref,
             dq_ref, dk_ref, dv_ref, dr_ref, dglr_ref, dgw2_ref, dgb_ref, dgnw_ref, ds_ref):
        step = pl.program_id(0)
        blk = NCH // GSUB - 1 - step

        @pl.when(step == 0)
        def _():
            ds_ref[...] = jnp.zeros_like(ds_ref)
            dgw2_ref[...] = jnp.zeros_like(dgw2_ref)
            dgb_ref[...] = jnp.zeros_like(dgb_ref)
            dgnw_ref[...] = jnp.zeros_like(dgnw_ref)

        q_blk, k_blk, v_blk, r_blk, glr_blk = _p_load(p0, p1, p2, p3, p4, p5, p6, p7)
        gw2_b = gw2_ref[...].astype(BF16)
        upper = lax.broadcasted_iota(jnp.int32, (CH, CH), 0) <= lax.broadcasted_iota(jnp.int32, (CH, CH), 1)
        gnw_v = gnw_ref[...]
        last = lax.broadcasted_iota(jnp.int32, (CH, 1), 0) == CH - 1
        for sub in reversed(range(GSUB)):
            rows = slice(sub * CH, (sub + 1) * CH)
            glr_v = glr_blk[rows]
            valid, g_raw, lower, G, Gl = _gla_common(GSUB * blk + sub, glr_v, gw2_ref[...], gb_ref[...])
            eG = jnp.exp(G)
            eN = jnp.exp(-G)
            eE = jnp.exp(Gl - G)
            dec = jnp.exp(Gl)
            dgnw_acc = jnp.zeros((1, DV), F32)
            dG_parts = []
            for h in range(HEADS):
                ks = slice(h * DK, (h + 1) * DK)
                vs = slice(h * DV, (h + 1) * DV)
                oh = o_ref[rows, vs]
                rv = r_blk[rows, vs]
                dg = dog_ref[rows, vs]
                sig = 1.0 / (1.0 + jnp.exp(-rv))
                gate = rv * sig
                rs = lax.rsqrt(jnp.mean(oh * oh, axis=-1, keepdims=True) + EPS)
                ohat = oh * rs
                dr_ref[rows, vs] = ((dg * (ohat * gnw_v)) * (sig * (1.0 + rv * (1.0 - sig)))).astype(BF16)
                don = dg * gate
                dgnw_acc = dgnw_acc + jnp.sum(don * ohat, axis=0, keepdims=True)
                gxn = don * gnw_v
                do = (rs * (gxn - ohat * jnp.mean(gxn * ohat, axis=-1, keepdims=True))).astype(BF16)
                kh = k_blk[rows, ks]
                vh = v_blk[rows, vs].astype(BF16)
                qd_f = (q_blk[rows, ks] * scale) * eG[:, ks]
                ki_f = kh * eN[:, ks]
                ke_f = kh * eE[:, ks]
                qd, ki, ke = qd_f.astype(BF16), ki_f.astype(BF16), ke_f.astype(BF16)
                spt = sp_ref[sub, h]
                dst = ds_ref[h]
                dst_b = dst.astype(BF16)
                a_t = jnp.where(upper, _dot_nt(ki, qd), 0.0).astype(BF16)
                da = jnp.where(lower, _dot_nt(do, vh), 0.0).astype(BF16)
                da_t = jnp.where(upper, _dot_nt(vh, do), 0.0).astype(BF16)
                dv_ref[rows, vs] = (_dot(a_t, do) + _dot_nt(ke, dst_b)).astype(BF16)
                dqd = _dot(da, ki) + _dot(do, spt.astype(BF16))
                dki = _dot(da_t, qd)
                dke = _dot(vh, dst_b)
                ddec = jnp.sum(spt * dst, axis=0, keepdims=True)
                ds_ref[h] = dst * dec[:, ks] + _dot_tn(do, qd)
                dq_ref[rows, ks] = ((dqd * eG[:, ks]) * scale).astype(BF16)
                dk_ref[rows, ks] = (dki * eN[:, ks] + dke * eE[:, ks]).astype(BF16)
                dke_ke = dke * ke_f
                dG = dqd * qd_f - dki * ki_f - dke_ke
                dGl = jnp.sum(dke_ke, axis=0, keepdims=True) + ddec * dec[:, ks]
                dG_parts.append(dG + jnp.where(last, dGl, 0.0))
            dgnw_ref[...] += dgnw_acc
            dG_all = jnp.concatenate(dG_parts, axis=1)
            dlogg = jnp.where(valid, _tri_sum(upper.astype(BF16), dG_all), 0.0)
            dg_raw = (dlogg * (1.0 / TAU)) * (1.0 / (1.0 + jnp.exp(g_raw)))
            dgb_ref[...] += jnp.sum(dg_raw, axis=0, keepdims=True)
            dg_b = dg_raw.astype(BF16)
            dgw2_ref[...] += _dot_tn(glr_v.astype(BF16), dg_b)
            dglr_ref[rows, :] = _dot_nt(dg_b, gw2_b).astype(BF16)

    def back(s):
        return NCH // GSUB - 1 - s

    rk = pl.BlockSpec((GSUB * CH, KW), lambda s: (back(s), 0))
    rv_ = pl.BlockSpec((GSUB * CH, GW), lambda s: (back(s), 0))
    rg = pl.BlockSpec((GSUB * CH, 128), lambda s: (back(s), 0))

    def full(shape):
        return pl.BlockSpec(shape, lambda s: tuple(0 for _ in shape))

    return pl.pallas_call(
        body, name="gla_bwd", grid=(NCH // GSUB,),
        in_specs=[rv_, rv_] + _p_specs(back) + [full((128, KW)), full((1, KW)), full((1, DV)),
                  pl.BlockSpec((GSUB, HEADS, DV, DK), lambda s: (back(s), 0, 0, 0)), ANY],
        out_specs=[rk, rk, rv_, rv_, rg, full((128, KW)), full((1, KW)), full((1, DV))],
        out_shape=[_sds((TP, KW), BF16), _sds((TP, KW), BF16), _sds((TP, GW), BF16), _sds((TP, GW), BF16),
                   _sds((TP, 128), BF16), _sds((128, KW), F32), _sds((1, KW), F32), _sds((1, DV), F32)],
        scratch_shapes=[pltpu.VMEM((HEADS, DV, DK), F32)],
        compiler_params=_cp(("arbitrary",)),
    )(dog, o, *([P] * 8), gw2, gb, gnw, sp, _dep(dep))


POOL_TR = 128
HALO = 16


def _pool_counts(base, nrows):
    rows = base + lax.broadcasted_iota(jnp.int32, (nrows, 1), 0)
    valid = (rows >= ROW_LO) & (rows < ROW_HI)
    t1 = (rows - ROW_LO + 1).astype(F32)
    cnts = [jnp.clip(t1, 1.0, float(w)) for w in WINDOWS]
    return valid, cnts


def _pool_fwd(P, pw, ps, dep=None):
    def body(cur_ref, prev_ref, pw_ref, ps_ref, dep_ref, y_ref, op_ref):
        i = pl.program_id(0)
        cur = cur_ref[0]
        full = jnp.concatenate([prev_ref[0], cur], axis=0)
        s2 = full + pltpu.roll(full, 1, 0)
        s4 = s2 + pltpu.roll(s2, 2, 0)
        s8 = s4 + pltpu.roll(s4, 4, 0)
        s16 = s8 + pltpu.roll(s8, 8, 0)
        valid, cnts = _pool_counts(i * POOL_TR, POOL_TR)
        for g, s in enumerate((s2, s4, s8, s16)):
            cs = slice(g * GC, (g + 1) * GC)
            y = s[HALO:, cs] / cnts[g] - cur[:, cs]
            yb = jnp.where(valid, y, 0.0).astype(BF16)
            y_ref[:, cs] = yb
            op_ref[:, cs] = (_dot(yb, pw_ref[g].astype(BF16)) * ps_ref[:, cs]).astype(BF16)

    row = pl.BlockSpec((POOL_TR, PW), lambda i: (i, 0))
    per = POOL_TR // HALO
    return pl.pallas_call(
        body, name="pool_fwd", grid=(TP // POOL_TR,),
        in_specs=[pl.BlockSpec((1, POOL_TR, PW), lambda i: (3, i, 0)),
                  pl.BlockSpec((1, HALO, PW), lambda i: (3, jnp.maximum(i * per - 1, 0), 0)),
                  pl.BlockSpec((4, GC, GC), lambda i: (0, 0, 0)), pl.BlockSpec((1, PW), lambda i: (0, 0)), ANY],
        out_specs=[row, row],
        out_shape=[_sds((TP, PW), BF16), _sds((TP, PW), BF16)],
        compiler_params=_cp(("arbitrary",)),
    )(P, P, pw, ps, _dep(dep))


def _pool_bwd(dop, y, pw, ps, dep=None):
    nblk = TP // HALO

    def body(cur_ref, nxt_ref, y_ref, pw_ref, ps_ref, dep_ref, dpu_ref, dpw_ref, dps_ref):
        i = pl.program_id(0)

        @pl.when(i == 0)
        def _():
            dpw_ref[...] = jnp.zeros_like(dpw_ref)
            dps_ref[...] = jnp.zeros_like(dps_ref)

        n_all = POOL_TR + HALO
        dcur = cur_ref[...]
        dall = jnp.concatenate([dcur, nxt_ref[...]], axis=0)
        valid, cnts = _pool_counts(i * POOL_TR, n_all)
        for g in range(4):
            cs = slice(g * GC, (g + 1) * GC)
            pwb = pw_ref[g].astype(BF16)
            yb = y_ref[:, cs]
            dyw = (dall[:, cs] * ps_ref[:, cs]).astype(BF16)
            dps_ref[:, cs] += jnp.sum(dcur[:, cs] * _dot(yb, pwb), axis=0, keepdims=True)
            dpw_ref[g] += _dot_tn(yb, dyw[0:POOL_TR, :])
            dyv = jnp.where(valid, _dot_nt(dyw, pwb), 0.0)
            e = dyv / cnts[g]
            w = WINDOWS[g]
            sh = 1
            while sh < w:
                e = e + pltpu.roll(e, n_all - sh, 0)
                sh *= 2
            dpu_ref[:, cs] = (e[0:POOL_TR, :] - dyv[0:POOL_TR, :]).astype(BF16)

    row = pl.BlockSpec((POOL_TR, PW), lambda i: (i, 0))
    per = POOL_TR // HALO
    return pl.pallas_call(
        body, name="pool_bwd", grid=(TP // POOL_TR,),
        in_specs=[pl.BlockSpec((POOL_TR, PW), lambda i: (i, 1)),
                  pl.BlockSpec((HALO, PW), lambda i: (jnp.minimum(i * per + per, nblk - 1), 1)),
                  row, pl.BlockSpec((4, GC, GC), lambda i: (0, 0, 0)), pl.BlockSpec((1, PW), lambda i: (0, 0)), ANY],
        out_specs=[row, pl.BlockSpec((4, GC, GC), lambda i: (0, 0, 0)), pl.BlockSpec((1, PW), lambda i: (0, 0))],
        out_shape=[_sds((TP, PW), BF16), _sds((4, GC, GC), F32), _sds((1, PW), F32)],
        compiler_params=_cp(("arbitrary",)),
    )(dop, dop, y, pw, ps, _dep(dep))


def _place():
    x, y, c = lax.axis_index("x"), lax.axis_index("y"), lax.axis_index("c")
    chips = [(1 - x, y), (x, 1 - y), (1 - x, 1 - y)]
    return x, y, c, chips


HBM = pl.BlockSpec(memory_space=pltpu.HBM)
SEM = pl.BlockSpec(memory_space=pltpu.SEMAPHORE)
EFFECT = pltpu.SideEffectType.DATAFLOW_SIDE_EFFECTING


def _cast_into(w, place, cols_out, name, dep=None):
    rows, cols = w.shape
    tr = 512

    def body(p_ref, w_ref, dep_ref, o_ref):
        if cols_out != cols:
            o_ref[0] = jnp.zeros((tr, cols_out), BF16)
            o_ref[0, :, 0:cols] = w_ref[...].astype(BF16)
        else:
            o_ref[0] = w_ref[...].astype(BF16)

    grid_spec = pltpu.PrefetchScalarGridSpec(
        num_scalar_prefetch=1, grid=(rows // tr,),
        in_specs=[pl.BlockSpec((tr, cols), lambda i, p: (i, 0)), ANY],
        out_specs=pl.BlockSpec((1, tr, cols_out), lambda i, p: (p[0], i, 0)))
    return pl.pallas_call(
        body, name=name, grid_spec=grid_spec,
        out_shape=_sds((N_CHIP, rows, cols_out), BF16),
        compiler_params=_cp(("arbitrary",)),
    )(place, w, _dep(dep))


def _cast_win(w, place, dep=None):
    rows, cols = w.shape
    tr = 512

    def body(p_ref, w_ref, dep_ref, o_ref, t_ref):
        t_ref[...] = jnp.zeros_like(t_ref)
        t_ref[:, 0:cols] = w_ref[...]
        t = t_ref[...]
        lane = lax.broadcasted_iota(jnp.int32, (tr, PAD_IN), 1)
        for kk in range(N_CHIP):
            @pl.when(p_ref[0] == kk)
            def _(kk=kk):
                if kk == 0:
                    placed = t
                elif kk < 3:
                    placed = pltpu.roll(t, 4 * kk, 1)
                else:
                    pool = pltpu.roll(t, PAD_IN - 4, 1)
                    gate = pltpu.roll(t, GW + 12, 1)
                    placed = jnp.where(lane < GW, pool, jnp.where((lane >= GW + 12) & (lane < GW + 16), gate, 0.0))
                o_ref[0] = placed.astype(BF16)

    grid_spec = pltpu.PrefetchScalarGridSpec(
        num_scalar_prefetch=1, grid=(rows // tr,),
        in_specs=[pl.BlockSpec((tr, cols), lambda i, p: (i, 0)), ANY],
        out_specs=pl.BlockSpec((1, tr, PAD_IN), lambda i, p: (p[0], i, 0)),
        scratch_shapes=[pltpu.VMEM((tr, PAD_IN), F32)])
    return pl.pallas_call(
        body, name="cast_win", grid_spec=grid_spec,
        out_shape=_sds((N_CHIP, rows, PAD_IN), BF16),
        compiler_params=_cp(("arbitrary",)),
    )(place, w, _dep(dep))


def _half_rows(ref, k, which):
    h = ref.shape[1] // 2
    return ref.at[k, pl.ds(pl.multiple_of(which * h, 8), h), :]


def _sent_rows(ref, k, which, whole):
    return ref.at[k] if whole else _half_rows(ref, k, which)


def _gather_start(ws, name, whole=None):
    n = len(ws)
    whole = whole or [False] * n

    def body(*refs):
        ins = refs[:n]
        ssems = refs[n:2 * n]
        rsems = refs[2 * n:3 * n]
        token = refs[4 * n]
        x, y, c, chips = _place()
        me = 2 * x + y
        for w in range(n):
            blk = _sent_rows(ins[w], me, c, whole[w])
            for j, chip in enumerate(chips):
                pltpu.make_async_remote_copy(src_ref=blk, dst_ref=blk, send_sem=ssems[w].at[j], recv_sem=rsems[w].at[j],
                                             device_id=(*chip, c), device_id_type=MESH).start()
        token[...] = jnp.zeros_like(token)

    sem3 = pltpu.SemaphoreType.DMA((3,))
    outs = pl.pallas_call(
        body, name=name,
        out_shape=tuple([sem3] * (2 * n) + [pltpu.HBM(w.shape, w.dtype) for w in ws] + [_sds((8, 128), F32)]),
        in_specs=(HBM,) * n, out_specs=(SEM,) * (2 * n) + (HBM,) * n + (VMEM_FULL,),
        input_output_aliases={w: 2 * n + w for w in range(n)},
        compiler_params=pltpu.CompilerParams(has_side_effects=EFFECT),
    )(*[pltpu.with_memory_space_constraint(w, pltpu.HBM) for w in ws])
    return outs[:n], outs[n:2 * n], outs[2 * n:3 * n], outs[3 * n]


def _gather_wait(w, ssem, rsem, after, name, whole=False):
    def body(w_ref, ssem_ref, rsem_ref, after_ref, out_ref):
        x, y, c, chips = _place()
        me = 2 * x + y
        mine = _sent_rows(w_ref, me, c, whole)
        for j, (cx, cy) in enumerate(chips):
            cp = pltpu.make_async_remote_copy(src_ref=mine, dst_ref=_sent_rows(w_ref, 2 * cx + cy, c, whole),
                                              send_sem=ssem_ref.at[j], recv_sem=rsem_ref.at[j],
                                              device_id=(cx, cy, c), device_id_type=MESH)
            cp.wait_send()
            cp.wait_recv()

    return pl.pallas_call(
        body, name=name, out_shape=pltpu.HBM(w.shape, w.dtype),
        in_specs=(HBM, SEM, SEM, ANY), out_specs=HBM, input_output_aliases={0: 0},
        compiler_params=pltpu.CompilerParams(has_side_effects=EFFECT),
    )(w, ssem, rsem, after)


def _gather_copies(ref, kind, ssem, rsem):
    x, y, c, _ = _place()
    xn, yn, sib = (1 - x, y, c), (x, 1 - y, c), (x, y, 1 - c)
    kx, ky, kd = 2 * (1 - x) + y, 2 * x + (1 - y), 2 * (1 - x) + (1 - y)
    half = ref.shape[1] // 2
    quarter = half // 2

    def piece(k, q):
        return ref.at[k, pl.ds(pl.multiple_of(c * half + q * quarter, 8), quarter), :]

    if kind == "d":
        blk = _half_rows(ref, 2 * x + y, c)
        pairs = [(blk, xn), (blk, yn)]
    elif kind == "r":
        pairs = [(piece(ky, 1), xn), (piece(kx, 0), yn)]
    elif kind == "fx":
        pairs = [(_half_rows(ref, kx, c), sib), (_half_rows(ref, ky, c), sib)]
    else:
        pairs = [(_half_rows(ref, kd, c), sib)]
    return [pltpu.make_async_remote_copy(src_ref=blk, dst_ref=blk, send_sem=ssem.at[i], recv_sem=rsem.at[i],
                                         device_id=to, device_id_type=MESH) for i, (blk, to) in enumerate(pairs)]


def _gather_step(name, arrs, waits, starts, sems_in=(), after=()):
    n, nw, ns = len(arrs), len(waits), len(starts)
    after = [a for a in after if a is not None] or [_dep(None)]

    def body(*refs):
        a_in = refs[:n]
        s_in = refs[n:n + 2 * nw]
        outs = refs[n + 2 * nw + len(after):]
        s_out = outs[:2 * ns]
        for i, (ai, kind) in enumerate(waits):
            for cp in _gather_copies(a_in[ai], kind, s_in[2 * i], s_in[2 * i + 1]):
                cp.wait_send()
                cp.wait_recv()
        for i, (ai, kind) in enumerate(starts):
            for cp in _gather_copies(a_in[ai], kind, s_out[2 * i], s_out[2 * i + 1]):
                cp.start()
        if ns:
            token = outs[2 * ns + n]
            token[...] = jnp.zeros_like(token)

    sem2 = pltpu.SemaphoreType.DMA((2,))
    flat_in = [s for pair in sems_in for s in pair]
    arrs = [pltpu.with_memory_space_constraint(a, pltpu.HBM) for a in arrs]
    outs = pl.pallas_call(
        body, name=name,
        out_shape=tuple([sem2] * (2 * ns) + [pltpu.HBM(a.shape, a.dtype) for a in arrs]
                        + ([_sds((8, 128), F32)] if ns else [])),
        in_specs=(HBM,) * n + (SEM,) * (2 * nw) + (ANY,) * len(after),
        out_specs=(SEM,) * (2 * ns) + (HBM,) * n + ((VMEM_FULL,) if ns else ()),
        input_output_aliases={i: 2 * ns + i for i in range(n)},
        compiler_params=pltpu.CompilerParams(has_side_effects=EFFECT),
    )(*arrs, *flat_in, *after)
    sems = [(outs[2 * i], outs[2 * i + 1]) for i in range(ns)]
    return sems, list(outs[2 * ns:2 * ns + n]), (outs[2 * ns + n] if ns else None)


def _rs_start(sb, name, after=None):
    _, half, cols = sb.shape

    def body(sb_ref, land_ref, after_ref, ssem, rsem, sb_out, land_out, token):
        x, y, c, chips = _place()
        for j, (cx, cy) in enumerate(chips):
            pltpu.make_async_remote_copy(src_ref=sb_ref.at[2 * cx + cy], dst_ref=land_ref.at[j], send_sem=ssem.at[j],
                                         recv_sem=rsem.at[j], device_id=(cx, cy, c), device_id_type=MESH).start()
        token[...] = jnp.zeros_like(token)

    sem3 = pltpu.SemaphoreType.DMA((3,))
    land = lax.empty((3, half, cols), BF16)
    return pl.pallas_call(
        body, name=name,
        out_shape=(sem3, sem3, pltpu.HBM(sb.shape, sb.dtype), pltpu.HBM(land.shape, land.dtype), _sds((8, 128), F32)),
        in_specs=(HBM, HBM, ANY), out_specs=(SEM, SEM, HBM, HBM, VMEM_FULL), input_output_aliases={0: 2, 1: 3},
        compiler_params=pltpu.CompilerParams(has_side_effects=EFFECT),
    )(pltpu.with_memory_space_constraint(sb, pltpu.HBM), pltpu.with_memory_space_constraint(land, pltpu.HBM), _dep(after))


def _rs_wait(items, after, name):
    n = len(items)

    def body(*refs):
        x, y, c, chips = _place()
        for i in range(n):
            sb_ref, land_ref, ssem_ref, rsem_ref = refs[4 * i:4 * i + 4]
            for j, (cx, cy) in enumerate(chips):
                cp = pltpu.make_async_remote_copy(src_ref=sb_ref.at[2 * cx + cy], dst_ref=land_ref.at[j],
                                                  send_sem=ssem_ref.at[j], recv_sem=rsem_ref.at[j],
                                                  device_id=(cx, cy, c), device_id_type=MESH)
                cp.wait_send()
                cp.wait_recv()

    outs = pl.pallas_call(
        body, name=name,
        out_shape=tuple(pltpu.HBM(a.shape, a.dtype) for it in items for a in it[:2]),
        in_specs=(HBM, HBM, SEM, SEM) * n + (ANY,), out_specs=(HBM,) * (2 * n),
        input_output_aliases={4 * i + k: 2 * i + k for i in range(n) for k in range(2)},
        compiler_params=pltpu.CompilerParams(has_side_effects=EFFECT),
    )(*[a for it in items for a in it], after)
    return [outs[2 * i + 1] for i in range(n)]


def _pair_copy(g_ref, land_ref, ssem, rsem):
    x, y, c, _ = _place()
    h = g_ref.shape[1] // 2
    src = g_ref.at[:, pl.ds(pl.multiple_of((1 - c) * h, 8), h), :]
    return pltpu.make_async_remote_copy(src_ref=src, dst_ref=land_ref, send_sem=ssem.at[0], recv_sem=rsem.at[0],
                                        device_id=(x, y, 1 - c), device_id_type=MESH)


def _pair_start(g, name):
    def body(g_ref, land_ref, ssem, rsem, g_out, land_out, token):
        _pair_copy(g_ref, land_ref, ssem, rsem).start()
        token[...] = jnp.zeros_like(token)

    sem1 = pltpu.SemaphoreType.DMA((1,))
    land = lax.empty((N_CHIP, g.shape[1] // 2, g.shape[2]), F32)
    return pl.pallas_call(
        body, name=name,
        out_shape=(sem1, sem1, pltpu.HBM(g.shape, g.dtype), pltpu.HBM(land.shape, land.dtype), _sds((8, 128), F32)),
        in_specs=(HBM, HBM), out_specs=(SEM, SEM, HBM, HBM, VMEM_FULL), input_output_aliases={0: 2, 1: 3},
        compiler_params=pltpu.CompilerParams(has_side_effects=EFFECT),
    )(pltpu.with_memory_space_constraint(g, pltpu.HBM), pltpu.with_memory_space_constraint(land, pltpu.HBM))


def _pair_wait(g, land, ssem, rsem, after, name):
    def body(g_ref, land_ref, ssem_ref, rsem_ref, after_ref, g_out, land_out):
        cp = _pair_copy(g_ref, land_ref, ssem_ref, rsem_ref)
        cp.wait_send()
        cp.wait_recv()

    return pl.pallas_call(
        body, name=name,
        out_shape=(pltpu.HBM(g.shape, g.dtype), pltpu.HBM(land.shape, land.dtype)),
        in_specs=(HBM, HBM, SEM, SEM, ANY), out_specs=(HBM, HBM), input_output_aliases={0: 0, 1: 1},
        compiler_params=pltpu.CompilerParams(has_side_effects=EFFECT),
    )(g, land, ssem, rsem, after)


def _pair_sum(g, rcv, place, name):
    _, rows, cols = g.shape
    half = rows // 2
    tr = min(512, half)
    nt = half // tr

    def body(p_ref, g_ref, r_ref, sb_ref, sf_ref):
        s = pl.program_id(1)
        tot = g_ref[0] + r_ref[0]
        sb_ref[0] = tot.astype(BF16)

        @pl.when(s == p_ref[0])
        def _():
            sf_ref[...] = tot

    grid_spec = pltpu.PrefetchScalarGridSpec(
        num_scalar_prefetch=1, grid=(nt, N_CHIP),
        in_specs=[pl.BlockSpec((1, tr, cols), lambda t, s, p: (s, p[1] * nt + t, 0)),
                  pl.BlockSpec((1, tr, cols), lambda t, s, p: (s, t, 0))],
        out_specs=[pl.BlockSpec((1, tr, cols), lambda t, s, p: (s, t, 0)),
                   pl.BlockSpec((tr, cols), lambda t, s, p: (t, 0))])
    return pl.pallas_call(
        body, name=name, grid_spec=grid_spec,
        out_shape=[_sds((N_CHIP, half, cols), BF16), _sds((half, cols), F32)],
        compiler_params=_cp(("arbitrary", "arbitrary")),
    )(place, g, rcv)


def _final_sum(sf, rb, place, name):
    half, cols = sf.shape
    tr = min(512, half)
    nt = half // tr

    def body(p_ref, sf_ref, r_ref, out_ref):
        acc = sf_ref[...]
        for j in range(3):
            acc = acc + r_ref[j].astype(F32)
        out_ref[...] = acc

    grid_spec = pltpu.PrefetchScalarGridSpec(
        num_scalar_prefetch=1, grid=(nt,),
        in_specs=[pl.BlockSpec((tr, cols), lambda t, p: (t, 0)), pl.BlockSpec((3, tr, cols), lambda t, p: (0, t, 0))],
        out_specs=pl.BlockSpec((tr, cols), lambda t, p: (p[1] * nt + t, 0)))
    return pl.pallas_call(
        body, name=name, grid_spec=grid_spec,
        out_shape=_sds((2 * half, cols), F32),
        compiler_params=_cp(("arbitrary",)),
    )(place, sf, rb)


def _half_copy(f_ref, which, ssem, rsem):
    x, y, c, _ = _place()
    h = f_ref.shape[0] // 2
    rows = f_ref.at[pl.ds(pl.multiple_of(which * h, 8), h), :]
    return pltpu.make_async_remote_copy(src_ref=rows, dst_ref=rows, send_sem=ssem.at[0], recv_sem=rsem.at[0],
                                        device_id=(x, y, 1 - c), device_id_type=MESH)


def _half_start(fulls, name, after=None):
    n = len(fulls)

    def body(*refs):
        for i in range(n):
            _half_copy(refs[i], lax.axis_index("c"), refs[n + 1 + 2 * i], refs[n + 2 + 2 * i]).start()
        token = refs[4 * n + 1]
        token[...] = jnp.zeros_like(token)

    sem1 = pltpu.SemaphoreType.DMA((1,))
    outs = pl.pallas_call(
        body, name=name,
        out_shape=tuple([sem1] * (2 * n) + [pltpu.HBM(f.shape, f.dtype) for f in fulls] + [_sds((8, 128), F32)]),
        in_specs=(HBM,) * n + (ANY,), out_specs=(SEM,) * (2 * n) + (HBM,) * n + (VMEM_FULL,),
        input_output_aliases={i: 2 * n + i for i in range(n)},
        compiler_params=pltpu.CompilerParams(has_side_effects=EFFECT),
    )(*[pltpu.with_memory_space_constraint(f, pltpu.HBM) for f in fulls], _dep(after))
    return [(outs[2 * i], outs[2 * i + 1], outs[2 * n + i]) for i in range(n)], outs[3 * n]


def _half_wait(items, after, name):
    n = len(items)

    def body(*refs):
        c = lax.axis_index("c")
        for i in range(n):
            ssem_ref, rsem_ref, f_ref = refs[3 * i:3 * i + 3]
            _half_copy(f_ref, c, ssem_ref, rsem_ref).wait_send()
            _half_copy(f_ref, 1 - c, ssem_ref, rsem_ref).wait_recv()

    return pl.pallas_call(
        body, name=name, out_shape=tuple(pltpu.HBM(it[2].shape, it[2].dtype) for it in items),
        in_specs=(SEM, SEM, HBM) * n + (ANY,) * len(after), out_specs=(HBM,) * n,
        input_output_aliases={3 * i + 2: i for i in range(n)},
        compiler_params=pltpu.CompilerParams(has_side_effects=EFFECT),
    )(*[a for it in items for a in it], *after)


def _small_copies(src_ref, land_ref, ssem, rsem, first):
    x, y, c, chips = _place()
    if first:
        return [pltpu.make_async_remote_copy(src_ref=src_ref, dst_ref=land_ref, send_sem=ssem.at[0], recv_sem=rsem.at[0],
                                             device_id=(x, y, 1 - c), device_id_type=MESH)]
    return [pltpu.make_async_remote_copy(src_ref=src_ref, dst_ref=land_ref.at[j], send_sem=ssem.at[j], recv_sem=rsem.at[j],
                                         device_id=(*chip, c), device_id_type=MESH) for j, chip in enumerate(chips)]


def _small_start(src, first, name, after=None):
    n = 1 if first else 3

    def body(src_ref, land_ref, after_ref, ssem, rsem, src_out, land_out, token):
        for cp in _small_copies(src_ref, land_ref, ssem, rsem, first):
            cp.start()
        token[...] = jnp.zeros_like(token)

    sems = pltpu.SemaphoreType.DMA((n,))
    land = lax.empty(src.shape if first else (3,) + src.shape, F32)
    return pl.pallas_call(
        body, name=name,
        out_shape=(sems, sems, pltpu.HBM(src.shape, F32), pltpu.HBM(land.shape, F32), _sds((8, 128), F32)),
        in_specs=(HBM, HBM, ANY), out_specs=(SEM, SEM, HBM, HBM, VMEM_FULL), input_output_aliases={0: 2, 1: 3},
        compiler_params=pltpu.CompilerParams(has_side_effects=EFFECT),
    )(pltpu.with_memory_space_constraint(src, pltpu.HBM), pltpu.with_memory_space_constraint(land, pltpu.HBM), _dep(after))


def _small_wait(src, land, ssem, rsem, first, after, name):
    def body(src_ref, land_ref, ssem_ref, rsem_ref, after_ref, src_out, land_out):
        for cp in _small_copies(src_ref, land_ref, ssem_ref, rsem_ref, first):
            cp.wait_send()
            cp.wait_recv()

    return pl.pallas_call(
        body, name=name,
        out_shape=(pltpu.HBM(src.shape, F32), pltpu.HBM(land.shape, F32)),
        in_specs=(HBM, HBM, SEM, SEM, ANY), out_specs=(HBM, HBM), input_output_aliases={0: 0, 1: 1},
        compiler_params=pltpu.CompilerParams(has_side_effects=EFFECT),
    )(src, land, ssem, rsem, after)


def _small_pair_sum(vec, got):
    def body(v_ref, g_ref, o_ref):
        o_ref[...] = v_ref[...] + g_ref[...]

    return pl.pallas_call(body, name="small_pair_sum", in_specs=[VMEM_FULL] * 2, out_specs=VMEM_FULL,
                          out_shape=_sds(vec.shape, F32), compiler_params=_cp())(vec, got)


def _small_chip_sum(pair, got, place):
    def body(p_ref, pair_ref, got_ref, o_ref):
        acc = None
        for kk in range(N_CHIP):
            d = jnp.bitwise_xor(p_ref[0], kk)
            t = jnp.where(d == 0, pair_ref[...], jnp.where(d == 2, got_ref[0], jnp.where(d == 1, got_ref[1], got_ref[2])))
            acc = t if acc is None else acc + t
        o_ref[...] = acc

    grid_spec = pltpu.PrefetchScalarGridSpec(
        num_scalar_prefetch=1, grid=(1,),
        in_specs=[pl.BlockSpec(pair.shape, lambda i, p: (0, 0)), pl.BlockSpec(got.shape, lambda i, p: (0, 0, 0))],
        out_specs=pl.BlockSpec(pair.shape, lambda i, p: (0, 0)))
    return pl.pallas_call(body, name="small_chip_sum", grid_spec=grid_spec, out_shape=_sds(pair.shape, F32),
                          compiler_params=_cp(("arbitrary",)))(place, pair, got)


def _adam_math(w, g, m, v):
    m = B1 * m + (1.0 - B1) * g
    v = B2 * v + (1.0 - B2) * (g * g)
    m_hat = m / (1.0 - B1 ** STEP)
    v_hat = v / (1.0 - B2 ** STEP)
    delta = -LR * (m_hat / (jnp.sqrt(v_hat) + AEPS) + WD * w)
    return delta, m, v


def _adam_big(w, g, m, v, name, dep=None):
    rows, cols = w.shape
    tr = 256

    def body(w_ref, g_ref, m_ref, v_ref, dep_ref, go_ref, d_ref, nm_ref, nv_ref):
        g = g_ref[...]
        d, nm, nv = _adam_math(w_ref[...], g, m_ref[...], v_ref[...])
        go_ref[...] = g
        d_ref[...] = d
        nm_ref[...] = nm
        nv_ref[...] = nv

    blk = pl.BlockSpec((tr, cols), lambda i: (i, 0))
    return pl.pallas_call(
        body, name=name, grid=(rows // tr,),
        in_specs=[blk] * 4 + [ANY], out_specs=[blk] * 4, out_shape=[_sds((rows, cols), F32)] * 4,
        compiler_params=_cp(("arbitrary",)),
    )(w, g, m, v, _dep(dep))


def _adam_small(ws, gs, ms, vs, dep=None):
    n = len(ws)

    def body(*refs):
        for i in range(n):
            d, nm, nv = _adam_math(refs[i][...], refs[n + i][...], refs[2 * n + i][...], refs[3 * n + i][...])
            refs[4 * n + 1 + i][...] = d
            refs[5 * n + 1 + i][...] = nm
            refs[6 * n + 1 + i][...] = nv

    shapes = [_sds(w.shape, F32) for w in ws]
    outs = pl.pallas_call(
        body, name="adam_small",
        in_specs=[VMEM_FULL] * (4 * n) + [ANY], out_specs=[VMEM_FULL] * (3 * n), out_shape=shapes * 3,
        compiler_params=_cp(),
    )(*ws, *gs, *ms, *vs, _dep(dep))
    return outs[:n], outs[n:2 * n], outs[2 * n:]


def _pad_rows8(a):
    flat = a.reshape(-1, 128)
    pad = (-flat.shape[0]) % 8
    if pad:
        flat = jnp.concatenate([flat, jnp.zeros((pad, 128), F32)], axis=0)
    return flat


def kernel(x, meta_tokens, norm1_w, w_in, gate_w2, gate_b, gla_norm_w, pool_w, pool_scale, w_out, norm2_w, mlp_w1, mlp_w2, final_norm_w, loss_target, m_meta_tokens, m_norm1_w, m_w_in, m_gate_w2, m_gate_b, m_gla_norm_w, m_pool_w, m_pool_scale, m_w_out, m_norm2_w, m_mlp_w1, m_mlp_w2, m_final_norm_w, v_meta_tokens, v_norm1_w, v_w_in, v_gate_w2, v_gate_b, v_gla_norm_w, v_pool_w, v_pool_scale, v_w_out, v_norm2_w, v_mlp_w1, v_mlp_w2, v_final_norm_w):
    cx, cy, cc = lax.axis_index("x"), lax.axis_index("y"), lax.axis_index("c")
    me = (2 * cx + cy).astype(jnp.int32)

    place = jnp.stack([me, cc.astype(jnp.int32)])
    fw = final_norm_w.reshape(1, D)

    mine = jnp.concatenate([meta_tokens.reshape(64, 128), gate_w2[0], pool_w[0].reshape(512, 128)], axis=0)
    small = lax.dynamic_update_slice(jnp.zeros((N_CHIP, 592, 128), F32), mine[None], (me, 0, 0))
    (s_sm,), (r_sm,), (f_sm,), tok = _gather_start([small], "gather_start_small", [True])
    (sem_win_d,), (win,), tok = _gather_step("gather_start_win", [_cast_win(w_in[0], place, tok)], [], [(0, "d")])
    wout, w1, w2 = (_cast_into(w_out[0], place, D, "cast_wout", tok), _cast_into(mlp_w1[0], place, D, "cast_w1", tok),
                    _cast_into(mlp_w2[0], place, D, "cast_w2", tok))
    small = _gather_wait(f_sm, s_sm, r_sm, w2, "gather_wait_small", True)
    metaF = jnp.concatenate([small[k, 0:64].reshape(N_META, 512) for k in range(N_CHIP)], axis=1)
    gw2F = jnp.concatenate([small[k, 64:80] for k in range(N_CHIP)], axis=1)
    pwF = jnp.concatenate([small[k, 80:592].reshape(4, 64, GC) for k in range(N_CHIP)], axis=1)

    fly = {"win": win, "wout": wout, "w1": w1, "w2": w2}
    sems = {"win_d": sem_win_d}

    def step(name, names, waits, starts, after):
        at = {nm: i for i, nm in enumerate(names)}
        new, arrs, tok = _gather_step(name, [fly[nm] for nm in names], [(at[nm], k) for nm, k in waits],
                                      [(at[nm], k) for nm, k in starts], [sems[nm + "_" + k] for nm, k in waits], after)
        fly.update(zip(names, arrs))
        sems.update({nm + "_" + k: s for (nm, k), s in zip(starts, new)})
        return tok

    def relay_first():
        return step("gather_relay_win", ["win", "wout", "w1"], [("win", "d")],
                    [("win", "r"), ("win", "fx"), ("wout", "d"), ("w1", "d")], [v_w_in[0]])

    def get_win(after):
        tok = step("gather_land_win", ["win"], [("win", "r")], [("win", "fd")], [after])
        step("gather_wait_win", ["win"], [("win", "fx"), ("win", "fd")], [], [tok])
        return fly["win"]

    def relay_mid(after):
        return step("gather_relay_mid", ["wout"], [("wout", "d")], [("wout", "r"), ("wout", "fx")], [after, m_w_in[0]])

    def land_wout(after):
        return step("gather_land_wout", ["wout", "w1", "w2"], [("wout", "r"), ("w1", "d")],
                    [("wout", "fd"), ("w1", "r"), ("w1", "fx"), ("w2", "d")], [after])

    def get_wout(after):
        step("gather_wait_wout", ["wout"], [("wout", "fx"), ("wout", "fd")], [], [after])
        tok = step("gather_land_w1", ["w1"], [("w1", "r")], [("w1", "fd")], [fly["wout"]])
        return fly["wout"].reshape(D, D), tok

    def get_w1(after):
        step("gather_wait_w1", ["w1"], [("w1", "fx"), ("w1", "fd")], [], [after])
        return fly["w1"]

    def relay_last(after):
        return step("gather_relay_w2", ["w2"], [("w2", "d")], [("w2", "r"), ("w2", "fx")], [after])

    def get_w2(after):
        tok = step("gather_land_w2", ["w2"], [("w2", "r")], [("w2", "fd")], [after])
        step("gather_wait_w2", ["w2"], [("w2", "fx"), ("w2", "fd")], [], [tok])
        return fly["w2"].reshape(DFF, D)

    pairs, pending = {}, {}

    halves = {}

    def reduce_(names, after, tag):
        items = [(pending[nm][3], pending[nm][4], pending[nm][1], pending[nm][2]) for nm in names]
        landed = _rs_wait(items, after, "rs_wait_" + tag)
        fulls = [_final_sum(pending[nm][0], rb, place, "final_sum_" + nm) for nm, rb in zip(names, landed)]
        sent, token = _half_start(fulls, "half_start_" + tag)
        halves.update(zip(names, sent))
        return token

    def grad_start(nm, g):
        ssem, rsem, g_thru, land, token = _pair_start(g, "pair_start_" + nm)
        pairs[nm] = (ssem, rsem, g_thru, land)
        if nm == "win":
            token = reduce_(["w2", "w1", "wout"], token, "mlp_wout")
        return token

    def grad_finish(nm, after):
        ssem, rsem, g_thru, land = pairs[nm]
        g, rcv = _pair_wait(g_thru, land, ssem, rsem, after, "pair_wait_" + nm)
        sb, sf = _pair_sum(g, rcv, place, "pair_sum_" + nm)
        ssem, rsem, sb_thru, land, token = _rs_start(sb, "rs_start_" + nm)
        pending[nm] = (sf, ssem, rsem, sb_thru, land)
        return token

    (grad_x, loss8, d_n1w, d_gb, d_gnw, d_ps, d_n2w, d_fw, d_meta, d_gw2, d_pw) = _local_step(
        x[0], loss_target[0], dict(relay_first=relay_first, win=get_win, relay_mid=relay_mid, land_wout=land_wout,
                                   wout=get_wout, w1=get_w1, relay_last=relay_last, w2=get_w2),
        metaF, gw2F, pwF, norm1_w, gate_b, gla_norm_w, pool_scale, norm2_w, fw, grad_start, grad_finish)
    return _reduce_and_update(
        me, place, pending, halves, reduce_, grad_x, loss8, d_n1w, d_gb, d_gnw, d_ps, d_n2w, d_fw, d_meta, d_gw2, d_pw,
        meta_tokens, norm1_w, w_in, gate_w2, gate_b, gla_norm_w, pool_w, pool_scale, w_out, norm2_w, mlp_w1, mlp_w2, fw,
        m_meta_tokens, m_norm1_w, m_w_in, m_gate_w2, m_gate_b, m_gla_norm_w, m_pool_w, m_pool_scale, m_w_out, m_norm2_w,
        m_mlp_w1, m_mlp_w2, m_final_norm_w, v_meta_tokens, v_norm1_w, v_w_in, v_gate_w2, v_gate_b, v_gla_norm_w, v_pool_w,
        v_pool_scale, v_w_out, v_norm2_w, v_mlp_w1, v_mlp_w2, v_final_norm_w)


def _local_step(x, target, gather, metaF, gw2F, pwF, norm1_w, gate_b, gla_norm_w, pool_scale, norm2_w, fw, grad_start,
                grad_finish):
    h0, u = _embed_norm(x, metaF, norm1_w, gather["relay_first"]())
    Win = gather["win"](u)
    P = _in_proj(u, Win)
    gw2p = jnp.pad(gw2F, ((0, 128 - RANK), (0, 0)))
    yb, op = _pool_fwd(P, pwF, pool_scale, gather["relay_mid"](P))
    o, og, sp = _gla_fwd(P, gw2p, gate_b, gla_norm_w, gather["land_wout"](op))
    Wout, tok = gather["wout"](og)
    h1 = _out_proj(og, op, Wout, h0, tok)
    n2 = _norm_rows(h1, norm2_w, "norm2")
    W1 = gather["w1"](n2)
    zr, a, tok = _mlp_up(n2, W1, 0)
    zr, a, _ = _mlp_up(n2, W1, 1, (zr, a), gather["relay_last"](tok))
    W2 = gather["w2"](a)
    h2 = _mlp_down(a, W2, h1)

    dh2b, d_fw, loss8 = _loss_head(h2, target, fw)
    tok = grad_start("w2", _grad_w2(a, dh2b).reshape(N_CHIP, D, D))
    dz = _mlp_dz(dh2b, W2, zr, tok)
    tok = grad_finish("w2", dz)
    tok = grad_start("w1", _grad_w1(n2, dz, tok))
    dn2 = _mlp_dn(dz, W1, tok)
    tok = grad_finish("w1", dn2)
    dh1b, d_n2w = _norm_bwd(dn2, h1, dh2b, norm2_w, "norm2_bwd", tok)
    dmixed = _mixed_grad(dh1b, Wout)
    tok = grad_start("wout", _grad_wout(og, op, dh1b))
    dpu, d_pw, d_ps = _pool_bwd(dmixed, yb, pwF, pool_scale, tok)
    dq, dk, dv, dr, dglr, d_gw2p, d_gb, d_gnw = _gla_bwd(dmixed, o, P, gw2p, gate_b, gla_norm_w, sp, tok)
    d_gw2 = d_gw2p[0:RANK]
    tok = grad_finish("wout", dq)
    tok = grad_start("win", _grad_win(u, dq, dk, dv, dr, dglr, dpu, tok))
    du = _in_grad(dq, dk, dv, dr, dglr, dpu, Win, tok)
    tok = grad_finish("win", du)
    grad_x, d_meta, d_n1w = _input_grad(du, h0, dh1b, norm1_w, tok)
    return grad_x, loss8, d_n1w, d_gb, d_gnw, d_ps, d_n2w, d_fw, d_meta, d_gw2, d_pw


def _reduce_and_update(me, place, pending, halves, reduce_, grad_x, loss8, d_n1w, d_gb, d_gnw, d_ps, d_n2w, d_fw, d_meta, d_gw2,
                       d_pw,
                       meta_tokens, norm1_w, w_in, gate_w2, gate_b, gla_norm_w, pool_w, pool_scale, w_out, norm2_w,
                       mlp_w1, mlp_w2, fw, m_meta_tokens, m_norm1_w, m_w_in, m_gate_w2, m_gate_b, m_gla_norm_w, m_pool_w,
                       m_pool_scale, m_w_out, m_norm2_w, m_mlp_w1, m_mlp_w2, m_final_norm_w, v_meta_tokens, v_norm1_w, v_w_in,
                       v_gate_w2, v_gate_b, v_gla_norm_w, v_pool_w, v_pool_scale, v_w_out, v_norm2_w, v_mlp_w1, v_mlp_w2,
                       v_final_norm_w):
    parts = [loss8, d_n1w, d_gb, d_gnw, d_ps, d_n2w, d_fw, d_meta, d_gw2, d_pw]
    packed = [_pad_rows8(p) for p in parts]
    sizes = [p.shape[0] for p in packed]
    vec = jnp.concatenate(packed, axis=0)

    big = {}
    params = {"w2": (mlp_w2[0], m_mlp_w2[0], v_mlp_w2[0]), "w1": (mlp_w1[0], m_mlp_w1[0], v_mlp_w1[0]),
              "wout": (w_out[0], m_w_out[0], v_w_out[0]), "win": (w_in[0], m_w_in[0], v_w_in[0])}

    def update(names, after, tag):
        fulls = _half_wait([halves[nm] for nm in names], [after], "half_wait_" + tag)
        tok = None
        for nm, full in zip(names, fulls):
            w, m, v = params[nm]
            big[nm] = _adam_big(w, full, m, v, "adam_" + nm, tok)
            tok = big[nm][3]
        return tok

    s1, r1, vec, land1, tok = _small_start(vec, True, "small_start_pair")
    tok = update(["w2"], tok, "w2")
    vec, got = _small_wait(vec, land1, s1, r1, True, tok, "small_wait_pair")
    pair = _small_pair_sum(vec, got)
    s2, r2, pair, land2, tok = _small_start(pair, False, "small_start_chips")
    tok = reduce_(["win"], tok, "win")
    tok = update(["w1"], tok, "w1")
    tok = update(["win"], tok, "win")
    pair, got = _small_wait(pair, land2, s2, r2, False, tok, "small_wait_chips")
    red = _small_chip_sum(pair, got, place)
    after = update(["wout"], red, "wout")
    offs = [0]
    for s in sizes:
        offs.append(offs[-1] + s)

    def take(i, shape):
        n = 1
        for d in shape:
            n *= d
        return red[offs[i]:offs[i] + n // 128].reshape(shape)

    loss = red[0, 0]
    G_n1w = take(1, (1, D))
    G_gb = take(2, (1, KW))
    G_gnw = take(3, (1, DV))
    G_ps = take(4, (1, PW))
    G_n2w = take(5, (1, D))
    G_fw = take(6, (1, D))
    G_meta = lax.dynamic_slice(take(7, (N_META, D)), (0, me * 512), (N_META, 512))
    G_gw2 = lax.dynamic_slice(take(8, (RANK, KW)), (0, me * 128), (RANK, 128))
    G_pw = lax.dynamic_slice(take(9, (4, GC, GC)), (0, me * 64, 0), (4, 64, GC))

    G_win, d_win, nm_win, nv_win = big["win"]
    G_wout, d_wout, nm_wout, nv_wout = big["wout"]
    G_w1, d_w1, nm_w1, nv_w1 = big["w1"]
    G_w2, d_w2, nm_w2, nv_w2 = big["w2"]
    ws = [meta_tokens, norm1_w, gate_w2[0], gate_b, gla_norm_w, pool_w[0], pool_scale, norm2_w, fw]
    gs = [G_meta, G_n1w, G_gw2, G_gb, G_gnw, G_pw, G_ps, G_n2w, G_fw]
    ms = [m_meta_tokens, m_norm1_w, m_gate_w2[0], m_gate_b, m_gla_norm_w, m_pool_w[0], m_pool_scale, m_norm2_w,
          m_final_norm_w.reshape(1, D)]
    vs = [v_meta_tokens, v_norm1_w, v_gate_w2[0], v_gate_b, v_gla_norm_w, v_pool_w[0], v_pool_scale, v_norm2_w,
          v_final_norm_w.reshape(1, D)]
    ds, nms, nvs = _adam_small(ws, gs, ms, vs, after)

    def assemble(small, win_, wout_, w1_, w2_):
        meta_, n1_, gw2_, gb_, gnw_, pw_, ps_, n2_, fw_ = small
        return (meta_, n1_, win_[None], gw2_[None], gb_, gnw_, pw_[None], ps_, wout_[None], n2_, w1_[None], w2_[None],
                fw_.reshape(D))

    grads_out = assemble(gs, G_win, G_wout, G_w1, G_w2)
    deltas = assemble(ds, d_win, d_wout, d_w1, d_w2)
    new_m = assemble(nms, nm_win, nm_wout, nm_w1, nm_w2)
    new_v = assemble(nvs, nv_win, nv_wout, nv_w1, nv_w2)
    return (loss, grad_x[None], *grads_out, *deltas, *new_m, *new_v)
```
